```python
import jax, jax.numpy as jnp
from jax import lax
import numpy as np

D_MODEL = 1024
BATCH = 8
SEQ = 4096
DEPTH = 2

CTX_LEN = 256
GRID_W = 64

GLA_HEADS = 4
GLA_DK = D_MODEL // 2 // GLA_HEADS
GLA_DV = D_MODEL // GLA_HEADS
GLA_K = GLA_HEADS * GLA_DK
GLA_V = GLA_HEADS * GLA_DV
GLA_LR = 16
GLA_TAU = 16.0
GLA_CHUNK = 64

CONV_W = D_MODEL // 2
CONV_K = 31

POOL_GROUPS = 4
POOL_W = D_MODEL // 2
POOL_GC = POOL_W // POOL_GROUPS
POOL_WINDOWS = (2, 4, 8, 16)

N_BRANCH = 3
D_FF = 4 * D_MODEL
EPS = 1e-6

SPLIT_SIZES = (GLA_K, GLA_K, GLA_V, GLA_V, GLA_LR, GLA_LR, CONV_W, CONV_W, POOL_W, N_BRANCH * D_MODEL)
IN_COLS = 2 * GLA_K + 2 * GLA_V + 2 * GLA_LR + 2 * CONV_W + POOL_W + N_BRANCH * D_MODEL

kernel_name = "hybrid_gla_conformer_pool_dit_block"


def rms_norm(x, g):
    x32 = x.astype(jnp.float32)
    y = x32 * lax.rsqrt(jnp.mean(x32 * x32, axis=-1, keepdims=True) + EPS)
    return (y * g.astype(jnp.float32)).astype(x.dtype)


def layer_norm(x, g, b):
    x32 = x.astype(jnp.float32)
    mu = jnp.mean(x32, axis=-1, keepdims=True)
    xc = x32 - mu
    y = xc * lax.rsqrt(jnp.mean(xc * xc, axis=-1, keepdims=True) + EPS)
    return (y * g.astype(jnp.float32) + b.astype(jnp.float32)).astype(x.dtype)


def modulate(x, shift, scale):
    return x * (1.0 + scale) + shift


def in_projection(h, w_in):
    parts, start = [], 0
    for size in SPLIT_SIZES:
        parts.append(h @ w_in[:, start:start + size])
        start += size
    return parts


def gla_chunk(q, k, v, la, s0):
    B, L, H, DK = q.shape
    DV = v.shape[-1]
    C = GLA_CHUNK
    nC = L // C
    q, k, la = (t.reshape(B, nC, C, H, DK) for t in (q, k, la))
    v = v.reshape(B, nC, C, H, DV)
    b = jnp.cumsum(la, axis=2)
    q_i = q * jnp.exp(b)
    k_i = k * jnp.exp(-b)
    mask = jnp.tril(jnp.ones((C, C), dtype=bool))
    att = jnp.einsum('bnthd,bnshd->bnhts', q_i, k_i)
    att = jnp.where(mask, att, 0.0)
    o = jnp.einsum('bnhts,bnshv->bnthv', att, v)
    b_end = b[:, :, -1]
    k_end = k * jnp.exp(b_end[:, :, None] - b)
    d_state = jnp.einsum('bnshd,bnshv->bnhdv', k_end, v)
    gamma = jnp.exp(b_end)

    def step(s, inp):
        g, ds = inp
        return g[..., None] * s + ds, s

    s_fin, s_start = lax.scan(step, s0, (jnp.moveaxis(gamma, 1, 0), jnp.moveaxis(d_state, 1, 0)))
    s_start = jnp.moveaxis(s_start, 0, 1)
    o = o + jnp.einsum('bnthd,bnhdv->bnthv', q_i, s_start)
    return o.reshape(B, L, H, DV), s_fin


def gla_bidir(q, k, v, la_f, la_b, s_f0, s_b0):
    o_f, s_f = gla_chunk(q, k, v, la_f, s_f0)
    rev = lambda t: jnp.flip(t, axis=1)
    o_b, s_b = gla_chunk(rev(q), rev(k), rev(v), rev(la_b), s_b0)
    return o_f + rev(o_b), s_f, s_b


def gla_inputs(parts, p):
    pq, pk, pv, _, plf, plb = parts[:6]
    B, L, _ = pq.shape
    q = pq.astype(jnp.float32).reshape(B, L, GLA_HEADS, GLA_DK) * (GLA_DK ** -0.5)
    k = pk.astype(jnp.float32).reshape(B, L, GLA_HEADS, GLA_DK)
    v = pv.astype(jnp.float32).reshape(B, L, GLA_HEADS, GLA_DV)

    def log_decay(plr, i):
        z = (plr @ p['w_decay'][i] + p['b_decay'][i]).astype(jnp.float32)
        return (jax.nn.log_sigmoid(z) / GLA_TAU).reshape(B, L, GLA_HEADS, GLA_DK)

    return q, k, v, log_decay(plf, 0), log_decay(plb, 1)


def gla_out(o, pg, p):
    B, L = o.shape[:2]
    o = o * lax.rsqrt(jnp.mean(o * o, axis=-1, keepdims=True) + EPS)
    o = o * p['g_gla'].astype(jnp.float32).reshape(GLA_HEADS, GLA_DV)
    o = o.reshape(B, L, GLA_V).astype(pg.dtype) * jax.nn.silu(pg)
    return o @ p['w_gla_o']


def depthwise_conv(u, w, b):
    C = u.shape[-1]
    y = lax.conv_general_dilated(u, w[:, None, :].astype(u.dtype), window_strides=(1,),
                                 padding=[(CONV_K // 2, CONV_K // 2)],
                                 dimension_numbers=('NWC', 'WIO', 'NWC'),
                                 feature_group_count=C)
    return y + b


def conv_branch(pa, pb, p, rows):
    u = pa * jax.nn.sigmoid(pb)
    B, L, C = u.shape
    seqs = u if rows is None else u.reshape(B * rows, GRID_W, C)
    y = depthwise_conv(seqs, p['w_dw'], p['b_dw']).reshape(B, L, C)
    y = jax.nn.silu(layer_norm(y, p['g_conv_ln'], p['b_conv_ln']))
    return y @ p['w_conv_o']


def window_mean(u, w):
    L = u.shape[1]
    left = w // 2
    right = w - 1 - left
    cs = jnp.cumsum(u.astype(jnp.float32), axis=1)
    cs = jnp.concatenate([jnp.zeros_like(cs[:, :1]), cs], axis=1)
    t = jnp.arange(L)
    lo = jnp.clip(t - left, 0, L)
    hi = jnp.clip(t + right + 1, 0, L)
    total = jnp.take(cs, hi, axis=1) - jnp.take(cs, lo, axis=1)
    cnt = (hi - lo).astype(jnp.float32).reshape((1, L) + (1,) * (u.ndim - 2))
    return (total / cnt).astype(u.dtype)


def pool_branch(u, p, rows):
    B, L, C = u.shape
    if rows is None:
        grid = u.reshape(B, L, POOL_GROUPS, POOL_GC)
    else:
        grid = u.reshape(B, rows, GRID_W, POOL_GROUPS, POOL_GC)
    pooled = jnp.stack([window_mean(grid[..., i, :], w) for i, w in enumerate(POOL_WINDOWS)], axis=-2)
    y = jnp.einsum('...gc,gcd->...gd', pooled - grid, p['w_pool_g'])
    y = y.reshape(B, L, C) * p['s_pool']
    return y @ p['w_pool_o']


def merge(ya, yb, yc, pgate, p):
    B, L, _ = pgate.shape
    gates = jax.nn.sigmoid(pgate.reshape(B, L, N_BRANCH, D_MODEL) + p['b_gate'])
    mixed = gates[:, :, 0] * ya + gates[:, :, 1] * yb + gates[:, :, 2] * yc
    return mixed @ p['w_out']


def token_mixer(h, hc, p, need_ctx_out):
    rows = h.shape[1] // GRID_W
    parts = in_projection(h, p['w_in'])
    cparts = in_projection(hc, p['w_in'])
    zero = jnp.zeros((hc.shape[0], GLA_HEADS, GLA_DK, GLA_DV), jnp.float32)
    cq, ck, cv, cla_f, cla_b = gla_inputs(cparts, p)
    co, cs_f, cs_b = gla_bidir(cq, ck, cv, cla_f, cla_b, zero, zero)
    q, k, v, la_f, la_b = gla_inputs(parts, p)
    o, _, _ = gla_bidir(q, k, v, la_f, la_b, cs_f, cs_b)
    ya = gla_out(o, parts[3], p)
    yb = conv_branch(parts[6], parts[7], p, rows)
    yc = pool_branch(parts[8], p, rows)
    y = merge(ya, yb, yc, parts[9], p)
    if not need_ctx_out:
        return y, None
    ca = gla_out(co, cparts[3], p)
    cb = conv_branch(cparts[6], cparts[7], p, None)
    cc = pool_branch(cparts[8], p, None)
    y_ctx = merge(ca, cb, cc, cparts[9], p)
    return y, y_ctx


def sq_relu_mlp(h, w1, w2):
    return jnp.square(jax.nn.relu(h @ w1)) @ w2


def _fwd_setup_inputs(seed: int = 0) -> dict:
    key = jax.random.key(seed)
    ks = iter(jax.random.split(key, 32))

    def nrm(shape, scale):
        return jax.random.normal(next(ks), shape, jnp.float32) * scale

    L = DEPTH
    return {
        'x': nrm((BATCH, SEQ, D_MODEL), 1.0),
        'c': nrm((BATCH, D_MODEL), 1.0),
        'ctx': nrm((BATCH, CTX_LEN, D_MODEL), 1.0),
        'c_ctx': nrm((D_MODEL,), 1.0),
        'w_ada': nrm((L, D_MODEL, 6 * D_MODEL), D_MODEL ** -0.5),
        'b_ada': nrm((L, 6 * D_MODEL), 0.02),
        'g_pre_mix': 1.0 + nrm((L, D_MODEL), 0.05),
        'g_post_mix': 1.0 + nrm((L, D_MODEL), 0.05),
        'g_pre_mlp': 1.0 + nrm((L, D_MODEL), 0.05),
        'g_post_mlp': 1.0 + nrm((L, D_MODEL), 0.05),
        'w_in': nrm((L, D_MODEL, IN_COLS), D_MODEL ** -0.5),
        'w_decay': nrm((L, 2, GLA_LR, GLA_K), GLA_LR ** -0.5),
        'b_decay': nrm((L, 2, GLA_K), 0.1),
        'g_gla': 1.0 + nrm((L, GLA_V), 0.05),
        'w_gla_o': nrm((L, GLA_V, D_MODEL), GLA_V ** -0.5),
        'w_dw': nrm((L, CONV_K, CONV_W), CONV_K ** -0.5),
        'b_dw': nrm((L, CONV_W), 0.02),
        'g_conv_ln': 1.0 + nrm((L, CONV_W), 0.05),
        'b_conv_ln': nrm((L, CONV_W), 0.02),
        'w_conv_o': nrm((L, CONV_W, D_MODEL), CONV_W ** -0.5),
        'w_pool_g': nrm((L, POOL_GROUPS, POOL_GC, POOL_GC), POOL_GC ** -0.5),
        's_pool': 1.0 + nrm((L, POOL_W), 0.1),
        'w_pool_o': nrm((L, POOL_W, D_MODEL), POOL_W ** -0.5),
        'b_gate': nrm((L, N_BRANCH, D_MODEL), 0.1),
        'w_out': nrm((L, D_MODEL, D_MODEL), D_MODEL ** -0.5),
        'w_mlp1': nrm((L, D_MODEL, D_FF), D_MODEL ** -0.5),
        'w_mlp2': nrm((L, D_FF, D_MODEL), D_FF ** -0.5),
    }


def _fwd_reference(x, c, ctx, c_ctx, w_ada, b_ada, g_pre_mix, g_post_mix, g_pre_mlp, g_post_mlp,
              w_in, w_decay, b_decay, g_gla, w_gla_o, w_dw, b_dw, g_conv_ln, b_conv_ln, w_conv_o,
              w_pool_g, s_pool, w_pool_o, b_gate, w_out, w_mlp1, w_mlp2):
    silu_c = jax.nn.silu(c)
    silu_cc = jax.nn.silu(c_ctx)
    for l in range(DEPTH):
        last = l == DEPTH - 1
        p = {
            'w_in': w_in[l], 'w_decay': w_decay[l], 'b_decay': b_decay[l], 'g_gla': g_gla[l],
            'w_gla_o': w_gla_o[l], 'w_dw': w_dw[l], 'b_dw': b_dw[l], 'g_conv_ln': g_conv_ln[l],
            'b_conv_ln': b_conv_ln[l], 'w_conv_o': w_conv_o[l], 'w_pool_g': w_pool_g[l],
            's_pool': s_pool[l], 'w_pool_o': w_pool_o[l], 'b_gate': b_gate[l], 'w_out': w_out[l],
        }
        mod = jnp.split((silu_c @ w_ada[l] + b_ada[l])[:, None, :], 6, axis=-1)
        mod_c = jnp.split(silu_cc @ w_ada[l] + b_ada[l], 6, axis=-1)

        h = modulate(rms_norm(x, g_pre_mix[l]), mod[0], mod[1])
        hc = modulate(rms_norm(ctx, g_pre_mix[l]), mod_c[0], mod_c[1])
        y, y_ctx = token_mixer(h, hc, p, not last)
        x = x + mod[2] * rms_norm(y, g_post_mix[l])
        h = modulate(rms_norm(x, g_pre_mlp[l]), mod[3], mod[4])
        x = x + mod[5] * rms_norm(sq_relu_mlp(h, w_mlp1[l], w_mlp2[l]), g_post_mlp[l])

        if not last:
            ctx = ctx + mod_c[2] * rms_norm(y_ctx, g_post_mix[l])
            hc = modulate(rms_norm(ctx, g_pre_mlp[l]), mod_c[3], mod_c[4])
            ctx = ctx + mod_c[5] * rms_norm(sq_relu_mlp(hc, w_mlp1[l], w_mlp2[l]), g_post_mlp[l])
    return x


import jax as _jax
import jax.numpy as _jnp

TWIN_FORMAT = 'train_step'
FWD_PARAMS = ['x', 'c', 'ctx', 'c_ctx', 'w_ada', 'b_ada', 'g_pre_mix', 'g_post_mix', 'g_pre_mlp', 'g_post_mlp', 'w_in', 'w_decay', 'b_decay', 'g_gla', 'w_gla_o', 'w_dw', 'b_dw', 'g_conv_ln', 'b_conv_ln', 'w_conv_o', 'w_pool_g', 's_pool', 'w_pool_o', 'b_gate', 'w_out', 'w_mlp1', 'w_mlp2']
TWIN_WEIGHTS = ['c_ctx', 'w_ada', 'b_ada', 'g_pre_mix', 'g_post_mix', 'g_pre_mlp', 'g_post_mlp', 'w_in', 'w_decay', 'b_decay', 'g_gla', 'w_gla_o', 'w_dw', 'b_dw', 'g_conv_ln', 'b_conv_ln', 'w_conv_o', 'w_pool_g', 's_pool', 'w_pool_o', 'b_gate', 'w_out', 'w_mlp1', 'w_mlp2']
TWIN_DIFF_INPUT = 'x'
TWIN_INPUTS = ['x', 'c', 'ctx', 'c_ctx', 'w_ada', 'b_ada', 'g_pre_mix', 'g_post_mix', 'g_pre_mlp', 'g_post_mlp', 'w_in', 'w_decay', 'b_decay', 'g_gla', 'w_gla_o', 'w_dw', 'b_dw', 'g_conv_ln', 'b_conv_ln', 'w_conv_o', 'w_pool_g', 's_pool', 'w_pool_o', 'b_gate', 'w_out', 'w_mlp1', 'w_mlp2', 'loss_target', 'm_c_ctx', 'm_w_ada', 'm_b_ada', 'm_g_pre_mix', 'm_g_post_mix', 'm_g_pre_mlp', 'm_g_post_mlp', 'm_w_in', 'm_w_decay', 'm_b_decay', 'm_g_gla', 'm_w_gla_o', 'm_w_dw', 'm_b_dw', 'm_g_conv_ln', 'm_b_conv_ln', 'm_w_conv_o', 'm_w_pool_g', 'm_s_pool', 'm_w_pool_o', 'm_b_gate', 'm_w_out', 'm_w_mlp1', 'm_w_mlp2', 'v_c_ctx', 'v_w_ada', 'v_b_ada', 'v_g_pre_mix', 'v_g_post_mix', 'v_g_pre_mlp', 'v_g_post_mlp', 'v_w_in', 'v_w_decay', 'v_b_decay', 'v_g_gla', 'v_w_gla_o', 'v_w_dw', 'v_b_dw', 'v_g_conv_ln', 'v_b_conv_ln', 'v_w_conv_o', 'v_w_pool_g', 'v_s_pool', 'v_w_pool_o', 'v_b_gate', 'v_w_out', 'v_w_mlp1', 'v_w_mlp2']
TWIN_OUTPUTS = ['loss', 'grad_x', 'grad_c_ctx', 'grad_w_ada', 'grad_b_ada', 'grad_g_pre_mix', 'grad_g_post_mix', 'grad_g_pre_mlp', 'grad_g_post_mlp', 'grad_w_in', 'grad_w_decay', 'grad_b_decay', 'grad_g_gla', 'grad_w_gla_o', 'grad_w_dw', 'grad_b_dw', 'grad_g_conv_ln', 'grad_b_conv_ln', 'grad_w_conv_o', 'grad_w_pool_g', 'grad_s_pool', 'grad_w_pool_o', 'grad_b_gate', 'grad_w_out', 'grad_w_mlp1', 'grad_w_mlp2', 'delta_c_ctx', 'delta_w_ada', 'delta_b_ada', 'delta_g_pre_mix', 'delta_g_post_mix', 'delta_g_pre_mlp', 'delta_g_post_mlp', 'delta_w_in', 'delta_w_decay', 'delta_b_decay', 'delta_g_gla', 'delta_w_gla_o', 'delta_w_dw', 'delta_b_dw', 'delta_g_conv_ln', 'delta_b_conv_ln', 'delta_w_conv_o', 'delta_w_pool_g', 'delta_s_pool', 'delta_w_pool_o', 'delta_b_gate', 'delta_w_out', 'delta_w_mlp1', 'delta_w_mlp2', 'new_m_c_ctx', 'new_m_w_ada', 'new_m_b_ada', 'new_m_g_pre_mix', 'new_m_g_post_mix', 'new_m_g_pre_mlp', 'new_m_g_post_mlp', 'new_m_w_in', 'new_m_w_decay', 'new_m_b_decay', 'new_m_g_gla', 'new_m_w_gla_o', 'new_m_w_dw', 'new_m_b_dw', 'new_m_g_conv_ln', 'new_m_b_conv_ln', 'new_m_w_conv_o', 'new_m_w_pool_g', 'new_m_s_pool', 'new_m_w_pool_o', 'new_m_b_gate', 'new_m_w_out', 'new_m_w_mlp1', 'new_m_w_mlp2', 'new_v_c_ctx', 'new_v_w_ada', 'new_v_b_ada', 'new_v_g_pre_mix', 'new_v_g_post_mix', 'new_v_g_pre_mlp', 'new_v_g_post_mlp', 'new_v_w_in', 'new_v_w_decay', 'new_v_b_decay', 'new_v_g_gla', 'new_v_w_gla_o', 'new_v_w_dw', 'new_v_b_dw', 'new_v_g_conv_ln', 'new_v_b_conv_ln', 'new_v_w_conv_o', 'new_v_w_pool_g', 'new_v_s_pool', 'new_v_w_pool_o', 'new_v_b_gate', 'new_v_w_out', 'new_v_w_mlp1', 'new_v_w_mlp2']
TWIN_LEAF_KINDS = {'loss': 'loss', 'grad_x': 'grad_x', 'grad_c_ctx': 'grad_w', 'grad_w_ada': 'grad_w', 'grad_b_ada': 'grad_w', 'grad_g_pre_mix': 'grad_w', 'grad_g_post_mix': 'grad_w', 'grad_g_pre_mlp': 'grad_w', 'grad_g_post_mlp': 'grad_w', 'grad_w_in': 'grad_w', 'grad_w_decay': 'grad_w', 'grad_b_decay': 'grad_w', 'grad_g_gla': 'grad_w', 'grad_w_gla_o': 'grad_w', 'grad_w_dw': 'grad_w', 'grad_b_dw': 'grad_w', 'grad_g_conv_ln': 'grad_w', 'grad_b_conv_ln': 'grad_w', 'grad_w_conv_o': 'grad_w', 'grad_w_pool_g': 'grad_w', 'grad_s_pool': 'grad_w', 'grad_w_pool_o': 'grad_w', 'grad_b_gate': 'grad_w', 'grad_w_out': 'grad_w', 'grad_w_mlp1': 'grad_w', 'grad_w_mlp2': 'grad_w', 'delta_c_ctx': 'delta_w', 'delta_w_ada': 'delta_w', 'delta_b_ada': 'delta_w', 'delta_g_pre_mix': 'delta_w', 'delta_g_post_mix': 'delta_w', 'delta_g_pre_mlp': 'delta_w', 'delta_g_post_mlp': 'delta_w', 'delta_w_in': 'delta_w', 'delta_w_decay': 'delta_w', 'delta_b_decay': 'delta_w', 'delta_g_gla': 'delta_w', 'delta_w_gla_o': 'delta_w', 'delta_w_dw': 'delta_w', 'delta_b_dw': 'delta_w', 'delta_g_conv_ln': 'delta_w', 'delta_b_conv_ln': 'delta_w', 'delta_w_conv_o': 'delta_w', 'delta_w_pool_g': 'delta_w', 'delta_s_pool': 'delta_w', 'delta_w_pool_o': 'delta_w', 'delta_b_gate': 'delta_w', 'delta_w_out': 'delta_w', 'delta_w_mlp1': 'delta_w', 'delta_w_mlp2': 'delta_w', 'new_m_c_ctx': 'new_m', 'new_m_w_ada': 'new_m', 'new_m_b_ada': 'new_m', 'new_m_g_pre_mix': 'new_m', 'new_m_g_post_mix': 'new_m', 'new_m_g_pre_mlp': 'new_m', 'new_m_g_post_mlp': 'new_m', 'new_m_w_in': 'new_m', 'new_m_w_decay': 'new_m', 'new_m_b_decay': 'new_m', 'new_m_g_gla': 'new_m', 'new_m_w_gla_o': 'new_m', 'new_m_w_dw': 'new_m', 'new_m_b_dw': 'new_m', 'new_m_g_conv_ln': 'new_m', 'new_m_b_conv_ln': 'new_m', 'new_m_w_conv_o': 'new_m', 'new_m_w_pool_g': 'new_m', 'new_m_s_pool': 'new_m', 'new_m_w_pool_o': 'new_m', 'new_m_b_gate': 'new_m', 'new_m_w_out': 'new_m', 'new_m_w_mlp1': 'new_m', 'new_m_w_mlp2': 'new_m', 'new_v_c_ctx': 'new_v', 'new_v_w_ada': 'new_v', 'new_v_b_ada': 'new_v', 'new_v_g_pre_mix': 'new_v', 'new_v_g_post_mix': 'new_v', 'new_v_g_pre_mlp': 'new_v', 'new_v_g_post_mlp': 'new_v', 'new_v_w_in': 'new_v', 'new_v_w_decay': 'new_v', 'new_v_b_decay': 'new_v', 'new_v_g_gla': 'new_v', 'new_v_w_gla_o': 'new_v', 'new_v_w_dw': 'new_v', 'new_v_b_dw': 'new_v', 'new_v_g_conv_ln': 'new_v', 'new_v_b_conv_ln': 'new_v', 'new_v_w_conv_o': 'new_v', 'new_v_w_pool_g': 'new_v', 'new_v_s_pool': 'new_v', 'new_v_w_pool_o': 'new_v', 'new_v_b_gate': 'new_v', 'new_v_w_out': 'new_v', 'new_v_w_mlp1': 'new_v', 'new_v_w_mlp2': 'new_v'}


def _forward(args):
    return _fwd_reference(*[args[k] for k in FWD_PARAMS])


def _output_shape():
    out = _jax.eval_shape(lambda: _forward(_fwd_setup_inputs(0)))
    return out.shape, out.dtype

N_MICROBATCH = 1
ADAM_LR = 0.001
ADAM_B1 = 0.9
ADAM_B2 = 0.999
ADAM_EPS = 1e-08
ADAM_WD = 0.01
ADAM_STEP = 10
PER_EXAMPLE_BATCH_AXIS = {'x': 0, 'c': 0, 'ctx': 0, 'loss_target': 0}
SHARED_INPUTS = []
_WEIGHT_DTYPES = {'c_ctx': _jnp.float32, 'w_ada': _jnp.float32, 'b_ada': _jnp.float32, 'g_pre_mix': _jnp.float32, 'g_post_mix': _jnp.float32, 'g_pre_mlp': _jnp.float32, 'g_post_mlp': _jnp.float32, 'w_in': _jnp.float32, 'w_decay': _jnp.float32, 'b_decay': _jnp.float32, 'g_gla': _jnp.float32, 'w_gla_o': _jnp.float32, 'w_dw': _jnp.float32, 'b_dw': _jnp.float32, 'g_conv_ln': _jnp.float32, 'b_conv_ln': _jnp.float32, 'w_conv_o': _jnp.float32, 'w_pool_g': _jnp.float32, 's_pool': _jnp.float32, 'w_pool_o': _jnp.float32, 'b_gate': _jnp.float32, 'w_out': _jnp.float32, 'w_mlp1': _jnp.float32, 'w_mlp2': _jnp.float32}
MOMENT_SCALE = {'c_ctx': 9.427557e-02, 'w_ada': 3.968515e+00, 'b_ada': 7.322525e+00, 'g_pre_mix': 6.056604e-01, 'g_post_mix': 1.516100e+01, 'g_pre_mlp': 7.653893e-01, 'g_post_mlp': 1.672676e+01, 'w_in': 4.582451e-01, 'w_decay': 1.642840e-01, 'b_decay': 2.418469e-01, 'g_gla': 5.733528e-01, 'w_gla_o': 5.581303e-01, 'w_dw': 1.205311e+00, 'b_dw': 4.311282e+00, 'g_conv_ln': 2.410161e+00, 'b_conv_ln': 2.945290e+00, 'w_conv_o': 1.074525e+00, 'w_pool_g': 4.722079e-01, 's_pool': 5.590442e-01, 'w_pool_o': 3.595282e-01, 'b_gate': 2.479727e-01, 'w_out': 1.186345e+00, 'w_mlp1': 1.165083e+00, 'w_mlp2': 3.444111e+00}


def _to_microbatches(a, axis):
    t = _jnp.moveaxis(a, axis, 0)
    t = t.reshape((N_MICROBATCH, t.shape[0] // N_MICROBATCH) + t.shape[1:])
    return _jnp.moveaxis(t, 1, axis + 1)


def setup_inputs(seed: int = 0) -> dict:
    inp = _fwd_setup_inputs(seed)
    key = _jax.random.fold_in(_jax.random.key(seed), 7919)
    shape, _ = _output_shape()
    out = dict(inp)
    out["loss_target"] = _jax.random.normal(_jax.random.fold_in(key, 0), shape, _jnp.float32)
    for i, name in enumerate(TWIN_WEIGHTS):
        w = inp[name].astype(_jnp.float32)
        if MOMENT_SCALE is None:
            s = _jnp.sqrt(_jnp.mean(_jnp.square(w)) + 1e-30)
        else:
            s = MOMENT_SCALE[name]
        km, kv = _jax.random.split(_jax.random.fold_in(key, i + 1))
        out[name] = w
        out["m_" + name] = s * _jax.random.normal(km, w.shape, _jnp.float32)
        out["v_" + name] = (s * s) * _jax.random.uniform(kv, w.shape, _jnp.float32, 0.5, 1.5)
    if N_MICROBATCH > 1:
        for name, axis in PER_EXAMPLE_BATCH_AXIS.items():
            out[name] = _to_microbatches(out[name], axis)
    return {'x': out['x'], 'c': out['c'], 'ctx': out['ctx'], 'c_ctx': out['c_ctx'], 'w_ada': out['w_ada'], 'b_ada': out['b_ada'], 'g_pre_mix': out['g_pre_mix'], 'g_post_mix': out['g_post_mix'], 'g_pre_mlp': out['g_pre_mlp'], 'g_post_mlp': out['g_post_mlp'], 'w_in': out['w_in'], 'w_decay': out['w_decay'], 'b_decay': out['b_decay'], 'g_gla': out['g_gla'], 'w_gla_o': out['w_gla_o'], 'w_dw': out['w_dw'], 'b_dw': out['b_dw'], 'g_conv_ln': out['g_conv_ln'], 'b_conv_ln': out['b_conv_ln'], 'w_conv_o': out['w_conv_o'], 'w_pool_g': out['w_pool_g'], 's_pool': out['s_pool'], 'w_pool_o': out['w_pool_o'], 'b_gate': out['b_gate'], 'w_out': out['w_out'], 'w_mlp1': out['w_mlp1'], 'w_mlp2': out['w_mlp2'], 'loss_target': out['loss_target'], 'm_c_ctx': out['m_c_ctx'], 'm_w_ada': out['m_w_ada'], 'm_b_ada': out['m_b_ada'], 'm_g_pre_mix': out['m_g_pre_mix'], 'm_g_post_mix': out['m_g_post_mix'], 'm_g_pre_mlp': out['m_g_pre_mlp'], 'm_g_post_mlp': out['m_g_post_mlp'], 'm_w_in': out['m_w_in'], 'm_w_decay': out['m_w_decay'], 'm_b_decay': out['m_b_decay'], 'm_g_gla': out['m_g_gla'], 'm_w_gla_o': out['m_w_gla_o'], 'm_w_dw': out['m_w_dw'], 'm_b_dw': out['m_b_dw'], 'm_g_conv_ln': out['m_g_conv_ln'], 'm_b_conv_ln': out['m_b_conv_ln'], 'm_w_conv_o': out['m_w_conv_o'], 'm_w_pool_g': out['m_w_pool_g'], 'm_s_pool': out['m_s_pool'], 'm_w_pool_o': out['m_w_pool_o'], 'm_b_gate': out['m_b_gate'], 'm_w_out': out['m_w_out'], 'm_w_mlp1': out['m_w_mlp1'], 'm_w_mlp2': out['m_w_mlp2'], 'v_c_ctx': out['v_c_ctx'], 'v_w_ada': out['v_w_ada'], 'v_b_ada': out['v_b_ada'], 'v_g_pre_mix': out['v_g_pre_mix'], 'v_g_post_mix': out['v_g_post_mix'], 'v_g_pre_mlp': out['v_g_pre_mlp'], 'v_g_post_mlp': out['v_g_post_mlp'], 'v_w_in': out['v_w_in'], 'v_w_decay': out['v_w_decay'], 'v_b_decay': out['v_b_decay'], 'v_g_gla': out['v_g_gla'], 'v_w_gla_o': out['v_w_gla_o'], 'v_w_dw': out['v_w_dw'], 'v_b_dw': out['v_b_dw'], 'v_g_conv_ln': out['v_g_conv_ln'], 'v_b_conv_ln': out['v_b_conv_ln'], 'v_w_conv_o': out['v_w_conv_o'], 'v_w_pool_g': out['v_w_pool_g'], 'v_s_pool': out['v_s_pool'], 'v_w_pool_o': out['v_w_pool_o'], 'v_b_gate': out['v_b_gate'], 'v_w_out': out['v_w_out'], 'v_w_mlp1': out['v_w_mlp1'], 'v_w_mlp2': out['v_w_mlp2']}


def _loss(weights, diff, rest, loss_target):
    with _jax.named_scope("forward"):
        args = {**rest, TWIN_DIFF_INPUT: diff, **{k: w.astype(_WEIGHT_DTYPES[k]) for k, w in weights.items()}}
        y = _forward(args)
    with _jax.named_scope("loss_head"):
        err = _jnp.square(y.astype(_jnp.float32) - loss_target)
        return 0.5 * _jnp.sum(_jnp.mean(err, axis=-1)) if err.ndim else 0.5 * err


def _adamw(w, g, m, v):
    m = ADAM_B1 * m + (1.0 - ADAM_B1) * g
    v = ADAM_B2 * v + (1.0 - ADAM_B2) * _jnp.square(g)
    m_hat = m / (1.0 - ADAM_B1 ** ADAM_STEP)
    v_hat = v / (1.0 - ADAM_B2 ** ADAM_STEP)
    delta = -ADAM_LR * (m_hat / (_jnp.sqrt(v_hat) + ADAM_EPS) + ADAM_WD * w)
    return delta, m, v


def reference(x, c, ctx, c_ctx, w_ada, b_ada, g_pre_mix, g_post_mix, g_pre_mlp, g_post_mlp, w_in, w_decay, b_decay, g_gla, w_gla_o, w_dw, b_dw, g_conv_ln, b_conv_ln, w_conv_o, w_pool_g, s_pool, w_pool_o, b_gate, w_out, w_mlp1, w_mlp2, loss_target, m_c_ctx, m_w_ada, m_b_ada, m_g_pre_mix, m_g_post_mix, m_g_pre_mlp, m_g_post_mlp, m_w_in, m_w_decay, m_b_decay, m_g_gla, m_w_gla_o, m_w_dw, m_b_dw, m_g_conv_ln, m_b_conv_ln, m_w_conv_o, m_w_pool_g, m_s_pool, m_w_pool_o, m_b_gate, m_w_out, m_w_mlp1, m_w_mlp2, v_c_ctx, v_w_ada, v_b_ada, v_g_pre_mix, v_g_post_mix, v_g_pre_mlp, v_g_post_mlp, v_w_in, v_w_decay, v_b_decay, v_g_gla, v_w_gla_o, v_w_dw, v_b_dw, v_g_conv_ln, v_b_conv_ln, v_w_conv_o, v_w_pool_g, v_s_pool, v_w_pool_o, v_b_gate, v_w_out, v_w_mlp1, v_w_mlp2):
    given = dict(x=x, c=c, ctx=ctx, c_ctx=c_ctx, w_ada=w_ada, b_ada=b_ada, g_pre_mix=g_pre_mix, g_post_mix=g_post_mix, g_pre_mlp=g_pre_mlp, g_post_mlp=g_post_mlp, w_in=w_in, w_decay=w_decay, b_decay=b_decay, g_gla=g_gla, w_gla_o=w_gla_o, w_dw=w_dw, b_dw=b_dw, g_conv_ln=g_conv_ln, b_conv_ln=b_conv_ln, w_conv_o=w_conv_o, w_pool_g=w_pool_g, s_pool=s_pool, w_pool_o=w_pool_o, b_gate=b_gate, w_out=w_out, w_mlp1=w_mlp1, w_mlp2=w_mlp2, loss_target=loss_target, m_c_ctx=m_c_ctx, m_w_ada=m_w_ada, m_b_ada=m_b_ada, m_g_pre_mix=m_g_pre_mix, m_g_post_mix=m_g_post_mix, m_g_pre_mlp=m_g_pre_mlp, m_g_post_mlp=m_g_post_mlp, m_w_in=m_w_in, m_w_decay=m_w_decay, m_b_decay=m_b_decay, m_g_gla=m_g_gla, m_w_gla_o=m_w_gla_o, m_w_dw=m_w_dw, m_b_dw=m_b_dw, m_g_conv_ln=m_g_conv_ln, m_b_conv_ln=m_b_conv_ln, m_w_conv_o=m_w_conv_o, m_w_pool_g=m_w_pool_g, m_s_pool=m_s_pool, m_w_pool_o=m_w_pool_o, m_b_gate=m_b_gate, m_w_out=m_w_out, m_w_mlp1=m_w_mlp1, m_w_mlp2=m_w_mlp2, v_c_ctx=v_c_ctx, v_w_ada=v_w_ada, v_b_ada=v_b_ada, v_g_pre_mix=v_g_pre_mix, v_g_post_mix=v_g_post_mix, v_g_pre_mlp=v_g_pre_mlp, v_g_post_mlp=v_g_post_mlp, v_w_in=v_w_in, v_w_decay=v_w_decay, v_b_decay=v_b_decay, v_g_gla=v_g_gla, v_w_gla_o=v_w_gla_o, v_w_dw=v_w_dw, v_b_dw=v_b_dw, v_g_conv_ln=v_g_conv_ln, v_b_conv_ln=v_b_conv_ln, v_w_conv_o=v_w_conv_o, v_w_pool_g=v_w_pool_g, v_s_pool=v_s_pool, v_w_pool_o=v_w_pool_o, v_b_gate=v_b_gate, v_w_out=v_w_out, v_w_mlp1=v_w_mlp1, v_w_mlp2=v_w_mlp2)
    weights = {n: given[n] for n in TWIN_WEIGHTS}
    shared = {n: given[n] for n in SHARED_INPUTS}
    per_example = {n: given[n] for n in ['x', 'c', 'ctx']}
    grad_fn = _jax.value_and_grad(_loss, argnums=(0, 1))

    def one_microbatch(ex, loss_target):
        ex = dict(ex)
        diff = ex.pop(TWIN_DIFF_INPUT)
        return grad_fn(weights, diff, {**shared, **ex}, loss_target)

    if N_MICROBATCH == 1:
        loss, (grad_w, grad_x) = one_microbatch(per_example, given["loss_target"])
    else:
        def body(carry, xs):
            loss_sum, grad_sum = carry
            l_k, (gw_k, gx_k) = one_microbatch(xs[0], xs[1])
            with _jax.named_scope("update"):
                return (loss_sum + l_k, _jax.tree.map(_jnp.add, grad_sum, gw_k)), gx_k

        init = (_jnp.zeros((), _jnp.float32), _jax.tree.map(_jnp.zeros_like, weights))
        (loss, grad_w), grad_x = _jax.lax.scan(body, init, (per_example, given["loss_target"]))
    with _jax.named_scope("update"):
        delta_w, new_m, new_v = {}, {}, {}
        for n in TWIN_WEIGHTS:
            delta_w[n], new_m[n], new_v[n] = _adamw(weights[n], grad_w[n], given["m_" + n], given["v_" + n])
    return (loss, grad_x, *[grad_w[n] for n in TWIN_WEIGHTS], *[delta_w[n] for n in TWIN_WEIGHTS],
            *[new_m[n] for n in TWIN_WEIGHTS], *[new_v[n] for n in TWIN_WEIGHTS])
```

```python
import functools
import itertools
import types

import jax
import jax.numpy as jnp
from jax import lax
from jax.experimental import pallas as pl
from jax.experimental.pallas import tpu as pltpu

F32 = jnp.float32
MM_DTYPE = jnp.bfloat16
VMEM_LIMIT_V7X = 56 * 1024 * 1024
LANES = 128
EPS = 1e-6

N_HEADS = 4
GLA_CHUNK = 64
GLA_TAU = 16.0
GLA_LR = 16
GRID_W = 64
POOL_WINDOWS = (2, 4, 8, 16)

ADAM_LR = 0.001
ADAM_B1 = 0.9
ADAM_B2 = 0.999
ADAM_EPS = 1e-08
ADAM_WD = 0.01
ADAM_STEP = 10

NN = (((1,), (0,)), ((), ()))
NT = (((1,), (1,)), ((), ()))
TN = (((0,), (0,)), ((), ()))

WEIGHTS = ['c_ctx', 'w_ada', 'b_ada', 'g_pre_mix', 'g_post_mix', 'g_pre_mlp', 'g_post_mlp', 'w_in', 'w_decay',
           'b_decay', 'g_gla', 'w_gla_o', 'w_dw', 'b_dw', 'g_conv_ln', 'b_conv_ln', 'w_conv_o', 'w_pool_g',
           's_pool', 'w_pool_o', 'b_gate', 'w_out', 'w_mlp1', 'w_mlp2']
BIG = {'w_ada': 2, 'w_in': 2, 'w_gla_o': 1, 'w_conv_o': 2, 'w_pool_o': 2, 'w_out': 1, 'w_mlp1': 2, 'w_mlp2': 1}
SMALL_SHARDED = {'w_decay': 3, 'b_decay': 2, 'w_dw': 2, 'b_gate': 2}
SMALL = [n for n in WEIGHTS if n not in BIG]


def _tile(n, prefs):
    for t in prefs:
        if n % t == 0:
            return t
    return n


def _cparams(sem=None, **kw):
    return pltpu.CompilerParams(dimension_semantics=sem, vmem_limit_bytes=VMEM_LIMIT_V7X, **kw)


def _dot(a, b, dims=NN):
    return lax.dot_general(a.astype(MM_DTYPE), b.astype(MM_DTYPE), dims, preferred_element_type=F32)


def matmul(a, b, mode, out_dtype, name, tm=None, tn=None, tk=None):
    a, aw, ablk = a if isinstance(a, tuple) else (a, a.shape[1], 0)
    if mode == 'nn':
        M, K, N = a.shape[0], aw, b.shape[1]
    elif mode == 'nt':
        M, K, N = a.shape[0], aw, b.shape[0]
    else:
        K, M, N = a.shape[0], aw, b.shape[1]
    big = (1088, 1024, 640, 544, 512, 320, 256, 128, 64, 32, 16, 8)
    if mode == 'tn':
        tm = tm or _tile(M, (1024, 512, 256, 128))
        tn = tn or _tile(N, (1024, 512, 256, 128))
        tk = tk or _tile(K, big)
    else:
        tm = tm or _tile(M, big)
        tn = tn or _tile(N, (512, 256, 128))
        tk = tk or _tile(K, (1024, 512, 256, 128))
    if aw != a.shape[1]:
        assert (mode == 'tn' and tm == aw) or (mode != 'tn' and tk == aw)
    nk = K // tk
    dims = {'nn': NN, 'nt': NT, 'tn': TN}[mode]

    def body(a_ref, b_ref, o_ref, *scr):
        p = _dot(a_ref[...], b_ref[...], dims)
        if nk == 1:
            o_ref[...] = p.astype(o_ref.dtype)
            return
        acc = scr[0]
        k = pl.program_id(2)

        @pl.when(k == 0)
        def _():
            acc[...] = p

        @pl.when(k > 0)
        def _():
            acc[...] += p

        @pl.when(k == nk - 1)
        def _():
            o_ref[...] = acc[...].astype(o_ref.dtype)

    if mode == 'nn':
        a_spec = pl.BlockSpec((tm, tk), lambda i, j, k: (i, k + ablk))
        b_spec = pl.BlockSpec((tk, tn), lambda i, j, k: (k, j))
    elif mode == 'nt':
        a_spec = pl.BlockSpec((tm, tk), lambda i, j, k: (i, k + ablk))
        b_spec = pl.BlockSpec((tn, tk), lambda i, j, k: (j, k))
    else:
        a_spec = pl.BlockSpec((tk, tm), lambda i, j, k: (k, i + ablk))
        b_spec = pl.BlockSpec((tk, tn), lambda i, j, k: (k, j))
    return pl.pallas_call(
        body, name=name, grid=(M // tm, N // tn, nk),
        in_specs=[a_spec, b_spec],
        out_specs=pl.BlockSpec((tm, tn), lambda i, j, k: (i, j)),
        out_shape=jax.ShapeDtypeStruct((M, N), out_dtype),
        scratch_shapes=[] if nk == 1 else [pltpu.VMEM((tm, tn), F32)],
        compiler_params=_cparams(("parallel", "parallel", "arbitrary")),
    )(a, b)


def group_mm(a, w, mode, out_dtype, name, b=None):
    T = a.shape[0]
    G, gc, _ = w.shape
    col = pl.BlockSpec((T, gc), lambda g: (0, g))
    wsp = pl.BlockSpec((1, gc, gc), lambda g: (g, 0, 0))
    if mode == 'tn':
        def body(a_ref, b_ref, o_ref):
            o_ref[0] = _dot(a_ref[...], b_ref[...], TN).astype(o_ref.dtype)
        return pl.pallas_call(body, name=name, grid=(G,), in_specs=[col, col], out_specs=wsp,
                              out_shape=jax.ShapeDtypeStruct((G, gc, gc), out_dtype),
                              compiler_params=_cparams(("parallel",)))(a, b)
    dims = NN if mode == 'nn' else NT

    def body(a_ref, w_ref, o_ref):
        o_ref[...] = _dot(a_ref[...], w_ref[0], dims).astype(o_ref.dtype)
    return pl.pallas_call(body, name=name, grid=(G,), in_specs=[col, wsp], out_specs=col,
                          out_shape=jax.ShapeDtypeStruct((T, G * gc), out_dtype),
                          compiler_params=_cparams(("parallel",)))(a, w)


def _rowspec(r):
    return r if isinstance(r, tuple) else (r, r.shape[1], 0)


def _row_specs(rows, segs, consts, tm, nctx):
    specs = [pl.BlockSpec((tm, w), lambda i, b=b: (i, b)) for _, w, b in rows]
    specs += [pl.BlockSpec((1,) + s.shape[1:], lambda i, n=s.ndim: (jnp.where(i >= nctx, 1, 0),) + (0,) * (n - 1))
              for s in segs]
    specs += [pl.BlockSpec(c.shape, lambda i, n=c.ndim: (0,) * n) for c in consts]
    return specs


def rowwise(fn, rows, segs, consts, outs, dm, name, tm=None):
    tm = tm or dm.tm
    nctx = dm.CTX // tm
    rows = [_rowspec(r) for r in rows]
    nr, ns, nc = len(rows), len(segs), len(consts)

    def body(*refs):
        rin = [r[...] for r in refs[:nr]]
        sin = [s[0] for s in refs[nr:nr + ns]]
        cin = [c[...] for c in refs[nr + ns:nr + ns + nc]]
        res = fn(*rin, *sin, *cin)
        for o_ref, v in zip(refs[nr + ns + nc:], res):
            o_ref[...] = v.astype(o_ref.dtype)

    res = pl.pallas_call(
        body, name=name, grid=(dm.T // tm,),
        in_specs=_row_specs(rows, segs, consts, tm, nctx),
        out_specs=[pl.BlockSpec((tm, w), lambda i: (i, 0)) for w, _ in outs],
        out_shape=[jax.ShapeDtypeStruct((dm.T, w), dt) for w, dt in outs],
        compiler_params=_cparams(("parallel",)),
    )(*[r[0] for r in rows], *segs, *consts)
    return res


def rowwise_vjp(fn, rows, segs, consts, cots, dm, name, tm=None, want=None, adds=None):
    tm = tm or dm.tm
    nctx = dm.CTX // tm
    rows = [_rowspec(r) for r in rows]
    cots = [_rowspec(r) for r in cots]
    adds = adds or {}
    nr, ns, nc, nct = len(rows), len(segs), len(consts), len(cots)
    want = want or [True] * nr
    widx = [k for k in range(nr) if want[k]]
    akeys = sorted(adds)

    def body(*refs):
        i = pl.program_id(0)
        rin = [r[...] for r in refs[:nr]]
        sin = [s[0] for s in refs[nr:nr + ns]]
        cin = [c[...] for c in refs[nr + ns:nr + ns + nc]]
        p = nr + ns + nc
        cot_refs = refs[p:p + nct]
        add_refs = dict(zip(akeys, refs[p + nct:p + nct + len(akeys)]))
        p = p + nct + len(akeys)
        rg_refs = refs[p:p + len(widx)]
        sg_refs = refs[p + len(widx):p + len(widx) + ns]
        cg_refs = refs[p + len(widx) + ns:]
        res, vjp = jax.vjp(fn, *rin, *sin, *cin)
        g = vjp(tuple(cr[...].astype(o.dtype) for cr, o in zip(cot_refs, res)))
        for o_ref, k in zip(rg_refs, widx):
            v = g[k].astype(F32)
            if k in add_refs:
                v = v + add_refs[k][...]
            o_ref[...] = v.astype(o_ref.dtype)
        first_seg = jnp.logical_or(i == 0, i == nctx)
        for o_ref, v in zip(sg_refs, g[nr:nr + ns]):
            @pl.when(first_seg)
            def _(o_ref=o_ref, v=v):
                o_ref[0] = v.astype(F32)

            @pl.when(jnp.logical_not(first_seg))
            def _(o_ref=o_ref, v=v):
                o_ref[0] += v.astype(F32)
        for o_ref, v in zip(cg_refs, g[nr + ns:]):
            @pl.when(i == 0)
            def _(o_ref=o_ref, v=v):
                o_ref[...] = v.astype(F32)

            @pl.when(i > 0)
            def _(o_ref=o_ref, v=v):
                o_ref[...] += v.astype(F32)

    in_specs = _row_specs(rows, segs, consts, tm, nctx)
    in_specs += [pl.BlockSpec((tm, w), lambda i, b=b: (i, b)) for _, w, b in cots]
    in_specs += [pl.BlockSpec((tm, adds[k].shape[1]), lambda i: (i, 0)) for k in akeys]
    out_specs = [pl.BlockSpec((tm, rows[k][1]), lambda i: (i, 0)) for k in widx]
    out_shape = [jax.ShapeDtypeStruct((dm.T, rows[k][1]), rows[k][0].dtype) for k in widx]
    out_specs += [pl.BlockSpec((1,) + s.shape[1:], lambda i, n=s.ndim: (jnp.where(i >= nctx, 1, 0),) + (0,) * (n - 1))
                  for s in segs]
    out_shape += [jax.ShapeDtypeStruct(s.shape, F32) for s in segs]
    out_specs += [pl.BlockSpec(c.shape, lambda i, n=c.ndim: (0,) * n) for c in consts]
    out_shape += [jax.ShapeDtypeStruct(c.shape, F32) for c in consts]
    res = pl.pallas_call(
        body, name=name, grid=(dm.T // tm,), in_specs=in_specs, out_specs=out_specs, out_shape=out_shape,
        compiler_params=_cparams(("arbitrary",)),
    )(*[r[0] for r in rows], *segs, *consts, *[r[0] for r in cots], *[adds[k] for k in akeys])
    rg = [None] * nr
    for k, v in zip(widx, res[:len(widx)]):
        rg[k] = v
    return rg, list(res[len(widx):len(widx) + ns]), list(res[len(widx) + ns:])


def _rms(x, g):
    return x * lax.rsqrt(jnp.mean(x * x, axis=-1, keepdims=True) + EPS) * g


def _sigmoid(x):
    return jax.nn.sigmoid(x)


def pre_fn(x, shift, scale, g):
    return ((_rms(x, g) * (1.0 + scale) + shift).astype(MM_DTYPE),)


def mid_fn(x, y, gate, shift, scale, g_post, g_pre):
    x1 = x + gate * _rms(y.astype(F32), g_post)
    return x1, (_rms(x1, g_pre) * (1.0 + scale) + shift).astype(MM_DTYPE)


def post_fn(x1, y2, gate, g):
    return (x1 + gate * _rms(y2.astype(F32), g),)


def relu2_fn(u):
    r = jnp.maximum(u.astype(F32), 0.0)
    return ((r * r).astype(MM_DTYPE),)


def decay_fn(z, bd):
    zz = z.astype(F32) + bd
    ls = jnp.minimum(zz, 0.0) - jnp.log(1.0 + jnp.exp(jnp.minimum(zz, -zz)))
    la = ls / GLA_TAU
    gk = la.shape[1] // 2
    return la[:, :gk], la[:, gk:]


def glu_fn(a, b):
    return (a.astype(F32) * _sigmoid(b.astype(F32)),)


def glaout_fn(o_f, o_b, og, g):
    o = o_f + o_b
    dv = o.shape[1] // N_HEADS
    hs = []
    for h in range(N_HEADS):
        oh = o[:, h * dv:(h + 1) * dv]
        hs.append(oh * lax.rsqrt(jnp.mean(oh * oh, axis=-1, keepdims=True) + EPS) * g[:, h * dv:(h + 1) * dv])
    og = og.astype(F32)
    return ((jnp.concatenate(hs, axis=1) * (og * _sigmoid(og))).astype(MM_DTYPE),)


def convpost_fn(y, b_dw, g, b):
    y = y + b_dw
    mu = jnp.mean(y, axis=-1, keepdims=True)
    xc = y - mu
    yn = xc * lax.rsqrt(jnp.mean(xc * xc, axis=-1, keepdims=True) + EPS) * g + b
    return ((yn * _sigmoid(yn)).astype(MM_DTYPE),)


def poolpost_fn(pc, s):
    return ((pc.astype(F32) * s).astype(MM_DTYPE),)


def merge_fn(ya, yb, yc, mg, bg0, bg1, bg2):
    d = ya.shape[1]
    mg = mg.astype(F32)
    mixed = (_sigmoid(mg[:, :d] + bg0) * ya.astype(F32) + _sigmoid(mg[:, d:2 * d] + bg1) * yb.astype(F32)
             + _sigmoid(mg[:, 2 * d:] + bg2) * yc.astype(F32))
    return (mixed.astype(MM_DTYPE),)


def _split_dot(lmat, x, dims):
    hi = x.astype(MM_DTYPE)
    lo = x - hi.astype(F32)
    return _dot(lmat, hi, dims) + _dot(lmat, lo, dims)


def _gla_block_order(dm, rev):
    nctx, nb = dm.CTX // dm.TB, dm.T // dm.TB

    def blk(i):
        if not rev:
            return i
        return jnp.where(i < nctx, nctx - 1 - i, nb - 1 - (i - nctx))
    return blk, nb


def _gla_tri(rev):
    c = GLA_CHUNK
    t = lax.broadcasted_iota(jnp.int32, (c, c), 0)
    s = lax.broadcasted_iota(jnp.int32, (c, c), 1)
    return (s >= t) if rev else (s <= t)


def _gla_chunk_terms(q, k, la, tri, scale):
    lmat = tri.astype(MM_DTYPE)
    b = _split_dot(lmat, la, NN)
    bend = jnp.sum(la, axis=0, keepdims=True)
    eb = jnp.exp(b)
    enb = jnp.exp(-b)
    ee = jnp.exp(bend - b)
    qi = q * scale * eb
    ki = k * enb
    kend = k * ee
    att = jnp.where(tri, _dot(qi, ki, NT), 0.0)
    return lmat, bend, eb, enb, ee, qi, ki, kend, att


def gla_fwd(P, la, rev, dm, name):
    c, tb, h_, dk, dv, d = GLA_CHUNK, dm.TB, N_HEADS, dm.DK, dm.DV, dm.D
    cpb = tb // c
    blk, nb = _gla_block_order(dm, rev)
    qb, kb, vb, lb = (5 * d) // dk, (5 * d + d // 2) // dk, (3 * d) // dv, (h_ if rev else 0)
    scale = dk ** -0.5
    order = list(range(cpb))[::-1] if rev else list(range(cpb))

    def body(q_ref, k_ref, v_ref, la_ref, o_ref, s_ref, st):
        @pl.when(pl.program_id(1) == 0)
        def _():
            st[...] = jnp.zeros_like(st)
        tri = _gla_tri(rev)
        for n, ci in enumerate(order):
            r = pl.ds(ci * c, c)
            q = q_ref[r, :].astype(F32)
            k = k_ref[r, :].astype(F32)
            v = v_ref[r, :]
            _, bend, _, _, _, qi, _, kend, att = _gla_chunk_terms(q, k, la_ref[r, :], tri, scale)
            s_in = st[...]
            o_ref[r, :] = _dot(att, v) + _dot(qi, s_in, NT)
            s_ref[n, 0] = s_in
            st[...] = jnp.exp(bend) * s_in + _dot(v, kend, TN)

    return pl.pallas_call(
        body, name=name, grid=(h_, nb),
        in_specs=[pl.BlockSpec((tb, dk), lambda h, i: (blk(i), qb + h)),
                  pl.BlockSpec((tb, dk), lambda h, i: (blk(i), kb + h)),
                  pl.BlockSpec((tb, dv), lambda h, i: (blk(i), vb + h)),
                  pl.BlockSpec((tb, dk), lambda h, i: (blk(i), lb + h))],
        out_specs=[pl.BlockSpec((tb, dv), lambda h, i: (blk(i), h)),
                   pl.BlockSpec((cpb, 1, dv, dk), lambda h, i: (i, h, 0, 0))],
        out_shape=[jax.ShapeDtypeStruct((dm.T, h_ * dv), F32),
                   jax.ShapeDtypeStruct((dm.T // c, h_, dv, dk), F32)],
        scratch_shapes=[pltpu.VMEM((dv, dk), F32)],
        compiler_params=_cparams(("parallel", "arbitrary")),
    )(P, P, P, la)


def gla_bwd(P, la, do, states, rev, dm, name):
    c, tb, h_, dk, dv, d = GLA_CHUNK, dm.TB, N_HEADS, dm.DK, dm.DV, dm.D
    cpb = tb // c
    blk, nb = _gla_block_order(dm, rev)
    qb, kb, vb, lb = (5 * d) // dk, (5 * d + d // 2) // dk, (3 * d) // dv, (h_ if rev else 0)
    scale = dk ** -0.5
    order = list(range(cpb))[::-1] if rev else list(range(cpb))

    def body(q_ref, k_ref, v_ref, la_ref, do_ref, s_ref, dq_ref, dk_ref, dv_ref, dla_ref, dst):
        @pl.when(pl.program_id(1) == 0)
        def _():
            dst[...] = jnp.zeros_like(dst)
        tri = _gla_tri(rev)
        for n in range(cpb - 1, -1, -1):
            r = pl.ds(order[n] * c, c)
            q = q_ref[r, :].astype(F32)
            k = k_ref[r, :].astype(F32)
            v = v_ref[r, :]
            lmat, bend, eb, enb, ee, qi, ki, kend, att = _gla_chunk_terms(q, k, la_ref[r, :], tri, scale)
            s_in = s_ref[n, 0]
            ds_out = dst[...]
            dob = do_ref[r, :]
            datt = jnp.where(tri, _dot(dob, v, NT), 0.0)
            dqi = _dot(datt, ki) + _dot(dob, s_in)
            dki = _dot(datt, qi, TN)
            dv_ref[r, :] = (_dot(att, dob, TN) + _dot(kend, ds_out, NT)).astype(dv_ref.dtype)
            dkend = _dot(v, ds_out)
            gam = jnp.exp(bend)
            dgam = jnp.sum(ds_out * s_in, axis=0, keepdims=True)
            dst[...] = gam * ds_out + _dot(dob, qi, TN)
            dq_ref[r, :] = (dqi * (scale * eb)).astype(dq_ref.dtype)
            dk_ref[r, :] = (dki * enb + dkend * ee).astype(dk_ref.dtype)
            db = dqi * qi - dki * ki - dkend * kend
            dbend = jnp.sum(dkend * kend, axis=0, keepdims=True) + dgam * gam
            dla_ref[r, :] = _split_dot(lmat, db, TN) + dbend

    def bi(j):
        return blk(nb - 1 - j)

    return pl.pallas_call(
        body, name=name, grid=(h_, nb),
        in_specs=[pl.BlockSpec((tb, dk), lambda h, j: (bi(j), qb + h)),
                  pl.BlockSpec((tb, dk), lambda h, j: (bi(j), kb + h)),
                  pl.BlockSpec((tb, dv), lambda h, j: (bi(j), vb + h)),
                  pl.BlockSpec((tb, dk), lambda h, j: (bi(j), lb + h)),
                  pl.BlockSpec((tb, dv), lambda h, j: (bi(j), h)),
                  pl.BlockSpec((cpb, 1, dv, dk), lambda h, j: (nb - 1 - j, h, 0, 0))],
        out_specs=[pl.BlockSpec((tb, dk), lambda h, j: (bi(j), h)),
                   pl.BlockSpec((tb, dk), lambda h, j: (bi(j), h)),
                   pl.BlockSpec((tb, dv), lambda h, j: (bi(j), h)),
                   pl.BlockSpec((tb, dk), lambda h, j: (bi(j), h))],
        out_shape=[jax.ShapeDtypeStruct((dm.T, h_ * dk), F32), jax.ShapeDtypeStruct((dm.T, h_ * dk), F32),
                   jax.ShapeDtypeStruct((dm.T, h_ * dv), F32), jax.ShapeDtypeStruct((dm.T, h_ * dk), F32)],
        scratch_shapes=[pltpu.VMEM((dv, dk), F32)],
        compiler_params=_cparams(("parallel", "arbitrary")),
    )(P, P, P, la, do, states)


def _pos(n, period):
    t = lax.broadcasted_iota(jnp.int32, (n, 1), 0)
    if period & (period - 1) == 0:
        return jnp.bitwise_and(t, period - 1)
    return lax.rem(t, period)


def _conv_segments(dm):
    return [(0, dm.CTX, dm.CTX), (dm.CTX, dm.SEQ, GRID_W)]


def conv_fwd(u, w, dm, name):
    kw, cw = w.shape
    segs = _conv_segments(dm)

    def body(u_ref, w_ref, y_ref):
        for r0, n, per in segs:
            useg = u_ref[r0:r0 + n, :]
            p = _pos(n, per)
            acc = jnp.zeros_like(useg)
            for kk in range(kw):
                d = kk - kw // 2
                sh = useg if d == 0 else pltpu.roll(useg, (-d) % n, 0)
                ok = jnp.logical_and(p + d >= 0, p + d < per)
                acc = acc + jnp.where(ok, sh, 0.0) * w_ref[kk:kk + 1, :]
            y_ref[r0:r0 + n, :] = acc

    return pl.pallas_call(
        body, name=name, grid=(cw // LANES,),
        in_specs=[pl.BlockSpec((dm.T, LANES), lambda j: (0, j)), pl.BlockSpec((kw, LANES), lambda j: (0, j))],
        out_specs=pl.BlockSpec((dm.T, LANES), lambda j: (0, j)),
        out_shape=jax.ShapeDtypeStruct((dm.T, cw), F32),
        compiler_params=_cparams(("parallel",)),
    )(u, w)


def conv_bwd(u, w, dy, dm, name):
    kw, cw = w.shape
    segs = _conv_segments(dm)

    def body(u_ref, w_ref, dy_ref, du_ref, dw_ref):
        dws = [jnp.zeros((1, LANES), F32)] * kw
        for r0, n, per in segs:
            useg = u_ref[r0:r0 + n, :]
            dyseg = dy_ref[r0:r0 + n, :]
            p = _pos(n, per)
            acc = jnp.zeros_like(useg)
            for kk in range(kw):
                d = kk - kw // 2
                shu = useg if d == 0 else pltpu.roll(useg, (-d) % n, 0)
                okf = jnp.logical_and(p + d >= 0, p + d < per)
                dws[kk] = dws[kk] + jnp.sum(jnp.where(okf, shu, 0.0) * dyseg, axis=0, keepdims=True)
                shd = dyseg if d == 0 else pltpu.roll(dyseg, d % n, 0)
                okb = jnp.logical_and(p - d >= 0, p - d < per)
                acc = acc + jnp.where(okb, shd, 0.0) * w_ref[kk:kk + 1, :]
            du_ref[r0:r0 + n, :] = acc
        for kk in range(kw):
            dw_ref[kk:kk + 1, :] = dws[kk]

    return pl.pallas_call(
        body, name=name, grid=(cw // LANES,),
        in_specs=[pl.BlockSpec((dm.T, LANES), lambda j: (0, j)), pl.BlockSpec((kw, LANES), lambda j: (0, j)),
                  pl.BlockSpec((dm.T, LANES), lambda j: (0, j))],
        out_specs=[pl.BlockSpec((dm.T, LANES), lambda j: (0, j)), pl.BlockSpec((kw, LANES), lambda j: (0, j))],
        out_shape=[jax.ShapeDtypeStruct((dm.T, cw), F32), jax.ShapeDtypeStruct((kw, cw), F32)],
        compiler_params=_cparams(("parallel",)),
    )(u, w, dy)


def pool_mix(u, transpose, dm, name):
    u, uw, ublk = _rowspec(u)
    gc = dm.GC
    ng = len(POOL_WINDOWS)
    rows = dm.SEQ // GRID_W
    segs = [(0, dm.CTX, 1, dm.CTX), (dm.CTX, dm.SEQ, GRID_W, rows)]

    def one_group(u_ref, o_ref, win):
        left = win // 2
        right = win - 1 - left
        for r0, n, stride, length in segs:
            useg = u_ref[r0:r0 + n, :].astype(F32)
            t = lax.broadcasted_iota(jnp.int32, (n, 1), 0)
            p = t if stride == 1 else jnp.right_shift(t, stride.bit_length() - 1)
            cnt = (jnp.minimum(p + right + 1, length) - jnp.maximum(p - left, 0)).astype(F32)
            src = useg / cnt if transpose else useg
            acc = jnp.zeros_like(useg)
            for d in range(-left, right + 1):
                dd = -d if transpose else d
                sh = src if d == 0 else pltpu.roll(src, (-dd * stride) % n, 0)
                ok = jnp.logical_and(p + dd >= 0, p + dd < length)
                acc = acc + jnp.where(ok, sh, 0.0)
            o_ref[r0:r0 + n, :] = (acc - useg) if transpose else (acc / cnt - useg)

    def body(u_ref, o_ref):
        g = pl.program_id(0)
        for gi, win in enumerate(POOL_WINDOWS):
            @pl.when(g == gi)
            def _(win=win):
                one_group(u_ref, o_ref, win)

    base = ublk * (uw // gc)
    return pl.pallas_call(
        body, name=name, grid=(ng,),
        in_specs=[pl.BlockSpec((dm.T, gc), lambda g: (0, base + g))],
        out_specs=pl.BlockSpec((dm.T, gc), lambda g: (0, g)),
        out_shape=jax.ShapeDtypeStruct((dm.T, ng * gc), F32),
        compiler_params=_cparams(("parallel",)),
    )(u)


def loss_head(x2, target, dm, name):
    tm, d = dm.tm, dm.D
    nctx = dm.CTX // tm

    def body(x_ref, t_ref, dx_ref, l_ref):
        i = pl.program_id(0)

        @pl.when(i == 0)
        def _():
            l_ref[...] = jnp.zeros_like(l_ref)

        @pl.when(i < nctx)
        def _():
            dx_ref[...] = jnp.zeros_like(dx_ref)

        @pl.when(i >= nctx)
        def _():
            e = x_ref[...] - t_ref[...]
            dx_ref[...] = e / d
            l_ref[...] += jnp.full(l_ref.shape, 0.5 * jnp.sum(jnp.mean(e * e, axis=-1)), F32)

    return pl.pallas_call(
        body, name=name, grid=(dm.T // tm,),
        in_specs=[pl.BlockSpec((tm, d), lambda i: (i, 0)),
                  pl.BlockSpec((tm, d), lambda i: (jnp.maximum(i - nctx, 0), 0))],
        out_specs=[pl.BlockSpec((tm, d), lambda i: (i, 0)), pl.BlockSpec((8, LANES), lambda i: (0, 0))],
        out_shape=[jax.ShapeDtypeStruct((dm.T, d), F32), jax.ShapeDtypeStruct((8, LANES), F32)],
        compiler_params=_cparams(("arbitrary",)),
    )(x2, target)


def adamw(w, g, m, v, name):
    r, c = w.shape
    tr = _tile(r, tuple(t for t in (512, 256, 128, 64, 32, 16, 8) if t * c * 4 <= (1 << 20)) or (8,))

    def body(w_ref, g_ref, m_ref, v_ref, d_ref, mo_ref, vo_ref):
        gg = g_ref[...]
        mm = ADAM_B1 * m_ref[...] + (1.0 - ADAM_B1) * gg
        vv = ADAM_B2 * v_ref[...] + (1.0 - ADAM_B2) * (gg * gg)
        m_hat = mm / (1.0 - ADAM_B1 ** ADAM_STEP)
        v_hat = vv / (1.0 - ADAM_B2 ** ADAM_STEP)
        d_ref[...] = -ADAM_LR * (m_hat / (jnp.sqrt(v_hat) + ADAM_EPS) + ADAM_WD * w_ref[...])
        mo_ref[...] = mm
        vo_ref[...] = vv

    spec = pl.BlockSpec((tr, c), lambda i: (i, 0))
    return pl.pallas_call(
        body, name=name, grid=(r // tr,), in_specs=[spec] * 4, out_specs=[spec] * 3,
        out_shape=[jax.ShapeDtypeStruct((r, c), F32)] * 3,
        compiler_params=_cparams(("parallel",)),
    )(w, g, m, v)


def slot_sum(buf, name):
    s, r, c = buf.shape
    tr = _tile(r, (256, 128, 64, 32, 16, 8))

    def body(b_ref, o_ref):
        acc = b_ref[0].astype(F32)
        for k in range(1, s):
            acc = acc + b_ref[k].astype(F32)
        o_ref[...] = acc

    return pl.pallas_call(
        body, name=name, grid=(r // tr,),
        in_specs=[pl.BlockSpec((s, tr, c), lambda i: (0, i, 0))],
        out_specs=pl.BlockSpec((tr, c), lambda i: (i, 0)),
        out_shape=jax.ShapeDtypeStruct((r, c), F32),
        compiler_params=_cparams(("parallel",)),
    )(buf)


def pair_add(g, r1, cidx, name):
    n, r, c = g.shape
    half = r // 2
    tr = _tile(half, (512, 256, 128, 64, 32, 16))
    nhb = half // tr

    def body(s_ref, g_ref, r_ref, o_ref):
        o_ref[...] = (g_ref[...].astype(F32) + r_ref[...].astype(F32)).astype(o_ref.dtype)

    return pl.pallas_call(
        body, name=name,
        grid_spec=pltpu.PrefetchScalarGridSpec(
            num_scalar_prefetch=1, grid=(n, nhb),
            in_specs=[pl.BlockSpec((1, tr, c), lambda k, i, s: (k, s[0] * nhb + i, 0)),
                      pl.BlockSpec((1, tr, c), lambda k, i, s: (k, i, 0))],
            out_specs=pl.BlockSpec((1, tr, c), lambda k, i, s: (k, i, 0))),
        out_shape=jax.ShapeDtypeStruct((n, half, c), g.dtype),
        compiler_params=_cparams(("parallel", "parallel")),
    )(cidx, g, r1)


def chip_add(h, r2, chip, name):
    _, r, c = h.shape
    tr = _tile(r, (512, 256, 128, 64, 32, 16))

    def body(s_ref, h_ref, r_ref, o_ref):
        acc = h_ref[0].astype(F32)
        for k in range(r2.shape[0]):
            acc = acc + r_ref[k].astype(F32)
        o_ref[...] = acc

    return pl.pallas_call(
        body, name=name,
        grid_spec=pltpu.PrefetchScalarGridSpec(
            num_scalar_prefetch=1, grid=(r // tr,),
            in_specs=[pl.BlockSpec((1, tr, c), lambda i, s: (s[0], i, 0)),
                      pl.BlockSpec((r2.shape[0], tr, c), lambda i, s: (0, i, 0))],
            out_specs=pl.BlockSpec((tr, c), lambda i, s: (i, 0))),
        out_shape=jax.ShapeDtypeStruct((r, c), F32),
        compiler_params=_cparams(("parallel",)),
    )(chip, h, r2)


MESH = pl.DeviceIdType.MESH
ANY = pl.BlockSpec(memory_space=pl.ANY)


def _place():
    return lax.axis_index("x"), lax.axis_index("y"), lax.axis_index("c")


def _rcopy(src, dst, ssem, rsem, dev):
    return pltpu.make_async_remote_copy(src_ref=src, dst_ref=dst, send_sem=ssem, recv_sem=rsem,
                                        device_id=dev, device_id_type=MESH)


def all_gather_chips(src, name):
    r, c_ = src.shape
    half = r // 2

    def body(src_ref, out_ref, ssem, rsem, lsem):
        x, y, c = _place()
        chip = 2 * x + y
        sib = (x, y, 1 - c)
        peers = [(1 - x, y), (x, 1 - y), (1 - x, 1 - y)]
        pidx = [2 * px + py for px, py in peers]

        def blk(ch, hc):
            return out_ref.at[ch, pl.ds(hc * half, half), :]

        mine = pltpu.make_async_copy(src_ref, out_ref.at[chip], lsem)
        mine.start()
        first = [_rcopy(src_ref.at[pl.ds(c * half, half), :], blk(chip, c), ssem.at[k], rsem.at[k], (px, py, c))
                 for k, (px, py) in enumerate(peers)]
        for cp in first:
            cp.start()
        passed = [_rcopy(blk(pidx[k], c), blk(pidx[k], c), ssem.at[3 + k], rsem.at[3 + k], sib) for k in range(3)]
        for k, (px, py) in enumerate(peers):
            _rcopy(blk(pidx[k], c), blk(pidx[k], c), ssem.at[k], rsem.at[k], (px, py, c)).wait_recv()
            passed[k].start()
        for k in range(3):
            _rcopy(blk(pidx[k], 1 - c), blk(pidx[k], 1 - c), ssem.at[3 + k], rsem.at[3 + k], sib).wait_recv()
        for cp in first + passed:
            cp.wait_send()
        mine.wait()

    return pl.pallas_call(
        body, name=name, in_specs=[ANY], out_specs=ANY,
        out_shape=jax.ShapeDtypeStruct((4, r, c_), src.dtype),
        scratch_shapes=[pltpu.SemaphoreType.DMA((6,)), pltpu.SemaphoreType.DMA((6,)), pltpu.SemaphoreType.DMA],
    )(src)


def pair_swap_halves(g, name):
    n, r, c_ = g.shape
    half = r // 2

    def body(g_ref, o_ref, ssem, rsem):
        x, y, c = _place()
        cp = _rcopy(g_ref.at[:, pl.ds((1 - c) * half, half), :], o_ref, ssem, rsem, (x, y, 1 - c))
        cp.start()
        cp.wait()

    return pl.pallas_call(
        body, name=name, in_specs=[ANY], out_specs=ANY,
        out_shape=jax.ShapeDtypeStruct((n, half, c_), g.dtype),
        scratch_shapes=[pltpu.SemaphoreType.DMA, pltpu.SemaphoreType.DMA],
    )(g)


def chip_exchange(h, name):
    _, r, c_ = h.shape

    def body(h_ref, o_ref, ssem, rsem):
        x, y, c = _place()
        peers = [(1 - x, y), (x, 1 - y), (1 - x, 1 - y)]
        cps = [_rcopy(h_ref.at[2 * px + py], o_ref.at[k], ssem.at[k], rsem.at[k], (px, py, c))
               for k, (px, py) in enumerate(peers)]
        for cp in cps:
            cp.start()
        for cp in cps:
            cp.wait()

    return pl.pallas_call(
        body, name=name, in_specs=[ANY], out_specs=ANY,
        out_shape=jax.ShapeDtypeStruct((3, r, c_), h.dtype),
        scratch_shapes=[pltpu.SemaphoreType.DMA((3,)), pltpu.SemaphoreType.DMA((3,))],
    )(h)


def pair_join_halves(f, name):
    half, c_ = f.shape

    def body(f_ref, o_ref, ssem, rsem, lsem):
        x, y, c = _place()
        mine = pltpu.make_async_copy(f_ref, o_ref.at[pl.ds(c * half, half), :], lsem)
        mine.start()
        cp = _rcopy(f_ref, o_ref.at[pl.ds(c * half, half), :], ssem, rsem, (x, y, 1 - c))
        cp.start()
        cp.wait_send()
        _rcopy(f_ref, o_ref.at[pl.ds((1 - c) * half, half), :], ssem, rsem, (x, y, 1 - c)).wait_recv()
        mine.wait()

    return pl.pallas_call(
        body, name=name, in_specs=[ANY], out_specs=ANY,
        out_shape=jax.ShapeDtypeStruct((2 * half, c_), f.dtype),
        scratch_shapes=[pltpu.SemaphoreType.DMA, pltpu.SemaphoreType.DMA, pltpu.SemaphoreType.DMA],
    )(f)


def gather_all_devices(buf, name):
    r, c_ = buf.shape
    offs = [o for o in itertools.product((0, 1), repeat=3) if o != (0, 0, 0)]

    def body(b_ref, o_ref, ssem, rsem, lsem):
        x, y, c = _place()
        me = 4 * x + 2 * y + c
        mine = pltpu.make_async_copy(b_ref, o_ref.at[me], lsem)
        mine.start()
        peers = [((x + dx) % 2, (y + dy) % 2, (c + dc) % 2) for dx, dy, dc in offs]
        cps = [_rcopy(b_ref, o_ref.at[me], ssem.at[k], rsem.at[k], p) for k, p in enumerate(peers)]
        for cp in cps:
            cp.start()
        for k, (px, py, pc) in enumerate(peers):
            _rcopy(b_ref, o_ref.at[4 * px + 2 * py + pc], ssem.at[k], rsem.at[k], (px, py, pc)).wait_recv()
        for cp in cps:
            cp.wait_send()
        mine.wait()

    return pl.pallas_call(
        body, name=name, in_specs=[ANY], out_specs=ANY,
        out_shape=jax.ShapeDtypeStruct((8, r, c_), buf.dtype),
        scratch_shapes=[pltpu.SemaphoreType.DMA((7,)), pltpu.SemaphoreType.DMA((7,)), pltpu.SemaphoreType.DMA],
    )(buf)


FLAT_COLS = 1024
FLAT_ROW_QUANTUM = 32


def _flatten_pad(parts, dtype):
    flat = jnp.concatenate([p.reshape(-1).astype(dtype) for p in parts])
    q = FLAT_COLS * FLAT_ROW_QUANTUM
    n = -(-flat.shape[0] // q) * q
    return jnp.pad(flat, (0, n - flat.shape[0])).reshape(n // FLAT_COLS, FLAT_COLS)


def _unshard(g4, axis):
    t = jnp.moveaxis(g4, 0, axis)
    s = t.shape
    return t.reshape(s[:axis] + (s[axis] * s[axis + 1],) + s[axis + 2:])


def _shard4(full, axis):
    s = full.shape
    t = full.reshape(s[:axis] + (4, s[axis] // 4) + s[axis + 1:])
    return jnp.moveaxis(t, axis, 0)


def _in_proj_layout(d):
    gk, gv, cw, pw = d // 2, d, d // 2, d // 2
    own = [('q', gk), ('k', gk), ('v', gv), ('og', gv), ('lrf', GLA_LR), ('lrb', GLA_LR), ('ga', cw), ('gb', cw),
           ('pu', pw), ('mg', 3 * d)]
    padded = [('mg', 3 * d), ('v', gv), ('og', gv), ('q', gk), ('k', gk), ('ga', cw), ('gb', cw), ('pu', pw),
              ('lrf', GLA_LR), ('lrb', GLA_LR), ('pad', d // 2 - 2 * GLA_LR)]
    return own, padded


def _pad_w_in(w, d):
    own, padded = _in_proj_layout(d)
    cols, start = {}, 0
    for n, wd in own:
        cols[n] = w[:, start:start + wd]
        start += wd
    return jnp.concatenate([cols[n] if n != 'pad' else jnp.zeros((w.shape[0], wd), w.dtype) for n, wd in padded], axis=1)


def _unpad_w_in(wp, d):
    own, padded = _in_proj_layout(d)
    cols, start = {}, 0
    for n, wd in padded:
        cols[n] = wp[:, start:start + wd]
        start += wd
    return jnp.concatenate([cols[n] for n, _ in own], axis=1)


def _silu_grad(z):
    s = jax.nn.sigmoid(z)
    return s + z * s * (1.0 - s)


def kernel(x, c, ctx, c_ctx, w_ada, b_ada, g_pre_mix, g_post_mix, g_pre_mlp, g_post_mlp, w_in, w_decay, b_decay, g_gla, w_gla_o, w_dw, b_dw, g_conv_ln, b_conv_ln, w_conv_o, w_pool_g, s_pool, w_pool_o, b_gate, w_out, w_mlp1, w_mlp2, loss_target, m_c_ctx, m_w_ada, m_b_ada, m_g_pre_mix, m_g_post_mix, m_g_pre_mlp, m_g_post_mlp, m_w_in, m_w_decay, m_b_decay, m_g_gla, m_w_gla_o, m_w_dw, m_b_dw, m_g_conv_ln, m_b_conv_ln, m_w_conv_o, m_w_pool_g, m_s_pool, m_w_pool_o, m_b_gate, m_w_out, m_w_mlp1, m_w_mlp2, v_c_ctx, v_w_ada, v_b_ada, v_g_pre_mix, v_g_post_mix, v_g_pre_mlp, v_g_post_mlp, v_w_in, v_w_decay, v_b_decay, v_g_gla, v_w_gla_o, v_w_dw, v_b_dw, v_g_conv_ln, v_b_conv_ln, v_w_conv_o, v_w_pool_g, v_s_pool, v_w_pool_o, v_b_gate, v_w_out, v_w_mlp1, v_w_mlp2):
    a = dict(locals())
    depth = w_in.shape[0]
    d = x.shape[-1]
    seq, nctx_rows = x.shape[1], ctx.shape[1]
    dm = types.SimpleNamespace(
        D=d, SEQ=seq, CTX=nctx_rows, T=seq + nctx_rows, DK=d // 8, DV=d // 4, GK=d // 2, GC=d // 8,
        tm=_tile(nctx_rows, (256, 128, 64)), TB=_tile(nctx_rows, (256, 128, 64)))
    assert dm.SEQ % dm.tm == 0 and dm.SEQ % GRID_W == 0 and dm.CTX % GLA_CHUNK == 0
    tmw = min(dm.tm, 128)
    chip = 2 * lax.axis_index("x") + lax.axis_index("y")
    core = lax.axis_index("c")
    chip1 = chip.astype(jnp.int32).reshape(1)
    core1 = core.astype(jnp.int32).reshape(1)

    big_names, small_names = list(BIG), list(SMALL_SHARDED)
    parts = [a[n].astype(MM_DTYPE) for n in big_names]
    if MM_DTYPE == jnp.bfloat16:
        parts += [lax.bitcast_convert_type(a[n], jnp.bfloat16) for n in small_names]
    else:
        parts += [a[n] for n in small_names]
    gathered = all_gather_chips(_flatten_pad(parts, MM_DTYPE), "all_gather_weights")
    gathered = gathered.reshape(4, -1)
    full, start = {}, 0
    for n in big_names:
        cnt = a[n].size
        full[n] = _unshard(gathered[:, start:start + cnt].reshape((4,) + a[n].shape), BIG[n])
        start += cnt
    for n in small_names:
        if MM_DTYPE == jnp.bfloat16:
            cnt = 2 * a[n].size
            blk = lax.bitcast_convert_type(gathered[:, start:start + cnt].reshape((4,) + a[n].shape + (2,)), F32)
        else:
            cnt = a[n].size
            blk = gathered[:, start:start + cnt].reshape((4,) + a[n].shape)
        full[n] = _unshard(blk, SMALL_SHARDED[n])
        start += cnt
    for n in SMALL:
        if n not in SMALL_SHARDED:
            full[n] = a[n]

    cvec = jnp.concatenate([c_ctx.reshape(1, d), c.reshape(1, d), jnp.zeros((6, d), F32)], axis=0)
    avec = (cvec * jax.nn.sigmoid(cvec)).astype(MM_DTYPE)

    def row(v):
        return v.reshape(1, -1)

    X = jnp.concatenate([ctx[0], x[0]], axis=0)
    saved = []
    gk, gv = dm.GK, d
    lrblk = (7 * d + d // 2) // LANES
    for l in range(depth):
        s = types.SimpleNamespace()
        s.w_in_p = _pad_w_in(full['w_in'][l], d)
        wd = full['w_decay'][l]
        wdp = jnp.zeros((LANES, 2 * gk), F32)
        wdp = wdp.at[:GLA_LR, :gk].set(wd[0]).at[GLA_LR:2 * GLA_LR, gk:].set(wd[1])
        s.wdp = wdp.astype(MM_DTYPE)
        s.bd = full['b_decay'][l].reshape(1, 2 * gk)
        modraw = matmul(avec, full['w_ada'][l], 'nn', F32, f"mod_{l}") + full['b_ada'][l][None, :]
        s.mod = [modraw[0:2, j * d:(j + 1) * d].reshape(2, 1, d) for j in range(6)]
        s.x = X
        (s.h,) = rowwise(pre_fn, [X], s.mod[0:2], [row(g_pre_mix[l])], [(d, MM_DTYPE)], dm, f"pre_{l}")
        s.P = matmul(s.h, s.w_in_p, 'nn', MM_DTYPE, f"in_proj_{l}")
        P = s.P
        s.z = matmul((P, LANES, lrblk), s.wdp, 'nn', F32, f"decay_proj_{l}", tk=LANES)
        la_f, la_b = rowwise(decay_fn, [s.z], [], [s.bd], [(gk, F32), (gk, F32)], dm, f"decay_{l}")
        s.la = jnp.concatenate([la_f, la_b], axis=1)
        s.o_f, s.st_f = gla_fwd(P, s.la, False, dm, f"gla_fwd_f_{l}")
        s.o_b, s.st_b = gla_fwd(P, s.la, True, dm, f"gla_fwd_b_{l}")
        (s.gin,) = rowwise(glaout_fn, [s.o_f, s.o_b, (P, d, 4)], [], [row(g_gla[l])], [(gv, MM_DTYPE)], dm,
                           f"gla_out_{l}")
        s.ya = matmul(s.gin, full['w_gla_o'][l], 'nn', F32, f"gla_o_{l}")
        (s.u,) = rowwise(glu_fn, [(P, d // 2, 12), (P, d // 2, 13)], [], [], [(d // 2, F32)], dm, f"glu_{l}")
        s.yconv = conv_fwd(s.u, full['w_dw'][l], dm, f"conv_{l}")
        (s.cin,) = rowwise(convpost_fn, [s.yconv], [], [row(b_dw[l]), row(g_conv_ln[l]), row(b_conv_ln[l])],
                           [(d // 2, MM_DTYPE)], dm, f"conv_post_{l}")
        s.yb = matmul(s.cin, full['w_conv_o'][l], 'nn', F32, f"conv_o_{l}")
        s.pm = pool_mix((P, d // 2, 14), False, dm, f"pool_mix_{l}")
        s.pc = group_mm(s.pm, w_pool_g[l], 'nn', F32, f"pool_g_{l}")
        (s.pin,) = rowwise(poolpost_fn, [s.pc], [], [row(s_pool[l])], [(d // 2, MM_DTYPE)], dm, f"pool_post_{l}")
        s.yc = matmul(s.pin, full['w_pool_o'][l], 'nn', F32, f"pool_o_{l}")
        s.bg = [row(full['b_gate'][l][j]) for j in range(3)]
        (s.mixed,) = rowwise(merge_fn, [s.ya, s.yb, s.yc, (P, 3 * d, 0)], [], s.bg, [(d, MM_DTYPE)], dm,
                             f"merge_{l}", tm=tmw)
        s.y = matmul(s.mixed, full['w_out'][l], 'nn', F32, f"out_proj_{l}")
        s.x1, s.h2 = rowwise(mid_fn, [X, s.y], s.mod[2:5], [row(g_post_mix[l]), row(g_pre_mlp[l])],
                             [(d, F32), (d, MM_DTYPE)], dm, f"mid_{l}")
        s.u1 = matmul(s.h2, full['w_mlp1'][l], 'nn', F32, f"mlp1_{l}")
        (s.act,) = rowwise(relu2_fn, [s.u1], [], [], [(4 * d, MM_DTYPE)], dm, f"relu2_{l}", tm=tmw)
        s.y2 = matmul(s.act, full['w_mlp2'][l], 'nn', F32, f"mlp2_{l}")
        (X,) = rowwise(post_fn, [s.x1, s.y2], s.mod[5:6], [row(g_post_mlp[l])], [(d, F32)], dm, f"post_{l}")
        saved.append(s)

    dX, lossv = loss_head(X, loss_target[0], dm, "loss_head")
    loss = lax.psum(lossv[0, 0], ("x", "y", "c"))

    grads = {n: [None] * depth for n in WEIGHTS if n != 'c_ctx'}
    g_cctx = jnp.zeros((d,), F32)
    for l in reversed(range(depth)):
        s = saved[l]
        P = s.P
        dmod = [None] * 6
        (dx1, dy2), (dmod[5],), (dg,) = rowwise_vjp(post_fn, [s.x1, s.y2], s.mod[5:6], [row(g_post_mlp[l])], [dX],
                                                     dm, f"post_bwd_{l}")
        grads['g_post_mlp'][l] = dg[0]
        dact = matmul(dy2, full['w_mlp2'][l], 'nt', MM_DTYPE, f"mlp2_dx_{l}")
        grads['w_mlp2'][l] = matmul(s.act, dy2, 'tn', MM_DTYPE, f"mlp2_dw_{l}")
        (du1,), _, _ = rowwise_vjp(relu2_fn, [s.u1], [], [], [dact], dm, f"relu2_bwd_{l}", tm=tmw)
        dh2 = matmul(du1, full['w_mlp1'][l], 'nt', MM_DTYPE, f"mlp1_dx_{l}")
        grads['w_mlp1'][l] = matmul(s.h2, du1, 'tn', MM_DTYPE, f"mlp1_dw_{l}")
        (dxa, dy), dmod[2:5], (dg1, dg2) = rowwise_vjp(
            mid_fn, [s.x, s.y], s.mod[2:5], [row(g_post_mix[l]), row(g_pre_mlp[l])], [dx1, dh2], dm, f"mid_bwd_{l}")
        grads['g_post_mix'][l], grads['g_pre_mlp'][l] = dg1[0], dg2[0]
        dmixed = matmul(dy, full['w_out'][l], 'nt', MM_DTYPE, f"out_proj_dx_{l}")
        grads['w_out'][l] = matmul(s.mixed, dy, 'tn', MM_DTYPE, f"out_proj_dw_{l}")
        (dya, dyb, dyc, dmg), _, dbg = rowwise_vjp(merge_fn, [s.ya, s.yb, s.yc, (P, 3 * d, 0)], [], s.bg, [dmixed],
                                                   dm, f"merge_bwd_{l}", tm=tmw)
        grads['b_gate'][l] = jnp.concatenate(dbg, axis=0)
        dgin = matmul(dya, full['w_gla_o'][l], 'nt', MM_DTYPE, f"gla_o_dx_{l}")
        grads['w_gla_o'][l] = matmul(s.gin, dya, 'tn', MM_DTYPE, f"gla_o_dw_{l}")
        dcin = matmul(dyb, full['w_conv_o'][l], 'nt', MM_DTYPE, f"conv_o_dx_{l}")
        grads['w_conv_o'][l] = matmul(s.cin, dyb, 'tn', MM_DTYPE, f"conv_o_dw_{l}")
        dpin = matmul(dyc, full['w_pool_o'][l], 'nt', MM_DTYPE, f"pool_o_dx_{l}")
        grads['w_pool_o'][l] = matmul(s.pin, dyc, 'tn', MM_DTYPE, f"pool_o_dw_{l}")
        (dpc,), _, (dsp,) = rowwise_vjp(poolpost_fn, [s.pc], [], [row(s_pool[l])], [dpin], dm, f"pool_post_bwd_{l}")
        grads['s_pool'][l] = dsp[0]
        grads['w_pool_g'][l] = group_mm(s.pm, w_pool_g[l], 'tn', F32, f"pool_g_dw_{l}", b=dpc)
        dpm = group_mm(dpc, w_pool_g[l], 'nt', F32, f"pool_g_dx_{l}")
        dpu = pool_mix(dpm, True, dm, f"pool_mix_bwd_{l}")
        (dyconv,), _, (dbdw, dgln, dbln) = rowwise_vjp(
            convpost_fn, [s.yconv], [], [row(b_dw[l]), row(g_conv_ln[l]), row(b_conv_ln[l])], [dcin], dm,
            f"conv_post_bwd_{l}")
        grads['b_dw'][l], grads['g_conv_ln'][l], grads['b_conv_ln'][l] = dbdw[0], dgln[0], dbln[0]
        du, grads['w_dw'][l] = conv_bwd(s.u, full['w_dw'][l], dyconv, dm, f"conv_bwd_{l}")
        (dga, dgb), _, _ = rowwise_vjp(glu_fn, [(P, d // 2, 12), (P, d // 2, 13)], [], [], [du], dm, f"glu_bwd_{l}")
        (do, _, dog), _, (dgg,) = rowwise_vjp(glaout_fn, [s.o_f, s.o_b, (P, d, 4)], [], [row(g_gla[l])], [dgin], dm,
                                              f"gla_out_bwd_{l}", want=[True, False, True])
        grads['g_gla'][l] = dgg[0]
        dqf, dkf, dvf, dlaf = gla_bwd(P, s.la, do, s.st_f, False, dm, f"gla_bwd_f_{l}")
        dqb, dkb, dvb, dlab = gla_bwd(P, s.la, do, s.st_b, True, dm, f"gla_bwd_b_{l}")
        (dz,), _, (dbd,) = rowwise_vjp(decay_fn, [s.z], [], [s.bd], [dlaf, dlab], dm, f"decay_bwd_{l}")
        grads['b_decay'][l] = dbd.reshape(2, gk)
        dwdp = matmul((P, LANES, lrblk), dz, 'tn', F32, f"decay_proj_dw_{l}", tm=LANES)
        grads['w_decay'][l] = jnp.stack([dwdp[:GLA_LR, :gk], dwdp[GLA_LR:2 * GLA_LR, gk:]])
        dlr = matmul(dz, s.wdp, 'nt', F32, f"decay_proj_dx_{l}")

        def asm_fn(dmg_, dvf_, dvb_, dog_, dqf_, dqb_, dkf_, dkb_, dga_, dgb_, dpu_, dlr_):
            f = lambda t: t.astype(F32)
            pad = jnp.zeros((dlr_.shape[0], d // 2 - LANES), F32)
            return (jnp.concatenate([f(dmg_), dvf_ + dvb_, f(dog_), dqf_ + dqb_, dkf_ + dkb_, f(dga_), f(dgb_),
                                     dpu_, dlr_, pad], axis=1).astype(MM_DTYPE),)
        (dP,) = rowwise(asm_fn, [dmg, dvf, dvb, dog, dqf, dqb, dkf, dkb, dga, dgb, dpu, dlr], [], [],
                        [(8 * d, MM_DTYPE)], dm, f"dproj_{l}", tm=tmw)
        dh = matmul(dP, s.w_in_p, 'nt', MM_DTYPE, f"in_proj_dx_{l}")
        grads['w_in'][l] = _unpad_w_in(matmul(s.h, dP, 'tn', MM_DTYPE, f"in_proj_dw_{l}"), d)
        (dX,), dmod[0:2], (dg,) = rowwise_vjp(pre_fn, [s.x], s.mod[0:2], [row(g_pre_mix[l])], [dh], dm,
                                               f"pre_bwd_{l}", adds={0: dxa})
        grads['g_pre_mix'][l] = dg[0]
        dmodflat = jnp.concatenate([jnp.concatenate([m_.reshape(2, d) for m_ in dmod], axis=1),
                                    jnp.zeros((6, 6 * d), F32)], axis=0)
        grads['b_ada'][l] = dmodflat[0] + dmodflat[1]
        grads['w_ada'][l] = matmul(avec, dmodflat, 'tn', MM_DTYPE, f"ada_dw_{l}")
        dav = matmul(dmodflat, full['w_ada'][l], 'nt', F32, f"ada_dx_{l}")
        g_cctx = g_cctx + dav[0] * _silu_grad(c_ctx)

    grad_x = dX[dm.CTX:][None]
    gfull = {n: jnp.stack(v) for n, v in grads.items()}
    gfull['c_ctx'] = g_cctx

    gflat = jnp.concatenate([_shard4(gfull[n], BIG[n]).reshape(4, -1).astype(MM_DTYPE) for n in big_names], axis=1)
    q = FLAT_COLS * FLAT_ROW_QUANTUM
    npad = -(-gflat.shape[1] // q) * q
    gbuf = jnp.pad(gflat, ((0, 0), (0, npad - gflat.shape[1]))).reshape(4, npad // FLAT_COLS, FLAT_COLS)
    r1 = pair_swap_halves(gbuf, "grad_pair_swap")
    hsum = pair_add(gbuf, r1, core1, "grad_pair_add")
    r2 = chip_exchange(hsum, "grad_chip_exchange")
    fhalf = chip_add(hsum, r2, chip1, "grad_chip_add")
    gred = pair_join_halves(fhalf, "grad_pair_join").reshape(-1)

    sflat = _flatten_pad([gfull[n].astype(F32) for n in SMALL], F32).reshape(-1, LANES)
    ssum = slot_sum(gather_all_devices(sflat, "small_grad_gather"), "small_grad_sum").reshape(-1)

    out_g, out_d, out_m, out_v = {}, {}, {}, {}
    start = 0
    for n in big_names:
        w = a[n]
        cnt = w.size
        g = gred[start:start + cnt].reshape(w.shape)
        start += cnt
        two = (w.shape[0] * w.shape[1], w.shape[2])
        dl, mn, vn = adamw(w.reshape(two), g.reshape(two), a['m_' + n].reshape(two), a['v_' + n].reshape(two),
                           f"adamw_{n}")
        out_g[n], out_d[n], out_m[n], out_v[n] = g, dl.reshape(w.shape), mn.reshape(w.shape), vn.reshape(w.shape)
    start = 0
    sg = {}
    for n in SMALL:
        cnt = gfull[n].size
        g = ssum[start:start + cnt].reshape(gfull[n].shape)
        start += cnt
        if n in SMALL_SHARDED:
            ax = SMALL_SHARDED[n]
            wdt = a[n].shape[ax]
            g = lax.dynamic_slice_in_dim(g, chip * wdt, wdt, axis=ax)
        sg[n] = g
    pk = lambda dct, pre: _flatten_pad([dct[pre + n] for n in SMALL], F32).reshape(-1, LANES)
    gs = _flatten_pad([sg[n] for n in SMALL], F32).reshape(-1, LANES)
    dl, mn, vn = adamw(pk(a, ''), gs, pk(a, 'm_'), pk(a, 'v_'), "adamw_small")
    dl, mn, vn = dl.reshape(-1), mn.reshape(-1), vn.reshape(-1)
    start = 0
    for n in SMALL:
        cnt, shp = a[n].size, a[n].shape
        out_g[n] = sg[n]
        out_d[n], out_m[n], out_v[n] = (t[start:start + cnt].reshape(shp) for t in (dl, mn, vn))
        start += cnt

    return (loss, grad_x, *[out_g[n] for n in WEIGHTS], *[out_d[n] for n in WEIGHTS],
            *[out_m[n] for n in WEIGHTS], *[out_v[n] for n in WEIGHTS])
```

```python
import functools
import itertools
import math
import types

import jax
import jax.numpy as jnp
from jax import lax
from jax.experimental import pallas as pl
from jax.experimental.pallas import tpu as pltpu

F32 = jnp.float32
MM_DTYPE = jnp.bfloat16
VMEM_LIMIT_V7X = 56 * 1024 * 1024
LANES = 128
EPS = 1e-6

N_HEADS = 4
GLA_CHUNK = 64
GLA_TAU = 16.0
GLA_LR = 16
GRID_W = 64
POOL_WINDOWS = (2, 4, 8, 16)

ADAM_LR = 0.001
ADAM_B1 = 0.9
ADAM_B2 = 0.999
ADAM_EPS = 1e-08
ADAM_WD = 0.01
ADAM_STEP = 10

NN = (((1,), (0,)), ((), ()))
NT = (((1,), (1,)), ((), ()))
TN = (((0,), (0,)), ((), ()))

WEIGHTS = ['c_ctx', 'w_ada', 'b_ada', 'g_pre_mix', 'g_post_mix', 'g_pre_mlp', 'g_post_mlp', 'w_in', 'w_decay',
           'b_decay', 'g_gla', 'w_gla_o', 'w_dw', 'b_dw', 'g_conv_ln', 'b_conv_ln', 'w_conv_o', 'w_pool_g',
           's_pool', 'w_pool_o', 'b_gate', 'w_out', 'w_mlp1', 'w_mlp2']
BIG = {'w_ada': 2, 'w_in': 2, 'w_gla_o': 1, 'w_conv_o': 2, 'w_pool_o': 2, 'w_out': 1, 'w_mlp1': 2, 'w_mlp2': 1}
SMALL_SHARDED = {'w_decay': 3, 'b_decay': 2, 'w_dw': 2, 'b_gate': 2}
SMALL = [n for n in WEIGHTS if n not in BIG]


def _tile(n, prefs):
    for t in prefs:
        if n % t == 0:
            return t
    return n


def _cparams(sem=None, **kw):
    return pltpu.CompilerParams(dimension_semantics=sem, vmem_limit_bytes=VMEM_LIMIT_V7X, **kw)


def _dot(a, b, dims=NN):
    return lax.dot_general(a.astype(MM_DTYPE), b.astype(MM_DTYPE), dims, preferred_element_type=F32)


def matmul(a, b, mode, out_dtype, name, tm=None, tn=None, tk=None, stack=None):
    a, aw, ablk = a if isinstance(a, tuple) else (a, a.shape[1], 0)
    b, bl = b if isinstance(b, tuple) else (b, None)
    bs = b.shape[-2:]
    if mode == 'nn':
        M, K, N = a.shape[0], aw, bs[1]
    elif mode == 'nt':
        M, K, N = a.shape[0], aw, bs[0]
    else:
        K, M, N = a.shape[0], aw, bs[1]
    big = (1088, 1024, 640, 544, 512, 320, 256, 128, 64, 32, 16, 8)
    if mode == 'tn':
        tm = tm or _tile(M, (1024, 512, 256, 128))
        tn = tn or _tile(N, (1024, 512, 256, 128))
        tk = tk or _tile(K, big)
    else:
        tm = tm or _tile(M, big)
        tn = tn or _tile(N, (512, 256, 128))
        tk = tk or _tile(K, (1024, 512, 256, 128))
    if aw != a.shape[1]:
        assert (mode == 'tn' and tm == aw) or (mode != 'tn' and tk == aw)
    nk = K // tk
    dims = {'nn': NN, 'nt': NT, 'tn': TN}[mode]

    def body(a_ref, b_ref, *rest):
        o_ref = rest[1] if (stack and stack[0] is not None) else rest[0]
        p = _dot(a_ref[...], b_ref[...], dims)
        if nk == 1:
            o_ref[...] = p.astype(o_ref.dtype)
            return
        acc = rest[-1]
        k = pl.program_id(2)

        @pl.when(k == 0)
        def _():
            acc[...] = p

        @pl.when(k > 0)
        def _():
            acc[...] += p

        @pl.when(k == nk - 1)
        def _():
            o_ref[...] = acc[...].astype(o_ref.dtype)

    lead = () if bl is None else (None,)
    pre = (lambda *ix: ix) if bl is None else (lambda *ix: (bl,) + ix)
    if mode == 'nn':
        a_spec = pl.BlockSpec((tm, tk), lambda i, j, k: (i, k + ablk))
        b_spec = pl.BlockSpec(lead + (tk, tn), lambda i, j, k: pre(k, j))
    elif mode == 'nt':
        a_spec = pl.BlockSpec((tm, tk), lambda i, j, k: (i, k + ablk))
        b_spec = pl.BlockSpec(lead + (tn, tk), lambda i, j, k: pre(j, k))
    else:
        a_spec = pl.BlockSpec((tk, tm), lambda i, j, k: (k, i + ablk))
        b_spec = pl.BlockSpec(lead + (tk, tn), lambda i, j, k: pre(k, j))
    in_specs, args, aliases = [a_spec, b_spec], [a, b], {}
    if stack is None:
        out_spec = pl.BlockSpec((tm, tn), lambda i, j, k: (i, j))
        out_shape = jax.ShapeDtypeStruct((M, N), out_dtype)
    else:
        buf, sl, depth = stack
        out_spec = pl.BlockSpec((None, tm, tn), lambda i, j, k: (sl, i, j))
        out_shape = jax.ShapeDtypeStruct((depth, M, N), out_dtype)
        if buf is not None:
            in_specs.append(pl.BlockSpec(memory_space=pl.ANY))
            args.append(buf)
            aliases = {2: 0}
    return pl.pallas_call(
        body, name=name, grid=(M // tm, N // tn, nk),
        in_specs=in_specs, out_specs=out_spec, out_shape=out_shape, input_output_aliases=aliases,
        scratch_shapes=[] if nk == 1 else [pltpu.VMEM((tm, tn), F32)],
        compiler_params=_cparams(("parallel", "parallel", "arbitrary")),
    )(*args)


def group_mm(a, w, mode, out_dtype, name, b=None):
    T = a.shape[0]
    G, gc, _ = w.shape
    col = pl.BlockSpec((T, gc), lambda g: (0, g))
    wsp = pl.BlockSpec((1, gc, gc), lambda g: (g, 0, 0))
    if mode == 'tn':
        def body(a_ref, b_ref, o_ref):
            o_ref[0] = _dot(a_ref[...], b_ref[...], TN).astype(o_ref.dtype)
        return pl.pallas_call(body, name=name, grid=(G,), in_specs=[col, col], out_specs=wsp,
                              out_shape=jax.ShapeDtypeStruct((G, gc, gc), out_dtype),
                              compiler_params=_cparams(("parallel",)))(a, b)
    dims = NN if mode == 'nn' else NT

    def body(a_ref, w_ref, o_ref):
        o_ref[...] = _dot(a_ref[...], w_ref[0], dims).astype(o_ref.dtype)
    return pl.pallas_call(body, name=name, grid=(G,), in_specs=[col, wsp], out_specs=col,
                          out_shape=jax.ShapeDtypeStruct((T, G * gc), out_dtype),
                          compiler_params=_cparams(("parallel",)))(a, w)


def _rowspec(r):
    return r if isinstance(r, tuple) else (r, r.shape[1], 0)


def _row_specs(rows, segs, consts, tm, nctx):
    specs = [pl.BlockSpec((tm, w), lambda i, b=b: (i, b)) for _, w, b in rows]
    specs += [pl.BlockSpec((1,) + s.shape[1:], lambda i, n=s.ndim: (jnp.where(i >= nctx, 1, 0),) + (0,) * (n - 1))
              for s in segs]
    specs += [pl.BlockSpec(c.shape, lambda i, n=c.ndim: (0,) * n) for c in consts]
    return specs


def rowwise(fn, rows, segs, consts, outs, dm, name, tm=None):
    tm = tm or dm.tm
    nctx = dm.CTX // tm
    rows = [_rowspec(r) for r in rows]
    nr, ns, nc = len(rows), len(segs), len(consts)

    def body(*refs):
        rin = [r[...] for r in refs[:nr]]
        sin = [s[0] for s in refs[nr:nr + ns]]
        cin = [c[...] for c in refs[nr + ns:nr + ns + nc]]
        res = fn(*rin, *sin, *cin)
        for o_ref, v in zip(refs[nr + ns + nc:], res):
            o_ref[...] = v.astype(o_ref.dtype)

    res = pl.pallas_call(
        body, name=name, grid=(dm.T // tm,),
        in_specs=_row_specs(rows, segs, consts, tm, nctx),
        out_specs=[pl.BlockSpec((tm, w), lambda i: (i, 0)) for w, _ in outs],
        out_shape=[jax.ShapeDtypeStruct((dm.T, w), dt) for w, dt in outs],
        compiler_params=_cparams(("parallel",)),
    )(*[r[0] for r in rows], *segs, *consts)
    return res


def rowwise_vjp(fn, rows, segs, consts, cots, dm, name, tm=None, want=None, adds=None):
    tm = tm or dm.tm
    nctx = dm.CTX // tm
    rows = [_rowspec(r) for r in rows]
    cots = [_rowspec(r) for r in cots]
    adds = adds or {}
    nr, ns, nc, nct = len(rows), len(segs), len(consts), len(cots)
    want = want or [True] * nr
    widx = [k for k in range(nr) if want[k]]
    akeys = sorted(adds)

    def body(*refs):
        i = pl.program_id(0)
        rin = [r[...] for r in refs[:nr]]
        sin = [s[0] for s in refs[nr:nr + ns]]
        cin = [c[...] for c in refs[nr + ns:nr + ns + nc]]
        p = nr + ns + nc
        cot_refs = refs[p:p + nct]
        add_refs = dict(zip(akeys, refs[p + nct:p + nct + len(akeys)]))
        p = p + nct + len(akeys)
        rg_refs = refs[p:p + len(widx)]
        sg_refs = refs[p + len(widx):p + len(widx) + ns]
        cg_refs = refs[p + len(widx) + ns:]
        res, vjp = jax.vjp(fn, *rin, *sin, *cin)
        g = vjp(tuple(cr[...].astype(o.dtype) for cr, o in zip(cot_refs, res)))
        for o_ref, k in zip(rg_refs, widx):
            v = g[k].astype(F32)
            if k in add_refs:
                v = v + add_refs[k][...]
            o_ref[...] = v.astype(o_ref.dtype)
        first_seg = jnp.logical_or(i == 0, i == nctx)
        for o_ref, v in zip(sg_refs, g[nr:nr + ns]):
            @pl.when(first_seg)
            def _(o_ref=o_ref, v=v):
                o_ref[0] = v.astype(F32)

            @pl.when(jnp.logical_not(first_seg))
            def _(o_ref=o_ref, v=v):
                o_ref[0] += v.astype(F32)
        for o_ref, v in zip(cg_refs, g[nr + ns:]):
            @pl.when(i == 0)
            def _(o_ref=o_ref, v=v):
                o_ref[...] = v.astype(F32)

            @pl.when(i > 0)
            def _(o_ref=o_ref, v=v):
                o_ref[...] += v.astype(F32)

    in_specs = _row_specs(rows, segs, consts, tm, nctx)
    in_specs += [pl.BlockSpec((tm, w), lambda i, b=b: (i, b)) for _, w, b in cots]
    in_specs += [pl.BlockSpec((tm, adds[k].shape[1]), lambda i: (i, 0)) for k in akeys]
    out_specs = [pl.BlockSpec((tm, rows[k][1]), lambda i: (i, 0)) for k in widx]
    out_shape = [jax.ShapeDtypeStruct((dm.T, rows[k][1]), rows[k][0].dtype) for k in widx]
    out_specs += [pl.BlockSpec((1,) + s.shape[1:], lambda i, n=s.ndim: (jnp.where(i >= nctx, 1, 0),) + (0,) * (n - 1))
                  for s in segs]
    out_shape += [jax.ShapeDtypeStruct(s.shape, F32) for s in segs]
    out_specs += [pl.BlockSpec(c.shape, lambda i, n=c.ndim: (0,) * n) for c in consts]
    out_shape += [jax.ShapeDtypeStruct(c.shape, F32) for c in consts]
    res = pl.pallas_call(
        body, name=name, grid=(dm.T // tm,), in_specs=in_specs, out_specs=out_specs, out_shape=out_shape,
        compiler_params=_cparams(("arbitrary",)),
    )(*[r[0] for r in rows], *segs, *consts, *[r[0] for r in cots], *[adds[k] for k in akeys])
    rg = [None] * nr
    for k, v in zip(widx, res[:len(widx)]):
        rg[k] = v
    return rg, list(res[len(widx):len(widx) + ns]), list(res[len(widx) + ns:])


def _rms(x, g):
    return x * lax.rsqrt(jnp.mean(x * x, axis=-1, keepdims=True) + EPS) * g


def _sigmoid(x):
    return jax.nn.sigmoid(x)


def pre_fn(x, shift, scale, g):
    return ((_rms(x, g) * (1.0 + scale) + shift).astype(MM_DTYPE),)


def mid_fn(x, y, gate, shift, scale, g_post, g_pre):
    x1 = x + gate * _rms(y.astype(F32), g_post)
    return x1, (_rms(x1, g_pre) * (1.0 + scale) + shift).astype(MM_DTYPE)


def post_fn(x1, y2, gate, g):
    return (x1 + gate * _rms(y2.astype(F32), g),)


def relu2_fn(u):
    r = jnp.maximum(u.astype(F32), 0.0)
    return ((r * r).astype(MM_DTYPE),)


def decay_fn(z, bd):
    zz = z.astype(F32) + bd
    ls = jnp.minimum(zz, 0.0) - jnp.log(1.0 + jnp.exp(jnp.minimum(zz, -zz)))
    la = ls / GLA_TAU
    gk = la.shape[1] // 2
    return la[:, :gk], la[:, gk:]


def glu_fn(a, b):
    return (a.astype(F32) * _sigmoid(b.astype(F32)),)


def glaout_fn(o_f, o_b, og, g):
    o = o_f + o_b
    dv = o.shape[1] // N_HEADS
    hs = []
    for h in range(N_HEADS):
        oh = o[:, h * dv:(h + 1) * dv]
        hs.append(oh * lax.rsqrt(jnp.mean(oh * oh, axis=-1, keepdims=True) + EPS) * g[:, h * dv:(h + 1) * dv])
    og = og.astype(F32)
    return ((jnp.concatenate(hs, axis=1) * (og * _sigmoid(og))).astype(MM_DTYPE),)


def convpost_fn(y, b_dw, g, b):
    y = y + b_dw
    mu = jnp.mean(y, axis=-1, keepdims=True)
    xc = y - mu
    yn = xc * lax.rsqrt(jnp.mean(xc * xc, axis=-1, keepdims=True) + EPS) * g + b
    return ((yn * _sigmoid(yn)).astype(MM_DTYPE),)


def poolpost_fn(pc, s):
    return ((pc.astype(F32) * s).astype(MM_DTYPE),)


def merge_fn(ya, yb, yc, mg, bg0, bg1, bg2):
    d = ya.shape[1]
    mg = mg.astype(F32)
    mixed = (_sigmoid(mg[:, :d] + bg0) * ya.astype(F32) + _sigmoid(mg[:, d:2 * d] + bg1) * yb.astype(F32)
             + _sigmoid(mg[:, 2 * d:] + bg2) * yc.astype(F32))
    return (mixed.astype(MM_DTYPE),)


def _split_dot(lmat, x, dims):
    hi = x.astype(MM_DTYPE)
    lo = x - hi.astype(F32)
    return _dot(lmat, hi, dims) + _dot(lmat, lo, dims)


def _gla_block_order(dm, rev):
    nctx, nb = dm.CTX // dm.TB, dm.T // dm.TB

    def blk(i):
        if not rev:
            return i
        return jnp.where(i < nctx, nctx - 1 - i, nb - 1 - (i - nctx))
    return blk, nb


def _gla_tri(rev):
    c = GLA_CHUNK
    t = lax.broadcasted_iota(jnp.int32, (c, c), 0)
    s = lax.broadcasted_iota(jnp.int32, (c, c), 1)
    return (s >= t) if rev else (s <= t)


def _gla_chunk_terms(q, k, la, tri, scale):
    lmat = tri.astype(MM_DTYPE)
    b = _split_dot(lmat, la, NN)
    bend = jnp.sum(la, axis=0, keepdims=True)
    eb = jnp.exp(b)
    enb = jnp.exp(-b)
    ee = jnp.exp(bend - b)
    qi = q * scale * eb
    ki = k * enb
    kend = k * ee
    att = jnp.where(tri, _dot(qi, ki, NT), 0.0)
    return lmat, bend, eb, enb, ee, qi, ki, kend, att


def gla_fwd(P, la, rev, dm, name):
    c, tb, h_, dk, dv, d = GLA_CHUNK, dm.TB, N_HEADS, dm.DK, dm.DV, dm.D
    cpb = tb // c
    blk, nb = _gla_block_order(dm, rev)
    qb, kb, vb, lb = (5 * d) // dk, (5 * d + d // 2) // dk, (3 * d) // dv, (h_ if rev else 0)
    scale = dk ** -0.5
    order = list(range(cpb))[::-1] if rev else list(range(cpb))

    def body(q_ref, k_ref, v_ref, la_ref, o_ref, s_ref, st):
        @pl.when(pl.program_id(1) == 0)
        def _():
            st[...] = jnp.zeros_like(st)
        tri = _gla_tri(rev)
        for n, ci in enumerate(order):
            r = pl.ds(ci * c, c)
            q = q_ref[r, :].astype(F32)
            k = k_ref[r, :].astype(F32)
            v = v_ref[r, :]
            _, bend, _, _, _, qi, _, kend, att = _gla_chunk_terms(q, k, la_ref[r, :], tri, scale)
            s_in = st[...]
            o_ref[r, :] = _dot(att, v) + _dot(qi, s_in, NT)
            s_ref[n, 0] = s_in
            st[...] = jnp.exp(bend) * s_in + _dot(v, kend, TN)

    return pl.pallas_call(
        body, name=name, grid=(h_, nb),
        in_specs=[pl.BlockSpec((tb, dk), lambda h, i: (blk(i), qb + h)),
                  pl.BlockSpec((tb, dk), lambda h, i: (blk(i), kb + h)),
                  pl.BlockSpec((tb, dv), lambda h, i: (blk(i), vb + h)),
                  pl.BlockSpec((tb, dk), lambda h, i: (blk(i), lb + h))],
        out_specs=[pl.BlockSpec((tb, dv), lambda h, i: (blk(i), h)),
                   pl.BlockSpec((cpb, 1, dv, dk), lambda h, i: (i, h, 0, 0))],
        out_shape=[jax.ShapeDtypeStruct((dm.T, h_ * dv), F32),
                   jax.ShapeDtypeStruct((dm.T // c, h_, dv, dk), F32)],
        scratch_shapes=[pltpu.VMEM((dv, dk), F32)],
        compiler_params=_cparams(("parallel", "arbitrary")),
    )(P, P, P, la)


def gla_bwd(P, la, do, states, rev, dm, name):
    c, tb, h_, dk, dv, d = GLA_CHUNK, dm.TB, N_HEADS, dm.DK, dm.DV, dm.D
    cpb = tb // c
    blk, nb = _gla_block_order(dm, rev)
    qb, kb, vb, lb = (5 * d) // dk, (5 * d + d // 2) // dk, (3 * d) // dv, (h_ if rev else 0)
    scale = dk ** -0.5
    order = list(range(cpb))[::-1] if rev else list(range(cpb))

    def body(q_ref, k_ref, v_ref, la_ref, do_ref, s_ref, dq_ref, dk_ref, dv_ref, dla_ref, dst):
        @pl.when(pl.program_id(1) == 0)
        def _():
            dst[...] = jnp.zeros_like(dst)
        tri = _gla_tri(rev)
        for n in range(cpb - 1, -1, -1):
            r = pl.ds(order[n] * c, c)
            q = q_ref[r, :].astype(F32)
            k = k_ref[r, :].astype(F32)
            v = v_ref[r, :]
            lmat, bend, eb, enb, ee, qi, ki, kend, att = _gla_chunk_terms(q, k, la_ref[r, :], tri, scale)
            s_in = s_ref[n, 0]
            ds_out = dst[...]
            dob = do_ref[r, :]
            datt = jnp.where(tri, _dot(dob, v, NT), 0.0)
            dqi = _dot(datt, ki) + _dot(dob, s_in)
            dki = _dot(datt, qi, TN)
            dv_ref[r, :] = (_dot(att, dob, TN) + _dot(kend, ds_out, NT)).astype(dv_ref.dtype)
            dkend = _dot(v, ds_out)
            gam = jnp.exp(bend)
            dgam = jnp.sum(ds_out * s_in, axis=0, keepdims=True)
            dst[...] = gam * ds_out + _dot(dob, qi, TN)
            dq_ref[r, :] = (dqi * (scale * eb)).astype(dq_ref.dtype)
            dk_ref[r, :] = (dki * enb + dkend * ee).astype(dk_ref.dtype)
            db = dqi * qi - dki * ki - dkend * kend
            dbend = jnp.sum(dkend * kend, axis=0, keepdims=True) + dgam * gam
            dla_ref[r, :] = _split_dot(lmat, db, TN) + dbend

    def bi(j):
        return blk(nb - 1 - j)

    return pl.pallas_call(
        body, name=name, grid=(h_, nb),
        in_specs=[pl.BlockSpec((tb, dk), lambda h, j: (bi(j), qb + h)),
                  pl.BlockSpec((tb, dk), lambda h, j: (bi(j), kb + h)),
                  pl.BlockSpec((tb, dv), lambda h, j: (bi(j), vb + h)),
                  pl.BlockSpec((tb, dk), lambda h, j: (bi(j), lb + h)),
                  pl.BlockSpec((tb, dv), lambda h, j: (bi(j), h)),
                  pl.BlockSpec((cpb, 1, dv, dk), lambda h, j: (nb - 1 - j, h, 0, 0))],
        out_specs=[pl.BlockSpec((tb, dk), lambda h, j: (bi(j), h)),
                   pl.BlockSpec((tb, dk), lambda h, j: (bi(j), h)),
                   pl.BlockSpec((tb, dv), lambda h, j: (bi(j), h)),
                   pl.BlockSpec((tb, dk), lambda h, j: (bi(j), h))],
        out_shape=[jax.ShapeDtypeStruct((dm.T, h_ * dk), F32), jax.ShapeDtypeStruct((dm.T, h_ * dk), F32),
                   jax.ShapeDtypeStruct((dm.T, h_ * dv), F32), jax.ShapeDtypeStruct((dm.T, h_ * dk), F32)],
        scratch_shapes=[pltpu.VMEM((dv, dk), F32)],
        compiler_params=_cparams(("parallel", "arbitrary")),
    )(P, P, P, la, do, states)


def _pos(n, period):
    t = lax.broadcasted_iota(jnp.int32, (n, 1), 0)
    if period & (period - 1) == 0:
        return jnp.bitwise_and(t, period - 1)
    return lax.rem(t, period)


def _conv_segments(dm):
    return [(0, dm.CTX, dm.CTX), (dm.CTX, dm.SEQ, GRID_W)]


def conv_fwd(u, w, dm, name):
    kw, cw = w.shape
    segs = _conv_segments(dm)

    def body(u_ref, w_ref, y_ref):
        for r0, n, per in segs:
            useg = u_ref[r0:r0 + n, :]
            p = _pos(n, per)
            acc = jnp.zeros_like(useg)
            for kk in range(kw):
                d = kk - kw // 2
                sh = useg if d == 0 else pltpu.roll(useg, (-d) % n, 0)
                ok = jnp.logical_and(p + d >= 0, p + d < per)
                acc = acc + jnp.where(ok, sh, 0.0) * w_ref[kk:kk + 1, :]
            y_ref[r0:r0 + n, :] = acc

    return pl.pallas_call(
        body, name=name, grid=(cw // LANES,),
        in_specs=[pl.BlockSpec((dm.T, LANES), lambda j: (0, j)), pl.BlockSpec((kw, LANES), lambda j: (0, j))],
        out_specs=pl.BlockSpec((dm.T, LANES), lambda j: (0, j)),
        out_shape=jax.ShapeDtypeStruct((dm.T, cw), F32),
        compiler_params=_cparams(("parallel",)),
    )(u, w)


def conv_bwd(u, w, dy, dm, name):
    kw, cw = w.shape
    segs = _conv_segments(dm)

    def body(u_ref, w_ref, dy_ref, du_ref, dw_ref):
        dws = [jnp.zeros((1, LANES), F32)] * kw
        for r0, n, per in segs:
            useg = u_ref[r0:r0 + n, :]
            dyseg = dy_ref[r0:r0 + n, :]
            p = _pos(n, per)
            acc = jnp.zeros_like(useg)
            for kk in range(kw):
                d = kk - kw // 2
                shu = useg if d == 0 else pltpu.roll(useg, (-d) % n, 0)
                okf = jnp.logical_and(p + d >= 0, p + d < per)
                dws[kk] = dws[kk] + jnp.sum(jnp.where(okf, shu, 0.0) * dyseg, axis=0, keepdims=True)
                shd = dyseg if d == 0 else pltpu.roll(dyseg, d % n, 0)
                okb = jnp.logical_and(p - d >= 0, p - d < per)
                acc = acc + jnp.where(okb, shd, 0.0) * w_ref[kk:kk + 1, :]
            du_ref[r0:r0 + n, :] = acc
        for kk in range(kw):
            dw_ref[kk:kk + 1, :] = dws[kk]

    return pl.pallas_call(
        body, name=name, grid=(cw // LANES,),
        in_specs=[pl.BlockSpec((dm.T, LANES), lambda j: (0, j)), pl.BlockSpec((kw, LANES), lambda j: (0, j)),
                  pl.BlockSpec((dm.T, LANES), lambda j: (0, j))],
        out_specs=[pl.BlockSpec((dm.T, LANES), lambda j: (0, j)), pl.BlockSpec((kw, LANES), lambda j: (0, j))],
        out_shape=[jax.ShapeDtypeStruct((dm.T, cw), F32), jax.ShapeDtypeStruct((kw, cw), F32)],
        compiler_params=_cparams(("parallel",)),
    )(u, w, dy)


def pool_mix(u, transpose, dm, name):
    u, uw, ublk = _rowspec(u)
    gc = dm.GC
    ng = len(POOL_WINDOWS)
    rows = dm.SEQ // GRID_W
    segs = [(0, dm.CTX, 1, dm.CTX), (dm.CTX, dm.SEQ, GRID_W, rows)]

    def one_group(u_ref, o_ref, win):
        left = win // 2
        right = win - 1 - left
        for r0, n, stride, length in segs:
            useg = u_ref[r0:r0 + n, :].astype(F32)
            t = lax.broadcasted_iota(jnp.int32, (n, 1), 0)
            p = t if stride == 1 else jnp.right_shift(t, stride.bit_length() - 1)
            cnt = (jnp.minimum(p + right + 1, length) - jnp.maximum(p - left, 0)).astype(F32)
            src = useg / cnt if transpose else useg
            acc = jnp.zeros_like(useg)
            for d in range(-left, right + 1):
                dd = -d if transpose else d
                sh = src if d == 0 else pltpu.roll(src, (-dd * stride) % n, 0)
                ok = jnp.logical_and(p + dd >= 0, p + dd < length)
                acc = acc + jnp.where(ok, sh, 0.0)
            o_ref[r0:r0 + n, :] = (acc - useg) if transpose else (acc / cnt - useg)

    def body(u_ref, o_ref):
        g = pl.program_id(0)
        for gi, win in enumerate(POOL_WINDOWS):
            @pl.when(g == gi)
            def _(win=win):
                one_group(u_ref, o_ref, win)

    base = ublk * (uw // gc)
    return pl.pallas_call(
        body, name=name, grid=(ng,),
        in_specs=[pl.BlockSpec((dm.T, gc), lambda g: (0, base + g))],
        out_specs=pl.BlockSpec((dm.T, gc), lambda g: (0, g)),
        out_shape=jax.ShapeDtypeStruct((dm.T, ng * gc), F32),
        compiler_params=_cparams(("parallel",)),
    )(u)


def loss_head(x2, target, dm, name):
    tm, d = dm.tm, dm.D
    nctx = dm.CTX // tm

    def body(x_ref, t_ref, dx_ref, l_ref):
        i = pl.program_id(0)

        @pl.when(i == 0)
        def _():
            l_ref[...] = jnp.zeros_like(l_ref)

        @pl.when(i < nctx)
        def _():
            dx_ref[...] = jnp.zeros_like(dx_ref)

        @pl.when(i >= nctx)
        def _():
            e = x_ref[...] - t_ref[...]
            dx_ref[...] = e / d
            l_ref[...] += jnp.full(l_ref.shape, 0.5 * jnp.sum(jnp.mean(e * e, axis=-1)), F32)

    return pl.pallas_call(
        body, name=name, grid=(dm.T // tm,),
        in_specs=[pl.BlockSpec((tm, d), lambda i: (i, 0)),
                  pl.BlockSpec((tm, d), lambda i: (jnp.maximum(i - nctx, 0), 0))],
        out_specs=[pl.BlockSpec((tm, d), lambda i: (i, 0)), pl.BlockSpec((8, LANES), lambda i: (0, 0))],
        out_shape=[jax.ShapeDtypeStruct((dm.T, d), F32), jax.ShapeDtypeStruct((8, LANES), F32)],
        compiler_params=_cparams(("arbitrary",)),
    )(x2, target)


def adamw(w, g, m, v, name):
    r, c = w.shape
    tr = _tile(r, tuple(t for t in (512, 256, 128, 64, 32, 16, 8) if t * c * 4 <= (1 << 20)) or (8,))

    def body(w_ref, g_ref, m_ref, v_ref, d_ref, mo_ref, vo_ref):
        gg = g_ref[...]
        mm = ADAM_B1 * m_ref[...] + (1.0 - ADAM_B1) * gg
        vv = ADAM_B2 * v_ref[...] + (1.0 - ADAM_B2) * (gg * gg)
        m_hat = mm / (1.0 - ADAM_B1 ** ADAM_STEP)
        v_hat = vv / (1.0 - ADAM_B2 ** ADAM_STEP)
        d_ref[...] = -ADAM_LR * (m_hat / (jnp.sqrt(v_hat) + ADAM_EPS) + ADAM_WD * w_ref[...])
        mo_ref[...] = mm
        vo_ref[...] = vv

    spec = pl.BlockSpec((tr, c), lambda i: (i, 0))
    return pl.pallas_call(
        body, name=name, grid=(r // tr,), in_specs=[spec] * 4, out_specs=[spec] * 3,
        out_shape=[jax.ShapeDtypeStruct((r, c), F32)] * 3,
        compiler_params=_cparams(("parallel",)),
    )(w, g, m, v)


def slot_sum(buf, name):
    s, r, c = buf.shape
    tr = _tile(r, (256, 128, 64, 32, 16, 8))

    def body(b_ref, o_ref):
        acc = b_ref[0].astype(F32)
        for k in range(1, s):
            acc = acc + b_ref[k].astype(F32)
        o_ref[...] = acc

    return pl.pallas_call(
        body, name=name, grid=(r // tr,),
        in_specs=[pl.BlockSpec((s, tr, c), lambda i: (0, i, 0))],
        out_specs=pl.BlockSpec((tr, c), lambda i: (i, 0)),
        out_shape=jax.ShapeDtypeStruct((r, c), F32),
        compiler_params=_cparams(("parallel",)),
    )(buf)


def pair_add(g, r1, cidx, name):
    _, k_, n_ = g.shape
    tr = _tile(k_, tuple(t for t in (1024, 512, 256, 128, 64, 32, 16) if t * n_ * 4 <= (2 << 20)))

    def body(s_ref, g_ref, r_ref, o_ref):
        o_ref[...] = (g_ref[...].astype(F32) + r_ref[...].astype(F32)).astype(o_ref.dtype)

    return pl.pallas_call(
        body, name=name,
        grid_spec=pltpu.PrefetchScalarGridSpec(
            num_scalar_prefetch=1, grid=(k_ // tr,),
            in_specs=[pl.BlockSpec((None, tr, n_), lambda i, s: (s[0], i, 0)),
                      pl.BlockSpec((tr, n_), lambda i, s: (i, 0))],
            out_specs=pl.BlockSpec((tr, n_), lambda i, s: (i, 0))),
        out_shape=jax.ShapeDtypeStruct((k_, n_), g.dtype),
        compiler_params=_cparams(("parallel",)),
    )(cidx, g, r1)


def chip_add(h, r2, axis, chip, name):
    _, kl, nl = r2.shape
    tr = _tile(kl, tuple(t for t in (1024, 512, 256, 128, 64, 32, 16) if t * nl * 4 <= (1 << 20)))
    nrb = kl // tr

    def body(s_ref, h_ref, r_ref, o_ref):
        acc = h_ref[...].astype(F32)
        for k in range(r2.shape[0]):
            acc = acc + r_ref[k].astype(F32)
        o_ref[...] = acc

    h_map = (lambda i, s: (s[0] * nrb + i, 0)) if axis == 0 else (lambda i, s: (i, s[0]))
    return pl.pallas_call(
        body, name=name,
        grid_spec=pltpu.PrefetchScalarGridSpec(
            num_scalar_prefetch=1, grid=(nrb,),
            in_specs=[pl.BlockSpec((tr, nl), h_map),
                      pl.BlockSpec((r2.shape[0], tr, nl), lambda i, s: (0, i, 0))],
            out_specs=pl.BlockSpec((tr, nl), lambda i, s: (i, 0))),
        out_shape=jax.ShapeDtypeStruct((kl, nl), F32),
        compiler_params=_cparams(("parallel",)),
    )(chip, h, r2)


MESH = pl.DeviceIdType.MESH
ANY = pl.BlockSpec(memory_space=pl.ANY)


def _place():
    return lax.axis_index("x"), lax.axis_index("y"), lax.axis_index("c")


def _peers(x, y):
    return [(1 - x, y), (x, 1 - y), (1 - x, 1 - y)]


def _rcopy(src, dst, ssem, rsem, dev):
    return pltpu.make_async_remote_copy(src_ref=src, dst_ref=dst, send_sem=ssem, recv_sem=rsem,
                                        device_id=dev, device_id_type=MESH)


def _window(ref, lead, axis, ch, width):
    nd = len(ref.shape) - len(lead)
    idx = tuple(lead) + tuple(pl.ds(ch * width, width) if k == axis else slice(None) for k in range(nd))
    return ref.at[idx]


def all_gather_weights(shards, axes, name):
    nw = len(shards)
    widths = [s.shape[ax] for s, ax in zip(shards, axes)]
    out_shape = [jax.ShapeDtypeStruct(s.shape[:ax] + (4 * s.shape[ax],) + s.shape[ax + 1:], s.dtype)
                 for s, ax in zip(shards, axes)]

    def body(*refs):
        src, out = refs[:nw], refs[nw:2 * nw]
        ssem, rsem, lsem = refs[2 * nw:]
        x, y, c = _place()
        chip = 2 * x + y
        sib = (x, y, 1 - c)
        peers = _peers(x, y)
        pidx = [2 * px + py for px, py in peers]

        def win(n, layer, ch):
            return _window(out[n], (layer,), axes[n] - 1, ch, widths[n])

        local = [pltpu.make_async_copy(src[n], _window(out[n], (), axes[n], chip, widths[n]), lsem.at[n])
                 for n in range(nw)]
        for cp in local:
            cp.start()
        first = [[_rcopy(src[n].at[c], win(n, c, chip), ssem.at[6 * n + k], rsem.at[6 * n + k], (px, py, c))
                  for k, (px, py) in enumerate(peers)] for n in range(nw)]
        for n in range(nw):
            for cp in first[n]:
                cp.start()
        passed = [[_rcopy(win(n, c, pidx[k]), win(n, c, pidx[k]), ssem.at[6 * n + 3 + k], rsem.at[6 * n + 3 + k], sib)
                   for k in range(3)] for n in range(nw)]
        for n in range(nw):
            for k, (px, py) in enumerate(peers):
                _rcopy(win(n, c, pidx[k]), win(n, c, pidx[k]), ssem.at[6 * n + k], rsem.at[6 * n + k],
                       (px, py, c)).wait_recv()
                passed[n][k].start()
        for n in range(nw):
            for k in range(3):
                _rcopy(win(n, 1 - c, pidx[k]), win(n, 1 - c, pidx[k]), ssem.at[6 * n + 3 + k],
                       rsem.at[6 * n + 3 + k], sib).wait_recv()
        for n in range(nw):
            for cp in first[n] + passed[n]:
                cp.wait_send()
        for cp in local:
            cp.wait()

    return pl.pallas_call(
        body, name=name, in_specs=[ANY] * nw, out_specs=[ANY] * nw, out_shape=out_shape,
        scratch_shapes=[pltpu.SemaphoreType.DMA((6 * nw,)), pltpu.SemaphoreType.DMA((6 * nw,)),
                        pltpu.SemaphoreType.DMA((nw,))],
    )(*shards)


def pair_swap_layers(gs, name):
    nw = len(gs)

    def body(*refs):
        g, o = refs[:nw], refs[nw:2 * nw]
        ssem, rsem = refs[2 * nw:]
        x, y, c = _place()
        cps = [_rcopy(g[n].at[1 - c], o[n], ssem.at[n], rsem.at[n], (x, y, 1 - c)) for n in range(nw)]
        for cp in cps:
            cp.start()
        for cp in cps:
            cp.wait()

    return pl.pallas_call(
        body, name=name, in_specs=[ANY] * nw, out_specs=[ANY] * nw,
        out_shape=[jax.ShapeDtypeStruct(g.shape[1:], g.dtype) for g in gs],
        scratch_shapes=[pltpu.SemaphoreType.DMA((nw,)), pltpu.SemaphoreType.DMA((nw,))],
    )(*gs)


def chip_exchange(hs, axes, name):
    nw = len(hs)
    shp = [tuple(d // 4 if k == ax else d for k, d in enumerate(h.shape)) for h, ax in zip(hs, axes)]

    def body(*refs):
        h, o = refs[:nw], refs[nw:2 * nw]
        ssem, rsem = refs[2 * nw:]
        x, y, c = _place()
        cps = [_rcopy(_window(h[n], (), axes[n], 2 * px + py, shp[n][axes[n]]), o[n].at[k], ssem.at[3 * n + k],
                      rsem.at[3 * n + k], (px, py, c))
               for n in range(nw) for k, (px, py) in enumerate(_peers(x, y))]
        for cp in cps:
            cp.start()
        for cp in cps:
            cp.wait()

    return pl.pallas_call(
        body, name=name, in_specs=[ANY] * nw, out_specs=[ANY] * nw,
        out_shape=[jax.ShapeDtypeStruct((3,) + sh, h.dtype) for sh, h in zip(shp, hs)],
        scratch_shapes=[pltpu.SemaphoreType.DMA((3 * nw,)), pltpu.SemaphoreType.DMA((3 * nw,))],
    )(*hs)


def pair_join_layers(fs, name):
    nw = len(fs)

    def body(*refs):
        f, o = refs[:nw], refs[nw:2 * nw]
        ssem, rsem, lsem = refs[2 * nw:]
        x, y, c = _place()
        sib = (x, y, 1 - c)
        local = [pltpu.make_async_copy(f[n], o[n].at[c], lsem.at[n]) for n in range(nw)]
        cps = [_rcopy(f[n], o[n].at[c], ssem.at[n], rsem.at[n], sib) for n in range(nw)]
        for cp in local + cps:
            cp.start()
        for n in range(nw):
            cps[n].wait_send()
            _rcopy(f[n], o[n].at[1 - c], ssem.at[n], rsem.at[n], sib).wait_recv()
        for cp in local:
            cp.wait()

    return pl.pallas_call(
        body, name=name, in_specs=[ANY] * nw, out_specs=[ANY] * nw,
        out_shape=[jax.ShapeDtypeStruct((2,) + f.shape, f.dtype) for f in fs],
        scratch_shapes=[pltpu.SemaphoreType.DMA((nw,)), pltpu.SemaphoreType.DMA((nw,)),
                        pltpu.SemaphoreType.DMA((nw,))],
    )(*fs)


def gather_all_devices(buf, name):
    r, c_ = buf.shape
    offs = [o for o in itertools.product((0, 1), repeat=3) if o != (0, 0, 0)]

    def body(b_ref, o_ref, ssem, rsem, lsem):
        x, y, c = _place()
        me = 4 * x + 2 * y + c
        mine = pltpu.make_async_copy(b_ref, o_ref.at[me], lsem)
        mine.start()
        peers = [((x + dx) % 2, (y + dy) % 2, (c + dc) % 2) for dx, dy, dc in offs]
        cps = [_rcopy(b_ref, o_ref.at[me], ssem.at[k], rsem.at[k], p) for k, p in enumerate(peers)]
        for cp in cps:
            cp.start()
        for k, (px, py, pc) in enumerate(peers):
            _rcopy(b_ref, o_ref.at[4 * px + 2 * py + pc], ssem.at[k], rsem.at[k], (px, py, pc)).wait_recv()
        for cp in cps:
            cp.wait_send()
        mine.wait()

    return pl.pallas_call(
        body, name=name, in_specs=[ANY], out_specs=ANY,
        out_shape=jax.ShapeDtypeStruct((8, r, c_), buf.dtype),
        scratch_shapes=[pltpu.SemaphoreType.DMA((7,)), pltpu.SemaphoreType.DMA((7,)), pltpu.SemaphoreType.DMA],
    )(buf)


def _flatten_pad(parts, dtype):
    flat = jnp.concatenate([p.reshape(-1).astype(dtype) for p in parts])
    q = 8 * LANES
    n = -(-flat.shape[0] // q) * q
    return jnp.pad(flat, (0, n - flat.shape[0])).reshape(n // LANES, LANES)


def _lane_pad(n):
    return -(-n // LANES) * LANES


def _in_proj_layout(d):
    gk, gv, cw, pw = d // 2, d, d // 2, d // 2
    own = [('q', gk), ('k', gk), ('v', gv), ('og', gv), ('lrf', GLA_LR), ('lrb', GLA_LR), ('ga', cw), ('gb', cw),
           ('pu', pw), ('mg', 3 * d)]
    padded = [('mg', 3 * d), ('v', gv), ('og', gv), ('q', gk), ('k', gk), ('ga', cw), ('gb', cw), ('pu', pw),
              ('lrf', GLA_LR), ('lrb', GLA_LR), ('pad', d // 2 - 2 * GLA_LR)]
    return own, padded


def _pad_w_in(w, d):
    own, padded = _in_proj_layout(d)
    cols, start = {}, 0
    for n, wd in own:
        cols[n] = w[:, start:start + wd]
        start += wd
    return jnp.concatenate([cols[n] if n != 'pad' else jnp.zeros((w.shape[0], wd), w.dtype) for n, wd in padded], axis=1)


def _unpad_w_in(wp, d):
    own, padded = _in_proj_layout(d)
    cols, start = {}, 0
    for n, wd in padded:
        cols[n] = wp[:, start:start + wd]
        start += wd
    return jnp.concatenate([cols[n] for n, _ in own], axis=1)


def _silu_grad(z):
    s = jax.nn.sigmoid(z)
    return s + z * s * (1.0 - s)


def kernel(x, c, ctx, c_ctx, w_ada, b_ada, g_pre_mix, g_post_mix, g_pre_mlp, g_post_mlp, w_in, w_decay, b_decay, g_gla, w_gla_o, w_dw, b_dw, g_conv_ln, b_conv_ln, w_conv_o, w_pool_g, s_pool, w_pool_o, b_gate, w_out, w_mlp1, w_mlp2, loss_target, m_c_ctx, m_w_ada, m_b_ada, m_g_pre_mix, m_g_post_mix, m_g_pre_mlp, m_g_post_mlp, m_w_in, m_w_decay, m_b_decay, m_g_gla, m_w_gla_o, m_w_dw, m_b_dw, m_g_conv_ln, m_b_conv_ln, m_w_conv_o, m_w_pool_g, m_s_pool, m_w_pool_o, m_b_gate, m_w_out, m_w_mlp1, m_w_mlp2, v_c_ctx, v_w_ada, v_b_ada, v_g_pre_mix, v_g_post_mix, v_g_pre_mlp, v_g_post_mlp, v_w_in, v_w_decay, v_b_decay, v_g_gla, v_w_gla_o, v_w_dw, v_b_dw, v_g_conv_ln, v_b_conv_ln, v_w_conv_o, v_w_pool_g, v_s_pool, v_w_pool_o, v_b_gate, v_w_out, v_w_mlp1, v_w_mlp2):
    a = dict(locals())
    depth = w_in.shape[0]
    d = x.shape[-1]
    seq, nctx_rows = x.shape[1], ctx.shape[1]
    dm = types.SimpleNamespace(
        D=d, SEQ=seq, CTX=nctx_rows, T=seq + nctx_rows, DK=d // 8, DV=d // 4, GK=d // 2, GC=d // 8,
        tm=_tile(nctx_rows, (256, 128, 64)), TB=_tile(nctx_rows, (256, 128, 64)))
    assert dm.SEQ % dm.tm == 0 and dm.SEQ % GRID_W == 0 and dm.CTX % GLA_CHUNK == 0
    tmw = min(dm.tm, 128)
    chip = 2 * lax.axis_index("x") + lax.axis_index("y")
    core = lax.axis_index("c")
    chip1 = chip.astype(jnp.int32).reshape(1)
    core1 = core.astype(jnp.int32).reshape(1)

    big_names, small_names = list(BIG), list(SMALL_SHARDED)
    wl = w_in.shape[2]
    wlp = _lane_pad(wl)
    def rows8(t):
        t = t.reshape(t.shape[0], -1, t.shape[-1])
        return jnp.pad(t, ((0, 0), (0, -t.shape[1] % 8), (0, 0)))
    shards = [(jnp.pad(a[n], ((0, 0), (0, 0), (0, wlp - wl))) if n == 'w_in' else a[n]).astype(MM_DTYPE)
              for n in big_names] + [rows8(a[n]) for n in small_names]
    gathered = all_gather_weights(shards, [BIG[n] for n in big_names] + [2] * len(small_names), "all_gather_weights")
    full = dict(zip(big_names, gathered))
    for n, g in zip(small_names, gathered[len(big_names):]):
        shp = a[n].shape
        full[n] = g[:, :math.prod(shp[1:-1])].reshape(shp[:-1] + (4 * shp[-1],))
    for n in SMALL:
        if n not in SMALL_SHARDED:
            full[n] = a[n]

    cvec = jnp.concatenate([c_ctx.reshape(1, d), c.reshape(1, d), jnp.zeros((6, d), F32)], axis=0)
    avec = (cvec * jax.nn.sigmoid(cvec)).astype(MM_DTYPE)

    def row(v):
        return v.reshape(1, -1)

    X = jnp.concatenate([ctx[0], x[0]], axis=0)
    saved = []
    gk, gv = dm.GK, d
    lrblk = (7 * d + d // 2) // LANES
    for l in range(depth):
        s = types.SimpleNamespace()
        s.w_in_p = _pad_w_in(full['w_in'][l].reshape(d, 4, wlp)[:, :, :wl].reshape(d, 4 * wl), d)
        wd = full['w_decay'][l]
        wdp = jnp.zeros((LANES, 2 * gk), F32)
        wdp = wdp.at[:GLA_LR, :gk].set(wd[0]).at[GLA_LR:2 * GLA_LR, gk:].set(wd[1])
        s.wdp = wdp.astype(MM_DTYPE)
        s.bd = full['b_decay'][l].reshape(1, 2 * gk)
        modraw = matmul(avec, (full['w_ada'], l), 'nn', F32, f"mod_{l}") + full['b_ada'][l][None, :]
        s.mod = [modraw[0:2, j * d:(j + 1) * d].reshape(2, 1, d) for j in range(6)]
        s.x = X
        (s.h,) = rowwise(pre_fn, [X], s.mod[0:2], [row(g_pre_mix[l])], [(d, MM_DTYPE)], dm, f"pre_{l}")
        s.P = matmul(s.h, s.w_in_p, 'nn', MM_DTYPE, f"in_proj_{l}")
        P = s.P
        s.z = matmul((P, LANES, lrblk), s.wdp, 'nn', F32, f"decay_proj_{l}", tk=LANES)
        la_f, la_b = rowwise(decay_fn, [s.z], [], [s.bd], [(gk, F32), (gk, F32)], dm, f"decay_{l}")
        s.la = jnp.concatenate([la_f, la_b], axis=1)
        s.o_f, s.st_f = gla_fwd(P, s.la, False, dm, f"gla_fwd_f_{l}")
        s.o_b, s.st_b = gla_fwd(P, s.la, True, dm, f"gla_fwd_b_{l}")
        (s.gin,) = rowwise(glaout_fn, [s.o_f, s.o_b, (P, d, 4)], [], [row(g_gla[l])], [(gv, MM_DTYPE)], dm,
                           f"gla_out_{l}")
        s.ya = matmul(s.gin, (full['w_gla_o'], l), 'nn', F32, f"gla_o_{l}")
        (s.u,) = rowwise(glu_fn, [(P, d // 2, 12), (P, d // 2, 13)], [], [], [(d // 2, F32)], dm, f"glu_{l}")
        s.yconv = conv_fwd(s.u, full['w_dw'][l], dm, f"conv_{l}")
        (s.cin,) = rowwise(convpost_fn, [s.yconv], [], [row(b_dw[l]), row(g_conv_ln[l]), row(b_conv_ln[l])],
                           [(d // 2, MM_DTYPE)], dm, f"conv_post_{l}")
        s.yb = matmul(s.cin, (full['w_conv_o'], l), 'nn', F32, f"conv_o_{l}")
        s.pm = pool_mix((P, d // 2, 14), False, dm, f"pool_mix_{l}")
        s.pc = group_mm(s.pm, w_pool_g[l], 'nn', F32, f"pool_g_{l}")
        (s.pin,) = rowwise(poolpost_fn, [s.pc], [], [row(s_pool[l])], [(d // 2, MM_DTYPE)], dm, f"pool_post_{l}")
        s.yc = matmul(s.pin, (full['w_pool_o'], l), 'nn', F32, f"pool_o_{l}")
        s.bg = [row(full['b_gate'][l][j]) for j in range(3)]
        (s.mixed,) = rowwise(merge_fn, [s.ya, s.yb, s.yc, (P, 3 * d, 0)], [], s.bg, [(d, MM_DTYPE)], dm,
                             f"merge_{l}", tm=tmw)
        s.y = matmul(s.mixed, (full['w_out'], l), 'nn', F32, f"out_proj_{l}")
        s.x1, s.h2 = rowwise(mid_fn, [X, s.y], s.mod[2:5], [row(g_post_mix[l]), row(g_pre_mlp[l])],
                             [(d, F32), (d, MM_DTYPE)], dm, f"mid_{l}")
        s.u1 = matmul(s.h2, (full['w_mlp1'], l), 'nn', F32, f"mlp1_{l}")
        (s.act,) = rowwise(relu2_fn, [s.u1], [], [], [(4 * d, MM_DTYPE)], dm, f"relu2_{l}", tm=tmw)
        s.y2 = matmul(s.act, (full['w_mlp2'], l), 'nn', F32, f"mlp2_{l}")
        (X,) = rowwise(post_fn, [s.x1, s.y2], s.mod[5:6], [row(g_post_mlp[l])], [(d, F32)], dm, f"post_{l}")
        saved.append(s)

    dX, lossv = loss_head(X, loss_target[0], dm, "loss_head")
    loss = lax.psum(lossv[0, 0], ("x", "y", "c"))

    grads = {n: [None] * depth for n in WEIGHTS if n != 'c_ctx' and n not in BIG}
    gbig = {n: None for n in BIG}
    win_l = [None] * depth
    g_cctx = jnp.zeros((d,), F32)
    for l in reversed(range(depth)):
        s = saved[l]
        P = s.P
        dmod = [None] * 6
        (dx1, dy2), (dmod[5],), (dg,) = rowwise_vjp(post_fn, [s.x1, s.y2], s.mod[5:6], [row(g_post_mlp[l])], [dX],
                                                     dm, f"post_bwd_{l}")
        grads['g_post_mlp'][l] = dg[0]
        dact = matmul(dy2, (full['w_mlp2'], l), 'nt', MM_DTYPE, f"mlp2_dx_{l}")
        gbig['w_mlp2'] = matmul(s.act, dy2, 'tn', MM_DTYPE, f"mlp2_dw_{l}", stack=(gbig['w_mlp2'], l, depth))
        (du1,), _, _ = rowwise_vjp(relu2_fn, [s.u1], [], [], [dact], dm, f"relu2_bwd_{l}", tm=tmw)
        dh2 = matmul(du1, (full['w_mlp1'], l), 'nt', MM_DTYPE, f"mlp1_dx_{l}")
        gbig['w_mlp1'] = matmul(s.h2, du1, 'tn', MM_DTYPE, f"mlp1_dw_{l}", stack=(gbig['w_mlp1'], l, depth))
        (dxa, dy), dmod[2:5], (dg1, dg2) = rowwise_vjp(
            mid_fn, [s.x, s.y], s.mod[2:5], [row(g_post_mix[l]), row(g_pre_mlp[l])], [dx1, dh2], dm, f"mid_bwd_{l}")
        grads['g_post_mix'][l], grads['g_pre_mlp'][l] = dg1[0], dg2[0]
        dmixed = matmul(dy, (full['w_out'], l), 'nt', MM_DTYPE, f"out_proj_dx_{l}")
        gbig['w_out'] = matmul(s.mixed, dy, 'tn', MM_DTYPE, f"out_proj_dw_{l}", stack=(gbig['w_out'], l, depth))
        (dya, dyb, dyc, dmg), _, dbg = rowwise_vjp(merge_fn, [s.ya, s.yb, s.yc, (P, 3 * d, 0)], [], s.bg, [dmixed],
                                                   dm, f"merge_bwd_{l}", tm=tmw)
        grads['b_gate'][l] = jnp.concatenate(dbg, axis=0)
        dgin = matmul(dya, (full['w_gla_o'], l), 'nt', MM_DTYPE, f"gla_o_dx_{l}")
        gbig['w_gla_o'] = matmul(s.gin, dya, 'tn', MM_DTYPE, f"gla_o_dw_{l}", stack=(gbig['w_gla_o'], l, depth))
        dcin = matmul(dyb, (full['w_conv_o'], l), 'nt', MM_DTYPE, f"conv_o_dx_{l}")
        gbig['w_conv_o'] = matmul(s.cin, dyb, 'tn', MM_DTYPE, f"conv_o_dw_{l}", stack=(gbig['w_conv_o'], l, depth))
        dpin = matmul(dyc, (full['w_pool_o'], l), 'nt', MM_DTYPE, f"pool_o_dx_{l}")
        gbig['w_pool_o'] = matmul(s.pin, dyc, 'tn', MM_DTYPE, f"pool_o_dw_{l}", stack=(gbig['w_pool_o'], l, depth))
        (dpc,), _, (dsp,) = rowwise_vjp(poolpost_fn, [s.pc], [], [row(s_pool[l])], [dpin], dm, f"pool_post_bwd_{l}")
        grads['s_pool'][l] = dsp[0]
        grads['w_pool_g'][l] = group_mm(s.pm, w_pool_g[l], 'tn', F32, f"pool_g_dw_{l}", b=dpc)
        dpm = group_mm(dpc, w_pool_g[l], 'nt', F32, f"pool_g_dx_{l}")
        dpu = pool_mix(dpm, True, dm, f"pool_mix_bwd_{l}")
        (dyconv,), _, (dbdw, dgln, dbln) = rowwise_vjp(
            convpost_fn, [s.yconv], [], [row(b_dw[l]), row(g_conv_ln[l]), row(b_conv_ln[l])], [dcin], dm,
            f"conv_post_bwd_{l}")
        grads['b_dw'][l], grads['g_conv_ln'][l], grads['b_conv_ln'][l] = dbdw[0], dgln[0], dbln[0]
        du, grads['w_dw'][l] = conv_bwd(s.u, full['w_dw'][l], dyconv, dm, f"conv_bwd_{l}")
        (dga, dgb), _, _ = rowwise_vjp(glu_fn, [(P, d // 2, 12), (P, d // 2, 13)], [], [], [du], dm, f"glu_bwd_{l}")
        (do, _, dog), _, (dgg,) = rowwise_vjp(glaout_fn, [s.o_f, s.o_b, (P, d, 4)], [], [row(g_gla[l])], [dgin], dm,
                                              f"gla_out_bwd_{l}", want=[True, False, True])
        grads['g_gla'][l] = dgg[0]
        dqf, dkf, dvf, dlaf = gla_bwd(P, s.la, do, s.st_f, False, dm, f"gla_bwd_f_{l}")
        dqb, dkb, dvb, dlab = gla_bwd(P, s.la, do, s.st_b, True, dm, f"gla_bwd_b_{l}")
        (dz,), _, (dbd,) = rowwise_vjp(decay_fn, [s.z], [], [s.bd], [dlaf, dlab], dm, f"decay_bwd_{l}")
        grads['b_decay'][l] = dbd.reshape(2, gk)
        dwdp = matmul((P, LANES, lrblk), dz, 'tn', F32, f"decay_proj_dw_{l}", tm=LANES)
        grads['w_decay'][l] = jnp.stack([dwdp[:GLA_LR, :gk], dwdp[GLA_LR:2 * GLA_LR, gk:]])
        dlr = matmul(dz, s.wdp, 'nt', F32, f"decay_proj_dx_{l}")

        def asm_fn(dmg_, dvf_, dvb_, dog_, dqf_, dqb_, dkf_, dkb_, dga_, dgb_, dpu_, dlr_):
            f = lambda t: t.astype(F32)
            pad = jnp.zeros((dlr_.shape[0], d // 2 - LANES), F32)
            return (jnp.concatenate([f(dmg_), dvf_ + dvb_, f(dog_), dqf_ + dqb_, dkf_ + dkb_, f(dga_), f(dgb_),
                                     dpu_, dlr_, pad], axis=1).astype(MM_DTYPE),)
        (dP,) = rowwise(asm_fn, [dmg, dvf, dvb, dog, dqf, dqb, dkf, dkb, dga, dgb, dpu, dlr], [], [],
                        [(8 * d, MM_DTYPE)], dm, f"dproj_{l}", tm=tmw)
        dh = matmul(dP, s.w_in_p, 'nt', MM_DTYPE, f"in_proj_dx_{l}")
        gwin = _unpad_w_in(matmul(s.h, dP, 'tn', MM_DTYPE, f"in_proj_dw_{l}"), d)
        win_l[l] = jnp.pad(gwin.reshape(d, 4, wl), ((0, 0), (0, 0), (0, wlp - wl))).reshape(d, 4 * wlp)
        (dX,), dmod[0:2], (dg,) = rowwise_vjp(pre_fn, [s.x], s.mod[0:2], [row(g_pre_mix[l])], [dh], dm,
                                               f"pre_bwd_{l}", adds={0: dxa})
        grads['g_pre_mix'][l] = dg[0]
        dmodflat = jnp.concatenate([jnp.concatenate([m_.reshape(2, d) for m_ in dmod], axis=1),
                                    jnp.zeros((6, 6 * d), F32)], axis=0)
        grads['b_ada'][l] = dmodflat[0] + dmodflat[1]
        gbig['w_ada'] = matmul(avec, dmodflat, 'tn', MM_DTYPE, f"ada_dw_{l}", stack=(gbig['w_ada'], l, depth))
        dav = matmul(dmodflat, (full['w_ada'], l), 'nt', F32, f"ada_dx_{l}")
        g_cctx = g_cctx + dav[0] * _silu_grad(c_ctx)

    grad_x = dX[dm.CTX:][None]
    gbig['w_in'] = jnp.stack(win_l)
    gfull = {n: jnp.stack(v) for n, v in grads.items()}
    gfull['c_ctx'] = g_cctx

    ax2 = [BIG[n] - 1 for n in big_names]
    r1 = pair_swap_layers([gbig[n] for n in big_names], "grad_pair_swap")
    hs = [pair_add(gbig[n], r, core1, f"grad_pair_add_{n}") for n, r in zip(big_names, r1)]
    r2 = chip_exchange(hs, ax2, "grad_chip_exchange")
    fs = [chip_add(h, r, ax, chip1, f"grad_chip_add_{n}") for n, h, r, ax in zip(big_names, hs, r2, ax2)]
    gred = dict(zip(big_names, pair_join_layers(fs, "grad_pair_join")))
    gred['w_in'] = gred['w_in'][:, :, :wl]

    sflat = _flatten_pad([gfull[n].astype(F32) for n in SMALL], F32)
    ssum = slot_sum(gather_all_devices(sflat, "small_grad_gather"), "small_grad_sum").reshape(-1)

    out_g, out_d, out_m, out_v = {}, {}, {}, {}
    for n in big_names:
        w = a[n]
        two = (w.shape[0] * w.shape[1], w.shape[2])
        dl, mn, vn = adamw(w.reshape(two), gred[n].reshape(two), a['m_' + n].reshape(two), a['v_' + n].reshape(two),
                           f"adamw_{n}")
        out_g[n], out_d[n], out_m[n], out_v[n] = gred[n], dl.reshape(w.shape), mn.reshape(w.shape), vn.reshape(w.shape)
    start = 0
    sg = {}
    for n in SMALL:
        cnt = gfull[n].size
        g = ssum[start:start + cnt].reshape(gfull[n].shape)
        start += cnt
        if n in SMALL_SHARDED:
            ax = SMALL_SHARDED[n]
            wdt = a[n].shape[ax]
            g = lax.dynamic_slice_in_dim(g, chip * wdt, wdt, axis=ax)
        sg[n] = g
    pk = lambda dct, pre: _flatten_pad([dct[pre + n] for n in SMALL], F32)
    gs = _flatten_pad([sg[n] for n in SMALL], F32)
    dl, mn, vn = adamw(pk(a, ''), gs, pk(a, 'm_'), pk(a, 'v_'), "adamw_small")
    dl, mn, vn = dl.reshape(-1), mn.reshape(-1), vn.reshape(-1)
    start = 0
    for n in SMALL:
        cnt, shp = a[n].size, a[n].shape
        out_g[n] = sg[n]
        out_d[n], out_m[n], out_v[n] = (t[start:start + cnt].reshape(shp) for t in (dl, mn, vn))
        start += cnt

    return (loss, grad_x, *[out_g[n] for n in WEIGHTS], *[out_d[n] for n in WEIGHTS],
            *[out_m[n] for n in WEIGHTS], *[out_v[n] for n in WEIGHTS])
```

```python
import functools
import itertools
import math
import types

import jax
import jax.numpy as jnp
from jax import lax
from jax.experimental import pallas as pl
from jax.experimental.pallas import tpu as pltpu

F32 = jnp.float32
MM_DTYPE = jnp.bfloat16
VMEM_LIMIT_V7X = 56 * 1024 * 1024
LANES = 128
EPS = 1e-6

N_HEADS = 4
GLA_CHUNK = 64
GLA_TAU = 16.0
GLA_LR = 16
GRID_W = 64
POOL_WINDOWS = (2, 4, 8, 16)

ADAM_LR = 0.001
ADAM_B1 = 0.9
ADAM_B2 = 0.999
ADAM_EPS = 1e-08
ADAM_WD = 0.01
ADAM_STEP = 10

NN = (((1,), (0,)), ((), ()))
NT = (((1,), (1,)), ((), ()))
TN = (((0,), (0,)), ((), ()))

WEIGHTS = ['c_ctx', 'w_ada', 'b_ada', 'g_pre_mix', 'g_post_mix', 'g_pre_mlp', 'g_post_mlp', 'w_in', 'w_decay',
           'b_decay', 'g_gla', 'w_gla_o', 'w_dw', 'b_dw', 'g_conv_ln', 'b_conv_ln', 'w_conv_o', 'w_pool_g',
           's_pool', 'w_pool_o', 'b_gate', 'w_out', 'w_mlp1', 'w_mlp2']
BIG = {'w_ada': 2, 'w_in': 2, 'w_gla_o': 1, 'w_conv_o': 2, 'w_pool_o': 2, 'w_out': 1, 'w_mlp1': 2, 'w_mlp2': 1}
SMALL_SHARDED = {'w_decay': 3, 'b_decay': 2, 'w_dw': 2, 'b_gate': 2}
SMALL = [n for n in WEIGHTS if n not in BIG]


def _tile(n, prefs):
    for t in prefs:
        if n % t == 0:
            return t
    return n


def _cparams(sem=None, **kw):
    return pltpu.CompilerParams(dimension_semantics=sem, vmem_limit_bytes=VMEM_LIMIT_V7X, **kw)


def _dot(a, b, dims=NN):
    return lax.dot_general(a.astype(MM_DTYPE), b.astype(MM_DTYPE), dims, preferred_element_type=F32)


def matmul(a, b, mode, out_dtype, name, tm=None, tn=None, tk=None, stack=None):
    a, aw, ablk = a if isinstance(a, tuple) else (a, a.shape[1], 0)
    b, bl = b if isinstance(b, tuple) else (b, None)
    bs = b.shape[-2:]
    if mode == 'nn':
        M, K, N = a.shape[0], aw, bs[1]
    elif mode == 'nt':
        M, K, N = a.shape[0], aw, bs[0]
    else:
        K, M, N = a.shape[0], aw, bs[1]
    big = (1088, 1024, 640, 544, 512, 320, 256, 128, 64, 32, 16, 8)
    if mode == 'tn':
        tm = tm or _tile(M, (1024, 512, 256, 128))
        tn = tn or _tile(N, (1024, 512, 256, 128))
        tk = tk or _tile(K, big)
    else:
        tm = tm or _tile(M, big)
        tn = tn or _tile(N, (512, 256, 128))
        tk = tk or _tile(K, (1024, 512, 256, 128))
    if aw != a.shape[1]:
        assert (mode == 'tn' and tm == aw) or (mode != 'tn' and tk == aw)
    nk = K // tk
    dims = {'nn': NN, 'nt': NT, 'tn': TN}[mode]

    def body(a_ref, b_ref, *rest):
        o_ref = rest[1] if (stack and stack[0] is not None) else rest[0]
        p = _dot(a_ref[...], b_ref[...], dims)
        if nk == 1:
            o_ref[...] = p.astype(o_ref.dtype)
            return
        acc = rest[-1]
        k = pl.program_id(2)

        @pl.when(k == 0)
        def _():
            acc[...] = p

        @pl.when(k > 0)
        def _():
            acc[...] += p

        @pl.when(k == nk - 1)
        def _():
            o_ref[...] = acc[...].astype(o_ref.dtype)

    lead = () if bl is None else (None,)
    pre = (lambda *ix: ix) if bl is None else (lambda *ix: (bl,) + ix)
    if mode == 'nn':
        a_spec = pl.BlockSpec((tm, tk), lambda i, j, k: (i, k + ablk))
        b_spec = pl.BlockSpec(lead + (tk, tn), lambda i, j, k: pre(k, j))
    elif mode == 'nt':
        a_spec = pl.BlockSpec((tm, tk), lambda i, j, k: (i, k + ablk))
        b_spec = pl.BlockSpec(lead + (tn, tk), lambda i, j, k: pre(j, k))
    else:
        a_spec = pl.BlockSpec((tk, tm), lambda i, j, k: (k, i + ablk))
        b_spec = pl.BlockSpec(lead + (tk, tn), lambda i, j, k: pre(k, j))
    in_specs, args, aliases = [a_spec, b_spec], [a, b], {}
    if stack is None:
        out_spec = pl.BlockSpec((tm, tn), lambda i, j, k: (i, j))
        out_shape = jax.ShapeDtypeStruct((M, N), out_dtype)
    else:
        buf, sl, depth = stack
        out_spec = pl.BlockSpec((None, tm, tn), lambda i, j, k: (sl, i, j))
        out_shape = jax.ShapeDtypeStruct((depth, M, N), out_dtype)
        if buf is not None:
            in_specs.append(pl.BlockSpec(memory_space=pl.ANY))
            args.append(buf)
            aliases = {2: 0}
    return pl.pallas_call(
        body, name=name, grid=(M // tm, N // tn, nk),
        in_specs=in_specs, out_specs=out_spec, out_shape=out_shape, input_output_aliases=aliases,
        scratch_shapes=[] if nk == 1 else [pltpu.VMEM((tm, tn), F32)],
        compiler_params=_cparams(("parallel", "parallel", "arbitrary")),
    )(*args)


def group_mm(a, w, mode, out_dtype, name, b=None):
    T = a.shape[0]
    G, gc, _ = w.shape
    col = pl.BlockSpec((T, gc), lambda g: (0, g))
    wsp = pl.BlockSpec((1, gc, gc), lambda g: (g, 0, 0))
    if mode == 'tn':
        def body(a_ref, b_ref, o_ref):
            o_ref[0] = _dot(a_ref[...], b_ref[...], TN).astype(o_ref.dtype)
        return pl.pallas_call(body, name=name, grid=(G,), in_specs=[col, col], out_specs=wsp,
                              out_shape=jax.ShapeDtypeStruct((G, gc, gc), out_dtype),
                              compiler_params=_cparams(("parallel",)))(a, b)
    dims = NN if mode == 'nn' else NT

    def body(a_ref, w_ref, o_ref):
        o_ref[...] = _dot(a_ref[...], w_ref[0], dims).astype(o_ref.dtype)
    return pl.pallas_call(body, name=name, grid=(G,), in_specs=[col, wsp], out_specs=col,
                          out_shape=jax.ShapeDtypeStruct((T, G * gc), out_dtype),
                          compiler_params=_cparams(("parallel",)))(a, w)


def _rowspec(r):
    return r if isinstance(r, tuple) else (r, r.shape[1], 0)


def _row_specs(rows, segs, consts, tm, nctx):
    specs = [pl.BlockSpec((tm, w), lambda i, b=b: (i, b)) for _, w, b in rows]
    specs += [pl.BlockSpec((1,) + s.shape[1:], lambda i, n=s.ndim: (jnp.where(i >= nctx, 1, 0),) + (0,) * (n - 1))
              for s in segs]
    specs += [pl.BlockSpec(c.shape, lambda i, n=c.ndim: (0,) * n) for c in consts]
    return specs


def rowwise(fn, rows, segs, consts, outs, dm, name, tm=None):
    tm = tm or dm.tm
    nctx = dm.CTX // tm
    rows = [_rowspec(r) for r in rows]
    nr, ns, nc = len(rows), len(segs), len(consts)

    def body(*refs):
        rin = [r[...] for r in refs[:nr]]
        sin = [s[0] for s in refs[nr:nr + ns]]
        cin = [c[...] for c in refs[nr + ns:nr + ns + nc]]
        res = fn(*rin, *sin, *cin)
        for o_ref, v in zip(refs[nr + ns + nc:], res):
            o_ref[...] = v.astype(o_ref.dtype)

    res = pl.pallas_call(
        body, name=name, grid=(dm.T // tm,),
        in_specs=_row_specs(rows, segs, consts, tm, nctx),
        out_specs=[pl.BlockSpec((tm, w), lambda i: (i, 0)) for w, _ in outs],
        out_shape=[jax.ShapeDtypeStruct((dm.T, w), dt) for w, dt in outs],
        compiler_params=_cparams(("parallel",)),
    )(*[r[0] for r in rows], *segs, *consts)
    return res


def rowwise_vjp(fn, rows, segs, consts, cots, dm, name, tm=None, want=None, adds=None):
    tm = tm or dm.tm
    nctx = dm.CTX // tm
    rows = [_rowspec(r) for r in rows]
    cots = [_rowspec(r) for r in cots]
    adds = adds or {}
    nr, ns, nc, nct = len(rows), len(segs), len(consts), len(cots)
    want = want or [True] * nr
    widx = [k for k in range(nr) if want[k]]
    akeys = sorted(adds)

    def body(*refs):
        i = pl.program_id(0)
        rin = [r[...] for r in refs[:nr]]
        sin = [s[0] for s in refs[nr:nr + ns]]
        cin = [c[...] for c in refs[nr + ns:nr + ns + nc]]
        p = nr + ns + nc
        cot_refs = refs[p:p + nct]
        add_refs = dict(zip(akeys, refs[p + nct:p + nct + len(akeys)]))
        p = p + nct + len(akeys)
        rg_refs = refs[p:p + len(widx)]
        sg_refs = refs[p + len(widx):p + len(widx) + ns]
        cg_refs = refs[p + len(widx) + ns:]
        res, vjp = jax.vjp(fn, *rin, *sin, *cin)
        g = vjp(tuple(cr[...].astype(o.dtype) for cr, o in zip(cot_refs, res)))
        for o_ref, k in zip(rg_refs, widx):
            v = g[k].astype(F32)
            if k in add_refs:
                v = v + add_refs[k][...]
            o_ref[...] = v.astype(o_ref.dtype)
        first_seg = jnp.logical_or(i == 0, i == nctx)
        for o_ref, v in zip(sg_refs, g[nr:nr + ns]):
            @pl.when(first_seg)
            def _(o_ref=o_ref, v=v):
                o_ref[0] = v.astype(F32)

            @pl.when(jnp.logical_not(first_seg))
            def _(o_ref=o_ref, v=v):
                o_ref[0] += v.astype(F32)
        for o_ref, v in zip(cg_refs, g[nr + ns:]):
            @pl.when(i == 0)
            def _(o_ref=o_ref, v=v):
                o_ref[...] = v.astype(F32)

            @pl.when(i > 0)
            def _(o_ref=o_ref, v=v):
                o_ref[...] += v.astype(F32)

    in_specs = _row_specs(rows, segs, consts, tm, nctx)
    in_specs += [pl.BlockSpec((tm, w), lambda i, b=b: (i, b)) for _, w, b in cots]
    in_specs += [pl.BlockSpec((tm, adds[k].shape[1]), lambda i: (i, 0)) for k in akeys]
    out_specs = [pl.BlockSpec((tm, rows[k][1]), lambda i: (i, 0)) for k in widx]
    out_shape = [jax.ShapeDtypeStruct((dm.T, rows[k][1]), rows[k][0].dtype) for k in widx]
    out_specs += [pl.BlockSpec((1,) + s.shape[1:], lambda i, n=s.ndim: (jnp.where(i >= nctx, 1, 0),) + (0,) * (n - 1))
                  for s in segs]
    out_shape += [jax.ShapeDtypeStruct(s.shape, F32) for s in segs]
    out_specs += [pl.BlockSpec(c.shape, lambda i, n=c.ndim: (0,) * n) for c in consts]
    out_shape += [jax.ShapeDtypeStruct(c.shape, F32) for c in consts]
    res = pl.pallas_call(
        body, name=name, grid=(dm.T // tm,), in_specs=in_specs, out_specs=out_specs, out_shape=out_shape,
        compiler_params=_cparams(("arbitrary",)),
    )(*[r[0] for r in rows], *segs, *consts, *[r[0] for r in cots], *[adds[k] for k in akeys])
    rg = [None] * nr
    for k, v in zip(widx, res[:len(widx)]):
        rg[k] = v
    return rg, list(res[len(widx):len(widx) + ns]), list(res[len(widx) + ns:])


def _rms(x, g):
    return x * lax.rsqrt(jnp.mean(x * x, axis=-1, keepdims=True) + EPS) * g


def _sigmoid(x):
    return jax.nn.sigmoid(x)


def pre_fn(x, shift, scale, g):
    return ((_rms(x, g) * (1.0 + scale) + shift).astype(MM_DTYPE),)


def mid_fn(x, y, gate, shift, scale, g_post, g_pre):
    x1 = x + gate * _rms(y.astype(F32), g_post)
    return x1, (_rms(x1, g_pre) * (1.0 + scale) + shift).astype(MM_DTYPE)


def post_fn(x1, y2, gate, g):
    return (x1 + gate * _rms(y2.astype(F32), g),)


def relu2_fn(u):
    r = jnp.maximum(u.astype(F32), 0.0)
    return ((r * r).astype(MM_DTYPE),)


def decay_fn(z, bd):
    zz = z.astype(F32) + bd
    ls = jnp.minimum(zz, 0.0) - jnp.log(1.0 + jnp.exp(jnp.minimum(zz, -zz)))
    la = ls / GLA_TAU
    gk = la.shape[1] // 2
    return la[:, :gk], la[:, gk:]


def glu_fn(a, b):
    return (a.astype(F32) * _sigmoid(b.astype(F32)),)


def glaout_fn(o_f, o_b, og, g):
    o = o_f + o_b
    dv = o.shape[1] // N_HEADS
    hs = []
    for h in range(N_HEADS):
        oh = o[:, h * dv:(h + 1) * dv]
        hs.append(oh * lax.rsqrt(jnp.mean(oh * oh, axis=-1, keepdims=True) + EPS) * g[:, h * dv:(h + 1) * dv])
    og = og.astype(F32)
    return ((jnp.concatenate(hs, axis=1) * (og * _sigmoid(og))).astype(MM_DTYPE),)


def convpost_fn(y, b_dw, g, b):
    y = y + b_dw
    mu = jnp.mean(y, axis=-1, keepdims=True)
    xc = y - mu
    yn = xc * lax.rsqrt(jnp.mean(xc * xc, axis=-1, keepdims=True) + EPS) * g + b
    return ((yn * _sigmoid(yn)).astype(MM_DTYPE),)


def poolpost_fn(pc, s):
    return ((pc.astype(F32) * s).astype(MM_DTYPE),)


def merge_fn(ya, yb, yc, mg, bg0, bg1, bg2):
    d = ya.shape[1]
    mg = mg.astype(F32)
    mixed = (_sigmoid(mg[:, :d] + bg0) * ya.astype(F32) + _sigmoid(mg[:, d:2 * d] + bg1) * yb.astype(F32)
             + _sigmoid(mg[:, 2 * d:] + bg2) * yc.astype(F32))
    return (mixed.astype(MM_DTYPE),)


def _split_dot(lmat, x, dims):
    hi = x.astype(MM_DTYPE)
    lo = x - hi.astype(F32)
    return _dot(lmat, hi, dims) + _dot(lmat, lo, dims)


def _gla_block_order(dm, rev):
    nctx, nb = dm.CTX // dm.TB, dm.T // dm.TB

    def blk(i):
        if not rev:
            return i
        return jnp.where(i < nctx, nctx - 1 - i, nb - 1 - (i - nctx))
    return blk, nb


def _gla_tri(rev):
    c = GLA_CHUNK
    t = lax.broadcasted_iota(jnp.int32, (c, c), 0)
    s = lax.broadcasted_iota(jnp.int32, (c, c), 1)
    return (s >= t) if rev else (s <= t)


def _gla_chunk_terms(q, k, la, tri, scale):
    lmat = tri.astype(MM_DTYPE)
    b = _split_dot(lmat, la, NN)
    bend = jnp.sum(la, axis=0, keepdims=True)
    eb = jnp.exp(b)
    enb = jnp.exp(-b)
    ee = jnp.exp(bend - b)
    qi = q * scale * eb
    ki = k * enb
    kend = k * ee
    att = jnp.where(tri, _dot(qi, ki, NT), 0.0)
    return lmat, bend, eb, enb, ee, qi, ki, kend, att


def gla_fwd(P, la, rev, dm, name):
    c, tb, h_, dk, dv, d = GLA_CHUNK, dm.TB, N_HEADS, dm.DK, dm.DV, dm.D
    cpb = tb // c
    blk, nb = _gla_block_order(dm, rev)
    qb, kb, vb, lb = (5 * d) // dk, (5 * d + d // 2) // dk, (3 * d) // dv, (h_ if rev else 0)
    scale = dk ** -0.5
    order = list(range(cpb))[::-1] if rev else list(range(cpb))

    def body(q_ref, k_ref, v_ref, la_ref, o_ref, s_ref, st):
        @pl.when(pl.program_id(1) == 0)
        def _():
            st[...] = jnp.zeros_like(st)
        tri = _gla_tri(rev)
        for n, ci in enumerate(order):
            r = pl.ds(ci * c, c)
            q = q_ref[r, :].astype(F32)
            k = k_ref[r, :].astype(F32)
            v = v_ref[r, :]
            _, bend, _, _, _, qi, _, kend, att = _gla_chunk_terms(q, k, la_ref[r, :], tri, scale)
            s_in = st[...]
            o_ref[r, :] = _dot(att, v) + _dot(qi, s_in, NT)
            s_ref[n, 0] = s_in
            st[...] = jnp.exp(bend) * s_in + _dot(v, kend, TN)

    return pl.pallas_call(
        body, name=name, grid=(h_, nb),
        in_specs=[pl.BlockSpec((tb, dk), lambda h, i: (blk(i), qb + h)),
                  pl.BlockSpec((tb, dk), lambda h, i: (blk(i), kb + h)),
                  pl.BlockSpec((tb, dv), lambda h, i: (blk(i), vb + h)),
                  pl.BlockSpec((tb, dk), lambda h, i: (blk(i), lb + h))],
        out_specs=[pl.BlockSpec((tb, dv), lambda h, i: (blk(i), h)),
                   pl.BlockSpec((cpb, 1, dv, dk), lambda h, i: (i, h, 0, 0))],
        out_shape=[jax.ShapeDtypeStruct((dm.T, h_ * dv), F32),
                   jax.ShapeDtypeStruct((dm.T // c, h_, dv, dk), F32)],
        scratch_shapes=[pltpu.VMEM((dv, dk), F32)],
        compiler_params=_cparams(("parallel", "arbitrary")),
    )(P, P, P, la)


def gla_bwd(P, la, do, states, rev, dm, name):
    c, tb, h_, dk, dv, d = GLA_CHUNK, dm.TB, N_HEADS, dm.DK, dm.DV, dm.D
    cpb = tb // c
    blk, nb = _gla_block_order(dm, rev)
    qb, kb, vb, lb = (5 * d) // dk, (5 * d + d // 2) // dk, (3 * d) // dv, (h_ if rev else 0)
    scale = dk ** -0.5
    order = list(range(cpb))[::-1] if rev else list(range(cpb))

    def body(q_ref, k_ref, v_ref, la_ref, do_ref, s_ref, dq_ref, dk_ref, dv_ref, dla_ref, dst):
        @pl.when(pl.program_id(1) == 0)
        def _():
            dst[...] = jnp.zeros_like(dst)
        tri = _gla_tri(rev)
        for n in range(cpb - 1, -1, -1):
            r = pl.ds(order[n] * c, c)
            q = q_ref[r, :].astype(F32)
            k = k_ref[r, :].astype(F32)
            v = v_ref[r, :]
            lmat, bend, eb, enb, ee, qi, ki, kend, att = _gla_chunk_terms(q, k, la_ref[r, :], tri, scale)
            s_in = s_ref[n, 0]
            ds_out = dst[...]
            dob = do_ref[r, :]
            datt = jnp.where(tri, _dot(dob, v, NT), 0.0)
            dqi = _dot(datt, ki) + _dot(dob, s_in)
            dki = _dot(datt, qi, TN)
            dv_ref[r, :] = (_dot(att, dob, TN) + _dot(kend, ds_out, NT)).astype(dv_ref.dtype)
            dkend = _dot(v, ds_out)
            gam = jnp.exp(bend)
            dgam = jnp.sum(ds_out * s_in, axis=0, keepdims=True)
            dst[...] = gam * ds_out + _dot(dob, qi, TN)
            dq_ref[r, :] = (dqi * (scale * eb)).astype(dq_ref.dtype)
            dk_ref[r, :] = (dki * enb + dkend * ee).astype(dk_ref.dtype)
            db = dqi * qi - dki * ki - dkend * kend
            dbend = jnp.sum(dkend * kend, axis=0, keepdims=True) + dgam * gam
            dla_ref[r, :] = _split_dot(lmat, db, TN) + dbend

    def bi(j):
        return blk(nb - 1 - j)

    return pl.pallas_call(
        body, name=name, grid=(h_, nb),
        in_specs=[pl.BlockSpec((tb, dk), lambda h, j: (bi(j), qb + h)),
                  pl.BlockSpec((tb, dk), lambda h, j: (bi(j), kb + h)),
                  pl.BlockSpec((tb, dv), lambda h, j: (bi(j), vb + h)),
                  pl.BlockSpec((tb, dk), lambda h, j: (bi(j), lb + h)),
                  pl.BlockSpec((tb, dv), lambda h, j: (bi(j), h)),
                  pl.BlockSpec((cpb, 1, dv, dk), lambda h, j: (nb - 1 - j, h, 0, 0))],
        out_specs=[pl.BlockSpec((tb, dk), lambda h, j: (bi(j), h)),
                   pl.BlockSpec((tb, dk), lambda h, j: (bi(j), h)),
                   pl.BlockSpec((tb, dv), lambda h, j: (bi(j), h)),
                   pl.BlockSpec((tb, dk), lambda h, j: (bi(j), h))],
        out_shape=[jax.ShapeDtypeStruct((dm.T, h_ * dk), F32), jax.ShapeDtypeStruct((dm.T, h_ * dk), F32),
                   jax.ShapeDtypeStruct((dm.T, h_ * dv), F32), jax.ShapeDtypeStruct((dm.T, h_ * dk), F32)],
        scratch_shapes=[pltpu.VMEM((dv, dk), F32)],
        compiler_params=_cparams(("parallel", "arbitrary")),
    )(P, P, P, la, do, states)


def _pos(n, period):
    t = lax.broadcasted_iota(jnp.int32, (n, 1), 0)
    if period & (period - 1) == 0:
        return jnp.bitwise_and(t, period - 1)
    return lax.rem(t, period)


def _conv_segments(dm):
    return [(0, dm.CTX, dm.CTX), (dm.CTX, dm.SEQ, GRID_W)]


def conv_fwd(u, w, dm, name):
    kw, cw = w.shape
    segs = _conv_segments(dm)

    def body(u_ref, w_ref, y_ref):
        for r0, n, per in segs:
            useg = u_ref[r0:r0 + n, :]
            p = _pos(n, per)
            acc = jnp.zeros_like(useg)
            for kk in range(kw):
                d = kk - kw // 2
                sh = useg if d == 0 else pltpu.roll(useg, (-d) % n, 0)
                ok = jnp.logical_and(p + d >= 0, p + d < per)
                acc = acc + jnp.where(ok, sh, 0.0) * w_ref[kk:kk + 1, :]
            y_ref[r0:r0 + n, :] = acc

    return pl.pallas_call(
        body, name=name, grid=(cw // LANES,),
        in_specs=[pl.BlockSpec((dm.T, LANES), lambda j: (0, j)), pl.BlockSpec((kw, LANES), lambda j: (0, j))],
        out_specs=pl.BlockSpec((dm.T, LANES), lambda j: (0, j)),
        out_shape=jax.ShapeDtypeStruct((dm.T, cw), F32),
        compiler_params=_cparams(("parallel",)),
    )(u, w)


def conv_bwd(u, w, dy, dm, name):
    kw, cw = w.shape
    segs = _conv_segments(dm)

    def body(u_ref, w_ref, dy_ref, du_ref, dw_ref):
        dws = [jnp.zeros((1, LANES), F32)] * kw
        for r0, n, per in segs:
            useg = u_ref[r0:r0 + n, :]
            dyseg = dy_ref[r0:r0 + n, :]
            p = _pos(n, per)
            acc = jnp.zeros_like(useg)
            for kk in range(kw):
                d = kk - kw // 2
                shu = useg if d == 0 else pltpu.roll(useg, (-d) % n, 0)
                okf = jnp.logical_and(p + d >= 0, p + d < per)
                dws[kk] = dws[kk] + jnp.sum(jnp.where(okf, shu, 0.0) * dyseg, axis=0, keepdims=True)
                shd = dyseg if d == 0 else pltpu.roll(dyseg, d % n, 0)
                okb = jnp.logical_and(p - d >= 0, p - d < per)
                acc = acc + jnp.where(okb, shd, 0.0) * w_ref[kk:kk + 1, :]
            du_ref[r0:r0 + n, :] = acc
        for kk in range(kw):
            dw_ref[kk:kk + 1, :] = dws[kk]

    return pl.pallas_call(
        body, name=name, grid=(cw // LANES,),
        in_specs=[pl.BlockSpec((dm.T, LANES), lambda j: (0, j)), pl.BlockSpec((kw, LANES), lambda j: (0, j)),
                  pl.BlockSpec((dm.T, LANES), lambda j: (0, j))],
        out_specs=[pl.BlockSpec((dm.T, LANES), lambda j: (0, j)), pl.BlockSpec((kw, LANES), lambda j: (0, j))],
        out_shape=[jax.ShapeDtypeStruct((dm.T, cw), F32), jax.ShapeDtypeStruct((kw, cw), F32)],
        compiler_params=_cparams(("parallel",)),
    )(u, w, dy)


def pool_mix(u, transpose, dm, name):
    u, uw, ublk = _rowspec(u)
    gc = dm.GC
    ng = len(POOL_WINDOWS)
    rows = dm.SEQ // GRID_W
    segs = [(0, dm.CTX, 1, dm.CTX), (dm.CTX, dm.SEQ, GRID_W, rows)]

    def one_group(u_ref, o_ref, win):
        left = win // 2
        right = win - 1 - left
        for r0, n, stride, length in segs:
            useg = u_ref[r0:r0 + n, :].astype(F32)
            t = lax.broadcasted_iota(jnp.int32, (n, 1), 0)
            p = t if stride == 1 else jnp.right_shift(t, stride.bit_length() - 1)
            cnt = (jnp.minimum(p + right + 1, length) - jnp.maximum(p - left, 0)).astype(F32)
            src = useg / cnt if transpose else useg
            acc = jnp.zeros_like(useg)
            for d in range(-left, right + 1):
                dd = -d if transpose else d
                sh = src if d == 0 else pltpu.roll(src, (-dd * stride) % n, 0)
                ok = jnp.logical_and(p + dd >= 0, p + dd < length)
                acc = acc + jnp.where(ok, sh, 0.0)
            o_ref[r0:r0 + n, :] = (acc - useg) if transpose else (acc / cnt - useg)

    def body(u_ref, o_ref):
        g = pl.program_id(0)
        for gi, win in enumerate(POOL_WINDOWS):
            @pl.when(g == gi)
            def _(win=win):
                one_group(u_ref, o_ref, win)

    base = ublk * (uw // gc)
    return pl.pallas_call(
        body, name=name, grid=(ng,),
        in_specs=[pl.BlockSpec((dm.T, gc), lambda g: (0, base + g))],
        out_specs=pl.BlockSpec((dm.T, gc), lambda g: (0, g)),
        out_shape=jax.ShapeDtypeStruct((dm.T, ng * gc), F32),
        compiler_params=_cparams(("parallel",)),
    )(u)


def loss_head(x2, target, dm, name):
    tm, d = dm.tm, dm.D
    nctx = dm.CTX // tm

    def body(x_ref, t_ref, dx_ref, l_ref):
        i = pl.program_id(0)

        @pl.when(i == 0)
        def _():
            l_ref[...] = jnp.zeros_like(l_ref)

        @pl.when(i < nctx)
        def _():
            dx_ref[...] = jnp.zeros_like(dx_ref)

        @pl.when(i >= nctx)
        def _():
            e = x_ref[...] - t_ref[...]
            dx_ref[...] = e / d
            l_ref[...] += jnp.full(l_ref.shape, 0.5 * jnp.sum(jnp.mean(e * e, axis=-1)), F32)

    return pl.pallas_call(
        body, name=name, grid=(dm.T // tm,),
        in_specs=[pl.BlockSpec((tm, d), lambda i: (i, 0)),
                  pl.BlockSpec((tm, d), lambda i: (jnp.maximum(i - nctx, 0), 0))],
        out_specs=[pl.BlockSpec((tm, d), lambda i: (i, 0)), pl.BlockSpec((8, LANES), lambda i: (0, 0))],
        out_shape=[jax.ShapeDtypeStruct((dm.T, d), F32), jax.ShapeDtypeStruct((8, LANES), F32)],
        compiler_params=_cparams(("arbitrary",)),
    )(x2, target)


def adamw(w, g, m, v, name):
    r, c = w.shape
    tr = _tile(r, tuple(t for t in (512, 256, 128, 64, 32, 16, 8) if t * c * 4 <= (1 << 20)) or (8,))

    def body(w_ref, g_ref, m_ref, v_ref, d_ref, mo_ref, vo_ref):
        gg = g_ref[...]
        mm = ADAM_B1 * m_ref[...] + (1.0 - ADAM_B1) * gg
        vv = ADAM_B2 * v_ref[...] + (1.0 - ADAM_B2) * (gg * gg)
        m_hat = mm / (1.0 - ADAM_B1 ** ADAM_STEP)
        v_hat = vv / (1.0 - ADAM_B2 ** ADAM_STEP)
        d_ref[...] = -ADAM_LR * (m_hat / (jnp.sqrt(v_hat) + ADAM_EPS) + ADAM_WD * w_ref[...])
        mo_ref[...] = mm
        vo_ref[...] = vv

    spec = pl.BlockSpec((tr, c), lambda i: (i, 0))
    return pl.pallas_call(
        body, name=name, grid=(r // tr,), in_specs=[spec] * 4, out_specs=[spec] * 3,
        out_shape=[jax.ShapeDtypeStruct((r, c), F32)] * 3,
        compiler_params=_cparams(("parallel",)),
    )(w, g, m, v)


def slot_sum(buf, name):
    s, r, c = buf.shape
    tr = _tile(r, (256, 128, 64, 32, 16, 8))

    def body(b_ref, o_ref):
        acc = b_ref[0].astype(F32)
        for k in range(1, s):
            acc = acc + b_ref[k].astype(F32)
        o_ref[...] = acc

    return pl.pallas_call(
        body, name=name, grid=(r // tr,),
        in_specs=[pl.BlockSpec((s, tr, c), lambda i: (0, i, 0))],
        out_specs=pl.BlockSpec((tr, c), lambda i: (i, 0)),
        out_shape=jax.ShapeDtypeStruct((r, c), F32),
        compiler_params=_cparams(("parallel",)),
    )(buf)


def pair_add(g, r1, cidx, name):
    _, k_, n_ = g.shape
    tr = _tile(k_, tuple(t for t in (1024, 512, 256, 128, 64, 32, 16) if t * n_ * 4 <= (2 << 20)))

    def body(s_ref, g_ref, r_ref, o_ref):
        o_ref[...] = (g_ref[...].astype(F32) + r_ref[...].astype(F32)).astype(o_ref.dtype)

    return pl.pallas_call(
        body, name=name,
        grid_spec=pltpu.PrefetchScalarGridSpec(
            num_scalar_prefetch=1, grid=(k_ // tr,),
            in_specs=[pl.BlockSpec((None, tr, n_), lambda i, s: (s[0], i, 0)),
                      pl.BlockSpec((tr, n_), lambda i, s: (i, 0))],
            out_specs=pl.BlockSpec((tr, n_), lambda i, s: (i, 0))),
        out_shape=jax.ShapeDtypeStruct((k_, n_), g.dtype),
        compiler_params=_cparams(("parallel",)),
    )(cidx, g, r1)


def chip_add(h, r2, axis, where, name):
    _, kl, nl = r2.shape
    tr = _tile(kl, tuple(t for t in (1024, 512, 256, 128, 64, 32, 16) if t * nl * 4 <= (1 << 20)))
    nrb = kl // tr

    def body(s_ref, h_ref, r_ref, o_ref):
        acc = h_ref[...].astype(F32)
        for k in range(r2.shape[0]):
            acc = acc + r_ref[k].astype(F32)
        o_ref[...] = acc

    h_map = (lambda i, s: (s[0] * nrb + i, 0)) if axis == 0 else (lambda i, s: (i, s[0]))
    return pl.pallas_call(
        body, name=name,
        grid_spec=pltpu.PrefetchScalarGridSpec(
            num_scalar_prefetch=1, grid=(nrb,),
            in_specs=[pl.BlockSpec((tr, nl), h_map),
                      pl.BlockSpec((r2.shape[0], tr, nl), lambda i, s: (0, i, 0))],
            out_specs=pl.BlockSpec((None, tr, nl), lambda i, s: (s[1], i, 0))),
        out_shape=jax.ShapeDtypeStruct((2, kl, nl), F32),
        compiler_params=_cparams(("parallel",)),
    )(where, h, r2)


MESH = pl.DeviceIdType.MESH
ANY = pl.BlockSpec(memory_space=pl.ANY)


def _place():
    return lax.axis_index("x"), lax.axis_index("y"), lax.axis_index("c")


def _peers(x, y):
    return [(1 - x, y), (x, 1 - y), (1 - x, 1 - y)]


def _rcopy(src, dst, ssem, rsem, dev):
    return pltpu.make_async_remote_copy(src_ref=src, dst_ref=dst, send_sem=ssem, recv_sem=rsem,
                                        device_id=dev, device_id_type=MESH)


def _window(ref, lead, axis, ch, width):
    nd = len(ref.shape) - len(lead)
    idx = tuple(lead) + tuple(pl.ds(ch * width, width) if k == axis else slice(None) for k in range(nd))
    return ref.at[idx]


def all_gather_weights(shards, axes, name):
    nw = len(shards)
    widths = [s.shape[ax] for s, ax in zip(shards, axes)]
    out_shape = [jax.ShapeDtypeStruct(s.shape[:ax] + (4 * s.shape[ax],) + s.shape[ax + 1:], s.dtype)
                 for s, ax in zip(shards, axes)]

    def body(*refs):
        src, out = refs[:nw], refs[nw:2 * nw]
        ssem, rsem, osend, orecv = refs[2 * nw:]
        x, y, c = _place()
        chip = 2 * x + y
        sib = (x, y, 1 - c)
        peers = _peers(x, y)
        pidx = [2 * px + py for px, py in peers]

        def win(n, layer, ch):
            return _window(out[n], (layer,), axes[n] - 1, ch, widths[n])

        def own(n):
            return _window(out[n], (), axes[n], chip, widths[n])

        mine = [_rcopy(src[n], own(n), osend.at[n], orecv.at[n], sib) for n in range(nw)]
        first = [[_rcopy(src[n].at[c], win(n, c, chip), ssem.at[6 * n + k], rsem.at[6 * n + k], (px, py, c))
                  for k, (px, py) in enumerate(peers)] for n in range(nw)]
        for n in range(nw):
            for cp in first[n]:
                cp.start()
        for cp in mine:
            cp.start()
        passed = [[_rcopy(win(n, c, pidx[k]), win(n, c, pidx[k]), ssem.at[6 * n + 3 + k], rsem.at[6 * n + 3 + k], sib)
                   for k in range(3)] for n in range(nw)]
        for n in range(nw):
            for k, (px, py) in enumerate(peers):
                _rcopy(win(n, c, pidx[k]), win(n, c, pidx[k]), ssem.at[6 * n + k], rsem.at[6 * n + k],
                       (px, py, c)).wait_recv()
                passed[n][k].start()
        for n in range(nw):
            for k in range(3):
                _rcopy(win(n, 1 - c, pidx[k]), win(n, 1 - c, pidx[k]), ssem.at[6 * n + 3 + k],
                       rsem.at[6 * n + 3 + k], sib).wait_recv()
        for n in range(nw):
            for cp in first[n] + passed[n]:
                cp.wait_send()
        for cp in mine:
            cp.wait()

    return pl.pallas_call(
        body, name=name, in_specs=[ANY] * nw, out_specs=[ANY] * nw, out_shape=out_shape,
        scratch_shapes=[pltpu.SemaphoreType.DMA((6 * nw,)), pltpu.SemaphoreType.DMA((6 * nw,)),
                        pltpu.SemaphoreType.DMA((nw,)), pltpu.SemaphoreType.DMA((nw,))],
    )(*shards)


def pair_swap_layers(gs, name):
    nw = len(gs)

    def body(*refs):
        g, o = refs[:nw], refs[nw:2 * nw]
        ssem, rsem = refs[2 * nw:]
        x, y, c = _place()
        cps = [_rcopy(g[n].at[1 - c], o[n], ssem.at[n], rsem.at[n], (x, y, 1 - c)) for n in range(nw)]
        for cp in cps:
            cp.start()
        for cp in cps:
            cp.wait()

    return pl.pallas_call(
        body, name=name, in_specs=[ANY] * nw, out_specs=[ANY] * nw,
        out_shape=[jax.ShapeDtypeStruct(g.shape[1:], g.dtype) for g in gs],
        scratch_shapes=[pltpu.SemaphoreType.DMA((nw,)), pltpu.SemaphoreType.DMA((nw,))],
    )(*gs)


def chip_exchange(hs, axes, name):
    nw = len(hs)
    shp = [tuple(d // 4 if k == ax else d for k, d in enumerate(h.shape)) for h, ax in zip(hs, axes)]

    def body(*refs):
        h, o = refs[:nw], refs[nw:2 * nw]
        ssem, rsem = refs[2 * nw:]
        x, y, c = _place()
        cps = [_rcopy(_window(h[n], (), axes[n], 2 * px + py, shp[n][axes[n]]), o[n].at[k], ssem.at[3 * n + k],
                      rsem.at[3 * n + k], (px, py, c))
               for n in range(nw) for k, (px, py) in enumerate(_peers(x, y))]
        for cp in cps:
            cp.start()
        for cp in cps:
            cp.wait()

    return pl.pallas_call(
        body, name=name, in_specs=[ANY] * nw, out_specs=[ANY] * nw,
        out_shape=[jax.ShapeDtypeStruct((3,) + sh, h.dtype) for sh, h in zip(shp, hs)],
        scratch_shapes=[pltpu.SemaphoreType.DMA((3 * nw,)), pltpu.SemaphoreType.DMA((3 * nw,))],
    )(*hs)


def pair_join_layers(fs, name):
    nw = len(fs)

    def body(*refs):
        o = refs[nw:2 * nw]
        ssem, rsem = refs[2 * nw:]
        x, y, c = _place()
        sib = (x, y, 1 - c)
        cps = [_rcopy(o[n].at[c], o[n].at[c], ssem.at[n], rsem.at[n], sib) for n in range(nw)]
        for cp in cps:
            cp.start()
        for n in range(nw):
            cps[n].wait_send()
            _rcopy(o[n].at[1 - c], o[n].at[1 - c], ssem.at[n], rsem.at[n], sib).wait_recv()

    return pl.pallas_call(
        body, name=name, in_specs=[ANY] * nw, out_specs=[ANY] * nw,
        out_shape=[jax.ShapeDtypeStruct(f.shape, f.dtype) for f in fs],
        input_output_aliases={n: n for n in range(nw)},
        scratch_shapes=[pltpu.SemaphoreType.DMA((nw,)), pltpu.SemaphoreType.DMA((nw,))],
    )(*fs)


def gather_all_devices(buf, name):
    r, c_ = buf.shape
    offs = [o for o in itertools.product((0, 1), repeat=3) if o != (0, 0, 0)]

    def body(b_ref, o_ref, ssem, rsem, lsem):
        x, y, c = _place()
        me = 4 * x + 2 * y + c
        mine = pltpu.make_async_copy(b_ref, o_ref.at[me], lsem)
        mine.start()
        peers = [((x + dx) % 2, (y + dy) % 2, (c + dc) % 2) for dx, dy, dc in offs]
        cps = [_rcopy(b_ref, o_ref.at[me], ssem.at[k], rsem.at[k], p) for k, p in enumerate(peers)]
        for cp in cps:
            cp.start()
        for k, (px, py, pc) in enumerate(peers):
            _rcopy(b_ref, o_ref.at[4 * px + 2 * py + pc], ssem.at[k], rsem.at[k], (px, py, pc)).wait_recv()
        for cp in cps:
            cp.wait_send()
        mine.wait()

    return pl.pallas_call(
        body, name=name, in_specs=[ANY], out_specs=ANY,
        out_shape=jax.ShapeDtypeStruct((8, r, c_), buf.dtype),
        scratch_shapes=[pltpu.SemaphoreType.DMA((7,)), pltpu.SemaphoreType.DMA((7,)), pltpu.SemaphoreType.DMA],
    )(buf)


def _flatten_pad(parts, dtype):
    flat = jnp.concatenate([p.reshape(-1).astype(dtype) for p in parts])
    q = 512 * LANES
    n = -(-flat.shape[0] // q) * q
    return jnp.pad(flat, (0, n - flat.shape[0])).reshape(n // LANES, LANES)


def _lane_pad(n):
    return -(-n // LANES) * LANES


def _in_proj_layout(d):
    gk, gv, cw, pw = d // 2, d, d // 2, d // 2
    own = [('q', gk), ('k', gk), ('v', gv), ('og', gv), ('lrf', GLA_LR), ('lrb', GLA_LR), ('ga', cw), ('gb', cw),
           ('pu', pw), ('mg', 3 * d)]
    padded = [('mg', 3 * d), ('v', gv), ('og', gv), ('q', gk), ('k', gk), ('ga', cw), ('gb', cw), ('pu', pw),
              ('lrf', GLA_LR), ('lrb', GLA_LR), ('pad', d // 2 - 2 * GLA_LR)]
    return own, padded


def _pad_w_in(w, d):
    own, padded = _in_proj_layout(d)
    cols, start = {}, 0
    for n, wd in own:
        cols[n] = w[:, start:start + wd]
        start += wd
    return jnp.concatenate([cols[n] if n != 'pad' else jnp.zeros((w.shape[0], wd), w.dtype) for n, wd in padded], axis=1)


def _unpad_w_in(wp, d):
    own, padded = _in_proj_layout(d)
    cols, start = {}, 0
    for n, wd in padded:
        cols[n] = wp[:, start:start + wd]
        start += wd
    return jnp.concatenate([cols[n] for n, _ in own], axis=1)


def _silu_grad(z):
    s = jax.nn.sigmoid(z)
    return s + z * s * (1.0 - s)


def kernel(x, c, ctx, c_ctx, w_ada, b_ada, g_pre_mix, g_post_mix, g_pre_mlp, g_post_mlp, w_in, w_decay, b_decay, g_gla, w_gla_o, w_dw, b_dw, g_conv_ln, b_conv_ln, w_conv_o, w_pool_g, s_pool, w_pool_o, b_gate, w_out, w_mlp1, w_mlp2, loss_target, m_c_ctx, m_w_ada, m_b_ada, m_g_pre_mix, m_g_post_mix, m_g_pre_mlp, m_g_post_mlp, m_w_in, m_w_decay, m_b_decay, m_g_gla, m_w_gla_o, m_w_dw, m_b_dw, m_g_conv_ln, m_b_conv_ln, m_w_conv_o, m_w_pool_g, m_s_pool, m_w_pool_o, m_b_gate, m_w_out, m_w_mlp1, m_w_mlp2, v_c_ctx, v_w_ada, v_b_ada, v_g_pre_mix, v_g_post_mix, v_g_pre_mlp, v_g_post_mlp, v_w_in, v_w_decay, v_b_decay, v_g_gla, v_w_gla_o, v_w_dw, v_b_dw, v_g_conv_ln, v_b_conv_ln, v_w_conv_o, v_w_pool_g, v_s_pool, v_w_pool_o, v_b_gate, v_w_out, v_w_mlp1, v_w_mlp2):
    a = dict(locals())
    depth = w_in.shape[0]
    d = x.shape[-1]
    seq, nctx_rows = x.shape[1], ctx.shape[1]
    dm = types.SimpleNamespace(
        D=d, SEQ=seq, CTX=nctx_rows, T=seq + nctx_rows, DK=d // 8, DV=d // 4, GK=d // 2, GC=d // 8,
        tm=_tile(nctx_rows, (256, 128, 64)), TB=_tile(nctx_rows, (256, 128, 64)))
    assert dm.SEQ % dm.tm == 0 and dm.SEQ % GRID_W == 0 and dm.CTX % GLA_CHUNK == 0
    tmw = min(dm.tm, 128)
    chip = 2 * lax.axis_index("x") + lax.axis_index("y")
    core = lax.axis_index("c")
    chip1 = chip.astype(jnp.int32).reshape(1)
    core1 = core.astype(jnp.int32).reshape(1)

    big_names, small_names = list(BIG), list(SMALL_SHARDED)
    wl = w_in.shape[2]
    wlp = _lane_pad(wl)
    def rows8(t):
        t = t.reshape(t.shape[0], -1, t.shape[-1])
        return jnp.pad(t, ((0, 0), (0, -t.shape[1] % 8), (0, 0)))
    shards = [(jnp.pad(a[n], ((0, 0), (0, 0), (0, wlp - wl))) if n == 'w_in' else a[n]).astype(MM_DTYPE)
              for n in big_names] + [rows8(a[n]) for n in small_names]
    gathered = all_gather_weights(shards, [BIG[n] for n in big_names] + [2] * len(small_names), "all_gather_weights")
    full = dict(zip(big_names, gathered))
    for n, g in zip(small_names, gathered[len(big_names):]):
        shp = a[n].shape
        full[n] = g[:, :math.prod(shp[1:-1])].reshape(shp[:-1] + (4 * shp[-1],))
    for n in SMALL:
        if n not in SMALL_SHARDED:
            full[n] = a[n]

    cvec = jnp.concatenate([c_ctx.reshape(1, d), c.reshape(1, d), jnp.zeros((6, d), F32)], axis=0)
    avec = (cvec * jax.nn.sigmoid(cvec)).astype(MM_DTYPE)

    def row(v):
        return v.reshape(1, -1)

    X = jnp.concatenate([ctx[0], x[0]], axis=0)
    saved = []
    gk, gv = dm.GK, d
    lrblk = (7 * d + d // 2) // LANES
    for l in range(depth):
        s = types.SimpleNamespace()
        s.w_in_p = _pad_w_in(full['w_in'][l].reshape(d, 4, wlp)[:, :, :wl].reshape(d, 4 * wl), d)
        wd = full['w_decay'][l]
        wdp = jnp.zeros((LANES, 2 * gk), F32)
        wdp = wdp.at[:GLA_LR, :gk].set(wd[0]).at[GLA_LR:2 * GLA_LR, gk:].set(wd[1])
        s.wdp = wdp.astype(MM_DTYPE)
        s.bd = full['b_decay'][l].reshape(1, 2 * gk)
        modraw = matmul(avec, (full['w_ada'], l), 'nn', F32, f"mod_{l}") + full['b_ada'][l][None, :]
        s.mod = [modraw[0:2, j * d:(j + 1) * d].reshape(2, 1, d) for j in range(6)]
        s.x = X
        (s.h,) = rowwise(pre_fn, [X], s.mod[0:2], [row(g_pre_mix[l])], [(d, MM_DTYPE)], dm, f"pre_{l}")
        s.P = matmul(s.h, s.w_in_p, 'nn', MM_DTYPE, f"in_proj_{l}")
        P = s.P
        s.z = matmul((P, LANES, lrblk), s.wdp, 'nn', F32, f"decay_proj_{l}", tk=LANES)
        la_f, la_b = rowwise(decay_fn, [s.z], [], [s.bd], [(gk, F32), (gk, F32)], dm, f"decay_{l}")
        s.la = jnp.concatenate([la_f, la_b], axis=1)
        s.o_f, s.st_f = gla_fwd(P, s.la, False, dm, f"gla_fwd_f_{l}")
        s.o_b, s.st_b = gla_fwd(P, s.la, True, dm, f"gla_fwd_b_{l}")
        (s.gin,) = rowwise(glaout_fn, [s.o_f, s.o_b, (P, d, 4)], [], [row(g_gla[l])], [(gv, MM_DTYPE)], dm,
                           f"gla_out_{l}")
        s.ya = matmul(s.gin, (full['w_gla_o'], l), 'nn', F32, f"gla_o_{l}")
        (s.u,) = rowwise(glu_fn, [(P, d // 2, 12), (P, d // 2, 13)], [], [], [(d // 2, F32)], dm, f"glu_{l}")
        s.yconv = conv_fwd(s.u, full['w_dw'][l], dm, f"conv_{l}")
        (s.cin,) = rowwise(convpost_fn, [s.yconv], [], [row(b_dw[l]), row(g_conv_ln[l]), row(b_conv_ln[l])],
                           [(d // 2, MM_DTYPE)], dm, f"conv_post_{l}")
        s.yb = matmul(s.cin, (full['w_conv_o'], l), 'nn', F32, f"conv_o_{l}")
        s.pm = pool_mix((P, d // 2, 14), False, dm, f"pool_mix_{l}")
        s.pc = group_mm(s.pm, w_pool_g[l], 'nn', F32, f"pool_g_{l}")
        (s.pin,) = rowwise(poolpost_fn, [s.pc], [], [row(s_pool[l])], [(d // 2, MM_DTYPE)], dm, f"pool_post_{l}")
        s.yc = matmul(s.pin, (full['w_pool_o'], l), 'nn', F32, f"pool_o_{l}")
        s.bg = [row(full['b_gate'][l][j]) for j in range(3)]
        (s.mixed,) = rowwise(merge_fn, [s.ya, s.yb, s.yc, (P, 3 * d, 0)], [], s.bg, [(d, MM_DTYPE)], dm,
                             f"merge_{l}", tm=tmw)
        s.y = matmul(s.mixed, (full['w_out'], l), 'nn', F32, f"out_proj_{l}")
        s.x1, s.h2 = rowwise(mid_fn, [X, s.y], s.mod[2:5], [row(g_post_mix[l]), row(g_pre_mlp[l])],
                             [(d, F32), (d, MM_DTYPE)], dm, f"mid_{l}")
        s.u1 = matmul(s.h2, (full['w_mlp1'], l), 'nn', F32, f"mlp1_{l}")
        (s.act,) = rowwise(relu2_fn, [s.u1], [], [], [(4 * d, MM_DTYPE)], dm, f"relu2_{l}", tm=tmw)
        s.y2 = matmul(s.act, (full['w_mlp2'], l), 'nn', F32, f"mlp2_{l}")
        (X,) = rowwise(post_fn, [s.x1, s.y2], s.mod[5:6], [row(g_post_mlp[l])], [(d, F32)], dm, f"post_{l}")
        saved.append(s)

    dX, lossv = loss_head(X, loss_target[0], dm, "loss_head")
    loss = lax.psum(lossv[0, 0], ("x", "y", "c"))

    grads = {n: [None] * depth for n in WEIGHTS if n != 'c_ctx' and n not in BIG}
    gbig = {n: None for n in BIG}
    win_l = [None] * depth
    g_cctx = jnp.zeros((d,), F32)
    for l in reversed(range(depth)):
        s = saved[l]
        P = s.P
        dmod = [None] * 6
        (dx1, dy2), (dmod[5],), (dg,) = rowwise_vjp(post_fn, [s.x1, s.y2], s.mod[5:6], [row(g_post_mlp[l])], [dX],
                                                     dm, f"post_bwd_{l}")
        grads['g_post_mlp'][l] = dg[0]
        dact = matmul(dy2, (full['w_mlp2'], l), 'nt', MM_DTYPE, f"mlp2_dx_{l}")
        gbig['w_mlp2'] = matmul(s.act, dy2, 'tn', MM_DTYPE, f"mlp2_dw_{l}", stack=(gbig['w_mlp2'], l, depth))
        (du1,), _, _ = rowwise_vjp(relu2_fn, [s.u1], [], [], [dact], dm, f"relu2_bwd_{l}", tm=tmw)
        dh2 = matmul(du1, (full['w_mlp1'], l), 'nt', MM_DTYPE, f"mlp1_dx_{l}")
        gbig['w_mlp1'] = matmul(s.h2, du1, 'tn', MM_DTYPE, f"mlp1_dw_{l}", stack=(gbig['w_mlp1'], l, depth))
        (dxa, dy), dmod[2:5], (dg1, dg2) = rowwise_vjp(
            mid_fn, [s.x, s.y], s.mod[2:5], [row(g_post_mix[l]), row(g_pre_mlp[l])], [dx1, dh2], dm, f"mid_bwd_{l}")
        grads['g_post_mix'][l], grads['g_pre_mlp'][l] = dg1[0], dg2[0]
        dmixed = matmul(dy, (full['w_out'], l), 'nt', MM_DTYPE, f"out_proj_dx_{l}")
        gbig['w_out'] = matmul(s.mixed, dy, 'tn', MM_DTYPE, f"out_proj_dw_{l}", stack=(gbig['w_out'], l, depth))
        (dya, dyb, dyc, dmg), _, dbg = rowwise_vjp(merge_fn, [s.ya, s.yb, s.yc, (P, 3 * d, 0)], [], s.bg, [dmixed],
                                                   dm, f"merge_bwd_{l}", tm=tmw)
        grads['b_gate'][l] = jnp.concatenate(dbg, axis=0)
        dgin = matmul(dya, (full['w_gla_o'], l), 'nt', MM_DTYPE, f"gla_o_dx_{l}")
        gbig['w_gla_o'] = matmul(s.gin, dya, 'tn', MM_DTYPE, f"gla_o_dw_{l}", stack=(gbig['w_gla_o'], l, depth))
        dcin = matmul(dyb, (full['w_conv_o'], l), 'nt', MM_DTYPE, f"conv_o_dx_{l}")
        gbig['w_conv_o'] = matmul(s.cin, dyb, 'tn', MM_DTYPE, f"conv_o_dw_{l}", stack=(gbig['w_conv_o'], l, depth))
        dpin = matmul(dyc, (full['w_pool_o'], l), 'nt', MM_DTYPE, f"pool_o_dx_{l}")
        gbig['w_pool_o'] = matmul(s.pin, dyc, 'tn', MM_DTYPE, f"pool_o_dw_{l}", stack=(gbig['w_pool_o'], l, depth))
        (dpc,), _, (dsp,) = rowwise_vjp(poolpost_fn, [s.pc], [], [row(s_pool[l])], [dpin], dm, f"pool_post_bwd_{l}")
        grads['s_pool'][l] = dsp[0]
        grads['w_pool_g'][l] = group_mm(s.pm, w_pool_g[l], 'tn', F32, f"pool_g_dw_{l}", b=dpc)
        dpm = group_mm(dpc, w_pool_g[l], 'nt', F32, f"pool_g_dx_{l}")
        dpu = pool_mix(dpm, True, dm, f"pool_mix_bwd_{l}")
        (dyconv,), _, (dbdw, dgln, dbln) = rowwise_vjp(
            convpost_fn, [s.yconv], [], [row(b_dw[l]), row(g_conv_ln[l]), row(b_conv_ln[l])], [dcin], dm,
            f"conv_post_bwd_{l}")
        grads['b_dw'][l], grads['g_conv_ln'][l], grads['b_conv_ln'][l] = dbdw[0], dgln[0], dbln[0]
        du, grads['w_dw'][l] = conv_bwd(s.u, full['w_dw'][l], dyconv, dm, f"conv_bwd_{l}")
        (dga, dgb), _, _ = rowwise_vjp(glu_fn, [(P, d // 2, 12), (P, d // 2, 13)], [], [], [du], dm, f"glu_bwd_{l}")
        (do, _, dog), _, (dgg,) = rowwise_vjp(glaout_fn, [s.o_f, s.o_b, (P, d, 4)], [], [row(g_gla[l])], [dgin], dm,
                                              f"gla_out_bwd_{l}", want=[True, False, True])
        grads['g_gla'][l] = dgg[0]
        dqf, dkf, dvf, dlaf = gla_bwd(P, s.la, do, s.st_f, False, dm, f"gla_bwd_f_{l}")
        dqb, dkb, dvb, dlab = gla_bwd(P, s.la, do, s.st_b, True, dm, f"gla_bwd_b_{l}")
        (dz,), _, (dbd,) = rowwise_vjp(decay_fn, [s.z], [], [s.bd], [dlaf, dlab], dm, f"decay_bwd_{l}")
        grads['b_decay'][l] = dbd.reshape(2, gk)
        dwdp = matmul((P, LANES, lrblk), dz, 'tn', F32, f"decay_proj_dw_{l}", tm=LANES)
        grads['w_decay'][l] = jnp.stack([dwdp[:GLA_LR, :gk], dwdp[GLA_LR:2 * GLA_LR, gk:]])
        dlr = matmul(dz, s.wdp, 'nt', F32, f"decay_proj_dx_{l}")

        def asm_fn(dmg_, dvf_, dvb_, dog_, dqf_, dqb_, dkf_, dkb_, dga_, dgb_, dpu_, dlr_):
            f = lambda t: t.astype(F32)
            pad = jnp.zeros((dlr_.shape[0], d // 2 - LANES), F32)
            return (jnp.concatenate([f(dmg_), dvf_ + dvb_, f(dog_), dqf_ + dqb_, dkf_ + dkb_, f(dga_), f(dgb_),
                                     dpu_, dlr_, pad], axis=1).astype(MM_DTYPE),)
        (dP,) = rowwise(asm_fn, [dmg, dvf, dvb, dog, dqf, dqb, dkf, dkb, dga, dgb, dpu, dlr], [], [],
                        [(8 * d, MM_DTYPE)], dm, f"dproj_{l}", tm=tmw)
        dh = matmul(dP, s.w_in_p, 'nt', MM_DTYPE, f"in_proj_dx_{l}")
        gwin = _unpad_w_in(matmul(s.h, dP, 'tn', MM_DTYPE, f"in_proj_dw_{l}"), d)
        win_l[l] = jnp.pad(gwin.reshape(d, 4, wl), ((0, 0), (0, 0), (0, wlp - wl))).reshape(d, 4 * wlp)
        (dX,), dmod[0:2], (dg,) = rowwise_vjp(pre_fn, [s.x], s.mod[0:2], [row(g_pre_mix[l])], [dh], dm,
                                               f"pre_bwd_{l}", adds={0: dxa})
        grads['g_pre_mix'][l] = dg[0]
        dmodflat = jnp.concatenate([jnp.concatenate([m_.reshape(2, d) for m_ in dmod], axis=1),
                                    jnp.zeros((6, 6 * d), F32)], axis=0)
        grads['b_ada'][l] = dmodflat[0] + dmodflat[1]
        gbig['w_ada'] = matmul(avec, dmodflat, 'tn', MM_DTYPE, f"ada_dw_{l}", stack=(gbig['w_ada'], l, depth))
        dav = matmul(dmodflat, (full['w_ada'], l), 'nt', F32, f"ada_dx_{l}")
        g_cctx = g_cctx + dav[0] * _silu_grad(c_ctx)

    grad_x = dX[dm.CTX:][None]
    gbig['w_in'] = jnp.stack(win_l)
    gfull = {n: jnp.stack(v) for n, v in grads.items()}
    gfull['c_ctx'] = g_cctx

    ax2 = [BIG[n] - 1 for n in big_names]
    r1 = pair_swap_layers([gbig[n] for n in big_names], "grad_pair_swap")
    hs = [pair_add(gbig[n], r, core1, f"grad_pair_add_{n}") for n, r in zip(big_names, r1)]
    r2 = chip_exchange(hs, ax2, "grad_chip_exchange")
    where = jnp.concatenate([chip1, core1])
    fs = [chip_add(h, r, ax, where, f"grad_chip_add_{n}") for n, h, r, ax in zip(big_names, hs, r2, ax2)]
    gred = dict(zip(big_names, pair_join_layers(fs, "grad_pair_join")))
    gred['w_in'] = gred['w_in'][:, :, :wl]

    sflat = _flatten_pad([gfull[n].astype(F32) for n in SMALL], F32)
    ssum = slot_sum(gather_all_devices(sflat, "small_grad_gather"), "small_grad_sum").reshape(-1)

    out_g, out_d, out_m, out_v = {}, {}, {}, {}
    for n in big_names:
        w = a[n]
        two = (w.shape[0] * w.shape[1], w.shape[2])
        dl, mn, vn = adamw(w.reshape(two), gred[n].reshape(two), a['m_' + n].reshape(two), a['v_' + n].reshape(two),
                           f"adamw_{n}")
        out_g[n], out_d[n], out_m[n], out_v[n] = gred[n], dl.reshape(w.shape), mn.reshape(w.shape), vn.reshape(w.shape)
    start = 0
    sg = {}
    for n in SMALL:
        cnt = gfull[n].size
        g = ssum[start:start + cnt].reshape(gfull[n].shape)
        start += cnt
        if n in SMALL_SHARDED:
            ax = SMALL_SHARDED[n]
            wdt = a[n].shape[ax]
            g = lax.dynamic_slice_in_dim(g, chip * wdt, wdt, axis=ax)
        sg[n] = g
    pk = lambda dct, pre: _flatten_pad([dct[pre + n] for n in SMALL], F32)
    gs = _flatten_pad([sg[n] for n in SMALL], F32)
    dl, mn, vn = adamw(pk(a, ''), gs, pk(a, 'm_'), pk(a, 'v_'), "adamw_small")
    dl, mn, vn = dl.reshape(-1), mn.reshape(-1), vn.reshape(-1)
    start = 0
    for n in SMALL:
        cnt, shp = a[n].size, a[n].shape
        out_g[n] = sg[n]
        out_d[n], out_m[n], out_v[n] = (t[start:start + cnt].reshape(shp) for t in (dl, mn, vn))
        start += cnt

    return (loss, grad_x, *[out_g[n] for n in WEIGHTS], *[out_d[n] for n in WEIGHTS],
            *[out_m[n] for n in WEIGHTS], *[out_v[n] for n in WEIGHTS])
```

```python
import functools
import itertools
import math
import types

import jax
import jax.numpy as jnp
from jax import lax
from jax.experimental import pallas as pl
from jax.experimental.pallas import tpu as pltpu

F32 = jnp.float32
MM_DTYPE = jnp.bfloat16
VMEM_LIMIT_V7X = 56 * 1024 * 1024
LANES = 128
EPS = 1e-6

N_HEADS = 4
GLA_CHUNK = 64
GLA_TAU = 16.0
GLA_LR = 16
GRID_W = 64
POOL_WINDOWS = (2, 4, 8, 16)

ADAM_LR = 0.001
ADAM_B1 = 0.9
ADAM_B2 = 0.999
ADAM_EPS = 1e-08
ADAM_WD = 0.01
ADAM_STEP = 10

NN = (((1,), (0,)), ((), ()))
NT = (((1,), (1,)), ((), ()))
TN = (((0,), (0,)), ((), ()))

WEIGHTS = ['c_ctx', 'w_ada', 'b_ada', 'g_pre_mix', 'g_post_mix', 'g_pre_mlp', 'g_post_mlp', 'w_in', 'w_decay',
           'b_decay', 'g_gla', 'w_gla_o', 'w_dw', 'b_dw', 'g_conv_ln', 'b_conv_ln', 'w_conv_o', 'w_pool_g',
           's_pool', 'w_pool_o', 'b_gate', 'w_out', 'w_mlp1', 'w_mlp2']
BIG = {'w_ada': 2, 'w_in': 2, 'w_gla_o': 1, 'w_conv_o': 2, 'w_pool_o': 2, 'w_out': 1, 'w_mlp1': 2, 'w_mlp2': 1}
SMALL_SHARDED = {'w_decay': 3, 'b_decay': 2, 'w_dw': 2, 'b_gate': 2}
SMALL = [n for n in WEIGHTS if n not in BIG]


def _tile(n, prefs):
    for t in prefs:
        if n % t == 0:
            return t
    return n


def _cparams(sem=None, **kw):
    return pltpu.CompilerParams(dimension_semantics=sem, vmem_limit_bytes=VMEM_LIMIT_V7X, **kw)


def _dot(a, b, dims=NN):
    return lax.dot_general(a.astype(MM_DTYPE), b.astype(MM_DTYPE), dims, preferred_element_type=F32)


def matmul(a, b, mode, out_dtype, name, tm=None, tn=None, tk=None, stack=None):
    a, aw, ablk = a if isinstance(a, tuple) else (a, a.shape[1], 0)
    b, bl = b if isinstance(b, tuple) else (b, None)
    bs = b.shape[-2:]
    if mode == 'nn':
        M, K, N = a.shape[0], aw, bs[1]
    elif mode == 'nt':
        M, K, N = a.shape[0], aw, bs[0]
    else:
        K, M, N = a.shape[0], aw, bs[1]
    big = (1088, 1024, 640, 544, 512, 320, 256, 128, 64, 32, 16, 8)
    if mode == 'tn':
        tm = tm or _tile(M, (1024, 512, 256, 128))
        tn = tn or _tile(N, (1024, 512, 256, 128))
        tk = tk or _tile(K, big)
    else:
        tm = tm or _tile(M, big)
        tn = tn or _tile(N, (512, 256, 128))
        tk = tk or _tile(K, (1024, 512, 256, 128))
    if aw != a.shape[1]:
        assert (mode == 'tn' and tm == aw) or (mode != 'tn' and tk == aw)
    nk = K // tk
    dims = {'nn': NN, 'nt': NT, 'tn': TN}[mode]

    def body(a_ref, b_ref, *rest):
        o_ref = rest[1] if (stack and stack[0] is not None) else rest[0]
        p = _dot(a_ref[...], b_ref[...], dims)
        if nk == 1:
            o_ref[...] = p.astype(o_ref.dtype)
            return
        acc = rest[-1]
        k = pl.program_id(2)

        @pl.when(k == 0)
        def _():
            acc[...] = p

        @pl.when(k > 0)
        def _():
            acc[...] += p

        @pl.when(k == nk - 1)
        def _():
            o_ref[...] = acc[...].astype(o_ref.dtype)

    lead = () if bl is None else (None,)
    pre = (lambda *ix: ix) if bl is None else (lambda *ix: (bl,) + ix)
    if mode == 'nn':
        a_spec = pl.BlockSpec((tm, tk), lambda i, j, k: (i, k + ablk))
        b_spec = pl.BlockSpec(lead + (tk, tn), lambda i, j, k: pre(k, j))
    elif mode == 'nt':
        a_spec = pl.BlockSpec((tm, tk), lambda i, j, k: (i, k + ablk))
        b_spec = pl.BlockSpec(lead + (tn, tk), lambda i, j, k: pre(j, k))
    else:
        a_spec = pl.BlockSpec((tk, tm), lambda i, j, k: (k, i + ablk))
        b_spec = pl.BlockSpec(lead + (tk, tn), lambda i, j, k: pre(k, j))
    in_specs, args, aliases = [a_spec, b_spec], [a, b], {}
    if stack is None:
        out_spec = pl.BlockSpec((tm, tn), lambda i, j, k: (i, j))
        out_shape = jax.ShapeDtypeStruct((M, N), out_dtype)
    else:
        buf, sl, depth = stack
        out_spec = pl.BlockSpec((None, tm, tn), lambda i, j, k: (sl, i, j))
        out_shape = jax.ShapeDtypeStruct((depth, M, N), out_dtype)
        if buf is not None:
            in_specs.append(pl.BlockSpec(memory_space=pl.ANY))
            args.append(buf)
            aliases = {2: 0}
    return pl.pallas_call(
        body, name=name, grid=(M // tm, N // tn, nk),
        in_specs=in_specs, out_specs=out_spec, out_shape=out_shape, input_output_aliases=aliases,
        scratch_shapes=[] if nk == 1 else [pltpu.VMEM((tm, tn), F32)],
        compiler_params=_cparams(("parallel", "parallel", "arbitrary")),
    )(*args)


def group_mm(a, w, mode, out_dtype, name, b=None):
    T = a.shape[0]
    G, gc, _ = w.shape
    col = pl.BlockSpec((T, gc), lambda g: (0, g))
    wsp = pl.BlockSpec((1, gc, gc), lambda g: (g, 0, 0))
    if mode == 'tn':
        def body(a_ref, b_ref, o_ref):
            o_ref[0] = _dot(a_ref[...], b_ref[...], TN).astype(o_ref.dtype)
        return pl.pallas_call(body, name=name, grid=(G,), in_specs=[col, col], out_specs=wsp,
                              out_shape=jax.ShapeDtypeStruct((G, gc, gc), out_dtype),
                              compiler_params=_cparams(("parallel",)))(a, b)
    dims = NN if mode == 'nn' else NT

    def body(a_ref, w_ref, o_ref):
        o_ref[...] = _dot(a_ref[...], w_ref[0], dims).astype(o_ref.dtype)
    return pl.pallas_call(body, name=name, grid=(G,), in_specs=[col, wsp], out_specs=col,
                          out_shape=jax.ShapeDtypeStruct((T, G * gc), out_dtype),
                          compiler_params=_cparams(("parallel",)))(a, w)


def _rowspec(r):
    return r if isinstance(r, tuple) else (r, r.shape[1], 0)


def _row_specs(rows, segs, consts, tm, nctx):
    specs = [pl.BlockSpec((tm, w), lambda i, b=b: (i, b)) for _, w, b in rows]
    specs += [pl.BlockSpec((1,) + s.shape[1:], lambda i, n=s.ndim: (jnp.where(i >= nctx, 1, 0),) + (0,) * (n - 1))
              for s in segs]
    specs += [pl.BlockSpec(c.shape, lambda i, n=c.ndim: (0,) * n) for c in consts]
    return specs


def rowwise(fn, rows, segs, consts, outs, dm, name, tm=None):
    tm = tm or dm.tm
    nctx = dm.CTX // tm
    rows = [_rowspec(r) for r in rows]
    nr, ns, nc = len(rows), len(segs), len(consts)

    def body(*refs):
        rin = [r[...] for r in refs[:nr]]
        sin = [s[0] for s in refs[nr:nr + ns]]
        cin = [c[...] for c in refs[nr + ns:nr + ns + nc]]
        res = fn(*rin, *sin, *cin)
        for o_ref, v in zip(refs[nr + ns + nc:], res):
            o_ref[...] = v.astype(o_ref.dtype)

    res = pl.pallas_call(
        body, name=name, grid=(dm.T // tm,),
        in_specs=_row_specs(rows, segs, consts, tm, nctx),
        out_specs=[pl.BlockSpec((tm, w), lambda i: (i, 0)) for w, _ in outs],
        out_shape=[jax.ShapeDtypeStruct((dm.T, w), dt) for w, dt in outs],
        compiler_params=_cparams(("parallel",)),
    )(*[r[0] for r in rows], *segs, *consts)
    return res


def rowwise_vjp(fn, rows, segs, consts, cots, dm, name, tm=None, want=None, adds=None):
    tm = tm or dm.tm
    nctx = dm.CTX // tm
    rows = [_rowspec(r) for r in rows]
    cots = [_rowspec(r) for r in cots]
    adds = adds or {}
    nr, ns, nc, nct = len(rows), len(segs), len(consts), len(cots)
    want = want or [True] * nr
    widx = [k for k in range(nr) if want[k]]
    akeys = sorted(adds)

    def body(*refs):
        i = pl.program_id(0)
        rin = [r[...] for r in refs[:nr]]
        sin = [s[0] for s in refs[nr:nr + ns]]
        cin = [c[...] for c in refs[nr + ns:nr + ns + nc]]
        p = nr + ns + nc
        cot_refs = refs[p:p + nct]
        add_refs = dict(zip(akeys, refs[p + nct:p + nct + len(akeys)]))
        p = p + nct + len(akeys)
        rg_refs = refs[p:p + len(widx)]
        sg_refs = refs[p + len(widx):p + len(widx) + ns]
        cg_refs = refs[p + len(widx) + ns:]
        res, vjp = jax.vjp(fn, *rin, *sin, *cin)
        g = vjp(tuple(cr[...].astype(o.dtype) for cr, o in zip(cot_refs, res)))
        for o_ref, k in zip(rg_refs, widx):
            v = g[k].astype(F32)
            if k in add_refs:
                v = v + add_refs[k][...]
            o_ref[...] = v.astype(o_ref.dtype)
        first_seg = jnp.logical_or(i == 0, i == nctx)
        for o_ref, v in zip(sg_refs, g[nr:nr + ns]):
            @pl.when(first_seg)
            def _(o_ref=o_ref, v=v):
                o_ref[0] = v.astype(F32)

            @pl.when(jnp.logical_not(first_seg))
            def _(o_ref=o_ref, v=v):
                o_ref[0] += v.astype(F32)
        for o_ref, v in zip(cg_refs, g[nr + ns:]):
            @pl.when(i == 0)
            def _(o_ref=o_ref, v=v):
                o_ref[...] = v.astype(F32)

            @pl.when(i > 0)
            def _(o_ref=o_ref, v=v):
                o_ref[...] += v.astype(F32)

    in_specs = _row_specs(rows, segs, consts, tm, nctx)
    in_specs += [pl.BlockSpec((tm, w), lambda i, b=b: (i, b)) for _, w, b in cots]
    in_specs += [pl.BlockSpec((tm, adds[k].shape[1]), lambda i: (i, 0)) for k in akeys]
    out_specs = [pl.BlockSpec((tm, rows[k][1]), lambda i: (i, 0)) for k in widx]
    out_shape = [jax.ShapeDtypeStruct((dm.T, rows[k][1]), rows[k][0].dtype) for k in widx]
    out_specs += [pl.BlockSpec((1,) + s.shape[1:], lambda i, n=s.ndim: (jnp.where(i >= nctx, 1, 0),) + (0,) * (n - 1))
                  for s in segs]
    out_shape += [jax.ShapeDtypeStruct(s.shape, F32) for s in segs]
    out_specs += [pl.BlockSpec(c.shape, lambda i, n=c.ndim: (0,) * n) for c in consts]
    out_shape += [jax.ShapeDtypeStruct(c.shape, F32) for c in consts]
    res = pl.pallas_call(
        body, name=name, grid=(dm.T // tm,), in_specs=in_specs, out_specs=out_specs, out_shape=out_shape,
        compiler_params=_cparams(("arbitrary",)),
    )(*[r[0] for r in rows], *segs, *consts, *[r[0] for r in cots], *[adds[k] for k in akeys])
    rg = [None] * nr
    for k, v in zip(widx, res[:len(widx)]):
        rg[k] = v
    return rg, list(res[len(widx):len(widx) + ns]), list(res[len(widx) + ns:])


def _rms(x, g):
    return x * lax.rsqrt(jnp.mean(x * x, axis=-1, keepdims=True) + EPS) * g


def _sigmoid(x):
    return jax.nn.sigmoid(x)


def pre_fn(x, shift, scale, g):
    return ((_rms(x, g) * (1.0 + scale) + shift).astype(MM_DTYPE),)


def mid_fn(x, y, gate, shift, scale, g_post, g_pre):
    x1 = x + gate * _rms(y.astype(F32), g_post)
    return x1, (_rms(x1, g_pre) * (1.0 + scale) + shift).astype(MM_DTYPE)


def post_fn(x1, y2, gate, g):
    return (x1 + gate * _rms(y2.astype(F32), g),)


def relu2_fn(u):
    r = jnp.maximum(u.astype(F32), 0.0)
    return ((r * r).astype(MM_DTYPE),)


def decay_fn(z, bd):
    zz = z.astype(F32) + bd
    ls = jnp.minimum(zz, 0.0) - jnp.log(1.0 + jnp.exp(jnp.minimum(zz, -zz)))
    la = ls / GLA_TAU
    gk = la.shape[1] // 2
    return la[:, :gk], la[:, gk:]


def glu_fn(a, b):
    return (a.astype(F32) * _sigmoid(b.astype(F32)),)


def glaout_fn(o_f, o_b, og, g):
    o = o_f + o_b
    dv = o.shape[1] // N_HEADS
    hs = []
    for h in range(N_HEADS):
        oh = o[:, h * dv:(h + 1) * dv]
        hs.append(oh * lax.rsqrt(jnp.mean(oh * oh, axis=-1, keepdims=True) + EPS) * g[:, h * dv:(h + 1) * dv])
    og = og.astype(F32)
    return ((jnp.concatenate(hs, axis=1) * (og * _sigmoid(og))).astype(MM_DTYPE),)


def convpost_fn(y, b_dw, g, b):
    y = y + b_dw
    mu = jnp.mean(y, axis=-1, keepdims=True)
    xc = y - mu
    yn = xc * lax.rsqrt(jnp.mean(xc * xc, axis=-1, keepdims=True) + EPS) * g + b
    return ((yn * _sigmoid(yn)).astype(MM_DTYPE),)


def poolpost_fn(pc, s):
    return ((pc.astype(F32) * s).astype(MM_DTYPE),)


def merge_fn(ya, yb, yc, mg, bg0, bg1, bg2):
    d = ya.shape[1]
    mg = mg.astype(F32)
    mixed = (_sigmoid(mg[:, :d] + bg0) * ya.astype(F32) + _sigmoid(mg[:, d:2 * d] + bg1) * yb.astype(F32)
             + _sigmoid(mg[:, 2 * d:] + bg2) * yc.astype(F32))
    return (mixed.astype(MM_DTYPE),)


def _split_dot(lmat, x, dims):
    hi = x.astype(MM_DTYPE)
    lo = x - hi.astype(F32)
    return _dot(lmat, hi, dims) + _dot(lmat, lo, dims)


def _gla_block_order(dm, rev):
    nctx, nb = dm.CTX // dm.TB, dm.T // dm.TB

    def blk(i):
        if not rev:
            return i
        return jnp.where(i < nctx, nctx - 1 - i, nb - 1 - (i - nctx))
    return blk, nb


def _gla_tri(rev):
    c = GLA_CHUNK
    t = lax.broadcasted_iota(jnp.int32, (c, c), 0)
    s = lax.broadcasted_iota(jnp.int32, (c, c), 1)
    return (s >= t) if rev else (s <= t)


def _gla_chunk_terms(q, k, la, tri, scale):
    lmat = tri.astype(MM_DTYPE)
    b = _split_dot(lmat, la, NN)
    bend = jnp.sum(la, axis=0, keepdims=True)
    eb = jnp.exp(b)
    enb = jnp.exp(-b)
    ee = jnp.exp(bend - b)
    qi = q * scale * eb
    ki = k * enb
    kend = k * ee
    att = jnp.where(tri, _dot(qi, ki, NT), 0.0)
    return lmat, bend, eb, enb, ee, qi, ki, kend, att


def gla_fwd(P, la, rev, dm, name):
    c, tb, h_, dk, dv, d = GLA_CHUNK, dm.TB, N_HEADS, dm.DK, dm.DV, dm.D
    cpb = tb // c
    blk, nb = _gla_block_order(dm, rev)
    qb, kb, vb, lb = (5 * d) // dk, (5 * d + d // 2) // dk, (3 * d) // dv, (h_ if rev else 0)
    scale = dk ** -0.5
    order = list(range(cpb))[::-1] if rev else list(range(cpb))

    def body(q_ref, k_ref, v_ref, la_ref, o_ref, s_ref, st):
        @pl.when(pl.program_id(1) == 0)
        def _():
            st[...] = jnp.zeros_like(st)
        tri = _gla_tri(rev)
        for n, ci in enumerate(order):
            r = pl.ds(ci * c, c)
            q = q_ref[r, :].astype(F32)
            k = k_ref[r, :].astype(F32)
            v = v_ref[r, :]
            _, bend, _, _, _, qi, _, kend, att = _gla_chunk_terms(q, k, la_ref[r, :], tri, scale)
            s_in = st[...]
            o_ref[r, :] = _dot(att, v) + _dot(qi, s_in, NT)
            s_ref[n, 0] = s_in
            st[...] = jnp.exp(bend) * s_in + _dot(v, kend, TN)

    return pl.pallas_call(
        body, name=name, grid=(h_, nb),
        in_specs=[pl.BlockSpec((tb, dk), lambda h, i: (blk(i), qb + h)),
                  pl.BlockSpec((tb, dk), lambda h, i: (blk(i), kb + h)),
                  pl.BlockSpec((tb, dv), lambda h, i: (blk(i), vb + h)),
                  pl.BlockSpec((tb, dk), lambda h, i: (blk(i), lb + h))],
        out_specs=[pl.BlockSpec((tb, dv), lambda h, i: (blk(i), h)),
                   pl.BlockSpec((cpb, 1, dv, dk), lambda h, i: (i, h, 0, 0))],
        out_shape=[jax.ShapeDtypeStruct((dm.T, h_ * dv), F32),
                   jax.ShapeDtypeStruct((dm.T // c, h_, dv, dk), F32)],
        scratch_shapes=[pltpu.VMEM((dv, dk), F32)],
        compiler_params=_cparams(("parallel", "arbitrary")),
    )(P, P, P, la)


def gla_bwd(P, la, do, states, rev, dm, name):
    c, tb, h_, dk, dv, d = GLA_CHUNK, dm.TB, N_HEADS, dm.DK, dm.DV, dm.D
    cpb = tb // c
    blk, nb = _gla_block_order(dm, rev)
    qb, kb, vb, lb = (5 * d) // dk, (5 * d + d // 2) // dk, (3 * d) // dv, (h_ if rev else 0)
    scale = dk ** -0.5
    order = list(range(cpb))[::-1] if rev else list(range(cpb))

    def body(q_ref, k_ref, v_ref, la_ref, do_ref, s_ref, dq_ref, dk_ref, dv_ref, dla_ref, dst):
        @pl.when(pl.program_id(1) == 0)
        def _():
            dst[...] = jnp.zeros_like(dst)
        tri = _gla_tri(rev)
        for n in range(cpb - 1, -1, -1):
            r = pl.ds(order[n] * c, c)
            q = q_ref[r, :].astype(F32)
            k = k_ref[r, :].astype(F32)
            v = v_ref[r, :]
            lmat, bend, eb, enb, ee, qi, ki, kend, att = _gla_chunk_terms(q, k, la_ref[r, :], tri, scale)
            s_in = s_ref[n, 0]
            ds_out = dst[...]
            dob = do_ref[r, :]
            datt = jnp.where(tri, _dot(dob, v, NT), 0.0)
            dqi = _dot(datt, ki) + _dot(dob, s_in)
            dki = _dot(datt, qi, TN)
            dv_ref[r, :] = (_dot(att, dob, TN) + _dot(kend, ds_out, NT)).astype(dv_ref.dtype)
            dkend = _dot(v, ds_out)
            gam = jnp.exp(bend)
            dgam = jnp.sum(ds_out * s_in, axis=0, keepdims=True)
            dst[...] = gam * ds_out + _dot(dob, qi, TN)
            dq_ref[r, :] = (dqi * (scale * eb)).astype(dq_ref.dtype)
            dk_ref[r, :] = (dki * enb + dkend * ee).astype(dk_ref.dtype)
            db = dqi * qi - dki * ki - dkend * kend
            dbend = jnp.sum(dkend * kend, axis=0, keepdims=True) + dgam * gam
            dla_ref[r, :] = _split_dot(lmat, db, TN) + dbend

    def bi(j):
        return blk(nb - 1 - j)

    return pl.pallas_call(
        body, name=name, grid=(h_, nb),
        in_specs=[pl.BlockSpec((tb, dk), lambda h, j: (bi(j), qb + h)),
                  pl.BlockSpec((tb, dk), lambda h, j: (bi(j), kb + h)),
                  pl.BlockSpec((tb, dv), lambda h, j: (bi(j), vb + h)),
                  pl.BlockSpec((tb, dk), lambda h, j: (bi(j), lb + h)),
                  pl.BlockSpec((tb, dv), lambda h, j: (bi(j), h)),
                  pl.BlockSpec((cpb, 1, dv, dk), lambda h, j: (nb - 1 - j, h, 0, 0))],
        out_specs=[pl.BlockSpec((tb, dk), lambda h, j: (bi(j), h)),
                   pl.BlockSpec((tb, dk), lambda h, j: (bi(j), h)),
                   pl.BlockSpec((tb, dv), lambda h, j: (bi(j), h)),
                   pl.BlockSpec((tb, dk), lambda h, j: (bi(j), h))],
        out_shape=[jax.ShapeDtypeStruct((dm.T, h_ * dk), F32), jax.ShapeDtypeStruct((dm.T, h_ * dk), F32),
                   jax.ShapeDtypeStruct((dm.T, h_ * dv), F32), jax.ShapeDtypeStruct((dm.T, h_ * dk), F32)],
        scratch_shapes=[pltpu.VMEM((dv, dk), F32)],
        compiler_params=_cparams(("parallel", "arbitrary")),
    )(P, P, P, la, do, states)


def _pos(n, period):
    t = lax.broadcasted_iota(jnp.int32, (n, 1), 0)
    if period & (period - 1) == 0:
        return jnp.bitwise_and(t, period - 1)
    return lax.rem(t, period)


def _conv_segments(dm):
    return [(0, dm.CTX, dm.CTX), (dm.CTX, dm.SEQ, GRID_W)]


def conv_fwd(u, w, dm, name):
    kw, cw = w.shape
    segs = _conv_segments(dm)

    def body(u_ref, w_ref, y_ref):
        for r0, n, per in segs:
            useg = u_ref[r0:r0 + n, :]
            p = _pos(n, per)
            acc = jnp.zeros_like(useg)
            for kk in range(kw):
                d = kk - kw // 2
                sh = useg if d == 0 else pltpu.roll(useg, (-d) % n, 0)
                ok = jnp.logical_and(p + d >= 0, p + d < per)
                acc = acc + jnp.where(ok, sh, 0.0) * w_ref[kk:kk + 1, :]
            y_ref[r0:r0 + n, :] = acc

    return pl.pallas_call(
        body, name=name, grid=(cw // LANES,),
        in_specs=[pl.BlockSpec((dm.T, LANES), lambda j: (0, j)), pl.BlockSpec((kw, LANES), lambda j: (0, j))],
        out_specs=pl.BlockSpec((dm.T, LANES), lambda j: (0, j)),
        out_shape=jax.ShapeDtypeStruct((dm.T, cw), F32),
        compiler_params=_cparams(("parallel",)),
    )(u, w)


def conv_bwd(u, w, dy, dm, name):
    kw, cw = w.shape
    segs = _conv_segments(dm)

    def body(u_ref, w_ref, dy_ref, du_ref, dw_ref):
        dws = [jnp.zeros((1, LANES), F32)] * kw
        for r0, n, per in segs:
            useg = u_ref[r0:r0 + n, :]
            dyseg = dy_ref[r0:r0 + n, :]
            p = _pos(n, per)
            acc = jnp.zeros_like(useg)
            for kk in range(kw):
                d = kk - kw // 2
                shu = useg if d == 0 else pltpu.roll(useg, (-d) % n, 0)
                okf = jnp.logical_and(p + d >= 0, p + d < per)
                dws[kk] = dws[kk] + jnp.sum(jnp.where(okf, shu, 0.0) * dyseg, axis=0, keepdims=True)
                shd = dyseg if d == 0 else pltpu.roll(dyseg, d % n, 0)
                okb = jnp.logical_and(p - d >= 0, p - d < per)
                acc = acc + jnp.where(okb, shd, 0.0) * w_ref[kk:kk + 1, :]
            du_ref[r0:r0 + n, :] = acc
        for kk in range(kw):
            dw_ref[kk:kk + 1, :] = dws[kk]

    return pl.pallas_call(
        body, name=name, grid=(cw // LANES,),
        in_specs=[pl.BlockSpec((dm.T, LANES), lambda j: (0, j)), pl.BlockSpec((kw, LANES), lambda j: (0, j)),
                  pl.BlockSpec((dm.T, LANES), lambda j: (0, j))],
        out_specs=[pl.BlockSpec((dm.T, LANES), lambda j: (0, j)), pl.BlockSpec((kw, LANES), lambda j: (0, j))],
        out_shape=[jax.ShapeDtypeStruct((dm.T, cw), F32), jax.ShapeDtypeStruct((kw, cw), F32)],
        compiler_params=_cparams(("parallel",)),
    )(u, w, dy)


def pool_mix(u, transpose, dm, name):
    u, uw, ublk = _rowspec(u)
    gc = dm.GC
    ng = len(POOL_WINDOWS)
    rows = dm.SEQ // GRID_W
    segs = [(0, dm.CTX, 1, dm.CTX), (dm.CTX, dm.SEQ, GRID_W, rows)]

    def one_group(u_ref, o_ref, win):
        left = win // 2
        right = win - 1 - left
        for r0, n, stride, length in segs:
            useg = u_ref[r0:r0 + n, :].astype(F32)
            t = lax.broadcasted_iota(jnp.int32, (n, 1), 0)
            p = t if stride == 1 else jnp.right_shift(t, stride.bit_length() - 1)
            cnt = (jnp.minimum(p + right + 1, length) - jnp.maximum(p - left, 0)).astype(F32)
            src = useg / cnt if transpose else useg
            acc = jnp.zeros_like(useg)
            for d in range(-left, right + 1):
                dd = -d if transpose else d
                sh = src if d == 0 else pltpu.roll(src, (-dd * stride) % n, 0)
                ok = jnp.logical_and(p + dd >= 0, p + dd < length)
                acc = acc + jnp.where(ok, sh, 0.0)
            o_ref[r0:r0 + n, :] = (acc - useg) if transpose else (acc / cnt - useg)

    def body(u_ref, o_ref):
        g = pl.program_id(0)
        for gi, win in enumerate(POOL_WINDOWS):
            @pl.when(g == gi)
            def _(win=win):
                one_group(u_ref, o_ref, win)

    base = ublk * (uw // gc)
    return pl.pallas_call(
        body, name=name, grid=(ng,),
        in_specs=[pl.BlockSpec((dm.T, gc), lambda g: (0, base + g))],
        out_specs=pl.BlockSpec((dm.T, gc), lambda g: (0, g)),
        out_shape=jax.ShapeDtypeStruct((dm.T, ng * gc), F32),
        compiler_params=_cparams(("parallel",)),
    )(u)


def loss_head(x2, target, dm, name):
    tm, d = dm.tm, dm.D
    nctx = dm.CTX // tm

    def body(x_ref, t_ref, dx_ref, l_ref):
        i = pl.program_id(0)

        @pl.when(i == 0)
        def _():
            l_ref[...] = jnp.zeros_like(l_ref)

        @pl.when(i < nctx)
        def _():
            dx_ref[...] = jnp.zeros_like(dx_ref)

        @pl.when(i >= nctx)
        def _():
            e = x_ref[...] - t_ref[...]
            dx_ref[...] = e / d
            l_ref[...] += jnp.full(l_ref.shape, 0.5 * jnp.sum(jnp.mean(e * e, axis=-1)), F32)

    return pl.pallas_call(
        body, name=name, grid=(dm.T // tm,),
        in_specs=[pl.BlockSpec((tm, d), lambda i: (i, 0)),
                  pl.BlockSpec((tm, d), lambda i: (jnp.maximum(i - nctx, 0), 0))],
        out_specs=[pl.BlockSpec((tm, d), lambda i: (i, 0)), pl.BlockSpec((8, LANES), lambda i: (0, 0))],
        out_shape=[jax.ShapeDtypeStruct((dm.T, d), F32), jax.ShapeDtypeStruct((8, LANES), F32)],
        compiler_params=_cparams(("arbitrary",)),
    )(x2, target)


def adamw(w, g, m, v, name):
    r, c = w.shape
    tr = _tile(r, tuple(t for t in (512, 256, 128, 64, 32, 16, 8) if t * c * 4 <= (1 << 20)) or (8,))

    def body(w_ref, g_ref, m_ref, v_ref, d_ref, mo_ref, vo_ref):
        gg = g_ref[...]
        mm = ADAM_B1 * m_ref[...] + (1.0 - ADAM_B1) * gg
        vv = ADAM_B2 * v_ref[...] + (1.0 - ADAM_B2) * (gg * gg)
        m_hat = mm / (1.0 - ADAM_B1 ** ADAM_STEP)
        v_hat = vv / (1.0 - ADAM_B2 ** ADAM_STEP)
        d_ref[...] = -ADAM_LR * (m_hat / (jnp.sqrt(v_hat) + ADAM_EPS) + ADAM_WD * w_ref[...])
        mo_ref[...] = mm
        vo_ref[...] = vv

    spec = pl.BlockSpec((tr, c), lambda i: (i, 0))
    return pl.pallas_call(
        body, name=name, grid=(r // tr,), in_specs=[spec] * 4, out_specs=[spec] * 3,
        out_shape=[jax.ShapeDtypeStruct((r, c), F32)] * 3,
        compiler_params=_cparams(("parallel",)),
    )(w, g, m, v)


def slot_sum(buf, name):
    s, r, c = buf.shape
    tr = _tile(r, (256, 128, 64, 32, 16, 8))

    def body(b_ref, o_ref):
        acc = b_ref[0].astype(F32)
        for k in range(1, s):
            acc = acc + b_ref[k].astype(F32)
        o_ref[...] = acc

    return pl.pallas_call(
        body, name=name, grid=(r // tr,),
        in_specs=[pl.BlockSpec((s, tr, c), lambda i: (0, i, 0))],
        out_specs=pl.BlockSpec((tr, c), lambda i: (i, 0)),
        out_shape=jax.ShapeDtypeStruct((r, c), F32),
        compiler_params=_cparams(("parallel",)),
    )(buf)


def pair_add(g, r1, cidx, name):
    ng, r_, n_ = r1.shape
    tr = _tile(r_, tuple(t for t in (1024, 512, 256, 128, 64, 32, 16) if t * n_ * 4 <= (2 << 20)))

    def body(s_ref, g_ref, r_ref, o_ref):
        o_ref[...] = (g_ref[...].astype(F32) + r_ref[...].astype(F32)).astype(o_ref.dtype)

    return pl.pallas_call(
        body, name=name,
        grid_spec=pltpu.PrefetchScalarGridSpec(
            num_scalar_prefetch=1, grid=(ng, r_ // tr),
            in_specs=[pl.BlockSpec((None, tr, n_), lambda k, i, s: (2 * k + s[0], i, 0)),
                      pl.BlockSpec((None, tr, n_), lambda k, i, s: (k, i, 0))],
            out_specs=pl.BlockSpec((None, tr, n_), lambda k, i, s: (k, i, 0))),
        out_shape=jax.ShapeDtypeStruct((ng, r_, n_), g.dtype),
        compiler_params=_cparams(("parallel", "parallel")),
    )(cidx, g, r1)


def chip_add(h, r2, axis, where, name, slab=True):
    _, kl, nl = r2.shape
    tr = _tile(kl, tuple(t for t in (1024, 512, 256, 128, 64, 32, 16) if t * nl * 4 <= (1 << 20)))
    nrb = kl // tr

    def body(s_ref, h_ref, r_ref, o_ref):
        acc = h_ref[...].astype(F32)
        for k in range(r2.shape[0]):
            acc = acc + r_ref[k].astype(F32)
        o_ref[...] = acc

    h_map = (lambda i, s: (s[0] * nrb + i, 0)) if axis == 0 else (lambda i, s: (i, s[0]))
    if slab:
        out_spec = pl.BlockSpec((None, tr, nl), lambda i, s: (s[1], i, 0))
        out_shape = jax.ShapeDtypeStruct((2, kl, nl), F32)
    else:
        out_spec = pl.BlockSpec((tr, nl), lambda i, s: (i, 0))
        out_shape = jax.ShapeDtypeStruct((kl, nl), F32)
    return pl.pallas_call(
        body, name=name,
        grid_spec=pltpu.PrefetchScalarGridSpec(
            num_scalar_prefetch=1, grid=(nrb,),
            in_specs=[pl.BlockSpec((tr, nl), h_map),
                      pl.BlockSpec((r2.shape[0], tr, nl), lambda i, s: (0, i, 0))],
            out_specs=out_spec),
        out_shape=out_shape,
        compiler_params=_cparams(("parallel",)),
    )(where, h, r2)


def adamw_layers(w, m, v, g0, s1a, s1b, name):
    _, a_, b_ = w.shape
    tr = _tile(a_, tuple(t for t in (512, 256, 128, 64, 32, 16, 8) if t * b_ * 4 <= (1 << 20)) or (8,))
    nrb = a_ // tr

    def update(g, w_ref, m_ref, v_ref, g_ref, d_ref, mo_ref, vo_ref):
        mm = ADAM_B1 * m_ref[...] + (1.0 - ADAM_B1) * g
        vv = ADAM_B2 * v_ref[...] + (1.0 - ADAM_B2) * (g * g)
        m_hat = mm / (1.0 - ADAM_B1 ** ADAM_STEP)
        v_hat = vv / (1.0 - ADAM_B2 ** ADAM_STEP)
        g_ref[...] = g
        d_ref[...] = -ADAM_LR * (m_hat / (jnp.sqrt(v_hat) + ADAM_EPS) + ADAM_WD * w_ref[...])
        mo_ref[...] = mm
        vo_ref[...] = vv

    def body(w_ref, m_ref, v_ref, g0_ref, sa_ref, sb_ref, *outs):
        layer = pl.program_id(0)

        @pl.when(layer == 0)
        def _():
            update(g0_ref[...], w_ref, m_ref, v_ref, *outs)

        @pl.when(layer == 1)
        def _():
            update(sa_ref[...] + sb_ref[...], w_ref, m_ref, v_ref, *outs)

    stacked = pl.BlockSpec((None, tr, b_), lambda l, i: (l, i, 0))
    return pl.pallas_call(
        body, name=name, grid=(2, nrb),
        in_specs=[stacked] * 3 + [pl.BlockSpec((tr, b_), lambda l, i: (i * (1 - l), 0))]
        + [pl.BlockSpec((tr, b_), lambda l, i: (i * l, 0))] * 2,
        out_specs=[stacked] * 4, out_shape=[jax.ShapeDtypeStruct(w.shape, F32)] * 4,
        compiler_params=_cparams(("arbitrary", "arbitrary")),
    )(w, m, v, g0, s1a, s1b)


MESH = pl.DeviceIdType.MESH
ANY = pl.BlockSpec(memory_space=pl.ANY)
HBM = pl.BlockSpec(memory_space=pltpu.HBM)
SEM = pl.BlockSpec(memory_space=pltpu.SEMAPHORE)
EFFECT = pltpu.SideEffectType.DATAFLOW_SIDE_EFFECTING


def _place():
    return lax.axis_index("x"), lax.axis_index("y"), lax.axis_index("c")


def _peers(x, y):
    return [(1 - x, y), (x, 1 - y), (1 - x, 1 - y)]


def _rcopy(src, dst, ssem, rsem, dev):
    return pltpu.make_async_remote_copy(src_ref=src, dst_ref=dst, send_sem=ssem, recv_sem=rsem,
                                        device_id=dev, device_id_type=MESH)


def _gathered_shape(src, kind):
    h, a_, b_ = src.shape
    return (h, a_, 4 * b_) if kind == 'col' else (4, h, a_, b_)


def _win(ref, kind, ch, width):
    return ref.at[:, :, pl.ds(ch * width, width)] if kind == 'col' else ref.at[ch]


def _rect(ref, kind, half, ch, width):
    return ref.at[half, :, pl.ds(ch * width, width)] if kind == 'col' else ref.at[ch, half]


def gather_halves(srcs, kinds, name):
    nw = len(srcs)
    widths = [s.shape[2] for s in srcs]

    def body(*refs):
        src, out = refs[:nw], refs[nw:2 * nw]
        ssem, rsem, osend, orecv = refs[2 * nw:]
        x, y, c = _place()
        chip = 2 * x + y
        sib = (x, y, 1 - c)
        peers = _peers(x, y)
        pidx = [2 * px + py for px, py in peers]

        def rect(n, half, ch):
            return _rect(out[n], kinds[n], half, ch, widths[n])

        mine = [_rcopy(src[n], _win(out[n], kinds[n], chip, widths[n]), osend.at[n], orecv.at[n], sib)
                for n in range(nw)]
        first = [[_rcopy(src[n].at[c], rect(n, c, chip), ssem.at[6 * n + k], rsem.at[6 * n + k], (px, py, c))
                  for k, (px, py) in enumerate(peers)] for n in range(nw)]
        for n in range(nw):
            for cp in first[n]:
                cp.start()
        for cp in mine:
            cp.start()
        passed = [[_rcopy(rect(n, c, pidx[k]), rect(n, c, pidx[k]), ssem.at[6 * n + 3 + k], rsem.at[6 * n + 3 + k], sib)
                   for k in range(3)] for n in range(nw)]
        for n in range(nw):
            for k, (px, py) in enumerate(peers):
                _rcopy(rect(n, c, pidx[k]), rect(n, c, pidx[k]), ssem.at[6 * n + k], rsem.at[6 * n + k],
                       (px, py, c)).wait_recv()
                passed[n][k].start()
        for n in range(nw):
            for k in range(3):
                _rcopy(rect(n, 1 - c, pidx[k]), rect(n, 1 - c, pidx[k]), ssem.at[6 * n + 3 + k],
                       rsem.at[6 * n + 3 + k], sib).wait_recv()
        for n in range(nw):
            for cp in first[n] + passed[n]:
                cp.wait_send()
        for cp in mine:
            cp.wait()

    return pl.pallas_call(
        body, name=name, in_specs=[ANY] * nw, out_specs=[ANY] * nw,
        out_shape=[jax.ShapeDtypeStruct(_gathered_shape(s, k), s.dtype) for s, k in zip(srcs, kinds)],
        scratch_shapes=[pltpu.SemaphoreType.DMA((6 * nw,)), pltpu.SemaphoreType.DMA((6 * nw,)),
                        pltpu.SemaphoreType.DMA((nw,)), pltpu.SemaphoreType.DMA((nw,))],
    )(*srcs)


def _gather_plan(kinds, widths):
    def plan(src, land, x, y, c):
        chip = 2 * x + y
        out = []
        for n in range(len(src)):
            mine = _win(land[n], kinds[n], chip, widths[n])
            for px, py in _peers(x, y):
                out.append((src[n], mine, (px, py, c), _win(land[n], kinds[n], 2 * px + py, widths[n])))
            out.append((src[n], mine, (x, y, 1 - c), mine))
        return out
    return plan


def _scatter_plan(axes, widths):
    def plan(src, land, x, y, c):
        out = []
        for n in range(len(src)):
            for k, (px, py) in enumerate(_peers(x, y)):
                ch = 2 * px + py
                view = (src[n].at[:, pl.ds(ch * widths[n], widths[n])] if axes[n] == 1
                        else src[n].at[pl.ds(ch * widths[n], widths[n]), :])
                out.append((view, land[n].at[k], (px, py, c), land[n].at[k]))
        return out
    return plan


def start_copies(srcs, lands, plan, ncopies, after, name):
    ns, nl = len(srcs), len(lands)

    def body(*refs):
        src, land = refs[:ns], refs[ns:ns + nl]
        ssem, rsem = refs[ns + nl + 1], refs[ns + nl + 2]
        token = refs[-1]
        x, y, c = _place()
        for k, (sv, dv, dev, _) in enumerate(plan(src, land, x, y, c)):
            _rcopy(sv, dv, ssem.at[k], rsem.at[k], dev).start()
        token[...] = jnp.zeros_like(token)

    hbm = lambda t: pltpu.HBM(t.shape, t.dtype)
    res = pl.pallas_call(
        body, name=name,
        out_shape=(pltpu.SemaphoreType.DMA((ncopies,)), pltpu.SemaphoreType.DMA((ncopies,)),
                   *[hbm(t) for t in srcs], *[hbm(t) for t in lands], jax.ShapeDtypeStruct((8, LANES), F32)),
        in_specs=[HBM] * (ns + nl) + [ANY],
        out_specs=(SEM, SEM, *[HBM] * (ns + nl), pl.BlockSpec(memory_space=pltpu.VMEM)),
        input_output_aliases={k: 2 + k for k in range(ns + nl)},
        compiler_params=pltpu.CompilerParams(has_side_effects=EFFECT),
    )(*[pltpu.with_memory_space_constraint(t, pltpu.HBM) for t in list(srcs) + list(lands)], after)
    return res[0], res[1], list(res[2:2 + ns]), list(res[2 + ns:2 + ns + nl]), res[-1]


def wait_copies(ssem, rsem, srcs, lands, plan, after, name):
    ns, nl = len(srcs), len(lands)

    def body(*refs):
        src, land = refs[:ns], refs[ns:ns + nl]
        ss, rs = refs[ns + nl], refs[ns + nl + 1]
        x, y, c = _place()
        for k, (sv, dv, dev, mine) in enumerate(plan(src, land, x, y, c)):
            cp = _rcopy(sv, mine, ss.at[k], rs.at[k], dev)
            cp.wait_send()
            cp.wait_recv()

    hbm = lambda t: pltpu.HBM(t.shape, t.dtype)
    res = pl.pallas_call(
        body, name=name,
        out_shape=(*[hbm(t) for t in srcs], *[hbm(t) for t in lands]),
        in_specs=[HBM] * (ns + nl) + [SEM, SEM, ANY], out_specs=tuple([HBM] * (ns + nl)),
        input_output_aliases={k: k for k in range(ns + nl)},
        compiler_params=pltpu.CompilerParams(has_side_effects=EFFECT),
    )(*srcs, *lands, ssem, rsem, after)
    return list(res[ns:])


def pair_swap_halves(gs, kinds, name):
    nw = len(gs)

    def other(ref, kind, half):
        return ref.at[half] if kind == 'col' else ref.at[:, half]

    def body(*refs):
        g, o = refs[:nw], refs[nw:2 * nw]
        ssem, rsem = refs[2 * nw:]
        x, y, c = _place()
        cps = [_rcopy(other(g[n], kinds[n], 1 - c), o[n], ssem.at[n], rsem.at[n], (x, y, 1 - c)) for n in range(nw)]
        for cp in cps:
            cp.start()
        for cp in cps:
            cp.wait()

    return pl.pallas_call(
        body, name=name, in_specs=[ANY] * nw, out_specs=[ANY] * nw,
        out_shape=[jax.ShapeDtypeStruct(g.shape[1:] if k == 'col' else (g.shape[0],) + g.shape[2:], g.dtype)
                   for g, k in zip(gs, kinds)],
        scratch_shapes=[pltpu.SemaphoreType.DMA((nw,)), pltpu.SemaphoreType.DMA((nw,))],
    )(*gs)


def pair_swap(fs, name):
    nw = len(fs)

    def body(*refs):
        f, o = refs[:nw], refs[nw:2 * nw]
        ssem, rsem = refs[2 * nw:]
        x, y, c = _place()
        cps = [_rcopy(f[n], o[n], ssem.at[n], rsem.at[n], (x, y, 1 - c)) for n in range(nw)]
        for cp in cps:
            cp.start()
        for cp in cps:
            cp.wait()

    return pl.pallas_call(
        body, name=name, in_specs=[ANY] * nw, out_specs=[ANY] * nw,
        out_shape=[jax.ShapeDtypeStruct(f.shape, f.dtype) for f in fs],
        scratch_shapes=[pltpu.SemaphoreType.DMA((nw,)), pltpu.SemaphoreType.DMA((nw,))],
    )(*fs)


def chip_exchange(hs, kinds, name):
    nw = len(hs)
    shp = [(h.shape[0], h.shape[1] // 4) if k == 'col' else h.shape[1:] for h, k in zip(hs, kinds)]

    def body(*refs):
        h, o = refs[:nw], refs[nw:2 * nw]
        ssem, rsem = refs[2 * nw:]
        x, y, c = _place()

        def win(n, ch):
            return h[n].at[:, pl.ds(ch * shp[n][1], shp[n][1])] if kinds[n] == 'col' else h[n].at[ch]

        cps = [_rcopy(win(n, 2 * px + py), o[n].at[k], ssem.at[3 * n + k], rsem.at[3 * n + k], (px, py, c))
               for n in range(nw) for k, (px, py) in enumerate(_peers(x, y))]
        for cp in cps:
            cp.start()
        for cp in cps:
            cp.wait()

    return pl.pallas_call(
        body, name=name, in_specs=[ANY] * nw, out_specs=[ANY] * nw,
        out_shape=[jax.ShapeDtypeStruct((3,) + sh, h.dtype) for sh, h in zip(shp, hs)],
        scratch_shapes=[pltpu.SemaphoreType.DMA((3 * nw,)), pltpu.SemaphoreType.DMA((3 * nw,))],
    )(*hs)


def pair_join_layers(fs, name):
    nw = len(fs)

    def body(*refs):
        o = refs[nw:2 * nw]
        ssem, rsem = refs[2 * nw:]
        x, y, c = _place()
        sib = (x, y, 1 - c)
        cps = [_rcopy(o[n].at[c], o[n].at[c], ssem.at[n], rsem.at[n], sib) for n in range(nw)]
        for cp in cps:
            cp.start()
        for n in range(nw):
            cps[n].wait_send()
            _rcopy(o[n].at[1 - c], o[n].at[1 - c], ssem.at[n], rsem.at[n], sib).wait_recv()

    return pl.pallas_call(
        body, name=name, in_specs=[ANY] * nw, out_specs=[ANY] * nw,
        out_shape=[jax.ShapeDtypeStruct(f.shape, f.dtype) for f in fs],
        input_output_aliases={n: n for n in range(nw)},
        scratch_shapes=[pltpu.SemaphoreType.DMA((nw,)), pltpu.SemaphoreType.DMA((nw,))],
    )(*fs)


def gather_all_devices(buf, name):
    r, c_ = buf.shape
    offs = [o for o in itertools.product((0, 1), repeat=3) if o != (0, 0, 0)]

    def body(b_ref, o_ref, ssem, rsem, lsem):
        x, y, c = _place()
        me = 4 * x + 2 * y + c
        mine = pltpu.make_async_copy(b_ref, o_ref.at[me], lsem)
        mine.start()
        peers = [((x + dx) % 2, (y + dy) % 2, (c + dc) % 2) for dx, dy, dc in offs]
        cps = [_rcopy(b_ref, o_ref.at[me], ssem.at[k], rsem.at[k], p) for k, p in enumerate(peers)]
        for cp in cps:
            cp.start()
        for k, (px, py, pc) in enumerate(peers):
            _rcopy(b_ref, o_ref.at[4 * px + 2 * py + pc], ssem.at[k], rsem.at[k], (px, py, pc)).wait_recv()
        for cp in cps:
            cp.wait_send()
        mine.wait()

    return pl.pallas_call(
        body, name=name, in_specs=[ANY], out_specs=ANY,
        out_shape=jax.ShapeDtypeStruct((8, r, c_), buf.dtype),
        scratch_shapes=[pltpu.SemaphoreType.DMA((7,)), pltpu.SemaphoreType.DMA((7,)), pltpu.SemaphoreType.DMA],
    )(buf)


def _flatten_pad(parts, dtype):
    flat = jnp.concatenate([p.reshape(-1).astype(dtype) for p in parts])
    q = 512 * LANES
    n = -(-flat.shape[0] // q) * q
    return jnp.pad(flat, (0, n - flat.shape[0])).reshape(n // LANES, LANES)


def _lane_pad(n):
    return -(-n // LANES) * LANES


def _in_proj_layout(d):
    gk, gv, cw, pw = d // 2, d, d // 2, d // 2
    own = [('q', gk), ('k', gk), ('v', gv), ('og', gv), ('lrf', GLA_LR), ('lrb', GLA_LR), ('ga', cw), ('gb', cw),
           ('pu', pw), ('mg', 3 * d)]
    padded = [('mg', 3 * d), ('v', gv), ('og', gv), ('q', gk), ('k', gk), ('ga', cw), ('gb', cw), ('pu', pw),
              ('lrf', GLA_LR), ('lrb', GLA_LR), ('pad', d // 2 - 2 * GLA_LR)]
    return own, padded


def _pad_w_in(w, d):
    own, padded = _in_proj_layout(d)
    cols, start = {}, 0
    for n, wd in own:
        cols[n] = w[:, start:start + wd]
        start += wd
    return jnp.concatenate([cols[n] if n != 'pad' else jnp.zeros((w.shape[0], wd), w.dtype) for n, wd in padded], axis=1)


def _unpad_w_in(wp, d):
    own, padded = _in_proj_layout(d)
    cols, start = {}, 0
    for n, wd in padded:
        cols[n] = wp[:, start:start + wd]
        start += wd
    return jnp.concatenate([cols[n] for n, _ in own], axis=1)


def _silu_grad(z):
    s = jax.nn.sigmoid(z)
    return s + z * s * (1.0 - s)


def kernel(x, c, ctx, c_ctx, w_ada, b_ada, g_pre_mix, g_post_mix, g_pre_mlp, g_post_mlp, w_in, w_decay, b_decay, g_gla, w_gla_o, w_dw, b_dw, g_conv_ln, b_conv_ln, w_conv_o, w_pool_g, s_pool, w_pool_o, b_gate, w_out, w_mlp1, w_mlp2, loss_target, m_c_ctx, m_w_ada, m_b_ada, m_g_pre_mix, m_g_post_mix, m_g_pre_mlp, m_g_post_mlp, m_w_in, m_w_decay, m_b_decay, m_g_gla, m_w_gla_o, m_w_dw, m_b_dw, m_g_conv_ln, m_b_conv_ln, m_w_conv_o, m_w_pool_g, m_s_pool, m_w_pool_o, m_b_gate, m_w_out, m_w_mlp1, m_w_mlp2, v_c_ctx, v_w_ada, v_b_ada, v_g_pre_mix, v_g_post_mix, v_g_pre_mlp, v_g_post_mlp, v_w_in, v_w_decay, v_b_decay, v_g_gla, v_w_gla_o, v_w_dw, v_b_dw, v_g_conv_ln, v_b_conv_ln, v_w_conv_o, v_w_pool_g, v_s_pool, v_w_pool_o, v_b_gate, v_w_out, v_w_mlp1, v_w_mlp2):
    a = dict(locals())
    depth = w_in.shape[0]
    d = x.shape[-1]
    seq, nctx_rows = x.shape[1], ctx.shape[1]
    dm = types.SimpleNamespace(
        D=d, SEQ=seq, CTX=nctx_rows, T=seq + nctx_rows, DK=d // 8, DV=d // 4, GK=d // 2, GC=d // 8,
        tm=_tile(nctx_rows, (256, 128, 64)), TB=_tile(nctx_rows, (256, 128, 64)))
    assert dm.SEQ % dm.tm == 0 and dm.SEQ % GRID_W == 0 and dm.CTX % GLA_CHUNK == 0
    tmw = min(dm.tm, 128)
    chip = 2 * lax.axis_index("x") + lax.axis_index("y")
    core = lax.axis_index("c")
    chip1 = chip.astype(jnp.int32).reshape(1)
    core1 = core.astype(jnp.int32).reshape(1)

    big_names, small_names = list(BIG), list(SMALL_SHARDED)
    nbig = len(big_names)
    kinds = ['col' if BIG[n] == 2 else 'row' for n in big_names]
    wl = w_in.shape[2]
    wlp = _lane_pad(wl)

    def rows8(t):
        t = t.reshape(t.shape[0], -1, t.shape[-1])
        return jnp.pad(t, ((0, 0), (0, -t.shape[1] % 8), (0, 0)))

    def halves(t):
        return t.reshape(2, t.shape[0] // 2, t.shape[1])

    def layer_src(l):
        return [halves((jnp.pad(a[n][l], ((0, 0), (0, wlp - wl))) if n == 'w_in' else a[n][l]).astype(MM_DTYPE))
                for n in big_names]

    def whole(t):
        return t.reshape(-1, t.shape[-1])

    g0 = gather_halves(layer_src(0) + [rows8(a[n]) for n in small_names], kinds + ['col'] * len(small_names),
                       "gather_layer0")
    src1 = layer_src(1)
    widths1 = [t.shape[2] for t in src1]
    plan1 = _gather_plan(kinds, widths1)
    lands1 = [lax.empty(_gathered_shape(t, k), t.dtype) for t, k in zip(src1, kinds)]
    ag_ss, ag_rs, src1, lands1, ag_token = start_copies(src1, lands1, plan1, 4 * nbig, g0[0], "gather_layer1_start")
    full = {n: [whole(t), None] for n, t in zip(big_names, g0)}
    for n, g in zip(small_names, g0[nbig:]):
        shp = a[n].shape
        full[n] = g[:, :math.prod(shp[1:-1])].reshape(shp[:-1] + (4 * shp[-1],))
    for n in SMALL:
        if n not in SMALL_SHARDED:
            full[n] = a[n]

    cvec = jnp.concatenate([c_ctx.reshape(1, d), c.reshape(1, d), jnp.zeros((6, d), F32)], axis=0)
    avec = (cvec * jax.nn.sigmoid(cvec) + ag_token[0, 0]).astype(MM_DTYPE)

    def row(v):
        return v.reshape(1, -1)

    X = jnp.concatenate([ctx[0], x[0]], axis=0)
    saved = []
    gk, gv = dm.GK, d
    lrblk = (7 * d + d // 2) // LANES
    for l in range(depth):
        if l == 1:
            got = wait_copies(ag_ss, ag_rs, src1, lands1, plan1, X, "gather_layer1_wait")
            for n, t in zip(big_names, got):
                full[n][1] = whole(t)
        s = types.SimpleNamespace()
        s.w_in_p = _pad_w_in(full['w_in'][l].reshape(d, 4, wlp)[:, :, :wl].reshape(d, 4 * wl), d)
        wd = full['w_decay'][l]
        wdp = jnp.zeros((LANES, 2 * gk), F32)
        wdp = wdp.at[:GLA_LR, :gk].set(wd[0]).at[GLA_LR:2 * GLA_LR, gk:].set(wd[1])
        s.wdp = wdp.astype(MM_DTYPE)
        s.bd = full['b_decay'][l].reshape(1, 2 * gk)
        modraw = matmul(avec, full['w_ada'][l], 'nn', F32, f"mod_{l}") + full['b_ada'][l][None, :]
        s.mod = [modraw[0:2, j * d:(j + 1) * d].reshape(2, 1, d) for j in range(6)]
        s.x = X
        (s.h,) = rowwise(pre_fn, [X], s.mod[0:2], [row(g_pre_mix[l])], [(d, MM_DTYPE)], dm, f"pre_{l}")
        s.P = matmul(s.h, s.w_in_p, 'nn', MM_DTYPE, f"in_proj_{l}")
        P = s.P
        s.z = matmul((P, LANES, lrblk), s.wdp, 'nn', F32, f"decay_proj_{l}", tk=LANES)
        la_f, la_b = rowwise(decay_fn, [s.z], [], [s.bd], [(gk, F32), (gk, F32)], dm, f"decay_{l}")
        s.la = jnp.concatenate([la_f, la_b], axis=1)
        s.o_f, s.st_f = gla_fwd(P, s.la, False, dm, f"gla_fwd_f_{l}")
        s.o_b, s.st_b = gla_fwd(P, s.la, True, dm, f"gla_fwd_b_{l}")
        (s.gin,) = rowwise(glaout_fn, [s.o_f, s.o_b, (P, d, 4)], [], [row(g_gla[l])], [(gv, MM_DTYPE)], dm,
                           f"gla_out_{l}")
        s.ya = matmul(s.gin, full['w_gla_o'][l], 'nn', F32, f"gla_o_{l}")
        (s.u,) = rowwise(glu_fn, [(P, d // 2, 12), (P, d // 2, 13)], [], [], [(d // 2, F32)], dm, f"glu_{l}")
        s.yconv = conv_fwd(s.u, full['w_dw'][l], dm, f"conv_{l}")
        (s.cin,) = rowwise(convpost_fn, [s.yconv], [], [row(b_dw[l]), row(g_conv_ln[l]), row(b_conv_ln[l])],
                           [(d // 2, MM_DTYPE)], dm, f"conv_post_{l}")
        s.yb = matmul(s.cin, full['w_conv_o'][l], 'nn', F32, f"conv_o_{l}")
        s.pm = pool_mix((P, d // 2, 14), False, dm, f"pool_mix_{l}")
        s.pc = group_mm(s.pm, w_pool_g[l], 'nn', F32, f"pool_g_{l}")
        (s.pin,) = rowwise(poolpost_fn, [s.pc], [], [row(s_pool[l])], [(d // 2, MM_DTYPE)], dm, f"pool_post_{l}")
        s.yc = matmul(s.pin, full['w_pool_o'][l], 'nn', F32, f"pool_o_{l}")
        s.bg = [row(full['b_gate'][l][j]) for j in range(3)]
        (s.mixed,) = rowwise(merge_fn, [s.ya, s.yb, s.yc, (P, 3 * d, 0)], [], s.bg, [(d, MM_DTYPE)], dm,
                             f"merge_{l}", tm=tmw)
        s.y = matmul(s.mixed, full['w_out'][l], 'nn', F32, f"out_proj_{l}")
        s.x1, s.h2 = rowwise(mid_fn, [X, s.y], s.mod[2:5], [row(g_post_mix[l]), row(g_pre_mlp[l])],
                             [(d, F32), (d, MM_DTYPE)], dm, f"mid_{l}")
        s.u1 = matmul(s.h2, full['w_mlp1'][l], 'nn', F32, f"mlp1_{l}")
        (s.act,) = rowwise(relu2_fn, [s.u1], [], [], [(4 * d, MM_DTYPE)], dm, f"relu2_{l}", tm=tmw)
        s.y2 = matmul(s.act, full['w_mlp2'][l], 'nn', F32, f"mlp2_{l}")
        (X,) = rowwise(post_fn, [s.x1, s.y2], s.mod[5:6], [row(g_post_mlp[l])], [(d, F32)], dm, f"post_{l}")
        saved.append(s)

    dX, lossv = loss_head(X, loss_target[0], dm, "loss_head")
    loss = lax.psum(lossv[0, 0], ("x", "y", "c"))

    grads = {n: [None] * depth for n in WEIGHTS if n != 'c_ctx' and n not in BIG}
    gbig = {n: [None] * depth for n in BIG}
    rs_token = None
    g_cctx = jnp.zeros((d,), F32)
    for l in reversed(range(depth)):
        s = saved[l]
        P = s.P
        dmod = [None] * 6
        gpm = row(g_post_mlp[l]) if rs_token is None else row(g_post_mlp[l]) + rs_token[0, 0]
        (dx1, dy2), (dmod[5],), (dg,) = rowwise_vjp(post_fn, [s.x1, s.y2], s.mod[5:6], [gpm], [dX],
                                                     dm, f"post_bwd_{l}")
        grads['g_post_mlp'][l] = dg[0]
        dact = matmul(dy2, full['w_mlp2'][l], 'nt', MM_DTYPE, f"mlp2_dx_{l}")
        gbig['w_mlp2'][l] = matmul(s.act, dy2, 'tn', MM_DTYPE, f"mlp2_dw_{l}")
        (du1,), _, _ = rowwise_vjp(relu2_fn, [s.u1], [], [], [dact], dm, f"relu2_bwd_{l}", tm=tmw)
        dh2 = matmul(du1, full['w_mlp1'][l], 'nt', MM_DTYPE, f"mlp1_dx_{l}")
        gbig['w_mlp1'][l] = matmul(s.h2, du1, 'tn', MM_DTYPE, f"mlp1_dw_{l}")
        (dxa, dy), dmod[2:5], (dg1, dg2) = rowwise_vjp(
            mid_fn, [s.x, s.y], s.mod[2:5], [row(g_post_mix[l]), row(g_pre_mlp[l])], [dx1, dh2], dm, f"mid_bwd_{l}")
        grads['g_post_mix'][l], grads['g_pre_mlp'][l] = dg1[0], dg2[0]
        dmixed = matmul(dy, full['w_out'][l], 'nt', MM_DTYPE, f"out_proj_dx_{l}")
        gbig['w_out'][l] = matmul(s.mixed, dy, 'tn', MM_DTYPE, f"out_proj_dw_{l}")
        (dya, dyb, dyc, dmg), _, dbg = rowwise_vjp(merge_fn, [s.ya, s.yb, s.yc, (P, 3 * d, 0)], [], s.bg, [dmixed],
                                                   dm, f"merge_bwd_{l}", tm=tmw)
        grads['b_gate'][l] = jnp.concatenate(dbg, axis=0)
        dgin = matmul(dya, full['w_gla_o'][l], 'nt', MM_DTYPE, f"gla_o_dx_{l}")
        gbig['w_gla_o'][l] = matmul(s.gin, dya, 'tn', MM_DTYPE, f"gla_o_dw_{l}")
        dcin = matmul(dyb, full['w_conv_o'][l], 'nt', MM_DTYPE, f"conv_o_dx_{l}")
        gbig['w_conv_o'][l] = matmul(s.cin, dyb, 'tn', MM_DTYPE, f"conv_o_dw_{l}")
        dpin = matmul(dyc, full['w_pool_o'][l], 'nt', MM_DTYPE, f"pool_o_dx_{l}")
        gbig['w_pool_o'][l] = matmul(s.pin, dyc, 'tn', MM_DTYPE, f"pool_o_dw_{l}")
        (dpc,), _, (dsp,) = rowwise_vjp(poolpost_fn, [s.pc], [], [row(s_pool[l])], [dpin], dm, f"pool_post_bwd_{l}")
        grads['s_pool'][l] = dsp[0]
        grads['w_pool_g'][l] = group_mm(s.pm, w_pool_g[l], 'tn', F32, f"pool_g_dw_{l}", b=dpc)
        dpm = group_mm(dpc, w_pool_g[l], 'nt', F32, f"pool_g_dx_{l}")
        dpu = pool_mix(dpm, True, dm, f"pool_mix_bwd_{l}")
        (dyconv,), _, (dbdw, dgln, dbln) = rowwise_vjp(
            convpost_fn, [s.yconv], [], [row(b_dw[l]), row(g_conv_ln[l]), row(b_conv_ln[l])], [dcin], dm,
            f"conv_post_bwd_{l}")
        grads['b_dw'][l], grads['g_conv_ln'][l], grads['b_conv_ln'][l] = dbdw[0], dgln[0], dbln[0]
        du, grads['w_dw'][l] = conv_bwd(s.u, full['w_dw'][l], dyconv, dm, f"conv_bwd_{l}")
        (dga, dgb), _, _ = rowwise_vjp(glu_fn, [(P, d // 2, 12), (P, d // 2, 13)], [], [], [du], dm, f"glu_bwd_{l}")
        (do, _, dog), _, (dgg,) = rowwise_vjp(glaout_fn, [s.o_f, s.o_b, (P, d, 4)], [], [row(g_gla[l])], [dgin], dm,
                                              f"gla_out_bwd_{l}", want=[True, False, True])
        grads['g_gla'][l] = dgg[0]
        dqf, dkf, dvf, dlaf = gla_bwd(P, s.la, do, s.st_f, False, dm, f"gla_bwd_f_{l}")
        dqb, dkb, dvb, dlab = gla_bwd(P, s.la, do, s.st_b, True, dm, f"gla_bwd_b_{l}")
        (dz,), _, (dbd,) = rowwise_vjp(decay_fn, [s.z], [], [s.bd], [dlaf, dlab], dm, f"decay_bwd_{l}")
        grads['b_decay'][l] = dbd.reshape(2, gk)
        dwdp = matmul((P, LANES, lrblk), dz, 'tn', F32, f"decay_proj_dw_{l}", tm=LANES)
        grads['w_decay'][l] = jnp.stack([dwdp[:GLA_LR, :gk], dwdp[GLA_LR:2 * GLA_LR, gk:]])
        dlr = matmul(dz, s.wdp, 'nt', F32, f"decay_proj_dx_{l}")

        def asm_fn(dmg_, dvf_, dvb_, dog_, dqf_, dqb_, dkf_, dkb_, dga_, dgb_, dpu_, dlr_):
            f = lambda t: t.astype(F32)
            pad = jnp.zeros((dlr_.shape[0], d // 2 - LANES), F32)
            return (jnp.concatenate([f(dmg_), dvf_ + dvb_, f(dog_), dqf_ + dqb_, dkf_ + dkb_, f(dga_), f(dgb_),
                                     dpu_, dlr_, pad], axis=1).astype(MM_DTYPE),)
        (dP,) = rowwise(asm_fn, [dmg, dvf, dvb, dog, dqf, dqb, dkf, dkb, dga, dgb, dpu, dlr], [], [],
                        [(8 * d, MM_DTYPE)], dm, f"dproj_{l}", tm=tmw)
        dh = matmul(dP, s.w_in_p, 'nt', MM_DTYPE, f"in_proj_dx_{l}")
        gwin = _unpad_w_in(matmul(s.h, dP, 'tn', MM_DTYPE, f"in_proj_dw_{l}"), d)
        gbig['w_in'][l] = jnp.pad(gwin.reshape(d, 4, wl), ((0, 0), (0, 0), (0, wlp - wl))).reshape(d, 4 * wlp)
        (dX,), dmod[0:2], (dg,) = rowwise_vjp(pre_fn, [s.x], s.mod[0:2], [row(g_pre_mix[l])], [dh], dm,
                                               f"pre_bwd_{l}", adds={0: dxa})
        grads['g_pre_mix'][l] = dg[0]
        dmodflat = jnp.concatenate([jnp.concatenate([m_.reshape(2, d) for m_ in dmod], axis=1),
                                    jnp.zeros((6, 6 * d), F32)], axis=0)
        grads['b_ada'][l] = dmodflat[0] + dmodflat[1]
        gbig['w_ada'][l] = matmul(avec, dmodflat, 'tn', MM_DTYPE, f"ada_dw_{l}")
        dav = matmul(dmodflat, full['w_ada'][l], 'nt', F32, f"ada_dx_{l}")
        g_cctx = g_cctx + dav[0] * _silu_grad(c_ctx)
        if l == 1:
            g1 = [gbig[n][1] for n in big_names]
            rs_widths = [t.shape[1] // 4 if k == 'col' else t.shape[0] // 4 for t, k in zip(g1, kinds)]
            rs_plan = _scatter_plan([BIG[n] - 1 for n in big_names], rs_widths)
            rs_lands = [lax.empty((3, t.shape[0], w) if k == 'col' else (3, w, t.shape[1]), t.dtype)
                        for t, w, k in zip(g1, rs_widths, kinds)]
            rs_ss, rs_rs, g1, rs_lands, rs_token = start_copies(g1, rs_lands, rs_plan, 3 * nbig, dav,
                                                                "grad_layer1_start")

    grad_x = dX[dm.CTX:][None]
    gfull = {n: jnp.stack(v) for n, v in grads.items()}
    gfull['c_ctx'] = g_cctx
    where = jnp.concatenate([chip1, core1])
    ax2 = [BIG[n] - 1 for n in big_names]

    def halves_view(t, k):
        return t.reshape(2, t.shape[0] // 2, t.shape[1]) if k == 'col' else t.reshape(4, 2, t.shape[0] // 8, t.shape[1])
    v0 = [halves_view(gbig[n][0], k) for n, k in zip(big_names, kinds)]
    r1 = pair_swap_halves(v0, kinds, "grad_pair_swap")
    hs = [pair_add(v.reshape((-1,) + v.shape[-2:]), r.reshape((-1,) + r.shape[-2:]), core1, f"grad_pair_add_{n}")
          for n, v, r in zip(big_names, v0, r1)]
    hx = [h[0] if k == 'col' else h for h, k in zip(hs, kinds)]
    r2 = chip_exchange(hx, kinds, "grad_chip_exchange")
    fs = [chip_add(h.reshape(-1, h.shape[-1]), r, ax, where, f"grad_chip_add_{n}")
          for n, h, r, ax in zip(big_names, hs, r2, ax2)]
    red0 = [t.reshape(-1, t.shape[-1]) for t in pair_join_layers(fs, "grad_pair_join")]

    rs_got = wait_copies(rs_ss, rs_rs, g1, rs_lands, rs_plan, dX, "grad_layer1_wait")
    s1a = [chip_add(g, r, ax, where, f"grad_layer1_add_{n}", slab=False)
           for n, g, r, ax in zip(big_names, g1, rs_got, ax2)]
    wi = big_names.index('w_in')
    red0[wi], s1a[wi] = red0[wi][:, :wl], s1a[wi][:, :wl]
    s1b = pair_swap(s1a, "grad_layer1_pair_swap")

    sflat = _flatten_pad([gfull[n].astype(F32) for n in SMALL], F32)
    ssum = slot_sum(gather_all_devices(sflat, "small_grad_gather"), "small_grad_sum").reshape(-1)

    out_g, out_d, out_m, out_v = {}, {}, {}, {}
    for k, n in enumerate(big_names):
        out_g[n], out_d[n], out_m[n], out_v[n] = adamw_layers(a[n], a['m_' + n], a['v_' + n], red0[k], s1a[k], s1b[k],
                                                              f"adamw_{n}")
    start = 0
    sg = {}
    for n in SMALL:
        cnt = gfull[n].size
        g = ssum[start:start + cnt].reshape(gfull[n].shape)
        start += cnt
        if n in SMALL_SHARDED:
            ax = SMALL_SHARDED[n]
            wdt = a[n].shape[ax]
            g = lax.dynamic_slice_in_dim(g, chip * wdt, wdt, axis=ax)
        sg[n] = g
    pk = lambda dct, pre: _flatten_pad([dct[pre + n] for n in SMALL], F32)
    gs = _flatten_pad([sg[n] for n in SMALL], F32)
    dl, mn, vn = adamw(pk(a, ''), gs, pk(a, 'm_'), pk(a, 'v_'), "adamw_small")
    dl, mn, vn = dl.reshape(-1), mn.reshape(-1), vn.reshape(-1)
    start = 0
    for n in SMALL:
        cnt, shp = a[n].size, a[n].shape
        out_g[n] = sg[n]
        out_d[n], out_m[n], out_v[n] = (t[start:start + cnt].reshape(shp) for t in (dl, mn, vn))
        start += cnt

    return (loss, grad_x, *[out_g[n] for n in WEIGHTS], *[out_d[n] for n in WEIGHTS],
            *[out_m[n] for n in WEIGHTS], *[out_v[n] for n in WEIGHTS])
```

```python
import functools
import itertools
import math
import types

import jax
import jax.numpy as jnp
from jax import lax
from jax.experimental import pallas as pl
from jax.experimental.pallas import tpu as pltpu

F32 = jnp.float32
MM_DTYPE = jnp.bfloat16
VMEM_LIMIT_V7X = 56 * 1024 * 1024
LANES = 128
EPS = 1e-6

N_HEADS = 4
GLA_CHUNK = 64
GLA_TAU = 16.0
GLA_LR = 16
GRID_W = 64
POOL_WINDOWS = (2, 4, 8, 16)

ADAM_LR = 0.001
ADAM_B1 = 0.9
ADAM_B2 = 0.999
ADAM_EPS = 1e-08
ADAM_WD = 0.01
ADAM_STEP = 10

NN = (((1,), (0,)), ((), ()))
NT = (((1,), (1,)), ((), ()))
TN = (((0,), (0,)), ((), ()))

WEIGHTS = ['c_ctx', 'w_ada', 'b_ada', 'g_pre_mix', 'g_post_mix', 'g_pre_mlp', 'g_post_mlp', 'w_in', 'w_decay',
           'b_decay', 'g_gla', 'w_gla_o', 'w_dw', 'b_dw', 'g_conv_ln', 'b_conv_ln', 'w_conv_o', 'w_pool_g',
           's_pool', 'w_pool_o', 'b_gate', 'w_out', 'w_mlp1', 'w_mlp2']
BIG = {'w_ada': 2, 'w_in': 2, 'w_gla_o': 1, 'w_conv_o': 2, 'w_pool_o': 2, 'w_out': 1, 'w_mlp1': 2, 'w_mlp2': 1}
SMALL_SHARDED = {'w_decay': 3, 'b_decay': 2, 'w_dw': 2, 'b_gate': 2}
SMALL = [n for n in WEIGHTS if n not in BIG]


def _tile(n, prefs):
    for t in prefs:
        if n % t == 0:
            return t
    return n


def _cparams(sem=None, **kw):
    return pltpu.CompilerParams(dimension_semantics=sem, vmem_limit_bytes=VMEM_LIMIT_V7X, **kw)


def _dot(a, b, dims=NN):
    return lax.dot_general(a.astype(MM_DTYPE), b.astype(MM_DTYPE), dims, preferred_element_type=F32)


def matmul(a, b, mode, out_dtype, name, tm=None, tn=None, tk=None, epi=None, extras=()):
    a, aw, ablk = a if isinstance(a, tuple) else (a, a.shape[1], 0)
    if mode == 'nn':
        M, K, N = a.shape[0], aw, b.shape[1]
    elif mode == 'nt':
        M, K, N = a.shape[0], aw, b.shape[0]
    else:
        K, M, N = a.shape[0], aw, b.shape[1]
    big = (1088, 1024, 640, 544, 512, 320, 256, 128, 64, 32, 16, 8)
    if mode == 'tn':
        tm = tm or _tile(M, (1024, 512, 256, 128))
        tn = tn or _tile(N, (1024, 512, 256, 128))
        tk = tk or _tile(K, big)
    else:
        tm = tm or _tile(M, big)
        tn = tn or _tile(N, (1024, 512, 256, 128))
        tk = tk or _tile(K, (1024, 512, 256, 128))
    if aw != a.shape[1]:
        assert (mode == 'tn' and tm == aw) or (mode != 'tn' and tk == aw)
    nk = K // tk
    ne = len(extras)
    dims = {'nn': NN, 'nt': NT, 'tn': TN}[mode]

    def body(a_ref, b_ref, *rest):
        e_refs, o_ref = rest[:ne], rest[ne]

        def finish(acc):
            if epi is not None:
                acc = epi(acc, *[e[...] for e in e_refs])
            o_ref[...] = acc.astype(o_ref.dtype)

        p = _dot(a_ref[...], b_ref[...], dims)
        if nk == 1:
            finish(p)
            return
        acc = rest[-1]
        k = pl.program_id(2)

        @pl.when(k == 0)
        def _():
            acc[...] = p

        @pl.when(k > 0)
        def _():
            acc[...] += p

        @pl.when(k == nk - 1)
        def _():
            finish(acc[...])

    if mode == 'nn':
        a_spec = pl.BlockSpec((tm, tk), lambda i, j, k: (i, k + ablk))
        b_spec = pl.BlockSpec((tk, tn), lambda i, j, k: (k, j))
    elif mode == 'nt':
        a_spec = pl.BlockSpec((tm, tk), lambda i, j, k: (i, k + ablk))
        b_spec = pl.BlockSpec((tn, tk), lambda i, j, k: (j, k))
    else:
        a_spec = pl.BlockSpec((tk, tm), lambda i, j, k: (k, i + ablk))
        b_spec = pl.BlockSpec((tk, tn), lambda i, j, k: (k, j))
    tile = pl.BlockSpec((tm, tn), lambda i, j, k: (i, j))
    return pl.pallas_call(
        body, name=name, grid=(M // tm, N // tn, nk),
        in_specs=[a_spec, b_spec] + [tile] * ne, out_specs=tile,
        out_shape=jax.ShapeDtypeStruct((M, N), out_dtype),
        scratch_shapes=[] if nk == 1 else [pltpu.VMEM((tm, tn), F32)],
        compiler_params=_cparams(("parallel", "parallel", "arbitrary")),
    )(a, b, *extras)


def group_mm(a, w, mode, out_dtype, name, b=None):
    T = a.shape[0]
    G, gc, _ = w.shape
    col = pl.BlockSpec((T, gc), lambda g: (0, g))
    wsp = pl.BlockSpec((1, gc, gc), lambda g: (g, 0, 0))
    if mode == 'tn':
        def body(a_ref, b_ref, o_ref):
            o_ref[0] = _dot(a_ref[...], b_ref[...], TN).astype(o_ref.dtype)
        return pl.pallas_call(body, name=name, grid=(G,), in_specs=[col, col], out_specs=wsp,
                              out_shape=jax.ShapeDtypeStruct((G, gc, gc), out_dtype),
                              compiler_params=_cparams(("parallel",)))(a, b)
    dims = NN if mode == 'nn' else NT

    def body(a_ref, w_ref, o_ref):
        o_ref[...] = _dot(a_ref[...], w_ref[0], dims).astype(o_ref.dtype)
    return pl.pallas_call(body, name=name, grid=(G,), in_specs=[col, wsp], out_specs=col,
                          out_shape=jax.ShapeDtypeStruct((T, G * gc), out_dtype),
                          compiler_params=_cparams(("parallel",)))(a, w)


def _rowspec(r):
    return r if isinstance(r, tuple) else (r, r.shape[1], 0)


def _row_specs(rows, segs, consts, tm, nctx):
    specs = [pl.BlockSpec((tm, w), lambda i, b=b: (i, b)) for _, w, b in rows]
    specs += [pl.BlockSpec((1,) + s.shape[1:], lambda i, n=s.ndim: (jnp.where(i >= nctx, 1, 0),) + (0,) * (n - 1))
              for s in segs]
    specs += [pl.BlockSpec(c.shape, lambda i, n=c.ndim: (0,) * n) for c in consts]
    return specs


def rowwise(fn, rows, segs, consts, outs, dm, name, tm=None):
    tm = tm or dm.tm
    nctx = dm.CTX // tm
    rows = [_rowspec(r) for r in rows]
    nr, ns, nc = len(rows), len(segs), len(consts)

    def body(*refs):
        rin = [r[...] for r in refs[:nr]]
        sin = [s[0] for s in refs[nr:nr + ns]]
        cin = [c[...] for c in refs[nr + ns:nr + ns + nc]]
        res = fn(*rin, *sin, *cin)
        for o_ref, v in zip(refs[nr + ns + nc:], res):
            o_ref[...] = v.astype(o_ref.dtype)

    res = pl.pallas_call(
        body, name=name, grid=(dm.T // tm,),
        in_specs=_row_specs(rows, segs, consts, tm, nctx),
        out_specs=[pl.BlockSpec((tm, w), lambda i: (i, 0)) for w, _ in outs],
        out_shape=[jax.ShapeDtypeStruct((dm.T, w), dt) for w, dt in outs],
        compiler_params=_cparams(("parallel",)),
    )(*[r[0] for r in rows], *segs, *consts)
    return res


def rowwise_vjp(fn, rows, segs, consts, cots, dm, name, tm=None, want=None, adds=None, narrow=()):
    tm = tm or dm.tm
    nctx = dm.CTX // tm
    rows = [_rowspec(r) for r in rows]
    cots = [_rowspec(r) for r in cots]
    adds = adds or {}
    nr, ns, nc, nct = len(rows), len(segs), len(consts), len(cots)
    want = want or [True] * nr
    widx = [k for k in range(nr) if want[k]]
    akeys = sorted(adds)

    def body(*refs):
        i = pl.program_id(0)
        rin = [r[...] for r in refs[:nr]]
        sin = [s[0] for s in refs[nr:nr + ns]]
        cin = [c[...] for c in refs[nr + ns:nr + ns + nc]]
        p = nr + ns + nc
        cot_refs = refs[p:p + nct]
        add_refs = dict(zip(akeys, refs[p + nct:p + nct + len(akeys)]))
        p = p + nct + len(akeys)
        rg_refs = refs[p:p + len(widx)]
        sg_refs = refs[p + len(widx):p + len(widx) + ns]
        cg_refs = refs[p + len(widx) + ns:]
        res, vjp = jax.vjp(fn, *rin, *sin, *cin)
        g = vjp(tuple(cr[...].astype(o.dtype) for cr, o in zip(cot_refs, res)))
        for o_ref, k in zip(rg_refs, widx):
            v = g[k].astype(F32)
            if k in add_refs:
                v = v + add_refs[k][...]
            o_ref[...] = v.astype(o_ref.dtype)
        first_seg = jnp.logical_or(i == 0, i == nctx)
        for o_ref, v in zip(sg_refs, g[nr:nr + ns]):
            @pl.when(first_seg)
            def _(o_ref=o_ref, v=v):
                o_ref[0] = v.astype(F32)

            @pl.when(jnp.logical_not(first_seg))
            def _(o_ref=o_ref, v=v):
                o_ref[0] += v.astype(F32)
        for o_ref, v in zip(cg_refs, g[nr + ns:]):
            @pl.when(i == 0)
            def _(o_ref=o_ref, v=v):
                o_ref[...] = v.astype(F32)

            @pl.when(i > 0)
            def _(o_ref=o_ref, v=v):
                o_ref[...] += v.astype(F32)

    in_specs = _row_specs(rows, segs, consts, tm, nctx)
    in_specs += [pl.BlockSpec((tm, w), lambda i, b=b: (i, b)) for _, w, b in cots]
    in_specs += [pl.BlockSpec((tm, adds[k].shape[1]), lambda i: (i, 0)) for k in akeys]
    out_specs = [pl.BlockSpec((tm, rows[k][1]), lambda i: (i, 0)) for k in widx]
    out_shape = [jax.ShapeDtypeStruct((dm.T, rows[k][1]), MM_DTYPE if k in narrow else rows[k][0].dtype)
                 for k in widx]
    out_specs += [pl.BlockSpec((1,) + s.shape[1:], lambda i, n=s.ndim: (jnp.where(i >= nctx, 1, 0),) + (0,) * (n - 1))
                  for s in segs]
    out_shape += [jax.ShapeDtypeStruct(s.shape, F32) for s in segs]
    out_specs += [pl.BlockSpec(c.shape, lambda i, n=c.ndim: (0,) * n) for c in consts]
    out_shape += [jax.ShapeDtypeStruct(c.shape, F32) for c in consts]
    res = pl.pallas_call(
        body, name=name, grid=(dm.T // tm,), in_specs=in_specs, out_specs=out_specs, out_shape=out_shape,
        compiler_params=_cparams(("arbitrary",)),
    )(*[r[0] for r in rows], *segs, *consts, *[r[0] for r in cots], *[adds[k] for k in akeys])
    rg = [None] * nr
    for k, v in zip(widx, res[:len(widx)]):
        rg[k] = v
    return rg, list(res[len(widx):len(widx) + ns]), list(res[len(widx) + ns:])


def _rms(x, g):
    return x * lax.rsqrt(jnp.mean(x * x, axis=-1, keepdims=True) + EPS) * g


def _sigmoid(x):
    return jax.nn.sigmoid(x)


def pre_fn(x, shift, scale, g):
    return ((_rms(x, g) * (1.0 + scale) + shift).astype(MM_DTYPE),)


def mid_fn(x, y, gate, shift, scale, g_post, g_pre):
    x1 = x + gate * _rms(y.astype(F32), g_post)
    return x1, (_rms(x1, g_pre) * (1.0 + scale) + shift).astype(MM_DTYPE)


def post_fn(x1, y2, gate, g):
    return (x1 + gate * _rms(y2.astype(F32), g),)


def relu2_epi(acc):
    r = jnp.maximum(acc, 0.0)
    return r * r


def relu2_bwd_epi(dact, act):
    return dact * (2.0 * jnp.sqrt(act.astype(F32)))


def decay_fn(z, bd):
    zz = z.astype(F32) + bd
    ls = jnp.minimum(zz, 0.0) - jnp.log(1.0 + jnp.exp(jnp.minimum(zz, -zz)))
    la = ls / GLA_TAU
    gk = la.shape[1] // 2
    return la[:, :gk], la[:, gk:]


def glu_fn(a, b):
    return (a.astype(F32) * _sigmoid(b.astype(F32)),)


def glaout_fn(o_f, o_b, og, g):
    o = o_f + o_b
    dv = o.shape[1] // N_HEADS
    hs = []
    for h in range(N_HEADS):
        oh = o[:, h * dv:(h + 1) * dv]
        hs.append(oh * lax.rsqrt(jnp.mean(oh * oh, axis=-1, keepdims=True) + EPS) * g[:, h * dv:(h + 1) * dv])
    og = og.astype(F32)
    return ((jnp.concatenate(hs, axis=1) * (og * _sigmoid(og))).astype(MM_DTYPE),)


def convpost_fn(y, b_dw, g, b):
    y = y + b_dw
    mu = jnp.mean(y, axis=-1, keepdims=True)
    xc = y - mu
    yn = xc * lax.rsqrt(jnp.mean(xc * xc, axis=-1, keepdims=True) + EPS) * g + b
    return ((yn * _sigmoid(yn)).astype(MM_DTYPE),)


def poolpost_fn(pc, s):
    return ((pc.astype(F32) * s).astype(MM_DTYPE),)


def merge_fn(ya, yb, yc, mg, bg0, bg1, bg2):
    d = ya.shape[1]
    mg = mg.astype(F32)
    mixed = (_sigmoid(mg[:, :d] + bg0) * ya.astype(F32) + _sigmoid(mg[:, d:2 * d] + bg1) * yb.astype(F32)
             + _sigmoid(mg[:, 2 * d:] + bg2) * yc.astype(F32))
    return (mixed.astype(MM_DTYPE),)


def _split_dot(lmat, x, dims):
    hi = x.astype(MM_DTYPE)
    lo = x - hi.astype(F32)
    return _dot(lmat, hi, dims) + _dot(lmat, lo, dims)


def _gla_block_order(dm, rev):
    nctx, nb = dm.CTX // dm.TB, dm.T // dm.TB

    def blk(i):
        if not rev:
            return i
        return jnp.where(i < nctx, nctx - 1 - i, nb - 1 - (i - nctx))
    return blk, nb


def _gla_tri(rev):
    c = GLA_CHUNK
    t = lax.broadcasted_iota(jnp.int32, (c, c), 0)
    s = lax.broadcasted_iota(jnp.int32, (c, c), 1)
    return (s >= t) if rev else (s <= t)


def _gla_chunk_terms(q, k, la, tri, scale):
    lmat = tri.astype(MM_DTYPE)
    b = _split_dot(lmat, la, NN)
    bend = jnp.sum(la, axis=0, keepdims=True)
    eb = jnp.exp(b)
    enb = jnp.exp(-b)
    ee = jnp.exp(bend - b)
    qi = q * scale * eb
    ki = k * enb
    kend = k * ee
    att = jnp.where(tri, _dot(qi, ki, NT), 0.0)
    return lmat, bend, eb, enb, ee, qi, ki, kend, att


def gla_fwd(P, la, rev, dm, name):
    c, tb, h_, dk, dv, d = GLA_CHUNK, dm.TB, N_HEADS, dm.DK, dm.DV, dm.D
    cpb = tb // c
    blk, nb = _gla_block_order(dm, rev)
    qb, kb, vb, lb = (5 * d) // dk, (5 * d + d // 2) // dk, (3 * d) // dv, (h_ if rev else 0)
    scale = dk ** -0.5
    order = list(range(cpb))[::-1] if rev else list(range(cpb))

    def body(q_ref, k_ref, v_ref, la_ref, o_ref, s_ref, st):
        @pl.when(pl.program_id(1) == 0)
        def _():
            st[...] = jnp.zeros_like(st)
        tri = _gla_tri(rev)
        for n, ci in enumerate(order):
            r = pl.ds(ci * c, c)
            q = q_ref[r, :].astype(F32)
            k = k_ref[r, :].astype(F32)
            v = v_ref[r, :]
            _, bend, _, _, _, qi, _, kend, att = _gla_chunk_terms(q, k, la_ref[r, :], tri, scale)
            s_in = st[...]
            o_ref[r, :] = _dot(att, v) + _dot(qi, s_in, NT)
            s_ref[n, 0] = s_in
            st[...] = jnp.exp(bend) * s_in + _dot(v, kend, TN)

    return pl.pallas_call(
        body, name=name, grid=(h_, nb),
        in_specs=[pl.BlockSpec((tb, dk), lambda h, i: (blk(i), qb + h)),
                  pl.BlockSpec((tb, dk), lambda h, i: (blk(i), kb + h)),
                  pl.BlockSpec((tb, dv), lambda h, i: (blk(i), vb + h)),
                  pl.BlockSpec((tb, dk), lambda h, i: (blk(i), lb + h))],
        out_specs=[pl.BlockSpec((tb, dv), lambda h, i: (blk(i), h)),
                   pl.BlockSpec((cpb, 1, dv, dk), lambda h, i: (i, h, 0, 0))],
        out_shape=[jax.ShapeDtypeStruct((dm.T, h_ * dv), F32),
                   jax.ShapeDtypeStruct((dm.T // c, h_, dv, dk), F32)],
        scratch_shapes=[pltpu.VMEM((dv, dk), F32)],
        compiler_params=_cparams(("parallel", "arbitrary")),
    )(P, P, P, la)


def gla_bwd(P, la, do, states, rev, dm, name):
    c, tb, h_, dk, dv, d = GLA_CHUNK, dm.TB, N_HEADS, dm.DK, dm.DV, dm.D
    cpb = tb // c
    blk, nb = _gla_block_order(dm, rev)
    qb, kb, vb, lb = (5 * d) // dk, (5 * d + d // 2) // dk, (3 * d) // dv, (h_ if rev else 0)
    scale = dk ** -0.5
    order = list(range(cpb))[::-1] if rev else list(range(cpb))

    def body(q_ref, k_ref, v_ref, la_ref, do_ref, s_ref, dq_ref, dk_ref, dv_ref, dla_ref, dst):
        @pl.when(pl.program_id(1) == 0)
        def _():
            dst[...] = jnp.zeros_like(dst)
        tri = _gla_tri(rev)
        for n in range(cpb - 1, -1, -1):
            r = pl.ds(order[n] * c, c)
            q = q_ref[r, :].astype(F32)
            k = k_ref[r, :].astype(F32)
            v = v_ref[r, :]
            lmat, bend, eb, enb, ee, qi, ki, kend, att = _gla_chunk_terms(q, k, la_ref[r, :], tri, scale)
            s_in = s_ref[n, 0]
            ds_out = dst[...]
            dob = do_ref[r, :]
            datt = jnp.where(tri, _dot(dob, v, NT), 0.0)
            dqi = _dot(datt, ki) + _dot(dob, s_in)
            dki = _dot(datt, qi, TN)
            dv_ref[r, :] = (_dot(att, dob, TN) + _dot(kend, ds_out, NT)).astype(dv_ref.dtype)
            dkend = _dot(v, ds_out)
            gam = jnp.exp(bend)
            dgam = jnp.sum(ds_out * s_in, axis=0, keepdims=True)
            dst[...] = gam * ds_out + _dot(dob, qi, TN)
            dq_ref[r, :] = (dqi * (scale * eb)).astype(dq_ref.dtype)
            dk_ref[r, :] = (dki * enb + dkend * ee).astype(dk_ref.dtype)
            db = dqi * qi - dki * ki - dkend * kend
            dbend = jnp.sum(dkend * kend, axis=0, keepdims=True) + dgam * gam
            dla_ref[r, :] = _split_dot(lmat, db, TN) + dbend

    def bi(j):
        return blk(nb - 1 - j)

    return pl.pallas_call(
        body, name=name, grid=(h_, nb),
        in_specs=[pl.BlockSpec((tb, dk), lambda h, j: (bi(j), qb + h)),
                  pl.BlockSpec((tb, dk), lambda h, j: (bi(j), kb + h)),
                  pl.BlockSpec((tb, dv), lambda h, j: (bi(j), vb + h)),
                  pl.BlockSpec((tb, dk), lambda h, j: (bi(j), lb + h)),
                  pl.BlockSpec((tb, dv), lambda h, j: (bi(j), h)),
                  pl.BlockSpec((cpb, 1, dv, dk), lambda h, j: (nb - 1 - j, h, 0, 0))],
        out_specs=[pl.BlockSpec((tb, dk), lambda h, j: (bi(j), h)),
                   pl.BlockSpec((tb, dk), lambda h, j: (bi(j), h)),
                   pl.BlockSpec((tb, dv), lambda h, j: (bi(j), h)),
                   pl.BlockSpec((tb, dk), lambda h, j: (bi(j), h))],
        out_shape=[jax.ShapeDtypeStruct((dm.T, h_ * dk), F32), jax.ShapeDtypeStruct((dm.T, h_ * dk), F32),
                   jax.ShapeDtypeStruct((dm.T, h_ * dv), F32), jax.ShapeDtypeStruct((dm.T, h_ * dk), F32)],
        scratch_shapes=[pltpu.VMEM((dv, dk), F32)],
        compiler_params=_cparams(("parallel", "arbitrary")),
    )(P, P, P, la, do, states)


def _pos(n, period):
    t = lax.broadcasted_iota(jnp.int32, (n, 1), 0)
    if period & (period - 1) == 0:
        return jnp.bitwise_and(t, period - 1)
    return lax.rem(t, period)


def _conv_segments(dm):
    return [(0, dm.CTX, dm.CTX), (dm.CTX, dm.SEQ, GRID_W)]


def conv_fwd(u, w, dm, name):
    kw, cw = w.shape
    segs = _conv_segments(dm)

    def body(u_ref, w_ref, y_ref):
        for r0, n, per in segs:
            useg = u_ref[r0:r0 + n, :]
            p = _pos(n, per)
            acc = jnp.zeros_like(useg)
            for kk in range(kw):
                d = kk - kw // 2
                sh = useg if d == 0 else pltpu.roll(useg, (-d) % n, 0)
                ok = jnp.logical_and(p + d >= 0, p + d < per)
                acc = acc + jnp.where(ok, sh, 0.0) * w_ref[kk:kk + 1, :]
            y_ref[r0:r0 + n, :] = acc

    return pl.pallas_call(
        body, name=name, grid=(cw // LANES,),
        in_specs=[pl.BlockSpec((dm.T, LANES), lambda j: (0, j)), pl.BlockSpec((kw, LANES), lambda j: (0, j))],
        out_specs=pl.BlockSpec((dm.T, LANES), lambda j: (0, j)),
        out_shape=jax.ShapeDtypeStruct((dm.T, cw), F32),
        compiler_params=_cparams(("parallel",)),
    )(u, w)


def conv_bwd(u, w, dy, dm, name):
    kw, cw = w.shape
    segs = _conv_segments(dm)

    def body(u_ref, w_ref, dy_ref, du_ref, dw_ref):
        dws = [jnp.zeros((1, LANES), F32)] * kw
        for r0, n, per in segs:
            useg = u_ref[r0:r0 + n, :]
            dyseg = dy_ref[r0:r0 + n, :]
            p = _pos(n, per)
            acc = jnp.zeros_like(useg)
            for kk in range(kw):
                d = kk - kw // 2
                shu = useg if d == 0 else pltpu.roll(useg, (-d) % n, 0)
                okf = jnp.logical_and(p + d >= 0, p + d < per)
                dws[kk] = dws[kk] + jnp.sum(jnp.where(okf, shu, 0.0) * dyseg, axis=0, keepdims=True)
                shd = dyseg if d == 0 else pltpu.roll(dyseg, d % n, 0)
                okb = jnp.logical_and(p - d >= 0, p - d < per)
                acc = acc + jnp.where(okb, shd, 0.0) * w_ref[kk:kk + 1, :]
            du_ref[r0:r0 + n, :] = acc
        for kk in range(kw):
            dw_ref[kk:kk + 1, :] = dws[kk]

    return pl.pallas_call(
        body, name=name, grid=(cw // LANES,),
        in_specs=[pl.BlockSpec((dm.T, LANES), lambda j: (0, j)), pl.BlockSpec((kw, LANES), lambda j: (0, j)),
                  pl.BlockSpec((dm.T, LANES), lambda j: (0, j))],
        out_specs=[pl.BlockSpec((dm.T, LANES), lambda j: (0, j)), pl.BlockSpec((kw, LANES), lambda j: (0, j))],
        out_shape=[jax.ShapeDtypeStruct((dm.T, cw), F32), jax.ShapeDtypeStruct((kw, cw), F32)],
        compiler_params=_cparams(("parallel",)),
    )(u, w, dy)


def pool_mix(u, transpose, dm, name):
    u, uw, ublk = _rowspec(u)
    gc = dm.GC
    ng = len(POOL_WINDOWS)
    rows = dm.SEQ // GRID_W
    segs = [(0, dm.CTX, 1, dm.CTX), (dm.CTX, dm.SEQ, GRID_W, rows)]

    def one_group(u_ref, o_ref, win):
        left = win // 2
        right = win - 1 - left
        for r0, n, stride, length in segs:
            useg = u_ref[r0:r0 + n, :].astype(F32)
            t = lax.broadcasted_iota(jnp.int32, (n, 1), 0)
            p = t if stride == 1 else jnp.right_shift(t, stride.bit_length() - 1)
            cnt = (jnp.minimum(p + right + 1, length) - jnp.maximum(p - left, 0)).astype(F32)
            src = useg / cnt if transpose else useg
            acc = jnp.zeros_like(useg)
            for d in range(-left, right + 1):
                dd = -d if transpose else d
                sh = src if d == 0 else pltpu.roll(src, (-dd * stride) % n, 0)
                ok = jnp.logical_and(p + dd >= 0, p + dd < length)
                acc = acc + jnp.where(ok, sh, 0.0)
            o_ref[r0:r0 + n, :] = (acc - useg) if transpose else (acc / cnt - useg)

    def body(u_ref, o_ref):
        g = pl.program_id(0)
        for gi, win in enumerate(POOL_WINDOWS):
            @pl.when(g == gi)
            def _(win=win):
                one_group(u_ref, o_ref, win)

    base = ublk * (uw // gc)
    return pl.pallas_call(
        body, name=name, grid=(ng,),
        in_specs=[pl.BlockSpec((dm.T, gc), lambda g: (0, base + g))],
        out_specs=pl.BlockSpec((dm.T, gc), lambda g: (0, g)),
        out_shape=jax.ShapeDtypeStruct((dm.T, ng * gc), F32),
        compiler_params=_cparams(("parallel",)),
    )(u)


def loss_head(x2, target, dm, name):
    tm, d = dm.tm, dm.D
    nctx = dm.CTX // tm

    def body(x_ref, t_ref, dx_ref, l_ref):
        i = pl.program_id(0)

        @pl.when(i == 0)
        def _():
            l_ref[...] = jnp.zeros_like(l_ref)

        @pl.when(i < nctx)
        def _():
            dx_ref[...] = jnp.zeros_like(dx_ref)

        @pl.when(i >= nctx)
        def _():
            e = x_ref[...] - t_ref[...]
            dx_ref[...] = e / d
            l_ref[...] += jnp.full(l_ref.shape, 0.5 * jnp.sum(jnp.mean(e * e, axis=-1)), F32)

    return pl.pallas_call(
        body, name=name, grid=(dm.T // tm,),
        in_specs=[pl.BlockSpec((tm, d), lambda i: (i, 0)),
                  pl.BlockSpec((tm, d), lambda i: (jnp.maximum(i - nctx, 0), 0))],
        out_specs=[pl.BlockSpec((tm, d), lambda i: (i, 0)), pl.BlockSpec((8, LANES), lambda i: (0, 0))],
        out_shape=[jax.ShapeDtypeStruct((dm.T, d), F32), jax.ShapeDtypeStruct((8, LANES), F32)],
        compiler_params=_cparams(("arbitrary",)),
    )(x2, target)


def adamw(w, g, m, v, name):
    r, c = w.shape
    tr = _tile(r, tuple(t for t in (512, 256, 128, 64, 32, 16, 8) if t * c * 4 <= (1 << 20)) or (8,))

    def body(w_ref, g_ref, m_ref, v_ref, d_ref, mo_ref, vo_ref):
        gg = g_ref[...]
        mm = ADAM_B1 * m_ref[...] + (1.0 - ADAM_B1) * gg
        vv = ADAM_B2 * v_ref[...] + (1.0 - ADAM_B2) * (gg * gg)
        m_hat = mm / (1.0 - ADAM_B1 ** ADAM_STEP)
        v_hat = vv / (1.0 - ADAM_B2 ** ADAM_STEP)
        d_ref[...] = -ADAM_LR * (m_hat / (jnp.sqrt(v_hat) + ADAM_EPS) + ADAM_WD * w_ref[...])
        mo_ref[...] = mm
        vo_ref[...] = vv

    spec = pl.BlockSpec((tr, c), lambda i: (i, 0))
    return pl.pallas_call(
        body, name=name, grid=(r // tr,), in_specs=[spec] * 4, out_specs=[spec] * 3,
        out_shape=[jax.ShapeDtypeStruct((r, c), F32)] * 3,
        compiler_params=_cparams(("parallel",)),
    )(w, g, m, v)


def slot_sum(buf, name):
    s, r, c = buf.shape
    tr = _tile(r, (256, 128, 64, 32, 16, 8))

    def body(b_ref, o_ref):
        acc = b_ref[0].astype(F32)
        for k in range(1, s):
            acc = acc + b_ref[k].astype(F32)
        o_ref[...] = acc

    return pl.pallas_call(
        body, name=name, grid=(r // tr,),
        in_specs=[pl.BlockSpec((s, tr, c), lambda i: (0, i, 0))],
        out_specs=pl.BlockSpec((tr, c), lambda i: (i, 0)),
        out_shape=jax.ShapeDtypeStruct((r, c), F32),
        compiler_params=_cparams(("parallel",)),
    )(buf)


def pair_add(g, r1, cidx, name):
    ng, r_, n_ = r1.shape
    tr = _tile(r_, tuple(t for t in (1024, 512, 256, 128, 64, 32, 16) if t * n_ * 4 <= (2 << 20)))

    def body(s_ref, g_ref, r_ref, o_ref):
        o_ref[...] = (g_ref[...].astype(F32) + r_ref[...].astype(F32)).astype(o_ref.dtype)

    return pl.pallas_call(
        body, name=name,
        grid_spec=pltpu.PrefetchScalarGridSpec(
            num_scalar_prefetch=1, grid=(ng, r_ // tr),
            in_specs=[pl.BlockSpec((None, tr, n_), lambda k, i, s: (2 * k + s[0], i, 0)),
                      pl.BlockSpec((None, tr, n_), lambda k, i, s: (k, i, 0))],
            out_specs=pl.BlockSpec((None, tr, n_), lambda k, i, s: (k, i, 0))),
        out_shape=jax.ShapeDtypeStruct((ng, r_, n_), g.dtype),
        compiler_params=_cparams(("parallel", "parallel")),
    )(cidx, g, r1)


def chip_add(h, r2, axis, where, name, slab=True):
    _, kl, nl = r2.shape
    tr = _tile(kl, tuple(t for t in (1024, 512, 256, 128, 64, 32, 16) if t * nl * 4 <= (1 << 20)))
    nrb = kl // tr

    def body(s_ref, h_ref, r_ref, o_ref):
        acc = h_ref[...].astype(F32)
        for k in range(r2.shape[0]):
            acc = acc + r_ref[k].astype(F32)
        o_ref[...] = acc

    h_map = (lambda i, s: (s[0] * nrb + i, 0)) if axis == 0 else (lambda i, s: (i, s[0]))
    if slab:
        out_spec = pl.BlockSpec((None, tr, nl), lambda i, s: (s[1], i, 0))
        out_shape = jax.ShapeDtypeStruct((2, kl, nl), F32)
    else:
        out_spec = pl.BlockSpec((tr, nl), lambda i, s: (i, 0))
        out_shape = jax.ShapeDtypeStruct((kl, nl), F32)
    return pl.pallas_call(
        body, name=name,
        grid_spec=pltpu.PrefetchScalarGridSpec(
            num_scalar_prefetch=1, grid=(nrb,),
            in_specs=[pl.BlockSpec((tr, nl), h_map),
                      pl.BlockSpec((r2.shape[0], tr, nl), lambda i, s: (0, i, 0))],
            out_specs=out_spec),
        out_shape=out_shape,
        compiler_params=_cparams(("parallel",)),
    )(where, h, r2)


def adamw_layers(w, m, v, terms0, terms1, name):
    _, a_, b_ = w.shape
    tr = _tile(a_, tuple(t for t in (512, 256, 128, 64, 32, 16, 8) if t * b_ * 4 <= (1 << 20)) or (8,))
    nrb = a_ // tr
    n0 = len(terms0)

    def update(g, w_ref, m_ref, v_ref, g_ref, d_ref, mo_ref, vo_ref):
        mm = ADAM_B1 * m_ref[...] + (1.0 - ADAM_B1) * g
        vv = ADAM_B2 * v_ref[...] + (1.0 - ADAM_B2) * (g * g)
        m_hat = mm / (1.0 - ADAM_B1 ** ADAM_STEP)
        v_hat = vv / (1.0 - ADAM_B2 ** ADAM_STEP)
        g_ref[...] = g
        d_ref[...] = -ADAM_LR * (m_hat / (jnp.sqrt(v_hat) + ADAM_EPS) + ADAM_WD * w_ref[...])
        mo_ref[...] = mm
        vo_ref[...] = vv

    def total(refs):
        g = refs[0][...]
        for r in refs[1:]:
            g = g + r[...]
        return g

    def body(w_ref, m_ref, v_ref, *rest):
        t_refs, outs = rest[:-4], rest[-4:]
        layer = pl.program_id(0)

        @pl.when(layer == 0)
        def _():
            update(total(t_refs[:n0]), w_ref, m_ref, v_ref, *outs)

        @pl.when(layer == 1)
        def _():
            update(total(t_refs[n0:]), w_ref, m_ref, v_ref, *outs)

    stacked = pl.BlockSpec((None, tr, b_), lambda l, i: (l, i, 0))
    return pl.pallas_call(
        body, name=name, grid=(2, nrb),
        in_specs=[stacked] * 3 + [pl.BlockSpec((tr, b_), lambda l, i: (i * (1 - l), 0))] * n0
        + [pl.BlockSpec((tr, b_), lambda l, i: (i * l, 0))] * len(terms1),
        out_specs=[stacked] * 4, out_shape=[jax.ShapeDtypeStruct(w.shape, F32)] * 4,
        compiler_params=_cparams(("arbitrary", "arbitrary")),
    )(w, m, v, *terms0, *terms1)


MESH = pl.DeviceIdType.MESH
ANY = pl.BlockSpec(memory_space=pl.ANY)
HBM = pl.BlockSpec(memory_space=pltpu.HBM)
SEM = pl.BlockSpec(memory_space=pltpu.SEMAPHORE)
EFFECT = pltpu.SideEffectType.DATAFLOW_SIDE_EFFECTING


def _place():
    return lax.axis_index("x"), lax.axis_index("y"), lax.axis_index("c")


def _peers(x, y):
    return [(1 - x, y), (x, 1 - y), (1 - x, 1 - y)]


def _rcopy(src, dst, ssem, rsem, dev):
    return pltpu.make_async_remote_copy(src_ref=src, dst_ref=dst, send_sem=ssem, recv_sem=rsem,
                                        device_id=dev, device_id_type=MESH)


def _gathered_shape(src, kind):
    h, a_, b_ = src.shape
    return (h, a_, 4 * b_) if kind == 'col' else (4, h, a_, b_)


def _win(ref, kind, ch, width):
    return ref.at[:, :, pl.ds(ch * width, width)] if kind == 'col' else ref.at[ch]


def _rect(ref, kind, half, ch, width):
    return ref.at[half, :, pl.ds(ch * width, width)] if kind == 'col' else ref.at[ch, half]


def gather_halves(srcs, kinds, name):
    nw = len(srcs)
    widths = [s.shape[2] for s in srcs]

    def body(*refs):
        src, out = refs[:nw], refs[nw:2 * nw]
        ssem, rsem, osend, orecv = refs[2 * nw:]
        x, y, c = _place()
        chip = 2 * x + y
        sib = (x, y, 1 - c)
        peers = _peers(x, y)
        pidx = [2 * px + py for px, py in peers]

        def rect(n, half, ch):
            return _rect(out[n], kinds[n], half, ch, widths[n])

        mine = [_rcopy(src[n], _win(out[n], kinds[n], chip, widths[n]), osend.at[n], orecv.at[n], sib)
                for n in range(nw)]
        first = [[_rcopy(src[n].at[c], rect(n, c, chip), ssem.at[6 * n + k], rsem.at[6 * n + k], (px, py, c))
                  for k, (px, py) in enumerate(peers)] for n in range(nw)]
        for n in range(nw):
            for cp in first[n]:
                cp.start()
        for cp in mine:
            cp.start()
        passed = [[_rcopy(rect(n, c, pidx[k]), rect(n, c, pidx[k]), ssem.at[6 * n + 3 + k], rsem.at[6 * n + 3 + k], sib)
                   for k in range(3)] for n in range(nw)]
        for n in range(nw):
            for k, (px, py) in enumerate(peers):
                _rcopy(rect(n, c, pidx[k]), rect(n, c, pidx[k]), ssem.at[6 * n + k], rsem.at[6 * n + k],
                       (px, py, c)).wait_recv()
                passed[n][k].start()
        for n in range(nw):
            for k in range(3):
                _rcopy(rect(n, 1 - c, pidx[k]), rect(n, 1 - c, pidx[k]), ssem.at[6 * n + 3 + k],
                       rsem.at[6 * n + 3 + k], sib).wait_recv()
        for n in range(nw):
            for cp in first[n] + passed[n]:
                cp.wait_send()
        for cp in mine:
            cp.wait()

    return pl.pallas_call(
        body, name=name, in_specs=[ANY] * nw, out_specs=[ANY] * nw,
        out_shape=[jax.ShapeDtypeStruct(_gathered_shape(s, k), s.dtype) for s, k in zip(srcs, kinds)],
        scratch_shapes=[pltpu.SemaphoreType.DMA((6 * nw,)), pltpu.SemaphoreType.DMA((6 * nw,)),
                        pltpu.SemaphoreType.DMA((nw,)), pltpu.SemaphoreType.DMA((nw,))],
    )(*srcs)


def _gather_plan(kinds, widths):
    def plan(src, land, x, y, c):
        chip = 2 * x + y
        out = []
        for n in range(len(src)):
            mine = _win(land[n], kinds[n], chip, widths[n])
            for px, py in _peers(x, y):
                out.append((src[n], mine, (px, py, c), _win(land[n], kinds[n], 2 * px + py, widths[n])))
            out.append((src[n], mine, (x, y, 1 - c), mine))
        return out
    return plan


def _scatter_plan(axes, widths):
    def plan(src, land, x, y, c):
        out = []
        for n in range(len(src)):
            for k, (px, py) in enumerate(_peers(x, y)):
                ch = 2 * px + py
                view = (src[n].at[:, pl.ds(ch * widths[n], widths[n])] if axes[n] == 1
                        else src[n].at[pl.ds(ch * widths[n], widths[n]), :])
                out.append((view, land[n].at[k], (px, py, c), land[n].at[k]))
        return out
    return plan


def start_copies(srcs, lands, plan, ncopies, after, name):
    ns, nl = len(srcs), len(lands)

    def body(*refs):
        src, land = refs[:ns], refs[ns:ns + nl]
        ssem, rsem = refs[ns + nl + 1], refs[ns + nl + 2]
        token = refs[-1]
        x, y, c = _place()
        for k, (sv, dv, dev, _) in enumerate(plan(src, land, x, y, c)):
            _rcopy(sv, dv, ssem.at[k], rsem.at[k], dev).start()
        token[...] = jnp.zeros_like(token)

    hbm = lambda t: pltpu.HBM(t.shape, t.dtype)
    res = pl.pallas_call(
        body, name=name,
        out_shape=(pltpu.SemaphoreType.DMA((ncopies,)), pltpu.SemaphoreType.DMA((ncopies,)),
                   *[hbm(t) for t in srcs], *[hbm(t) for t in lands], jax.ShapeDtypeStruct((8, LANES), F32)),
        in_specs=[HBM] * (ns + nl) + [ANY],
        out_specs=(SEM, SEM, *[HBM] * (ns + nl), pl.BlockSpec(memory_space=pltpu.VMEM)),
        input_output_aliases={k: 2 + k for k in range(ns + nl)},
        compiler_params=pltpu.CompilerParams(has_side_effects=EFFECT),
    )(*[pltpu.with_memory_space_constraint(t, pltpu.HBM) for t in list(srcs) + list(lands)], after)
    return res[0], res[1], list(res[2:2 + ns]), list(res[2 + ns:2 + ns + nl]), res[-1]


def wait_copies(ssem, rsem, srcs, lands, plan, after, name):
    ns, nl = len(srcs), len(lands)

    def body(*refs):
        src, land = refs[:ns], refs[ns:ns + nl]
        ss, rs = refs[ns + nl], refs[ns + nl + 1]
        x, y, c = _place()
        for k, (sv, dv, dev, mine) in enumerate(plan(src, land, x, y, c)):
            cp = _rcopy(sv, mine, ss.at[k], rs.at[k], dev)
            cp.wait_send()
            cp.wait_recv()

    hbm = lambda t: pltpu.HBM(t.shape, t.dtype)
    res = pl.pallas_call(
        body, name=name,
        out_shape=(*[hbm(t) for t in srcs], *[hbm(t) for t in lands]),
        in_specs=[HBM] * (ns + nl) + [SEM, SEM, ANY], out_specs=tuple([HBM] * (ns + nl)),
        input_output_aliases={k: k for k in range(ns + nl)},
        compiler_params=pltpu.CompilerParams(has_side_effects=EFFECT),
    )(*srcs, *lands, ssem, rsem, after)
    return list(res[ns:])


def pair_swap_halves(gs, kinds, name):
    nw = len(gs)

    def other(ref, kind, half):
        return ref.at[half] if kind == 'col' else ref.at[:, half]

    def body(*refs):
        g, o = refs[:nw], refs[nw:2 * nw]
        ssem, rsem = refs[2 * nw:]
        x, y, c = _place()
        cps = [_rcopy(other(g[n], kinds[n], 1 - c), o[n], ssem.at[n], rsem.at[n], (x, y, 1 - c)) for n in range(nw)]
        for cp in cps:
            cp.start()
        for cp in cps:
            cp.wait()

    return pl.pallas_call(
        body, name=name, in_specs=[ANY] * nw, out_specs=[ANY] * nw,
        out_shape=[jax.ShapeDtypeStruct(g.shape[1:] if k == 'col' else (g.shape[0],) + g.shape[2:], g.dtype)
                   for g, k in zip(gs, kinds)],
        scratch_shapes=[pltpu.SemaphoreType.DMA((nw,)), pltpu.SemaphoreType.DMA((nw,))],
    )(*gs)


def pair_swap(fs, name):
    nw = len(fs)

    def body(*refs):
        f, o = refs[:nw], refs[nw:2 * nw]
        ssem, rsem = refs[2 * nw:]
        x, y, c = _place()
        cps = [_rcopy(f[n], o[n], ssem.at[n], rsem.at[n], (x, y, 1 - c)) for n in range(nw)]
        for cp in cps:
            cp.start()
        for cp in cps:
            cp.wait()

    return pl.pallas_call(
        body, name=name, in_specs=[ANY] * nw, out_specs=[ANY] * nw,
        out_shape=[jax.ShapeDtypeStruct(f.shape, f.dtype) for f in fs],
        scratch_shapes=[pltpu.SemaphoreType.DMA((nw,)), pltpu.SemaphoreType.DMA((nw,))],
    )(*fs)


def chip_exchange(hs, kinds, name):
    nw = len(hs)
    shp = [(h.shape[0], h.shape[1] // 4) if k == 'col' else h.shape[1:] for h, k in zip(hs, kinds)]

    def body(*refs):
        h, o = refs[:nw], refs[nw:2 * nw]
        ssem, rsem = refs[2 * nw:]
        x, y, c = _place()

        def win(n, ch):
            return h[n].at[:, pl.ds(ch * shp[n][1], shp[n][1])] if kinds[n] == 'col' else h[n].at[ch]

        cps = [_rcopy(win(n, 2 * px + py), o[n].at[k], ssem.at[3 * n + k], rsem.at[3 * n + k], (px, py, c))
               for n in range(nw) for k, (px, py) in enumerate(_peers(x, y))]
        for cp in cps:
            cp.start()
        for cp in cps:
            cp.wait()

    return pl.pallas_call(
        body, name=name, in_specs=[ANY] * nw, out_specs=[ANY] * nw,
        out_shape=[jax.ShapeDtypeStruct((3,) + sh, h.dtype) for sh, h in zip(shp, hs)],
        scratch_shapes=[pltpu.SemaphoreType.DMA((3 * nw,)), pltpu.SemaphoreType.DMA((3 * nw,))],
    )(*hs)


def pair_join_layers(fs, name):
    nw = len(fs)

    def body(*refs):
        o = refs[nw:2 * nw]
        ssem, rsem = refs[2 * nw:]
        x, y, c = _place()
        sib = (x, y, 1 - c)
        cps = [_rcopy(o[n].at[c], o[n].at[c], ssem.at[n], rsem.at[n], sib) for n in range(nw)]
        for cp in cps:
            cp.start()
        for n in range(nw):
            cps[n].wait_send()
            _rcopy(o[n].at[1 - c], o[n].at[1 - c], ssem.at[n], rsem.at[n], sib).wait_recv()

    return pl.pallas_call(
        body, name=name, in_specs=[ANY] * nw, out_specs=[ANY] * nw,
        out_shape=[jax.ShapeDtypeStruct(f.shape, f.dtype) for f in fs],
        input_output_aliases={n: n for n in range(nw)},
        scratch_shapes=[pltpu.SemaphoreType.DMA((nw,)), pltpu.SemaphoreType.DMA((nw,))],
    )(*fs)


def gather_all_devices(buf, name):
    r, c_ = buf.shape
    offs = [o for o in itertools.product((0, 1), repeat=3) if o != (0, 0, 0)]

    def body(b_ref, o_ref, ssem, rsem, lsem):
        x, y, c = _place()
        me = 4 * x + 2 * y + c
        mine = pltpu.make_async_copy(b_ref, o_ref.at[me], lsem)
        mine.start()
        peers = [((x + dx) % 2, (y + dy) % 2, (c + dc) % 2) for dx, dy, dc in offs]
        cps = [_rcopy(b_ref, o_ref.at[me], ssem.at[k], rsem.at[k], p) for k, p in enumerate(peers)]
        for cp in cps:
            cp.start()
        for k, (px, py, pc) in enumerate(peers):
            _rcopy(b_ref, o_ref.at[4 * px + 2 * py + pc], ssem.at[k], rsem.at[k], (px, py, pc)).wait_recv()
        for cp in cps:
            cp.wait_send()
        mine.wait()

    return pl.pallas_call(
        body, name=name, in_specs=[ANY], out_specs=ANY,
        out_shape=jax.ShapeDtypeStruct((8, r, c_), buf.dtype),
        scratch_shapes=[pltpu.SemaphoreType.DMA((7,)), pltpu.SemaphoreType.DMA((7,)), pltpu.SemaphoreType.DMA],
    )(buf)


def _flatten_pad(parts, dtype):
    flat = jnp.concatenate([p.reshape(-1).astype(dtype) for p in parts])
    q = 512 * LANES
    n = -(-flat.shape[0] // q) * q
    return jnp.pad(flat, (0, n - flat.shape[0])).reshape(n // LANES, LANES)


def _lane_pad(n):
    return -(-n // LANES) * LANES


def _in_proj_layout(d):
    gk, gv, cw, pw = d // 2, d, d // 2, d // 2
    own = [('q', gk), ('k', gk), ('v', gv), ('og', gv), ('lrf', GLA_LR), ('lrb', GLA_LR), ('ga', cw), ('gb', cw),
           ('pu', pw), ('mg', 3 * d)]
    padded = [('mg', 3 * d), ('v', gv), ('og', gv), ('q', gk), ('k', gk), ('ga', cw), ('gb', cw), ('pu', pw),
              ('lrf', GLA_LR), ('lrb', GLA_LR), ('pad', d // 2 - 2 * GLA_LR)]
    return own, padded


def _pad_w_in(w, d):
    own, padded = _in_proj_layout(d)
    cols, start = {}, 0
    for n, wd in own:
        cols[n] = w[:, start:start + wd]
        start += wd
    return jnp.concatenate([cols[n] if n != 'pad' else jnp.zeros((w.shape[0], wd), w.dtype) for n, wd in padded], axis=1)


def _unpad_w_in(wp, d):
    own, padded = _in_proj_layout(d)
    cols, start = {}, 0
    for n, wd in padded:
        cols[n] = wp[:, start:start + wd]
        start += wd
    return jnp.concatenate([cols[n] for n, _ in own], axis=1)


def _silu_grad(z):
    s = jax.nn.sigmoid(z)
    return s + z * s * (1.0 - s)


def kernel(x, c, ctx, c_ctx, w_ada, b_ada, g_pre_mix, g_post_mix, g_pre_mlp, g_post_mlp, w_in, w_decay, b_decay, g_gla, w_gla_o, w_dw, b_dw, g_conv_ln, b_conv_ln, w_conv_o, w_pool_g, s_pool, w_pool_o, b_gate, w_out, w_mlp1, w_mlp2, loss_target, m_c_ctx, m_w_ada, m_b_ada, m_g_pre_mix, m_g_post_mix, m_g_pre_mlp, m_g_post_mlp, m_w_in, m_w_decay, m_b_decay, m_g_gla, m_w_gla_o, m_w_dw, m_b_dw, m_g_conv_ln, m_b_conv_ln, m_w_conv_o, m_w_pool_g, m_s_pool, m_w_pool_o, m_b_gate, m_w_out, m_w_mlp1, m_w_mlp2, v_c_ctx, v_w_ada, v_b_ada, v_g_pre_mix, v_g_post_mix, v_g_pre_mlp, v_g_post_mlp, v_w_in, v_w_decay, v_b_decay, v_g_gla, v_w_gla_o, v_w_dw, v_b_dw, v_g_conv_ln, v_b_conv_ln, v_w_conv_o, v_w_pool_g, v_s_pool, v_w_pool_o, v_b_gate, v_w_out, v_w_mlp1, v_w_mlp2):
    a = dict(locals())
    depth = w_in.shape[0]
    d = x.shape[-1]
    seq, nctx_rows = x.shape[1], ctx.shape[1]
    dm = types.SimpleNamespace(
        D=d, SEQ=seq, CTX=nctx_rows, T=seq + nctx_rows, DK=d // 8, DV=d // 4, GK=d // 2, GC=d // 8,
        tm=_tile(nctx_rows, (256, 128, 64)), TB=_tile(nctx_rows, (256, 128, 64)))
    assert dm.SEQ % dm.tm == 0 and dm.SEQ % GRID_W == 0 and dm.CTX % GLA_CHUNK == 0
    tmw = min(dm.tm, 128)
    chip = 2 * lax.axis_index("x") + lax.axis_index("y")
    core = lax.axis_index("c")
    chip1 = chip.astype(jnp.int32).reshape(1)
    core1 = core.astype(jnp.int32).reshape(1)

    big_names, small_names = list(BIG), list(SMALL_SHARDED)
    nbig = len(big_names)
    kinds = ['col' if BIG[n] == 2 else 'row' for n in big_names]
    wl = w_in.shape[2]
    wlp = _lane_pad(wl)

    def rows8(t):
        t = t.reshape(t.shape[0], -1, t.shape[-1])
        return jnp.pad(t, ((0, 0), (0, -t.shape[1] % 8), (0, 0)))

    def halves(t):
        return t.reshape(2, t.shape[0] // 2, t.shape[1])

    def layer_src(l):
        return [halves((jnp.pad(a[n][l], ((0, 0), (0, wlp - wl))) if n == 'w_in' else a[n][l]).astype(MM_DTYPE))
                for n in big_names]

    def whole(t):
        return t.reshape(-1, t.shape[-1])

    late = [big_names.index(n) for n in ('w_mlp1', 'w_mlp2')]
    early = [k for k in range(nbig) if k not in late]
    src0, src1 = layer_src(0), layer_src(1)
    g0 = gather_halves([src0[k] for k in early] + [rows8(a[n]) for n in small_names],
                       [kinds[k] for k in early] + ['col'] * len(small_names), "gather_layer0")

    def start_gather(srcs, knds, after, name):
        plan = _gather_plan(knds, [t.shape[2] for t in srcs])
        lands = [lax.empty(_gathered_shape(t, k), t.dtype) for t, k in zip(srcs, knds)]
        return (plan,) + start_copies(srcs, lands, plan, 4 * len(srcs), after, name)

    ag0 = start_gather([src0[k] for k in late], [kinds[k] for k in late], g0[0], "gather_layer0_mlp_start")
    ag1 = start_gather(src1, kinds, ag0[-1], "gather_layer1_start")
    ag_token = ag1[-1]
    full = {n: [None, None] for n in big_names}
    for k, t in zip(early, g0):
        full[big_names[k]][0] = whole(t)
    for n, g in zip(small_names, g0[len(early):]):
        shp = a[n].shape
        full[n] = g[:, :math.prod(shp[1:-1])].reshape(shp[:-1] + (4 * shp[-1],))
    for n in SMALL:
        if n not in SMALL_SHARDED:
            full[n] = a[n]

    cvec = jnp.concatenate([c_ctx.reshape(1, d), c.reshape(1, d), jnp.zeros((6, d), F32)], axis=0)
    avec = (cvec * jax.nn.sigmoid(cvec) + ag_token[0, 0]).astype(MM_DTYPE)

    def row(v):
        return v.reshape(1, -1)

    X = jnp.concatenate([ctx[0], x[0]], axis=0)
    saved = []
    gk, gv = dm.GK, d
    lrblk = (7 * d + d // 2) // LANES
    for l in range(depth):
        if l == 1:
            got = wait_copies(ag1[1], ag1[2], ag1[3], ag1[4], ag1[0], X, "gather_layer1_wait")
            for n, t in zip(big_names, got):
                full[n][1] = whole(t)
        s = types.SimpleNamespace()
        s.w_in_p = _pad_w_in(full['w_in'][l].reshape(d, 4, wlp)[:, :, :wl].reshape(d, 4 * wl), d)
        wd = full['w_decay'][l]
        wdp = jnp.zeros((LANES, 2 * gk), F32)
        wdp = wdp.at[:GLA_LR, :gk].set(wd[0]).at[GLA_LR:2 * GLA_LR, gk:].set(wd[1])
        s.wdp = wdp.astype(MM_DTYPE)
        s.bd = full['b_decay'][l].reshape(1, 2 * gk)
        modraw = matmul(avec, full['w_ada'][l], 'nn', F32, f"mod_{l}") + full['b_ada'][l][None, :]
        s.mod = [modraw[0:2, j * d:(j + 1) * d].reshape(2, 1, d) for j in range(6)]
        s.x = X
        (s.h,) = rowwise(pre_fn, [X], s.mod[0:2], [row(g_pre_mix[l])], [(d, MM_DTYPE)], dm, f"pre_{l}")
        s.P = matmul(s.h, s.w_in_p, 'nn', MM_DTYPE, f"in_proj_{l}")
        P = s.P
        s.z = matmul((P, LANES, lrblk), s.wdp, 'nn', F32, f"decay_proj_{l}", tk=LANES)
        la_f, la_b = rowwise(decay_fn, [s.z], [], [s.bd], [(gk, F32), (gk, F32)], dm, f"decay_{l}")
        s.la = jnp.concatenate([la_f, la_b], axis=1)
        s.o_f, s.st_f = gla_fwd(P, s.la, False, dm, f"gla_fwd_f_{l}")
        s.o_b, s.st_b = gla_fwd(P, s.la, True, dm, f"gla_fwd_b_{l}")
        (s.gin,) = rowwise(glaout_fn, [s.o_f, s.o_b, (P, d, 4)], [], [row(g_gla[l])], [(gv, MM_DTYPE)], dm,
                           f"gla_out_{l}")
        s.ya = matmul(s.gin, full['w_gla_o'][l], 'nn', F32, f"gla_o_{l}")
        (s.u,) = rowwise(glu_fn, [(P, d // 2, 12), (P, d // 2, 13)], [], [], [(d // 2, F32)], dm, f"glu_{l}")
        s.yconv = conv_fwd(s.u, full['w_dw'][l], dm, f"conv_{l}")
        (s.cin,) = rowwise(convpost_fn, [s.yconv], [], [row(b_dw[l]), row(g_conv_ln[l]), row(b_conv_ln[l])],
                           [(d // 2, MM_DTYPE)], dm, f"conv_post_{l}")
        s.yb = matmul(s.cin, full['w_conv_o'][l], 'nn', F32, f"conv_o_{l}")
        s.pm = pool_mix((P, d // 2, 14), False, dm, f"pool_mix_{l}")
        s.pc = group_mm(s.pm, w_pool_g[l], 'nn', F32, f"pool_g_{l}")
        (s.pin,) = rowwise(poolpost_fn, [s.pc], [], [row(s_pool[l])], [(d // 2, MM_DTYPE)], dm, f"pool_post_{l}")
        s.yc = matmul(s.pin, full['w_pool_o'][l], 'nn', F32, f"pool_o_{l}")
        s.bg = [row(full['b_gate'][l][j]) for j in range(3)]
        (s.mixed,) = rowwise(merge_fn, [s.ya, s.yb, s.yc, (P, 3 * d, 0)], [], s.bg, [(d, MM_DTYPE)], dm,
                             f"merge_{l}", tm=tmw)
        s.y = matmul(s.mixed, full['w_out'][l], 'nn', F32, f"out_proj_{l}")
        if l == 0:
            got = wait_copies(ag0[1], ag0[2], ag0[3], ag0[4], ag0[0], s.y, "gather_layer0_mlp_wait")
            for k, t in zip(late, got):
                full[big_names[k]][0] = whole(t)
        s.x1, s.h2 = rowwise(mid_fn, [X, s.y], s.mod[2:5], [row(g_post_mix[l]), row(g_pre_mlp[l])],
                             [(d, F32), (d, MM_DTYPE)], dm, f"mid_{l}")
        s.act = matmul(s.h2, full['w_mlp1'][l], 'nn', MM_DTYPE, f"mlp1_{l}", epi=relu2_epi)
        s.y2 = matmul(s.act, full['w_mlp2'][l], 'nn', F32, f"mlp2_{l}")
        (X,) = rowwise(post_fn, [s.x1, s.y2], s.mod[5:6], [row(g_post_mlp[l])], [(d, F32)], dm, f"post_{l}")
        saved.append(s)

    dX, lossv = loss_head(X, loss_target[0], dm, "loss_head")
    loss = lax.psum(lossv[0, 0], ("x", "y", "c"))

    grads = {n: [None] * depth for n in WEIGHTS if n != 'c_ctx' and n not in BIG}
    gbig = {n: [None] * depth for n in BIG}
    rs_token = None

    def start_scatter(idx, layer, after, name):
        gs = [gbig[big_names[k]][layer] for k in idx]
        wd = [t.shape[1] // 4 if kinds[k] == 'col' else t.shape[0] // 4 for t, k in zip(gs, idx)]
        plan = _scatter_plan([BIG[big_names[k]] - 1 for k in idx], wd)
        lands = [lax.empty((3, t.shape[0], w) if kinds[k] == 'col' else (3, w, t.shape[1]), t.dtype)
                 for t, w, k in zip(gs, wd, idx)]
        return (plan,) + start_copies(gs, lands, plan, 3 * len(gs), after, name)

    g_cctx = jnp.zeros((d,), F32)
    for l in reversed(range(depth)):
        s = saved[l]
        P = s.P
        dmod = [None] * 6
        gpm = row(g_post_mlp[l]) if rs_token is None else row(g_post_mlp[l]) + rs_token[0, 0]
        (dx1, dy2), (dmod[5],), (dg,) = rowwise_vjp(post_fn, [s.x1, s.y2], s.mod[5:6], [gpm], [dX],
                                                     dm, f"post_bwd_{l}", narrow=(1,))
        grads['g_post_mlp'][l] = dg[0]
        du1 = matmul(dy2, full['w_mlp2'][l], 'nt', MM_DTYPE, f"mlp2_dx_{l}", epi=relu2_bwd_epi, extras=[s.act])
        gbig['w_mlp2'][l] = matmul(s.act, dy2, 'tn', MM_DTYPE, f"mlp2_dw_{l}")
        dh2 = matmul(du1, full['w_mlp1'][l], 'nt', MM_DTYPE, f"mlp1_dx_{l}")
        gbig['w_mlp1'][l] = matmul(s.h2, du1, 'tn', MM_DTYPE, f"mlp1_dw_{l}")
        gpx = row(g_post_mix[l])
        if l == 0:
            rs0 = start_scatter(late, 0, dh2, "grad_layer0_mlp_start")
            gpx = gpx + rs0[-1][0, 0]
        (dxa, dy), dmod[2:5], (dg1, dg2) = rowwise_vjp(
            mid_fn, [s.x, s.y], s.mod[2:5], [gpx, row(g_pre_mlp[l])], [dx1, dh2], dm, f"mid_bwd_{l}", narrow=(1,))
        grads['g_post_mix'][l], grads['g_pre_mlp'][l] = dg1[0], dg2[0]
        dmixed = matmul(dy, full['w_out'][l], 'nt', MM_DTYPE, f"out_proj_dx_{l}")
        gbig['w_out'][l] = matmul(s.mixed, dy, 'tn', MM_DTYPE, f"out_proj_dw_{l}")
        (dya, dyb, dyc, dmg), _, dbg = rowwise_vjp(merge_fn, [s.ya, s.yb, s.yc, (P, 3 * d, 0)], [], s.bg, [dmixed],
                                                   dm, f"merge_bwd_{l}", tm=tmw, narrow=(0, 1, 2))
        grads['b_gate'][l] = jnp.concatenate(dbg, axis=0)
        dgin = matmul(dya, full['w_gla_o'][l], 'nt', MM_DTYPE, f"gla_o_dx_{l}")
        gbig['w_gla_o'][l] = matmul(s.gin, dya, 'tn', MM_DTYPE, f"gla_o_dw_{l}")
        dcin = matmul(dyb, full['w_conv_o'][l], 'nt', MM_DTYPE, f"conv_o_dx_{l}")
        gbig['w_conv_o'][l] = matmul(s.cin, dyb, 'tn', MM_DTYPE, f"conv_o_dw_{l}")
        dpin = matmul(dyc, full['w_pool_o'][l], 'nt', MM_DTYPE, f"pool_o_dx_{l}")
        gbig['w_pool_o'][l] = matmul(s.pin, dyc, 'tn', MM_DTYPE, f"pool_o_dw_{l}")
        (dpc,), _, (dsp,) = rowwise_vjp(poolpost_fn, [s.pc], [], [row(s_pool[l])], [dpin], dm, f"pool_post_bwd_{l}")
        grads['s_pool'][l] = dsp[0]
        grads['w_pool_g'][l] = group_mm(s.pm, w_pool_g[l], 'tn', F32, f"pool_g_dw_{l}", b=dpc)
        dpm = group_mm(dpc, w_pool_g[l], 'nt', F32, f"pool_g_dx_{l}")
        dpu = pool_mix(dpm, True, dm, f"pool_mix_bwd_{l}")
        (dyconv,), _, (dbdw, dgln, dbln) = rowwise_vjp(
            convpost_fn, [s.yconv], [], [row(b_dw[l]), row(g_conv_ln[l]), row(b_conv_ln[l])], [dcin], dm,
            f"conv_post_bwd_{l}")
        grads['b_dw'][l], grads['g_conv_ln'][l], grads['b_conv_ln'][l] = dbdw[0], dgln[0], dbln[0]
        du, grads['w_dw'][l] = conv_bwd(s.u, full['w_dw'][l], dyconv, dm, f"conv_bwd_{l}")
        (dga, dgb), _, _ = rowwise_vjp(glu_fn, [(P, d // 2, 12), (P, d // 2, 13)], [], [], [du], dm, f"glu_bwd_{l}")
        (do, _, dog), _, (dgg,) = rowwise_vjp(glaout_fn, [s.o_f, s.o_b, (P, d, 4)], [], [row(g_gla[l])], [dgin], dm,
                                              f"gla_out_bwd_{l}", want=[True, False, True])
        grads['g_gla'][l] = dgg[0]
        dqf, dkf, dvf, dlaf = gla_bwd(P, s.la, do, s.st_f, False, dm, f"gla_bwd_f_{l}")
        dqb, dkb, dvb, dlab = gla_bwd(P, s.la, do, s.st_b, True, dm, f"gla_bwd_b_{l}")
        (dz,), _, (dbd,) = rowwise_vjp(decay_fn, [s.z], [], [s.bd], [dlaf, dlab], dm, f"decay_bwd_{l}", narrow=(0,))
        grads['b_decay'][l] = dbd.reshape(2, gk)
        dwdp = matmul((P, LANES, lrblk), dz, 'tn', F32, f"decay_proj_dw_{l}", tm=LANES)
        grads['w_decay'][l] = jnp.stack([dwdp[:GLA_LR, :gk], dwdp[GLA_LR:2 * GLA_LR, gk:]])
        dlr = matmul(dz, s.wdp, 'nt', F32, f"decay_proj_dx_{l}")

        def asm_fn(dmg_, dvf_, dvb_, dog_, dqf_, dqb_, dkf_, dkb_, dga_, dgb_, dpu_, dlr_):
            f = lambda t: t.astype(F32)
            pad = jnp.zeros((dlr_.shape[0], d // 2 - LANES), F32)
            return (jnp.concatenate([f(dmg_), dvf_ + dvb_, f(dog_), dqf_ + dqb_, dkf_ + dkb_, f(dga_), f(dgb_),
                                     dpu_, dlr_, pad], axis=1).astype(MM_DTYPE),)
        (dP,) = rowwise(asm_fn, [dmg, dvf, dvb, dog, dqf, dqb, dkf, dkb, dga, dgb, dpu, dlr], [], [],
                        [(8 * d, MM_DTYPE)], dm, f"dproj_{l}", tm=tmw)
        dh = matmul(dP, s.w_in_p, 'nt', MM_DTYPE, f"in_proj_dx_{l}")
        gwin = _unpad_w_in(matmul(s.h, dP, 'tn', MM_DTYPE, f"in_proj_dw_{l}"), d)
        gbig['w_in'][l] = jnp.pad(gwin.reshape(d, 4, wl), ((0, 0), (0, 0), (0, wlp - wl))).reshape(d, 4 * wlp)
        (dX,), dmod[0:2], (dg,) = rowwise_vjp(pre_fn, [s.x], s.mod[0:2], [row(g_pre_mix[l])], [dh], dm,
                                               f"pre_bwd_{l}", adds={0: dxa})
        grads['g_pre_mix'][l] = dg[0]
        dmodflat = jnp.concatenate([jnp.concatenate([m_.reshape(2, d) for m_ in dmod], axis=1),
                                    jnp.zeros((6, 6 * d), F32)], axis=0)
        grads['b_ada'][l] = dmodflat[0] + dmodflat[1]
        gbig['w_ada'][l] = matmul(avec, dmodflat, 'tn', MM_DTYPE, f"ada_dw_{l}")
        dav = matmul(dmodflat, full['w_ada'][l], 'nt', F32, f"ada_dx_{l}")
        g_cctx = g_cctx + dav[0] * _silu_grad(c_ctx)
        if l == 1:
            rs1 = start_scatter(list(range(nbig)), 1, dav, "grad_layer1_start")
            rs_token = rs1[-1]

    grad_x = dX[dm.CTX:][None]
    gfull = {n: jnp.stack(v) for n, v in grads.items()}
    gfull['c_ctx'] = g_cctx
    where = jnp.concatenate([chip1, core1])

    def halves_view(t, k):
        return t.reshape(2, t.shape[0] // 2, t.shape[1]) if k == 'col' else t.reshape(4, 2, t.shape[0] // 8, t.shape[1])
    enames = [big_names[k] for k in early]
    ekinds = [kinds[k] for k in early]
    v0 = [halves_view(gbig[n][0], k) for n, k in zip(enames, ekinds)]
    r1 = pair_swap_halves(v0, ekinds, "grad_pair_swap")
    hs = [pair_add(v.reshape((-1,) + v.shape[-2:]), r.reshape((-1,) + r.shape[-2:]), core1, f"grad_pair_add_{n}")
          for n, v, r in zip(enames, v0, r1)]
    hx = [h.reshape(h.shape[1:]) if k == 'col' else h for h, k in zip(hs, ekinds)]
    r2 = chip_exchange(hx, ekinds, "grad_chip_exchange")
    fs = [chip_add(h.reshape(-1, h.shape[-1]), r, BIG[n] - 1, where, f"grad_chip_add_{n}")
          for n, h, r in zip(enames, hs, r2)]
    red0 = dict(zip(enames, [[t.reshape(-1, t.shape[-1])] for t in pair_join_layers(fs, "grad_pair_join")]))

    got0 = wait_copies(rs0[1], rs0[2], rs0[3], rs0[4], rs0[0], dX, "grad_layer0_mlp_wait")
    got1 = wait_copies(rs1[1], rs1[2], rs1[3], rs1[4], rs1[0], dX, "grad_layer1_wait")
    sa = [chip_add(g, r, BIG[big_names[k]] - 1, where, f"grad_layer0_add_{big_names[k]}", slab=False)
          for k, g, r in zip(late, rs0[3], got0)]
    sa += [chip_add(g, r, BIG[n] - 1, where, f"grad_layer1_add_{n}", slab=False)
           for n, g, r in zip(big_names, rs1[3], got1)]
    wi = len(late) + big_names.index('w_in')
    red0['w_in'], sa[wi] = [red0['w_in'][0][:, :wl]], sa[wi][:, :wl]
    sb = pair_swap(sa, "grad_late_pair_swap")
    for j, k in enumerate(late):
        red0[big_names[k]] = [sa[j], sb[j]]
    red1 = {n: [sa[len(late) + k], sb[len(late) + k]] for k, n in enumerate(big_names)}

    sflat = _flatten_pad([gfull[n].astype(F32) for n in SMALL], F32)
    ssum = slot_sum(gather_all_devices(sflat, "small_grad_gather"), "small_grad_sum").reshape(-1)

    out_g, out_d, out_m, out_v = {}, {}, {}, {}
    for k, n in enumerate(big_names):
        out_g[n], out_d[n], out_m[n], out_v[n] = adamw_layers(a[n], a['m_' + n], a['v_' + n], red0[n], red1[n],
                                                              f"adamw_{n}")
    start = 0
    sg = {}
    for n in SMALL:
        cnt = gfull[n].size
        g = ssum[start:start + cnt].reshape(gfull[n].shape)
        start += cnt
        if n in SMALL_SHARDED:
            ax = SMALL_SHARDED[n]
            wdt = a[n].shape[ax]
            g = lax.dynamic_slice_in_dim(g, chip * wdt, wdt, axis=ax)
        sg[n] = g
    pk = lambda dct, pre: _flatten_pad([dct[pre + n] for n in SMALL], F32)
    gs = _flatten_pad([sg[n] for n in SMALL], F32)
    dl, mn, vn = adamw(pk(a, ''), gs, pk(a, 'm_'), pk(a, 'v_'), "adamw_small")
    dl, mn, vn = dl.reshape(-1), mn.reshape(-1), vn.reshape(-1)
    start = 0
    for n in SMALL:
        cnt, shp = a[n].size, a[n].shape
        out_g[n] = sg[n]
        out_d[n], out_m[n], out_v[n] = (t[start:start + cnt].reshape(shp) for t in (dl, mn, vn))
        start += cnt

    return (loss, grad_x, *[out_g[n] for n in WEIGHTS], *[out_d[n] for n in WEIGHTS],
            *[out_m[n] for n in WEIGHTS], *[out_v[n] for n in WEIGHTS])
```

```python
import functools
import itertools
import math
import types

import jax
import jax.numpy as jnp
from jax import lax
from jax.experimental import pallas as pl
from jax.experimental.pallas import tpu as pltpu

F32 = jnp.float32
MM_DTYPE = jnp.bfloat16
VMEM_LIMIT_V7X = 56 * 1024 * 1024
LANES = 128
EPS = 1e-6

N_HEADS = 4
GLA_CHUNK = 64
GLA_TAU = 16.0
GLA_LR = 16
GRID_W = 64
POOL_WINDOWS = (2, 4, 8, 16)

ADAM_LR = 0.001
ADAM_B1 = 0.9
ADAM_B2 = 0.999
ADAM_EPS = 1e-08
ADAM_WD = 0.01
ADAM_STEP = 10

NN = (((1,), (0,)), ((), ()))
NT = (((1,), (1,)), ((), ()))
TN = (((0,), (0,)), ((), ()))

WEIGHTS = ['c_ctx', 'w_ada', 'b_ada', 'g_pre_mix', 'g_post_mix', 'g_pre_mlp', 'g_post_mlp', 'w_in', 'w_decay',
           'b_decay', 'g_gla', 'w_gla_o', 'w_dw', 'b_dw', 'g_conv_ln', 'b_conv_ln', 'w_conv_o', 'w_pool_g',
           's_pool', 'w_pool_o', 'b_gate', 'w_out', 'w_mlp1', 'w_mlp2']
BIG = {'w_ada': 2, 'w_in': 2, 'w_gla_o': 1, 'w_conv_o': 2, 'w_pool_o': 2, 'w_out': 1, 'w_mlp1': 2, 'w_mlp2': 1}
SMALL_SHARDED = {'w_decay': 3, 'b_decay': 2, 'w_dw': 2, 'b_gate': 2}
SMALL = [n for n in WEIGHTS if n not in BIG]


def _tile(n, prefs):
    for t in prefs:
        if n % t == 0:
            return t
    return n


def _cparams(sem=None, **kw):
    return pltpu.CompilerParams(dimension_semantics=sem, vmem_limit_bytes=VMEM_LIMIT_V7X, **kw)


def _dot(a, b, dims=NN):
    return lax.dot_general(a.astype(MM_DTYPE), b.astype(MM_DTYPE), dims, preferred_element_type=F32)


def matmul(a, b, mode, out_dtype, name, tm=None, tn=None, tk=None, epi=None, extras=()):
    a, aw, ablk = a if isinstance(a, tuple) else (a, a.shape[1], 0)
    if mode == 'nn':
        M, K, N = a.shape[0], aw, b.shape[1]
    elif mode == 'nt':
        M, K, N = a.shape[0], aw, b.shape[0]
    else:
        K, M, N = a.shape[0], aw, b.shape[1]
    big = (1088, 1024, 640, 544, 512, 320, 256, 128, 64, 32, 16, 8)
    if mode == 'tn':
        tm = tm or _tile(M, (1024, 512, 256, 128))
        tn = tn or _tile(N, (1024, 512, 256, 128))
        tk = tk or _tile(K, big)
    else:
        tm = tm or _tile(M, big)
        tn = tn or _tile(N, (1024, 512, 256, 128))
        tk = tk or _tile(K, (1024, 512, 256, 128))
    if aw != a.shape[1]:
        assert (mode == 'tn' and tm == aw) or (mode != 'tn' and tk == aw)
    nk = K // tk
    ne = len(extras)
    dims = {'nn': NN, 'nt': NT, 'tn': TN}[mode]

    def body(a_ref, b_ref, *rest):
        e_refs, o_ref = rest[:ne], rest[ne]

        def finish(acc):
            if epi is not None:
                acc = epi(acc, *[e[...] for e in e_refs])
            o_ref[...] = acc.astype(o_ref.dtype)

        p = _dot(a_ref[...], b_ref[...], dims)
        if nk == 1:
            finish(p)
            return
        acc = rest[-1]
        k = pl.program_id(2)

        @pl.when(k == 0)
        def _():
            acc[...] = p

        @pl.when(k > 0)
        def _():
            acc[...] += p

        @pl.when(k == nk - 1)
        def _():
            finish(acc[...])

    if mode == 'nn':
        a_spec = pl.BlockSpec((tm, tk), lambda i, j, k: (i, k + ablk))
        b_spec = pl.BlockSpec((tk, tn), lambda i, j, k: (k, j))
    elif mode == 'nt':
        a_spec = pl.BlockSpec((tm, tk), lambda i, j, k: (i, k + ablk))
        b_spec = pl.BlockSpec((tn, tk), lambda i, j, k: (j, k))
    else:
        a_spec = pl.BlockSpec((tk, tm), lambda i, j, k: (k, i + ablk))
        b_spec = pl.BlockSpec((tk, tn), lambda i, j, k: (k, j))
    tile = pl.BlockSpec((tm, tn), lambda i, j, k: (i, j))
    return pl.pallas_call(
        body, name=name, grid=(M // tm, N // tn, nk),
        in_specs=[a_spec, b_spec] + [tile] * ne, out_specs=tile,
        out_shape=jax.ShapeDtypeStruct((M, N), out_dtype),
        scratch_shapes=[] if nk == 1 else [pltpu.VMEM((tm, tn), F32)],
        compiler_params=_cparams(("parallel", "parallel", "arbitrary")),
    )(a, b, *extras)


def group_mm(a, w, mode, out_dtype, name, b=None):
    T = a.shape[0]
    G, gc, _ = w.shape
    col = pl.BlockSpec((T, gc), lambda g: (0, g))
    wsp = pl.BlockSpec((1, gc, gc), lambda g: (g, 0, 0))
    if mode == 'tn':
        def body(a_ref, b_ref, o_ref):
            o_ref[0] = _dot(a_ref[...], b_ref[...], TN).astype(o_ref.dtype)
        return pl.pallas_call(body, name=name, grid=(G,), in_specs=[col, col], out_specs=wsp,
                              out_shape=jax.ShapeDtypeStruct((G, gc, gc), out_dtype),
                              compiler_params=_cparams(("parallel",)))(a, b)
    dims = NN if mode == 'nn' else NT

    def body(a_ref, w_ref, o_ref):
        o_ref[...] = _dot(a_ref[...], w_ref[0], dims).astype(o_ref.dtype)
    return pl.pallas_call(body, name=name, grid=(G,), in_specs=[col, wsp], out_specs=col,
                          out_shape=jax.ShapeDtypeStruct((T, G * gc), out_dtype),
                          compiler_params=_cparams(("parallel",)))(a, w)


def _rowspec(r):
    return r if isinstance(r, tuple) else (r, r.shape[1], 0)


def _row_specs(rows, segs, consts, tm, nctx):
    specs = [pl.BlockSpec((tm, w), lambda i, b=b: (i, b)) for _, w, b in rows]
    specs += [pl.BlockSpec((1,) + s.shape[1:], lambda i, n=s.ndim: (jnp.where(i >= nctx, 1, 0),) + (0,) * (n - 1))
              for s in segs]
    specs += [pl.BlockSpec(c.shape, lambda i, n=c.ndim: (0,) * n) for c in consts]
    return specs


def rowwise(fn, rows, segs, consts, outs, dm, name, tm=None):
    tm = tm or dm.tm
    nctx = dm.CTX // tm
    rows = [_rowspec(r) for r in rows]
    nr, ns, nc = len(rows), len(segs), len(consts)

    def body(*refs):
        rin = [r[...] for r in refs[:nr]]
        sin = [s[0] for s in refs[nr:nr + ns]]
        cin = [c[...] for c in refs[nr + ns:nr + ns + nc]]
        res = fn(*rin, *sin, *cin)
        for o_ref, v in zip(refs[nr + ns + nc:], res):
            o_ref[...] = v.astype(o_ref.dtype)

    res = pl.pallas_call(
        body, name=name, grid=(dm.T // tm,),
        in_specs=_row_specs(rows, segs, consts, tm, nctx),
        out_specs=[pl.BlockSpec((tm, w), lambda i: (i, 0)) for w, _ in outs],
        out_shape=[jax.ShapeDtypeStruct((dm.T, w), dt) for w, dt in outs],
        compiler_params=_cparams(("parallel",)),
    )(*[r[0] for r in rows], *segs, *consts)
    return res


def rowwise_vjp(fn, rows, segs, consts, cots, dm, name, tm=None, want=None, adds=None, narrow=()):
    tm = tm or dm.tm
    nctx = dm.CTX // tm
    rows = [_rowspec(r) for r in rows]
    cots = [_rowspec(r) for r in cots]
    adds = adds or {}
    nr, ns, nc, nct = len(rows), len(segs), len(consts), len(cots)
    want = want or [True] * nr
    widx = [k for k in range(nr) if want[k]]
    akeys = sorted(adds)

    def body(*refs):
        i = pl.program_id(0)
        rin = [r[...] for r in refs[:nr]]
        sin = [s[0] for s in refs[nr:nr + ns]]
        cin = [c[...] for c in refs[nr + ns:nr + ns + nc]]
        p = nr + ns + nc
        cot_refs = refs[p:p + nct]
        add_refs = dict(zip(akeys, refs[p + nct:p + nct + len(akeys)]))
        p = p + nct + len(akeys)
        rg_refs = refs[p:p + len(widx)]
        sg_refs = refs[p + len(widx):p + len(widx) + ns]
        cg_refs = refs[p + len(widx) + ns:]
        res, vjp = jax.vjp(fn, *rin, *sin, *cin)
        g = vjp(tuple(cr[...].astype(o.dtype) for cr, o in zip(cot_refs, res)))
        for o_ref, k in zip(rg_refs, widx):
            v = g[k].astype(F32)
            if k in add_refs:
                v = v + add_refs[k][...]
            o_ref[...] = v.astype(o_ref.dtype)
        first_seg = jnp.logical_or(i == 0, i == nctx)
        for o_ref, v in zip(sg_refs, g[nr:nr + ns]):
            @pl.when(first_seg)
            def _(o_ref=o_ref, v=v):
                o_ref[0] = v.astype(F32)

            @pl.when(jnp.logical_not(first_seg))
            def _(o_ref=o_ref, v=v):
                o_ref[0] += v.astype(F32)
        for o_ref, v in zip(cg_refs, g[nr + ns:]):
            @pl.when(i == 0)
            def _(o_ref=o_ref, v=v):
                o_ref[...] = v.astype(F32)

            @pl.when(i > 0)
            def _(o_ref=o_ref, v=v):
                o_ref[...] += v.astype(F32)

    in_specs = _row_specs(rows, segs, consts, tm, nctx)
    in_specs += [pl.BlockSpec((tm, w), lambda i, b=b: (i, b)) for _, w, b in cots]
    in_specs += [pl.BlockSpec((tm, adds[k].shape[1]), lambda i: (i, 0)) for k in akeys]
    out_specs = [pl.BlockSpec((tm, rows[k][1]), lambda i: (i, 0)) for k in widx]
    out_shape = [jax.ShapeDtypeStruct((dm.T, rows[k][1]), MM_DTYPE if k in narrow else rows[k][0].dtype)
                 for k in widx]
    out_specs += [pl.BlockSpec((1,) + s.shape[1:], lambda i, n=s.ndim: (jnp.where(i >= nctx, 1, 0),) + (0,) * (n - 1))
                  for s in segs]
    out_shape += [jax.ShapeDtypeStruct(s.shape, F32) for s in segs]
    out_specs += [pl.BlockSpec(c.shape, lambda i, n=c.ndim: (0,) * n) for c in consts]
    out_shape += [jax.ShapeDtypeStruct(c.shape, F32) for c in consts]
    res = pl.pallas_call(
        body, name=name, grid=(dm.T // tm,), in_specs=in_specs, out_specs=out_specs, out_shape=out_shape,
        compiler_params=_cparams(("arbitrary",)),
    )(*[r[0] for r in rows], *segs, *consts, *[r[0] for r in cots], *[adds[k] for k in akeys])
    rg = [None] * nr
    for k, v in zip(widx, res[:len(widx)]):
        rg[k] = v
    return rg, list(res[len(widx):len(widx) + ns]), list(res[len(widx) + ns:])


def _rms(x, g):
    return x * lax.rsqrt(jnp.mean(x * x, axis=-1, keepdims=True) + EPS) * g


def _sigmoid(x):
    return jax.nn.sigmoid(x)


def pre_fn(x, shift, scale, g):
    return ((_rms(x, g) * (1.0 + scale) + shift).astype(MM_DTYPE),)


def mid_fn(x, y, gate, shift, scale, g_post, g_pre):
    x1 = x + gate * _rms(y.astype(F32), g_post)
    return x1, (_rms(x1, g_pre) * (1.0 + scale) + shift).astype(MM_DTYPE)


def post_fn(x1, y2, gate, g):
    return (x1 + gate * _rms(y2.astype(F32), g),)


def relu2_epi(acc):
    r = jnp.maximum(acc, 0.0)
    return r * r


def relu2_bwd_epi(dact, act):
    return dact * (2.0 * jnp.sqrt(act.astype(F32)))


def decay_fn(z, bd):
    zz = z.astype(F32) + bd
    ls = jnp.minimum(zz, 0.0) - jnp.log(1.0 + jnp.exp(jnp.minimum(zz, -zz)))
    la = ls / GLA_TAU
    gk = la.shape[1] // 2
    return la[:, :gk], la[:, gk:]


def glu_fn(a, b):
    return (a.astype(F32) * _sigmoid(b.astype(F32)),)


def glaout_fn(o_f, o_b, og, g):
    o = o_f + o_b
    dv = o.shape[1] // N_HEADS
    hs = []
    for h in range(N_HEADS):
        oh = o[:, h * dv:(h + 1) * dv]
        hs.append(oh * lax.rsqrt(jnp.mean(oh * oh, axis=-1, keepdims=True) + EPS) * g[:, h * dv:(h + 1) * dv])
    og = og.astype(F32)
    return ((jnp.concatenate(hs, axis=1) * (og * _sigmoid(og))).astype(MM_DTYPE),)


def convpost_fn(y, b_dw, g, b):
    y = y + b_dw
    mu = jnp.mean(y, axis=-1, keepdims=True)
    xc = y - mu
    yn = xc * lax.rsqrt(jnp.mean(xc * xc, axis=-1, keepdims=True) + EPS) * g + b
    return ((yn * _sigmoid(yn)).astype(MM_DTYPE),)


def poolpost_fn(pc, s):
    return ((pc.astype(F32) * s).astype(MM_DTYPE),)


def merge_fn(ya, yb, yc, mg, bg0, bg1, bg2):
    d = ya.shape[1]
    mg = mg.astype(F32)
    mixed = (_sigmoid(mg[:, :d] + bg0) * ya.astype(F32) + _sigmoid(mg[:, d:2 * d] + bg1) * yb.astype(F32)
             + _sigmoid(mg[:, 2 * d:] + bg2) * yc.astype(F32))
    return (mixed.astype(MM_DTYPE),)


def _split_dot(lmat, x, dims):
    hi = x.astype(MM_DTYPE)
    lo = x - hi.astype(F32)
    return _dot(lmat, hi, dims) + _dot(lmat, lo, dims)


def _gla_block_order(dm, rev):
    nctx, nb = dm.CTX // dm.TB, dm.T // dm.TB

    def blk(i):
        if not rev:
            return i
        return jnp.where(i < nctx, nctx - 1 - i, nb - 1 - (i - nctx))
    return blk, nb


def _gla_tri(rev):
    c = GLA_CHUNK
    t = lax.broadcasted_iota(jnp.int32, (c, c), 0)
    s = lax.broadcasted_iota(jnp.int32, (c, c), 1)
    return (s >= t) if rev else (s <= t)


def _gla_chunk_terms(q, k, la, tri, scale):
    lmat = tri.astype(MM_DTYPE)
    b = _split_dot(lmat, la, NN)
    bend = jnp.sum(la, axis=0, keepdims=True)
    eb = jnp.exp(b)
    enb = jnp.exp(-b)
    ee = jnp.exp(bend - b)
    qi = q * scale * eb
    ki = k * enb
    kend = k * ee
    att = jnp.where(tri, _dot(qi, ki, NT), 0.0)
    return lmat, bend, eb, enb, ee, qi, ki, kend, att


def gla_fwd(P, la, rev, dm, name):
    c, tb, h_, dk, dv, d = GLA_CHUNK, dm.TB, N_HEADS, dm.DK, dm.DV, dm.D
    cpb = tb // c
    blk, nb = _gla_block_order(dm, rev)
    qb, kb, vb, lb = (5 * d) // dk, (5 * d + d // 2) // dk, (3 * d) // dv, (h_ if rev else 0)
    scale = dk ** -0.5
    order = list(range(cpb))[::-1] if rev else list(range(cpb))

    def body(q_ref, k_ref, v_ref, la_ref, o_ref, s_ref, st):
        @pl.when(pl.program_id(1) == 0)
        def _():
            st[...] = jnp.zeros_like(st)
        tri = _gla_tri(rev)
        for n, ci in enumerate(order):
            r = pl.ds(ci * c, c)
            q = q_ref[r, :].astype(F32)
            k = k_ref[r, :].astype(F32)
            v = v_ref[r, :]
            _, bend, _, _, _, qi, _, kend, att = _gla_chunk_terms(q, k, la_ref[r, :], tri, scale)
            s_in = st[...]
            o_ref[r, :] = _dot(att, v) + _dot(qi, s_in, NT)
            s_ref[n, 0] = s_in
            st[...] = jnp.exp(bend) * s_in + _dot(v, kend, TN)

    return pl.pallas_call(
        body, name=name, grid=(h_, nb),
        in_specs=[pl.BlockSpec((tb, dk), lambda h, i: (blk(i), qb + h)),
                  pl.BlockSpec((tb, dk), lambda h, i: (blk(i), kb + h)),
                  pl.BlockSpec((tb, dv), lambda h, i: (blk(i), vb + h)),
                  pl.BlockSpec((tb, dk), lambda h, i: (blk(i), lb + h))],
        out_specs=[pl.BlockSpec((tb, dv), lambda h, i: (blk(i), h)),
                   pl.BlockSpec((cpb, 1, dv, dk), lambda h, i: (i, h, 0, 0))],
        out_shape=[jax.ShapeDtypeStruct((dm.T, h_ * dv), F32),
                   jax.ShapeDtypeStruct((dm.T // c, h_, dv, dk), F32)],
        scratch_shapes=[pltpu.VMEM((dv, dk), F32)],
        compiler_params=_cparams(("parallel", "arbitrary")),
    )(P, P, P, la)


def gla_bwd(P, la, do, states, rev, dm, name):
    c, tb, h_, dk, dv, d = GLA_CHUNK, dm.TB, N_HEADS, dm.DK, dm.DV, dm.D
    cpb = tb // c
    blk, nb = _gla_block_order(dm, rev)
    qb, kb, vb, lb = (5 * d) // dk, (5 * d + d // 2) // dk, (3 * d) // dv, (h_ if rev else 0)
    scale = dk ** -0.5
    order = list(range(cpb))[::-1] if rev else list(range(cpb))

    def body(q_ref, k_ref, v_ref, la_ref, do_ref, s_ref, dq_ref, dk_ref, dv_ref, dla_ref, dst):
        @pl.when(pl.program_id(1) == 0)
        def _():
            dst[...] = jnp.zeros_like(dst)
        tri = _gla_tri(rev)
        for n in range(cpb - 1, -1, -1):
            r = pl.ds(order[n] * c, c)
            q = q_ref[r, :].astype(F32)
            k = k_ref[r, :].astype(F32)
            v = v_ref[r, :]
            lmat, bend, eb, enb, ee, qi, ki, kend, att = _gla_chunk_terms(q, k, la_ref[r, :], tri, scale)
            s_in = s_ref[n, 0]
            ds_out = dst[...]
            dob = do_ref[r, :]
            datt = jnp.where(tri, _dot(dob, v, NT), 0.0)
            dqi = _dot(datt, ki) + _dot(dob, s_in)
            dki = _dot(datt, qi, TN)
            dv_ref[r, :] = (_dot(att, dob, TN) + _dot(kend, ds_out, NT)).astype(dv_ref.dtype)
            dkend = _dot(v, ds_out)
            gam = jnp.exp(bend)
            dgam = jnp.sum(ds_out * s_in, axis=0, keepdims=True)
            dst[...] = gam * ds_out + _dot(dob, qi, TN)
            dq_ref[r, :] = (dqi * (scale * eb)).astype(dq_ref.dtype)
            dk_ref[r, :] = (dki * enb + dkend * ee).astype(dk_ref.dtype)
            db = dqi * qi - dki * ki - dkend * kend
            dbend = jnp.sum(dkend * kend, axis=0, keepdims=True) + dgam * gam
            dla_ref[r, :] = _split_dot(lmat, db, TN) + dbend

    def bi(j):
        return blk(nb - 1 - j)

    return pl.pallas_call(
        body, name=name, grid=(h_, nb),
        in_specs=[pl.BlockSpec((tb, dk), lambda h, j: (bi(j), qb + h)),
                  pl.BlockSpec((tb, dk), lambda h, j: (bi(j), kb + h)),
                  pl.BlockSpec((tb, dv), lambda h, j: (bi(j), vb + h)),
                  pl.BlockSpec((tb, dk), lambda h, j: (bi(j), lb + h)),
                  pl.BlockSpec((tb, dv), lambda h, j: (bi(j), h)),
                  pl.BlockSpec((cpb, 1, dv, dk), lambda h, j: (nb - 1 - j, h, 0, 0))],
        out_specs=[pl.BlockSpec((tb, dk), lambda h, j: (bi(j), h)),
                   pl.BlockSpec((tb, dk), lambda h, j: (bi(j), h)),
                   pl.BlockSpec((tb, dv), lambda h, j: (bi(j), h)),
                   pl.BlockSpec((tb, dk), lambda h, j: (bi(j), h))],
        out_shape=[jax.ShapeDtypeStruct((dm.T, h_ * dk), F32), jax.ShapeDtypeStruct((dm.T, h_ * dk), F32),
                   jax.ShapeDtypeStruct((dm.T, h_ * dv), F32), jax.ShapeDtypeStruct((dm.T, h_ * dk), F32)],
        scratch_shapes=[pltpu.VMEM((dv, dk), F32)],
        compiler_params=_cparams(("parallel", "arbitrary")),
    )(P, P, P, la, do, states)


def _pos(n, period):
    t = lax.broadcasted_iota(jnp.int32, (n, 1), 0)
    if period & (period - 1) == 0:
        return jnp.bitwise_and(t, period - 1)
    return lax.rem(t, period)


def _conv_segments(dm):
    return [(0, dm.CTX, dm.CTX), (dm.CTX, dm.SEQ, GRID_W)]


def conv_fwd(u, w, dm, name):
    kw, cw = w.shape
    segs = _conv_segments(dm)

    def body(u_ref, w_ref, y_ref):
        for r0, n, per in segs:
            useg = u_ref[r0:r0 + n, :]
            p = _pos(n, per)
            acc = jnp.zeros_like(useg)
            for kk in range(kw):
                d = kk - kw // 2
                sh = useg if d == 0 else pltpu.roll(useg, (-d) % n, 0)
                ok = jnp.logical_and(p + d >= 0, p + d < per)
                acc = acc + jnp.where(ok, sh, 0.0) * w_ref[kk:kk + 1, :]
            y_ref[r0:r0 + n, :] = acc

    return pl.pallas_call(
        body, name=name, grid=(cw // LANES,),
        in_specs=[pl.BlockSpec((dm.T, LANES), lambda j: (0, j)), pl.BlockSpec((kw, LANES), lambda j: (0, j))],
        out_specs=pl.BlockSpec((dm.T, LANES), lambda j: (0, j)),
        out_shape=jax.ShapeDtypeStruct((dm.T, cw), F32),
        compiler_params=_cparams(("parallel",)),
    )(u, w)


def conv_bwd(u, w, dy, dm, name):
    kw, cw = w.shape
    segs = _conv_segments(dm)

    def body(u_ref, w_ref, dy_ref, du_ref, dw_ref):
        dws = [jnp.zeros((1, LANES), F32)] * kw
        for r0, n, per in segs:
            useg = u_ref[r0:r0 + n, :]
            dyseg = dy_ref[r0:r0 + n, :]
            p = _pos(n, per)
            acc = jnp.zeros_like(useg)
            for kk in range(kw):
                d = kk - kw // 2
                shu = useg if d == 0 else pltpu.roll(useg, (-d) % n, 0)
                okf = jnp.logical_and(p + d >= 0, p + d < per)
                dws[kk] = dws[kk] + jnp.sum(jnp.where(okf, shu, 0.0) * dyseg, axis=0, keepdims=True)
                shd = dyseg if d == 0 else pltpu.roll(dyseg, d % n, 0)
                okb = jnp.logical_and(p - d >= 0, p - d < per)
                acc = acc + jnp.where(okb, shd, 0.0) * w_ref[kk:kk + 1, :]
            du_ref[r0:r0 + n, :] = acc
        for kk in range(kw):
            dw_ref[kk:kk + 1, :] = dws[kk]

    return pl.pallas_call(
        body, name=name, grid=(cw // LANES,),
        in_specs=[pl.BlockSpec((dm.T, LANES), lambda j: (0, j)), pl.BlockSpec((kw, LANES), lambda j: (0, j)),
                  pl.BlockSpec((dm.T, LANES), lambda j: (0, j))],
        out_specs=[pl.BlockSpec((dm.T, LANES), lambda j: (0, j)), pl.BlockSpec((kw, LANES), lambda j: (0, j))],
        out_shape=[jax.ShapeDtypeStruct((dm.T, cw), F32), jax.ShapeDtypeStruct((kw, cw), F32)],
        compiler_params=_cparams(("parallel",)),
    )(u, w, dy)


def pool_mix(u, transpose, dm, name):
    u, uw, ublk = _rowspec(u)
    gc = dm.GC
    ng = len(POOL_WINDOWS)
    rows = dm.SEQ // GRID_W
    segs = [(0, dm.CTX, 1, dm.CTX), (dm.CTX, dm.SEQ, GRID_W, rows)]

    def one_group(u_ref, o_ref, win):
        left = win // 2
        right = win - 1 - left
        for r0, n, stride, length in segs:
            useg = u_ref[r0:r0 + n, :].astype(F32)
            t = lax.broadcasted_iota(jnp.int32, (n, 1), 0)
            p = t if stride == 1 else jnp.right_shift(t, stride.bit_length() - 1)
            cnt = (jnp.minimum(p + right + 1, length) - jnp.maximum(p - left, 0)).astype(F32)
            src = useg / cnt if transpose else useg
            acc = jnp.zeros_like(useg)
            for d in range(-left, right + 1):
                dd = -d if transpose else d
                sh = src if d == 0 else pltpu.roll(src, (-dd * stride) % n, 0)
                ok = jnp.logical_and(p + dd >= 0, p + dd < length)
                acc = acc + jnp.where(ok, sh, 0.0)
            o_ref[r0:r0 + n, :] = (acc - useg) if transpose else (acc / cnt - useg)

    def body(u_ref, o_ref):
        g = pl.program_id(0)
        for gi, win in enumerate(POOL_WINDOWS):
            @pl.when(g == gi)
            def _(win=win):
                one_group(u_ref, o_ref, win)

    base = ublk * (uw // gc)
    return pl.pallas_call(
        body, name=name, grid=(ng,),
        in_specs=[pl.BlockSpec((dm.T, gc), lambda g: (0, base + g))],
        out_specs=pl.BlockSpec((dm.T, gc), lambda g: (0, g)),
        out_shape=jax.ShapeDtypeStruct((dm.T, ng * gc), F32),
        compiler_params=_cparams(("parallel",)),
    )(u)


def loss_head(x2, target, dm, name):
    tm, d = dm.tm, dm.D
    nctx = dm.CTX // tm

    def body(x_ref, t_ref, dx_ref, l_ref):
        i = pl.program_id(0)

        @pl.when(i == 0)
        def _():
            l_ref[...] = jnp.zeros_like(l_ref)

        @pl.when(i < nctx)
        def _():
            dx_ref[...] = jnp.zeros_like(dx_ref)

        @pl.when(i >= nctx)
        def _():
            e = x_ref[...] - t_ref[...]
            dx_ref[...] = e / d
            l_ref[...] += jnp.full(l_ref.shape, 0.5 * jnp.sum(jnp.mean(e * e, axis=-1)), F32)

    return pl.pallas_call(
        body, name=name, grid=(dm.T // tm,),
        in_specs=[pl.BlockSpec((tm, d), lambda i: (i, 0)),
                  pl.BlockSpec((tm, d), lambda i: (jnp.maximum(i - nctx, 0), 0))],
        out_specs=[pl.BlockSpec((tm, d), lambda i: (i, 0)), pl.BlockSpec((8, LANES), lambda i: (0, 0))],
        out_shape=[jax.ShapeDtypeStruct((dm.T, d), F32), jax.ShapeDtypeStruct((8, LANES), F32)],
        compiler_params=_cparams(("arbitrary",)),
    )(x2, target)


def adamw(w, g, m, v, name):
    r, c = w.shape
    tr = _tile(r, tuple(t for t in (512, 256, 128, 64, 32, 16, 8) if t * c * 4 <= (1 << 20)) or (8,))

    def body(w_ref, g_ref, m_ref, v_ref, d_ref, mo_ref, vo_ref):
        gg = g_ref[...]
        mm = ADAM_B1 * m_ref[...] + (1.0 - ADAM_B1) * gg
        vv = ADAM_B2 * v_ref[...] + (1.0 - ADAM_B2) * (gg * gg)
        m_hat = mm / (1.0 - ADAM_B1 ** ADAM_STEP)
        v_hat = vv / (1.0 - ADAM_B2 ** ADAM_STEP)
        d_ref[...] = -ADAM_LR * (m_hat / (jnp.sqrt(v_hat) + ADAM_EPS) + ADAM_WD * w_ref[...])
        mo_ref[...] = mm
        vo_ref[...] = vv

    spec = pl.BlockSpec((tr, c), lambda i: (i, 0))
    return pl.pallas_call(
        body, name=name, grid=(r // tr,), in_specs=[spec] * 4, out_specs=[spec] * 3,
        out_shape=[jax.ShapeDtypeStruct((r, c), F32)] * 3,
        compiler_params=_cparams(("parallel",)),
    )(w, g, m, v)


def slot_sum(buf, name):
    s, r, c = buf.shape
    tr = _tile(r, (256, 128, 64, 32, 16, 8))

    def body(b_ref, o_ref):
        acc = b_ref[0].astype(F32)
        for k in range(1, s):
            acc = acc + b_ref[k].astype(F32)
        o_ref[...] = acc

    return pl.pallas_call(
        body, name=name, grid=(r // tr,),
        in_specs=[pl.BlockSpec((s, tr, c), lambda i: (0, i, 0))],
        out_specs=pl.BlockSpec((tr, c), lambda i: (i, 0)),
        out_shape=jax.ShapeDtypeStruct((r, c), F32),
        compiler_params=_cparams(("parallel",)),
    )(buf)


def pair_add(g, r1, cidx, name):
    ng, r_, n_ = r1.shape
    tr = _tile(r_, tuple(t for t in (1024, 512, 256, 128, 64, 32, 16) if t * n_ * 4 <= (2 << 20)))

    def body(s_ref, g_ref, r_ref, o_ref):
        o_ref[...] = (g_ref[...].astype(F32) + r_ref[...].astype(F32)).astype(o_ref.dtype)

    return pl.pallas_call(
        body, name=name,
        grid_spec=pltpu.PrefetchScalarGridSpec(
            num_scalar_prefetch=1, grid=(ng, r_ // tr),
            in_specs=[pl.BlockSpec((None, tr, n_), lambda k, i, s: (2 * k + s[0], i, 0)),
                      pl.BlockSpec((None, tr, n_), lambda k, i, s: (k, i, 0))],
            out_specs=pl.BlockSpec((None, tr, n_), lambda k, i, s: (k, i, 0))),
        out_shape=jax.ShapeDtypeStruct((ng, r_, n_), g.dtype),
        compiler_params=_cparams(("parallel", "parallel")),
    )(cidx, g, r1)


def chip_add(h, r2, axis, where, name, slab=True):
    _, kl, nl = r2.shape
    tr = _tile(kl, tuple(t for t in (1024, 512, 256, 128, 64, 32, 16) if t * nl * 4 <= (1 << 20)))
    nrb = kl // tr

    def body(s_ref, h_ref, r_ref, o_ref):
        acc = h_ref[...].astype(F32)
        for k in range(r2.shape[0]):
            acc = acc + r_ref[k].astype(F32)
        o_ref[...] = acc

    h_map = (lambda i, s: (s[0] * nrb + i, 0)) if axis == 0 else (lambda i, s: (i, s[0]))
    if slab:
        out_spec = pl.BlockSpec((None, tr, nl), lambda i, s: (s[1], i, 0))
        out_shape = jax.ShapeDtypeStruct((2, kl, nl), F32)
    else:
        out_spec = pl.BlockSpec((tr, nl), lambda i, s: (i, 0))
        out_shape = jax.ShapeDtypeStruct((kl, nl), F32)
    return pl.pallas_call(
        body, name=name,
        grid_spec=pltpu.PrefetchScalarGridSpec(
            num_scalar_prefetch=1, grid=(nrb,),
            in_specs=[pl.BlockSpec((tr, nl), h_map),
                      pl.BlockSpec((r2.shape[0], tr, nl), lambda i, s: (0, i, 0))],
            out_specs=out_spec),
        out_shape=out_shape,
        compiler_params=_cparams(("parallel",)),
    )(where, h, r2)


def adamw_layers(w, m, v, terms0, terms1, name):
    _, a_, b_ = w.shape
    tr = _tile(a_, tuple(t for t in (512, 256, 128, 64, 32) if t * b_ * 4 <= (1 << 20)))
    by_cols = tr == a_ and a_ * b_ * 4 > (1 << 20)
    blk = (a_, LANES) if by_cols else (tr, b_)
    steps = b_ // LANES if by_cols else a_ // tr
    at = (lambda i: (0, i)) if by_cols else (lambda i: (i, 0))
    n0 = len(terms0)

    def update(g, w_ref, m_ref, v_ref, g_ref, d_ref, mo_ref, vo_ref):
        mm = ADAM_B1 * m_ref[...] + (1.0 - ADAM_B1) * g
        vv = ADAM_B2 * v_ref[...] + (1.0 - ADAM_B2) * (g * g)
        m_hat = mm / (1.0 - ADAM_B1 ** ADAM_STEP)
        v_hat = vv / (1.0 - ADAM_B2 ** ADAM_STEP)
        g_ref[...] = g
        d_ref[...] = -ADAM_LR * (m_hat / (jnp.sqrt(v_hat) + ADAM_EPS) + ADAM_WD * w_ref[...])
        mo_ref[...] = mm
        vo_ref[...] = vv

    def total(refs):
        g = refs[0][...]
        for r in refs[1:]:
            g = g + r[...]
        return g

    def body(w_ref, m_ref, v_ref, *rest):
        t_refs, outs = rest[:-4], rest[-4:]
        layer = pl.program_id(0)

        @pl.when(layer == 0)
        def _():
            update(total(t_refs[:n0]), w_ref, m_ref, v_ref, *outs)

        @pl.when(layer == 1)
        def _():
            update(total(t_refs[n0:]), w_ref, m_ref, v_ref, *outs)

    stacked = pl.BlockSpec((None,) + blk, lambda l, i: (l,) + at(i))
    return pl.pallas_call(
        body, name=name, grid=(2, steps),
        in_specs=[stacked] * 3 + [pl.BlockSpec(blk, lambda l, i: at(i * (1 - l)))] * n0
        + [pl.BlockSpec(blk, lambda l, i: at(i * l))] * len(terms1),
        out_specs=[stacked] * 4, out_shape=[jax.ShapeDtypeStruct(w.shape, F32)] * 4,
        compiler_params=_cparams(("arbitrary", "arbitrary")),
    )(w, m, v, *terms0, *terms1)


MESH = pl.DeviceIdType.MESH
ANY = pl.BlockSpec(memory_space=pl.ANY)
HBM = pl.BlockSpec(memory_space=pltpu.HBM)
SEM = pl.BlockSpec(memory_space=pltpu.SEMAPHORE)
EFFECT = pltpu.SideEffectType.DATAFLOW_SIDE_EFFECTING


def _place():
    return lax.axis_index("x"), lax.axis_index("y"), lax.axis_index("c")


def _peers(x, y):
    return [(1 - x, y), (x, 1 - y), (1 - x, 1 - y)]


def _rcopy(src, dst, ssem, rsem, dev):
    return pltpu.make_async_remote_copy(src_ref=src, dst_ref=dst, send_sem=ssem, recv_sem=rsem,
                                        device_id=dev, device_id_type=MESH)


def _gathered_shape(src, kind):
    h, a_, b_ = src.shape
    return (h, a_, 4 * b_) if kind == 'col' else (4, h, a_, b_)


def _win(ref, kind, ch, width):
    return ref.at[:, :, pl.ds(ch * width, width)] if kind == 'col' else ref.at[ch]


def _rect(ref, kind, half, ch, width):
    return ref.at[half, :, pl.ds(ch * width, width)] if kind == 'col' else ref.at[ch, half]


def gather_halves(srcs, kinds, name):
    nw = len(srcs)
    widths = [s.shape[2] for s in srcs]

    def body(*refs):
        src, out = refs[:nw], refs[nw:2 * nw]
        ssem, rsem, osend, orecv = refs[2 * nw:]
        x, y, c = _place()
        chip = 2 * x + y
        sib = (x, y, 1 - c)
        peers = _peers(x, y)
        pidx = [2 * px + py for px, py in peers]

        def rect(n, half, ch):
            return _rect(out[n], kinds[n], half, ch, widths[n])

        mine = [_rcopy(src[n], _win(out[n], kinds[n], chip, widths[n]), osend.at[n], orecv.at[n], sib)
                for n in range(nw)]
        first = [[_rcopy(src[n].at[c], rect(n, c, chip), ssem.at[6 * n + k], rsem.at[6 * n + k], (px, py, c))
                  for k, (px, py) in enumerate(peers)] for n in range(nw)]
        for n in range(nw):
            for cp in first[n]:
                cp.start()
        for cp in mine:
            cp.start()
        passed = [[_rcopy(rect(n, c, pidx[k]), rect(n, c, pidx[k]), ssem.at[6 * n + 3 + k], rsem.at[6 * n + 3 + k], sib)
                   for k in range(3)] for n in range(nw)]
        for n in range(nw):
            for k, (px, py) in enumerate(peers):
                _rcopy(rect(n, c, pidx[k]), rect(n, c, pidx[k]), ssem.at[6 * n + k], rsem.at[6 * n + k],
                       (px, py, c)).wait_recv()
                passed[n][k].start()
        for n in range(nw):
            for k in range(3):
                _rcopy(rect(n, 1 - c, pidx[k]), rect(n, 1 - c, pidx[k]), ssem.at[6 * n + 3 + k],
                       rsem.at[6 * n + 3 + k], sib).wait_recv()
        for n in range(nw):
            for cp in first[n] + passed[n]:
                cp.wait_send()
        for cp in mine:
            cp.wait()

    return pl.pallas_call(
        body, name=name, in_specs=[ANY] * nw, out_specs=[ANY] * nw,
        out_shape=[jax.ShapeDtypeStruct(_gathered_shape(s, k), s.dtype) for s, k in zip(srcs, kinds)],
        scratch_shapes=[pltpu.SemaphoreType.DMA((6 * nw,)), pltpu.SemaphoreType.DMA((6 * nw,)),
                        pltpu.SemaphoreType.DMA((nw,)), pltpu.SemaphoreType.DMA((nw,))],
    )(*srcs)


def _gather_plan(kinds, widths):
    def plan(src, land, x, y, c):
        chip = 2 * x + y
        out = []
        for n in range(len(src)):
            mine = _win(land[n], kinds[n], chip, widths[n])
            for px, py in _peers(x, y):
                out.append((src[n], mine, (px, py, c), _win(land[n], kinds[n], 2 * px + py, widths[n])))
            out.append((src[n], mine, (x, y, 1 - c), mine))
        return out
    return plan


def _scatter_plan(axes, widths):
    def plan(src, land, x, y, c):
        out = []
        for n in range(len(src)):
            for k, (px, py) in enumerate(_peers(x, y)):
                ch = 2 * px + py
                view = (src[n].at[:, pl.ds(ch * widths[n], widths[n])] if axes[n] == 1
                        else src[n].at[pl.ds(ch * widths[n], widths[n]), :])
                out.append((view, land[n].at[k], (px, py, c), land[n].at[k]))
        return out
    return plan


def start_copies(srcs, lands, plan, ncopies, after, name):
    ns, nl = len(srcs), len(lands)

    def body(*refs):
        src, land = refs[:ns], refs[ns:ns + nl]
        ssem, rsem = refs[ns + nl + 1], refs[ns + nl + 2]
        token = refs[-1]
        x, y, c = _place()
        for k, (sv, dv, dev, _) in enumerate(plan(src, land, x, y, c)):
            _rcopy(sv, dv, ssem.at[k], rsem.at[k], dev).start()
        token[...] = jnp.zeros_like(token)

    hbm = lambda t: pltpu.HBM(t.shape, t.dtype)
    res = pl.pallas_call(
        body, name=name,
        out_shape=(pltpu.SemaphoreType.DMA((ncopies,)), pltpu.SemaphoreType.DMA((ncopies,)),
                   *[hbm(t) for t in srcs], *[hbm(t) for t in lands], jax.ShapeDtypeStruct((8, LANES), F32)),
        in_specs=[HBM] * (ns + nl) + [ANY],
        out_specs=(SEM, SEM, *[HBM] * (ns + nl), pl.BlockSpec(memory_space=pltpu.VMEM)),
        input_output_aliases={k: 2 + k for k in range(ns + nl)},
        compiler_params=pltpu.CompilerParams(has_side_effects=EFFECT),
    )(*[pltpu.with_memory_space_constraint(t, pltpu.HBM) for t in list(srcs) + list(lands)], after)
    return res[0], res[1], list(res[2:2 + ns]), list(res[2 + ns:2 + ns + nl]), res[-1]


def wait_copies(ssem, rsem, srcs, lands, plan, after, name):
    ns, nl = len(srcs), len(lands)

    def body(*refs):
        src, land = refs[:ns], refs[ns:ns + nl]
        ss, rs = refs[ns + nl], refs[ns + nl + 1]
        x, y, c = _place()
        for k, (sv, dv, dev, mine) in enumerate(plan(src, land, x, y, c)):
            cp = _rcopy(sv, mine, ss.at[k], rs.at[k], dev)
            cp.wait_send()
            cp.wait_recv()

    hbm = lambda t: pltpu.HBM(t.shape, t.dtype)
    res = pl.pallas_call(
        body, name=name,
        out_shape=(*[hbm(t) for t in srcs], *[hbm(t) for t in lands]),
        in_specs=[HBM] * (ns + nl) + [SEM, SEM, ANY], out_specs=tuple([HBM] * (ns + nl)),
        input_output_aliases={k: k for k in range(ns + nl)},
        compiler_params=pltpu.CompilerParams(has_side_effects=EFFECT),
    )(*srcs, *lands, ssem, rsem, after)
    return list(res[ns:])


def pair_swap_halves(gs, kinds, name):
    nw = len(gs)

    def other(ref, kind, half):
        return ref.at[half] if kind == 'col' else ref.at[:, half]

    def body(*refs):
        g, o = refs[:nw], refs[nw:2 * nw]
        ssem, rsem = refs[2 * nw:]
        x, y, c = _place()
        cps = [_rcopy(other(g[n], kinds[n], 1 - c), o[n], ssem.at[n], rsem.at[n], (x, y, 1 - c)) for n in range(nw)]
        for cp in cps:
            cp.start()
        for cp in cps:
            cp.wait()

    return pl.pallas_call(
        body, name=name, in_specs=[ANY] * nw, out_specs=[ANY] * nw,
        out_shape=[jax.ShapeDtypeStruct(g.shape[1:] if k == 'col' else (g.shape[0],) + g.shape[2:], g.dtype)
                   for g, k in zip(gs, kinds)],
        scratch_shapes=[pltpu.SemaphoreType.DMA((nw,)), pltpu.SemaphoreType.DMA((nw,))],
    )(*gs)


def pair_swap(fs, name):
    nw = len(fs)

    def body(*refs):
        f, o = refs[:nw], refs[nw:2 * nw]
        ssem, rsem = refs[2 * nw:]
        x, y, c = _place()
        cps = [_rcopy(f[n], o[n], ssem.at[n], rsem.at[n], (x, y, 1 - c)) for n in range(nw)]
        for cp in cps:
            cp.start()
        for cp in cps:
            cp.wait()

    return pl.pallas_call(
        body, name=name, in_specs=[ANY] * nw, out_specs=[ANY] * nw,
        out_shape=[jax.ShapeDtypeStruct(f.shape, f.dtype) for f in fs],
        scratch_shapes=[pltpu.SemaphoreType.DMA((nw,)), pltpu.SemaphoreType.DMA((nw,))],
    )(*fs)


def chip_exchange(hs, kinds, name):
    nw = len(hs)
    shp = [(h.shape[0], h.shape[1] // 4) if k == 'col' else h.shape[1:] for h, k in zip(hs, kinds)]

    def body(*refs):
        h, o = refs[:nw], refs[nw:2 * nw]
        ssem, rsem = refs[2 * nw:]
        x, y, c = _place()

        def win(n, ch):
            return h[n].at[:, pl.ds(ch * shp[n][1], shp[n][1])] if kinds[n] == 'col' else h[n].at[ch]

        cps = [_rcopy(win(n, 2 * px + py), o[n].at[k], ssem.at[3 * n + k], rsem.at[3 * n + k], (px, py, c))
               for n in range(nw) for k, (px, py) in enumerate(_peers(x, y))]
        for cp in cps:
            cp.start()
        for cp in cps:
            cp.wait()

    return pl.pallas_call(
        body, name=name, in_specs=[ANY] * nw, out_specs=[ANY] * nw,
        out_shape=[jax.ShapeDtypeStruct((3,) + sh, h.dtype) for sh, h in zip(shp, hs)],
        scratch_shapes=[pltpu.SemaphoreType.DMA((3 * nw,)), pltpu.SemaphoreType.DMA((3 * nw,))],
    )(*hs)


def pair_join_layers(fs, name):
    nw = len(fs)

    def body(*refs):
        o = refs[nw:2 * nw]
        ssem, rsem = refs[2 * nw:]
        x, y, c = _place()
        sib = (x, y, 1 - c)
        cps = [_rcopy(o[n].at[c], o[n].at[c], ssem.at[n], rsem.at[n], sib) for n in range(nw)]
        for cp in cps:
            cp.start()
        for n in range(nw):
            cps[n].wait_send()
            _rcopy(o[n].at[1 - c], o[n].at[1 - c], ssem.at[n], rsem.at[n], sib).wait_recv()

    return pl.pallas_call(
        body, name=name, in_specs=[ANY] * nw, out_specs=[ANY] * nw,
        out_shape=[jax.ShapeDtypeStruct(f.shape, f.dtype) for f in fs],
        input_output_aliases={n: n for n in range(nw)},
        scratch_shapes=[pltpu.SemaphoreType.DMA((nw,)), pltpu.SemaphoreType.DMA((nw,))],
    )(*fs)


def gather_all_devices(buf, name):
    r, c_ = buf.shape
    offs = [o for o in itertools.product((0, 1), repeat=3) if o != (0, 0, 0)]

    def body(b_ref, o_ref, ssem, rsem, lsem):
        x, y, c = _place()
        me = 4 * x + 2 * y + c
        mine = pltpu.make_async_copy(b_ref, o_ref.at[me], lsem)
        mine.start()
        peers = [((x + dx) % 2, (y + dy) % 2, (c + dc) % 2) for dx, dy, dc in offs]
        cps = [_rcopy(b_ref, o_ref.at[me], ssem.at[k], rsem.at[k], p) for k, p in enumerate(peers)]
        for cp in cps:
            cp.start()
        for k, (px, py, pc) in enumerate(peers):
            _rcopy(b_ref, o_ref.at[4 * px + 2 * py + pc], ssem.at[k], rsem.at[k], (px, py, pc)).wait_recv()
        for cp in cps:
            cp.wait_send()
        mine.wait()

    return pl.pallas_call(
        body, name=name, in_specs=[ANY], out_specs=ANY,
        out_shape=jax.ShapeDtypeStruct((8, r, c_), buf.dtype),
        scratch_shapes=[pltpu.SemaphoreType.DMA((7,)), pltpu.SemaphoreType.DMA((7,)), pltpu.SemaphoreType.DMA],
    )(buf)


def _flatten_pad(parts, dtype):
    flat = jnp.concatenate([p.reshape(-1).astype(dtype) for p in parts])
    q = 512 * LANES
    n = -(-flat.shape[0] // q) * q
    return jnp.pad(flat, (0, n - flat.shape[0])).reshape(n // LANES, LANES)


def _lane_pad(n):
    return -(-n // LANES) * LANES


def _in_proj_layout(d):
    gk, gv, cw, pw = d // 2, d, d // 2, d // 2
    own = [('q', gk), ('k', gk), ('v', gv), ('og', gv), ('lrf', GLA_LR), ('lrb', GLA_LR), ('ga', cw), ('gb', cw),
           ('pu', pw), ('mg', 3 * d)]
    padded = [('mg', 3 * d), ('v', gv), ('og', gv), ('q', gk), ('k', gk), ('ga', cw), ('gb', cw), ('pu', pw),
              ('lrf', GLA_LR), ('lrb', GLA_LR), ('pad', d // 2 - 2 * GLA_LR)]
    return own, padded


def _row_pieces(src, lo, hi, wl, wlp):
    out = []
    for k in range(4):
        s0, s1 = max(lo, k * wl), min(hi, (k + 1) * wl)
        if s0 < s1:
            out.append(src[k * wlp + s0 - k * wl:k * wlp + s1 - k * wl])
    return out


def _w_in_t_to_proj(g, d, wl, wlp):
    own, padded = _in_proj_layout(d)
    at, start = {}, 0
    for n, wd in own:
        at[n] = (start, start + wd)
        start += wd
    parts = []
    for n, wd in padded:
        parts += [jnp.zeros((wd, g.shape[1]), g.dtype)] if n == 'pad' else _row_pieces(g, *at[n], wl, wlp)
    return jnp.concatenate(parts, axis=0)


def _proj_to_w_in_t(gp, d, wl, wlp):
    own, padded = _in_proj_layout(d)
    pat, start = {}, 0
    for n, wd in padded:
        pat[n] = start
        start += wd
    parts = []
    for k in range(4):
        start = 0
        for n, wd in own:
            s0, s1 = max(start, k * wl), min(start + wd, (k + 1) * wl)
            if s0 < s1:
                parts.append(gp[pat[n] + s0 - start:pat[n] + s1 - start])
            start += wd
        parts.append(jnp.zeros((wlp - wl, gp.shape[1]), gp.dtype))
    return jnp.concatenate(parts, axis=0)


def _silu_grad(z):
    s = jax.nn.sigmoid(z)
    return s + z * s * (1.0 - s)


def kernel(x, c, ctx, c_ctx, w_ada, b_ada, g_pre_mix, g_post_mix, g_pre_mlp, g_post_mlp, w_in, w_decay, b_decay, g_gla, w_gla_o, w_dw, b_dw, g_conv_ln, b_conv_ln, w_conv_o, w_pool_g, s_pool, w_pool_o, b_gate, w_out, w_mlp1, w_mlp2, loss_target, m_c_ctx, m_w_ada, m_b_ada, m_g_pre_mix, m_g_post_mix, m_g_pre_mlp, m_g_post_mlp, m_w_in, m_w_decay, m_b_decay, m_g_gla, m_w_gla_o, m_w_dw, m_b_dw, m_g_conv_ln, m_b_conv_ln, m_w_conv_o, m_w_pool_g, m_s_pool, m_w_pool_o, m_b_gate, m_w_out, m_w_mlp1, m_w_mlp2, v_c_ctx, v_w_ada, v_b_ada, v_g_pre_mix, v_g_post_mix, v_g_pre_mlp, v_g_post_mlp, v_w_in, v_w_decay, v_b_decay, v_g_gla, v_w_gla_o, v_w_dw, v_b_dw, v_g_conv_ln, v_b_conv_ln, v_w_conv_o, v_w_pool_g, v_s_pool, v_w_pool_o, v_b_gate, v_w_out, v_w_mlp1, v_w_mlp2):
    a = dict(locals())
    for n in ('w_in', 'm_w_in', 'v_w_in'):
        a[n] = jnp.swapaxes(a[n], 1, 2)
    big_axis = dict(BIG, w_in=1)
    depth = w_in.shape[0]
    d = x.shape[-1]
    seq, nctx_rows = x.shape[1], ctx.shape[1]
    dm = types.SimpleNamespace(
        D=d, SEQ=seq, CTX=nctx_rows, T=seq + nctx_rows, DK=d // 8, DV=d // 4, GK=d // 2, GC=d // 8,
        tm=_tile(nctx_rows, (256, 128, 64)), TB=_tile(nctx_rows, (256, 128, 64)))
    assert dm.SEQ % dm.tm == 0 and dm.SEQ % GRID_W == 0 and dm.CTX % GLA_CHUNK == 0
    tmw = min(dm.tm, 128)
    chip = 2 * lax.axis_index("x") + lax.axis_index("y")
    core = lax.axis_index("c")
    chip1 = chip.astype(jnp.int32).reshape(1)
    core1 = core.astype(jnp.int32).reshape(1)

    big_names, small_names = list(BIG), list(SMALL_SHARDED)
    nbig = len(big_names)
    kinds = ['col' if big_axis[n] == 2 else 'row' for n in big_names]
    wl = w_in.shape[2]
    wlp = _lane_pad(wl)

    def rows8(t):
        t = t.reshape(t.shape[0], -1, t.shape[-1])
        return jnp.pad(t, ((0, 0), (0, -t.shape[1] % 8), (0, 0)))

    def halves(t):
        return t.reshape(2, t.shape[0] // 2, t.shape[1])

    def layer_src(l):
        return [halves((jnp.pad(a[n][l], ((0, wlp - wl), (0, 0))) if n == 'w_in' else a[n][l]).astype(MM_DTYPE))
                for n in big_names]

    def whole(t):
        return t.reshape(-1, t.shape[-1])

    late = [big_names.index(n) for n in ('w_mlp1', 'w_mlp2')]
    early = [k for k in range(nbig) if k not in late]
    src0, src1 = layer_src(0), layer_src(1)
    g0 = gather_halves([src0[k] for k in early] + [rows8(a[n]) for n in small_names],
                       [kinds[k] for k in early] + ['col'] * len(small_names), "gather_layer0")

    def start_gather(srcs, knds, after, name):
        plan = _gather_plan(knds, [t.shape[2] for t in srcs])
        lands = [lax.empty(_gathered_shape(t, k), t.dtype) for t, k in zip(srcs, knds)]
        return (plan,) + start_copies(srcs, lands, plan, 4 * len(srcs), after, name)

    ag0 = start_gather([src0[k] for k in late], [kinds[k] for k in late], g0[0], "gather_layer0_mlp_start")
    ag1 = start_gather(src1, kinds, ag0[-1], "gather_layer1_start")
    ag_token = ag1[-1]
    full = {n: [None, None] for n in big_names}
    for k, t in zip(early, g0):
        full[big_names[k]][0] = whole(t)
    for n, g in zip(small_names, g0[len(early):]):
        shp = a[n].shape
        full[n] = g[:, :math.prod(shp[1:-1])].reshape(shp[:-1] + (4 * shp[-1],))
    for n in SMALL:
        if n not in SMALL_SHARDED:
            full[n] = a[n]

    cvec = jnp.concatenate([c_ctx.reshape(1, d), c.reshape(1, d), jnp.zeros((6, d), F32)], axis=0)
    avec = (cvec * jax.nn.sigmoid(cvec) + ag_token[0, 0]).astype(MM_DTYPE)

    def row(v):
        return v.reshape(1, -1)

    X = jnp.concatenate([ctx[0], x[0]], axis=0)
    saved = []
    gk, gv = dm.GK, d
    lrblk = (7 * d + d // 2) // LANES
    for l in range(depth):
        if l == 1:
            got = wait_copies(ag1[1], ag1[2], ag1[3], ag1[4], ag1[0], X, "gather_layer1_wait")
            for n, t in zip(big_names, got):
                full[n][1] = whole(t)
        s = types.SimpleNamespace()
        s.w_in_p = _w_in_t_to_proj(full['w_in'][l], d, wl, wlp)
        wd = full['w_decay'][l]
        wdp = jnp.zeros((LANES, 2 * gk), F32)
        wdp = wdp.at[:GLA_LR, :gk].set(wd[0]).at[GLA_LR:2 * GLA_LR, gk:].set(wd[1])
        s.wdp = wdp.astype(MM_DTYPE)
        s.bd = full['b_decay'][l].reshape(1, 2 * gk)
        modraw = matmul(avec, full['w_ada'][l], 'nn', F32, f"mod_{l}") + full['b_ada'][l][None, :]
        s.mod = [modraw[0:2, j * d:(j + 1) * d].reshape(2, 1, d) for j in range(6)]
        s.x = X
        (s.h,) = rowwise(pre_fn, [X], s.mod[0:2], [row(g_pre_mix[l])], [(d, MM_DTYPE)], dm, f"pre_{l}")
        s.P = matmul(s.h, s.w_in_p, 'nt', MM_DTYPE, f"in_proj_{l}")
        P = s.P
        s.z = matmul((P, LANES, lrblk), s.wdp, 'nn', F32, f"decay_proj_{l}", tk=LANES)
        la_f, la_b = rowwise(decay_fn, [s.z], [], [s.bd], [(gk, F32), (gk, F32)], dm, f"decay_{l}")
        s.la = jnp.concatenate([la_f, la_b], axis=1)
        s.o_f, s.st_f = gla_fwd(P, s.la, False, dm, f"gla_fwd_f_{l}")
        s.o_b, s.st_b = gla_fwd(P, s.la, True, dm, f"gla_fwd_b_{l}")
        (s.gin,) = rowwise(glaout_fn, [s.o_f, s.o_b, (P, d, 4)], [], [row(g_gla[l])], [(gv, MM_DTYPE)], dm,
                           f"gla_out_{l}")
        s.ya = matmul(s.gin, full['w_gla_o'][l], 'nn', F32, f"gla_o_{l}")
        (s.u,) = rowwise(glu_fn, [(P, d // 2, 12), (P, d // 2, 13)], [], [], [(d // 2, F32)], dm, f"glu_{l}")
        s.yconv = conv_fwd(s.u, full['w_dw'][l], dm, f"conv_{l}")
        (s.cin,) = rowwise(convpost_fn, [s.yconv], [], [row(b_dw[l]), row(g_conv_ln[l]), row(b_conv_ln[l])],
                           [(d // 2, MM_DTYPE)], dm, f"conv_post_{l}")
        s.yb = matmul(s.cin, full['w_conv_o'][l], 'nn', F32, f"conv_o_{l}")
        s.pm = pool_mix((P, d // 2, 14), False, dm, f"pool_mix_{l}")
        s.pc = group_mm(s.pm, w_pool_g[l], 'nn', F32, f"pool_g_{l}")
        (s.pin,) = rowwise(poolpost_fn, [s.pc], [], [row(s_pool[l])], [(d // 2, MM_DTYPE)], dm, f"pool_post_{l}")
        s.yc = matmul(s.pin, full['w_pool_o'][l], 'nn', F32, f"pool_o_{l}")
        s.bg = [row(full['b_gate'][l][j]) for j in range(3)]
        (s.mixed,) = rowwise(merge_fn, [s.ya, s.yb, s.yc, (P, 3 * d, 0)], [], s.bg, [(d, MM_DTYPE)], dm,
                             f"merge_{l}", tm=tmw)
        s.y = matmul(s.mixed, full['w_out'][l], 'nn', F32, f"out_proj_{l}")
        if l == 0:
            got = wait_copies(ag0[1], ag0[2], ag0[3], ag0[4], ag0[0], s.y, "gather_layer0_mlp_wait")
            for k, t in zip(late, got):
                full[big_names[k]][0] = whole(t)
        s.x1, s.h2 = rowwise(mid_fn, [X, s.y], s.mod[2:5], [row(g_post_mix[l]), row(g_pre_mlp[l])],
                             [(d, F32), (d, MM_DTYPE)], dm, f"mid_{l}")
        s.act = matmul(s.h2, full['w_mlp1'][l], 'nn', MM_DTYPE, f"mlp1_{l}", epi=relu2_epi)
        s.y2 = matmul(s.act, full['w_mlp2'][l], 'nn', F32, f"mlp2_{l}")
        (X,) = rowwise(post_fn, [s.x1, s.y2], s.mod[5:6], [row(g_post_mlp[l])], [(d, F32)], dm, f"post_{l}")
        saved.append(s)

    dX, lossv = loss_head(X, loss_target[0], dm, "loss_head")
    loss = lax.psum(lossv[0, 0], ("x", "y", "c"))

    grads = {n: [None] * depth for n in WEIGHTS if n != 'c_ctx' and n not in BIG}
    gbig = {n: [None] * depth for n in BIG}
    rs_token = None

    def start_scatter(idx, layer, after, name):
        gs = [gbig[big_names[k]][layer] for k in idx]
        wd = [t.shape[1] // 4 if kinds[k] == 'col' else t.shape[0] // 4 for t, k in zip(gs, idx)]
        plan = _scatter_plan([big_axis[big_names[k]] - 1 for k in idx], wd)
        lands = [lax.empty((3, t.shape[0], w) if kinds[k] == 'col' else (3, w, t.shape[1]), t.dtype)
                 for t, w, k in zip(gs, wd, idx)]
        return (plan,) + start_copies(gs, lands, plan, 3 * len(gs), after, name)

    g_cctx = jnp.zeros((d,), F32)
    for l in reversed(range(depth)):
        s = saved[l]
        P = s.P
        dmod = [None] * 6
        gpm = row(g_post_mlp[l]) if rs_token is None else row(g_post_mlp[l]) + rs_token[0, 0]
        (dx1, dy2), (dmod[5],), (dg,) = rowwise_vjp(post_fn, [s.x1, s.y2], s.mod[5:6], [gpm], [dX],
                                                     dm, f"post_bwd_{l}", narrow=(1,))
        grads['g_post_mlp'][l] = dg[0]
        du1 = matmul(dy2, full['w_mlp2'][l], 'nt', MM_DTYPE, f"mlp2_dx_{l}", epi=relu2_bwd_epi, extras=[s.act])
        gbig['w_mlp2'][l] = matmul(s.act, dy2, 'tn', MM_DTYPE, f"mlp2_dw_{l}")
        dh2 = matmul(du1, full['w_mlp1'][l], 'nt', MM_DTYPE, f"mlp1_dx_{l}")
        gbig['w_mlp1'][l] = matmul(s.h2, du1, 'tn', MM_DTYPE, f"mlp1_dw_{l}")
        gpx = row(g_post_mix[l])
        if l == 0:
            rs0 = start_scatter(late, 0, dh2, "grad_layer0_mlp_start")
            gpx = gpx + rs0[-1][0, 0]
        (dxa, dy), dmod[2:5], (dg1, dg2) = rowwise_vjp(
            mid_fn, [s.x, s.y], s.mod[2:5], [gpx, row(g_pre_mlp[l])], [dx1, dh2], dm, f"mid_bwd_{l}", narrow=(1,))
        grads['g_post_mix'][l], grads['g_pre_mlp'][l] = dg1[0], dg2[0]
        dmixed = matmul(dy, full['w_out'][l], 'nt', MM_DTYPE, f"out_proj_dx_{l}")
        gbig['w_out'][l] = matmul(s.mixed, dy, 'tn', MM_DTYPE, f"out_proj_dw_{l}")
        (dya, dyb, dyc, dmg), _, dbg = rowwise_vjp(merge_fn, [s.ya, s.yb, s.yc, (P, 3 * d, 0)], [], s.bg, [dmixed],
                                                   dm, f"merge_bwd_{l}", tm=tmw, narrow=(0, 1, 2))
        grads['b_gate'][l] = jnp.concatenate(dbg, axis=0)
        dgin = matmul(dya, full['w_gla_o'][l], 'nt', MM_DTYPE, f"gla_o_dx_{l}")
        gbig['w_gla_o'][l] = matmul(s.gin, dya, 'tn', MM_DTYPE, f"gla_o_dw_{l}")
        dcin = matmul(dyb, full['w_conv_o'][l], 'nt', MM_DTYPE, f"conv_o_dx_{l}")
        gbig['w_conv_o'][l] = matmul(s.cin, dyb, 'tn', MM_DTYPE, f"conv_o_dw_{l}")
        dpin = matmul(dyc, full['w_pool_o'][l], 'nt', MM_DTYPE, f"pool_o_dx_{l}")
        gbig['w_pool_o'][l] = matmul(s.pin, dyc, 'tn', MM_DTYPE, f"pool_o_dw_{l}")
        (dpc,), _, (dsp,) = rowwise_vjp(poolpost_fn, [s.pc], [], [row(s_pool[l])], [dpin], dm, f"pool_post_bwd_{l}")
        grads['s_pool'][l] = dsp[0]
        grads['w_pool_g'][l] = group_mm(s.pm, w_pool_g[l], 'tn', F32, f"pool_g_dw_{l}", b=dpc)
        dpm = group_mm(dpc, w_pool_g[l], 'nt', F32, f"pool_g_dx_{l}")
        dpu = pool_mix(dpm, True, dm, f"pool_mix_bwd_{l}")
        (dyconv,), _, (dbdw, dgln, dbln) = rowwise_vjp(
            convpost_fn, [s.yconv], [], [row(b_dw[l]), row(g_conv_ln[l]), row(b_conv_ln[l])], [dcin], dm,
            f"conv_post_bwd_{l}")
        grads['b_dw'][l], grads['g_conv_ln'][l], grads['b_conv_ln'][l] = dbdw[0], dgln[0], dbln[0]
        du, grads['w_dw'][l] = conv_bwd(s.u, full['w_dw'][l], dyconv, dm, f"conv_bwd_{l}")
        (dga, dgb), _, _ = rowwise_vjp(glu_fn, [(P, d // 2, 12), (P, d // 2, 13)], [], [], [du], dm, f"glu_bwd_{l}")
        (do, _, dog), _, (dgg,) = rowwise_vjp(glaout_fn, [s.o_f, s.o_b, (P, d, 4)], [], [row(g_gla[l])], [dgin], dm,
                                              f"gla_out_bwd_{l}", want=[True, False, True])
        grads['g_gla'][l] = dgg[0]
        dqf, dkf, dvf, dlaf = gla_bwd(P, s.la, do, s.st_f, False, dm, f"gla_bwd_f_{l}")
        dqb, dkb, dvb, dlab = gla_bwd(P, s.la, do, s.st_b, True, dm, f"gla_bwd_b_{l}")
        (dz,), _, (dbd,) = rowwise_vjp(decay_fn, [s.z], [], [s.bd], [dlaf, dlab], dm, f"decay_bwd_{l}", narrow=(0,))
        grads['b_decay'][l] = dbd.reshape(2, gk)
        dwdp = matmul((P, LANES, lrblk), dz, 'tn', F32, f"decay_proj_dw_{l}", tm=LANES)
        grads['w_decay'][l] = jnp.stack([dwdp[:GLA_LR, :gk], dwdp[GLA_LR:2 * GLA_LR, gk:]])
        dlr = matmul(dz, s.wdp, 'nt', F32, f"decay_proj_dx_{l}")

        def asm_fn(dmg_, dvf_, dvb_, dog_, dqf_, dqb_, dkf_, dkb_, dga_, dgb_, dpu_, dlr_):
            f = lambda t: t.astype(F32)
            pad = jnp.zeros((dlr_.shape[0], d // 2 - LANES), F32)
            return (jnp.concatenate([f(dmg_), dvf_ + dvb_, f(dog_), dqf_ + dqb_, dkf_ + dkb_, f(dga_), f(dgb_),
                                     dpu_, dlr_, pad], axis=1).astype(MM_DTYPE),)
        (dP,) = rowwise(asm_fn, [dmg, dvf, dvb, dog, dqf, dqb, dkf, dkb, dga, dgb, dpu, dlr], [], [],
                        [(8 * d, MM_DTYPE)], dm, f"dproj_{l}", tm=tmw)
        dh = matmul(dP, s.w_in_p, 'nn', MM_DTYPE, f"in_proj_dx_{l}")
        gbig['w_in'][l] = _proj_to_w_in_t(matmul(dP, s.h, 'tn', MM_DTYPE, f"in_proj_dw_{l}"), d, wl, wlp)
        (dX,), dmod[0:2], (dg,) = rowwise_vjp(pre_fn, [s.x], s.mod[0:2], [row(g_pre_mix[l])], [dh], dm,
                                               f"pre_bwd_{l}", adds={0: dxa})
        grads['g_pre_mix'][l] = dg[0]
        dmodflat = jnp.concatenate([jnp.concatenate([m_.reshape(2, d) for m_ in dmod], axis=1),
                                    jnp.zeros((6, 6 * d), F32)], axis=0)
        grads['b_ada'][l] = dmodflat[0] + dmodflat[1]
        gbig['w_ada'][l] = matmul(avec, dmodflat, 'tn', MM_DTYPE, f"ada_dw_{l}")
        dav = matmul(dmodflat, full['w_ada'][l], 'nt', F32, f"ada_dx_{l}")
        g_cctx = g_cctx + dav[0] * _silu_grad(c_ctx)
        if l == 1:
            rs1 = start_scatter(list(range(nbig)), 1, dav, "grad_layer1_start")
            rs_token = rs1[-1]

    grad_x = dX[dm.CTX:][None]
    gfull = {n: jnp.stack(v) for n, v in grads.items()}
    gfull['c_ctx'] = g_cctx
    where = jnp.concatenate([chip1, core1])

    def halves_view(t, k):
        return t.reshape(2, t.shape[0] // 2, t.shape[1]) if k == 'col' else t.reshape(4, 2, t.shape[0] // 8, t.shape[1])
    enames = [big_names[k] for k in early]
    ekinds = [kinds[k] for k in early]
    v0 = [halves_view(gbig[n][0], k) for n, k in zip(enames, ekinds)]
    r1 = pair_swap_halves(v0, ekinds, "grad_pair_swap")
    hs = [pair_add(v.reshape((-1,) + v.shape[-2:]), r.reshape((-1,) + r.shape[-2:]), core1, f"grad_pair_add_{n}")
          for n, v, r in zip(enames, v0, r1)]
    hx = [h.reshape(h.shape[1:]) if k == 'col' else h for h, k in zip(hs, ekinds)]
    r2 = chip_exchange(hx, ekinds, "grad_chip_exchange")
    fs = [chip_add(h.reshape(-1, h.shape[-1]), r, big_axis[n] - 1, where, f"grad_chip_add_{n}")
          for n, h, r in zip(enames, hs, r2)]
    red0 = dict(zip(enames, [[t.reshape(-1, t.shape[-1])] for t in pair_join_layers(fs, "grad_pair_join")]))

    got0 = wait_copies(rs0[1], rs0[2], rs0[3], rs0[4], rs0[0], dX, "grad_layer0_mlp_wait")
    got1 = wait_copies(rs1[1], rs1[2], rs1[3], rs1[4], rs1[0], dX, "grad_layer1_wait")
    sa = [chip_add(g, r, big_axis[big_names[k]] - 1, where, f"grad_layer0_add_{big_names[k]}", slab=False)
          for k, g, r in zip(late, rs0[3], got0)]
    sa += [chip_add(g, r, big_axis[n] - 1, where, f"grad_layer1_add_{n}", slab=False)
           for n, g, r in zip(big_names, rs1[3], got1)]
    sb = pair_swap(sa, "grad_late_pair_swap")
    for j, k in enumerate(late):
        red0[big_names[k]] = [sa[j], sb[j]]
    red1 = {n: [sa[len(late) + k], sb[len(late) + k]] for k, n in enumerate(big_names)}

    sflat = _flatten_pad([gfull[n].astype(F32) for n in SMALL], F32)
    ssum = slot_sum(gather_all_devices(sflat, "small_grad_gather"), "small_grad_sum").reshape(-1)

    out_g, out_d, out_m, out_v = {}, {}, {}, {}
    for k, n in enumerate(big_names):
        out_g[n], out_d[n], out_m[n], out_v[n] = adamw_layers(a[n], a['m_' + n], a['v_' + n], red0[n], red1[n],
                                                              f"adamw_{n}")
    start = 0
    sg = {}
    for n in SMALL:
        cnt = gfull[n].size
        g = ssum[start:start + cnt].reshape(gfull[n].shape)
        start += cnt
        if n in SMALL_SHARDED:
            ax = SMALL_SHARDED[n]
            wdt = a[n].shape[ax]
            g = lax.dynamic_slice_in_dim(g, chip * wdt, wdt, axis=ax)
        sg[n] = g
    pk = lambda dct, pre: _flatten_pad([dct[pre + n] for n in SMALL], F32)
    gs = _flatten_pad([sg[n] for n in SMALL], F32)
    dl, mn, vn = adamw(pk(a, ''), gs, pk(a, 'm_'), pk(a, 'v_'), "adamw_small")
    dl, mn, vn = dl.reshape(-1), mn.reshape(-1), vn.reshape(-1)
    start = 0
    for n in SMALL:
        cnt, shp = a[n].size, a[n].shape
        out_g[n] = sg[n]
        out_d[n], out_m[n], out_v[n] = (t[start:start + cnt].reshape(shp) for t in (dl, mn, vn))
        start += cnt

    for dct in (out_g, out_d, out_m, out_v):
        dct['w_in'] = jnp.swapaxes(dct['w_in'], 1, 2)
    return (loss, grad_x, *[out_g[n] for n in WEIGHTS], *[out_d[n] for n in WEIGHTS],
            *[out_m[n] for n in WEIGHTS], *[out_v[n] for n in WEIGHTS])
```

```python
import functools
import itertools
import math
import types

import jax
import jax.numpy as jnp
from jax import lax
from jax.experimental import pallas as pl
from jax.experimental.pallas import tpu as pltpu

F32 = jnp.float32
MM_DTYPE = jnp.bfloat16
VMEM_LIMIT_V7X = 56 * 1024 * 1024
LANES = 128
EPS = 1e-6

N_HEADS = 4
GLA_CHUNK = 64
GLA_TAU = 16.0
GLA_LR = 16
GRID_W = 64
POOL_WINDOWS = (2, 4, 8, 16)

ADAM_LR = 0.001
ADAM_B1 = 0.9
ADAM_B2 = 0.999
ADAM_EPS = 1e-08
ADAM_WD = 0.01
ADAM_STEP = 10

NN = (((1,), (0,)), ((), ()))
NT = (((1,), (1,)), ((), ()))
TN = (((0,), (0,)), ((), ()))

WEIGHTS = ['c_ctx', 'w_ada', 'b_ada', 'g_pre_mix', 'g_post_mix', 'g_pre_mlp', 'g_post_mlp', 'w_in', 'w_decay',
           'b_decay', 'g_gla', 'w_gla_o', 'w_dw', 'b_dw', 'g_conv_ln', 'b_conv_ln', 'w_conv_o', 'w_pool_g',
           's_pool', 'w_pool_o', 'b_gate', 'w_out', 'w_mlp1', 'w_mlp2']
BIG = {'w_ada': 2, 'w_in': 2, 'w_gla_o': 1, 'w_conv_o': 2, 'w_pool_o': 2, 'w_out': 1, 'w_mlp1': 2, 'w_mlp2': 1}
SMALL_SHARDED = {'w_decay': 3, 'b_decay': 2, 'w_dw': 2, 'b_gate': 2}
SMALL = [n for n in WEIGHTS if n not in BIG]


def _tile(n, prefs):
    for t in prefs:
        if n % t == 0:
            return t
    return n


def _cparams(sem=None, **kw):
    return pltpu.CompilerParams(dimension_semantics=sem, vmem_limit_bytes=VMEM_LIMIT_V7X, **kw)


def _dot(a, b, dims=NN):
    return lax.dot_general(a.astype(MM_DTYPE), b.astype(MM_DTYPE), dims, preferred_element_type=F32)


def matmul(a, b, mode, out_dtype, name, tm=None, tn=None, tk=None, epi=None, extras=()):
    a, aw, ablk = a if isinstance(a, tuple) else (a, a.shape[1], 0)
    if mode == 'nn':
        M, K, N = a.shape[0], aw, b.shape[1]
    elif mode == 'nt':
        M, K, N = a.shape[0], aw, b.shape[0]
    else:
        K, M, N = a.shape[0], aw, b.shape[1]
    big = (1088, 1024, 640, 544, 512, 320, 256, 128, 64, 32, 16, 8)
    if mode == 'tn':
        tm = tm or _tile(M, (1024, 512, 256, 128))
        tn = tn or _tile(N, (1024, 512, 256, 128))
        tk = tk or _tile(K, big)
    else:
        tm = tm or _tile(M, big)
        tn = tn or _tile(N, (1024, 512, 256, 128))
        tk = tk or _tile(K, (1024, 512, 256, 128))
    if aw != a.shape[1]:
        assert (mode == 'tn' and tm == aw) or (mode != 'tn' and tk == aw)
    nk = K // tk
    ne = len(extras)
    dims = {'nn': NN, 'nt': NT, 'tn': TN}[mode]

    def body(a_ref, b_ref, *rest):
        e_refs, o_ref = rest[:ne], rest[ne]

        def finish(acc):
            if epi is not None:
                acc = epi(acc, *[e[...] for e in e_refs])
            o_ref[...] = acc.astype(o_ref.dtype)

        p = _dot(a_ref[...], b_ref[...], dims)
        if nk == 1:
            finish(p)
            return
        acc = rest[-1]
        k = pl.program_id(2)

        @pl.when(k == 0)
        def _():
            acc[...] = p

        @pl.when(k > 0)
        def _():
            acc[...] += p

        @pl.when(k == nk - 1)
        def _():
            finish(acc[...])

    if mode == 'nn':
        a_spec = pl.BlockSpec((tm, tk), lambda i, j, k: (i, k + ablk))
        b_spec = pl.BlockSpec((tk, tn), lambda i, j, k: (k, j))
    elif mode == 'nt':
        a_spec = pl.BlockSpec((tm, tk), lambda i, j, k: (i, k + ablk))
        b_spec = pl.BlockSpec((tn, tk), lambda i, j, k: (j, k))
    else:
        a_spec = pl.BlockSpec((tk, tm), lambda i, j, k: (k, i + ablk))
        b_spec = pl.BlockSpec((tk, tn), lambda i, j, k: (k, j))
    tile = pl.BlockSpec((tm, tn), lambda i, j, k: (i, j))
    return pl.pallas_call(
        body, name=name, grid=(M // tm, N // tn, nk),
        in_specs=[a_spec, b_spec] + [tile] * ne, out_specs=tile,
        out_shape=jax.ShapeDtypeStruct((M, N), out_dtype),
        scratch_shapes=[] if nk == 1 else [pltpu.VMEM((tm, tn), F32)],
        compiler_params=_cparams(("parallel", "parallel", "arbitrary")),
    )(a, b, *extras)


def group_mm(a, w, mode, out_dtype, name, b=None):
    T = a.shape[0]
    G, gc, _ = w.shape
    col = pl.BlockSpec((T, gc), lambda g: (0, g))
    wsp = pl.BlockSpec((1, gc, gc), lambda g: (g, 0, 0))
    if mode == 'tn':
        def body(a_ref, b_ref, o_ref):
            o_ref[0] = _dot(a_ref[...], b_ref[...], TN).astype(o_ref.dtype)
        return pl.pallas_call(body, name=name, grid=(G,), in_specs=[col, col], out_specs=wsp,
                              out_shape=jax.ShapeDtypeStruct((G, gc, gc), out_dtype),
                              compiler_params=_cparams(("parallel",)))(a, b)
    dims = NN if mode == 'nn' else NT

    def body(a_ref, w_ref, o_ref):
        o_ref[...] = _dot(a_ref[...], w_ref[0], dims).astype(o_ref.dtype)
    return pl.pallas_call(body, name=name, grid=(G,), in_specs=[col, wsp], out_specs=col,
                          out_shape=jax.ShapeDtypeStruct((T, G * gc), out_dtype),
                          compiler_params=_cparams(("parallel",)))(a, w)


def _rowspec(r):
    return r if isinstance(r, tuple) else (r, r.shape[1], 0)


def _row_specs(rows, segs, consts, tm, nctx):
    specs = [pl.BlockSpec((tm, w), lambda i, b=b: (i, b)) for _, w, b in rows]
    specs += [pl.BlockSpec((1,) + s.shape[1:], lambda i, n=s.ndim: (jnp.where(i >= nctx, 1, 0),) + (0,) * (n - 1))
              for s in segs]
    specs += [pl.BlockSpec(c.shape, lambda i, n=c.ndim: (0,) * n) for c in consts]
    return specs


def rowwise(fn, rows, segs, consts, outs, dm, name, tm=None):
    tm = tm or dm.tm
    nctx = dm.CTX // tm
    rows = [_rowspec(r) for r in rows]
    nr, ns, nc = len(rows), len(segs), len(consts)

    def body(*refs):
        rin = [r[...] for r in refs[:nr]]
        sin = [s[0] for s in refs[nr:nr + ns]]
        cin = [c[...] for c in refs[nr + ns:nr + ns + nc]]
        res = fn(*rin, *sin, *cin)
        for o_ref, v in zip(refs[nr + ns + nc:], res):
            o_ref[...] = v.astype(o_ref.dtype)

    res = pl.pallas_call(
        body, name=name, grid=(dm.T // tm,),
        in_specs=_row_specs(rows, segs, consts, tm, nctx),
        out_specs=[pl.BlockSpec((tm, w), lambda i: (i, 0)) for w, _ in outs],
        out_shape=[jax.ShapeDtypeStruct((dm.T, w), dt) for w, dt in outs],
        compiler_params=_cparams(("parallel",)),
    )(*[r[0] for r in rows], *segs, *consts)
    return res


def rowwise_vjp(fn, rows, segs, consts, cots, dm, name, tm=None, want=None, adds=None, narrow=()):
    tm = tm or dm.tm
    nctx = dm.CTX // tm
    rows = [_rowspec(r) for r in rows]
    cots = [_rowspec(r) for r in cots]
    adds = adds or {}
    nr, ns, nc, nct = len(rows), len(segs), len(consts), len(cots)
    want = want or [True] * nr
    widx = [k for k in range(nr) if want[k]]
    akeys = sorted(adds)

    def body(*refs):
        i = pl.program_id(0)
        rin = [r[...] for r in refs[:nr]]
        sin = [s[0] for s in refs[nr:nr + ns]]
        cin = [c[...] for c in refs[nr + ns:nr + ns + nc]]
        p = nr + ns + nc
        cot_refs = refs[p:p + nct]
        add_refs = dict(zip(akeys, refs[p + nct:p + nct + len(akeys)]))
        p = p + nct + len(akeys)
        rg_refs = refs[p:p + len(widx)]
        sg_refs = refs[p + len(widx):p + len(widx) + ns]
        cg_refs = refs[p + len(widx) + ns:]
        res, vjp = jax.vjp(fn, *rin, *sin, *cin)
        g = vjp(tuple(cr[...].astype(o.dtype) for cr, o in zip(cot_refs, res)))
        for o_ref, k in zip(rg_refs, widx):
            v = g[k].astype(F32)
            if k in add_refs:
                v = v + add_refs[k][...]
            o_ref[...] = v.astype(o_ref.dtype)
        first_seg = jnp.logical_or(i == 0, i == nctx)
        for o_ref, v in zip(sg_refs, g[nr:nr + ns]):
            @pl.when(first_seg)
            def _(o_ref=o_ref, v=v):
                o_ref[0] = v.astype(F32)

            @pl.when(jnp.logical_not(first_seg))
            def _(o_ref=o_ref, v=v):
                o_ref[0] += v.astype(F32)
        for o_ref, v in zip(cg_refs, g[nr + ns:]):
            @pl.when(i == 0)
            def _(o_ref=o_ref, v=v):
                o_ref[...] = v.astype(F32)

            @pl.when(i > 0)
            def _(o_ref=o_ref, v=v):
                o_ref[...] += v.astype(F32)

    in_specs = _row_specs(rows, segs, consts, tm, nctx)
    in_specs += [pl.BlockSpec((tm, w), lambda i, b=b: (i, b)) for _, w, b in cots]
    in_specs += [pl.BlockSpec((tm, adds[k].shape[1]), lambda i: (i, 0)) for k in akeys]
    out_specs = [pl.BlockSpec((tm, rows[k][1]), lambda i: (i, 0)) for k in widx]
    out_shape = [jax.ShapeDtypeStruct((dm.T, rows[k][1]), MM_DTYPE if k in narrow else rows[k][0].dtype)
                 for k in widx]
    out_specs += [pl.BlockSpec((1,) + s.shape[1:], lambda i, n=s.ndim: (jnp.where(i >= nctx, 1, 0),) + (0,) * (n - 1))
                  for s in segs]
    out_shape += [jax.ShapeDtypeStruct(s.shape, F32) for s in segs]
    out_specs += [pl.BlockSpec(c.shape, lambda i, n=c.ndim: (0,) * n) for c in consts]
    out_shape += [jax.ShapeDtypeStruct(c.shape, F32) for c in consts]
    res = pl.pallas_call(
        body, name=name, grid=(dm.T // tm,), in_specs=in_specs, out_specs=out_specs, out_shape=out_shape,
        compiler_params=_cparams(("arbitrary",)),
    )(*[r[0] for r in rows], *segs, *consts, *[r[0] for r in cots], *[adds[k] for k in akeys])
    rg = [None] * nr
    for k, v in zip(widx, res[:len(widx)]):
        rg[k] = v
    return rg, list(res[len(widx):len(widx) + ns]), list(res[len(widx) + ns:])


def _rms(x, g):
    return x * lax.rsqrt(jnp.mean(x * x, axis=-1, keepdims=True) + EPS) * g


def _sigmoid(x):
    return jax.nn.sigmoid(x)


def pre_fn(x, shift, scale, g):
    return ((_rms(x, g) * (1.0 + scale) + shift).astype(MM_DTYPE),)


def mid_fn(x, y, gate, shift, scale, g_post, g_pre):
    x1 = x + gate * _rms(y.astype(F32), g_post)
    return x1, (_rms(x1, g_pre) * (1.0 + scale) + shift).astype(MM_DTYPE)


def post_fn(x1, y2, gate, g):
    return (x1 + gate * _rms(y2.astype(F32), g),)


def relu2_epi(acc):
    r = jnp.maximum(acc, 0.0)
    return r * r


def relu2_bwd_epi(dact, act):
    return dact * (2.0 * jnp.sqrt(act.astype(F32)))


def decay_fn(z, bd):
    zz = z.astype(F32) + bd
    ls = jnp.minimum(zz, 0.0) - jnp.log(1.0 + jnp.exp(jnp.minimum(zz, -zz)))
    la = ls / GLA_TAU
    gk = la.shape[1] // 2
    return la[:, :gk], la[:, gk:]


def glu_fn(a, b):
    return (a.astype(F32) * _sigmoid(b.astype(F32)),)


def glaout_fn(o_f, o_b, og, g):
    o = o_f + o_b
    dv = o.shape[1] // N_HEADS
    hs = []
    for h in range(N_HEADS):
        oh = o[:, h * dv:(h + 1) * dv]
        hs.append(oh * lax.rsqrt(jnp.mean(oh * oh, axis=-1, keepdims=True) + EPS) * g[:, h * dv:(h + 1) * dv])
    og = og.astype(F32)
    return ((jnp.concatenate(hs, axis=1) * (og * _sigmoid(og))).astype(MM_DTYPE),)


def convpost_fn(y, b_dw, g, b):
    y = y + b_dw
    mu = jnp.mean(y, axis=-1, keepdims=True)
    xc = y - mu
    yn = xc * lax.rsqrt(jnp.mean(xc * xc, axis=-1, keepdims=True) + EPS) * g + b
    return ((yn * _sigmoid(yn)).astype(MM_DTYPE),)


def poolpost_fn(pc, s):
    return ((pc.astype(F32) * s).astype(MM_DTYPE),)


def merge_fn(ya, yb, yc, mg, bg0, bg1, bg2):
    d = ya.shape[1]
    mg = mg.astype(F32)
    mixed = (_sigmoid(mg[:, :d] + bg0) * ya.astype(F32) + _sigmoid(mg[:, d:2 * d] + bg1) * yb.astype(F32)
             + _sigmoid(mg[:, 2 * d:] + bg2) * yc.astype(F32))
    return (mixed.astype(MM_DTYPE),)


def _split_dot(lmat, x, dims):
    hi = x.astype(MM_DTYPE)
    lo = x - hi.astype(F32)
    return _dot(lmat, hi, dims) + _dot(lmat, lo, dims)


def _gla_block_order(dm, rev):
    nctx, nb = dm.CTX // dm.TB, dm.T // dm.TB

    def blk(i):
        if not rev:
            return i
        return jnp.where(i < nctx, nctx - 1 - i, nb - 1 - (i - nctx))
    return blk, nb


def _gla_tri(rev):
    c = GLA_CHUNK
    t = lax.broadcasted_iota(jnp.int32, (c, c), 0)
    s = lax.broadcasted_iota(jnp.int32, (c, c), 1)
    return (s >= t) if rev else (s <= t)


def _gla_chunk_terms(q, k, la, tri, scale):
    lmat = tri.astype(MM_DTYPE)
    b = _split_dot(lmat, la, NN)
    bend = jnp.sum(la, axis=0, keepdims=True)
    eb = jnp.exp(b)
    enb = jnp.exp(-b)
    ee = jnp.exp(bend - b)
    qi = q * scale * eb
    ki = k * enb
    kend = k * ee
    att = jnp.where(tri, _dot(qi, ki, NT), 0.0)
    return lmat, bend, eb, enb, ee, qi, ki, kend, att


def gla_fwd(P, la, rev, dm, name):
    c, tb, h_, dk, dv, d = GLA_CHUNK, dm.TB, N_HEADS, dm.DK, dm.DV, dm.D
    cpb = tb // c
    blk, nb = _gla_block_order(dm, rev)
    gk, gv = h_ * dk, h_ * dv
    qb, kb, vb, lb = (5 * d) // gk, (5 * d + d // 2) // gk, (3 * d) // gv, (1 if rev else 0)
    scale = dk ** -0.5
    order = list(range(cpb))[::-1] if rev else list(range(cpb))

    def body(q_ref, k_ref, v_ref, la_ref, o_ref, s_ref, st):
        @pl.when(pl.program_id(0) == 0)
        def _():
            st[...] = jnp.zeros_like(st)
        tri = _gla_tri(rev)
        for n, ci in enumerate(order):
            r = pl.ds(ci * c, c)
            for hh in range(h_):
                ck, cv = pl.ds(hh * dk, dk), pl.ds(hh * dv, dv)
                q = q_ref[r, ck].astype(F32)
                k = k_ref[r, ck].astype(F32)
                v = v_ref[r, cv]
                _, bend, _, _, _, qi, _, kend, att = _gla_chunk_terms(q, k, la_ref[r, ck], tri, scale)
                s_in = st[hh]
                o_ref[r, cv] = _dot(att, v) + _dot(qi, s_in, NT)
                s_ref[n, hh] = s_in
                st[hh] = jnp.exp(bend) * s_in + _dot(v, kend, TN)

    return pl.pallas_call(
        body, name=name, grid=(nb,),
        in_specs=[pl.BlockSpec((tb, gk), lambda i: (blk(i), qb)),
                  pl.BlockSpec((tb, gk), lambda i: (blk(i), kb)),
                  pl.BlockSpec((tb, gv), lambda i: (blk(i), vb)),
                  pl.BlockSpec((tb, gk), lambda i: (blk(i), lb))],
        out_specs=[pl.BlockSpec((tb, gv), lambda i: (blk(i), 0)),
                   pl.BlockSpec((cpb, h_, dv, dk), lambda i: (i, 0, 0, 0))],
        out_shape=[jax.ShapeDtypeStruct((dm.T, gv), F32),
                   jax.ShapeDtypeStruct((dm.T // c, h_, dv, dk), F32)],
        scratch_shapes=[pltpu.VMEM((h_, dv, dk), F32)],
        compiler_params=_cparams(("arbitrary",)),
    )(P, P, P, la)


def gla_bwd(P, la, do, states, rev, dm, name):
    c, tb, h_, dk, dv, d = GLA_CHUNK, dm.TB, N_HEADS, dm.DK, dm.DV, dm.D
    cpb = tb // c
    blk, nb = _gla_block_order(dm, rev)
    gk, gv = h_ * dk, h_ * dv
    qb, kb, vb, lb = (5 * d) // gk, (5 * d + d // 2) // gk, (3 * d) // gv, (1 if rev else 0)
    scale = dk ** -0.5
    order = list(range(cpb))[::-1] if rev else list(range(cpb))

    def body(q_ref, k_ref, v_ref, la_ref, do_ref, s_ref, dq_ref, dk_ref, dv_ref, dla_ref, dst):
        @pl.when(pl.program_id(0) == 0)
        def _():
            dst[...] = jnp.zeros_like(dst)
        tri = _gla_tri(rev)
        for n in range(cpb - 1, -1, -1):
            r = pl.ds(order[n] * c, c)
            for hh in range(h_):
                ck, cv = pl.ds(hh * dk, dk), pl.ds(hh * dv, dv)
                q = q_ref[r, ck].astype(F32)
                k = k_ref[r, ck].astype(F32)
                v = v_ref[r, cv]
                lmat, bend, eb, enb, ee, qi, ki, kend, att = _gla_chunk_terms(q, k, la_ref[r, ck], tri, scale)
                s_in = s_ref[n, hh]
                ds_out = dst[hh]
                dob = do_ref[r, cv]
                datt = jnp.where(tri, _dot(dob, v, NT), 0.0)
                dqi = _dot(datt, ki) + _dot(dob, s_in)
                dki = _dot(datt, qi, TN)
                dv_ref[r, cv] = (_dot(att, dob, TN) + _dot(kend, ds_out, NT)).astype(dv_ref.dtype)
                dkend = _dot(v, ds_out)
                gam = jnp.exp(bend)
                dgam = jnp.sum(ds_out * s_in, axis=0, keepdims=True)
                dst[hh] = gam * ds_out + _dot(dob, qi, TN)
                dq_ref[r, ck] = (dqi * (scale * eb)).astype(dq_ref.dtype)
                dk_ref[r, ck] = (dki * enb + dkend * ee).astype(dk_ref.dtype)
                db = dqi * qi - dki * ki - dkend * kend
                dbend = jnp.sum(dkend * kend, axis=0, keepdims=True) + dgam * gam
                dla_ref[r, ck] = _split_dot(lmat, db, TN) + dbend

    def bi(j):
        return blk(nb - 1 - j)

    return pl.pallas_call(
        body, name=name, grid=(nb,),
        in_specs=[pl.BlockSpec((tb, gk), lambda j: (bi(j), qb)),
                  pl.BlockSpec((tb, gk), lambda j: (bi(j), kb)),
                  pl.BlockSpec((tb, gv), lambda j: (bi(j), vb)),
                  pl.BlockSpec((tb, gk), lambda j: (bi(j), lb)),
                  pl.BlockSpec((tb, gv), lambda j: (bi(j), 0)),
                  pl.BlockSpec((cpb, h_, dv, dk), lambda j: (nb - 1 - j, 0, 0, 0))],
        out_specs=[pl.BlockSpec((tb, gk), lambda j: (bi(j), 0)),
                   pl.BlockSpec((tb, gk), lambda j: (bi(j), 0)),
                   pl.BlockSpec((tb, gv), lambda j: (bi(j), 0)),
                   pl.BlockSpec((tb, gk), lambda j: (bi(j), 0))],
        out_shape=[jax.ShapeDtypeStruct((dm.T, gk), F32), jax.ShapeDtypeStruct((dm.T, gk), F32),
                   jax.ShapeDtypeStruct((dm.T, gv), F32), jax.ShapeDtypeStruct((dm.T, gk), F32)],
        scratch_shapes=[pltpu.VMEM((h_, dv, dk), F32)],
        compiler_params=_cparams(("arbitrary",)),
    )(P, P, P, la, do, states)


def _pos(n, period):
    t = lax.broadcasted_iota(jnp.int32, (n, 1), 0)
    if period & (period - 1) == 0:
        return jnp.bitwise_and(t, period - 1)
    return lax.rem(t, period)


def _conv_segments(dm):
    return [(0, dm.CTX, dm.CTX), (dm.CTX, dm.SEQ, GRID_W)]


def conv_fwd(u, w, dm, name):
    kw, cw = w.shape
    segs = _conv_segments(dm)

    def body(u_ref, w_ref, y_ref):
        for r0, n, per in segs:
            useg = u_ref[r0:r0 + n, :]
            p = _pos(n, per)
            acc = jnp.zeros_like(useg)
            for kk in range(kw):
                d = kk - kw // 2
                sh = useg if d == 0 else pltpu.roll(useg, (-d) % n, 0)
                ok = jnp.logical_and(p + d >= 0, p + d < per)
                acc = acc + jnp.where(ok, sh, 0.0) * w_ref[kk:kk + 1, :]
            y_ref[r0:r0 + n, :] = acc

    return pl.pallas_call(
        body, name=name, grid=(cw // LANES,),
        in_specs=[pl.BlockSpec((dm.T, LANES), lambda j: (0, j)), pl.BlockSpec((kw, LANES), lambda j: (0, j))],
        out_specs=pl.BlockSpec((dm.T, LANES), lambda j: (0, j)),
        out_shape=jax.ShapeDtypeStruct((dm.T, cw), F32),
        compiler_params=_cparams(("parallel",)),
    )(u, w)


def conv_bwd(u, w, dy, dm, name):
    kw, cw = w.shape
    segs = _conv_segments(dm)

    def body(u_ref, w_ref, dy_ref, du_ref, dw_ref):
        dws = [jnp.zeros((1, LANES), F32)] * kw
        for r0, n, per in segs:
            useg = u_ref[r0:r0 + n, :]
            dyseg = dy_ref[r0:r0 + n, :]
            p = _pos(n, per)
            acc = jnp.zeros_like(useg)
            for kk in range(kw):
                d = kk - kw // 2
                shu = useg if d == 0 else pltpu.roll(useg, (-d) % n, 0)
                okf = jnp.logical_and(p + d >= 0, p + d < per)
                dws[kk] = dws[kk] + jnp.sum(jnp.where(okf, shu, 0.0) * dyseg, axis=0, keepdims=True)
                shd = dyseg if d == 0 else pltpu.roll(dyseg, d % n, 0)
                okb = jnp.logical_and(p - d >= 0, p - d < per)
                acc = acc + jnp.where(okb, shd, 0.0) * w_ref[kk:kk + 1, :]
            du_ref[r0:r0 + n, :] = acc
        for kk in range(kw):
            dw_ref[kk:kk + 1, :] = dws[kk]

    return pl.pallas_call(
        body, name=name, grid=(cw // LANES,),
        in_specs=[pl.BlockSpec((dm.T, LANES), lambda j: (0, j)), pl.BlockSpec((kw, LANES), lambda j: (0, j)),
                  pl.BlockSpec((dm.T, LANES), lambda j: (0, j))],
        out_specs=[pl.BlockSpec((dm.T, LANES), lambda j: (0, j)), pl.BlockSpec((kw, LANES), lambda j: (0, j))],
        out_shape=[jax.ShapeDtypeStruct((dm.T, cw), F32), jax.ShapeDtypeStruct((kw, cw), F32)],
        compiler_params=_cparams(("parallel",)),
    )(u, w, dy)


def pool_mix(u, transpose, dm, name):
    u, uw, ublk = _rowspec(u)
    gc = dm.GC
    ng = len(POOL_WINDOWS)
    rows = dm.SEQ // GRID_W
    segs = [(0, dm.CTX, 1, dm.CTX), (dm.CTX, dm.SEQ, GRID_W, rows)]

    def one_group(u_ref, o_ref, win):
        left = win // 2
        right = win - 1 - left
        for r0, n, stride, length in segs:
            useg = u_ref[r0:r0 + n, :].astype(F32)
            t = lax.broadcasted_iota(jnp.int32, (n, 1), 0)
            p = t if stride == 1 else jnp.right_shift(t, stride.bit_length() - 1)
            cnt = (jnp.minimum(p + right + 1, length) - jnp.maximum(p - left, 0)).astype(F32)
            src = useg / cnt if transpose else useg
            acc = jnp.zeros_like(useg)
            for d in range(-left, right + 1):
                dd = -d if transpose else d
                sh = src if d == 0 else pltpu.roll(src, (-dd * stride) % n, 0)
                ok = jnp.logical_and(p + dd >= 0, p + dd < length)
                acc = acc + jnp.where(ok, sh, 0.0)
            o_ref[r0:r0 + n, :] = (acc - useg) if transpose else (acc / cnt - useg)

    def body(u_ref, o_ref):
        g = pl.program_id(0)
        for gi, win in enumerate(POOL_WINDOWS):
            @pl.when(g == gi)
            def _(win=win):
                one_group(u_ref, o_ref, win)

    base = ublk * (uw // gc)
    return pl.pallas_call(
        body, name=name, grid=(ng,),
        in_specs=[pl.BlockSpec((dm.T, gc), lambda g: (0, base + g))],
        out_specs=pl.BlockSpec((dm.T, gc), lambda g: (0, g)),
        out_shape=jax.ShapeDtypeStruct((dm.T, ng * gc), F32),
        compiler_params=_cparams(("parallel",)),
    )(u)


def loss_head(x2, target, dm, name):
    tm, d = dm.tm, dm.D
    nctx = dm.CTX // tm

    def body(x_ref, t_ref, dx_ref, l_ref):
        i = pl.program_id(0)

        @pl.when(i == 0)
        def _():
            l_ref[...] = jnp.zeros_like(l_ref)

        @pl.when(i < nctx)
        def _():
            dx_ref[...] = jnp.zeros_like(dx_ref)

        @pl.when(i >= nctx)
        def _():
            e = x_ref[...] - t_ref[...]
            dx_ref[...] = e / d
            l_ref[...] += jnp.full(l_ref.shape, 0.5 * jnp.sum(jnp.mean(e * e, axis=-1)), F32)

    return pl.pallas_call(
        body, name=name, grid=(dm.T // tm,),
        in_specs=[pl.BlockSpec((tm, d), lambda i: (i, 0)),
                  pl.BlockSpec((tm, d), lambda i: (jnp.maximum(i - nctx, 0), 0))],
        out_specs=[pl.BlockSpec((tm, d), lambda i: (i, 0)), pl.BlockSpec((8, LANES), lambda i: (0, 0))],
        out_shape=[jax.ShapeDtypeStruct((dm.T, d), F32), jax.ShapeDtypeStruct((8, LANES), F32)],
        compiler_params=_cparams(("arbitrary",)),
    )(x2, target)


def adamw(w, g, m, v, name):
    r, c = w.shape
    tr = _tile(r, tuple(t for t in (512, 256, 128, 64, 32, 16, 8) if t * c * 4 <= (1 << 20)) or (8,))

    def body(w_ref, g_ref, m_ref, v_ref, d_ref, mo_ref, vo_ref):
        gg = g_ref[...]
        mm = ADAM_B1 * m_ref[...] + (1.0 - ADAM_B1) * gg
        vv = ADAM_B2 * v_ref[...] + (1.0 - ADAM_B2) * (gg * gg)
        m_hat = mm / (1.0 - ADAM_B1 ** ADAM_STEP)
        v_hat = vv / (1.0 - ADAM_B2 ** ADAM_STEP)
        d_ref[...] = -ADAM_LR * (m_hat / (jnp.sqrt(v_hat) + ADAM_EPS) + ADAM_WD * w_ref[...])
        mo_ref[...] = mm
        vo_ref[...] = vv

    spec = pl.BlockSpec((tr, c), lambda i: (i, 0))
    return pl.pallas_call(
        body, name=name, grid=(r // tr,), in_specs=[spec] * 4, out_specs=[spec] * 3,
        out_shape=[jax.ShapeDtypeStruct((r, c), F32)] * 3,
        compiler_params=_cparams(("parallel",)),
    )(w, g, m, v)


def slot_sum(buf, name):
    s, r, c = buf.shape
    tr = _tile(r, (256, 128, 64, 32, 16, 8))

    def body(b_ref, o_ref):
        acc = b_ref[0].astype(F32)
        for k in range(1, s):
            acc = acc + b_ref[k].astype(F32)
        o_ref[...] = acc

    return pl.pallas_call(
        body, name=name, grid=(r // tr,),
        in_specs=[pl.BlockSpec((s, tr, c), lambda i: (0, i, 0))],
        out_specs=pl.BlockSpec((tr, c), lambda i: (i, 0)),
        out_shape=jax.ShapeDtypeStruct((r, c), F32),
        compiler_params=_cparams(("parallel",)),
    )(buf)


def pair_add(g, r1, cidx, name):
    ng, r_, n_ = r1.shape
    tr = _tile(r_, tuple(t for t in (1024, 512, 256, 128, 64, 32, 16) if t * n_ * 4 <= (2 << 20)))

    def body(s_ref, g_ref, r_ref, o_ref):
        o_ref[...] = (g_ref[...].astype(F32) + r_ref[...].astype(F32)).astype(o_ref.dtype)

    return pl.pallas_call(
        body, name=name,
        grid_spec=pltpu.PrefetchScalarGridSpec(
            num_scalar_prefetch=1, grid=(ng, r_ // tr),
            in_specs=[pl.BlockSpec((None, tr, n_), lambda k, i, s: (2 * k + s[0], i, 0)),
                      pl.BlockSpec((None, tr, n_), lambda k, i, s: (k, i, 0))],
            out_specs=pl.BlockSpec((None, tr, n_), lambda k, i, s: (k, i, 0))),
        out_shape=jax.ShapeDtypeStruct((ng, r_, n_), g.dtype),
        compiler_params=_cparams(("parallel", "parallel")),
    )(cidx, g, r1)


def chip_add(h, r2, axis, where, name, slab=True):
    _, kl, nl = r2.shape
    tr = _tile(kl, tuple(t for t in (1024, 512, 256, 128, 64, 32, 16) if t * nl * 4 <= (1 << 20)))
    nrb = kl // tr

    def body(s_ref, h_ref, r_ref, o_ref):
        acc = h_ref[...].astype(F32)
        for k in range(r2.shape[0]):
            acc = acc + r_ref[k].astype(F32)
        o_ref[...] = acc

    h_map = (lambda i, s: (s[0] * nrb + i, 0)) if axis == 0 else (lambda i, s: (i, s[0]))
    if slab:
        out_spec = pl.BlockSpec((None, tr, nl), lambda i, s: (s[1], i, 0))
        out_shape = jax.ShapeDtypeStruct((2, kl, nl), F32)
    else:
        out_spec = pl.BlockSpec((tr, nl), lambda i, s: (i, 0))
        out_shape = jax.ShapeDtypeStruct((kl, nl), F32)
    return pl.pallas_call(
        body, name=name,
        grid_spec=pltpu.PrefetchScalarGridSpec(
            num_scalar_prefetch=1, grid=(nrb,),
            in_specs=[pl.BlockSpec((tr, nl), h_map),
                      pl.BlockSpec((r2.shape[0], tr, nl), lambda i, s: (0, i, 0))],
            out_specs=out_spec),
        out_shape=out_shape,
        compiler_params=_cparams(("parallel",)),
    )(where, h, r2)


def adamw_layers(w, m, v, terms0, terms1, name):
    _, a_, b_ = w.shape
    tr = _tile(a_, tuple(t for t in (512, 256, 128, 64, 32) if t * b_ * 4 <= (1 << 20)))
    by_cols = tr == a_ and a_ * b_ * 4 > (1 << 20)
    blk = (a_, LANES) if by_cols else (tr, b_)
    steps = b_ // LANES if by_cols else a_ // tr
    at = (lambda i: (0, i)) if by_cols else (lambda i: (i, 0))
    n0 = len(terms0)

    def update(g, w_ref, m_ref, v_ref, g_ref, d_ref, mo_ref, vo_ref):
        mm = ADAM_B1 * m_ref[...] + (1.0 - ADAM_B1) * g
        vv = ADAM_B2 * v_ref[...] + (1.0 - ADAM_B2) * (g * g)
        m_hat = mm / (1.0 - ADAM_B1 ** ADAM_STEP)
        v_hat = vv / (1.0 - ADAM_B2 ** ADAM_STEP)
        g_ref[...] = g
        d_ref[...] = -ADAM_LR * (m_hat / (jnp.sqrt(v_hat) + ADAM_EPS) + ADAM_WD * w_ref[...])
        mo_ref[...] = mm
        vo_ref[...] = vv

    def total(refs):
        g = refs[0][...]
        for r in refs[1:]:
            g = g + r[...]
        return g

    def body(w_ref, m_ref, v_ref, *rest):
        t_refs, outs = rest[:-4], rest[-4:]
        layer = pl.program_id(0)

        @pl.when(layer == 0)
        def _():
            update(total(t_refs[:n0]), w_ref, m_ref, v_ref, *outs)

        @pl.when(layer == 1)
        def _():
            update(total(t_refs[n0:]), w_ref, m_ref, v_ref, *outs)

    stacked = pl.BlockSpec((None,) + blk, lambda l, i: (l,) + at(i))
    return pl.pallas_call(
        body, name=name, grid=(2, steps),
        in_specs=[stacked] * 3 + [pl.BlockSpec(blk, lambda l, i: at(i * (1 - l)))] * n0
        + [pl.BlockSpec(blk, lambda l, i: at(i * l))] * len(terms1),
        out_specs=[stacked] * 4, out_shape=[jax.ShapeDtypeStruct(w.shape, F32)] * 4,
        compiler_params=_cparams(("arbitrary", "arbitrary")),
    )(w, m, v, *terms0, *terms1)


MESH = pl.DeviceIdType.MESH
ANY = pl.BlockSpec(memory_space=pl.ANY)
HBM = pl.BlockSpec(memory_space=pltpu.HBM)
SEM = pl.BlockSpec(memory_space=pltpu.SEMAPHORE)
EFFECT = pltpu.SideEffectType.DATAFLOW_SIDE_EFFECTING


def _place():
    return lax.axis_index("x"), lax.axis_index("y"), lax.axis_index("c")


def _peers(x, y):
    return [(1 - x, y), (x, 1 - y), (1 - x, 1 - y)]


def _rcopy(src, dst, ssem, rsem, dev):
    return pltpu.make_async_remote_copy(src_ref=src, dst_ref=dst, send_sem=ssem, recv_sem=rsem,
                                        device_id=dev, device_id_type=MESH)


def _gathered_shape(src, kind):
    h, a_, b_ = src.shape
    return (h, a_, 4 * b_) if kind == 'col' else (4, h, a_, b_)


def _win(ref, kind, ch, width):
    return ref.at[:, :, pl.ds(ch * width, width)] if kind == 'col' else ref.at[ch]


def _rect(ref, kind, half, ch, width):
    return ref.at[half, :, pl.ds(ch * width, width)] if kind == 'col' else ref.at[ch, half]


def gather_halves(srcs, kinds, name):
    nw = len(srcs)
    widths = [s.shape[2] for s in srcs]

    def body(*refs):
        src, out = refs[:nw], refs[nw:2 * nw]
        ssem, rsem, osend, orecv = refs[2 * nw:]
        x, y, c = _place()
        chip = 2 * x + y
        sib = (x, y, 1 - c)
        peers = _peers(x, y)
        pidx = [2 * px + py for px, py in peers]

        def rect(n, half, ch):
            return _rect(out[n], kinds[n], half, ch, widths[n])

        mine = [_rcopy(src[n], _win(out[n], kinds[n], chip, widths[n]), osend.at[n], orecv.at[n], sib)
                for n in range(nw)]
        first = [[_rcopy(src[n].at[c], rect(n, c, chip), ssem.at[6 * n + k], rsem.at[6 * n + k], (px, py, c))
                  for k, (px, py) in enumerate(peers)] for n in range(nw)]
        for n in range(nw):
            for cp in first[n]:
                cp.start()
        for cp in mine:
            cp.start()
        passed = [[_rcopy(rect(n, c, pidx[k]), rect(n, c, pidx[k]), ssem.at[6 * n + 3 + k], rsem.at[6 * n + 3 + k], sib)
                   for k in range(3)] for n in range(nw)]
        for n in range(nw):
            for k, (px, py) in enumerate(peers):
                _rcopy(rect(n, c, pidx[k]), rect(n, c, pidx[k]), ssem.at[6 * n + k], rsem.at[6 * n + k],
                       (px, py, c)).wait_recv()
                passed[n][k].start()
        for n in range(nw):
            for k in range(3):
                _rcopy(rect(n, 1 - c, pidx[k]), rect(n, 1 - c, pidx[k]), ssem.at[6 * n + 3 + k],
                       rsem.at[6 * n + 3 + k], sib).wait_recv()
        for n in range(nw):
            for cp in first[n] + passed[n]:
                cp.wait_send()
        for cp in mine:
            cp.wait()

    return pl.pallas_call(
        body, name=name, in_specs=[ANY] * nw, out_specs=[ANY] * nw,
        out_shape=[jax.ShapeDtypeStruct(_gathered_shape(s, k), s.dtype) for s, k in zip(srcs, kinds)],
        scratch_shapes=[pltpu.SemaphoreType.DMA((6 * nw,)), pltpu.SemaphoreType.DMA((6 * nw,)),
                        pltpu.SemaphoreType.DMA((nw,)), pltpu.SemaphoreType.DMA((nw,))],
    )(*srcs)


def _gather_plan(kinds, widths):
    def plan(src, land, x, y, c):
        chip = 2 * x + y
        out = []
        for n in range(len(src)):
            mine = _win(land[n], kinds[n], chip, widths[n])
            for px, py in _peers(x, y):
                out.append((src[n], mine, (px, py, c), _win(land[n], kinds[n], 2 * px + py, widths[n])))
            out.append((src[n], mine, (x, y, 1 - c), mine))
        return out
    return plan


def _scatter_plan(axes, widths):
    def plan(src, land, x, y, c):
        out = []
        for n in range(len(src)):
            for k, (px, py) in enumerate(_peers(x, y)):
                ch = 2 * px + py
                view = (src[n].at[:, pl.ds(ch * widths[n], widths[n])] if axes[n] == 1
                        else src[n].at[pl.ds(ch * widths[n], widths[n]), :])
                out.append((view, land[n].at[k], (px, py, c), land[n].at[k]))
        return out
    return plan


def start_copies(srcs, lands, plan, ncopies, after, name):
    ns, nl = len(srcs), len(lands)

    def body(*refs):
        src, land = refs[:ns], refs[ns:ns + nl]
        ssem, rsem = refs[ns + nl + 1], refs[ns + nl + 2]
        token = refs[-1]
        x, y, c = _place()
        for k, (sv, dv, dev, _) in enumerate(plan(src, land, x, y, c)):
            _rcopy(sv, dv, ssem.at[k], rsem.at[k], dev).start()
        token[...] = jnp.zeros_like(token)

    hbm = lambda t: pltpu.HBM(t.shape, t.dtype)
    res = pl.pallas_call(
        body, name=name,
        out_shape=(pltpu.SemaphoreType.DMA((ncopies,)), pltpu.SemaphoreType.DMA((ncopies,)),
                   *[hbm(t) for t in srcs], *[hbm(t) for t in lands], jax.ShapeDtypeStruct((8, LANES), F32)),
        in_specs=[HBM] * (ns + nl) + [ANY],
        out_specs=(SEM, SEM, *[HBM] * (ns + nl), pl.BlockSpec(memory_space=pltpu.VMEM)),
        input_output_aliases={k: 2 + k for k in range(ns + nl)},
        compiler_params=pltpu.CompilerParams(has_side_effects=EFFECT),
    )(*[pltpu.with_memory_space_constraint(t, pltpu.HBM) for t in list(srcs) + list(lands)], after)
    return res[0], res[1], list(res[2:2 + ns]), list(res[2 + ns:2 + ns + nl]), res[-1]


def wait_copies(ssem, rsem, srcs, lands, plan, after, name):
    ns, nl = len(srcs), len(lands)

    def body(*refs):
        src, land = refs[:ns], refs[ns:ns + nl]
        ss, rs = refs[ns + nl], refs[ns + nl + 1]
        x, y, c = _place()
        for k, (sv, dv, dev, mine) in enumerate(plan(src, land, x, y, c)):
            cp = _rcopy(sv, mine, ss.at[k], rs.at[k], dev)
            cp.wait_send()
            cp.wait_recv()

    hbm = lambda t: pltpu.HBM(t.shape, t.dtype)
    res = pl.pallas_call(
        body, name=name,
        out_shape=(*[hbm(t) for t in srcs], *[hbm(t) for t in lands]),
        in_specs=[HBM] * (ns + nl) + [SEM, SEM, ANY], out_specs=tuple([HBM] * (ns + nl)),
        input_output_aliases={k: k for k in range(ns + nl)},
        compiler_params=pltpu.CompilerParams(has_side_effects=EFFECT),
    )(*srcs, *lands, ssem, rsem, after)
    return list(res[ns:])


def pair_swap_halves(gs, kinds, name):
    nw = len(gs)

    def other(ref, kind, half):
        return ref.at[half] if kind == 'col' else ref.at[:, half]

    def body(*refs):
        g, o = refs[:nw], refs[nw:2 * nw]
        ssem, rsem = refs[2 * nw:]
        x, y, c = _place()
        cps = [_rcopy(other(g[n], kinds[n], 1 - c), o[n], ssem.at[n], rsem.at[n], (x, y, 1 - c)) for n in range(nw)]
        for cp in cps:
            cp.start()
        for cp in cps:
            cp.wait()

    return pl.pallas_call(
        body, name=name, in_specs=[ANY] * nw, out_specs=[ANY] * nw,
        out_shape=[jax.ShapeDtypeStruct(g.shape[1:] if k == 'col' else (g.shape[0],) + g.shape[2:], g.dtype)
                   for g, k in zip(gs, kinds)],
        scratch_shapes=[pltpu.SemaphoreType.DMA((nw,)), pltpu.SemaphoreType.DMA((nw,))],
    )(*gs)


def pair_swap(fs, name):
    nw = len(fs)

    def body(*refs):
        f, o = refs[:nw], refs[nw:2 * nw]
        ssem, rsem = refs[2 * nw:]
        x, y, c = _place()
        cps = [_rcopy(f[n], o[n], ssem.at[n], rsem.at[n], (x, y, 1 - c)) for n in range(nw)]
        for cp in cps:
            cp.start()
        for cp in cps:
            cp.wait()

    return pl.pallas_call(
        body, name=name, in_specs=[ANY] * nw, out_specs=[ANY] * nw,
        out_shape=[jax.ShapeDtypeStruct(f.shape, f.dtype) for f in fs],
        scratch_shapes=[pltpu.SemaphoreType.DMA((nw,)), pltpu.SemaphoreType.DMA((nw,))],
    )(*fs)


def chip_exchange(hs, kinds, name):
    nw = len(hs)
    shp = [(h.shape[0], h.shape[1] // 4) if k == 'col' else h.shape[1:] for h, k in zip(hs, kinds)]

    def body(*refs):
        h, o = refs[:nw], refs[nw:2 * nw]
        ssem, rsem = refs[2 * nw:]
        x, y, c = _place()

        def win(n, ch):
            return h[n].at[:, pl.ds(ch * shp[n][1], shp[n][1])] if kinds[n] == 'col' else h[n].at[ch]

        cps = [_rcopy(win(n, 2 * px + py), o[n].at[k], ssem.at[3 * n + k], rsem.at[3 * n + k], (px, py, c))
               for n in range(nw) for k, (px, py) in enumerate(_peers(x, y))]
        for cp in cps:
            cp.start()
        for cp in cps:
            cp.wait()

    return pl.pallas_call(
        body, name=name, in_specs=[ANY] * nw, out_specs=[ANY] * nw,
        out_shape=[jax.ShapeDtypeStruct((3,) + sh, h.dtype) for sh, h in zip(shp, hs)],
        scratch_shapes=[pltpu.SemaphoreType.DMA((3 * nw,)), pltpu.SemaphoreType.DMA((3 * nw,))],
    )(*hs)


def pair_join_layers(fs, name):
    nw = len(fs)

    def body(*refs):
        o = refs[nw:2 * nw]
        ssem, rsem = refs[2 * nw:]
        x, y, c = _place()
        sib = (x, y, 1 - c)
        cps = [_rcopy(o[n].at[c], o[n].at[c], ssem.at[n], rsem.at[n], sib) for n in range(nw)]
        for cp in cps:
            cp.start()
        for n in range(nw):
            cps[n].wait_send()
            _rcopy(o[n].at[1 - c], o[n].at[1 - c], ssem.at[n], rsem.at[n], sib).wait_recv()

    return pl.pallas_call(
        body, name=name, in_specs=[ANY] * nw, out_specs=[ANY] * nw,
        out_shape=[jax.ShapeDtypeStruct(f.shape, f.dtype) for f in fs],
        input_output_aliases={n: n for n in range(nw)},
        scratch_shapes=[pltpu.SemaphoreType.DMA((nw,)), pltpu.SemaphoreType.DMA((nw,))],
    )(*fs)


def gather_all_devices(buf, name):
    r, c_ = buf.shape
    offs = [o for o in itertools.product((0, 1), repeat=3) if o != (0, 0, 0)]

    def body(b_ref, o_ref, ssem, rsem, lsem):
        x, y, c = _place()
        me = 4 * x + 2 * y + c
        mine = pltpu.make_async_copy(b_ref, o_ref.at[me], lsem)
        mine.start()
        peers = [((x + dx) % 2, (y + dy) % 2, (c + dc) % 2) for dx, dy, dc in offs]
        cps = [_rcopy(b_ref, o_ref.at[me], ssem.at[k], rsem.at[k], p) for k, p in enumerate(peers)]
        for cp in cps:
            cp.start()
        for k, (px, py, pc) in enumerate(peers):
            _rcopy(b_ref, o_ref.at[4 * px + 2 * py + pc], ssem.at[k], rsem.at[k], (px, py, pc)).wait_recv()
        for cp in cps:
            cp.wait_send()
        mine.wait()

    return pl.pallas_call(
        body, name=name, in_specs=[ANY], out_specs=ANY,
        out_shape=jax.ShapeDtypeStruct((8, r, c_), buf.dtype),
        scratch_shapes=[pltpu.SemaphoreType.DMA((7,)), pltpu.SemaphoreType.DMA((7,)), pltpu.SemaphoreType.DMA],
    )(buf)


def _flatten_pad(parts, dtype):
    flat = jnp.concatenate([p.reshape(-1).astype(dtype) for p in parts])
    q = 512 * LANES
    n = -(-flat.shape[0] // q) * q
    return jnp.pad(flat, (0, n - flat.shape[0])).reshape(n // LANES, LANES)


def _lane_pad(n):
    return -(-n // LANES) * LANES


def _in_proj_layout(d):
    gk, gv, cw, pw = d // 2, d, d // 2, d // 2
    own = [('q', gk), ('k', gk), ('v', gv), ('og', gv), ('lrf', GLA_LR), ('lrb', GLA_LR), ('ga', cw), ('gb', cw),
           ('pu', pw), ('mg', 3 * d)]
    padded = [('mg', 3 * d), ('v', gv), ('og', gv), ('q', gk), ('k', gk), ('ga', cw), ('gb', cw), ('pu', pw),
              ('lrf', GLA_LR), ('lrb', GLA_LR), ('pad', d // 2 - 2 * GLA_LR)]
    return own, padded


def _row_pieces(src, lo, hi, wl, wlp):
    out = []
    for k in range(4):
        s0, s1 = max(lo, k * wl), min(hi, (k + 1) * wl)
        if s0 < s1:
            out.append(src[k * wlp + s0 - k * wl:k * wlp + s1 - k * wl])
    return out


def _w_in_t_to_proj(g, d, wl, wlp):
    own, padded = _in_proj_layout(d)
    at, start = {}, 0
    for n, wd in own:
        at[n] = (start, start + wd)
        start += wd
    parts = []
    for n, wd in padded:
        parts += [jnp.zeros((wd, g.shape[1]), g.dtype)] if n == 'pad' else _row_pieces(g, *at[n], wl, wlp)
    return jnp.concatenate(parts, axis=0)


def _proj_to_w_in_t(gp, d, wl, wlp):
    own, padded = _in_proj_layout(d)
    pat, start = {}, 0
    for n, wd in padded:
        pat[n] = start
        start += wd
    parts = []
    for k in range(4):
        start = 0
        for n, wd in own:
            s0, s1 = max(start, k * wl), min(start + wd, (k + 1) * wl)
            if s0 < s1:
                parts.append(gp[pat[n] + s0 - start:pat[n] + s1 - start])
            start += wd
        parts.append(jnp.zeros((wlp - wl, gp.shape[1]), gp.dtype))
    return jnp.concatenate(parts, axis=0)


def _silu_grad(z):
    s = jax.nn.sigmoid(z)
    return s + z * s * (1.0 - s)


def kernel(x, c, ctx, c_ctx, w_ada, b_ada, g_pre_mix, g_post_mix, g_pre_mlp, g_post_mlp, w_in, w_decay, b_decay, g_gla, w_gla_o, w_dw, b_dw, g_conv_ln, b_conv_ln, w_conv_o, w_pool_g, s_pool, w_pool_o, b_gate, w_out, w_mlp1, w_mlp2, loss_target, m_c_ctx, m_w_ada, m_b_ada, m_g_pre_mix, m_g_post_mix, m_g_pre_mlp, m_g_post_mlp, m_w_in, m_w_decay, m_b_decay, m_g_gla, m_w_gla_o, m_w_dw, m_b_dw, m_g_conv_ln, m_b_conv_ln, m_w_conv_o, m_w_pool_g, m_s_pool, m_w_pool_o, m_b_gate, m_w_out, m_w_mlp1, m_w_mlp2, v_c_ctx, v_w_ada, v_b_ada, v_g_pre_mix, v_g_post_mix, v_g_pre_mlp, v_g_post_mlp, v_w_in, v_w_decay, v_b_decay, v_g_gla, v_w_gla_o, v_w_dw, v_b_dw, v_g_conv_ln, v_b_conv_ln, v_w_conv_o, v_w_pool_g, v_s_pool, v_w_pool_o, v_b_gate, v_w_out, v_w_mlp1, v_w_mlp2):
    a = dict(locals())
    for n in ('w_in', 'm_w_in', 'v_w_in'):
        a[n] = jnp.swapaxes(a[n], 1, 2)
    big_axis = dict(BIG, w_in=1)
    depth = w_in.shape[0]
    d = x.shape[-1]
    seq, nctx_rows = x.shape[1], ctx.shape[1]
    dm = types.SimpleNamespace(
        D=d, SEQ=seq, CTX=nctx_rows, T=seq + nctx_rows, DK=d // 8, DV=d // 4, GK=d // 2, GC=d // 8,
        tm=_tile(nctx_rows, (256, 128, 64)), TB=_tile(nctx_rows, (256, 128, 64)))
    assert dm.SEQ % dm.tm == 0 and dm.SEQ % GRID_W == 0 and dm.CTX % GLA_CHUNK == 0
    tmw = min(dm.tm, 128)
    chip = 2 * lax.axis_index("x") + lax.axis_index("y")
    core = lax.axis_index("c")
    chip1 = chip.astype(jnp.int32).reshape(1)
    core1 = core.astype(jnp.int32).reshape(1)

    big_names, small_names = list(BIG), list(SMALL_SHARDED)
    nbig = len(big_names)
    kinds = ['col' if big_axis[n] == 2 else 'row' for n in big_names]
    wl = w_in.shape[2]
    wlp = _lane_pad(wl)

    def rows8(t):
        t = t.reshape(t.shape[0], -1, t.shape[-1])
        return jnp.pad(t, ((0, 0), (0, -t.shape[1] % 8), (0, 0)))

    def halves(t):
        return t.reshape(2, t.shape[0] // 2, t.shape[1])

    def layer_src(l):
        return [halves((jnp.pad(a[n][l], ((0, wlp - wl), (0, 0))) if n == 'w_in' else a[n][l]).astype(MM_DTYPE))
                for n in big_names]

    def whole(t):
        return t.reshape(-1, t.shape[-1])

    late = [big_names.index(n) for n in ('w_mlp1', 'w_mlp2')]
    early = [k for k in range(nbig) if k not in late]
    src0, src1 = layer_src(0), layer_src(1)
    g0 = gather_halves([src0[k] for k in early] + [rows8(a[n]) for n in small_names],
                       [kinds[k] for k in early] + ['col'] * len(small_names), "gather_layer0")

    def start_gather(srcs, knds, after, name):
        plan = _gather_plan(knds, [t.shape[2] for t in srcs])
        lands = [lax.empty(_gathered_shape(t, k), t.dtype) for t, k in zip(srcs, knds)]
        return (plan,) + start_copies(srcs, lands, plan, 4 * len(srcs), after, name)

    ag0 = start_gather([src0[k] for k in late], [kinds[k] for k in late], g0[0], "gather_layer0_mlp_start")
    ag1 = start_gather(src1, kinds, ag0[-1], "gather_layer1_start")
    ag_token = ag1[-1]
    full = {n: [None, None] for n in big_names}
    for k, t in zip(early, g0):
        full[big_names[k]][0] = whole(t)
    for n, g in zip(small_names, g0[len(early):]):
        shp = a[n].shape
        full[n] = g[:, :math.prod(shp[1:-1])].reshape(shp[:-1] + (4 * shp[-1],))
    for n in SMALL:
        if n not in SMALL_SHARDED:
            full[n] = a[n]

    cvec = jnp.concatenate([c_ctx.reshape(1, d), c.reshape(1, d), jnp.zeros((6, d), F32)], axis=0)
    avec = (cvec * jax.nn.sigmoid(cvec) + ag_token[0, 0]).astype(MM_DTYPE)

    def row(v):
        return v.reshape(1, -1)

    X = jnp.concatenate([ctx[0], x[0]], axis=0)
    saved = []
    gk, gv = dm.GK, d
    lrblk = (7 * d + d // 2) // LANES
    for l in range(depth):
        if l == 1:
            got = wait_copies(ag1[1], ag1[2], ag1[3], ag1[4], ag1[0], X, "gather_layer1_wait")
            for n, t in zip(big_names, got):
                full[n][1] = whole(t)
        s = types.SimpleNamespace()
        s.w_in_p = _w_in_t_to_proj(full['w_in'][l], d, wl, wlp)
        wd = full['w_decay'][l]
        wdp = jnp.zeros((LANES, 2 * gk), F32)
        wdp = wdp.at[:GLA_LR, :gk].set(wd[0]).at[GLA_LR:2 * GLA_LR, gk:].set(wd[1])
        s.wdp = wdp.astype(MM_DTYPE)
        s.bd = full['b_decay'][l].reshape(1, 2 * gk)
        modraw = matmul(avec, full['w_ada'][l], 'nn', F32, f"mod_{l}") + full['b_ada'][l][None, :]
        s.mod = [modraw[0:2, j * d:(j + 1) * d].reshape(2, 1, d) for j in range(6)]
        s.x = X
        (s.h,) = rowwise(pre_fn, [X], s.mod[0:2], [row(g_pre_mix[l])], [(d, MM_DTYPE)], dm, f"pre_{l}")
        s.P = matmul(s.h, s.w_in_p, 'nt', MM_DTYPE, f"in_proj_{l}")
        P = s.P
        s.z = matmul((P, LANES, lrblk), s.wdp, 'nn', F32, f"decay_proj_{l}", tk=LANES)
        la_f, la_b = rowwise(decay_fn, [s.z], [], [s.bd], [(gk, F32), (gk, F32)], dm, f"decay_{l}")
        s.la = jnp.concatenate([la_f, la_b], axis=1)
        s.o_f, s.st_f = gla_fwd(P, s.la, False, dm, f"gla_fwd_f_{l}")
        s.o_b, s.st_b = gla_fwd(P, s.la, True, dm, f"gla_fwd_b_{l}")
        (s.gin,) = rowwise(glaout_fn, [s.o_f, s.o_b, (P, d, 4)], [], [row(g_gla[l])], [(gv, MM_DTYPE)], dm,
                           f"gla_out_{l}")
        s.ya = matmul(s.gin, full['w_gla_o'][l], 'nn', F32, f"gla_o_{l}")
        (s.u,) = rowwise(glu_fn, [(P, d // 2, 12), (P, d // 2, 13)], [], [], [(d // 2, F32)], dm, f"glu_{l}")
        s.yconv = conv_fwd(s.u, full['w_dw'][l], dm, f"conv_{l}")
        (s.cin,) = rowwise(convpost_fn, [s.yconv], [], [row(b_dw[l]), row(g_conv_ln[l]), row(b_conv_ln[l])],
                           [(d // 2, MM_DTYPE)], dm, f"conv_post_{l}")
        s.yb = matmul(s.cin, full['w_conv_o'][l], 'nn', F32, f"conv_o_{l}")
        s.pm = pool_mix((P, d // 2, 14), False, dm, f"pool_mix_{l}")
        s.pc = group_mm(s.pm, w_pool_g[l], 'nn', F32, f"pool_g_{l}")
        (s.pin,) = rowwise(poolpost_fn, [s.pc], [], [row(s_pool[l])], [(d // 2, MM_DTYPE)], dm, f"pool_post_{l}")
        s.yc = matmul(s.pin, full['w_pool_o'][l], 'nn', F32, f"pool_o_{l}")
        s.bg = [row(full['b_gate'][l][j]) for j in range(3)]
        (s.mixed,) = rowwise(merge_fn, [s.ya, s.yb, s.yc, (P, 3 * d, 0)], [], s.bg, [(d, MM_DTYPE)], dm,
                             f"merge_{l}", tm=tmw)
        s.y = matmul(s.mixed, full['w_out'][l], 'nn', F32, f"out_proj_{l}")
        if l == 0:
            got = wait_copies(ag0[1], ag0[2], ag0[3], ag0[4], ag0[0], s.y, "gather_layer0_mlp_wait")
            for k, t in zip(late, got):
                full[big_names[k]][0] = whole(t)
        s.x1, s.h2 = rowwise(mid_fn, [X, s.y], s.mod[2:5], [row(g_post_mix[l]), row(g_pre_mlp[l])],
                             [(d, F32), (d, MM_DTYPE)], dm, f"mid_{l}")
        s.act = matmul(s.h2, full['w_mlp1'][l], 'nn', MM_DTYPE, f"mlp1_{l}", epi=relu2_epi)
        s.y2 = matmul(s.act, full['w_mlp2'][l], 'nn', F32, f"mlp2_{l}")
        (X,) = rowwise(post_fn, [s.x1, s.y2], s.mod[5:6], [row(g_post_mlp[l])], [(d, F32)], dm, f"post_{l}")
        saved.append(s)

    dX, lossv = loss_head(X, loss_target[0], dm, "loss_head")
    loss = lax.psum(lossv[0, 0], ("x", "y", "c"))

    grads = {n: [None] * depth for n in WEIGHTS if n != 'c_ctx' and n not in BIG}
    gbig = {n: [None] * depth for n in BIG}
    rs_token = None

    def start_scatter(idx, layer, after, name):
        gs = [gbig[big_names[k]][layer] for k in idx]
        wd = [t.shape[1] // 4 if kinds[k] == 'col' else t.shape[0] // 4 for t, k in zip(gs, idx)]
        plan = _scatter_plan([big_axis[big_names[k]] - 1 for k in idx], wd)
        lands = [lax.empty((3, t.shape[0], w) if kinds[k] == 'col' else (3, w, t.shape[1]), t.dtype)
                 for t, w, k in zip(gs, wd, idx)]
        return (plan,) + start_copies(gs, lands, plan, 3 * len(gs), after, name)

    g_cctx = jnp.zeros((d,), F32)
    for l in reversed(range(depth)):
        s = saved[l]
        P = s.P
        dmod = [None] * 6
        gpm = row(g_post_mlp[l]) if rs_token is None else row(g_post_mlp[l]) + rs_token[0, 0]
        (dx1, dy2), (dmod[5],), (dg,) = rowwise_vjp(post_fn, [s.x1, s.y2], s.mod[5:6], [gpm], [dX],
                                                     dm, f"post_bwd_{l}", narrow=(1,))
        grads['g_post_mlp'][l] = dg[0]
        du1 = matmul(dy2, full['w_mlp2'][l], 'nt', MM_DTYPE, f"mlp2_dx_{l}", epi=relu2_bwd_epi, extras=[s.act])
        gbig['w_mlp2'][l] = matmul(s.act, dy2, 'tn', MM_DTYPE, f"mlp2_dw_{l}")
        dh2 = matmul(du1, full['w_mlp1'][l], 'nt', MM_DTYPE, f"mlp1_dx_{l}")
        gbig['w_mlp1'][l] = matmul(s.h2, du1, 'tn', MM_DTYPE, f"mlp1_dw_{l}")
        gpx = row(g_post_mix[l])
        if l == 0:
            rs0 = start_scatter(late, 0, dh2, "grad_layer0_mlp_start")
            gpx = gpx + rs0[-1][0, 0]
        (dxa, dy), dmod[2:5], (dg1, dg2) = rowwise_vjp(
            mid_fn, [s.x, s.y], s.mod[2:5], [gpx, row(g_pre_mlp[l])], [dx1, dh2], dm, f"mid_bwd_{l}", narrow=(1,))
        grads['g_post_mix'][l], grads['g_pre_mlp'][l] = dg1[0], dg2[0]
        dmixed = matmul(dy, full['w_out'][l], 'nt', MM_DTYPE, f"out_proj_dx_{l}")
        gbig['w_out'][l] = matmul(s.mixed, dy, 'tn', MM_DTYPE, f"out_proj_dw_{l}")
        (dya, dyb, dyc, dmg), _, dbg = rowwise_vjp(merge_fn, [s.ya, s.yb, s.yc, (P, 3 * d, 0)], [], s.bg, [dmixed],
                                                   dm, f"merge_bwd_{l}", tm=tmw, narrow=(0, 1, 2))
        grads['b_gate'][l] = jnp.concatenate(dbg, axis=0)
        dgin = matmul(dya, full['w_gla_o'][l], 'nt', MM_DTYPE, f"gla_o_dx_{l}")
        gbig['w_gla_o'][l] = matmul(s.gin, dya, 'tn', MM_DTYPE, f"gla_o_dw_{l}")
        dcin = matmul(dyb, full['w_conv_o'][l], 'nt', MM_DTYPE, f"conv_o_dx_{l}")
        gbig['w_conv_o'][l] = matmul(s.cin, dyb, 'tn', MM_DTYPE, f"conv_o_dw_{l}")
        dpin = matmul(dyc, full['w_pool_o'][l], 'nt', MM_DTYPE, f"pool_o_dx_{l}")
        gbig['w_pool_o'][l] = matmul(s.pin, dyc, 'tn', MM_DTYPE, f"pool_o_dw_{l}")
        (dpc,), _, (dsp,) = rowwise_vjp(poolpost_fn, [s.pc], [], [row(s_pool[l])], [dpin], dm, f"pool_post_bwd_{l}")
        grads['s_pool'][l] = dsp[0]
        grads['w_pool_g'][l] = group_mm(s.pm, w_pool_g[l], 'tn', F32, f"pool_g_dw_{l}", b=dpc)
        dpm = group_mm(dpc, w_pool_g[l], 'nt', F32, f"pool_g_dx_{l}")
        dpu = pool_mix(dpm, True, dm, f"pool_mix_bwd_{l}")
        (dyconv,), _, (dbdw, dgln, dbln) = rowwise_vjp(
            convpost_fn, [s.yconv], [], [row(b_dw[l]), row(g_conv_ln[l]), row(b_conv_ln[l])], [dcin], dm,
            f"conv_post_bwd_{l}")
        grads['b_dw'][l], grads['g_conv_ln'][l], grads['b_conv_ln'][l] = dbdw[0], dgln[0], dbln[0]
        du, grads['w_dw'][l] = conv_bwd(s.u, full['w_dw'][l], dyconv, dm, f"conv_bwd_{l}")
        (dga, dgb), _, _ = rowwise_vjp(glu_fn, [(P, d // 2, 12), (P, d // 2, 13)], [], [], [du], dm, f"glu_bwd_{l}")
        (do, _, dog), _, (dgg,) = rowwise_vjp(glaout_fn, [s.o_f, s.o_b, (P, d, 4)], [], [row(g_gla[l])], [dgin], dm,
                                              f"gla_out_bwd_{l}", want=[True, False, True])
        grads['g_gla'][l] = dgg[0]
        dqf, dkf, dvf, dlaf = gla_bwd(P, s.la, do, s.st_f, False, dm, f"gla_bwd_f_{l}")
        dqb, dkb, dvb, dlab = gla_bwd(P, s.la, do, s.st_b, True, dm, f"gla_bwd_b_{l}")
        (dz,), _, (dbd,) = rowwise_vjp(decay_fn, [s.z], [], [s.bd], [dlaf, dlab], dm, f"decay_bwd_{l}", narrow=(0,))
        grads['b_decay'][l] = dbd.reshape(2, gk)
        dwdp = matmul((P, LANES, lrblk), dz, 'tn', F32, f"decay_proj_dw_{l}", tm=LANES)
        grads['w_decay'][l] = jnp.stack([dwdp[:GLA_LR, :gk], dwdp[GLA_LR:2 * GLA_LR, gk:]])
        dlr = matmul(dz, s.wdp, 'nt', F32, f"decay_proj_dx_{l}")

        def asm_fn(dmg_, dvf_, dvb_, dog_, dqf_, dqb_, dkf_, dkb_, dga_, dgb_, dpu_, dlr_):
            f = lambda t: t.astype(F32)
            pad = jnp.zeros((dlr_.shape[0], d // 2 - LANES), F32)
            return (jnp.concatenate([f(dmg_), dvf_ + dvb_, f(dog_), dqf_ + dqb_, dkf_ + dkb_, f(dga_), f(dgb_),
                                     dpu_, dlr_, pad], axis=1).astype(MM_DTYPE),)
        (dP,) = rowwise(asm_fn, [dmg, dvf, dvb, dog, dqf, dqb, dkf, dkb, dga, dgb, dpu, dlr], [], [],
                        [(8 * d, MM_DTYPE)], dm, f"dproj_{l}", tm=tmw)
        dh = matmul(dP, s.w_in_p, 'nn', MM_DTYPE, f"in_proj_dx_{l}")
        gbig['w_in'][l] = _proj_to_w_in_t(matmul(dP, s.h, 'tn', MM_DTYPE, f"in_proj_dw_{l}"), d, wl, wlp)
        (dX,), dmod[0:2], (dg,) = rowwise_vjp(pre_fn, [s.x], s.mod[0:2], [row(g_pre_mix[l])], [dh], dm,
                                               f"pre_bwd_{l}", adds={0: dxa})
        grads['g_pre_mix'][l] = dg[0]
        dmodflat = jnp.concatenate([jnp.concatenate([m_.reshape(2, d) for m_ in dmod], axis=1),
                                    jnp.zeros((6, 6 * d), F32)], axis=0)
        grads['b_ada'][l] = dmodflat[0] + dmodflat[1]
        gbig['w_ada'][l] = matmul(avec, dmodflat, 'tn', MM_DTYPE, f"ada_dw_{l}")
        dav = matmul(dmodflat, full['w_ada'][l], 'nt', F32, f"ada_dx_{l}")
        g_cctx = g_cctx + dav[0] * _silu_grad(c_ctx)
        if l == 1:
            rs1 = start_scatter(list(range(nbig)), 1, dav, "grad_layer1_start")
            rs_token = rs1[-1]

    grad_x = dX[dm.CTX:][None]
    gfull = {n: jnp.stack(v) for n, v in grads.items()}
    gfull['c_ctx'] = g_cctx
    where = jnp.concatenate([chip1, core1])

    def halves_view(t, k):
        return t.reshape(2, t.shape[0] // 2, t.shape[1]) if k == 'col' else t.reshape(4, 2, t.shape[0] // 8, t.shape[1])
    enames = [big_names[k] for k in early]
    ekinds = [kinds[k] for k in early]
    v0 = [halves_view(gbig[n][0], k) for n, k in zip(enames, ekinds)]
    r1 = pair_swap_halves(v0, ekinds, "grad_pair_swap")
    hs = [pair_add(v.reshape((-1,) + v.shape[-2:]), r.reshape((-1,) + r.shape[-2:]), core1, f"grad_pair_add_{n}")
          for n, v, r in zip(enames, v0, r1)]
    hx = [h.reshape(h.shape[1:]) if k == 'col' else h for h, k in zip(hs, ekinds)]
    r2 = chip_exchange(hx, ekinds, "grad_chip_exchange")
    fs = [chip_add(h.reshape(-1, h.shape[-1]), r, big_axis[n] - 1, where, f"grad_chip_add_{n}")
          for n, h, r in zip(enames, hs, r2)]
    red0 = dict(zip(enames, [[t.reshape(-1, t.shape[-1])] for t in pair_join_layers(fs, "grad_pair_join")]))

    got0 = wait_copies(rs0[1], rs0[2], rs0[3], rs0[4], rs0[0], dX, "grad_layer0_mlp_wait")
    got1 = wait_copies(rs1[1], rs1[2], rs1[3], rs1[4], rs1[0], dX, "grad_layer1_wait")
    sa = [chip_add(g, r, big_axis[big_names[k]] - 1, where, f"grad_layer0_add_{big_names[k]}", slab=False)
          for k, g, r in zip(late, rs0[3], got0)]
    sa += [chip_add(g, r, big_axis[n] - 1, where, f"grad_layer1_add_{n}", slab=False)
           for n, g, r in zip(big_names, rs1[3], got1)]
    sb = pair_swap(sa, "grad_late_pair_swap")
    for j, k in enumerate(late):
        red0[big_names[k]] = [sa[j], sb[j]]
    red1 = {n: [sa[len(late) + k], sb[len(late) + k]] for k, n in enumerate(big_names)}

    sflat = _flatten_pad([gfull[n].astype(F32) for n in SMALL], F32)
    ssum = slot_sum(gather_all_devices(sflat, "small_grad_gather"), "small_grad_sum").reshape(-1)

    out_g, out_d, out_m, out_v = {}, {}, {}, {}
    for k, n in enumerate(big_names):
        out_g[n], out_d[n], out_m[n], out_v[n] = adamw_layers(a[n], a['m_' + n], a['v_' + n], red0[n], red1[n],
                                                              f"adamw_{n}")
    start = 0
    sg = {}
    for n in SMALL:
        cnt = gfull[n].size
        g = ssum[start:start + cnt].reshape(gfull[n].shape)
        start += cnt
        if n in SMALL_SHARDED:
            ax = SMALL_SHARDED[n]
            wdt = a[n].shape[ax]
            g = lax.dynamic_slice_in_dim(g, chip * wdt, wdt, axis=ax)
        sg[n] = g
    pk = lambda dct, pre: _flatten_pad([dct[pre + n] for n in SMALL], F32)
    gs = _flatten_pad([sg[n] for n in SMALL], F32)
    dl, mn, vn = adamw(pk(a, ''), gs, pk(a, 'm_'), pk(a, 'v_'), "adamw_small")
    dl, mn, vn = dl.reshape(-1), mn.reshape(-1), vn.reshape(-1)
    start = 0
    for n in SMALL:
        cnt, shp = a[n].size, a[n].shape
        out_g[n] = sg[n]
        out_d[n], out_m[n], out_v[n] = (t[start:start + cnt].reshape(shp) for t in (dl, mn, vn))
        start += cnt

    for dct in (out_g, out_d, out_m, out_v):
        dct['w_in'] = jnp.swapaxes(dct['w_in'], 1, 2)
    return (loss, grad_x, *[out_g[n] for n in WEIGHTS], *[out_d[n] for n in WEIGHTS],
            *[out_m[n] for n in WEIGHTS], *[out_v[n] for n in WEIGHTS])
```

```python
import functools
import itertools
import math
import types

import jax
import jax.numpy as jnp
from jax import lax
from jax.experimental import pallas as pl
from jax.experimental.pallas import tpu as pltpu

F32 = jnp.float32
MM_DTYPE = jnp.bfloat16
VMEM_LIMIT_V7X = 56 * 1024 * 1024
LANES = 128
EPS = 1e-6

N_HEADS = 4
GLA_CHUNK = 64
GLA_TAU = 16.0
GLA_LR = 16
GRID_W = 64
POOL_WINDOWS = (2, 4, 8, 16)

ADAM_LR = 0.001
ADAM_B1 = 0.9
ADAM_B2 = 0.999
ADAM_EPS = 1e-08
ADAM_WD = 0.01
ADAM_STEP = 10

NN = (((1,), (0,)), ((), ()))
NT = (((1,), (1,)), ((), ()))
TN = (((0,), (0,)), ((), ()))

WEIGHTS = ['c_ctx', 'w_ada', 'b_ada', 'g_pre_mix', 'g_post_mix', 'g_pre_mlp', 'g_post_mlp', 'w_in', 'w_decay',
           'b_decay', 'g_gla', 'w_gla_o', 'w_dw', 'b_dw', 'g_conv_ln', 'b_conv_ln', 'w_conv_o', 'w_pool_g',
           's_pool', 'w_pool_o', 'b_gate', 'w_out', 'w_mlp1', 'w_mlp2']
BIG = {'w_ada': 2, 'w_in': 2, 'w_gla_o': 1, 'w_conv_o': 2, 'w_pool_o': 2, 'w_out': 1, 'w_mlp1': 2, 'w_mlp2': 1}
SMALL_SHARDED = {'w_decay': 3, 'b_decay': 2, 'w_dw': 2, 'b_gate': 2}
SMALL = [n for n in WEIGHTS if n not in BIG]


def _tile(n, prefs):
    for t in prefs:
        if n % t == 0:
            return t
    return n


def _cparams(sem=None, **kw):
    return pltpu.CompilerParams(dimension_semantics=sem, vmem_limit_bytes=VMEM_LIMIT_V7X, **kw)


def _dot(a, b, dims=NN):
    return lax.dot_general(a.astype(MM_DTYPE), b.astype(MM_DTYPE), dims, preferred_element_type=F32)


def matmul(a, b, mode, out_dtype, name, tm=None, tn=None, tk=None, epi=None, extras=(), into=None):
    a, aw, ablk = a if isinstance(a, tuple) else (a, a.shape[1], 0)
    if mode == 'nn':
        M, K, N = a.shape[0], aw, b.shape[1]
    elif mode == 'nt':
        M, K, N = a.shape[0], aw, b.shape[0]
    else:
        K, M, N = a.shape[0], aw, b.shape[1]
    big = (1088, 1024, 640, 544, 512, 320, 256, 128, 64, 32, 16, 8)
    if mode == 'tn':
        tm = tm or _tile(M, (1024, 512, 256, 128))
        tn = tn or _tile(N, (1024, 512, 256, 128))
        tk = tk or _tile(K, big)
    else:
        tm = tm or _tile(M, big)
        tn = tn or _tile(N, (1024, 512, 256, 128))
        tk = tk or _tile(K, (1024, 512, 256, 128))
    if aw != a.shape[1]:
        assert (mode == 'tn' and tm == aw) or (mode != 'tn' and tk == aw)
    nk = K // tk
    ne = len(extras)
    dims = {'nn': NN, 'nt': NT, 'tn': TN}[mode]

    def body(a_ref, b_ref, *rest):
        e_refs, o_ref = rest[:ne], rest[ne + (into is not None)]

        def finish(acc):
            if epi is not None:
                acc = epi(acc, *[e[...] for e in e_refs])
            o_ref[...] = acc.astype(o_ref.dtype)

        p = _dot(a_ref[...], b_ref[...], dims)
        if nk == 1:
            finish(p)
            return
        acc = rest[-1]
        k = pl.program_id(2)

        @pl.when(k == 0)
        def _():
            acc[...] = p

        @pl.when(k > 0)
        def _():
            acc[...] += p

        @pl.when(k == nk - 1)
        def _():
            finish(acc[...])

    if mode == 'nn':
        a_spec = pl.BlockSpec((tm, tk), lambda i, j, k: (i, k + ablk))
        b_spec = pl.BlockSpec((tk, tn), lambda i, j, k: (k, j))
    elif mode == 'nt':
        a_spec = pl.BlockSpec((tm, tk), lambda i, j, k: (i, k + ablk))
        b_spec = pl.BlockSpec((tn, tk), lambda i, j, k: (j, k))
    else:
        a_spec = pl.BlockSpec((tk, tm), lambda i, j, k: (k, i + ablk))
        b_spec = pl.BlockSpec((tk, tn), lambda i, j, k: (k, j))
    tile = pl.BlockSpec((tm, tn), lambda i, j, k: (i, j))
    if into is None:
        out_spec, out_shape, more, extra, aliases = tile, jax.ShapeDtypeStruct((M, N), out_dtype), [], [], {}
    else:
        buf, oblk = into
        out_spec = pl.BlockSpec((tm, tn), lambda i, j, k: (i, oblk * (N // tn) + j))
        out_shape = jax.ShapeDtypeStruct(buf.shape, buf.dtype)
        more, extra, aliases = [pl.BlockSpec(memory_space=pl.ANY)], [buf], {2 + ne: 0}
    return pl.pallas_call(
        body, name=name, grid=(M // tm, N // tn, nk),
        in_specs=[a_spec, b_spec] + [tile] * ne + more, out_specs=out_spec,
        out_shape=out_shape, input_output_aliases=aliases,
        scratch_shapes=[] if nk == 1 else [pltpu.VMEM((tm, tn), F32)],
        compiler_params=_cparams(("parallel", "parallel", "arbitrary")),
    )(a, b, *extras, *extra)


def group_mm(a, w, mode, out_dtype, name, b=None):
    T = a.shape[0]
    G, gc, _ = w.shape
    col = pl.BlockSpec((T, gc), lambda g: (0, g))
    wsp = pl.BlockSpec((1, gc, gc), lambda g: (g, 0, 0))
    if mode == 'tn':
        def body(a_ref, b_ref, o_ref):
            o_ref[0] = _dot(a_ref[...], b_ref[...], TN).astype(o_ref.dtype)
        return pl.pallas_call(body, name=name, grid=(G,), in_specs=[col, col], out_specs=wsp,
                              out_shape=jax.ShapeDtypeStruct((G, gc, gc), out_dtype),
                              compiler_params=_cparams(("parallel",)))(a, b)
    dims = NN if mode == 'nn' else NT

    def body(a_ref, w_ref, o_ref):
        o_ref[...] = _dot(a_ref[...], w_ref[0], dims).astype(o_ref.dtype)
    return pl.pallas_call(body, name=name, grid=(G,), in_specs=[col, wsp], out_specs=col,
                          out_shape=jax.ShapeDtypeStruct((T, G * gc), out_dtype),
                          compiler_params=_cparams(("parallel",)))(a, w)


def _rowspec(r):
    return r if isinstance(r, tuple) else (r, r.shape[1], 0)


def _row_specs(rows, segs, consts, tm, nctx):
    specs = [pl.BlockSpec((tm, w), lambda i, b=b: (i, b)) for _, w, b in rows]
    specs += [pl.BlockSpec((1,) + s.shape[1:], lambda i, n=s.ndim: (jnp.where(i >= nctx, 1, 0),) + (0,) * (n - 1))
              for s in segs]
    specs += [pl.BlockSpec(c.shape, lambda i, n=c.ndim: (0,) * n) for c in consts]
    return specs


def rowwise(fn, rows, segs, consts, outs, dm, name, tm=None):
    tm = tm or dm.tm
    nctx = dm.CTX // tm
    rows = [_rowspec(r) for r in rows]
    nr, ns, nc = len(rows), len(segs), len(consts)

    def body(*refs):
        rin = [r[...] for r in refs[:nr]]
        sin = [s[0] for s in refs[nr:nr + ns]]
        cin = [c[...] for c in refs[nr + ns:nr + ns + nc]]
        res = fn(*rin, *sin, *cin)
        for o_ref, v in zip(refs[nr + ns + nc:], res):
            o_ref[...] = v.astype(o_ref.dtype)

    res = pl.pallas_call(
        body, name=name, grid=(dm.T // tm,),
        in_specs=_row_specs(rows, segs, consts, tm, nctx),
        out_specs=[pl.BlockSpec((tm, w), lambda i: (i, 0)) for w, _ in outs],
        out_shape=[jax.ShapeDtypeStruct((dm.T, w), dt) for w, dt in outs],
        compiler_params=_cparams(("parallel",)),
    )(*[r[0] for r in rows], *segs, *consts)
    return res


def rowwise_vjp(fn, rows, segs, consts, cots, dm, name, tm=None, want=None, adds=None, narrow=(), into=None):
    tm = tm or dm.tm
    nctx = dm.CTX // tm
    rows = [_rowspec(r) for r in rows]
    cots = [_rowspec(r) for r in cots]
    adds = adds or {}
    nr, ns, nc, nct = len(rows), len(segs), len(consts), len(cots)
    want = want or [True] * nr
    widx = [k for k in range(nr) if want[k]]
    akeys = sorted(adds)

    def body(*refs):
        i = pl.program_id(0)
        rin = [r[...] for r in refs[:nr]]
        sin = [s[0] for s in refs[nr:nr + ns]]
        cin = [c[...] for c in refs[nr + ns:nr + ns + nc]]
        p = nr + ns + nc
        cot_refs = refs[p:p + nct]
        add_refs = dict(zip(akeys, refs[p + nct:p + nct + len(akeys)]))
        p = p + nct + len(akeys) + (1 if (into is not None and into[1] is not None) else 0)
        rg_refs = refs[p:p + len(widx)]
        sg_refs = refs[p + len(widx):p + len(widx) + ns]
        cg_refs = refs[p + len(widx) + ns:]
        res, vjp = jax.vjp(fn, *rin, *sin, *cin)
        g = vjp(tuple(cr[...].astype(o.dtype) for cr, o in zip(cot_refs, res)))
        for o_ref, k in zip(rg_refs, widx):
            v = g[k].astype(F32)
            if k in add_refs:
                v = v + add_refs[k][...]
            o_ref[...] = v.astype(o_ref.dtype)
        first_seg = jnp.logical_or(i == 0, i == nctx)
        for o_ref, v in zip(sg_refs, g[nr:nr + ns]):
            @pl.when(first_seg)
            def _(o_ref=o_ref, v=v):
                o_ref[0] = v.astype(F32)

            @pl.when(jnp.logical_not(first_seg))
            def _(o_ref=o_ref, v=v):
                o_ref[0] += v.astype(F32)
        for o_ref, v in zip(cg_refs, g[nr + ns:]):
            @pl.when(i == 0)
            def _(o_ref=o_ref, v=v):
                o_ref[...] = v.astype(F32)

            @pl.when(i > 0)
            def _(o_ref=o_ref, v=v):
                o_ref[...] += v.astype(F32)

    in_specs = _row_specs(rows, segs, consts, tm, nctx)
    in_specs += [pl.BlockSpec((tm, w), lambda i, b=b: (i, b)) for _, w, b in cots]
    in_specs += [pl.BlockSpec((tm, adds[k].shape[1]), lambda i: (i, 0)) for k in akeys]
    out_specs = [pl.BlockSpec((tm, rows[k][1]), lambda i: (i, 0)) for k in widx]
    out_shape = [jax.ShapeDtypeStruct((dm.T, rows[k][1]), MM_DTYPE if k in narrow else rows[k][0].dtype)
                 for k in widx]
    extra, aliases = [], {}
    if into is not None:
        ik, ibuf, ishape = into
        out_specs[widx.index(ik)] = pl.BlockSpec((tm, rows[ik][1]), lambda i, b=rows[ik][2]: (i, b))
        out_shape[widx.index(ik)] = jax.ShapeDtypeStruct(ishape, MM_DTYPE)
        if ibuf is not None:
            aliases = {len(in_specs): widx.index(ik)}
            in_specs = in_specs + [pl.BlockSpec(memory_space=pl.ANY)]
            extra = [ibuf]
    out_specs += [pl.BlockSpec((1,) + s.shape[1:], lambda i, n=s.ndim: (jnp.where(i >= nctx, 1, 0),) + (0,) * (n - 1))
                  for s in segs]
    out_shape += [jax.ShapeDtypeStruct(s.shape, F32) for s in segs]
    out_specs += [pl.BlockSpec(c.shape, lambda i, n=c.ndim: (0,) * n) for c in consts]
    out_shape += [jax.ShapeDtypeStruct(c.shape, F32) for c in consts]
    res = pl.pallas_call(
        body, name=name, grid=(dm.T // tm,), in_specs=in_specs, out_specs=out_specs, out_shape=out_shape,
        input_output_aliases=aliases, compiler_params=_cparams(("arbitrary",)),
    )(*[r[0] for r in rows], *segs, *consts, *[r[0] for r in cots], *[adds[k] for k in akeys], *extra)
    rg = [None] * nr
    for k, v in zip(widx, res[:len(widx)]):
        rg[k] = v
    return rg, list(res[len(widx):len(widx) + ns]), list(res[len(widx) + ns:])


def _rms(x, g):
    return x * lax.rsqrt(jnp.mean(x * x, axis=-1, keepdims=True) + EPS) * g


def _sigmoid(x):
    return jax.nn.sigmoid(x)


def pre_fn(x, shift, scale, g):
    return ((_rms(x, g) * (1.0 + scale) + shift).astype(MM_DTYPE),)


def mid_fn(x, y, gate, shift, scale, g_post, g_pre):
    x1 = x + gate * _rms(y.astype(F32), g_post)
    return x1, (_rms(x1, g_pre) * (1.0 + scale) + shift).astype(MM_DTYPE)


def post_fn(x1, y2, gate, g):
    return (x1 + gate * _rms(y2.astype(F32), g),)


def relu2_epi(acc):
    r = jnp.maximum(acc, 0.0)
    return r * r


def relu2_bwd_epi(dact, act):
    return dact * (2.0 * jnp.sqrt(act.astype(F32)))


def decay_fn(z, bd):
    zz = z.astype(F32) + bd
    ls = jnp.minimum(zz, 0.0) - jnp.log(1.0 + jnp.exp(jnp.minimum(zz, -zz)))
    la = ls / GLA_TAU
    gk = la.shape[1] // 2
    return la[:, :gk], la[:, gk:]


def glu_fn(ab):
    h = ab.shape[1] // 2
    return (ab[:, :h].astype(F32) * _sigmoid(ab[:, h:].astype(F32)),)


def glaout_fn(o_f, o_b, og, g):
    o = o_f + o_b
    dv = o.shape[1] // N_HEADS
    hs = []
    for h in range(N_HEADS):
        oh = o[:, h * dv:(h + 1) * dv]
        hs.append(oh * lax.rsqrt(jnp.mean(oh * oh, axis=-1, keepdims=True) + EPS) * g[:, h * dv:(h + 1) * dv])
    og = og.astype(F32)
    return ((jnp.concatenate(hs, axis=1) * (og * _sigmoid(og))).astype(MM_DTYPE),)


def convpost_fn(y, b_dw, g, b):
    y = y + b_dw
    mu = jnp.mean(y, axis=-1, keepdims=True)
    xc = y - mu
    yn = xc * lax.rsqrt(jnp.mean(xc * xc, axis=-1, keepdims=True) + EPS) * g + b
    return ((yn * _sigmoid(yn)).astype(MM_DTYPE),)


def poolpost_fn(pc, s):
    return ((pc.astype(F32) * s).astype(MM_DTYPE),)


def merge_fn(ya, yb, yc, mg, bg0, bg1, bg2):
    d = ya.shape[1]
    mg = mg.astype(F32)
    mixed = (_sigmoid(mg[:, :d] + bg0) * ya.astype(F32) + _sigmoid(mg[:, d:2 * d] + bg1) * yb.astype(F32)
             + _sigmoid(mg[:, 2 * d:] + bg2) * yc.astype(F32))
    return (mixed.astype(MM_DTYPE),)


def _split_dot(lmat, x, dims):
    hi = x.astype(MM_DTYPE)
    lo = x - hi.astype(F32)
    return _dot(lmat, hi, dims) + _dot(lmat, lo, dims)


def _gla_block_order(dm, rev):
    nctx, nb = dm.CTX // dm.TB, dm.T // dm.TB

    def blk(i):
        if not rev:
            return i
        return jnp.where(i < nctx, nctx - 1 - i, nb - 1 - (i - nctx))
    return blk, nb


def _gla_tri(rev):
    c = GLA_CHUNK
    t = lax.broadcasted_iota(jnp.int32, (c, c), 0)
    s = lax.broadcasted_iota(jnp.int32, (c, c), 1)
    return (s >= t) if rev else (s <= t)


def _gla_chunk_terms(q, k, la, tri, scale):
    lmat = tri.astype(MM_DTYPE)
    b = _split_dot(lmat, la, NN)
    bend = jnp.sum(la, axis=0, keepdims=True)
    eb = jnp.exp(b)
    enb = jnp.exp(-b)
    ee = jnp.exp(bend - b)
    qi = q * scale * eb
    ki = k * enb
    kend = k * ee
    att = jnp.where(tri, _dot(qi, ki, NT), 0.0)
    return lmat, bend, eb, enb, ee, qi, ki, kend, att


def gla_fwd(P, la, rev, dm, name):
    c, tb, h_, dk, dv, d = GLA_CHUNK, dm.TB, N_HEADS, dm.DK, dm.DV, dm.D
    cpb = tb // c
    blk, nb = _gla_block_order(dm, rev)
    gk, gv = h_ * dk, h_ * dv
    qb, kb, vb, lb = (5 * d) // gk, (5 * d + d // 2) // gk, (4 * d) // gv, (1 if rev else 0)
    scale = dk ** -0.5
    order = list(range(cpb))[::-1] if rev else list(range(cpb))

    def body(q_ref, k_ref, v_ref, la_ref, o_ref, s_ref, st):
        @pl.when(pl.program_id(0) == 0)
        def _():
            st[...] = jnp.zeros_like(st)
        tri = _gla_tri(rev)
        for n, ci in enumerate(order):
            r = pl.ds(ci * c, c)
            for hh in range(h_):
                ck, cv = pl.ds(hh * dk, dk), pl.ds(hh * dv, dv)
                q = q_ref[r, ck].astype(F32)
                k = k_ref[r, ck].astype(F32)
                v = v_ref[r, cv]
                _, bend, _, _, _, qi, _, kend, att = _gla_chunk_terms(q, k, la_ref[r, ck], tri, scale)
                s_in = st[hh]
                o_ref[r, cv] = _dot(att, v) + _dot(qi, s_in, NT)
                s_ref[n, hh] = s_in
                st[hh] = jnp.exp(bend) * s_in + _dot(v, kend, TN)

    return pl.pallas_call(
        body, name=name, grid=(nb,),
        in_specs=[pl.BlockSpec((tb, gk), lambda i: (blk(i), qb)),
                  pl.BlockSpec((tb, gk), lambda i: (blk(i), kb)),
                  pl.BlockSpec((tb, gv), lambda i: (blk(i), vb)),
                  pl.BlockSpec((tb, gk), lambda i: (blk(i), lb))],
        out_specs=[pl.BlockSpec((tb, gv), lambda i: (blk(i), 0)),
                   pl.BlockSpec((cpb, h_, dv, dk), lambda i: (i, 0, 0, 0))],
        out_shape=[jax.ShapeDtypeStruct((dm.T, gv), F32),
                   jax.ShapeDtypeStruct((dm.T // c, h_, dv, dk), F32)],
        scratch_shapes=[pltpu.VMEM((h_, dv, dk), F32)],
        compiler_params=_cparams(("arbitrary",)),
    )(P, P, P, la)


def gla_bwd(P, la, do, states, rev, dm, name, prev=None, into=None):
    c, tb, h_, dk, dv, d = GLA_CHUNK, dm.TB, N_HEADS, dm.DK, dm.DV, dm.D
    cpb = tb // c
    blk, nb = _gla_block_order(dm, rev)
    gk, gv = h_ * dk, h_ * dv
    qb, kb, vb, lb = (5 * d) // gk, (5 * d + d // 2) // gk, (4 * d) // gv, (1 if rev else 0)
    scale = dk ** -0.5
    order = list(range(cpb))[::-1] if rev else list(range(cpb))

    fused = prev is not None

    def body(q_ref, k_ref, v_ref, la_ref, do_ref, s_ref, *rest):
        if fused:
            pq_ref, pk_ref, pv_ref, _, w_ref, dla_ref, dst = rest
        else:
            dq_ref, dk_ref, dv_ref, dla_ref, dst = rest

        def put(kind, r, cols, val):
            if not fused:
                {'q': dq_ref, 'k': dk_ref, 'v': dv_ref}[kind][r, cols] = val
                return
            p_ref, off = {'q': (pq_ref, gv), 'k': (pk_ref, gv + gk), 'v': (pv_ref, 0)}[kind]
            w_ref[r, pl.ds(off + cols.start, cols.size)] = (val + p_ref[r, cols]).astype(w_ref.dtype)

        @pl.when(pl.program_id(0) == 0)
        def _():
            dst[...] = jnp.zeros_like(dst)
        tri = _gla_tri(rev)
        for n in range(cpb - 1, -1, -1):
            r = pl.ds(order[n] * c, c)
            for hh in range(h_):
                ck, cv = pl.ds(hh * dk, dk), pl.ds(hh * dv, dv)
                q = q_ref[r, ck].astype(F32)
                k = k_ref[r, ck].astype(F32)
                v = v_ref[r, cv]
                lmat, bend, eb, enb, ee, qi, ki, kend, att = _gla_chunk_terms(q, k, la_ref[r, ck], tri, scale)
                s_in = s_ref[n, hh]
                ds_out = dst[hh]
                dob = do_ref[r, cv]
                datt = jnp.where(tri, _dot(dob, v, NT), 0.0)
                dqi = _dot(datt, ki) + _dot(dob, s_in)
                dki = _dot(datt, qi, TN)
                put('v', r, cv, _dot(att, dob, TN) + _dot(kend, ds_out, NT))
                dkend = _dot(v, ds_out)
                gam = jnp.exp(bend)
                dgam = jnp.sum(ds_out * s_in, axis=0, keepdims=True)
                dst[hh] = gam * ds_out + _dot(dob, qi, TN)
                put('q', r, ck, dqi * (scale * eb))
                put('k', r, ck, dki * enb + dkend * ee)
                db = dqi * qi - dki * ki - dkend * kend
                dbend = jnp.sum(dkend * kend, axis=0, keepdims=True) + dgam * gam
                dla_ref[r, ck] = _split_dot(lmat, db, TN) + dbend

    def bi(j):
        return blk(nb - 1 - j)

    in_specs = [
        pl.BlockSpec((tb, gk), lambda j: (bi(j), qb)),
        pl.BlockSpec((tb, gk), lambda j: (bi(j), kb)),
        pl.BlockSpec((tb, gv), lambda j: (bi(j), vb)),
        pl.BlockSpec((tb, gk), lambda j: (bi(j), lb)),
        pl.BlockSpec((tb, gv), lambda j: (bi(j), 0)),
        pl.BlockSpec((cpb, h_, dv, dk), lambda j: (nb - 1 - j, 0, 0, 0)),
    ]
    small = pl.BlockSpec((tb, gk), lambda j: (bi(j), 0))
    wide = pl.BlockSpec((tb, gv), lambda j: (bi(j), 0))
    if not fused:
        return pl.pallas_call(
            body, name=name, grid=(nb,), in_specs=in_specs, out_specs=[small, small, wide, small],
            out_shape=[jax.ShapeDtypeStruct((dm.T, gk), F32), jax.ShapeDtypeStruct((dm.T, gk), F32),
                       jax.ShapeDtypeStruct((dm.T, gv), F32), jax.ShapeDtypeStruct((dm.T, gk), F32)],
            scratch_shapes=[pltpu.VMEM((h_, dv, dk), F32)],
            compiler_params=_cparams(("arbitrary",)),
        )(P, P, P, la, do, states)
    return pl.pallas_call(
        body, name=name, grid=(nb,),
        in_specs=in_specs + [small, small, wide, pl.BlockSpec(memory_space=pl.ANY)],
        out_specs=[pl.BlockSpec((tb, 2 * gv), lambda j: (bi(j), vb // 2)), small],
        out_shape=[jax.ShapeDtypeStruct(into.shape, into.dtype), jax.ShapeDtypeStruct((dm.T, gk), F32)],
        input_output_aliases={9: 0},
        scratch_shapes=[pltpu.VMEM((h_, dv, dk), F32)],
        compiler_params=_cparams(("arbitrary",)),
    )(P, P, P, la, do, states, *prev, into)


def _pos(n, period):
    t = lax.broadcasted_iota(jnp.int32, (n, 1), 0)
    if period & (period - 1) == 0:
        return jnp.bitwise_and(t, period - 1)
    return lax.rem(t, period)


def _conv_segments(dm):
    return [(0, dm.CTX, dm.CTX), (dm.CTX, dm.SEQ, GRID_W)]


def conv_fwd(u, w, dm, name):
    kw, cw = w.shape
    segs = _conv_segments(dm)

    def body(u_ref, w_ref, y_ref):
        for r0, n, per in segs:
            useg = u_ref[r0:r0 + n, :]
            p = _pos(n, per)
            acc = jnp.zeros_like(useg)
            for kk in range(kw):
                d = kk - kw // 2
                sh = useg if d == 0 else pltpu.roll(useg, (-d) % n, 0)
                ok = jnp.logical_and(p + d >= 0, p + d < per)
                acc = acc + jnp.where(ok, sh, 0.0) * w_ref[kk:kk + 1, :]
            y_ref[r0:r0 + n, :] = acc

    return pl.pallas_call(
        body, name=name, grid=(cw // LANES,),
        in_specs=[pl.BlockSpec((dm.T, LANES), lambda j: (0, j)), pl.BlockSpec((kw, LANES), lambda j: (0, j))],
        out_specs=pl.BlockSpec((dm.T, LANES), lambda j: (0, j)),
        out_shape=jax.ShapeDtypeStruct((dm.T, cw), F32),
        compiler_params=_cparams(("parallel",)),
    )(u, w)


def conv_bwd(u, w, dy, dm, name):
    kw, cw = w.shape
    segs = _conv_segments(dm)

    def body(u_ref, w_ref, dy_ref, du_ref, dw_ref):
        dws = [jnp.zeros((1, LANES), F32)] * kw
        for r0, n, per in segs:
            useg = u_ref[r0:r0 + n, :]
            dyseg = dy_ref[r0:r0 + n, :]
            p = _pos(n, per)
            acc = jnp.zeros_like(useg)
            for kk in range(kw):
                d = kk - kw // 2
                shu = useg if d == 0 else pltpu.roll(useg, (-d) % n, 0)
                okf = jnp.logical_and(p + d >= 0, p + d < per)
                dws[kk] = dws[kk] + jnp.sum(jnp.where(okf, shu, 0.0) * dyseg, axis=0, keepdims=True)
                shd = dyseg if d == 0 else pltpu.roll(dyseg, d % n, 0)
                okb = jnp.logical_and(p - d >= 0, p - d < per)
                acc = acc + jnp.where(okb, shd, 0.0) * w_ref[kk:kk + 1, :]
            du_ref[r0:r0 + n, :] = acc
        for kk in range(kw):
            dw_ref[kk:kk + 1, :] = dws[kk]

    return pl.pallas_call(
        body, name=name, grid=(cw // LANES,),
        in_specs=[pl.BlockSpec((dm.T, LANES), lambda j: (0, j)), pl.BlockSpec((kw, LANES), lambda j: (0, j)),
                  pl.BlockSpec((dm.T, LANES), lambda j: (0, j))],
        out_specs=[pl.BlockSpec((dm.T, LANES), lambda j: (0, j)), pl.BlockSpec((kw, LANES), lambda j: (0, j))],
        out_shape=[jax.ShapeDtypeStruct((dm.T, cw), F32), jax.ShapeDtypeStruct((kw, cw), F32)],
        compiler_params=_cparams(("parallel",)),
    )(u, w, dy)


def pool_mix(u, transpose, dm, name, into=None):
    u, uw, ublk = _rowspec(u)
    gc = dm.GC
    ng = len(POOL_WINDOWS)
    rows = dm.SEQ // GRID_W
    segs = [(0, dm.CTX, 1, dm.CTX), (dm.CTX, dm.SEQ, GRID_W, rows)]

    def one_group(u_ref, o_ref, win):
        left = win // 2
        right = win - 1 - left
        for r0, n, stride, length in segs:
            useg = u_ref[r0:r0 + n, :].astype(F32)
            t = lax.broadcasted_iota(jnp.int32, (n, 1), 0)
            p = t if stride == 1 else jnp.right_shift(t, stride.bit_length() - 1)
            cnt = (jnp.minimum(p + right + 1, length) - jnp.maximum(p - left, 0)).astype(F32)
            src = useg / cnt if transpose else useg
            acc = jnp.zeros_like(useg)
            for d in range(-left, right + 1):
                dd = -d if transpose else d
                sh = src if d == 0 else pltpu.roll(src, (-dd * stride) % n, 0)
                ok = jnp.logical_and(p + dd >= 0, p + dd < length)
                acc = acc + jnp.where(ok, sh, 0.0)
            o_ref[r0:r0 + n, :] = ((acc - useg) if transpose else (acc / cnt - useg)).astype(o_ref.dtype)

    def body(u_ref, *rest):
        o_ref = rest[-1]
        g = pl.program_id(0)
        for gi, win in enumerate(POOL_WINDOWS):
            @pl.when(g == gi)
            def _(win=win):
                one_group(u_ref, o_ref, win)

    base = ublk * (uw // gc)
    if into is None:
        obase, out_shape, more, extra, aliases = 0, jax.ShapeDtypeStruct((dm.T, ng * gc), F32), [], [], {}
    else:
        buf, oblk = into
        obase, out_shape = oblk * ng, jax.ShapeDtypeStruct(buf.shape, buf.dtype)
        more, extra, aliases = [pl.BlockSpec(memory_space=pl.ANY)], [buf], {1: 0}
    return pl.pallas_call(
        body, name=name, grid=(ng,),
        in_specs=[pl.BlockSpec((dm.T, gc), lambda g: (0, base + g))] + more,
        out_specs=pl.BlockSpec((dm.T, gc), lambda g: (0, obase + g)),
        out_shape=out_shape, input_output_aliases=aliases,
        compiler_params=_cparams(("parallel",)),
    )(u, *extra)


def loss_head(x2, target, dm, name):
    tm, d = dm.tm, dm.D
    nctx = dm.CTX // tm

    def body(x_ref, t_ref, dx_ref, l_ref):
        i = pl.program_id(0)

        @pl.when(i == 0)
        def _():
            l_ref[...] = jnp.zeros_like(l_ref)

        @pl.when(i < nctx)
        def _():
            dx_ref[...] = jnp.zeros_like(dx_ref)

        @pl.when(i >= nctx)
        def _():
            e = x_ref[...] - t_ref[...]
            dx_ref[...] = e / d
            l_ref[...] += jnp.full(l_ref.shape, 0.5 * jnp.sum(jnp.mean(e * e, axis=-1)), F32)

    return pl.pallas_call(
        body, name=name, grid=(dm.T // tm,),
        in_specs=[pl.BlockSpec((tm, d), lambda i: (i, 0)),
                  pl.BlockSpec((tm, d), lambda i: (jnp.maximum(i - nctx, 0), 0))],
        out_specs=[pl.BlockSpec((tm, d), lambda i: (i, 0)), pl.BlockSpec((8, LANES), lambda i: (0, 0))],
        out_shape=[jax.ShapeDtypeStruct((dm.T, d), F32), jax.ShapeDtypeStruct((8, LANES), F32)],
        compiler_params=_cparams(("arbitrary",)),
    )(x2, target)


def adamw(w, g, m, v, name):
    r, c = w.shape
    tr = _tile(r, tuple(t for t in (512, 256, 128, 64, 32, 16, 8) if t * c * 4 <= (1 << 20)) or (8,))

    def body(w_ref, g_ref, m_ref, v_ref, d_ref, mo_ref, vo_ref):
        gg = g_ref[...]
        mm = ADAM_B1 * m_ref[...] + (1.0 - ADAM_B1) * gg
        vv = ADAM_B2 * v_ref[...] + (1.0 - ADAM_B2) * (gg * gg)
        m_hat = mm / (1.0 - ADAM_B1 ** ADAM_STEP)
        v_hat = vv / (1.0 - ADAM_B2 ** ADAM_STEP)
        d_ref[...] = -ADAM_LR * (m_hat / (jnp.sqrt(v_hat) + ADAM_EPS) + ADAM_WD * w_ref[...])
        mo_ref[...] = mm
        vo_ref[...] = vv

    spec = pl.BlockSpec((tr, c), lambda i: (i, 0))
    return pl.pallas_call(
        body, name=name, grid=(r // tr,), in_specs=[spec] * 4, out_specs=[spec] * 3,
        out_shape=[jax.ShapeDtypeStruct((r, c), F32)] * 3,
        compiler_params=_cparams(("parallel",)),
    )(w, g, m, v)


def slot_sum(buf, name):
    s, r, c = buf.shape
    tr = _tile(r, (256, 128, 64, 32, 16, 8))

    def body(b_ref, o_ref):
        acc = b_ref[0].astype(F32)
        for k in range(1, s):
            acc = acc + b_ref[k].astype(F32)
        o_ref[...] = acc

    return pl.pallas_call(
        body, name=name, grid=(r // tr,),
        in_specs=[pl.BlockSpec((s, tr, c), lambda i: (0, i, 0))],
        out_specs=pl.BlockSpec((tr, c), lambda i: (i, 0)),
        out_shape=jax.ShapeDtypeStruct((r, c), F32),
        compiler_params=_cparams(("parallel",)),
    )(buf)


def pair_add(g, r1, cidx, name):
    ng, r_, n_ = r1.shape
    tr = _tile(r_, tuple(t for t in (1024, 512, 256, 128, 64, 32, 16) if t * n_ * 4 <= (2 << 20)))

    def body(s_ref, g_ref, r_ref, o_ref):
        o_ref[...] = (g_ref[...].astype(F32) + r_ref[...].astype(F32)).astype(o_ref.dtype)

    return pl.pallas_call(
        body, name=name,
        grid_spec=pltpu.PrefetchScalarGridSpec(
            num_scalar_prefetch=1, grid=(ng, r_ // tr),
            in_specs=[pl.BlockSpec((None, tr, n_), lambda k, i, s: (2 * k + s[0], i, 0)),
                      pl.BlockSpec((None, tr, n_), lambda k, i, s: (k, i, 0))],
            out_specs=pl.BlockSpec((None, tr, n_), lambda k, i, s: (k, i, 0))),
        out_shape=jax.ShapeDtypeStruct((ng, r_, n_), g.dtype),
        compiler_params=_cparams(("parallel", "parallel")),
    )(cidx, g, r1)


def chip_add(h, r2, axis, where, name, slab=True):
    _, kl, nl = r2.shape
    tr = _tile(kl, tuple(t for t in (1024, 512, 256, 128, 64, 32, 16) if t * nl * 4 <= (1 << 20)))
    nrb = kl // tr

    def body(s_ref, h_ref, r_ref, o_ref):
        acc = h_ref[...].astype(F32)
        for k in range(r2.shape[0]):
            acc = acc + r_ref[k].astype(F32)
        o_ref[...] = acc

    h_map = (lambda i, s: (s[0] * nrb + i, 0)) if axis == 0 else (lambda i, s: (i, s[0]))
    if slab:
        out_spec = pl.BlockSpec((None, tr, nl), lambda i, s: (s[1], i, 0))
        out_shape = jax.ShapeDtypeStruct((2, kl, nl), F32)
    else:
        out_spec = pl.BlockSpec((tr, nl), lambda i, s: (i, 0))
        out_shape = jax.ShapeDtypeStruct((kl, nl), F32)
    return pl.pallas_call(
        body, name=name,
        grid_spec=pltpu.PrefetchScalarGridSpec(
            num_scalar_prefetch=1, grid=(nrb,),
            in_specs=[pl.BlockSpec((tr, nl), h_map),
                      pl.BlockSpec((r2.shape[0], tr, nl), lambda i, s: (0, i, 0))],
            out_specs=out_spec),
        out_shape=out_shape,
        compiler_params=_cparams(("parallel",)),
    )(where, h, r2)


def adamw_layers(w, m, v, terms0, terms1, name):
    _, a_, b_ = w.shape
    tr = _tile(a_, tuple(t for t in (512, 256, 128, 64, 32) if t * b_ * 4 <= (1 << 20)))
    by_cols = tr == a_ and a_ * b_ * 4 > (1 << 20)
    blk = (a_, LANES) if by_cols else (tr, b_)
    steps = b_ // LANES if by_cols else a_ // tr
    at = (lambda i: (0, i)) if by_cols else (lambda i: (i, 0))
    n0 = len(terms0)

    def update(g, w_ref, m_ref, v_ref, g_ref, d_ref, mo_ref, vo_ref):
        mm = ADAM_B1 * m_ref[...] + (1.0 - ADAM_B1) * g
        vv = ADAM_B2 * v_ref[...] + (1.0 - ADAM_B2) * (g * g)
        m_hat = mm / (1.0 - ADAM_B1 ** ADAM_STEP)
        v_hat = vv / (1.0 - ADAM_B2 ** ADAM_STEP)
        g_ref[...] = g
        d_ref[...] = -ADAM_LR * (m_hat / (jnp.sqrt(v_hat) + ADAM_EPS) + ADAM_WD * w_ref[...])
        mo_ref[...] = mm
        vo_ref[...] = vv

    def total(refs):
        g = refs[0][...]
        for r in refs[1:]:
            g = g + r[...]
        return g

    def body(w_ref, m_ref, v_ref, *rest):
        t_refs, outs = rest[:-4], rest[-4:]
        layer = pl.program_id(0)

        @pl.when(layer == 0)
        def _():
            update(total(t_refs[:n0]), w_ref, m_ref, v_ref, *outs)

        @pl.when(layer == 1)
        def _():
            update(total(t_refs[n0:]), w_ref, m_ref, v_ref, *outs)

    stacked = pl.BlockSpec((None,) + blk, lambda l, i: (l,) + at(i))
    return pl.pallas_call(
        body, name=name, grid=(2, steps),
        in_specs=[stacked] * 3 + [pl.BlockSpec(blk, lambda l, i: at(i * (1 - l)))] * n0
        + [pl.BlockSpec(blk, lambda l, i: at(i * l))] * len(terms1),
        out_specs=[stacked] * 4, out_shape=[jax.ShapeDtypeStruct(w.shape, F32)] * 4,
        compiler_params=_cparams(("arbitrary", "arbitrary")),
    )(w, m, v, *terms0, *terms1)


MESH = pl.DeviceIdType.MESH
ANY = pl.BlockSpec(memory_space=pl.ANY)
HBM = pl.BlockSpec(memory_space=pltpu.HBM)
SEM = pl.BlockSpec(memory_space=pltpu.SEMAPHORE)
EFFECT = pltpu.SideEffectType.DATAFLOW_SIDE_EFFECTING


def _place():
    return lax.axis_index("x"), lax.axis_index("y"), lax.axis_index("c")


def _peers(x, y):
    return [(1 - x, y), (x, 1 - y), (1 - x, 1 - y)]


def _rcopy(src, dst, ssem, rsem, dev):
    return pltpu.make_async_remote_copy(src_ref=src, dst_ref=dst, send_sem=ssem, recv_sem=rsem,
                                        device_id=dev, device_id_type=MESH)


def _gathered_shape(src, kind):
    h, a_, b_ = src.shape
    return (h, a_, 4 * b_) if kind == 'col' else (4, h, a_, b_)


def _win(ref, kind, ch, width):
    return ref.at[:, :, pl.ds(ch * width, width)] if kind == 'col' else ref.at[ch]


def _rect(ref, kind, half, ch, width):
    return ref.at[half, :, pl.ds(ch * width, width)] if kind == 'col' else ref.at[ch, half]


def gather_halves(srcs, kinds, name):
    nw = len(srcs)
    widths = [s.shape[2] for s in srcs]

    def body(*refs):
        src, out = refs[:nw], refs[nw:2 * nw]
        ssem, rsem, osend, orecv = refs[2 * nw:]
        x, y, c = _place()
        chip = 2 * x + y
        sib = (x, y, 1 - c)
        peers = _peers(x, y)
        pidx = [2 * px + py for px, py in peers]

        def rect(n, half, ch):
            return _rect(out[n], kinds[n], half, ch, widths[n])

        mine = [_rcopy(src[n], _win(out[n], kinds[n], chip, widths[n]), osend.at[n], orecv.at[n], sib)
                for n in range(nw)]
        first = [[_rcopy(src[n].at[c], rect(n, c, chip), ssem.at[6 * n + k], rsem.at[6 * n + k], (px, py, c))
                  for k, (px, py) in enumerate(peers)] for n in range(nw)]
        for n in range(nw):
            for cp in first[n]:
                cp.start()
        for cp in mine:
            cp.start()
        passed = [[_rcopy(rect(n, c, pidx[k]), rect(n, c, pidx[k]), ssem.at[6 * n + 3 + k], rsem.at[6 * n + 3 + k], sib)
                   for k in range(3)] for n in range(nw)]
        for n in range(nw):
            for k, (px, py) in enumerate(peers):
                _rcopy(rect(n, c, pidx[k]), rect(n, c, pidx[k]), ssem.at[6 * n + k], rsem.at[6 * n + k],
                       (px, py, c)).wait_recv()
                passed[n][k].start()
        for n in range(nw):
            for k in range(3):
                _rcopy(rect(n, 1 - c, pidx[k]), rect(n, 1 - c, pidx[k]), ssem.at[6 * n + 3 + k],
                       rsem.at[6 * n + 3 + k], sib).wait_recv()
        for n in range(nw):
            for cp in first[n] + passed[n]:
                cp.wait_send()
        for cp in mine:
            cp.wait()

    return pl.pallas_call(
        body, name=name, in_specs=[ANY] * nw, out_specs=[ANY] * nw,
        out_shape=[jax.ShapeDtypeStruct(_gathered_shape(s, k), s.dtype) for s, k in zip(srcs, kinds)],
        scratch_shapes=[pltpu.SemaphoreType.DMA((6 * nw,)), pltpu.SemaphoreType.DMA((6 * nw,)),
                        pltpu.SemaphoreType.DMA((nw,)), pltpu.SemaphoreType.DMA((nw,))],
    )(*srcs)


def _gather_plan(kinds, widths):
    def plan(src, land, x, y, c):
        chip = 2 * x + y
        out = []
        for n in range(len(src)):
            mine = _win(land[n], kinds[n], chip, widths[n])
            for px, py in _peers(x, y):
                out.append((src[n], mine, (px, py, c), _win(land[n], kinds[n], 2 * px + py, widths[n])))
            out.append((src[n], mine, (x, y, 1 - c), mine))
        return out
    return plan


def _scatter_plan(axes, widths):
    def plan(src, land, x, y, c):
        out = []
        for n in range(len(src)):
            for k, (px, py) in enumerate(_peers(x, y)):
                ch = 2 * px + py
                view = (src[n].at[:, pl.ds(ch * widths[n], widths[n])] if axes[n] == 1
                        else src[n].at[pl.ds(ch * widths[n], widths[n]), :])
                out.append((view, land[n].at[k], (px, py, c), land[n].at[k]))
        return out
    return plan


def start_copies(srcs, lands, plan, ncopies, after, name):
    ns, nl = len(srcs), len(lands)

    def body(*refs):
        src, land = refs[:ns], refs[ns:ns + nl]
        ssem, rsem = refs[ns + nl + 1], refs[ns + nl + 2]
        token = refs[-1]
        x, y, c = _place()
        for k, (sv, dv, dev, _) in enumerate(plan(src, land, x, y, c)):
            _rcopy(sv, dv, ssem.at[k], rsem.at[k], dev).start()
        token[...] = jnp.zeros_like(token)

    hbm = lambda t: pltpu.HBM(t.shape, t.dtype)
    res = pl.pallas_call(
        body, name=name,
        out_shape=(pltpu.SemaphoreType.DMA((ncopies,)), pltpu.SemaphoreType.DMA((ncopies,)),
                   *[hbm(t) for t in srcs], *[hbm(t) for t in lands], jax.ShapeDtypeStruct((8, LANES), F32)),
        in_specs=[HBM] * (ns + nl) + [ANY],
        out_specs=(SEM, SEM, *[HBM] * (ns + nl), pl.BlockSpec(memory_space=pltpu.VMEM)),
        input_output_aliases={k: 2 + k for k in range(ns + nl)},
        compiler_params=pltpu.CompilerParams(has_side_effects=EFFECT),
    )(*[pltpu.with_memory_space_constraint(t, pltpu.HBM) for t in list(srcs) + list(lands)], after)
    return res[0], res[1], list(res[2:2 + ns]), list(res[2 + ns:2 + ns + nl]), res[-1]


def wait_copies(ssem, rsem, srcs, lands, plan, after, name):
    ns, nl = len(srcs), len(lands)

    def body(*refs):
        src, land = refs[:ns], refs[ns:ns + nl]
        ss, rs = refs[ns + nl], refs[ns + nl + 1]
        x, y, c = _place()
        for k, (sv, dv, dev, mine) in enumerate(plan(src, land, x, y, c)):
            cp = _rcopy(sv, mine, ss.at[k], rs.at[k], dev)
            cp.wait_send()
            cp.wait_recv()

    hbm = lambda t: pltpu.HBM(t.shape, t.dtype)
    res = pl.pallas_call(
        body, name=name,
        out_shape=(*[hbm(t) for t in srcs], *[hbm(t) for t in lands]),
        in_specs=[HBM] * (ns + nl) + [SEM, SEM, ANY], out_specs=tuple([HBM] * (ns + nl)),
        input_output_aliases={k: k for k in range(ns + nl)},
        compiler_params=pltpu.CompilerParams(has_side_effects=EFFECT),
    )(*srcs, *lands, ssem, rsem, after)
    return list(res[ns:])


def pair_swap_halves(gs, kinds, name):
    nw = len(gs)

    def other(ref, kind, half):
        return ref.at[half] if kind == 'col' else ref.at[:, half]

    def body(*refs):
        g, o = refs[:nw], refs[nw:2 * nw]
        ssem, rsem = refs[2 * nw:]
        x, y, c = _place()
        cps = [_rcopy(other(g[n], kinds[n], 1 - c), o[n], ssem.at[n], rsem.at[n], (x, y, 1 - c)) for n in range(nw)]
        for cp in cps:
            cp.start()
        for cp in cps:
            cp.wait()

    return pl.pallas_call(
        body, name=name, in_specs=[ANY] * nw, out_specs=[ANY] * nw,
        out_shape=[jax.ShapeDtypeStruct(g.shape[1:] if k == 'col' else (g.shape[0],) + g.shape[2:], g.dtype)
                   for g, k in zip(gs, kinds)],
        scratch_shapes=[pltpu.SemaphoreType.DMA((nw,)), pltpu.SemaphoreType.DMA((nw,))],
    )(*gs)


def pair_swap(fs, name):
    nw = len(fs)

    def body(*refs):
        f, o = refs[:nw], refs[nw:2 * nw]
        ssem, rsem = refs[2 * nw:]
        x, y, c = _place()
        cps = [_rcopy(f[n], o[n], ssem.at[n], rsem.at[n], (x, y, 1 - c)) for n in range(nw)]
        for cp in cps:
            cp.start()
        for cp in cps:
            cp.wait()

    return pl.pallas_call(
        body, name=name, in_specs=[ANY] * nw, out_specs=[ANY] * nw,
        out_shape=[jax.ShapeDtypeStruct(f.shape, f.dtype) for f in fs],
        scratch_shapes=[pltpu.SemaphoreType.DMA((nw,)), pltpu.SemaphoreType.DMA((nw,))],
    )(*fs)


def chip_exchange(hs, kinds, name):
    nw = len(hs)
    shp = [(h.shape[0], h.shape[1] // 4) if k == 'col' else h.shape[1:] for h, k in zip(hs, kinds)]

    def body(*refs):
        h, o = refs[:nw], refs[nw:2 * nw]
        ssem, rsem = refs[2 * nw:]
        x, y, c = _place()

        def win(n, ch):
            return h[n].at[:, pl.ds(ch * shp[n][1], shp[n][1])] if kinds[n] == 'col' else h[n].at[ch]

        cps = [_rcopy(win(n, 2 * px + py), o[n].at[k], ssem.at[3 * n + k], rsem.at[3 * n + k], (px, py, c))
               for n in range(nw) for k, (px, py) in enumerate(_peers(x, y))]
        for cp in cps:
            cp.start()
        for cp in cps:
            cp.wait()

    return pl.pallas_call(
        body, name=name, in_specs=[ANY] * nw, out_specs=[ANY] * nw,
        out_shape=[jax.ShapeDtypeStruct((3,) + sh, h.dtype) for sh, h in zip(shp, hs)],
        scratch_shapes=[pltpu.SemaphoreType.DMA((3 * nw,)), pltpu.SemaphoreType.DMA((3 * nw,))],
    )(*hs)


def pair_join_layers(fs, name):
    nw = len(fs)

    def body(*refs):
        o = refs[nw:2 * nw]
        ssem, rsem = refs[2 * nw:]
        x, y, c = _place()
        sib = (x, y, 1 - c)
        cps = [_rcopy(o[n].at[c], o[n].at[c], ssem.at[n], rsem.at[n], sib) for n in range(nw)]
        for cp in cps:
            cp.start()
        for n in range(nw):
            cps[n].wait_send()
            _rcopy(o[n].at[1 - c], o[n].at[1 - c], ssem.at[n], rsem.at[n], sib).wait_recv()

    return pl.pallas_call(
        body, name=name, in_specs=[ANY] * nw, out_specs=[ANY] * nw,
        out_shape=[jax.ShapeDtypeStruct(f.shape, f.dtype) for f in fs],
        input_output_aliases={n: n for n in range(nw)},
        scratch_shapes=[pltpu.SemaphoreType.DMA((nw,)), pltpu.SemaphoreType.DMA((nw,))],
    )(*fs)


def gather_all_devices(buf, name):
    r, c_ = buf.shape
    offs = [o for o in itertools.product((0, 1), repeat=3) if o != (0, 0, 0)]

    def body(b_ref, o_ref, ssem, rsem, lsem):
        x, y, c = _place()
        me = 4 * x + 2 * y + c
        mine = pltpu.make_async_copy(b_ref, o_ref.at[me], lsem)
        mine.start()
        peers = [((x + dx) % 2, (y + dy) % 2, (c + dc) % 2) for dx, dy, dc in offs]
        cps = [_rcopy(b_ref, o_ref.at[me], ssem.at[k], rsem.at[k], p) for k, p in enumerate(peers)]
        for cp in cps:
            cp.start()
        for k, (px, py, pc) in enumerate(peers):
            _rcopy(b_ref, o_ref.at[4 * px + 2 * py + pc], ssem.at[k], rsem.at[k], (px, py, pc)).wait_recv()
        for cp in cps:
            cp.wait_send()
        mine.wait()

    return pl.pallas_call(
        body, name=name, in_specs=[ANY], out_specs=ANY,
        out_shape=jax.ShapeDtypeStruct((8, r, c_), buf.dtype),
        scratch_shapes=[pltpu.SemaphoreType.DMA((7,)), pltpu.SemaphoreType.DMA((7,)), pltpu.SemaphoreType.DMA],
    )(buf)


def _flatten_pad(parts, dtype):
    flat = jnp.concatenate([p.reshape(-1).astype(dtype) for p in parts])
    q = 512 * LANES
    n = -(-flat.shape[0] // q) * q
    return jnp.pad(flat, (0, n - flat.shape[0])).reshape(n // LANES, LANES)


def _lane_pad(n):
    return -(-n // LANES) * LANES


def _in_proj_layout(d):
    gk, gv, cw, pw = d // 2, d, d // 2, d // 2
    own = [('q', gk), ('k', gk), ('v', gv), ('og', gv), ('lrf', GLA_LR), ('lrb', GLA_LR), ('ga', cw), ('gb', cw),
           ('pu', pw), ('mg', 3 * d)]
    padded = [('mg', 3 * d), ('og', gv), ('v', gv), ('q', gk), ('k', gk), ('ga', cw), ('gb', cw), ('pu', pw),
              ('lrf', GLA_LR), ('lrb', GLA_LR), ('pad', d // 2 - 2 * GLA_LR)]
    return own, padded


def _row_pieces(src, lo, hi, wl, wlp):
    out = []
    for k in range(4):
        s0, s1 = max(lo, k * wl), min(hi, (k + 1) * wl)
        if s0 < s1:
            out.append(src[k * wlp + s0 - k * wl:k * wlp + s1 - k * wl])
    return out


def _w_in_t_to_proj(g, d, wl, wlp):
    own, padded = _in_proj_layout(d)
    at, start = {}, 0
    for n, wd in own:
        at[n] = (start, start + wd)
        start += wd
    parts = []
    for n, wd in padded:
        parts += [jnp.zeros((wd, g.shape[1]), g.dtype)] if n == 'pad' else _row_pieces(g, *at[n], wl, wlp)
    return jnp.concatenate(parts, axis=0)


def _proj_to_w_in_t(gp, d, wl, wlp):
    own, padded = _in_proj_layout(d)
    pat, start = {}, 0
    for n, wd in padded:
        pat[n] = start
        start += wd
    parts = []
    for k in range(4):
        start = 0
        for n, wd in own:
            s0, s1 = max(start, k * wl), min(start + wd, (k + 1) * wl)
            if s0 < s1:
                parts.append(gp[pat[n] + s0 - start:pat[n] + s1 - start])
            start += wd
        parts.append(jnp.zeros((wlp - wl, gp.shape[1]), gp.dtype))
    return jnp.concatenate(parts, axis=0)


def _silu_grad(z):
    s = jax.nn.sigmoid(z)
    return s + z * s * (1.0 - s)


def kernel(x, c, ctx, c_ctx, w_ada, b_ada, g_pre_mix, g_post_mix, g_pre_mlp, g_post_mlp, w_in, w_decay, b_decay, g_gla, w_gla_o, w_dw, b_dw, g_conv_ln, b_conv_ln, w_conv_o, w_pool_g, s_pool, w_pool_o, b_gate, w_out, w_mlp1, w_mlp2, loss_target, m_c_ctx, m_w_ada, m_b_ada, m_g_pre_mix, m_g_post_mix, m_g_pre_mlp, m_g_post_mlp, m_w_in, m_w_decay, m_b_decay, m_g_gla, m_w_gla_o, m_w_dw, m_b_dw, m_g_conv_ln, m_b_conv_ln, m_w_conv_o, m_w_pool_g, m_s_pool, m_w_pool_o, m_b_gate, m_w_out, m_w_mlp1, m_w_mlp2, v_c_ctx, v_w_ada, v_b_ada, v_g_pre_mix, v_g_post_mix, v_g_pre_mlp, v_g_post_mlp, v_w_in, v_w_decay, v_b_decay, v_g_gla, v_w_gla_o, v_w_dw, v_b_dw, v_g_conv_ln, v_b_conv_ln, v_w_conv_o, v_w_pool_g, v_s_pool, v_w_pool_o, v_b_gate, v_w_out, v_w_mlp1, v_w_mlp2):
    a = dict(locals())
    for n in ('w_in', 'm_w_in', 'v_w_in'):
        a[n] = jnp.swapaxes(a[n], 1, 2)
    big_axis = dict(BIG, w_in=1)
    depth = w_in.shape[0]
    d = x.shape[-1]
    seq, nctx_rows = x.shape[1], ctx.shape[1]
    dm = types.SimpleNamespace(
        D=d, SEQ=seq, CTX=nctx_rows, T=seq + nctx_rows, DK=d // 8, DV=d // 4, GK=d // 2, GC=d // 8,
        tm=_tile(nctx_rows, (256, 128, 64)), TB=_tile(nctx_rows, (256, 128, 64)))
    assert dm.SEQ % dm.tm == 0 and dm.SEQ % GRID_W == 0 and dm.CTX % GLA_CHUNK == 0
    tmw = min(dm.tm, 128)
    chip = 2 * lax.axis_index("x") + lax.axis_index("y")
    core = lax.axis_index("c")
    chip1 = chip.astype(jnp.int32).reshape(1)
    core1 = core.astype(jnp.int32).reshape(1)

    big_names, small_names = list(BIG), list(SMALL_SHARDED)
    nbig = len(big_names)
    kinds = ['col' if big_axis[n] == 2 else 'row' for n in big_names]
    wl = w_in.shape[2]
    wlp = _lane_pad(wl)

    def rows8(t):
        t = t.reshape(t.shape[0], -1, t.shape[-1])
        return jnp.pad(t, ((0, 0), (0, -t.shape[1] % 8), (0, 0)))

    def halves(t):
        return t.reshape(2, t.shape[0] // 2, t.shape[1])

    def layer_src(l):
        return [halves((jnp.pad(a[n][l], ((0, wlp - wl), (0, 0))) if n == 'w_in' else a[n][l]).astype(MM_DTYPE))
                for n in big_names]

    def whole(t):
        return t.reshape(-1, t.shape[-1])

    late = [big_names.index(n) for n in ('w_gla_o', 'w_conv_o', 'w_pool_o', 'w_out', 'w_mlp1', 'w_mlp2')]
    early = [k for k in range(nbig) if k not in late]
    src0, src1 = layer_src(0), layer_src(1)
    g0 = gather_halves([src0[k] for k in early] + [rows8(a[n]) for n in small_names],
                       [kinds[k] for k in early] + ['col'] * len(small_names), "gather_layer0")

    def start_gather(srcs, knds, after, name):
        plan = _gather_plan(knds, [t.shape[2] for t in srcs])
        lands = [lax.empty(_gathered_shape(t, k), t.dtype) for t, k in zip(srcs, knds)]
        return (plan,) + start_copies(srcs, lands, plan, 4 * len(srcs), after, name)

    ag0 = start_gather([src0[k] for k in late], [kinds[k] for k in late], g0[0], "gather_layer0_late_start")
    ag1 = start_gather(src1, kinds, ag0[-1], "gather_layer1_start")
    ag_token = ag1[-1]
    full = {n: [None, None] for n in big_names}
    for k, t in zip(early, g0):
        full[big_names[k]][0] = whole(t)
    for n, g in zip(small_names, g0[len(early):]):
        shp = a[n].shape
        full[n] = g[:, :math.prod(shp[1:-1])].reshape(shp[:-1] + (4 * shp[-1],))
    for n in SMALL:
        if n not in SMALL_SHARDED:
            full[n] = a[n]

    cvec = jnp.concatenate([c_ctx.reshape(1, d), c.reshape(1, d), jnp.zeros((6, d), F32)], axis=0)
    avec = (cvec * jax.nn.sigmoid(cvec) + ag_token[0, 0]).astype(MM_DTYPE)

    def row(v):
        return v.reshape(1, -1)

    X = jnp.concatenate([ctx[0], x[0]], axis=0)
    saved = []
    gk, gv = dm.GK, d
    lrblk = (7 * d + d // 2) // LANES
    for l in range(depth):
        if l == 1:
            got = wait_copies(ag1[1], ag1[2], ag1[3], ag1[4], ag1[0], X, "gather_layer1_wait")
            for n, t in zip(big_names, got):
                full[n][1] = whole(t)
        s = types.SimpleNamespace()
        s.w_in_p = _w_in_t_to_proj(full['w_in'][l], d, wl, wlp)
        wd = full['w_decay'][l]
        wdp = jnp.zeros((LANES, 2 * gk), F32)
        wdp = wdp.at[:GLA_LR, :gk].set(wd[0]).at[GLA_LR:2 * GLA_LR, gk:].set(wd[1])
        s.wdp = wdp.astype(MM_DTYPE)
        s.wdp_wide = jnp.pad(s.wdp, ((0, d // 2 - LANES), (0, 0)))
        s.bd = full['b_decay'][l].reshape(1, 2 * gk)
        modraw = matmul(avec, full['w_ada'][l], 'nn', F32, f"mod_{l}") + full['b_ada'][l][None, :]
        s.mod = [modraw[0:2, j * d:(j + 1) * d].reshape(2, 1, d) for j in range(6)]
        s.x = X
        (s.h,) = rowwise(pre_fn, [X], s.mod[0:2], [row(g_pre_mix[l])], [(d, MM_DTYPE)], dm, f"pre_{l}")
        s.P = matmul(s.h, s.w_in_p, 'nt', MM_DTYPE, f"in_proj_{l}")
        P = s.P
        s.z = matmul((P, LANES, lrblk), s.wdp, 'nn', F32, f"decay_proj_{l}", tk=LANES)
        la_f, la_b = rowwise(decay_fn, [s.z], [], [s.bd], [(gk, F32), (gk, F32)], dm, f"decay_{l}")
        s.la = jnp.concatenate([la_f, la_b], axis=1)
        s.o_f, s.st_f = gla_fwd(P, s.la, False, dm, f"gla_fwd_f_{l}")
        s.o_b, s.st_b = gla_fwd(P, s.la, True, dm, f"gla_fwd_b_{l}")
        (s.gin,) = rowwise(glaout_fn, [s.o_f, s.o_b, (P, d, 3)], [], [row(g_gla[l])], [(gv, MM_DTYPE)], dm,
                           f"gla_out_{l}")
        if l == 0:
            got = wait_copies(ag0[1], ag0[2], ag0[3], ag0[4], ag0[0], s.gin, "gather_layer0_late_wait")
            for k, t in zip(late, got):
                full[big_names[k]][0] = whole(t)
        s.ya = matmul(s.gin, full['w_gla_o'][l], 'nn', MM_DTYPE, f"gla_o_{l}")
        (s.u,) = rowwise(glu_fn, [(P, d, 6)], [], [], [(d // 2, F32)], dm, f"glu_{l}")
        s.yconv = conv_fwd(s.u, full['w_dw'][l], dm, f"conv_{l}")
        (s.cin,) = rowwise(convpost_fn, [s.yconv], [], [row(b_dw[l]), row(g_conv_ln[l]), row(b_conv_ln[l])],
                           [(d // 2, MM_DTYPE)], dm, f"conv_post_{l}")
        s.yb = matmul(s.cin, full['w_conv_o'][l], 'nn', MM_DTYPE, f"conv_o_{l}")
        s.pm = pool_mix((P, d // 2, 14), False, dm, f"pool_mix_{l}")
        s.pc = group_mm(s.pm, w_pool_g[l], 'nn', F32, f"pool_g_{l}")
        (s.pin,) = rowwise(poolpost_fn, [s.pc], [], [row(s_pool[l])], [(d // 2, MM_DTYPE)], dm, f"pool_post_{l}")
        s.yc = matmul(s.pin, full['w_pool_o'][l], 'nn', MM_DTYPE, f"pool_o_{l}")
        s.bg = [row(full['b_gate'][l][j]) for j in range(3)]
        (s.mixed,) = rowwise(merge_fn, [s.ya, s.yb, s.yc, (P, 3 * d, 0)], [], s.bg, [(d, MM_DTYPE)], dm,
                             f"merge_{l}", tm=tmw)
        s.y = matmul(s.mixed, full['w_out'][l], 'nn', MM_DTYPE, f"out_proj_{l}")
        s.x1, s.h2 = rowwise(mid_fn, [X, s.y], s.mod[2:5], [row(g_post_mix[l]), row(g_pre_mlp[l])],
                             [(d, F32), (d, MM_DTYPE)], dm, f"mid_{l}")
        s.act = matmul(s.h2, full['w_mlp1'][l], 'nn', MM_DTYPE, f"mlp1_{l}", epi=relu2_epi)
        s.y2 = matmul(s.act, full['w_mlp2'][l], 'nn', MM_DTYPE, f"mlp2_{l}")
        (X,) = rowwise(post_fn, [s.x1, s.y2], s.mod[5:6], [row(g_post_mlp[l])], [(d, F32)], dm, f"post_{l}")
        saved.append(s)

    dX, lossv = loss_head(X, loss_target[0], dm, "loss_head")
    loss = lax.psum(lossv[0, 0], ("x", "y", "c"))

    grads = {n: [None] * depth for n in WEIGHTS if n != 'c_ctx' and n not in BIG}
    gbig = {n: [None] * depth for n in BIG}
    rs_token = None

    def start_scatter(idx, layer, after, name):
        gs = [gbig[big_names[k]][layer] for k in idx]
        wd = [t.shape[1] // 4 if kinds[k] == 'col' else t.shape[0] // 4 for t, k in zip(gs, idx)]
        plan = _scatter_plan([big_axis[big_names[k]] - 1 for k in idx], wd)
        lands = [lax.empty((3, t.shape[0], w) if kinds[k] == 'col' else (3, w, t.shape[1]), t.dtype)
                 for t, w, k in zip(gs, wd, idx)]
        return (plan,) + start_copies(gs, lands, plan, 3 * len(gs), after, name)

    g_cctx = jnp.zeros((d,), F32)
    for l in reversed(range(depth)):
        s = saved[l]
        P = s.P
        dmod = [None] * 6
        gpm = row(g_post_mlp[l]) if rs_token is None else row(g_post_mlp[l]) + rs_token[0, 0]
        (dx1, dy2), (dmod[5],), (dg,) = rowwise_vjp(post_fn, [s.x1, s.y2], s.mod[5:6], [gpm], [dX],
                                                     dm, f"post_bwd_{l}", narrow=(1,))
        grads['g_post_mlp'][l] = dg[0]
        du1 = matmul(dy2, full['w_mlp2'][l], 'nt', MM_DTYPE, f"mlp2_dx_{l}", epi=relu2_bwd_epi, extras=[s.act])
        gbig['w_mlp2'][l] = matmul(s.act, dy2, 'tn', MM_DTYPE, f"mlp2_dw_{l}")
        dh2 = matmul(du1, full['w_mlp1'][l], 'nt', MM_DTYPE, f"mlp1_dx_{l}")
        gbig['w_mlp1'][l] = matmul(s.h2, du1, 'tn', MM_DTYPE, f"mlp1_dw_{l}")
        gpx = row(g_post_mix[l])
        (dxa, dy), dmod[2:5], (dg1, dg2) = rowwise_vjp(
            mid_fn, [s.x, s.y], s.mod[2:5], [gpx, row(g_pre_mlp[l])], [dx1, dh2], dm, f"mid_bwd_{l}", narrow=(1,))
        grads['g_post_mix'][l], grads['g_pre_mlp'][l] = dg1[0], dg2[0]
        dmixed = matmul(dy, full['w_out'][l], 'nt', MM_DTYPE, f"out_proj_dx_{l}")
        gbig['w_out'][l] = matmul(s.mixed, dy, 'tn', MM_DTYPE, f"out_proj_dw_{l}")
        (dya, dyb, dyc, dP), _, dbg = rowwise_vjp(merge_fn, [s.ya, s.yb, s.yc, (P, 3 * d, 0)], [], s.bg, [dmixed],
                                                  dm, f"merge_bwd_{l}", tm=tmw, narrow=(0, 1, 2),
                                                  into=(3, None, P.shape))
        grads['b_gate'][l] = jnp.concatenate(dbg, axis=0)
        dgin = matmul(dya, full['w_gla_o'][l], 'nt', MM_DTYPE, f"gla_o_dx_{l}")
        gbig['w_gla_o'][l] = matmul(s.gin, dya, 'tn', MM_DTYPE, f"gla_o_dw_{l}")
        dcin = matmul(dyb, full['w_conv_o'][l], 'nt', MM_DTYPE, f"conv_o_dx_{l}")
        gbig['w_conv_o'][l] = matmul(s.cin, dyb, 'tn', MM_DTYPE, f"conv_o_dw_{l}")
        dpin = matmul(dyc, full['w_pool_o'][l], 'nt', MM_DTYPE, f"pool_o_dx_{l}")
        gbig['w_pool_o'][l] = matmul(s.pin, dyc, 'tn', MM_DTYPE, f"pool_o_dw_{l}")
        sp = row(s_pool[l])
        if l == 0:
            rs0 = start_scatter(late, 0, dpin, "grad_layer0_late_start")
            sp = sp + rs0[-1][0, 0]
        (dpc,), _, (dsp,) = rowwise_vjp(poolpost_fn, [s.pc], [], [sp], [dpin], dm, f"pool_post_bwd_{l}")
        grads['s_pool'][l] = dsp[0]
        grads['w_pool_g'][l] = group_mm(s.pm, w_pool_g[l], 'tn', F32, f"pool_g_dw_{l}", b=dpc)
        dpm = group_mm(dpc, w_pool_g[l], 'nt', F32, f"pool_g_dx_{l}")
        dP = pool_mix(dpm, True, dm, f"pool_mix_bwd_{l}", into=(dP, 14))
        (dyconv,), _, (dbdw, dgln, dbln) = rowwise_vjp(
            convpost_fn, [s.yconv], [], [row(b_dw[l]), row(g_conv_ln[l]), row(b_conv_ln[l])], [dcin], dm,
            f"conv_post_bwd_{l}")
        grads['b_dw'][l], grads['g_conv_ln'][l], grads['b_conv_ln'][l] = dbdw[0], dgln[0], dbln[0]
        du, grads['w_dw'][l] = conv_bwd(s.u, full['w_dw'][l], dyconv, dm, f"conv_bwd_{l}")
        (dP,), _, _ = rowwise_vjp(glu_fn, [(P, d, 6)], [], [], [du], dm, f"glu_bwd_{l}", into=(0, dP, P.shape))
        (do, _, dP), _, (dgg,) = rowwise_vjp(glaout_fn, [s.o_f, s.o_b, (P, d, 3)], [], [row(g_gla[l])], [dgin], dm,
                                             f"gla_out_bwd_{l}", want=[True, False, True], into=(2, dP, P.shape))
        grads['g_gla'][l] = dgg[0]
        dqf, dkf, dvf, dlaf = gla_bwd(P, s.la, do, s.st_f, False, dm, f"gla_bwd_f_{l}")
        dP, dlab = gla_bwd(P, s.la, do, s.st_b, True, dm, f"gla_bwd_b_{l}", prev=(dqf, dkf, dvf), into=dP)
        (dz,), _, (dbd,) = rowwise_vjp(decay_fn, [s.z], [], [s.bd], [dlaf, dlab], dm, f"decay_bwd_{l}", narrow=(0,))
        grads['b_decay'][l] = dbd.reshape(2, gk)
        dwdp = matmul((P, LANES, lrblk), dz, 'tn', F32, f"decay_proj_dw_{l}", tm=LANES)
        grads['w_decay'][l] = jnp.stack([dwdp[:GLA_LR, :gk], dwdp[GLA_LR:2 * GLA_LR, gk:]])
        dP = matmul(dz, s.wdp_wide, 'nt', MM_DTYPE, f"decay_proj_dx_{l}", into=(dP, 15))
        dh = matmul(dP, s.w_in_p, 'nn', MM_DTYPE, f"in_proj_dx_{l}")
        gbig['w_in'][l] = _proj_to_w_in_t(matmul(dP, s.h, 'tn', MM_DTYPE, f"in_proj_dw_{l}"), d, wl, wlp)
        (dX,), dmod[0:2], (dg,) = rowwise_vjp(pre_fn, [s.x], s.mod[0:2], [row(g_pre_mix[l])], [dh], dm,
                                               f"pre_bwd_{l}", adds={0: dxa})
        grads['g_pre_mix'][l] = dg[0]
        dmodflat = jnp.concatenate([jnp.concatenate([m_.reshape(2, d) for m_ in dmod], axis=1),
                                    jnp.zeros((6, 6 * d), F32)], axis=0)
        grads['b_ada'][l] = dmodflat[0] + dmodflat[1]
        gbig['w_ada'][l] = matmul(avec, dmodflat, 'tn', MM_DTYPE, f"ada_dw_{l}")
        dav = matmul(dmodflat, full['w_ada'][l], 'nt', F32, f"ada_dx_{l}")
        g_cctx = g_cctx + dav[0] * _silu_grad(c_ctx)
        if l == 1:
            rs1 = start_scatter(list(range(nbig)), 1, dav, "grad_layer1_start")
            rs_token = rs1[-1]

    grad_x = dX[dm.CTX:][None]
    gfull = {n: jnp.stack(v) for n, v in grads.items()}
    gfull['c_ctx'] = g_cctx
    where = jnp.concatenate([chip1, core1])

    def halves_view(t, k):
        return t.reshape(2, t.shape[0] // 2, t.shape[1]) if k == 'col' else t.reshape(4, 2, t.shape[0] // 8, t.shape[1])
    enames = [big_names[k] for k in early]
    ekinds = [kinds[k] for k in early]
    v0 = [halves_view(gbig[n][0], k) for n, k in zip(enames, ekinds)]
    r1 = pair_swap_halves(v0, ekinds, "grad_pair_swap")
    hs = [pair_add(v.reshape((-1,) + v.shape[-2:]), r.reshape((-1,) + r.shape[-2:]), core1, f"grad_pair_add_{n}")
          for n, v, r in zip(enames, v0, r1)]
    hx = [h.reshape(h.shape[1:]) if k == 'col' else h for h, k in zip(hs, ekinds)]
    r2 = chip_exchange(hx, ekinds, "grad_chip_exchange")
    fs = [chip_add(h.reshape(-1, h.shape[-1]), r, big_axis[n] - 1, where, f"grad_chip_add_{n}")
          for n, h, r in zip(enames, hs, r2)]
    red0 = dict(zip(enames, [[t.reshape(-1, t.shape[-1])] for t in pair_join_layers(fs, "grad_pair_join")]))

    got0 = wait_copies(rs0[1], rs0[2], rs0[3], rs0[4], rs0[0], dX, "grad_layer0_late_wait")
    got1 = wait_copies(rs1[1], rs1[2], rs1[3], rs1[4], rs1[0], dX, "grad_layer1_wait")
    sa = [chip_add(g, r, big_axis[big_names[k]] - 1, where, f"grad_layer0_add_{big_names[k]}", slab=False)
          for k, g, r in zip(late, rs0[3], got0)]
    sa += [chip_add(g, r, big_axis[n] - 1, where, f"grad_layer1_add_{n}", slab=False)
           for n, g, r in zip(big_names, rs1[3], got1)]
    sb = pair_swap(sa, "grad_late_pair_swap")
    for j, k in enumerate(late):
        red0[big_names[k]] = [sa[j], sb[j]]
    red1 = {n: [sa[len(late) + k], sb[len(late) + k]] for k, n in enumerate(big_names)}

    sflat = _flatten_pad([gfull[n].astype(F32) for n in SMALL], F32)
    ssum = slot_sum(gather_all_devices(sflat, "small_grad_gather"), "small_grad_sum").reshape(-1)

    out_g, out_d, out_m, out_v = {}, {}, {}, {}
    for k, n in enumerate(big_names):
        out_g[n], out_d[n], out_m[n], out_v[n] = adamw_layers(a[n], a['m_' + n], a['v_' + n], red0[n], red1[n],
                                                              f"adamw_{n}")
    start = 0
    sg = {}
    for n in SMALL:
        cnt = gfull[n].size
        g = ssum[start:start + cnt].reshape(gfull[n].shape)
        start += cnt
        if n in SMALL_SHARDED:
            ax = SMALL_SHARDED[n]
            wdt = a[n].shape[ax]
            g = lax.dynamic_slice_in_dim(g, chip * wdt, wdt, axis=ax)
        sg[n] = g
    pk = lambda dct, pre: _flatten_pad([dct[pre + n] for n in SMALL], F32)
    gs = _flatten_pad([sg[n] for n in SMALL], F32)
    dl, mn, vn = adamw(pk(a, ''), gs, pk(a, 'm_'), pk(a, 'v_'), "adamw_small")
    dl, mn, vn = dl.reshape(-1), mn.reshape(-1), vn.reshape(-1)
    start = 0
    for n in SMALL:
        cnt, shp = a[n].size, a[n].shape
        out_g[n] = sg[n]
        out_d[n], out_m[n], out_v[n] = (t[start:start + cnt].reshape(shp) for t in (dl, mn, vn))
        start += cnt

    for dct in (out_g, out_d, out_m, out_v):
        dct['w_in'] = jnp.swapaxes(dct['w_in'], 1, 2)
    return (loss, grad_x, *[out_g[n] for n in WEIGHTS], *[out_d[n] for n in WEIGHTS],
            *[out_m[n] for n in WEIGHTS], *[out_v[n] for n in WEIGHTS])
```

```python
import functools
import math
import types

import jax
import jax.numpy as jnp
from jax import lax
from jax.experimental import pallas as pl
from jax.experimental.pallas import tpu as pltpu

F32 = jnp.float32
MM_DTYPE = jnp.bfloat16
VMEM_LIMIT_V7X = 56 * 1024 * 1024
LANES = 128
EPS = 1e-6

N_HEADS = 4
GLA_CHUNK = 64
GLA_TAU = 16.0
GLA_LR = 16
GRID_W = 64
POOL_WINDOWS = (2, 4, 8, 16)

ADAM_LR = 0.001
ADAM_B1 = 0.9
ADAM_B2 = 0.999
ADAM_EPS = 1e-08
ADAM_WD = 0.01
ADAM_STEP = 10

NN = (((1,), (0,)), ((), ()))
NT = (((1,), (1,)), ((), ()))
TN = (((0,), (0,)), ((), ()))

WEIGHTS = ['c_ctx', 'w_ada', 'b_ada', 'g_pre_mix', 'g_post_mix', 'g_pre_mlp', 'g_post_mlp', 'w_in', 'w_decay',
           'b_decay', 'g_gla', 'w_gla_o', 'w_dw', 'b_dw', 'g_conv_ln', 'b_conv_ln', 'w_conv_o', 'w_pool_g',
           's_pool', 'w_pool_o', 'b_gate', 'w_out', 'w_mlp1', 'w_mlp2']
BIG = {'w_ada': 2, 'w_in': 2, 'w_gla_o': 1, 'w_conv_o': 2, 'w_pool_o': 2, 'w_out': 1, 'w_mlp1': 2, 'w_mlp2': 1}
SMALL_SHARDED = {'w_decay': 3, 'b_decay': 2, 'w_dw': 2, 'b_gate': 2}
SMALL = [n for n in WEIGHTS if n not in BIG]


def _tile(n, prefs):
    for t in prefs:
        if n % t == 0:
            return t
    return n


def _cparams(sem=None, **kw):
    return pltpu.CompilerParams(dimension_semantics=sem, vmem_limit_bytes=VMEM_LIMIT_V7X, **kw)


def _dot(a, b, dims=NN):
    return lax.dot_general(a.astype(MM_DTYPE), b.astype(MM_DTYPE), dims, preferred_element_type=F32)


def matmul(a, b, mode, out_dtype, name, tm=None, tn=None, tk=None, epi=None, extras=(), into=None):
    a, aw, ablk = a if isinstance(a, tuple) else (a, a.shape[1], 0)
    if mode == 'nn':
        M, K, N = a.shape[0], aw, b.shape[1]
    elif mode == 'nt':
        M, K, N = a.shape[0], aw, b.shape[0]
    else:
        K, M, N = a.shape[0], aw, b.shape[1]
    big = (1088, 1024, 640, 544, 512, 320, 256, 128, 64, 32, 16, 8)
    if mode == 'tn':
        tm = tm or _tile(M, (1024, 512, 256, 128))
        tn = tn or _tile(N, (1024, 512, 256, 128))
        tk = tk or _tile(K, big)
    else:
        tm = tm or _tile(M, big)
        tn = tn or _tile(N, (1024, 512, 256, 128))
        tk = tk or _tile(K, (1024, 512, 256, 128))
    if aw != a.shape[1]:
        assert (mode == 'tn' and tm == aw) or (mode != 'tn' and tk == aw)
    nk = K // tk
    ne = len(extras)
    dims = {'nn': NN, 'nt': NT, 'tn': TN}[mode]

    def body(a_ref, b_ref, *rest):
        e_refs, o_ref = rest[:ne], rest[ne + (into is not None)]

        def finish(acc):
            if epi is not None:
                acc = epi(acc, *[e[...] for e in e_refs])
            o_ref[...] = acc.astype(o_ref.dtype)

        p = _dot(a_ref[...], b_ref[...], dims)
        if nk == 1:
            finish(p)
            return
        acc = rest[-1]
        k = pl.program_id(2)

        @pl.when(k == 0)
        def _():
            acc[...] = p

        @pl.when(k > 0)
        def _():
            acc[...] += p

        @pl.when(k == nk - 1)
        def _():
            finish(acc[...])

    if mode == 'nn':
        a_spec = pl.BlockSpec((tm, tk), lambda i, j, k: (i, k + ablk))
        b_spec = pl.BlockSpec((tk, tn), lambda i, j, k: (k, j))
    elif mode == 'nt':
        a_spec = pl.BlockSpec((tm, tk), lambda i, j, k: (i, k + ablk))
        b_spec = pl.BlockSpec((tn, tk), lambda i, j, k: (j, k))
    else:
        a_spec = pl.BlockSpec((tk, tm), lambda i, j, k: (k, i + ablk))
        b_spec = pl.BlockSpec((tk, tn), lambda i, j, k: (k, j))
    tile = pl.BlockSpec((tm, tn), lambda i, j, k: (i, j))
    if into is None:
        out_spec, out_shape, more, extra, aliases = tile, jax.ShapeDtypeStruct((M, N), out_dtype), [], [], {}
    else:
        buf, oblk = into
        out_spec = pl.BlockSpec((tm, tn), lambda i, j, k: (i, oblk * (N // tn) + j))
        out_shape = jax.ShapeDtypeStruct(buf.shape, buf.dtype)
        more, extra, aliases = [pl.BlockSpec(memory_space=pl.ANY)], [buf], {2 + ne: 0}
    return pl.pallas_call(
        body, name=name, grid=(M // tm, N // tn, nk),
        in_specs=[a_spec, b_spec] + [tile] * ne + more, out_specs=out_spec,
        out_shape=out_shape, input_output_aliases=aliases,
        scratch_shapes=[] if nk == 1 else [pltpu.VMEM((tm, tn), F32)],
        compiler_params=_cparams(("parallel", "parallel", "arbitrary")),
    )(a, b, *extras, *extra)


def group_mm(a, w, mode, out_dtype, name, b=None):
    T = a.shape[0]
    G, gc, _ = w.shape
    col = pl.BlockSpec((T, gc), lambda g: (0, g))
    wsp = pl.BlockSpec((1, gc, gc), lambda g: (g, 0, 0))
    if mode == 'tn':
        def body(a_ref, b_ref, o_ref):
            o_ref[0] = _dot(a_ref[...], b_ref[...], TN).astype(o_ref.dtype)
        return pl.pallas_call(body, name=name, grid=(G,), in_specs=[col, col], out_specs=wsp,
                              out_shape=jax.ShapeDtypeStruct((G, gc, gc), out_dtype),
                              compiler_params=_cparams(("parallel",)))(a, b)
    dims = NN if mode == 'nn' else NT

    def body(a_ref, w_ref, o_ref):
        o_ref[...] = _dot(a_ref[...], w_ref[0], dims).astype(o_ref.dtype)
    return pl.pallas_call(body, name=name, grid=(G,), in_specs=[col, wsp], out_specs=col,
                          out_shape=jax.ShapeDtypeStruct((T, G * gc), out_dtype),
                          compiler_params=_cparams(("parallel",)))(a, w)


def _rowspec(r):
    return r if isinstance(r, tuple) else (r, r.shape[1], 0)


def _row_specs(rows, segs, consts, tm, nctx):
    specs = [pl.BlockSpec((tm, w), lambda i, b=b: (i, b)) for _, w, b in rows]
    specs += [pl.BlockSpec((1,) + s.shape[1:], lambda i, n=s.ndim: (jnp.where(i >= nctx, 1, 0),) + (0,) * (n - 1))
              for s in segs]
    specs += [pl.BlockSpec(c.shape, lambda i, n=c.ndim: (0,) * n) for c in consts]
    return specs


def rowwise(fn, rows, segs, consts, outs, dm, name, tm=None):
    tm = tm or dm.tm
    nctx = dm.CTX // tm
    rows = [_rowspec(r) for r in rows]
    nr, ns, nc = len(rows), len(segs), len(consts)

    def body(*refs):
        rin = [r[...] for r in refs[:nr]]
        sin = [s[0] for s in refs[nr:nr + ns]]
        cin = [c[...] for c in refs[nr + ns:nr + ns + nc]]
        res = fn(*rin, *sin, *cin)
        for o_ref, v in zip(refs[nr + ns + nc:], res):
            o_ref[...] = v.astype(o_ref.dtype)

    res = pl.pallas_call(
        body, name=name, grid=(dm.T // tm,),
        in_specs=_row_specs(rows, segs, consts, tm, nctx),
        out_specs=[pl.BlockSpec((tm, w), lambda i: (i, 0)) for w, _ in outs],
        out_shape=[jax.ShapeDtypeStruct((dm.T, w), dt) for w, dt in outs],
        compiler_params=_cparams(("parallel",)),
    )(*[r[0] for r in rows], *segs, *consts)
    return res


def rowwise_vjp(fn, rows, segs, consts, cots, dm, name, tm=None, want=None, adds=None, narrow=(), into=None):
    tm = tm or dm.tm
    nctx = dm.CTX // tm
    rows = [_rowspec(r) for r in rows]
    cots = [_rowspec(r) for r in cots]
    adds = adds or {}
    nr, ns, nc, nct = len(rows), len(segs), len(consts), len(cots)
    want = want or [True] * nr
    widx = [k for k in range(nr) if want[k]]
    akeys = sorted(adds)

    def body(*refs):
        i = pl.program_id(0)
        rin = [r[...] for r in refs[:nr]]
        sin = [s[0] for s in refs[nr:nr + ns]]
        cin = [c[...] for c in refs[nr + ns:nr + ns + nc]]
        p = nr + ns + nc
        cot_refs = refs[p:p + nct]
        add_refs = dict(zip(akeys, refs[p + nct:p + nct + len(akeys)]))
        p = p + nct + len(akeys) + (1 if (into is not None and into[1] is not None) else 0)
        rg_refs = refs[p:p + len(widx)]
        sg_refs = refs[p + len(widx):p + len(widx) + ns]
        cg_refs = refs[p + len(widx) + ns:]
        res, vjp = jax.vjp(fn, *rin, *sin, *cin)
        g = vjp(tuple(cr[...].astype(o.dtype) for cr, o in zip(cot_refs, res)))
        for o_ref, k in zip(rg_refs, widx):
            v = g[k].astype(F32)
            if k in add_refs:
                v = v + add_refs[k][...]
            o_ref[...] = v.astype(o_ref.dtype)
        first_seg = jnp.logical_or(i == 0, i == nctx)
        for o_ref, v in zip(sg_refs, g[nr:nr + ns]):
            @pl.when(first_seg)
            def _(o_ref=o_ref, v=v):
                o_ref[0] = v.astype(F32)

            @pl.when(jnp.logical_not(first_seg))
            def _(o_ref=o_ref, v=v):
                o_ref[0] += v.astype(F32)
        for o_ref, v in zip(cg_refs, g[nr + ns:]):
            @pl.when(i == 0)
            def _(o_ref=o_ref, v=v):
                o_ref[...] = v.astype(F32)

            @pl.when(i > 0)
            def _(o_ref=o_ref, v=v):
                o_ref[...] += v.astype(F32)

    in_specs = _row_specs(rows, segs, consts, tm, nctx)
    in_specs += [pl.BlockSpec((tm, w), lambda i, b=b: (i, b)) for _, w, b in cots]
    in_specs += [pl.BlockSpec((tm, adds[k].shape[1]), lambda i: (i, 0)) for k in akeys]
    out_specs = [pl.BlockSpec((tm, rows[k][1]), lambda i: (i, 0)) for k in widx]
    out_shape = [jax.ShapeDtypeStruct((dm.T, rows[k][1]), MM_DTYPE if k in narrow else rows[k][0].dtype)
                 for k in widx]
    extra, aliases = [], {}
    if into is not None:
        ik, ibuf, ishape = into
        out_specs[widx.index(ik)] = pl.BlockSpec((tm, rows[ik][1]), lambda i, b=rows[ik][2]: (i, b))
        out_shape[widx.index(ik)] = jax.ShapeDtypeStruct(ishape, MM_DTYPE)
        if ibuf is not None:
            aliases = {len(in_specs): widx.index(ik)}
            in_specs = in_specs + [pl.BlockSpec(memory_space=pl.ANY)]
            extra = [ibuf]
    out_specs += [pl.BlockSpec((1,) + s.shape[1:], lambda i, n=s.ndim: (jnp.where(i >= nctx, 1, 0),) + (0,) * (n - 1))
                  for s in segs]
    out_shape += [jax.ShapeDtypeStruct(s.shape, F32) for s in segs]
    out_specs += [pl.BlockSpec(c.shape, lambda i, n=c.ndim: (0,) * n) for c in consts]
    out_shape += [jax.ShapeDtypeStruct(c.shape, F32) for c in consts]
    res = pl.pallas_call(
        body, name=name, grid=(dm.T // tm,), in_specs=in_specs, out_specs=out_specs, out_shape=out_shape,
        input_output_aliases=aliases, compiler_params=_cparams(("arbitrary",)),
    )(*[r[0] for r in rows], *segs, *consts, *[r[0] for r in cots], *[adds[k] for k in akeys], *extra)
    rg = [None] * nr
    for k, v in zip(widx, res[:len(widx)]):
        rg[k] = v
    return rg, list(res[len(widx):len(widx) + ns]), list(res[len(widx) + ns:])


def _rms(x, g):
    return x * lax.rsqrt(jnp.mean(x * x, axis=-1, keepdims=True) + EPS) * g


def _sigmoid(x):
    return jax.nn.sigmoid(x)


def pre_fn(x, shift, scale, g):
    return ((_rms(x, g) * (1.0 + scale) + shift).astype(MM_DTYPE),)


def mid_fn(x, y, gate, shift, scale, g_post, g_pre):
    x1 = x + gate * _rms(y.astype(F32), g_post)
    return x1, (_rms(x1, g_pre) * (1.0 + scale) + shift).astype(MM_DTYPE)


def post_fn(x1, y2, gate, g):
    return (x1 + gate * _rms(y2.astype(F32), g),)


def relu2_epi(acc):
    r = jnp.maximum(acc, 0.0)
    return r * r


def relu2_bwd_epi(dact, act):
    return dact * (2.0 * jnp.sqrt(act.astype(F32)))


def decay_fn(z, bd):
    zz = z.astype(F32) + bd
    ls = jnp.minimum(zz, 0.0) - jnp.log(1.0 + jnp.exp(jnp.minimum(zz, -zz)))
    la = ls / GLA_TAU
    gk = la.shape[1] // 2
    return la[:, :gk], la[:, gk:]


def glu_fn(ab):
    h = ab.shape[1] // 2
    return (ab[:, :h].astype(F32) * _sigmoid(ab[:, h:].astype(F32)),)


def glaout_fn(o_f, o_b, og, g):
    o = o_f + o_b
    dv = o.shape[1] // N_HEADS
    hs = []
    for h in range(N_HEADS):
        oh = o[:, h * dv:(h + 1) * dv]
        hs.append(oh * lax.rsqrt(jnp.mean(oh * oh, axis=-1, keepdims=True) + EPS) * g[:, h * dv:(h + 1) * dv])
    og = og.astype(F32)
    return ((jnp.concatenate(hs, axis=1) * (og * _sigmoid(og))).astype(MM_DTYPE),)


def convpost_fn(y, b_dw, g, b):
    y = y + b_dw
    mu = jnp.mean(y, axis=-1, keepdims=True)
    xc = y - mu
    yn = xc * lax.rsqrt(jnp.mean(xc * xc, axis=-1, keepdims=True) + EPS) * g + b
    return ((yn * _sigmoid(yn)).astype(MM_DTYPE),)


def poolpost_fn(pc, s):
    return ((pc.astype(F32) * s).astype(MM_DTYPE),)


def merge_fn(ya, yb, yc, mg, bg0, bg1, bg2):
    d = ya.shape[1]
    mg = mg.astype(F32)
    mixed = (_sigmoid(mg[:, :d] + bg0) * ya.astype(F32) + _sigmoid(mg[:, d:2 * d] + bg1) * yb.astype(F32)
             + _sigmoid(mg[:, 2 * d:] + bg2) * yc.astype(F32))
    return (mixed.astype(MM_DTYPE),)


def _split_dot(lmat, x, dims):
    hi = x.astype(MM_DTYPE)
    lo = x - hi.astype(F32)
    return _dot(lmat, hi, dims) + _dot(lmat, lo, dims)


def _gla_block_order(dm, rev):
    nctx, nb = dm.CTX // dm.TB, dm.T // dm.TB

    def blk(i):
        if not rev:
            return i
        return jnp.where(i < nctx, nctx - 1 - i, nb - 1 - (i - nctx))
    return blk, nb


def _gla_tri(rev):
    c = GLA_CHUNK
    t = lax.broadcasted_iota(jnp.int32, (c, c), 0)
    s = lax.broadcasted_iota(jnp.int32, (c, c), 1)
    return (s >= t) if rev else (s <= t)


def _gla_chunk_terms(q, k, la, tri, scale):
    lmat = tri.astype(MM_DTYPE)
    b = _split_dot(lmat, la, NN)
    bend = jnp.sum(la, axis=0, keepdims=True)
    eb = jnp.exp(b)
    enb = jnp.exp(-b)
    ee = jnp.exp(bend - b)
    qi = q * scale * eb
    ki = k * enb
    kend = k * ee
    att = jnp.where(tri, _dot(qi, ki, NT), 0.0)
    return lmat, bend, eb, enb, ee, qi, ki, kend, att


def gla_fwd(P, la, rev, dm, name):
    c, tb, h_, dk, dv, d = GLA_CHUNK, dm.TB, N_HEADS, dm.DK, dm.DV, dm.D
    cpb = tb // c
    blk, nb = _gla_block_order(dm, rev)
    gk, gv = h_ * dk, h_ * dv
    qb, kb, vb, lb = (5 * d) // gk, (5 * d + d // 2) // gk, (4 * d) // gv, (1 if rev else 0)
    scale = dk ** -0.5
    order = list(range(cpb))[::-1] if rev else list(range(cpb))

    def body(q_ref, k_ref, v_ref, la_ref, o_ref, s_ref, st):
        @pl.when(pl.program_id(0) == 0)
        def _():
            st[...] = jnp.zeros_like(st)
        tri = _gla_tri(rev)
        for n, ci in enumerate(order):
            r = pl.ds(ci * c, c)
            for hh in range(h_):
                ck, cv = pl.ds(hh * dk, dk), pl.ds(hh * dv, dv)
                q = q_ref[r, ck].astype(F32)
                k = k_ref[r, ck].astype(F32)
                v = v_ref[r, cv]
                _, bend, _, _, _, qi, _, kend, att = _gla_chunk_terms(q, k, la_ref[r, ck], tri, scale)
                s_in = st[hh]
                o_ref[r, cv] = _dot(att, v) + _dot(qi, s_in, NT)
                s_ref[n, hh] = s_in
                st[hh] = jnp.exp(bend) * s_in + _dot(v, kend, TN)

    return pl.pallas_call(
        body, name=name, grid=(nb,),
        in_specs=[pl.BlockSpec((tb, gk), lambda i: (blk(i), qb)),
                  pl.BlockSpec((tb, gk), lambda i: (blk(i), kb)),
                  pl.BlockSpec((tb, gv), lambda i: (blk(i), vb)),
                  pl.BlockSpec((tb, gk), lambda i: (blk(i), lb))],
        out_specs=[pl.BlockSpec((tb, gv), lambda i: (blk(i), 0)),
                   pl.BlockSpec((cpb, h_, dv, dk), lambda i: (i, 0, 0, 0))],
        out_shape=[jax.ShapeDtypeStruct((dm.T, gv), F32),
                   jax.ShapeDtypeStruct((dm.T // c, h_, dv, dk), F32)],
        scratch_shapes=[pltpu.VMEM((h_, dv, dk), F32)],
        compiler_params=_cparams(("arbitrary",)),
    )(P, P, P, la)


def gla_bwd(P, la, do, states, rev, dm, name, prev=None, into=None):
    c, tb, h_, dk, dv, d = GLA_CHUNK, dm.TB, N_HEADS, dm.DK, dm.DV, dm.D
    cpb = tb // c
    blk, nb = _gla_block_order(dm, rev)
    gk, gv = h_ * dk, h_ * dv
    qb, kb, vb, lb = (5 * d) // gk, (5 * d + d // 2) // gk, (4 * d) // gv, (1 if rev else 0)
    scale = dk ** -0.5
    order = list(range(cpb))[::-1] if rev else list(range(cpb))

    fused = prev is not None

    def body(q_ref, k_ref, v_ref, la_ref, do_ref, s_ref, *rest):
        if fused:
            pq_ref, pk_ref, pv_ref, _, w_ref, dla_ref, dst = rest
        else:
            dq_ref, dk_ref, dv_ref, dla_ref, dst = rest

        def put(kind, r, cols, val):
            if not fused:
                {'q': dq_ref, 'k': dk_ref, 'v': dv_ref}[kind][r, cols] = val
                return
            p_ref, off = {'q': (pq_ref, gv), 'k': (pk_ref, gv + gk), 'v': (pv_ref, 0)}[kind]
            w_ref[r, pl.ds(off + cols.start, cols.size)] = (val + p_ref[r, cols]).astype(w_ref.dtype)

        @pl.when(pl.program_id(0) == 0)
        def _():
            dst[...] = jnp.zeros_like(dst)
        tri = _gla_tri(rev)
        for n in range(cpb - 1, -1, -1):
            r = pl.ds(order[n] * c, c)
            for hh in range(h_):
                ck, cv = pl.ds(hh * dk, dk), pl.ds(hh * dv, dv)
                q = q_ref[r, ck].astype(F32)
                k = k_ref[r, ck].astype(F32)
                v = v_ref[r, cv]
                lmat, bend, eb, enb, ee, qi, ki, kend, att = _gla_chunk_terms(q, k, la_ref[r, ck], tri, scale)
                s_in = s_ref[n, hh]
                ds_out = dst[hh]
                dob = do_ref[r, cv]
                datt = jnp.where(tri, _dot(dob, v, NT), 0.0)
                dqi = _dot(datt, ki) + _dot(dob, s_in)
                dki = _dot(datt, qi, TN)
                put('v', r, cv, _dot(att, dob, TN) + _dot(kend, ds_out, NT))
                dkend = _dot(v, ds_out)
                gam = jnp.exp(bend)
                dgam = jnp.sum(ds_out * s_in, axis=0, keepdims=True)
                dst[hh] = gam * ds_out + _dot(dob, qi, TN)
                put('q', r, ck, dqi * (scale * eb))
                put('k', r, ck, dki * enb + dkend * ee)
                db = dqi * qi - dki * ki - dkend * kend
                dbend = jnp.sum(dkend * kend, axis=0, keepdims=True) + dgam * gam
                dla_ref[r, ck] = _split_dot(lmat, db, TN) + dbend

    def bi(j):
        return blk(nb - 1 - j)

    in_specs = [
        pl.BlockSpec((tb, gk), lambda j: (bi(j), qb)),
        pl.BlockSpec((tb, gk), lambda j: (bi(j), kb)),
        pl.BlockSpec((tb, gv), lambda j: (bi(j), vb)),
        pl.BlockSpec((tb, gk), lambda j: (bi(j), lb)),
        pl.BlockSpec((tb, gv), lambda j: (bi(j), 0)),
        pl.BlockSpec((cpb, h_, dv, dk), lambda j: (nb - 1 - j, 0, 0, 0)),
    ]
    small = pl.BlockSpec((tb, gk), lambda j: (bi(j), 0))
    wide = pl.BlockSpec((tb, gv), lambda j: (bi(j), 0))
    if not fused:
        return pl.pallas_call(
            body, name=name, grid=(nb,), in_specs=in_specs, out_specs=[small, small, wide, small],
            out_shape=[jax.ShapeDtypeStruct((dm.T, gk), F32), jax.ShapeDtypeStruct((dm.T, gk), F32),
                       jax.ShapeDtypeStruct((dm.T, gv), F32), jax.ShapeDtypeStruct((dm.T, gk), F32)],
            scratch_shapes=[pltpu.VMEM((h_, dv, dk), F32)],
            compiler_params=_cparams(("arbitrary",)),
        )(P, P, P, la, do, states)
    return pl.pallas_call(
        body, name=name, grid=(nb,),
        in_specs=in_specs + [small, small, wide, pl.BlockSpec(memory_space=pl.ANY)],
        out_specs=[pl.BlockSpec((tb, 2 * gv), lambda j: (bi(j), vb // 2)), small],
        out_shape=[jax.ShapeDtypeStruct(into.shape, into.dtype), jax.ShapeDtypeStruct((dm.T, gk), F32)],
        input_output_aliases={9: 0},
        scratch_shapes=[pltpu.VMEM((h_, dv, dk), F32)],
        compiler_params=_cparams(("arbitrary",)),
    )(P, P, P, la, do, states, *prev, into)


def _pos(n, period):
    t = lax.broadcasted_iota(jnp.int32, (n, 1), 0)
    if period & (period - 1) == 0:
        return jnp.bitwise_and(t, period - 1)
    return lax.rem(t, period)


def _conv_segments(dm):
    return [(0, dm.CTX, dm.CTX), (dm.CTX, dm.SEQ, GRID_W)]


def conv_fwd(u, w, dm, name):
    kw, cw = w.shape
    segs = _conv_segments(dm)

    def body(u_ref, w_ref, y_ref):
        for r0, n, per in segs:
            useg = u_ref[r0:r0 + n, :]
            p = _pos(n, per)
            acc = jnp.zeros_like(useg)
            for kk in range(kw):
                d = kk - kw // 2
                sh = useg if d == 0 else pltpu.roll(useg, (-d) % n, 0)
                ok = jnp.logical_and(p + d >= 0, p + d < per)
                acc = acc + jnp.where(ok, sh, 0.0) * w_ref[kk:kk + 1, :]
            y_ref[r0:r0 + n, :] = acc

    return pl.pallas_call(
        body, name=name, grid=(cw // LANES,),
        in_specs=[pl.BlockSpec((dm.T, LANES), lambda j: (0, j)), pl.BlockSpec((kw, LANES), lambda j: (0, j))],
        out_specs=pl.BlockSpec((dm.T, LANES), lambda j: (0, j)),
        out_shape=jax.ShapeDtypeStruct((dm.T, cw), F32),
        compiler_params=_cparams(("parallel",)),
    )(u, w)


def conv_bwd(u, w, dy, dm, name):
    kw, cw = w.shape
    segs = _conv_segments(dm)

    def body(u_ref, w_ref, dy_ref, du_ref, dw_ref):
        dws = [jnp.zeros((1, LANES), F32)] * kw
        for r0, n, per in segs:
            useg = u_ref[r0:r0 + n, :]
            dyseg = dy_ref[r0:r0 + n, :]
            p = _pos(n, per)
            acc = jnp.zeros_like(useg)
            for kk in range(kw):
                d = kk - kw // 2
                shu = useg if d == 0 else pltpu.roll(useg, (-d) % n, 0)
                okf = jnp.logical_and(p + d >= 0, p + d < per)
                dws[kk] = dws[kk] + jnp.sum(jnp.where(okf, shu, 0.0) * dyseg, axis=0, keepdims=True)
                shd = dyseg if d == 0 else pltpu.roll(dyseg, d % n, 0)
                okb = jnp.logical_and(p - d >= 0, p - d < per)
                acc = acc + jnp.where(okb, shd, 0.0) * w_ref[kk:kk + 1, :]
            du_ref[r0:r0 + n, :] = acc
        for kk in range(kw):
            dw_ref[kk:kk + 1, :] = dws[kk]

    return pl.pallas_call(
        body, name=name, grid=(cw // LANES,),
        in_specs=[pl.BlockSpec((dm.T, LANES), lambda j: (0, j)), pl.BlockSpec((kw, LANES), lambda j: (0, j)),
                  pl.BlockSpec((dm.T, LANES), lambda j: (0, j))],
        out_specs=[pl.BlockSpec((dm.T, LANES), lambda j: (0, j)), pl.BlockSpec((kw, LANES), lambda j: (0, j))],
        out_shape=[jax.ShapeDtypeStruct((dm.T, cw), F32), jax.ShapeDtypeStruct((kw, cw), F32)],
        compiler_params=_cparams(("parallel",)),
    )(u, w, dy)


def pool_mix(u, transpose, dm, name, into=None):
    u, uw, ublk = _rowspec(u)
    gc = dm.GC
    ng = len(POOL_WINDOWS)
    rows = dm.SEQ // GRID_W
    segs = [(0, dm.CTX, 1, dm.CTX), (dm.CTX, dm.SEQ, GRID_W, rows)]

    def one_group(u_ref, o_ref, win):
        left = win // 2
        right = win - 1 - left
        for r0, n, stride, length in segs:
            useg = u_ref[r0:r0 + n, :].astype(F32)
            t = lax.broadcasted_iota(jnp.int32, (n, 1), 0)
            p = t if stride == 1 else jnp.right_shift(t, stride.bit_length() - 1)
            cnt = (jnp.minimum(p + right + 1, length) - jnp.maximum(p - left, 0)).astype(F32)
            src = useg / cnt if transpose else useg
            acc = jnp.zeros_like(useg)
            for d in range(-left, right + 1):
                dd = -d if transpose else d
                sh = src if d == 0 else pltpu.roll(src, (-dd * stride) % n, 0)
                ok = jnp.logical_and(p + dd >= 0, p + dd < length)
                acc = acc + jnp.where(ok, sh, 0.0)
            o_ref[r0:r0 + n, :] = ((acc - useg) if transpose else (acc / cnt - useg)).astype(o_ref.dtype)

    def body(u_ref, *rest):
        o_ref = rest[-1]
        g = pl.program_id(0)
        for gi, win in enumerate(POOL_WINDOWS):
            @pl.when(g == gi)
            def _(win=win):
                one_group(u_ref, o_ref, win)

    base = ublk * (uw // gc)
    if into is None:
        obase, out_shape, more, extra, aliases = 0, jax.ShapeDtypeStruct((dm.T, ng * gc), F32), [], [], {}
    else:
        buf, oblk = into
        obase, out_shape = oblk * ng, jax.ShapeDtypeStruct(buf.shape, buf.dtype)
        more, extra, aliases = [pl.BlockSpec(memory_space=pl.ANY)], [buf], {1: 0}
    return pl.pallas_call(
        body, name=name, grid=(ng,),
        in_specs=[pl.BlockSpec((dm.T, gc), lambda g: (0, base + g))] + more,
        out_specs=pl.BlockSpec((dm.T, gc), lambda g: (0, obase + g)),
        out_shape=out_shape, input_output_aliases=aliases,
        compiler_params=_cparams(("parallel",)),
    )(u, *extra)


def loss_head(x2, target, dm, name):
    tm, d = dm.tm, dm.D
    nctx = dm.CTX // tm

    def body(x_ref, t_ref, dx_ref, l_ref):
        i = pl.program_id(0)

        @pl.when(i == 0)
        def _():
            l_ref[...] = jnp.zeros_like(l_ref)

        @pl.when(i < nctx)
        def _():
            dx_ref[...] = jnp.zeros_like(dx_ref)

        @pl.when(i >= nctx)
        def _():
            e = x_ref[...] - t_ref[...]
            dx_ref[...] = e / d
            l_ref[...] += jnp.full(l_ref.shape, 0.5 * jnp.sum(jnp.mean(e * e, axis=-1)), F32)

    return pl.pallas_call(
        body, name=name, grid=(dm.T // tm,),
        in_specs=[pl.BlockSpec((tm, d), lambda i: (i, 0)),
                  pl.BlockSpec((tm, d), lambda i: (jnp.maximum(i - nctx, 0), 0))],
        out_specs=[pl.BlockSpec((tm, d), lambda i: (i, 0)), pl.BlockSpec((8, LANES), lambda i: (0, 0))],
        out_shape=[jax.ShapeDtypeStruct((dm.T, d), F32), jax.ShapeDtypeStruct((8, LANES), F32)],
        compiler_params=_cparams(("arbitrary",)),
    )(x2, target)


def adamw(w, g, m, v, name):
    r, c = w.shape
    tr = _tile(r, tuple(t for t in (512, 256, 128, 64, 32, 16, 8) if t * c * 4 <= (1 << 20)) or (8,))

    def body(w_ref, g_ref, m_ref, v_ref, d_ref, mo_ref, vo_ref):
        gg = g_ref[...]
        mm = ADAM_B1 * m_ref[...] + (1.0 - ADAM_B1) * gg
        vv = ADAM_B2 * v_ref[...] + (1.0 - ADAM_B2) * (gg * gg)
        m_hat = mm / (1.0 - ADAM_B1 ** ADAM_STEP)
        v_hat = vv / (1.0 - ADAM_B2 ** ADAM_STEP)
        d_ref[...] = -ADAM_LR * (m_hat / (jnp.sqrt(v_hat) + ADAM_EPS) + ADAM_WD * w_ref[...])
        mo_ref[...] = mm
        vo_ref[...] = vv

    spec = pl.BlockSpec((tr, c), lambda i: (i, 0))
    return pl.pallas_call(
        body, name=name, grid=(r // tr,), in_specs=[spec] * 4, out_specs=[spec] * 3,
        out_shape=[jax.ShapeDtypeStruct((r, c), F32)] * 3,
        compiler_params=_cparams(("parallel",)),
    )(w, g, m, v)


def pair_add(g, r1, cidx, name):
    ng, r_, n_ = r1.shape
    tr = _tile(r_, tuple(t for t in (1024, 512, 256, 128, 64, 32, 16) if t * n_ * 4 <= (2 << 20)))

    def body(s_ref, g_ref, r_ref, o_ref):
        o_ref[...] = (g_ref[...].astype(F32) + r_ref[...].astype(F32)).astype(o_ref.dtype)

    return pl.pallas_call(
        body, name=name,
        grid_spec=pltpu.PrefetchScalarGridSpec(
            num_scalar_prefetch=1, grid=(ng, r_ // tr),
            in_specs=[pl.BlockSpec((None, tr, n_), lambda k, i, s: (2 * k + s[0], i, 0)),
                      pl.BlockSpec((None, tr, n_), lambda k, i, s: (k, i, 0))],
            out_specs=pl.BlockSpec((None, tr, n_), lambda k, i, s: (k, i, 0))),
        out_shape=jax.ShapeDtypeStruct((ng, r_, n_), g.dtype),
        compiler_params=_cparams(("parallel", "parallel")),
    )(cidx, g, r1)


def chip_add(h, r2, axis, where, name, slab=True):
    _, kl, nl = r2.shape
    tr = _tile(kl, tuple(t for t in (1024, 512, 256, 128, 64, 32, 16) if t * nl * 4 <= (1 << 20)))
    nrb = kl // tr

    def body(s_ref, h_ref, r_ref, o_ref):
        acc = h_ref[...].astype(F32)
        for k in range(r2.shape[0]):
            acc = acc + r_ref[k].astype(F32)
        o_ref[...] = acc

    h_map = (lambda i, s: (s[0] * nrb + i, 0)) if axis == 0 else (lambda i, s: (i, s[0]))
    if slab:
        out_spec = pl.BlockSpec((None, tr, nl), lambda i, s: (s[1], i, 0))
        out_shape = jax.ShapeDtypeStruct((2, kl, nl), F32)
    else:
        out_spec = pl.BlockSpec((tr, nl), lambda i, s: (i, 0))
        out_shape = jax.ShapeDtypeStruct((kl, nl), F32)
    return pl.pallas_call(
        body, name=name,
        grid_spec=pltpu.PrefetchScalarGridSpec(
            num_scalar_prefetch=1, grid=(nrb,),
            in_specs=[pl.BlockSpec((tr, nl), h_map),
                      pl.BlockSpec((r2.shape[0], tr, nl), lambda i, s: (0, i, 0))],
            out_specs=out_spec),
        out_shape=out_shape,
        compiler_params=_cparams(("parallel",)),
    )(where, h, r2)


def adamw_layers(w, m, v, terms0, terms1, name):
    _, a_, b_ = w.shape
    tr = _tile(a_, tuple(t for t in (512, 256, 128, 64, 32) if t * b_ * 4 <= (1 << 20)))
    by_cols = tr == a_ and a_ * b_ * 4 > (1 << 20)
    blk = (a_, LANES) if by_cols else (tr, b_)
    steps = b_ // LANES if by_cols else a_ // tr
    at = (lambda i: (0, i)) if by_cols else (lambda i: (i, 0))
    n0 = len(terms0)

    def update(g, w_ref, m_ref, v_ref, g_ref, d_ref, mo_ref, vo_ref):
        mm = ADAM_B1 * m_ref[...] + (1.0 - ADAM_B1) * g
        vv = ADAM_B2 * v_ref[...] + (1.0 - ADAM_B2) * (g * g)
        m_hat = mm / (1.0 - ADAM_B1 ** ADAM_STEP)
        v_hat = vv / (1.0 - ADAM_B2 ** ADAM_STEP)
        g_ref[...] = g
        d_ref[...] = -ADAM_LR * (m_hat / (jnp.sqrt(v_hat) + ADAM_EPS) + ADAM_WD * w_ref[...])
        mo_ref[...] = mm
        vo_ref[...] = vv

    def total(refs):
        g = refs[0][...]
        for r in refs[1:]:
            g = g + r[...]
        return g

    def body(w_ref, m_ref, v_ref, *rest):
        t_refs, outs = rest[:-4], rest[-4:]
        layer = pl.program_id(0)

        @pl.when(layer == 0)
        def _():
            update(total(t_refs[:n0]), w_ref, m_ref, v_ref, *outs)

        @pl.when(layer == 1)
        def _():
            update(total(t_refs[n0:]), w_ref, m_ref, v_ref, *outs)

    stacked = pl.BlockSpec((None,) + blk, lambda l, i: (l,) + at(i))
    return pl.pallas_call(
        body, name=name, grid=(2, steps),
        in_specs=[stacked] * 3 + [pl.BlockSpec(blk, lambda l, i: at(i * (1 - l)))] * n0
        + [pl.BlockSpec(blk, lambda l, i: at(i * l))] * len(terms1),
        out_specs=[stacked] * 4, out_shape=[jax.ShapeDtypeStruct(w.shape, F32)] * 4,
        compiler_params=_cparams(("arbitrary", "arbitrary")),
    )(w, m, v, *terms0, *terms1)


MESH = pl.DeviceIdType.MESH
ANY = pl.BlockSpec(memory_space=pl.ANY)
HBM = pl.BlockSpec(memory_space=pltpu.HBM)
SEM = pl.BlockSpec(memory_space=pltpu.SEMAPHORE)
EFFECT = pltpu.SideEffectType.DATAFLOW_SIDE_EFFECTING


def _place():
    return lax.axis_index("x"), lax.axis_index("y"), lax.axis_index("c")


def _peers(x, y):
    return [(1 - x, y), (x, 1 - y), (1 - x, 1 - y)]


def _rcopy(src, dst, ssem, rsem, dev):
    return pltpu.make_async_remote_copy(src_ref=src, dst_ref=dst, send_sem=ssem, recv_sem=rsem,
                                        device_id=dev, device_id_type=MESH)


def _gathered_shape(src, kind):
    h, a_, b_ = src.shape
    return (h, a_, 4 * b_) if kind == 'col' else (4, h, a_, b_)


def _win(ref, kind, ch, width):
    return ref.at[:, :, pl.ds(ch * width, width)] if kind == 'col' else ref.at[ch]


def _rect(ref, kind, half, ch, width):
    return ref.at[half, :, pl.ds(ch * width, width)] if kind == 'col' else ref.at[ch, half]


def gather_halves(srcs, kinds, name):
    nw = len(srcs)
    widths = [s.shape[2] for s in srcs]

    def body(*refs):
        src, out = refs[:nw], refs[nw:2 * nw]
        ssem, rsem, osend, orecv = refs[2 * nw:]
        x, y, c = _place()
        chip = 2 * x + y
        sib = (x, y, 1 - c)
        peers = _peers(x, y)
        pidx = [2 * px + py for px, py in peers]

        def rect(n, half, ch):
            return _rect(out[n], kinds[n], half, ch, widths[n])

        mine = [_rcopy(src[n], _win(out[n], kinds[n], chip, widths[n]), osend.at[n], orecv.at[n], sib)
                for n in range(nw)]
        first = [[_rcopy(src[n].at[c], rect(n, c, chip), ssem.at[6 * n + k], rsem.at[6 * n + k], (px, py, c))
                  for k, (px, py) in enumerate(peers)] for n in range(nw)]
        for n in range(nw):
            for cp in first[n]:
                cp.start()
        for cp in mine:
            cp.start()
        passed = [[_rcopy(rect(n, c, pidx[k]), rect(n, c, pidx[k]), ssem.at[6 * n + 3 + k], rsem.at[6 * n + 3 + k], sib)
                   for k in range(3)] for n in range(nw)]
        for n in range(nw):
            for k, (px, py) in enumerate(peers):
                _rcopy(rect(n, c, pidx[k]), rect(n, c, pidx[k]), ssem.at[6 * n + k], rsem.at[6 * n + k],
                       (px, py, c)).wait_recv()
                passed[n][k].start()
        for n in range(nw):
            for k in range(3):
                _rcopy(rect(n, 1 - c, pidx[k]), rect(n, 1 - c, pidx[k]), ssem.at[6 * n + 3 + k],
                       rsem.at[6 * n + 3 + k], sib).wait_recv()
        for n in range(nw):
            for cp in first[n] + passed[n]:
                cp.wait_send()
        for cp in mine:
            cp.wait()

    return pl.pallas_call(
        body, name=name, in_specs=[ANY] * nw, out_specs=[ANY] * nw,
        out_shape=[jax.ShapeDtypeStruct(_gathered_shape(s, k), s.dtype) for s, k in zip(srcs, kinds)],
        scratch_shapes=[pltpu.SemaphoreType.DMA((6 * nw,)), pltpu.SemaphoreType.DMA((6 * nw,)),
                        pltpu.SemaphoreType.DMA((nw,)), pltpu.SemaphoreType.DMA((nw,))],
    )(*srcs)


def _gather_plan(kinds, widths):
    def plan(src, land, x, y, c):
        chip = 2 * x + y
        out = []
        for n in range(len(src)):
            for px, py in _peers(x, y):
                out.append((src[n].at[c], _rect(land[n], kinds[n], c, chip, widths[n]), (px, py, c),
                            _rect(land[n], kinds[n], c, 2 * px + py, widths[n])))
            mine = _win(land[n], kinds[n], chip, widths[n])
            out.append((src[n], mine, (x, y, 1 - c), mine))
        return out
    return plan


def forward_halves(lands, kinds, name):
    nw = len(lands)
    widths = [t.shape[-1] // 4 if k == 'col' else t.shape[-1] for t, k in zip(lands, kinds)]

    def body(*refs):
        o = refs[nw:2 * nw]
        ssem, rsem = refs[2 * nw:]
        x, y, c = _place()
        sib = (x, y, 1 - c)
        pidx = [2 * px + py for px, py in _peers(x, y)]
        cps = [_rcopy(_rect(o[n], kinds[n], c, pidx[j], widths[n]), _rect(o[n], kinds[n], c, pidx[j], widths[n]),
                      ssem.at[3 * n + j], rsem.at[3 * n + j], sib) for n in range(nw) for j in range(3)]
        for cp in cps:
            cp.start()
        for n in range(nw):
            for j in range(3):
                cps[3 * n + j].wait_send()
                _rcopy(_rect(o[n], kinds[n], 1 - c, pidx[j], widths[n]), _rect(o[n], kinds[n], 1 - c, pidx[j], widths[n]),
                       ssem.at[3 * n + j], rsem.at[3 * n + j], sib).wait_recv()

    return pl.pallas_call(
        body, name=name, in_specs=[ANY] * nw, out_specs=[ANY] * nw,
        out_shape=[jax.ShapeDtypeStruct(t.shape, t.dtype) for t in lands],
        input_output_aliases={n: n for n in range(nw)},
        scratch_shapes=[pltpu.SemaphoreType.DMA((3 * nw,)), pltpu.SemaphoreType.DMA((3 * nw,))],
    )(*lands)


def _scatter_plan(axes, widths):
    def plan(src, land, x, y, c):
        out = []
        for n in range(len(src)):
            for k, (px, py) in enumerate(_peers(x, y)):
                ch = 2 * px + py
                view = (src[n].at[:, pl.ds(ch * widths[n], widths[n])] if axes[n] == 1
                        else src[n].at[pl.ds(ch * widths[n], widths[n]), :])
                out.append((view, land[n].at[k], (px, py, c), land[n].at[k]))
        return out
    return plan


def start_copies(srcs, lands, plan, ncopies, after, name):
    ns, nl = len(srcs), len(lands)

    def body(*refs):
        src, land = refs[:ns], refs[ns:ns + nl]
        ssem, rsem = refs[ns + nl + 1], refs[ns + nl + 2]
        token = refs[-1]
        x, y, c = _place()
        for k, (sv, dv, dev, _) in enumerate(plan(src, land, x, y, c)):
            _rcopy(sv, dv, ssem.at[k], rsem.at[k], dev).start()
        token[...] = jnp.zeros_like(token)

    hbm = lambda t: pltpu.HBM(t.shape, t.dtype)
    res = pl.pallas_call(
        body, name=name,
        out_shape=(pltpu.SemaphoreType.DMA((ncopies,)), pltpu.SemaphoreType.DMA((ncopies,)),
                   *[hbm(t) for t in srcs], *[hbm(t) for t in lands], jax.ShapeDtypeStruct((8, LANES), F32)),
        in_specs=[HBM] * (ns + nl) + [ANY],
        out_specs=(SEM, SEM, *[HBM] * (ns + nl), pl.BlockSpec(memory_space=pltpu.VMEM)),
        input_output_aliases={k: 2 + k for k in range(ns + nl)},
        compiler_params=pltpu.CompilerParams(has_side_effects=EFFECT),
    )(*[pltpu.with_memory_space_constraint(t, pltpu.HBM) for t in list(srcs) + list(lands)], after)
    return res[0], res[1], list(res[2:2 + ns]), list(res[2 + ns:2 + ns + nl]), res[-1]


def wait_copies(ssem, rsem, srcs, lands, plan, after, name):
    ns, nl = len(srcs), len(lands)

    def body(*refs):
        src, land = refs[:ns], refs[ns:ns + nl]
        ss, rs = refs[ns + nl], refs[ns + nl + 1]
        x, y, c = _place()
        for k, (sv, dv, dev, mine) in enumerate(plan(src, land, x, y, c)):
            cp = _rcopy(sv, mine, ss.at[k], rs.at[k], dev)
            cp.wait_send()
            cp.wait_recv()

    hbm = lambda t: pltpu.HBM(t.shape, t.dtype)
    res = pl.pallas_call(
        body, name=name,
        out_shape=(*[hbm(t) for t in srcs], *[hbm(t) for t in lands]),
        in_specs=[HBM] * (ns + nl) + [SEM, SEM, ANY], out_specs=tuple([HBM] * (ns + nl)),
        input_output_aliases={k: k for k in range(ns + nl)},
        compiler_params=pltpu.CompilerParams(has_side_effects=EFFECT),
    )(*srcs, *lands, ssem, rsem, after)
    return list(res[ns:])


def pair_swap_halves(gs, kinds, name):
    nw = len(gs)

    def other(ref, kind, half):
        return ref.at[half] if kind == 'col' else ref.at[:, half]

    def body(*refs):
        g, o = refs[:nw], refs[nw:2 * nw]
        ssem, rsem = refs[2 * nw:]
        x, y, c = _place()
        cps = [_rcopy(other(g[n], kinds[n], 1 - c), o[n], ssem.at[n], rsem.at[n], (x, y, 1 - c)) for n in range(nw)]
        for cp in cps:
            cp.start()
        for cp in cps:
            cp.wait()

    return pl.pallas_call(
        body, name=name, in_specs=[ANY] * nw, out_specs=[ANY] * nw,
        out_shape=[jax.ShapeDtypeStruct(g.shape[1:] if k == 'col' else (g.shape[0],) + g.shape[2:], g.dtype)
                   for g, k in zip(gs, kinds)],
        scratch_shapes=[pltpu.SemaphoreType.DMA((nw,)), pltpu.SemaphoreType.DMA((nw,))],
    )(*gs)


def pair_swap(fs, name):
    nw = len(fs)

    def body(*refs):
        f, o = refs[:nw], refs[nw:2 * nw]
        ssem, rsem = refs[2 * nw:]
        x, y, c = _place()
        cps = [_rcopy(f[n], o[n], ssem.at[n], rsem.at[n], (x, y, 1 - c)) for n in range(nw)]
        for cp in cps:
            cp.start()
        for cp in cps:
            cp.wait()

    return pl.pallas_call(
        body, name=name, in_specs=[ANY] * nw, out_specs=[ANY] * nw,
        out_shape=[jax.ShapeDtypeStruct(f.shape, f.dtype) for f in fs],
        scratch_shapes=[pltpu.SemaphoreType.DMA((nw,)), pltpu.SemaphoreType.DMA((nw,))],
    )(*fs)


def chip_exchange(hs, kinds, name):
    nw = len(hs)
    shp = [(h.shape[0], h.shape[1] // 4) if k == 'col' else h.shape[1:] for h, k in zip(hs, kinds)]

    def body(*refs):
        h, o = refs[:nw], refs[nw:2 * nw]
        ssem, rsem = refs[2 * nw:]
        x, y, c = _place()

        def win(n, ch):
            return h[n].at[:, pl.ds(ch * shp[n][1], shp[n][1])] if kinds[n] == 'col' else h[n].at[ch]

        cps = [_rcopy(win(n, 2 * px + py), o[n].at[k], ssem.at[3 * n + k], rsem.at[3 * n + k], (px, py, c))
               for n in range(nw) for k, (px, py) in enumerate(_peers(x, y))]
        for cp in cps:
            cp.start()
        for cp in cps:
            cp.wait()

    return pl.pallas_call(
        body, name=name, in_specs=[ANY] * nw, out_specs=[ANY] * nw,
        out_shape=[jax.ShapeDtypeStruct((3,) + sh, h.dtype) for sh, h in zip(shp, hs)],
        scratch_shapes=[pltpu.SemaphoreType.DMA((3 * nw,)), pltpu.SemaphoreType.DMA((3 * nw,))],
    )(*hs)


def chip_broadcast(h, name):
    def body(h_ref, o_ref, ssem, rsem):
        x, y, c = _place()
        cps = [_rcopy(h_ref, o_ref.at[j], ssem.at[j], rsem.at[j], (px, py, c)) for j, (px, py) in enumerate(_peers(x, y))]
        for cp in cps:
            cp.start()
        for cp in cps:
            cp.wait()

    return pl.pallas_call(
        body, name=name, in_specs=[ANY], out_specs=ANY,
        out_shape=jax.ShapeDtypeStruct((3,) + h.shape, h.dtype),
        scratch_shapes=[pltpu.SemaphoreType.DMA((3,)), pltpu.SemaphoreType.DMA((3,))],
    )(h)


def quad_sum(h, r, cidx, name):
    r_, c_ = h.shape
    tr = _tile(r_, (512, 256, 128, 64, 32, 16, 8))

    def body(s_ref, h_ref, r_ref, o_ref):
        o_ref[...] = (h_ref[...] + r_ref[2]) + (r_ref[0] + r_ref[1])

    return pl.pallas_call(
        body, name=name,
        grid_spec=pltpu.PrefetchScalarGridSpec(
            num_scalar_prefetch=1, grid=(r_ // tr,),
            in_specs=[pl.BlockSpec((tr, c_), lambda i, s: (i, 0)), pl.BlockSpec((3, tr, c_), lambda i, s: (0, i, 0))],
            out_specs=pl.BlockSpec((None, tr, c_), lambda i, s: (s[0], i, 0))),
        out_shape=jax.ShapeDtypeStruct((2, r_, c_), F32),
        compiler_params=_cparams(("parallel",)),
    )(cidx, h, r)


def pair_join_layers(fs, name):
    nw = len(fs)

    def body(*refs):
        o = refs[nw:2 * nw]
        ssem, rsem = refs[2 * nw:]
        x, y, c = _place()
        sib = (x, y, 1 - c)
        cps = [_rcopy(o[n].at[c], o[n].at[c], ssem.at[n], rsem.at[n], sib) for n in range(nw)]
        for cp in cps:
            cp.start()
        for n in range(nw):
            cps[n].wait_send()
            _rcopy(o[n].at[1 - c], o[n].at[1 - c], ssem.at[n], rsem.at[n], sib).wait_recv()

    return pl.pallas_call(
        body, name=name, in_specs=[ANY] * nw, out_specs=[ANY] * nw,
        out_shape=[jax.ShapeDtypeStruct(f.shape, f.dtype) for f in fs],
        input_output_aliases={n: n for n in range(nw)},
        scratch_shapes=[pltpu.SemaphoreType.DMA((nw,)), pltpu.SemaphoreType.DMA((nw,))],
    )(*fs)


def _flatten_pad(parts, dtype):
    flat = jnp.concatenate([p.reshape(-1).astype(dtype) for p in parts])
    q = 512 * LANES
    n = -(-flat.shape[0] // q) * q
    return jnp.pad(flat, (0, n - flat.shape[0])).reshape(n // LANES, LANES)


def _lane_pad(n):
    return -(-n // LANES) * LANES


def _in_proj_layout(d):
    gk, gv, cw, pw = d // 2, d, d // 2, d // 2
    own = [('q', gk), ('k', gk), ('v', gv), ('og', gv), ('lrf', GLA_LR), ('lrb', GLA_LR), ('ga', cw), ('gb', cw),
           ('pu', pw), ('mg', 3 * d)]
    padded = [('mg', 3 * d), ('og', gv), ('v', gv), ('q', gk), ('k', gk), ('ga', cw), ('gb', cw), ('pu', pw),
              ('lrf', GLA_LR), ('lrb', GLA_LR), ('pad', d // 2 - 2 * GLA_LR)]
    return own, padded


def _row_pieces(src, lo, hi, wl, wlp):
    out = []
    for k in range(4):
        s0, s1 = max(lo, k * wl), min(hi, (k + 1) * wl)
        if s0 < s1:
            out.append(src[k * wlp + s0 - k * wl:k * wlp + s1 - k * wl])
    return out


def _w_in_t_to_proj(g, d, wl, wlp):
    own, padded = _in_proj_layout(d)
    at, start = {}, 0
    for n, wd in own:
        at[n] = (start, start + wd)
        start += wd
    parts = []
    for n, wd in padded:
        parts += [jnp.zeros((wd, g.shape[1]), g.dtype)] if n == 'pad' else _row_pieces(g, *at[n], wl, wlp)
    return jnp.concatenate(parts, axis=0)


def _proj_to_w_in_t(gp, d, wl, wlp):
    own, padded = _in_proj_layout(d)
    pat, start = {}, 0
    for n, wd in padded:
        pat[n] = start
        start += wd
    parts = []
    for k in range(4):
        start = 0
        for n, wd in own:
            s0, s1 = max(start, k * wl), min(start + wd, (k + 1) * wl)
            if s0 < s1:
                parts.append(gp[pat[n] + s0 - start:pat[n] + s1 - start])
            start += wd
        parts.append(jnp.zeros((wlp - wl, gp.shape[1]), gp.dtype))
    return jnp.concatenate(parts, axis=0)


def _silu_grad(z):
    s = jax.nn.sigmoid(z)
    return s + z * s * (1.0 - s)


def kernel(x, c, ctx, c_ctx, w_ada, b_ada, g_pre_mix, g_post_mix, g_pre_mlp, g_post_mlp, w_in, w_decay, b_decay, g_gla, w_gla_o, w_dw, b_dw, g_conv_ln, b_conv_ln, w_conv_o, w_pool_g, s_pool, w_pool_o, b_gate, w_out, w_mlp1, w_mlp2, loss_target, m_c_ctx, m_w_ada, m_b_ada, m_g_pre_mix, m_g_post_mix, m_g_pre_mlp, m_g_post_mlp, m_w_in, m_w_decay, m_b_decay, m_g_gla, m_w_gla_o, m_w_dw, m_b_dw, m_g_conv_ln, m_b_conv_ln, m_w_conv_o, m_w_pool_g, m_s_pool, m_w_pool_o, m_b_gate, m_w_out, m_w_mlp1, m_w_mlp2, v_c_ctx, v_w_ada, v_b_ada, v_g_pre_mix, v_g_post_mix, v_g_pre_mlp, v_g_post_mlp, v_w_in, v_w_decay, v_b_decay, v_g_gla, v_w_gla_o, v_w_dw, v_b_dw, v_g_conv_ln, v_b_conv_ln, v_w_conv_o, v_w_pool_g, v_s_pool, v_w_pool_o, v_b_gate, v_w_out, v_w_mlp1, v_w_mlp2):
    a = dict(locals())
    for n in ('w_in', 'm_w_in', 'v_w_in'):
        a[n] = jnp.swapaxes(a[n], 1, 2)
    big_axis = dict(BIG, w_in=1)
    depth = w_in.shape[0]
    d = x.shape[-1]
    seq, nctx_rows = x.shape[1], ctx.shape[1]
    dm = types.SimpleNamespace(
        D=d, SEQ=seq, CTX=nctx_rows, T=seq + nctx_rows, DK=d // 8, DV=d // 4, GK=d // 2, GC=d // 8,
        tm=_tile(nctx_rows, (256, 128, 64)), TB=_tile(nctx_rows, (256, 128, 64)))
    assert dm.SEQ % dm.tm == 0 and dm.SEQ % GRID_W == 0 and dm.CTX % GLA_CHUNK == 0
    tmw = min(dm.tm, 128)
    chip = 2 * lax.axis_index("x") + lax.axis_index("y")
    core = lax.axis_index("c")
    chip1 = chip.astype(jnp.int32).reshape(1)
    core1 = core.astype(jnp.int32).reshape(1)

    big_names, small_names = list(BIG), list(SMALL_SHARDED)
    nbig = len(big_names)
    kinds = ['col' if big_axis[n] == 2 else 'row' for n in big_names]
    wl = w_in.shape[2]
    wlp = _lane_pad(wl)

    def rows8(t):
        t = t.reshape(t.shape[0], -1, t.shape[-1])
        return jnp.pad(t, ((0, 0), (0, -t.shape[1] % 8), (0, 0)))

    def halves(t):
        return t.reshape(2, t.shape[0] // 2, t.shape[1])

    def layer_src(l):
        return [halves((jnp.pad(a[n][l], ((0, wlp - wl), (0, 0))) if n == 'w_in' else a[n][l]).astype(MM_DTYPE))
                for n in big_names]

    def whole(t):
        return t.reshape(-1, t.shape[-1])

    late = [big_names.index(n) for n in ('w_gla_o', 'w_conv_o', 'w_pool_o', 'w_out', 'w_mlp1', 'w_mlp2')]
    early = [k for k in range(nbig) if k not in late]
    src0, src1 = layer_src(0), layer_src(1)
    g0 = gather_halves([src0[k] for k in early] + [rows8(a[n]) for n in small_names],
                       [kinds[k] for k in early] + ['col'] * len(small_names), "gather_layer0")

    def start_gather(srcs, knds, after, name):
        plan = _gather_plan(knds, [t.shape[2] for t in srcs])
        lands = [lax.empty(_gathered_shape(t, k), t.dtype) for t, k in zip(srcs, knds)]
        return (plan,) + start_copies(srcs, lands, plan, 4 * len(srcs), after, name)

    ag0 = start_gather([src0[k] for k in late], [kinds[k] for k in late], g0[0], "gather_layer0_late_start")
    ag1 = start_gather(src1, kinds, ag0[-1], "gather_layer1_start")
    ag_token = ag1[-1]
    full = {n: [None, None] for n in big_names}
    for k, t in zip(early, g0):
        full[big_names[k]][0] = whole(t)
    for n, g in zip(small_names, g0[len(early):]):
        shp = a[n].shape
        full[n] = g[:, :math.prod(shp[1:-1])].reshape(shp[:-1] + (4 * shp[-1],))
    for n in SMALL:
        if n not in SMALL_SHARDED:
            full[n] = a[n]

    cvec = jnp.concatenate([c_ctx.reshape(1, d), c.reshape(1, d), jnp.zeros((6, d), F32)], axis=0)
    avec = (cvec * jax.nn.sigmoid(cvec) + ag_token[0, 0]).astype(MM_DTYPE)

    def row(v):
        return v.reshape(1, -1)

    X = jnp.concatenate([ctx[0], x[0]], axis=0)
    saved = []
    gk, gv = dm.GK, d
    lrblk = (7 * d + d // 2) // LANES
    for l in range(depth):
        if l == 1:
            got = wait_copies(ag1[1], ag1[2], ag1[3], ag1[4], ag1[0], X, "gather_layer1_wait")
            got = forward_halves(got, kinds, "gather_layer1_forward")
            for n, t in zip(big_names, got):
                full[n][1] = whole(t)
        s = types.SimpleNamespace()
        s.w_in_p = _w_in_t_to_proj(full['w_in'][l], d, wl, wlp)
        wd = full['w_decay'][l]
        wdp = jnp.zeros((LANES, 2 * gk), F32)
        wdp = wdp.at[:GLA_LR, :gk].set(wd[0]).at[GLA_LR:2 * GLA_LR, gk:].set(wd[1])
        s.wdp = wdp.astype(MM_DTYPE)
        s.wdp_wide = jnp.pad(s.wdp, ((0, d // 2 - LANES), (0, 0)))
        s.bd = full['b_decay'][l].reshape(1, 2 * gk)
        modraw = matmul(avec, full['w_ada'][l], 'nn', F32, f"mod_{l}") + full['b_ada'][l][None, :]
        s.mod = [modraw[0:2, j * d:(j + 1) * d].reshape(2, 1, d) for j in range(6)]
        s.x = X
        (s.h,) = rowwise(pre_fn, [X], s.mod[0:2], [row(g_pre_mix[l])], [(d, MM_DTYPE)], dm, f"pre_{l}")
        s.P = matmul(s.h, s.w_in_p, 'nt', MM_DTYPE, f"in_proj_{l}")
        P = s.P
        s.z = matmul((P, LANES, lrblk), s.wdp, 'nn', F32, f"decay_proj_{l}", tk=LANES)
        la_f, la_b = rowwise(decay_fn, [s.z], [], [s.bd], [(gk, F32), (gk, F32)], dm, f"decay_{l}")
        s.la = jnp.concatenate([la_f, la_b], axis=1)
        s.o_f, s.st_f = gla_fwd(P, s.la, False, dm, f"gla_fwd_f_{l}")
        s.o_b, s.st_b = gla_fwd(P, s.la, True, dm, f"gla_fwd_b_{l}")
        (s.gin,) = rowwise(glaout_fn, [s.o_f, s.o_b, (P, d, 3)], [], [row(g_gla[l])], [(gv, MM_DTYPE)], dm,
                           f"gla_out_{l}")
        if l == 0:
            got = wait_copies(ag0[1], ag0[2], ag0[3], ag0[4], ag0[0], s.gin, "gather_layer0_late_wait")
            got = forward_halves(got, [kinds[k] for k in late], "gather_layer0_late_forward")
            for k, t in zip(late, got):
                full[big_names[k]][0] = whole(t)
        s.ya = matmul(s.gin, full['w_gla_o'][l], 'nn', MM_DTYPE, f"gla_o_{l}")
        (s.u,) = rowwise(glu_fn, [(P, d, 6)], [], [], [(d // 2, F32)], dm, f"glu_{l}")
        s.yconv = conv_fwd(s.u, full['w_dw'][l], dm, f"conv_{l}")
        (s.cin,) = rowwise(convpost_fn, [s.yconv], [], [row(b_dw[l]), row(g_conv_ln[l]), row(b_conv_ln[l])],
                           [(d // 2, MM_DTYPE)], dm, f"conv_post_{l}")
        s.yb = matmul(s.cin, full['w_conv_o'][l], 'nn', MM_DTYPE, f"conv_o_{l}")
        s.pm = pool_mix((P, d // 2, 14), False, dm, f"pool_mix_{l}")
        s.pc = group_mm(s.pm, w_pool_g[l], 'nn', F32, f"pool_g_{l}")
        (s.pin,) = rowwise(poolpost_fn, [s.pc], [], [row(s_pool[l])], [(d // 2, MM_DTYPE)], dm, f"pool_post_{l}")
        s.yc = matmul(s.pin, full['w_pool_o'][l], 'nn', MM_DTYPE, f"pool_o_{l}")
        s.bg = [row(full['b_gate'][l][j]) for j in range(3)]
        (s.mixed,) = rowwise(merge_fn, [s.ya, s.yb, s.yc, (P, 3 * d, 0)], [], s.bg, [(d, MM_DTYPE)], dm,
                             f"merge_{l}", tm=tmw)
        s.y = matmul(s.mixed, full['w_out'][l], 'nn', MM_DTYPE, f"out_proj_{l}")
        s.x1, s.h2 = rowwise(mid_fn, [X, s.y], s.mod[2:5], [row(g_post_mix[l]), row(g_pre_mlp[l])],
                             [(d, F32), (d, MM_DTYPE)], dm, f"mid_{l}")
        s.act = matmul(s.h2, full['w_mlp1'][l], 'nn', MM_DTYPE, f"mlp1_{l}", epi=relu2_epi)
        s.y2 = matmul(s.act, full['w_mlp2'][l], 'nn', MM_DTYPE, f"mlp2_{l}")
        (X,) = rowwise(post_fn, [s.x1, s.y2], s.mod[5:6], [row(g_post_mlp[l])], [(d, F32)], dm, f"post_{l}")
        saved.append(s)

    dX, lossv = loss_head(X, loss_target[0], dm, "loss_head")
    loss = lax.psum(lossv[0, 0], ("x", "y", "c"))

    grads = {n: [None] * depth for n in WEIGHTS if n != 'c_ctx' and n not in BIG}
    gbig = {n: [None] * depth for n in BIG}
    rs_token = None

    def start_scatter(idx, layer, after, name):
        gs = [gbig[big_names[k]][layer] for k in idx]
        wd = [t.shape[1] // 4 if kinds[k] == 'col' else t.shape[0] // 4 for t, k in zip(gs, idx)]
        plan = _scatter_plan([big_axis[big_names[k]] - 1 for k in idx], wd)
        lands = [lax.empty((3, t.shape[0], w) if kinds[k] == 'col' else (3, w, t.shape[1]), t.dtype)
                 for t, w, k in zip(gs, wd, idx)]
        return (plan,) + start_copies(gs, lands, plan, 3 * len(gs), after, name)

    g_cctx = jnp.zeros((d,), F32)
    for l in reversed(range(depth)):
        s = saved[l]
        P = s.P
        dmod = [None] * 6
        gpm = row(g_post_mlp[l]) if rs_token is None else row(g_post_mlp[l]) + rs_token[0, 0]
        (dx1, dy2), (dmod[5],), (dg,) = rowwise_vjp(post_fn, [s.x1, s.y2], s.mod[5:6], [gpm], [dX],
                                                     dm, f"post_bwd_{l}", narrow=(1,))
        grads['g_post_mlp'][l] = dg[0]
        du1 = matmul(dy2, full['w_mlp2'][l], 'nt', MM_DTYPE, f"mlp2_dx_{l}", epi=relu2_bwd_epi, extras=[s.act])
        gbig['w_mlp2'][l] = matmul(s.act, dy2, 'tn', MM_DTYPE, f"mlp2_dw_{l}")
        dh2 = matmul(du1, full['w_mlp1'][l], 'nt', MM_DTYPE, f"mlp1_dx_{l}")
        gbig['w_mlp1'][l] = matmul(s.h2, du1, 'tn', MM_DTYPE, f"mlp1_dw_{l}")
        gpx = row(g_post_mix[l])
        (dxa, dy), dmod[2:5], (dg1, dg2) = rowwise_vjp(
            mid_fn, [s.x, s.y], s.mod[2:5], [gpx, row(g_pre_mlp[l])], [dx1, dh2], dm, f"mid_bwd_{l}", narrow=(1,))
        grads['g_post_mix'][l], grads['g_pre_mlp'][l] = dg1[0], dg2[0]
        dmixed = matmul(dy, full['w_out'][l], 'nt', MM_DTYPE, f"out_proj_dx_{l}")
        gbig['w_out'][l] = matmul(s.mixed, dy, 'tn', MM_DTYPE, f"out_proj_dw_{l}")
        (dya, dyb, dyc, dP), _, dbg = rowwise_vjp(merge_fn, [s.ya, s.yb, s.yc, (P, 3 * d, 0)], [], s.bg, [dmixed],
                                                  dm, f"merge_bwd_{l}", tm=tmw, narrow=(0, 1, 2),
                                                  into=(3, None, P.shape))
        grads['b_gate'][l] = jnp.concatenate(dbg, axis=0)
        dgin = matmul(dya, full['w_gla_o'][l], 'nt', MM_DTYPE, f"gla_o_dx_{l}")
        gbig['w_gla_o'][l] = matmul(s.gin, dya, 'tn', MM_DTYPE, f"gla_o_dw_{l}")
        dcin = matmul(dyb, full['w_conv_o'][l], 'nt', MM_DTYPE, f"conv_o_dx_{l}")
        gbig['w_conv_o'][l] = matmul(s.cin, dyb, 'tn', MM_DTYPE, f"conv_o_dw_{l}")
        dpin = matmul(dyc, full['w_pool_o'][l], 'nt', MM_DTYPE, f"pool_o_dx_{l}")
        gbig['w_pool_o'][l] = matmul(s.pin, dyc, 'tn', MM_DTYPE, f"pool_o_dw_{l}")
        sp = row(s_pool[l])
        if l == 0:
            rs0 = start_scatter(late, 0, dpin, "grad_layer0_late_start")
            sp = sp + rs0[-1][0, 0]
        (dpc,), _, (dsp,) = rowwise_vjp(poolpost_fn, [s.pc], [], [sp], [dpin], dm, f"pool_post_bwd_{l}")
        grads['s_pool'][l] = dsp[0]
        grads['w_pool_g'][l] = group_mm(s.pm, w_pool_g[l], 'tn', F32, f"pool_g_dw_{l}", b=dpc)
        dpm = group_mm(dpc, w_pool_g[l], 'nt', F32, f"pool_g_dx_{l}")
        dP = pool_mix(dpm, True, dm, f"pool_mix_bwd_{l}", into=(dP, 14))
        (dyconv,), _, (dbdw, dgln, dbln) = rowwise_vjp(
            convpost_fn, [s.yconv], [], [row(b_dw[l]), row(g_conv_ln[l]), row(b_conv_ln[l])], [dcin], dm,
            f"conv_post_bwd_{l}")
        grads['b_dw'][l], grads['g_conv_ln'][l], grads['b_conv_ln'][l] = dbdw[0], dgln[0], dbln[0]
        du, grads['w_dw'][l] = conv_bwd(s.u, full['w_dw'][l], dyconv, dm, f"conv_bwd_{l}")
        (dP,), _, _ = rowwise_vjp(glu_fn, [(P, d, 6)], [], [], [du], dm, f"glu_bwd_{l}", into=(0, dP, P.shape))
        (do, _, dP), _, (dgg,) = rowwise_vjp(glaout_fn, [s.o_f, s.o_b, (P, d, 3)], [], [row(g_gla[l])], [dgin], dm,
                                             f"gla_out_bwd_{l}", want=[True, False, True], into=(2, dP, P.shape))
        grads['g_gla'][l] = dgg[0]
        dqf, dkf, dvf, dlaf = gla_bwd(P, s.la, do, s.st_f, False, dm, f"gla_bwd_f_{l}")
        dP, dlab = gla_bwd(P, s.la, do, s.st_b, True, dm, f"gla_bwd_b_{l}", prev=(dqf, dkf, dvf), into=dP)
        (dz,), _, (dbd,) = rowwise_vjp(decay_fn, [s.z], [], [s.bd], [dlaf, dlab], dm, f"decay_bwd_{l}", narrow=(0,))
        grads['b_decay'][l] = dbd.reshape(2, gk)
        dwdp = matmul((P, LANES, lrblk), dz, 'tn', F32, f"decay_proj_dw_{l}", tm=LANES)
        grads['w_decay'][l] = jnp.stack([dwdp[:GLA_LR, :gk], dwdp[GLA_LR:2 * GLA_LR, gk:]])
        dP = matmul(dz, s.wdp_wide, 'nt', MM_DTYPE, f"decay_proj_dx_{l}", into=(dP, 15))
        dh = matmul(dP, s.w_in_p, 'nn', MM_DTYPE, f"in_proj_dx_{l}")
        gbig['w_in'][l] = _proj_to_w_in_t(matmul(dP, s.h, 'tn', MM_DTYPE, f"in_proj_dw_{l}"), d, wl, wlp)
        (dX,), dmod[0:2], (dg,) = rowwise_vjp(pre_fn, [s.x], s.mod[0:2], [row(g_pre_mix[l])], [dh], dm,
                                               f"pre_bwd_{l}", adds={0: dxa})
        grads['g_pre_mix'][l] = dg[0]
        dmodflat = jnp.concatenate([jnp.concatenate([m_.reshape(2, d) for m_ in dmod], axis=1),
                                    jnp.zeros((6, 6 * d), F32)], axis=0)
        grads['b_ada'][l] = dmodflat[0] + dmodflat[1]
        gbig['w_ada'][l] = matmul(avec, dmodflat, 'tn', MM_DTYPE, f"ada_dw_{l}")
        dav = matmul(dmodflat, full['w_ada'][l], 'nt', F32, f"ada_dx_{l}")
        g_cctx = g_cctx + dav[0] * _silu_grad(c_ctx)
        if l == 1:
            rs1 = start_scatter(list(range(nbig)), 1, dav, "grad_layer1_start")
            rs_token = rs1[-1]

    grad_x = dX[dm.CTX:][None]
    gfull = {n: jnp.stack(v) for n, v in grads.items()}
    gfull['c_ctx'] = g_cctx
    where = jnp.concatenate([chip1, core1])

    def halves_view(t, k):
        return t.reshape(2, t.shape[0] // 2, t.shape[1]) if k == 'col' else t.reshape(4, 2, t.shape[0] // 8, t.shape[1])
    enames = [big_names[k] for k in early]
    ekinds = [kinds[k] for k in early]
    v0 = [halves_view(gbig[n][0], k) for n, k in zip(enames, ekinds)]
    r1 = pair_swap_halves(v0, ekinds, "grad_pair_swap")
    hs = [pair_add(v.reshape((-1,) + v.shape[-2:]), r.reshape((-1,) + r.shape[-2:]), core1, f"grad_pair_add_{n}")
          for n, v, r in zip(enames, v0, r1)]
    hx = [h.reshape(h.shape[1:]) if k == 'col' else h for h, k in zip(hs, ekinds)]
    r2 = chip_exchange(hx, ekinds, "grad_chip_exchange")
    fs = [chip_add(h.reshape(-1, h.shape[-1]), r, big_axis[n] - 1, where, f"grad_chip_add_{n}")
          for n, h, r in zip(enames, hs, r2)]
    red0 = dict(zip(enames, [[t.reshape(-1, t.shape[-1])] for t in pair_join_layers(fs, "grad_pair_join")]))

    got0 = wait_copies(rs0[1], rs0[2], rs0[3], rs0[4], rs0[0], dX, "grad_layer0_late_wait")
    got1 = wait_copies(rs1[1], rs1[2], rs1[3], rs1[4], rs1[0], dX, "grad_layer1_wait")
    sa = [chip_add(g, r, big_axis[big_names[k]] - 1, where, f"grad_layer0_add_{big_names[k]}", slab=False)
          for k, g, r in zip(late, rs0[3], got0)]
    sa += [chip_add(g, r, big_axis[n] - 1, where, f"grad_layer1_add_{n}", slab=False)
           for n, g, r in zip(big_names, rs1[3], got1)]
    sb = pair_swap(sa, "grad_late_pair_swap")
    for j, k in enumerate(late):
        red0[big_names[k]] = [sa[j], sb[j]]
    red1 = {n: [sa[len(late) + k], sb[len(late) + k]] for k, n in enumerate(big_names)}

    sflat = _flatten_pad([gfull[n].astype(F32) for n in SMALL], F32)
    sv = sflat.reshape(2, sflat.shape[0] // 2, LANES)
    (sr,) = pair_swap_halves([sv], ['col'], "small_grad_pair_swap")
    sh = pair_add(sv, sr[None], core1, "small_grad_pair_add")[0]
    sq = quad_sum(sh, chip_broadcast(sh, "small_grad_chip_exchange"), core1, "small_grad_chip_sum")
    (ssum,) = pair_join_layers([sq], "small_grad_pair_join")
    ssum = ssum.reshape(-1)

    out_g, out_d, out_m, out_v = {}, {}, {}, {}
    for k, n in enumerate(big_names):
        out_g[n], out_d[n], out_m[n], out_v[n] = adamw_layers(a[n], a['m_' + n], a['v_' + n], red0[n], red1[n],
                                                              f"adamw_{n}")
    start = 0
    sg = {}
    for n in SMALL:
        cnt = gfull[n].size
        g = ssum[start:start + cnt].reshape(gfull[n].shape)
        start += cnt
        if n in SMALL_SHARDED:
            ax = SMALL_SHARDED[n]
            wdt = a[n].shape[ax]
            g = lax.dynamic_slice_in_dim(g, chip * wdt, wdt, axis=ax)
        sg[n] = g
    pk = lambda dct, pre: _flatten_pad([dct[pre + n] for n in SMALL], F32)
    gs = _flatten_pad([sg[n] for n in SMALL], F32)
    dl, mn, vn = adamw(pk(a, ''), gs, pk(a, 'm_'), pk(a, 'v_'), "adamw_small")
    dl, mn, vn = dl.reshape(-1), mn.reshape(-1), vn.reshape(-1)
    start = 0
    for n in SMALL:
        cnt, shp = a[n].size, a[n].shape
        out_g[n] = sg[n]
        out_d[n], out_m[n], out_v[n] = (t[start:start + cnt].reshape(shp) for t in (dl, mn, vn))
        start += cnt

    for dct in (out_g, out_d, out_m, out_v):
        dct['w_in'] = jnp.swapaxes(dct['w_in'], 1, 2)
    return (loss, grad_x, *[out_g[n] for n in WEIGHTS], *[out_d[n] for n in WEIGHTS],
            *[out_m[n] for n in WEIGHTS], *[out_v[n] for n in WEIGHTS])
```

```python
import functools
import math
import types

import jax
import jax.numpy as jnp
from jax import lax
from jax.experimental import pallas as pl
from jax.experimental.pallas import tpu as pltpu

F32 = jnp.float32
MM_DTYPE = jnp.bfloat16
VMEM_LIMIT_V7X = 56 * 1024 * 1024
LANES = 128
EPS = 1e-6

N_HEADS = 4
GLA_CHUNK = 64
GLA_TAU = 16.0
GLA_LR = 16
GRID_W = 64
POOL_WINDOWS = (2, 4, 8, 16)

ADAM_LR = 0.001
ADAM_B1 = 0.9
ADAM_B2 = 0.999
ADAM_EPS = 1e-08
ADAM_WD = 0.01
ADAM_STEP = 10

NN = (((1,), (0,)), ((), ()))
NT = (((1,), (1,)), ((), ()))
TN = (((0,), (0,)), ((), ()))

WEIGHTS = ['c_ctx', 'w_ada', 'b_ada', 'g_pre_mix', 'g_post_mix', 'g_pre_mlp', 'g_post_mlp', 'w_in', 'w_decay',
           'b_decay', 'g_gla', 'w_gla_o', 'w_dw', 'b_dw', 'g_conv_ln', 'b_conv_ln', 'w_conv_o', 'w_pool_g',
           's_pool', 'w_pool_o', 'b_gate', 'w_out', 'w_mlp1', 'w_mlp2']
BIG = {'w_ada': 2, 'w_in': 2, 'w_gla_o': 1, 'w_conv_o': 2, 'w_pool_o': 2, 'w_out': 1, 'w_mlp1': 2, 'w_mlp2': 1}
SMALL_SHARDED = {'w_decay': 3, 'b_decay': 2, 'w_dw': 2, 'b_gate': 2}
SMALL = [n for n in WEIGHTS if n not in BIG]


def _tile(n, prefs):
    for t in prefs:
        if n % t == 0:
            return t
    return n


def _cparams(sem=None, **kw):
    return pltpu.CompilerParams(dimension_semantics=sem, vmem_limit_bytes=VMEM_LIMIT_V7X, **kw)


def _dot(a, b, dims=NN):
    return lax.dot_general(a.astype(MM_DTYPE), b.astype(MM_DTYPE), dims, preferred_element_type=F32)


def matmul(a, b, mode, out_dtype, name, tm=None, tn=None, tk=None, epi=None, extras=(), into=None):
    a, aw, ablk = a if isinstance(a, tuple) else (a, a.shape[1], 0)
    if mode == 'nn':
        M, K, N = a.shape[0], aw, b.shape[1]
    elif mode == 'nt':
        M, K, N = a.shape[0], aw, b.shape[0]
    else:
        K, M, N = a.shape[0], aw, b.shape[1]
    big = (1088, 1024, 640, 544, 512, 320, 256, 128, 64, 32, 16, 8)
    if mode == 'tn':
        tm = tm or _tile(M, (1024, 512, 256, 128))
        tn = tn or _tile(N, (1024, 512, 256, 128))
        tk = tk or _tile(K, big)
    else:
        tm = tm or _tile(M, big)
        tn = tn or _tile(N, (1024, 512, 256, 128))
        tk = tk or _tile(K, (1024, 512, 256, 128))
    if aw != a.shape[1]:
        assert (mode == 'tn' and tm == aw) or (mode != 'tn' and tk == aw)
    nk = K // tk
    ne = len(extras)
    dims = {'nn': NN, 'nt': NT, 'tn': TN}[mode]

    def body(a_ref, b_ref, *rest):
        e_refs, o_ref = rest[:ne], rest[ne + (into is not None)]

        def finish(acc):
            if epi is not None:
                acc = epi(acc, *[e[...] for e in e_refs])
            o_ref[...] = acc.astype(o_ref.dtype)

        p = _dot(a_ref[...], b_ref[...], dims)
        if nk == 1:
            finish(p)
            return
        acc = rest[-1]
        k = pl.program_id(2)

        @pl.when(k == 0)
        def _():
            acc[...] = p

        @pl.when(k > 0)
        def _():
            acc[...] += p

        @pl.when(k == nk - 1)
        def _():
            finish(acc[...])

    if mode == 'nn':
        a_spec = pl.BlockSpec((tm, tk), lambda i, j, k: (i, k + ablk))
        b_spec = pl.BlockSpec((tk, tn), lambda i, j, k: (k, j))
    elif mode == 'nt':
        a_spec = pl.BlockSpec((tm, tk), lambda i, j, k: (i, k + ablk))
        b_spec = pl.BlockSpec((tn, tk), lambda i, j, k: (j, k))
    else:
        a_spec = pl.BlockSpec((tk, tm), lambda i, j, k: (k, i + ablk))
        b_spec = pl.BlockSpec((tk, tn), lambda i, j, k: (k, j))
    tile = pl.BlockSpec((tm, tn), lambda i, j, k: (i, j))
    if into is None:
        out_spec, out_shape, more, extra, aliases = tile, jax.ShapeDtypeStruct((M, N), out_dtype), [], [], {}
    else:
        buf, oblk = into
        out_spec = pl.BlockSpec((tm, tn), lambda i, j, k: (i, oblk * (N // tn) + j))
        out_shape = jax.ShapeDtypeStruct(buf.shape, buf.dtype)
        more, extra, aliases = [pl.BlockSpec(memory_space=pl.ANY)], [buf], {2 + ne: 0}
    return pl.pallas_call(
        body, name=name, grid=(M // tm, N // tn, nk),
        in_specs=[a_spec, b_spec] + [tile] * ne + more, out_specs=out_spec,
        out_shape=out_shape, input_output_aliases=aliases,
        scratch_shapes=[] if nk == 1 else [pltpu.VMEM((tm, tn), F32)],
        compiler_params=_cparams(("parallel", "parallel", "arbitrary")),
    )(a, b, *extras, *extra)


def group_mm(a, w, mode, out_dtype, name, b=None):
    T = a.shape[0]
    G, gc, _ = w.shape
    col = pl.BlockSpec((T, gc), lambda g: (0, g))
    wsp = pl.BlockSpec((1, gc, gc), lambda g: (g, 0, 0))
    if mode == 'tn':
        def body(a_ref, b_ref, o_ref):
            o_ref[0] = _dot(a_ref[...], b_ref[...], TN).astype(o_ref.dtype)
        return pl.pallas_call(body, name=name, grid=(G,), in_specs=[col, col], out_specs=wsp,
                              out_shape=jax.ShapeDtypeStruct((G, gc, gc), out_dtype),
                              compiler_params=_cparams(("parallel",)))(a, b)
    dims = NN if mode == 'nn' else NT

    def body(a_ref, w_ref, o_ref):
        o_ref[...] = _dot(a_ref[...], w_ref[0], dims).astype(o_ref.dtype)
    return pl.pallas_call(body, name=name, grid=(G,), in_specs=[col, wsp], out_specs=col,
                          out_shape=jax.ShapeDtypeStruct((T, G * gc), out_dtype),
                          compiler_params=_cparams(("parallel",)))(a, w)


def _rowspec(r):
    return r if isinstance(r, tuple) else (r, r.shape[1], 0)


def _row_specs(rows, segs, consts, tm, nctx):
    specs = [pl.BlockSpec((tm, w), lambda i, b=b: (i, b)) for _, w, b in rows]
    specs += [pl.BlockSpec((1,) + s.shape[1:], lambda i, n=s.ndim: (jnp.where(i >= nctx, 1, 0),) + (0,) * (n - 1))
              for s in segs]
    specs += [pl.BlockSpec(c.shape, lambda i, n=c.ndim: (0,) * n) for c in consts]
    return specs


def rowwise(fn, rows, segs, consts, outs, dm, name, tm=None):
    tm = tm or dm.tm
    nctx = dm.CTX // tm
    rows = [_rowspec(r) for r in rows]
    nr, ns, nc = len(rows), len(segs), len(consts)

    def body(*refs):
        rin = [r[...] for r in refs[:nr]]
        sin = [s[0] for s in refs[nr:nr + ns]]
        cin = [c[...] for c in refs[nr + ns:nr + ns + nc]]
        res = fn(*rin, *sin, *cin)
        for o_ref, v in zip(refs[nr + ns + nc:], res):
            o_ref[...] = v.astype(o_ref.dtype)

    res = pl.pallas_call(
        body, name=name, grid=(dm.T // tm,),
        in_specs=_row_specs(rows, segs, consts, tm, nctx),
        out_specs=[pl.BlockSpec((tm, w), lambda i: (i, 0)) for w, _ in outs],
        out_shape=[jax.ShapeDtypeStruct((dm.T, w), dt) for w, dt in outs],
        compiler_params=_cparams(("parallel",)),
    )(*[r[0] for r in rows], *segs, *consts)
    return res


def rowwise_vjp(fn, rows, segs, consts, cots, dm, name, tm=None, want=None, adds=None, narrow=(), into=None):
    tm = tm or dm.tm
    nctx = dm.CTX // tm
    rows = [_rowspec(r) for r in rows]
    cots = [_rowspec(r) for r in cots]
    adds = adds or {}
    nr, ns, nc, nct = len(rows), len(segs), len(consts), len(cots)
    want = want or [True] * nr
    widx = [k for k in range(nr) if want[k]]
    akeys = sorted(adds)

    def body(*refs):
        i = pl.program_id(0)
        rin = [r[...] for r in refs[:nr]]
        sin = [s[0] for s in refs[nr:nr + ns]]
        cin = [c[...] for c in refs[nr + ns:nr + ns + nc]]
        p = nr + ns + nc
        cot_refs = refs[p:p + nct]
        add_refs = dict(zip(akeys, refs[p + nct:p + nct + len(akeys)]))
        p = p + nct + len(akeys) + (1 if (into is not None and into[1] is not None) else 0)
        rg_refs = refs[p:p + len(widx)]
        sg_refs = refs[p + len(widx):p + len(widx) + ns]
        cg_refs = refs[p + len(widx) + ns:]
        res, vjp = jax.vjp(fn, *rin, *sin, *cin)
        g = vjp(tuple(cr[...].astype(o.dtype) for cr, o in zip(cot_refs, res)))
        for o_ref, k in zip(rg_refs, widx):
            v = g[k].astype(F32)
            if k in add_refs:
                v = v + add_refs[k][...]
            o_ref[...] = v.astype(o_ref.dtype)
        first_seg = jnp.logical_or(i == 0, i == nctx)
        for o_ref, v in zip(sg_refs, g[nr:nr + ns]):
            @pl.when(first_seg)
            def _(o_ref=o_ref, v=v):
                o_ref[0] = v.astype(F32)

            @pl.when(jnp.logical_not(first_seg))
            def _(o_ref=o_ref, v=v):
                o_ref[0] += v.astype(F32)
        for o_ref, v in zip(cg_refs, g[nr + ns:]):
            @pl.when(i == 0)
            def _(o_ref=o_ref, v=v):
                o_ref[...] = v.astype(F32)

            @pl.when(i > 0)
            def _(o_ref=o_ref, v=v):
                o_ref[...] += v.astype(F32)

    in_specs = _row_specs(rows, segs, consts, tm, nctx)
    in_specs += [pl.BlockSpec((tm, w), lambda i, b=b: (i, b)) for _, w, b in cots]
    in_specs += [pl.BlockSpec((tm, adds[k].shape[1]), lambda i: (i, 0)) for k in akeys]
    out_specs = [pl.BlockSpec((tm, rows[k][1]), lambda i: (i, 0)) for k in widx]
    out_shape = [jax.ShapeDtypeStruct((dm.T, rows[k][1]), MM_DTYPE if k in narrow else rows[k][0].dtype)
                 for k in widx]
    extra, aliases = [], {}
    if into is not None:
        ik, ibuf, ishape = into
        out_specs[widx.index(ik)] = pl.BlockSpec((tm, rows[ik][1]), lambda i, b=rows[ik][2]: (i, b))
        out_shape[widx.index(ik)] = jax.ShapeDtypeStruct(ishape, MM_DTYPE)
        if ibuf is not None:
            aliases = {len(in_specs): widx.index(ik)}
            in_specs = in_specs + [pl.BlockSpec(memory_space=pl.ANY)]
            extra = [ibuf]
    out_specs += [pl.BlockSpec((1,) + s.shape[1:], lambda i, n=s.ndim: (jnp.where(i >= nctx, 1, 0),) + (0,) * (n - 1))
                  for s in segs]
    out_shape += [jax.ShapeDtypeStruct(s.shape, F32) for s in segs]
    out_specs += [pl.BlockSpec(c.shape, lambda i, n=c.ndim: (0,) * n) for c in consts]
    out_shape += [jax.ShapeDtypeStruct(c.shape, F32) for c in consts]
    res = pl.pallas_call(
        body, name=name, grid=(dm.T // tm,), in_specs=in_specs, out_specs=out_specs, out_shape=out_shape,
        input_output_aliases=aliases, compiler_params=_cparams(("arbitrary",)),
    )(*[r[0] for r in rows], *segs, *consts, *[r[0] for r in cots], *[adds[k] for k in akeys], *extra)
    rg = [None] * nr
    for k, v in zip(widx, res[:len(widx)]):
        rg[k] = v
    return rg, list(res[len(widx):len(widx) + ns]), list(res[len(widx) + ns:])


def _rms(x, g):
    return x * lax.rsqrt(jnp.mean(x * x, axis=-1, keepdims=True) + EPS) * g


def _sigmoid(x):
    return jax.nn.sigmoid(x)


def pre_fn(x, shift, scale, g):
    return ((_rms(x, g) * (1.0 + scale) + shift).astype(MM_DTYPE),)


def mid_fn(x, y, gate, shift, scale, g_post, g_pre):
    x1 = x + gate * _rms(y.astype(F32), g_post)
    return x1, (_rms(x1, g_pre) * (1.0 + scale) + shift).astype(MM_DTYPE)


def post_fn(x1, y2, gate, g):
    return (x1 + gate * _rms(y2.astype(F32), g),)


def relu2_epi(acc):
    r = jnp.maximum(acc, 0.0)
    return r * r


def relu2_bwd_epi(dact, act):
    return dact * (2.0 * jnp.sqrt(act.astype(F32)))


def decay_fn(z, bd):
    zz = z.astype(F32) + bd
    ls = jnp.minimum(zz, 0.0) - jnp.log(1.0 + jnp.exp(jnp.minimum(zz, -zz)))
    la = ls / GLA_TAU
    gk = la.shape[1] // 2
    return la[:, :gk], la[:, gk:]


def glu_fn(ab):
    h = ab.shape[1] // 2
    return (ab[:, :h].astype(F32) * _sigmoid(ab[:, h:].astype(F32)),)


def glaout_fn(o_f, o_b, og, g):
    o = o_f + o_b
    dv = o.shape[1] // N_HEADS
    hs = []
    for h in range(N_HEADS):
        oh = o[:, h * dv:(h + 1) * dv]
        hs.append(oh * lax.rsqrt(jnp.mean(oh * oh, axis=-1, keepdims=True) + EPS) * g[:, h * dv:(h + 1) * dv])
    og = og.astype(F32)
    return ((jnp.concatenate(hs, axis=1) * (og * _sigmoid(og))).astype(MM_DTYPE),)


def convpost_fn(y, b_dw, g, b):
    y = y + b_dw
    mu = jnp.mean(y, axis=-1, keepdims=True)
    xc = y - mu
    yn = xc * lax.rsqrt(jnp.mean(xc * xc, axis=-1, keepdims=True) + EPS) * g + b
    return ((yn * _sigmoid(yn)).astype(MM_DTYPE),)


def poolpost_fn(pc, s):
    return ((pc.astype(F32) * s).astype(MM_DTYPE),)


def merge_fn(ya, yb, yc, mg, bg0, bg1, bg2):
    d = ya.shape[1]
    mg = mg.astype(F32)
    mixed = (_sigmoid(mg[:, :d] + bg0) * ya.astype(F32) + _sigmoid(mg[:, d:2 * d] + bg1) * yb.astype(F32)
             + _sigmoid(mg[:, 2 * d:] + bg2) * yc.astype(F32))
    return (mixed.astype(MM_DTYPE),)


def _split_dot(lmat, x, dims):
    hi = x.astype(MM_DTYPE)
    lo = x - hi.astype(F32)
    return _dot(lmat, hi, dims) + _dot(lmat, lo, dims)


def _gla_block_order(dm, rev):
    nctx, nb = dm.CTX // dm.TB, dm.T // dm.TB

    def blk(i):
        if not rev:
            return i
        return jnp.where(i < nctx, nctx - 1 - i, nb - 1 - (i - nctx))
    return blk, nb


def _gla_tri(rev):
    c = GLA_CHUNK
    t = lax.broadcasted_iota(jnp.int32, (c, c), 0)
    s = lax.broadcasted_iota(jnp.int32, (c, c), 1)
    return (s >= t) if rev else (s <= t)


def _gla_chunk_terms(q, k, la, tri, scale):
    lmat = tri.astype(MM_DTYPE)
    b = _split_dot(lmat, la, NN)
    bend = jnp.sum(la, axis=0, keepdims=True)
    eb = jnp.exp(b)
    enb = jnp.exp(-b)
    ee = jnp.exp(bend - b)
    qi = q * scale * eb
    ki = k * enb
    kend = k * ee
    att = jnp.where(tri, _dot(qi, ki, NT), 0.0)
    return lmat, bend, eb, enb, ee, qi, ki, kend, att


def gla_fwd(P, la, rev, dm, name):
    c, tb, h_, dk, dv, d = GLA_CHUNK, dm.TB, N_HEADS, dm.DK, dm.DV, dm.D
    cpb = tb // c
    blk, nb = _gla_block_order(dm, rev)
    gk, gv = h_ * dk, h_ * dv
    qb, kb, vb, lb = (5 * d) // gk, (5 * d + d // 2) // gk, (4 * d) // gv, (1 if rev else 0)
    scale = dk ** -0.5
    order = list(range(cpb))[::-1] if rev else list(range(cpb))

    def body(q_ref, k_ref, v_ref, la_ref, o_ref, s_ref, st):
        @pl.when(pl.program_id(0) == 0)
        def _():
            st[...] = jnp.zeros_like(st)
        tri = _gla_tri(rev)
        for n, ci in enumerate(order):
            r = pl.ds(ci * c, c)
            for hh in range(h_):
                ck, cv = pl.ds(hh * dk, dk), pl.ds(hh * dv, dv)
                q = q_ref[r, ck].astype(F32)
                k = k_ref[r, ck].astype(F32)
                v = v_ref[r, cv]
                _, bend, _, _, _, qi, _, kend, att = _gla_chunk_terms(q, k, la_ref[r, ck], tri, scale)
                s_in = st[hh]
                o_ref[r, cv] = _dot(att, v) + _dot(qi, s_in, NT)
                s_ref[n, hh] = s_in
                st[hh] = jnp.exp(bend) * s_in + _dot(v, kend, TN)

    return pl.pallas_call(
        body, name=name, grid=(nb,),
        in_specs=[pl.BlockSpec((tb, gk), lambda i: (blk(i), qb)),
                  pl.BlockSpec((tb, gk), lambda i: (blk(i), kb)),
                  pl.BlockSpec((tb, gv), lambda i: (blk(i), vb)),
                  pl.BlockSpec((tb, gk), lambda i: (blk(i), lb))],
        out_specs=[pl.BlockSpec((tb, gv), lambda i: (blk(i), 0)),
                   pl.BlockSpec((cpb, h_, dv, dk), lambda i: (i, 0, 0, 0))],
        out_shape=[jax.ShapeDtypeStruct((dm.T, gv), F32),
                   jax.ShapeDtypeStruct((dm.T // c, h_, dv, dk), F32)],
        scratch_shapes=[pltpu.VMEM((h_, dv, dk), F32)],
        compiler_params=_cparams(("arbitrary",)),
    )(P, P, P, la)


def gla_bwd(P, la, do, states, rev, dm, name, prev=None, into=None):
    c, tb, h_, dk, dv, d = GLA_CHUNK, dm.TB, N_HEADS, dm.DK, dm.DV, dm.D
    cpb = tb // c
    blk, nb = _gla_block_order(dm, rev)
    gk, gv = h_ * dk, h_ * dv
    qb, kb, vb, lb = (5 * d) // gk, (5 * d + d // 2) // gk, (4 * d) // gv, (1 if rev else 0)
    scale = dk ** -0.5
    order = list(range(cpb))[::-1] if rev else list(range(cpb))

    fused = prev is not None

    def body(q_ref, k_ref, v_ref, la_ref, do_ref, s_ref, *rest):
        if fused:
            pq_ref, pk_ref, pv_ref, _, w_ref, dla_ref, dst = rest
        else:
            dq_ref, dk_ref, dv_ref, dla_ref, dst = rest

        def put(kind, r, cols, val):
            if not fused:
                {'q': dq_ref, 'k': dk_ref, 'v': dv_ref}[kind][r, cols] = val
                return
            p_ref, off = {'q': (pq_ref, gv), 'k': (pk_ref, gv + gk), 'v': (pv_ref, 0)}[kind]
            w_ref[r, pl.ds(off + cols.start, cols.size)] = (val + p_ref[r, cols]).astype(w_ref.dtype)

        @pl.when(pl.program_id(0) == 0)
        def _():
            dst[...] = jnp.zeros_like(dst)
        tri = _gla_tri(rev)
        for n in range(cpb - 1, -1, -1):
            r = pl.ds(order[n] * c, c)
            for hh in range(h_):
                ck, cv = pl.ds(hh * dk, dk), pl.ds(hh * dv, dv)
                q = q_ref[r, ck].astype(F32)
                k = k_ref[r, ck].astype(F32)
                v = v_ref[r, cv]
                lmat, bend, eb, enb, ee, qi, ki, kend, att = _gla_chunk_terms(q, k, la_ref[r, ck], tri, scale)
                s_in = s_ref[n, hh]
                ds_out = dst[hh]
                dob = do_ref[r, cv]
                datt = jnp.where(tri, _dot(dob, v, NT), 0.0)
                dqi = _dot(datt, ki) + _dot(dob, s_in)
                dki = _dot(datt, qi, TN)
                put('v', r, cv, _dot(att, dob, TN) + _dot(kend, ds_out, NT))
                dkend = _dot(v, ds_out)
                gam = jnp.exp(bend)
                dgam = jnp.sum(ds_out * s_in, axis=0, keepdims=True)
                dst[hh] = gam * ds_out + _dot(dob, qi, TN)
                put('q', r, ck, dqi * (scale * eb))
                put('k', r, ck, dki * enb + dkend * ee)
                db = dqi * qi - dki * ki - dkend * kend
                dbend = jnp.sum(dkend * kend, axis=0, keepdims=True) + dgam * gam
                dla_ref[r, ck] = _split_dot(lmat, db, TN) + dbend

    def bi(j):
        return blk(nb - 1 - j)

    in_specs = [
        pl.BlockSpec((tb, gk), lambda j: (bi(j), qb)),
        pl.BlockSpec((tb, gk), lambda j: (bi(j), kb)),
        pl.BlockSpec((tb, gv), lambda j: (bi(j), vb)),
        pl.BlockSpec((tb, gk), lambda j: (bi(j), lb)),
        pl.BlockSpec((tb, gv), lambda j: (bi(j), 0)),
        pl.BlockSpec((cpb, h_, dv, dk), lambda j: (nb - 1 - j, 0, 0, 0)),
    ]
    small = pl.BlockSpec((tb, gk), lambda j: (bi(j), 0))
    wide = pl.BlockSpec((tb, gv), lambda j: (bi(j), 0))
    if not fused:
        return pl.pallas_call(
            body, name=name, grid=(nb,), in_specs=in_specs, out_specs=[small, small, wide, small],
            out_shape=[jax.ShapeDtypeStruct((dm.T, gk), F32), jax.ShapeDtypeStruct((dm.T, gk), F32),
                       jax.ShapeDtypeStruct((dm.T, gv), F32), jax.ShapeDtypeStruct((dm.T, gk), F32)],
            scratch_shapes=[pltpu.VMEM((h_, dv, dk), F32)],
            compiler_params=_cparams(("arbitrary",)),
        )(P, P, P, la, do, states)
    return pl.pallas_call(
        body, name=name, grid=(nb,),
        in_specs=in_specs + [small, small, wide, pl.BlockSpec(memory_space=pl.ANY)],
        out_specs=[pl.BlockSpec((tb, 2 * gv), lambda j: (bi(j), vb // 2)), small],
        out_shape=[jax.ShapeDtypeStruct(into.shape, into.dtype), jax.ShapeDtypeStruct((dm.T, gk), F32)],
        input_output_aliases={9: 0},
        scratch_shapes=[pltpu.VMEM((h_, dv, dk), F32)],
        compiler_params=_cparams(("arbitrary",)),
    )(P, P, P, la, do, states, *prev, into)


def _pos(n, period):
    t = lax.broadcasted_iota(jnp.int32, (n, 1), 0)
    if period & (period - 1) == 0:
        return jnp.bitwise_and(t, period - 1)
    return lax.rem(t, period)


def _conv_segments(dm):
    return [(0, dm.CTX, dm.CTX), (dm.CTX, dm.SEQ, GRID_W)]


def conv_fwd(u, w, dm, name):
    kw, cw = w.shape
    segs = _conv_segments(dm)

    def body(u_ref, w_ref, y_ref):
        for r0, n, per in segs:
            useg = u_ref[r0:r0 + n, :]
            p = _pos(n, per)
            acc = jnp.zeros_like(useg)
            for kk in range(kw):
                d = kk - kw // 2
                sh = useg if d == 0 else pltpu.roll(useg, (-d) % n, 0)
                ok = jnp.logical_and(p + d >= 0, p + d < per)
                acc = acc + jnp.where(ok, sh, 0.0) * w_ref[kk:kk + 1, :]
            y_ref[r0:r0 + n, :] = acc

    return pl.pallas_call(
        body, name=name, grid=(cw // LANES,),
        in_specs=[pl.BlockSpec((dm.T, LANES), lambda j: (0, j)), pl.BlockSpec((kw, LANES), lambda j: (0, j))],
        out_specs=pl.BlockSpec((dm.T, LANES), lambda j: (0, j)),
        out_shape=jax.ShapeDtypeStruct((dm.T, cw), F32),
        compiler_params=_cparams(("parallel",)),
    )(u, w)


def conv_bwd(u, w, dy, dm, name):
    kw, cw = w.shape
    segs = _conv_segments(dm)

    def body(u_ref, w_ref, dy_ref, du_ref, dw_ref):
        dws = [jnp.zeros((1, LANES), F32)] * kw
        for r0, n, per in segs:
            useg = u_ref[r0:r0 + n, :]
            dyseg = dy_ref[r0:r0 + n, :]
            p = _pos(n, per)
            acc = jnp.zeros_like(useg)
            for kk in range(kw):
                d = kk - kw // 2
                shu = useg if d == 0 else pltpu.roll(useg, (-d) % n, 0)
                okf = jnp.logical_and(p + d >= 0, p + d < per)
                dws[kk] = dws[kk] + jnp.sum(jnp.where(okf, shu, 0.0) * dyseg, axis=0, keepdims=True)
                shd = dyseg if d == 0 else pltpu.roll(dyseg, d % n, 0)
                okb = jnp.logical_and(p - d >= 0, p - d < per)
                acc = acc + jnp.where(okb, shd, 0.0) * w_ref[kk:kk + 1, :]
            du_ref[r0:r0 + n, :] = acc
        for kk in range(kw):
            dw_ref[kk:kk + 1, :] = dws[kk]

    return pl.pallas_call(
        body, name=name, grid=(cw // LANES,),
        in_specs=[pl.BlockSpec((dm.T, LANES), lambda j: (0, j)), pl.BlockSpec((kw, LANES), lambda j: (0, j)),
                  pl.BlockSpec((dm.T, LANES), lambda j: (0, j))],
        out_specs=[pl.BlockSpec((dm.T, LANES), lambda j: (0, j)), pl.BlockSpec((kw, LANES), lambda j: (0, j))],
        out_shape=[jax.ShapeDtypeStruct((dm.T, cw), F32), jax.ShapeDtypeStruct((kw, cw), F32)],
        compiler_params=_cparams(("parallel",)),
    )(u, w, dy)


def pool_mix(u, transpose, dm, name, into=None):
    u, uw, ublk = _rowspec(u)
    gc = dm.GC
    ng = len(POOL_WINDOWS)
    rows = dm.SEQ // GRID_W
    segs = [(0, dm.CTX, 1, dm.CTX), (dm.CTX, dm.SEQ, GRID_W, rows)]

    def one_group(u_ref, o_ref, win):
        left = win // 2
        right = win - 1 - left
        for r0, n, stride, length in segs:
            useg = u_ref[r0:r0 + n, :].astype(F32)
            t = lax.broadcasted_iota(jnp.int32, (n, 1), 0)
            p = t if stride == 1 else jnp.right_shift(t, stride.bit_length() - 1)
            cnt = (jnp.minimum(p + right + 1, length) - jnp.maximum(p - left, 0)).astype(F32)
            src = useg / cnt if transpose else useg
            acc = jnp.zeros_like(useg)
            for d in range(-left, right + 1):
                dd = -d if transpose else d
                sh = src if d == 0 else pltpu.roll(src, (-dd * stride) % n, 0)
                ok = jnp.logical_and(p + dd >= 0, p + dd < length)
                acc = acc + jnp.where(ok, sh, 0.0)
            o_ref[r0:r0 + n, :] = ((acc - useg) if transpose else (acc / cnt - useg)).astype(o_ref.dtype)

    def body(u_ref, *rest):
        o_ref = rest[-1]
        g = pl.program_id(0)
        for gi, win in enumerate(POOL_WINDOWS):
            @pl.when(g == gi)
            def _(win=win):
                one_group(u_ref, o_ref, win)

    base = ublk * (uw // gc)
    if into is None:
        obase, out_shape, more, extra, aliases = 0, jax.ShapeDtypeStruct((dm.T, ng * gc), F32), [], [], {}
    else:
        buf, oblk = into
        obase, out_shape = oblk * ng, jax.ShapeDtypeStruct(buf.shape, buf.dtype)
        more, extra, aliases = [pl.BlockSpec(memory_space=pl.ANY)], [buf], {1: 0}
    return pl.pallas_call(
        body, name=name, grid=(ng,),
        in_specs=[pl.BlockSpec((dm.T, gc), lambda g: (0, base + g))] + more,
        out_specs=pl.BlockSpec((dm.T, gc), lambda g: (0, obase + g)),
        out_shape=out_shape, input_output_aliases=aliases,
        compiler_params=_cparams(("parallel",)),
    )(u, *extra)


def loss_head(x2, target, dm, name):
    tm, d = dm.tm, dm.D
    nctx = dm.CTX // tm

    def body(x_ref, t_ref, dx_ref, l_ref):
        i = pl.program_id(0)

        @pl.when(i == 0)
        def _():
            l_ref[...] = jnp.zeros_like(l_ref)

        @pl.when(i < nctx)
        def _():
            dx_ref[...] = jnp.zeros_like(dx_ref)

        @pl.when(i >= nctx)
        def _():
            e = x_ref[...] - t_ref[...]
            dx_ref[...] = e / d
            l_ref[...] += jnp.full(l_ref.shape, 0.5 * jnp.sum(jnp.mean(e * e, axis=-1)), F32)

    return pl.pallas_call(
        body, name=name, grid=(dm.T // tm,),
        in_specs=[pl.BlockSpec((tm, d), lambda i: (i, 0)),
                  pl.BlockSpec((tm, d), lambda i: (jnp.maximum(i - nctx, 0), 0))],
        out_specs=[pl.BlockSpec((tm, d), lambda i: (i, 0)), pl.BlockSpec((8, LANES), lambda i: (0, 0))],
        out_shape=[jax.ShapeDtypeStruct((dm.T, d), F32), jax.ShapeDtypeStruct((8, LANES), F32)],
        compiler_params=_cparams(("arbitrary",)),
    )(x2, target)


def adamw(w, g, m, v, name):
    r, c = w.shape
    tr = _tile(r, tuple(t for t in (512, 256, 128, 64, 32, 16, 8) if t * c * 4 <= (1 << 20)) or (8,))

    def body(w_ref, g_ref, m_ref, v_ref, d_ref, mo_ref, vo_ref):
        gg = g_ref[...]
        mm = ADAM_B1 * m_ref[...] + (1.0 - ADAM_B1) * gg
        vv = ADAM_B2 * v_ref[...] + (1.0 - ADAM_B2) * (gg * gg)
        m_hat = mm / (1.0 - ADAM_B1 ** ADAM_STEP)
        v_hat = vv / (1.0 - ADAM_B2 ** ADAM_STEP)
        d_ref[...] = -ADAM_LR * (m_hat / (jnp.sqrt(v_hat) + ADAM_EPS) + ADAM_WD * w_ref[...])
        mo_ref[...] = mm
        vo_ref[...] = vv

    spec = pl.BlockSpec((tr, c), lambda i: (i, 0))
    return pl.pallas_call(
        body, name=name, grid=(r // tr,), in_specs=[spec] * 4, out_specs=[spec] * 3,
        out_shape=[jax.ShapeDtypeStruct((r, c), F32)] * 3,
        compiler_params=_cparams(("parallel",)),
    )(w, g, m, v)


def pair_add(g, r1, cidx, name):
    ng, r_, n_ = r1.shape
    tr = _tile(r_, tuple(t for t in (1024, 512, 256, 128, 64, 32, 16) if t * n_ * 4 <= (2 << 20)))

    def body(s_ref, g_ref, r_ref, o_ref):
        o_ref[...] = (g_ref[...].astype(F32) + r_ref[...].astype(F32)).astype(o_ref.dtype)

    return pl.pallas_call(
        body, name=name,
        grid_spec=pltpu.PrefetchScalarGridSpec(
            num_scalar_prefetch=1, grid=(ng, r_ // tr),
            in_specs=[pl.BlockSpec((None, tr, n_), lambda k, i, s: (2 * k + s[0], i, 0)),
                      pl.BlockSpec((None, tr, n_), lambda k, i, s: (k, i, 0))],
            out_specs=pl.BlockSpec((None, tr, n_), lambda k, i, s: (k, i, 0))),
        out_shape=jax.ShapeDtypeStruct((ng, r_, n_), g.dtype),
        compiler_params=_cparams(("parallel", "parallel")),
    )(cidx, g, r1)


def chip_add(h, r2, axis, where, name, slab=True):
    _, kl, nl = r2.shape
    tr = _tile(kl, tuple(t for t in (1024, 512, 256, 128, 64, 32, 16) if t * nl * 4 <= (1 << 20)))
    nrb = kl // tr

    def body(s_ref, h_ref, r_ref, o_ref):
        acc = h_ref[...].astype(F32)
        for k in range(r2.shape[0]):
            acc = acc + r_ref[k].astype(F32)
        o_ref[...] = acc

    h_map = (lambda i, s: (s[0] * nrb + i, 0)) if axis == 0 else (lambda i, s: (i, s[0]))
    if slab:
        out_spec = pl.BlockSpec((None, tr, nl), lambda i, s: (s[1], i, 0))
        out_shape = jax.ShapeDtypeStruct((2, kl, nl), F32)
    else:
        out_spec = pl.BlockSpec((tr, nl), lambda i, s: (i, 0))
        out_shape = jax.ShapeDtypeStruct((kl, nl), F32)
    return pl.pallas_call(
        body, name=name,
        grid_spec=pltpu.PrefetchScalarGridSpec(
            num_scalar_prefetch=1, grid=(nrb,),
            in_specs=[pl.BlockSpec((tr, nl), h_map),
                      pl.BlockSpec((r2.shape[0], tr, nl), lambda i, s: (0, i, 0))],
            out_specs=out_spec),
        out_shape=out_shape,
        compiler_params=_cparams(("parallel",)),
    )(where, h, r2)


def adamw_layers(w, m, v, terms0, terms1, name):
    _, a_, b_ = w.shape
    tr = _tile(a_, tuple(t for t in (512, 256, 128, 64, 32) if t * b_ * 4 <= (1 << 20)))
    by_cols = tr == a_ and a_ * b_ * 4 > (1 << 20)
    blk = (a_, LANES) if by_cols else (tr, b_)
    steps = b_ // LANES if by_cols else a_ // tr
    at = (lambda i: (0, i)) if by_cols else (lambda i: (i, 0))
    n0 = len(terms0)

    def update(g, w_ref, m_ref, v_ref, g_ref, d_ref, mo_ref, vo_ref):
        mm = ADAM_B1 * m_ref[...] + (1.0 - ADAM_B1) * g
        vv = ADAM_B2 * v_ref[...] + (1.0 - ADAM_B2) * (g * g)
        m_hat = mm / (1.0 - ADAM_B1 ** ADAM_STEP)
        v_hat = vv / (1.0 - ADAM_B2 ** ADAM_STEP)
        g_ref[...] = g
        d_ref[...] = -ADAM_LR * (m_hat / (jnp.sqrt(v_hat) + ADAM_EPS) + ADAM_WD * w_ref[...])
        mo_ref[...] = mm
        vo_ref[...] = vv

    def total(refs):
        g = refs[0][...]
        for r in refs[1:]:
            g = g + r[...]
        return g

    def body(w_ref, m_ref, v_ref, *rest):
        t_refs, outs = rest[:-4], rest[-4:]
        layer = pl.program_id(0)

        @pl.when(layer == 0)
        def _():
            update(total(t_refs[:n0]), w_ref, m_ref, v_ref, *outs)

        @pl.when(layer == 1)
        def _():
            update(total(t_refs[n0:]), w_ref, m_ref, v_ref, *outs)

    stacked = pl.BlockSpec((None,) + blk, lambda l, i: (l,) + at(i))
    return pl.pallas_call(
        body, name=name, grid=(2, steps),
        in_specs=[stacked] * 3 + [pl.BlockSpec(blk, lambda l, i: at(i * (1 - l)))] * n0
        + [pl.BlockSpec(blk, lambda l, i: at(i * l))] * len(terms1),
        out_specs=[stacked] * 4, out_shape=[jax.ShapeDtypeStruct(w.shape, F32)] * 4,
        compiler_params=_cparams(("arbitrary", "arbitrary")),
    )(w, m, v, *terms0, *terms1)


MESH = pl.DeviceIdType.MESH
ANY = pl.BlockSpec(memory_space=pl.ANY)
HBM = pl.BlockSpec(memory_space=pltpu.HBM)
SEM = pl.BlockSpec(memory_space=pltpu.SEMAPHORE)
EFFECT = pltpu.SideEffectType.DATAFLOW_SIDE_EFFECTING


def _place():
    return lax.axis_index("x"), lax.axis_index("y"), lax.axis_index("c")


def _peers(x, y):
    return [(1 - x, y), (x, 1 - y), (1 - x, 1 - y)]


def _rcopy(src, dst, ssem, rsem, dev):
    return pltpu.make_async_remote_copy(src_ref=src, dst_ref=dst, send_sem=ssem, recv_sem=rsem,
                                        device_id=dev, device_id_type=MESH)


def _gathered_shape(src, kind):
    h, a_, b_ = src.shape
    return (h, a_, 4 * b_) if kind == 'col' else (4, h, a_, b_)


def _win(ref, kind, ch, width):
    return ref.at[:, :, pl.ds(ch * width, width)] if kind == 'col' else ref.at[ch]


def _rect(ref, kind, half, ch, width):
    return ref.at[half, :, pl.ds(ch * width, width)] if kind == 'col' else ref.at[ch, half]


def gather_halves(srcs, kinds, name):
    nw = len(srcs)
    widths = [s.shape[2] for s in srcs]

    def body(*refs):
        src, out = refs[:nw], refs[nw:2 * nw]
        ssem, rsem, osend, orecv = refs[2 * nw:]
        x, y, c = _place()
        chip = 2 * x + y
        sib = (x, y, 1 - c)
        peers = _peers(x, y)
        pidx = [2 * px + py for px, py in peers]

        def rect(n, half, ch):
            return _rect(out[n], kinds[n], half, ch, widths[n])

        mine = [_rcopy(src[n], _win(out[n], kinds[n], chip, widths[n]), osend.at[n], orecv.at[n], sib)
                for n in range(nw)]
        first = [[_rcopy(src[n].at[c], rect(n, c, chip), ssem.at[6 * n + k], rsem.at[6 * n + k], (px, py, c))
                  for k, (px, py) in enumerate(peers)] for n in range(nw)]
        for n in range(nw):
            for cp in first[n]:
                cp.start()
        for cp in mine:
            cp.start()
        passed = [[_rcopy(rect(n, c, pidx[k]), rect(n, c, pidx[k]), ssem.at[6 * n + 3 + k], rsem.at[6 * n + 3 + k], sib)
                   for k in range(3)] for n in range(nw)]
        for n in range(nw):
            for k, (px, py) in enumerate(peers):
                _rcopy(rect(n, c, pidx[k]), rect(n, c, pidx[k]), ssem.at[6 * n + k], rsem.at[6 * n + k],
                       (px, py, c)).wait_recv()
                passed[n][k].start()
        for n in range(nw):
            for k in range(3):
                _rcopy(rect(n, 1 - c, pidx[k]), rect(n, 1 - c, pidx[k]), ssem.at[6 * n + 3 + k],
                       rsem.at[6 * n + 3 + k], sib).wait_recv()
        for n in range(nw):
            for cp in first[n] + passed[n]:
                cp.wait_send()
        for cp in mine:
            cp.wait()

    return pl.pallas_call(
        body, name=name, in_specs=[ANY] * nw, out_specs=[ANY] * nw,
        out_shape=[jax.ShapeDtypeStruct(_gathered_shape(s, k), s.dtype) for s, k in zip(srcs, kinds)],
        scratch_shapes=[pltpu.SemaphoreType.DMA((6 * nw,)), pltpu.SemaphoreType.DMA((6 * nw,)),
                        pltpu.SemaphoreType.DMA((nw,)), pltpu.SemaphoreType.DMA((nw,))],
    )(*srcs)


def _gather_plan(kinds, widths):
    def plan(src, land, x, y, c):
        chip = 2 * x + y
        out = []
        for n in range(len(src)):
            for px, py in _peers(x, y):
                out.append((src[n].at[c], _rect(land[n], kinds[n], c, chip, widths[n]), (px, py, c),
                            _rect(land[n], kinds[n], c, 2 * px + py, widths[n])))
            mine = _win(land[n], kinds[n], chip, widths[n])
            out.append((src[n], mine, (x, y, 1 - c), mine))
        return out
    return plan


def forward_halves(lands, kinds, name):
    nw = len(lands)
    widths = [t.shape[-1] // 4 if k == 'col' else t.shape[-1] for t, k in zip(lands, kinds)]

    def body(*refs):
        o = refs[nw:2 * nw]
        ssem, rsem = refs[2 * nw:]
        x, y, c = _place()
        sib = (x, y, 1 - c)
        pidx = [2 * px + py for px, py in _peers(x, y)]
        cps = [_rcopy(_rect(o[n], kinds[n], c, pidx[j], widths[n]), _rect(o[n], kinds[n], c, pidx[j], widths[n]),
                      ssem.at[3 * n + j], rsem.at[3 * n + j], sib) for n in range(nw) for j in range(3)]
        for cp in cps:
            cp.start()
        for n in range(nw):
            for j in range(3):
                cps[3 * n + j].wait_send()
                _rcopy(_rect(o[n], kinds[n], 1 - c, pidx[j], widths[n]), _rect(o[n], kinds[n], 1 - c, pidx[j], widths[n]),
                       ssem.at[3 * n + j], rsem.at[3 * n + j], sib).wait_recv()

    return pl.pallas_call(
        body, name=name, in_specs=[ANY] * nw, out_specs=[ANY] * nw,
        out_shape=[jax.ShapeDtypeStruct(t.shape, t.dtype) for t in lands],
        input_output_aliases={n: n for n in range(nw)},
        scratch_shapes=[pltpu.SemaphoreType.DMA((3 * nw,)), pltpu.SemaphoreType.DMA((3 * nw,))],
    )(*lands)


def _scatter_plan(axes, widths):
    def plan(src, land, x, y, c):
        out = []
        for n in range(len(src)):
            for k, (px, py) in enumerate(_peers(x, y)):
                ch = 2 * px + py
                view = (src[n].at[:, pl.ds(ch * widths[n], widths[n])] if axes[n] == 1
                        else src[n].at[pl.ds(ch * widths[n], widths[n]), :])
                out.append((view, land[n].at[k], (px, py, c), land[n].at[k]))
        return out
    return plan


def _exchange_plan(kinds):
    def plan(src, land, x, y, c):
        out = []
        for n in range(len(src)):
            w = land[n].shape[2]
            for j, (px, py) in enumerate(_peers(x, y)):
                ch = 2 * px + py
                view = src[n].at[:, pl.ds(ch * w, w)] if kinds[n] == 'col' else src[n].at[ch]
                out.append((view, land[n].at[j], (px, py, c), land[n].at[j]))
        return out
    return plan


def start_copies(srcs, lands, plan, ncopies, after, name):
    ns, nl = len(srcs), len(lands)

    def body(*refs):
        src, land = refs[:ns], refs[ns:ns + nl]
        ssem, rsem = refs[ns + nl + 1], refs[ns + nl + 2]
        token = refs[-1]
        x, y, c = _place()
        for k, (sv, dv, dev, _) in enumerate(plan(src, land, x, y, c)):
            _rcopy(sv, dv, ssem.at[k], rsem.at[k], dev).start()
        token[...] = jnp.zeros_like(token)

    hbm = lambda t: pltpu.HBM(t.shape, t.dtype)
    res = pl.pallas_call(
        body, name=name,
        out_shape=(pltpu.SemaphoreType.DMA((ncopies,)), pltpu.SemaphoreType.DMA((ncopies,)),
                   *[hbm(t) for t in srcs], *[hbm(t) for t in lands], jax.ShapeDtypeStruct((8, LANES), F32)),
        in_specs=[HBM] * (ns + nl) + [ANY],
        out_specs=(SEM, SEM, *[HBM] * (ns + nl), pl.BlockSpec(memory_space=pltpu.VMEM)),
        input_output_aliases={k: 2 + k for k in range(ns + nl)},
        compiler_params=pltpu.CompilerParams(has_side_effects=EFFECT),
    )(*[pltpu.with_memory_space_constraint(t, pltpu.HBM) for t in list(srcs) + list(lands)], after)
    return res[0], res[1], list(res[2:2 + ns]), list(res[2 + ns:2 + ns + nl]), res[-1]


def wait_copies(ssem, rsem, srcs, lands, plan, after, name):
    ns, nl = len(srcs), len(lands)

    def body(*refs):
        src, land = refs[:ns], refs[ns:ns + nl]
        ss, rs = refs[ns + nl], refs[ns + nl + 1]
        x, y, c = _place()
        for k, (sv, dv, dev, mine) in enumerate(plan(src, land, x, y, c)):
            cp = _rcopy(sv, mine, ss.at[k], rs.at[k], dev)
            cp.wait_send()
            cp.wait_recv()

    hbm = lambda t: pltpu.HBM(t.shape, t.dtype)
    res = pl.pallas_call(
        body, name=name,
        out_shape=(*[hbm(t) for t in srcs], *[hbm(t) for t in lands]),
        in_specs=[HBM] * (ns + nl) + [SEM, SEM, ANY], out_specs=tuple([HBM] * (ns + nl)),
        input_output_aliases={k: k for k in range(ns + nl)},
        compiler_params=pltpu.CompilerParams(has_side_effects=EFFECT),
    )(*srcs, *lands, ssem, rsem, after)
    return list(res[ns:])


def pair_swap_halves(gs, kinds, name):
    nw = len(gs)

    def other(ref, kind, half):
        return ref.at[half] if kind == 'col' else ref.at[:, half]

    def body(*refs):
        g, o = refs[:nw], refs[nw:2 * nw]
        ssem, rsem = refs[2 * nw:]
        x, y, c = _place()
        cps = [_rcopy(other(g[n], kinds[n], 1 - c), o[n], ssem.at[n], rsem.at[n], (x, y, 1 - c)) for n in range(nw)]
        for cp in cps:
            cp.start()
        for cp in cps:
            cp.wait()

    return pl.pallas_call(
        body, name=name, in_specs=[ANY] * nw, out_specs=[ANY] * nw,
        out_shape=[jax.ShapeDtypeStruct(g.shape[1:] if k == 'col' else (g.shape[0],) + g.shape[2:], g.dtype)
                   for g, k in zip(gs, kinds)],
        scratch_shapes=[pltpu.SemaphoreType.DMA((nw,)), pltpu.SemaphoreType.DMA((nw,))],
    )(*gs)


def pair_swap(fs, name):
    nw = len(fs)

    def body(*refs):
        f, o = refs[:nw], refs[nw:2 * nw]
        ssem, rsem = refs[2 * nw:]
        x, y, c = _place()
        cps = [_rcopy(f[n], o[n], ssem.at[n], rsem.at[n], (x, y, 1 - c)) for n in range(nw)]
        for cp in cps:
            cp.start()
        for cp in cps:
            cp.wait()

    return pl.pallas_call(
        body, name=name, in_specs=[ANY] * nw, out_specs=[ANY] * nw,
        out_shape=[jax.ShapeDtypeStruct(f.shape, f.dtype) for f in fs],
        scratch_shapes=[pltpu.SemaphoreType.DMA((nw,)), pltpu.SemaphoreType.DMA((nw,))],
    )(*fs)


def chip_broadcast(h, name):
    def body(h_ref, o_ref, ssem, rsem):
        x, y, c = _place()
        cps = [_rcopy(h_ref, o_ref.at[j], ssem.at[j], rsem.at[j], (px, py, c)) for j, (px, py) in enumerate(_peers(x, y))]
        for cp in cps:
            cp.start()
        for cp in cps:
            cp.wait()

    return pl.pallas_call(
        body, name=name, in_specs=[ANY], out_specs=ANY,
        out_shape=jax.ShapeDtypeStruct((3,) + h.shape, h.dtype),
        scratch_shapes=[pltpu.SemaphoreType.DMA((3,)), pltpu.SemaphoreType.DMA((3,))],
    )(h)


def quad_sum(h, r, cidx, name):
    r_, c_ = h.shape
    tr = _tile(r_, (512, 256, 128, 64, 32, 16, 8))

    def body(s_ref, h_ref, r_ref, o_ref):
        o_ref[...] = (h_ref[...] + r_ref[2]) + (r_ref[0] + r_ref[1])

    return pl.pallas_call(
        body, name=name,
        grid_spec=pltpu.PrefetchScalarGridSpec(
            num_scalar_prefetch=1, grid=(r_ // tr,),
            in_specs=[pl.BlockSpec((tr, c_), lambda i, s: (i, 0)), pl.BlockSpec((3, tr, c_), lambda i, s: (0, i, 0))],
            out_specs=pl.BlockSpec((None, tr, c_), lambda i, s: (s[0], i, 0))),
        out_shape=jax.ShapeDtypeStruct((2, r_, c_), F32),
        compiler_params=_cparams(("parallel",)),
    )(cidx, h, r)


def pair_join_layers(fs, name):
    nw = len(fs)

    def body(*refs):
        o = refs[nw:2 * nw]
        ssem, rsem = refs[2 * nw:]
        x, y, c = _place()
        sib = (x, y, 1 - c)
        cps = [_rcopy(o[n].at[c], o[n].at[c], ssem.at[n], rsem.at[n], sib) for n in range(nw)]
        for cp in cps:
            cp.start()
        for n in range(nw):
            cps[n].wait_send()
            _rcopy(o[n].at[1 - c], o[n].at[1 - c], ssem.at[n], rsem.at[n], sib).wait_recv()

    return pl.pallas_call(
        body, name=name, in_specs=[ANY] * nw, out_specs=[ANY] * nw,
        out_shape=[jax.ShapeDtypeStruct(f.shape, f.dtype) for f in fs],
        input_output_aliases={n: n for n in range(nw)},
        scratch_shapes=[pltpu.SemaphoreType.DMA((nw,)), pltpu.SemaphoreType.DMA((nw,))],
    )(*fs)


def _flatten_pad(parts, dtype):
    flat = jnp.concatenate([p.reshape(-1).astype(dtype) for p in parts])
    q = 512 * LANES
    n = -(-flat.shape[0] // q) * q
    return jnp.pad(flat, (0, n - flat.shape[0])).reshape(n // LANES, LANES)


def _lane_pad(n):
    return -(-n // LANES) * LANES


def _in_proj_layout(d):
    gk, gv, cw, pw = d // 2, d, d // 2, d // 2
    own = [('q', gk), ('k', gk), ('v', gv), ('og', gv), ('lrf', GLA_LR), ('lrb', GLA_LR), ('ga', cw), ('gb', cw),
           ('pu', pw), ('mg', 3 * d)]
    padded = [('mg', 3 * d), ('og', gv), ('v', gv), ('q', gk), ('k', gk), ('ga', cw), ('gb', cw), ('pu', pw),
              ('lrf', GLA_LR), ('lrb', GLA_LR), ('pad', d // 2 - 2 * GLA_LR)]
    return own, padded


def _row_pieces(src, lo, hi, wl, wlp):
    out = []
    for k in range(4):
        s0, s1 = max(lo, k * wl), min(hi, (k + 1) * wl)
        if s0 < s1:
            out.append(src[k * wlp + s0 - k * wl:k * wlp + s1 - k * wl])
    return out


def _w_in_t_to_proj(g, d, wl, wlp):
    own, padded = _in_proj_layout(d)
    at, start = {}, 0
    for n, wd in own:
        at[n] = (start, start + wd)
        start += wd
    parts = []
    for n, wd in padded:
        parts += [jnp.zeros((wd, g.shape[1]), g.dtype)] if n == 'pad' else _row_pieces(g, *at[n], wl, wlp)
    return jnp.concatenate(parts, axis=0)


def _proj_to_w_in_t(gp, d, wl, wlp):
    own, padded = _in_proj_layout(d)
    pat, start = {}, 0
    for n, wd in padded:
        pat[n] = start
        start += wd
    parts = []
    for k in range(4):
        start = 0
        for n, wd in own:
            s0, s1 = max(start, k * wl), min(start + wd, (k + 1) * wl)
            if s0 < s1:
                parts.append(gp[pat[n] + s0 - start:pat[n] + s1 - start])
            start += wd
        parts.append(jnp.zeros((wlp - wl, gp.shape[1]), gp.dtype))
    return jnp.concatenate(parts, axis=0)


def _silu_grad(z):
    s = jax.nn.sigmoid(z)
    return s + z * s * (1.0 - s)


def kernel(x, c, ctx, c_ctx, w_ada, b_ada, g_pre_mix, g_post_mix, g_pre_mlp, g_post_mlp, w_in, w_decay, b_decay, g_gla, w_gla_o, w_dw, b_dw, g_conv_ln, b_conv_ln, w_conv_o, w_pool_g, s_pool, w_pool_o, b_gate, w_out, w_mlp1, w_mlp2, loss_target, m_c_ctx, m_w_ada, m_b_ada, m_g_pre_mix, m_g_post_mix, m_g_pre_mlp, m_g_post_mlp, m_w_in, m_w_decay, m_b_decay, m_g_gla, m_w_gla_o, m_w_dw, m_b_dw, m_g_conv_ln, m_b_conv_ln, m_w_conv_o, m_w_pool_g, m_s_pool, m_w_pool_o, m_b_gate, m_w_out, m_w_mlp1, m_w_mlp2, v_c_ctx, v_w_ada, v_b_ada, v_g_pre_mix, v_g_post_mix, v_g_pre_mlp, v_g_post_mlp, v_w_in, v_w_decay, v_b_decay, v_g_gla, v_w_gla_o, v_w_dw, v_b_dw, v_g_conv_ln, v_b_conv_ln, v_w_conv_o, v_w_pool_g, v_s_pool, v_w_pool_o, v_b_gate, v_w_out, v_w_mlp1, v_w_mlp2):
    a = dict(locals())
    for n in ('w_in', 'm_w_in', 'v_w_in'):
        a[n] = jnp.swapaxes(a[n], 1, 2)
    big_axis = dict(BIG, w_in=1)
    depth = w_in.shape[0]
    d = x.shape[-1]
    seq, nctx_rows = x.shape[1], ctx.shape[1]
    dm = types.SimpleNamespace(
        D=d, SEQ=seq, CTX=nctx_rows, T=seq + nctx_rows, DK=d // 8, DV=d // 4, GK=d // 2, GC=d // 8,
        tm=_tile(nctx_rows, (256, 128, 64)), TB=_tile(nctx_rows, (256, 128, 64)))
    assert dm.SEQ % dm.tm == 0 and dm.SEQ % GRID_W == 0 and dm.CTX % GLA_CHUNK == 0
    tmw = min(dm.tm, 128)
    chip = 2 * lax.axis_index("x") + lax.axis_index("y")
    core = lax.axis_index("c")
    chip1 = chip.astype(jnp.int32).reshape(1)
    core1 = core.astype(jnp.int32).reshape(1)

    big_names, small_names = list(BIG), list(SMALL_SHARDED)
    nbig = len(big_names)
    kinds = ['col' if big_axis[n] == 2 else 'row' for n in big_names]
    wl = w_in.shape[2]
    wlp = _lane_pad(wl)

    def rows8(t):
        t = t.reshape(t.shape[0], -1, t.shape[-1])
        return jnp.pad(t, ((0, 0), (0, -t.shape[1] % 8), (0, 0)))

    def halves(t):
        return t.reshape(2, t.shape[0] // 2, t.shape[1])

    def layer_src(l):
        return [halves((jnp.pad(a[n][l], ((0, wlp - wl), (0, 0))) if n == 'w_in' else a[n][l]).astype(MM_DTYPE))
                for n in big_names]

    def whole(t):
        return t.reshape(-1, t.shape[-1])

    late = [big_names.index(n) for n in ('w_gla_o', 'w_conv_o', 'w_pool_o', 'w_out', 'w_mlp1', 'w_mlp2')]
    early = [k for k in range(nbig) if k not in late]
    src0, src1 = layer_src(0), layer_src(1)
    g0 = gather_halves([src0[k] for k in early] + [rows8(a[n]) for n in small_names],
                       [kinds[k] for k in early] + ['col'] * len(small_names), "gather_layer0")

    def start_gather(srcs, knds, after, name):
        plan = _gather_plan(knds, [t.shape[2] for t in srcs])
        lands = [lax.empty(_gathered_shape(t, k), t.dtype) for t, k in zip(srcs, knds)]
        return (plan,) + start_copies(srcs, lands, plan, 4 * len(srcs), after, name)

    ag0 = start_gather([src0[k] for k in late], [kinds[k] for k in late], g0[0], "gather_layer0_late_start")
    ag1 = start_gather(src1, kinds, ag0[-1], "gather_layer1_start")
    ag_token = ag1[-1]
    full = {n: [None, None] for n in big_names}
    for k, t in zip(early, g0):
        full[big_names[k]][0] = whole(t)
    for n, g in zip(small_names, g0[len(early):]):
        shp = a[n].shape
        full[n] = g[:, :math.prod(shp[1:-1])].reshape(shp[:-1] + (4 * shp[-1],))
    for n in SMALL:
        if n not in SMALL_SHARDED:
            full[n] = a[n]

    cvec = jnp.concatenate([c_ctx.reshape(1, d), c.reshape(1, d), jnp.zeros((6, d), F32)], axis=0)
    avec = (cvec * jax.nn.sigmoid(cvec) + ag_token[0, 0]).astype(MM_DTYPE)

    def row(v):
        return v.reshape(1, -1)

    X = jnp.concatenate([ctx[0], x[0]], axis=0)
    saved = []
    gk, gv = dm.GK, d
    lrblk = (7 * d + d // 2) // LANES
    for l in range(depth):
        if l == 1:
            got = wait_copies(ag1[1], ag1[2], ag1[3], ag1[4], ag1[0], X, "gather_layer1_wait")
            got = forward_halves(got, kinds, "gather_layer1_forward")
            for n, t in zip(big_names, got):
                full[n][1] = whole(t)
        s = types.SimpleNamespace()
        s.w_in_p = _w_in_t_to_proj(full['w_in'][l], d, wl, wlp)
        wd = full['w_decay'][l]
        wdp = jnp.zeros((LANES, 2 * gk), F32)
        wdp = wdp.at[:GLA_LR, :gk].set(wd[0]).at[GLA_LR:2 * GLA_LR, gk:].set(wd[1])
        s.wdp = wdp.astype(MM_DTYPE)
        s.wdp_wide = jnp.pad(s.wdp, ((0, d // 2 - LANES), (0, 0)))
        s.bd = full['b_decay'][l].reshape(1, 2 * gk)
        modraw = matmul(avec, full['w_ada'][l], 'nn', F32, f"mod_{l}") + full['b_ada'][l][None, :]
        s.mod = [modraw[0:2, j * d:(j + 1) * d].reshape(2, 1, d) for j in range(6)]
        s.x = X
        (s.h,) = rowwise(pre_fn, [X], s.mod[0:2], [row(g_pre_mix[l])], [(d, MM_DTYPE)], dm, f"pre_{l}")
        s.P = matmul(s.h, s.w_in_p, 'nt', MM_DTYPE, f"in_proj_{l}")
        P = s.P
        s.z = matmul((P, LANES, lrblk), s.wdp, 'nn', F32, f"decay_proj_{l}", tk=LANES)
        la_f, la_b = rowwise(decay_fn, [s.z], [], [s.bd], [(gk, F32), (gk, F32)], dm, f"decay_{l}")
        s.la = jnp.concatenate([la_f, la_b], axis=1)
        s.o_f, s.st_f = gla_fwd(P, s.la, False, dm, f"gla_fwd_f_{l}")
        s.o_b, s.st_b = gla_fwd(P, s.la, True, dm, f"gla_fwd_b_{l}")
        (s.gin,) = rowwise(glaout_fn, [s.o_f, s.o_b, (P, d, 3)], [], [row(g_gla[l])], [(gv, MM_DTYPE)], dm,
                           f"gla_out_{l}")
        if l == 0:
            got = wait_copies(ag0[1], ag0[2], ag0[3], ag0[4], ag0[0], s.gin, "gather_layer0_late_wait")
            got = forward_halves(got, [kinds[k] for k in late], "gather_layer0_late_forward")
            for k, t in zip(late, got):
                full[big_names[k]][0] = whole(t)
        s.ya = matmul(s.gin, full['w_gla_o'][l], 'nn', MM_DTYPE, f"gla_o_{l}")
        (s.u,) = rowwise(glu_fn, [(P, d, 6)], [], [], [(d // 2, F32)], dm, f"glu_{l}")
        s.yconv = conv_fwd(s.u, full['w_dw'][l], dm, f"conv_{l}")
        (s.cin,) = rowwise(convpost_fn, [s.yconv], [], [row(b_dw[l]), row(g_conv_ln[l]), row(b_conv_ln[l])],
                           [(d // 2, MM_DTYPE)], dm, f"conv_post_{l}")
        s.yb = matmul(s.cin, full['w_conv_o'][l], 'nn', MM_DTYPE, f"conv_o_{l}")
        s.pm = pool_mix((P, d // 2, 14), False, dm, f"pool_mix_{l}")
        s.pc = group_mm(s.pm, w_pool_g[l], 'nn', F32, f"pool_g_{l}")
        (s.pin,) = rowwise(poolpost_fn, [s.pc], [], [row(s_pool[l])], [(d // 2, MM_DTYPE)], dm, f"pool_post_{l}")
        s.yc = matmul(s.pin, full['w_pool_o'][l], 'nn', MM_DTYPE, f"pool_o_{l}")
        s.bg = [row(full['b_gate'][l][j]) for j in range(3)]
        (s.mixed,) = rowwise(merge_fn, [s.ya, s.yb, s.yc, (P, 3 * d, 0)], [], s.bg, [(d, MM_DTYPE)], dm,
                             f"merge_{l}", tm=tmw)
        s.y = matmul(s.mixed, full['w_out'][l], 'nn', MM_DTYPE, f"out_proj_{l}")
        s.x1, s.h2 = rowwise(mid_fn, [X, s.y], s.mod[2:5], [row(g_post_mix[l]), row(g_pre_mlp[l])],
                             [(d, F32), (d, MM_DTYPE)], dm, f"mid_{l}")
        s.act = matmul(s.h2, full['w_mlp1'][l], 'nn', MM_DTYPE, f"mlp1_{l}", epi=relu2_epi)
        s.y2 = matmul(s.act, full['w_mlp2'][l], 'nn', MM_DTYPE, f"mlp2_{l}")
        (X,) = rowwise(post_fn, [s.x1, s.y2], s.mod[5:6], [row(g_post_mlp[l])], [(d, F32)], dm, f"post_{l}")
        saved.append(s)

    dX, lossv = loss_head(X, loss_target[0], dm, "loss_head")
    loss = lax.psum(lossv[0, 0], ("x", "y", "c"))

    grads = {n: [None] * depth for n in WEIGHTS if n != 'c_ctx' and n not in BIG}
    gbig = {n: [None] * depth for n in BIG}
    rs_token = None

    def start_scatter(idx, layer, after, name):
        gs = [gbig[big_names[k]][layer] for k in idx]
        wd = [t.shape[1] // 4 if kinds[k] == 'col' else t.shape[0] // 4 for t, k in zip(gs, idx)]
        plan = _scatter_plan([big_axis[big_names[k]] - 1 for k in idx], wd)
        lands = [lax.empty((3, t.shape[0], w) if kinds[k] == 'col' else (3, w, t.shape[1]), t.dtype)
                 for t, w, k in zip(gs, wd, idx)]
        return (plan,) + start_copies(gs, lands, plan, 3 * len(gs), after, name)

    g_cctx = jnp.zeros((d,), F32)
    for l in reversed(range(depth)):
        s = saved[l]
        P = s.P
        dmod = [None] * 6
        gpm = row(g_post_mlp[l]) if rs_token is None else row(g_post_mlp[l]) + rs_token[0, 0]
        (dx1, dy2), (dmod[5],), (dg,) = rowwise_vjp(post_fn, [s.x1, s.y2], s.mod[5:6], [gpm], [dX],
                                                     dm, f"post_bwd_{l}", narrow=(1,))
        grads['g_post_mlp'][l] = dg[0]
        du1 = matmul(dy2, full['w_mlp2'][l], 'nt', MM_DTYPE, f"mlp2_dx_{l}", epi=relu2_bwd_epi, extras=[s.act])
        gbig['w_mlp2'][l] = matmul(s.act, dy2, 'tn', MM_DTYPE, f"mlp2_dw_{l}")
        dh2 = matmul(du1, full['w_mlp1'][l], 'nt', MM_DTYPE, f"mlp1_dx_{l}")
        gbig['w_mlp1'][l] = matmul(s.h2, du1, 'tn', MM_DTYPE, f"mlp1_dw_{l}")
        gpx = row(g_post_mix[l])
        (dxa, dy), dmod[2:5], (dg1, dg2) = rowwise_vjp(
            mid_fn, [s.x, s.y], s.mod[2:5], [gpx, row(g_pre_mlp[l])], [dx1, dh2], dm, f"mid_bwd_{l}", narrow=(1,))
        grads['g_post_mix'][l], grads['g_pre_mlp'][l] = dg1[0], dg2[0]
        dmixed = matmul(dy, full['w_out'][l], 'nt', MM_DTYPE, f"out_proj_dx_{l}")
        gbig['w_out'][l] = matmul(s.mixed, dy, 'tn', MM_DTYPE, f"out_proj_dw_{l}")
        (dya, dyb, dyc, dP), _, dbg = rowwise_vjp(merge_fn, [s.ya, s.yb, s.yc, (P, 3 * d, 0)], [], s.bg, [dmixed],
                                                  dm, f"merge_bwd_{l}", tm=tmw, narrow=(0, 1, 2),
                                                  into=(3, None, P.shape))
        grads['b_gate'][l] = jnp.concatenate(dbg, axis=0)
        dgin = matmul(dya, full['w_gla_o'][l], 'nt', MM_DTYPE, f"gla_o_dx_{l}")
        gbig['w_gla_o'][l] = matmul(s.gin, dya, 'tn', MM_DTYPE, f"gla_o_dw_{l}")
        dcin = matmul(dyb, full['w_conv_o'][l], 'nt', MM_DTYPE, f"conv_o_dx_{l}")
        gbig['w_conv_o'][l] = matmul(s.cin, dyb, 'tn', MM_DTYPE, f"conv_o_dw_{l}")
        dpin = matmul(dyc, full['w_pool_o'][l], 'nt', MM_DTYPE, f"pool_o_dx_{l}")
        gbig['w_pool_o'][l] = matmul(s.pin, dyc, 'tn', MM_DTYPE, f"pool_o_dw_{l}")
        sp = row(s_pool[l])
        if l == 0:
            rs0 = start_scatter(late, 0, dpin, "grad_layer0_late_start")
            sp = sp + rs0[-1][0, 0]
        (dpc,), _, (dsp,) = rowwise_vjp(poolpost_fn, [s.pc], [], [sp], [dpin], dm, f"pool_post_bwd_{l}")
        grads['s_pool'][l] = dsp[0]
        grads['w_pool_g'][l] = group_mm(s.pm, w_pool_g[l], 'tn', F32, f"pool_g_dw_{l}", b=dpc)
        dpm = group_mm(dpc, w_pool_g[l], 'nt', F32, f"pool_g_dx_{l}")
        dP = pool_mix(dpm, True, dm, f"pool_mix_bwd_{l}", into=(dP, 14))
        (dyconv,), _, (dbdw, dgln, dbln) = rowwise_vjp(
            convpost_fn, [s.yconv], [], [row(b_dw[l]), row(g_conv_ln[l]), row(b_conv_ln[l])], [dcin], dm,
            f"conv_post_bwd_{l}")
        grads['b_dw'][l], grads['g_conv_ln'][l], grads['b_conv_ln'][l] = dbdw[0], dgln[0], dbln[0]
        du, grads['w_dw'][l] = conv_bwd(s.u, full['w_dw'][l], dyconv, dm, f"conv_bwd_{l}")
        (dP,), _, _ = rowwise_vjp(glu_fn, [(P, d, 6)], [], [], [du], dm, f"glu_bwd_{l}", into=(0, dP, P.shape))
        (do, _, dP), _, (dgg,) = rowwise_vjp(glaout_fn, [s.o_f, s.o_b, (P, d, 3)], [], [row(g_gla[l])], [dgin], dm,
                                             f"gla_out_bwd_{l}", want=[True, False, True], into=(2, dP, P.shape))
        grads['g_gla'][l] = dgg[0]
        dqf, dkf, dvf, dlaf = gla_bwd(P, s.la, do, s.st_f, False, dm, f"gla_bwd_f_{l}")
        dP, dlab = gla_bwd(P, s.la, do, s.st_b, True, dm, f"gla_bwd_b_{l}", prev=(dqf, dkf, dvf), into=dP)
        (dz,), _, (dbd,) = rowwise_vjp(decay_fn, [s.z], [], [s.bd], [dlaf, dlab], dm, f"decay_bwd_{l}", narrow=(0,))
        grads['b_decay'][l] = dbd.reshape(2, gk)
        dwdp = matmul((P, LANES, lrblk), dz, 'tn', F32, f"decay_proj_dw_{l}", tm=LANES)
        grads['w_decay'][l] = jnp.stack([dwdp[:GLA_LR, :gk], dwdp[GLA_LR:2 * GLA_LR, gk:]])
        dP = matmul(dz, s.wdp_wide, 'nt', MM_DTYPE, f"decay_proj_dx_{l}", into=(dP, 15))
        dh = matmul(dP, s.w_in_p, 'nn', MM_DTYPE, f"in_proj_dx_{l}")
        gbig['w_in'][l] = _proj_to_w_in_t(matmul(dP, s.h, 'tn', MM_DTYPE, f"in_proj_dw_{l}"), d, wl, wlp)
        (dX,), dmod[0:2], (dg,) = rowwise_vjp(pre_fn, [s.x], s.mod[0:2], [row(g_pre_mix[l])], [dh], dm,
                                               f"pre_bwd_{l}", adds={0: dxa})
        grads['g_pre_mix'][l] = dg[0]
        dmodflat = jnp.concatenate([jnp.concatenate([m_.reshape(2, d) for m_ in dmod], axis=1),
                                    jnp.zeros((6, 6 * d), F32)], axis=0)
        grads['b_ada'][l] = dmodflat[0] + dmodflat[1]
        gbig['w_ada'][l] = matmul(avec, dmodflat, 'tn', MM_DTYPE, f"ada_dw_{l}")
        dav = matmul(dmodflat, full['w_ada'][l], 'nt', F32, f"ada_dx_{l}")
        g_cctx = g_cctx + dav[0] * _silu_grad(c_ctx)
        if l == 1:
            rs1 = start_scatter(list(range(nbig)), 1, dav, "grad_layer1_start")
            rs_token = rs1[-1]

    grad_x = dX[dm.CTX:][None]
    gfull = {n: jnp.stack(v) for n, v in grads.items()}
    gfull['c_ctx'] = g_cctx
    where = jnp.concatenate([chip1, core1])

    def halves_view(t, k):
        return t.reshape(2, t.shape[0] // 2, t.shape[1]) if k == 'col' else t.reshape(4, 2, t.shape[0] // 8, t.shape[1])
    enames = [big_names[k] for k in early]
    ekinds = [kinds[k] for k in early]
    v0 = [halves_view(gbig[n][0], k) for n, k in zip(enames, ekinds)]
    r1 = pair_swap_halves(v0, ekinds, "grad_pair_swap")
    hs = [pair_add(v.reshape((-1,) + v.shape[-2:]), r.reshape((-1,) + r.shape[-2:]), core1, f"grad_pair_add_{n}")
          for n, v, r in zip(enames, v0, r1)]
    hx = [h.reshape(h.shape[1:]) if k == 'col' else h for h, k in zip(hs, ekinds)]
    ex_plan = _exchange_plan(ekinds)
    ex_lands = [lax.empty((3, h.shape[0], h.shape[1] // 4) if k == 'col' else (3,) + h.shape[1:], h.dtype)
                for h, k in zip(hx, ekinds)]
    ex = (ex_plan,) + start_copies(hx, ex_lands, ex_plan, 3 * len(hx), hx[0], "grad_chip_exchange_start")

    got0 = wait_copies(rs0[1], rs0[2], rs0[3], rs0[4], rs0[0], ex[-1], "grad_layer0_late_wait")
    got1 = wait_copies(rs1[1], rs1[2], rs1[3], rs1[4], rs1[0], ex[-1], "grad_layer1_wait")
    sa = [chip_add(g, r, big_axis[big_names[k]] - 1, where, f"grad_layer0_add_{big_names[k]}", slab=False)
          for k, g, r in zip(late, rs0[3], got0)]
    sa += [chip_add(g, r, big_axis[n] - 1, where, f"grad_layer1_add_{n}", slab=False)
           for n, g, r in zip(big_names, rs1[3], got1)]
    sb = pair_swap(sa, "grad_late_pair_swap")
    red0 = {big_names[k]: [sa[j], sb[j]] for j, k in enumerate(late)}
    red1 = {n: [sa[len(late) + k], sb[len(late) + k]] for k, n in enumerate(big_names)}

    sflat = _flatten_pad([gfull[n].astype(F32) for n in SMALL], F32)
    sv = sflat.reshape(2, sflat.shape[0] // 2, LANES)
    (sr,) = pair_swap_halves([sv], ['col'], "small_grad_pair_swap")
    sh = pair_add(sv, sr[None], core1, "small_grad_pair_add")[0]
    sq = quad_sum(sh, chip_broadcast(sh, "small_grad_chip_exchange"), core1, "small_grad_chip_sum")
    (ssum,) = pair_join_layers([sq], "small_grad_pair_join")
    ssum = ssum.reshape(-1)

    out_g, out_d, out_m, out_v = {}, {}, {}, {}

    def update_big(n, terms0):
        out_g[n], out_d[n], out_m[n], out_v[n] = adamw_layers(a[n], a['m_' + n], a['v_' + n], terms0, red1[n],
                                                              f"adamw_{n}")
    for k in late:
        update_big(big_names[k], red0[big_names[k]])
    start = 0
    sg = {}
    for n in SMALL:
        cnt = gfull[n].size
        g = ssum[start:start + cnt].reshape(gfull[n].shape)
        start += cnt
        if n in SMALL_SHARDED:
            ax = SMALL_SHARDED[n]
            wdt = a[n].shape[ax]
            g = lax.dynamic_slice_in_dim(g, chip * wdt, wdt, axis=ax)
        sg[n] = g
    pk = lambda dct, pre: _flatten_pad([dct[pre + n] for n in SMALL], F32)
    gs = _flatten_pad([sg[n] for n in SMALL], F32)
    dl, mn, vn = adamw(pk(a, ''), gs, pk(a, 'm_'), pk(a, 'v_'), "adamw_small")
    r2 = wait_copies(ex[1], ex[2], ex[3], ex[4], ex[0], dl, "grad_chip_exchange_wait")
    dl, mn, vn = dl.reshape(-1), mn.reshape(-1), vn.reshape(-1)
    start = 0
    for n in SMALL:
        cnt, shp = a[n].size, a[n].shape
        out_g[n] = sg[n]
        out_d[n], out_m[n], out_v[n] = (t[start:start + cnt].reshape(shp) for t in (dl, mn, vn))
        start += cnt
    fs = [chip_add(h.reshape(-1, h.shape[-1]), r, big_axis[n] - 1, where, f"grad_chip_add_{n}")
          for n, h, r in zip(enames, ex[3], r2)]
    for n, t in zip(enames, pair_join_layers(fs, "grad_pair_join")):
        update_big(n, [t.reshape(-1, t.shape[-1])])
    for dct in (out_g, out_d, out_m, out_v):
        dct['w_in'] = jnp.swapaxes(dct['w_in'], 1, 2)
    return (loss, grad_x, *[out_g[n] for n in WEIGHTS], *[out_d[n] for n in WEIGHTS],
            *[out_m[n] for n in WEIGHTS], *[out_v[n] for n in WEIGHTS])
```

```python
import functools
import math
import types

import jax
import jax.numpy as jnp
from jax import lax
from jax.experimental import pallas as pl
from jax.experimental.pallas import tpu as pltpu

F32 = jnp.float32
MM_DTYPE = jnp.bfloat16
VMEM_LIMIT_V7X = 56 * 1024 * 1024
LANES = 128
EPS = 1e-6

N_HEADS = 4
GLA_CHUNK = 64
GLA_TAU = 16.0
GLA_LR = 16
GRID_W = 64
POOL_WINDOWS = (2, 4, 8, 16)

ADAM_LR = 0.001
ADAM_B1 = 0.9
ADAM_B2 = 0.999
ADAM_EPS = 1e-08
ADAM_WD = 0.01
ADAM_STEP = 10

NN = (((1,), (0,)), ((), ()))
NT = (((1,), (1,)), ((), ()))
TN = (((0,), (0,)), ((), ()))

WEIGHTS = ['c_ctx', 'w_ada', 'b_ada', 'g_pre_mix', 'g_post_mix', 'g_pre_mlp', 'g_post_mlp', 'w_in', 'w_decay',
           'b_decay', 'g_gla', 'w_gla_o', 'w_dw', 'b_dw', 'g_conv_ln', 'b_conv_ln', 'w_conv_o', 'w_pool_g',
           's_pool', 'w_pool_o', 'b_gate', 'w_out', 'w_mlp1', 'w_mlp2']
BIG = {'w_ada': 2, 'w_in': 2, 'w_gla_o': 1, 'w_conv_o': 2, 'w_pool_o': 2, 'w_out': 1, 'w_mlp1': 2, 'w_mlp2': 1}
SMALL_SHARDED = {'w_decay': 3, 'b_decay': 2, 'w_dw': 2, 'b_gate': 2}
SMALL = [n for n in WEIGHTS if n not in BIG]


def _tile(n, prefs):
    for t in prefs:
        if n % t == 0:
            return t
    return n


def _cparams(sem=None, **kw):
    return pltpu.CompilerParams(dimension_semantics=sem, vmem_limit_bytes=VMEM_LIMIT_V7X, **kw)


def _dot(a, b, dims=NN):
    return lax.dot_general(a.astype(MM_DTYPE), b.astype(MM_DTYPE), dims, preferred_element_type=F32)


def matmul(a, b, mode, out_dtype, name, tm=None, tn=None, tk=None, epi=None, extras=(), into=None):
    a, aw, ablk = a if isinstance(a, tuple) else (a, a.shape[1], 0)
    if mode == 'nn':
        M, K, N = a.shape[0], aw, b.shape[1]
    elif mode == 'nt':
        M, K, N = a.shape[0], aw, b.shape[0]
    else:
        K, M, N = a.shape[0], aw, b.shape[1]
    big = (1088, 1024, 640, 544, 512, 320, 256, 128, 64, 32, 16, 8)
    if mode == 'tn':
        tm = tm or _tile(M, (1024, 512, 256, 128))
        tn = tn or _tile(N, (1024, 512, 256, 128))
        tk = tk or _tile(K, big)
    else:
        tm = tm or _tile(M, big)
        tn = tn or _tile(N, (1024, 512, 256, 128))
        tk = tk or _tile(K, (1024, 512, 256, 128))
    if aw != a.shape[1]:
        assert (mode == 'tn' and tm == aw) or (mode != 'tn' and tk == aw)
    nk = K // tk
    ne = len(extras)
    dims = {'nn': NN, 'nt': NT, 'tn': TN}[mode]

    def body(a_ref, b_ref, *rest):
        e_refs, o_ref = rest[:ne], rest[ne + (into is not None)]

        def finish(acc):
            if epi is not None:
                acc = epi(acc, *[e[...] for e in e_refs])
            o_ref[...] = acc.astype(o_ref.dtype)

        p = _dot(a_ref[...], b_ref[...], dims)
        if nk == 1:
            finish(p)
            return
        acc = rest[-1]
        k = pl.program_id(2)

        @pl.when(k == 0)
        def _():
            acc[...] = p

        @pl.when(k > 0)
        def _():
            acc[...] += p

        @pl.when(k == nk - 1)
        def _():
            finish(acc[...])

    if mode == 'nn':
        a_spec = pl.BlockSpec((tm, tk), lambda i, j, k: (i, k + ablk))
        b_spec = pl.BlockSpec((tk, tn), lambda i, j, k: (k, j))
    elif mode == 'nt':
        a_spec = pl.BlockSpec((tm, tk), lambda i, j, k: (i, k + ablk))
        b_spec = pl.BlockSpec((tn, tk), lambda i, j, k: (j, k))
    else:
        a_spec = pl.BlockSpec((tk, tm), lambda i, j, k: (k, i + ablk))
        b_spec = pl.BlockSpec((tk, tn), lambda i, j, k: (k, j))
    tile = pl.BlockSpec((tm, tn), lambda i, j, k: (i, j))
    if into is None:
        out_spec, out_shape, more, extra, aliases = tile, jax.ShapeDtypeStruct((M, N), out_dtype), [], [], {}
    else:
        buf, oblk = into
        out_spec = pl.BlockSpec((tm, tn), lambda i, j, k: (i, oblk * (N // tn) + j))
        out_shape = jax.ShapeDtypeStruct(buf.shape, buf.dtype)
        more, extra, aliases = [pl.BlockSpec(memory_space=pl.ANY)], [buf], {2 + ne: 0}
    return pl.pallas_call(
        body, name=name, grid=(M // tm, N // tn, nk),
        in_specs=[a_spec, b_spec] + [tile] * ne + more, out_specs=out_spec,
        out_shape=out_shape, input_output_aliases=aliases,
        scratch_shapes=[] if nk == 1 else [pltpu.VMEM((tm, tn), F32)],
        compiler_params=_cparams(("parallel", "parallel", "arbitrary")),
    )(a, b, *extras, *extra)


def group_mm(a, w, mode, out_dtype, name, b=None):
    T = a.shape[0]
    G, gc, _ = w.shape
    col = pl.BlockSpec((T, gc), lambda g: (0, g))
    wsp = pl.BlockSpec((1, gc, gc), lambda g: (g, 0, 0))
    if mode == 'tn':
        def body(a_ref, b_ref, o_ref):
            o_ref[0] = _dot(a_ref[...], b_ref[...], TN).astype(o_ref.dtype)
        return pl.pallas_call(body, name=name, grid=(G,), in_specs=[col, col], out_specs=wsp,
                              out_shape=jax.ShapeDtypeStruct((G, gc, gc), out_dtype),
                              compiler_params=_cparams(("parallel",)))(a, b)
    dims = NN if mode == 'nn' else NT

    def body(a_ref, w_ref, o_ref):
        o_ref[...] = _dot(a_ref[...], w_ref[0], dims).astype(o_ref.dtype)
    return pl.pallas_call(body, name=name, grid=(G,), in_specs=[col, wsp], out_specs=col,
                          out_shape=jax.ShapeDtypeStruct((T, G * gc), out_dtype),
                          compiler_params=_cparams(("parallel",)))(a, w)


def _rowspec(r):
    return r if isinstance(r, tuple) else (r, r.shape[1], 0)


def _row_specs(rows, segs, consts, tm, nctx):
    specs = [pl.BlockSpec((tm, w), lambda i, b=b: (i, b)) for _, w, b in rows]
    specs += [pl.BlockSpec((1,) + s.shape[1:], lambda i, n=s.ndim: (jnp.where(i >= nctx, 1, 0),) + (0,) * (n - 1))
              for s in segs]
    specs += [pl.BlockSpec(c.shape, lambda i, n=c.ndim: (0,) * n) for c in consts]
    return specs


def rowwise(fn, rows, segs, consts, outs, dm, name, tm=None):
    tm = tm or dm.tm
    nctx = dm.CTX // tm
    rows = [_rowspec(r) for r in rows]
    nr, ns, nc = len(rows), len(segs), len(consts)

    def body(*refs):
        rin = [r[...] for r in refs[:nr]]
        sin = [s[0] for s in refs[nr:nr + ns]]
        cin = [c[...] for c in refs[nr + ns:nr + ns + nc]]
        res = fn(*rin, *sin, *cin)
        for o_ref, v in zip(refs[nr + ns + nc:], res):
            o_ref[...] = v.astype(o_ref.dtype)

    res = pl.pallas_call(
        body, name=name, grid=(dm.T // tm,),
        in_specs=_row_specs(rows, segs, consts, tm, nctx),
        out_specs=[pl.BlockSpec((tm, w), lambda i: (i, 0)) for w, _ in outs],
        out_shape=[jax.ShapeDtypeStruct((dm.T, w), dt) for w, dt in outs],
        compiler_params=_cparams(("parallel",)),
    )(*[r[0] for r in rows], *segs, *consts)
    return res


def rowwise_vjp(fn, rows, segs, consts, cots, dm, name, tm=None, want=None, adds=None, narrow=(), into=None):
    tm = tm or dm.tm
    nctx = dm.CTX // tm
    rows = [_rowspec(r) for r in rows]
    cots = [_rowspec(r) for r in cots]
    adds = adds or {}
    nr, ns, nc, nct = len(rows), len(segs), len(consts), len(cots)
    want = want or [True] * nr
    widx = [k for k in range(nr) if want[k]]
    akeys = sorted(adds)

    def body(*refs):
        i = pl.program_id(0)
        rin = [r[...] for r in refs[:nr]]
        sin = [s[0] for s in refs[nr:nr + ns]]
        cin = [c[...] for c in refs[nr + ns:nr + ns + nc]]
        p = nr + ns + nc
        cot_refs = refs[p:p + nct]
        add_refs = dict(zip(akeys, refs[p + nct:p + nct + len(akeys)]))
        p = p + nct + len(akeys) + (1 if (into is not None and into[1] is not None) else 0)
        rg_refs = refs[p:p + len(widx)]
        sg_refs = refs[p + len(widx):p + len(widx) + ns]
        cg_refs = refs[p + len(widx) + ns:]
        res, vjp = jax.vjp(fn, *rin, *sin, *cin)
        g = vjp(tuple(cr[...].astype(o.dtype) for cr, o in zip(cot_refs, res)))
        for o_ref, k in zip(rg_refs, widx):
            v = g[k].astype(F32)
            if k in add_refs:
                v = v + add_refs[k][...]
            o_ref[...] = v.astype(o_ref.dtype)
        first_seg = jnp.logical_or(i == 0, i == nctx)
        for o_ref, v in zip(sg_refs, g[nr:nr + ns]):
            @pl.when(first_seg)
            def _(o_ref=o_ref, v=v):
                o_ref[0] = v.astype(F32)

            @pl.when(jnp.logical_not(first_seg))
            def _(o_ref=o_ref, v=v):
                o_ref[0] += v.astype(F32)
        for o_ref, v in zip(cg_refs, g[nr + ns:]):
            @pl.when(i == 0)
            def _(o_ref=o_ref, v=v):
                o_ref[...] = v.astype(F32)

            @pl.when(i > 0)
            def _(o_ref=o_ref, v=v):
                o_ref[...] += v.astype(F32)

    in_specs = _row_specs(rows, segs, consts, tm, nctx)
    in_specs += [pl.BlockSpec((tm, w), lambda i, b=b: (i, b)) for _, w, b in cots]
    in_specs += [pl.BlockSpec((tm, adds[k].shape[1]), lambda i: (i, 0)) for k in akeys]
    out_specs = [pl.BlockSpec((tm, rows[k][1]), lambda i: (i, 0)) for k in widx]
    out_shape = [jax.ShapeDtypeStruct((dm.T, rows[k][1]), MM_DTYPE if k in narrow else rows[k][0].dtype)
                 for k in widx]
    extra, aliases = [], {}
    if into is not None:
        ik, ibuf, ishape = into
        out_specs[widx.index(ik)] = pl.BlockSpec((tm, rows[ik][1]), lambda i, b=rows[ik][2]: (i, b))
        out_shape[widx.index(ik)] = jax.ShapeDtypeStruct(ishape, MM_DTYPE)
        if ibuf is not None:
            aliases = {len(in_specs): widx.index(ik)}
            in_specs = in_specs + [pl.BlockSpec(memory_space=pl.ANY)]
            extra = [ibuf]
    out_specs += [pl.BlockSpec((1,) + s.shape[1:], lambda i, n=s.ndim: (jnp.where(i >= nctx, 1, 0),) + (0,) * (n - 1))
                  for s in segs]
    out_shape += [jax.ShapeDtypeStruct(s.shape, F32) for s in segs]
    out_specs += [pl.BlockSpec(c.shape, lambda i, n=c.ndim: (0,) * n) for c in consts]
    out_shape += [jax.ShapeDtypeStruct(c.shape, F32) for c in consts]
    res = pl.pallas_call(
        body, name=name, grid=(dm.T // tm,), in_specs=in_specs, out_specs=out_specs, out_shape=out_shape,
        input_output_aliases=aliases, compiler_params=_cparams(("arbitrary",)),
    )(*[r[0] for r in rows], *segs, *consts, *[r[0] for r in cots], *[adds[k] for k in akeys], *extra)
    rg = [None] * nr
    for k, v in zip(widx, res[:len(widx)]):
        rg[k] = v
    return rg, list(res[len(widx):len(widx) + ns]), list(res[len(widx) + ns:])


def _rms(x, g):
    return x * lax.rsqrt(jnp.mean(x * x, axis=-1, keepdims=True) + EPS) * g


def _sigmoid(x):
    return jax.nn.sigmoid(x)


def pre_fn(x, shift, scale, g):
    return ((_rms(x, g) * (1.0 + scale) + shift).astype(MM_DTYPE),)


def mid_fn(x, y, gate, shift, scale, g_post, g_pre):
    x1 = x + gate * _rms(y.astype(F32), g_post)
    return x1, (_rms(x1, g_pre) * (1.0 + scale) + shift).astype(MM_DTYPE)


def post_fn(x1, y2, gate, g):
    return (x1 + gate * _rms(y2.astype(F32), g),)


def relu2_epi(acc):
    r = jnp.maximum(acc, 0.0)
    return r * r


def relu2_bwd_epi(dact, act):
    return dact * (2.0 * jnp.sqrt(act.astype(F32)))


def decay_fn(z, bd):
    zz = z.astype(F32) + bd
    ls = jnp.minimum(zz, 0.0) - jnp.log(1.0 + jnp.exp(jnp.minimum(zz, -zz)))
    la = ls / GLA_TAU
    gk = la.shape[1] // 2
    return la[:, :gk], la[:, gk:]


def glu_fn(ab):
    h = ab.shape[1] // 2
    return (ab[:, :h].astype(F32) * _sigmoid(ab[:, h:].astype(F32)),)


def glaout_fn(o_f, o_b, og, g):
    o = o_f + o_b
    dv = o.shape[1] // N_HEADS
    hs = []
    for h in range(N_HEADS):
        oh = o[:, h * dv:(h + 1) * dv]
        hs.append(oh * lax.rsqrt(jnp.mean(oh * oh, axis=-1, keepdims=True) + EPS) * g[:, h * dv:(h + 1) * dv])
    og = og.astype(F32)
    return ((jnp.concatenate(hs, axis=1) * (og * _sigmoid(og))).astype(MM_DTYPE),)


def convpost_fn(y, b_dw, g, b):
    y = y + b_dw
    mu = jnp.mean(y, axis=-1, keepdims=True)
    xc = y - mu
    yn = xc * lax.rsqrt(jnp.mean(xc * xc, axis=-1, keepdims=True) + EPS) * g + b
    return ((yn * _sigmoid(yn)).astype(MM_DTYPE),)


def poolpost_fn(pc, s):
    return ((pc.astype(F32) * s).astype(MM_DTYPE),)


def merge_fn(ya, yb, yc, mg, bg0, bg1, bg2):
    d = ya.shape[1]
    mg = mg.astype(F32)
    mixed = (_sigmoid(mg[:, :d] + bg0) * ya.astype(F32) + _sigmoid(mg[:, d:2 * d] + bg1) * yb.astype(F32)
             + _sigmoid(mg[:, 2 * d:] + bg2) * yc.astype(F32))
    return (mixed.astype(MM_DTYPE),)


def _split_dot(lmat, x, dims):
    hi = x.astype(MM_DTYPE)
    lo = x - hi.astype(F32)
    return _dot(lmat, hi, dims) + _dot(lmat, lo, dims)


def _gla_block_order(dm, rev):
    nctx, nb = dm.CTX // dm.TB, dm.T // dm.TB

    def blk(i):
        if not rev:
            return i
        return jnp.where(i < nctx, nctx - 1 - i, nb - 1 - (i - nctx))
    return blk, nb


def _gla_tri(rev):
    c = GLA_CHUNK
    t = lax.broadcasted_iota(jnp.int32, (c, c), 0)
    s = lax.broadcasted_iota(jnp.int32, (c, c), 1)
    return (s >= t) if rev else (s <= t)


def _gla_chunk_terms(q, k, la, tri, scale):
    lmat = tri.astype(MM_DTYPE)
    b = _split_dot(lmat, la, NN)
    bend = jnp.sum(la, axis=0, keepdims=True)
    eb = jnp.exp(b)
    enb = jnp.exp(-b)
    ee = jnp.exp(bend - b)
    qi = q * scale * eb
    ki = k * enb
    kend = k * ee
    att = jnp.where(tri, _dot(qi, ki, NT), 0.0)
    return lmat, bend, eb, enb, ee, qi, ki, kend, att


def gla_fwd(P, la, rev, dm, name):
    c, tb, h_, dk, dv, d = GLA_CHUNK, dm.TB, N_HEADS, dm.DK, dm.DV, dm.D
    cpb = tb // c
    blk, nb = _gla_block_order(dm, rev)
    gk, gv = h_ * dk, h_ * dv
    qb, kb, vb, lb = (5 * d) // gk, (5 * d + d // 2) // gk, (4 * d) // gv, (1 if rev else 0)
    scale = dk ** -0.5
    order = list(range(cpb))[::-1] if rev else list(range(cpb))

    def body(q_ref, k_ref, v_ref, la_ref, o_ref, s_ref, st):
        @pl.when(pl.program_id(0) == 0)
        def _():
            st[...] = jnp.zeros_like(st)
        tri = _gla_tri(rev)
        for n, ci in enumerate(order):
            r = pl.ds(ci * c, c)
            for hh in range(h_):
                ck, cv = pl.ds(hh * dk, dk), pl.ds(hh * dv, dv)
                q = q_ref[r, ck].astype(F32)
                k = k_ref[r, ck].astype(F32)
                v = v_ref[r, cv]
                _, bend, _, _, _, qi, _, kend, att = _gla_chunk_terms(q, k, la_ref[r, ck], tri, scale)
                s_in = st[hh]
                o_ref[r, cv] = _dot(att, v) + _dot(qi, s_in, NT)
                s_ref[n, hh] = s_in
                st[hh] = jnp.exp(bend) * s_in + _dot(v, kend, TN)

    return pl.pallas_call(
        body, name=name, grid=(nb,),
        in_specs=[pl.BlockSpec((tb, gk), lambda i: (blk(i), qb)),
                  pl.BlockSpec((tb, gk), lambda i: (blk(i), kb)),
                  pl.BlockSpec((tb, gv), lambda i: (blk(i), vb)),
                  pl.BlockSpec((tb, gk), lambda i: (blk(i), lb))],
        out_specs=[pl.BlockSpec((tb, gv), lambda i: (blk(i), 0)),
                   pl.BlockSpec((cpb, h_, dv, dk), lambda i: (i, 0, 0, 0))],
        out_shape=[jax.ShapeDtypeStruct((dm.T, gv), F32),
                   jax.ShapeDtypeStruct((dm.T // c, h_, dv, dk), F32)],
        scratch_shapes=[pltpu.VMEM((h_, dv, dk), F32)],
        compiler_params=_cparams(("arbitrary",)),
    )(P, P, P, la)


def gla_bwd(P, la, do, states, rev, dm, name, prev=None, into=None):
    c, tb, h_, dk, dv, d = GLA_CHUNK, dm.TB, N_HEADS, dm.DK, dm.DV, dm.D
    cpb = tb // c
    blk, nb = _gla_block_order(dm, rev)
    gk, gv = h_ * dk, h_ * dv
    qb, kb, vb, lb = (5 * d) // gk, (5 * d + d // 2) // gk, (4 * d) // gv, (1 if rev else 0)
    scale = dk ** -0.5
    order = list(range(cpb))[::-1] if rev else list(range(cpb))

    fused = prev is not None

    def body(q_ref, k_ref, v_ref, la_ref, do_ref, s_ref, *rest):
        if fused:
            pq_ref, pk_ref, pv_ref, _, w_ref, dla_ref, dst = rest
        else:
            dq_ref, dk_ref, dv_ref, dla_ref, dst = rest

        def put(kind, r, cols, val):
            if not fused:
                {'q': dq_ref, 'k': dk_ref, 'v': dv_ref}[kind][r, cols] = val
                return
            p_ref, off = {'q': (pq_ref, gv), 'k': (pk_ref, gv + gk), 'v': (pv_ref, 0)}[kind]
            w_ref[r, pl.ds(off + cols.start, cols.size)] = (val + p_ref[r, cols]).astype(w_ref.dtype)

        @pl.when(pl.program_id(0) == 0)
        def _():
            dst[...] = jnp.zeros_like(dst)
        tri = _gla_tri(rev)
        for n in range(cpb - 1, -1, -1):
            r = pl.ds(order[n] * c, c)
            for hh in range(h_):
                ck, cv = pl.ds(hh * dk, dk), pl.ds(hh * dv, dv)
                q = q_ref[r, ck].astype(F32)
                k = k_ref[r, ck].astype(F32)
                v = v_ref[r, cv]
                lmat, bend, eb, enb, ee, qi, ki, kend, att = _gla_chunk_terms(q, k, la_ref[r, ck], tri, scale)
                s_in = s_ref[n, hh]
                ds_out = dst[hh]
                dob = do_ref[r, cv]
                datt = jnp.where(tri, _dot(dob, v, NT), 0.0)
                dqi = _dot(datt, ki) + _dot(dob, s_in)
                dki = _dot(datt, qi, TN)
                put('v', r, cv, _dot(att, dob, TN) + _dot(kend, ds_out, NT))
                dkend = _dot(v, ds_out)
                gam = jnp.exp(bend)
                dgam = jnp.sum(ds_out * s_in, axis=0, keepdims=True)
                dst[hh] = gam * ds_out + _dot(dob, qi, TN)
                put('q', r, ck, dqi * (scale * eb))
                put('k', r, ck, dki * enb + dkend * ee)
                db = dqi * qi - dki * ki - dkend * kend
                dbend = jnp.sum(dkend * kend, axis=0, keepdims=True) + dgam * gam
                dla_ref[r, ck] = _split_dot(lmat, db, TN) + dbend

    def bi(j):
        return blk(nb - 1 - j)

    in_specs = [
        pl.BlockSpec((tb, gk), lambda j: (bi(j), qb)),
        pl.BlockSpec((tb, gk), lambda j: (bi(j), kb)),
        pl.BlockSpec((tb, gv), lambda j: (bi(j), vb)),
        pl.BlockSpec((tb, gk), lambda j: (bi(j), lb)),
        pl.BlockSpec((tb, gv), lambda j: (bi(j), 0)),
        pl.BlockSpec((cpb, h_, dv, dk), lambda j: (nb - 1 - j, 0, 0, 0)),
    ]
    small = pl.BlockSpec((tb, gk), lambda j: (bi(j), 0))
    wide = pl.BlockSpec((tb, gv), lambda j: (bi(j), 0))
    if not fused:
        return pl.pallas_call(
            body, name=name, grid=(nb,), in_specs=in_specs, out_specs=[small, small, wide, small],
            out_shape=[jax.ShapeDtypeStruct((dm.T, gk), F32), jax.ShapeDtypeStruct((dm.T, gk), F32),
                       jax.ShapeDtypeStruct((dm.T, gv), F32), jax.ShapeDtypeStruct((dm.T, gk), F32)],
            scratch_shapes=[pltpu.VMEM((h_, dv, dk), F32)],
            compiler_params=_cparams(("arbitrary",)),
        )(P, P, P, la, do, states)
    return pl.pallas_call(
        body, name=name, grid=(nb,),
        in_specs=in_specs + [small, small, wide, pl.BlockSpec(memory_space=pl.ANY)],
        out_specs=[pl.BlockSpec((tb, 2 * gv), lambda j: (bi(j), vb // 2)), small],
        out_shape=[jax.ShapeDtypeStruct(into.shape, into.dtype), jax.ShapeDtypeStruct((dm.T, gk), F32)],
        input_output_aliases={9: 0},
        scratch_shapes=[pltpu.VMEM((h_, dv, dk), F32)],
        compiler_params=_cparams(("arbitrary",)),
    )(P, P, P, la, do, states, *prev, into)


def _pos(n, period):
    t = lax.broadcasted_iota(jnp.int32, (n, 1), 0)
    if period & (period - 1) == 0:
        return jnp.bitwise_and(t, period - 1)
    return lax.rem(t, period)


def _conv_segments(dm):
    return [(0, dm.CTX, dm.CTX), (dm.CTX, dm.SEQ, GRID_W)]


def conv_fwd(u, w, dm, name):
    kw, cw = w.shape
    segs = _conv_segments(dm)

    def body(u_ref, w_ref, y_ref):
        for r0, n, per in segs:
            useg = u_ref[r0:r0 + n, :]
            p = _pos(n, per)
            acc = jnp.zeros_like(useg)
            for kk in range(kw):
                d = kk - kw // 2
                sh = useg if d == 0 else pltpu.roll(useg, (-d) % n, 0)
                ok = jnp.logical_and(p + d >= 0, p + d < per)
                acc = acc + jnp.where(ok, sh, 0.0) * w_ref[kk:kk + 1, :]
            y_ref[r0:r0 + n, :] = acc

    return pl.pallas_call(
        body, name=name, grid=(cw // LANES,),
        in_specs=[pl.BlockSpec((dm.T, LANES), lambda j: (0, j)), pl.BlockSpec((kw, LANES), lambda j: (0, j))],
        out_specs=pl.BlockSpec((dm.T, LANES), lambda j: (0, j)),
        out_shape=jax.ShapeDtypeStruct((dm.T, cw), F32),
        compiler_params=_cparams(("parallel",)),
    )(u, w)


def conv_bwd(u, w, dy, dm, name):
    kw, cw = w.shape
    segs = _conv_segments(dm)

    def body(u_ref, w_ref, dy_ref, du_ref, dw_ref):
        dws = [jnp.zeros((1, LANES), F32)] * kw
        for r0, n, per in segs:
            useg = u_ref[r0:r0 + n, :]
            dyseg = dy_ref[r0:r0 + n, :]
            p = _pos(n, per)
            acc = jnp.zeros_like(useg)
            for kk in range(kw):
                d = kk - kw // 2
                shu = useg if d == 0 else pltpu.roll(useg, (-d) % n, 0)
                okf = jnp.logical_and(p + d >= 0, p + d < per)
                dws[kk] = dws[kk] + jnp.sum(jnp.where(okf, shu, 0.0) * dyseg, axis=0, keepdims=True)
                shd = dyseg if d == 0 else pltpu.roll(dyseg, d % n, 0)
                okb = jnp.logical_and(p - d >= 0, p - d < per)
                acc = acc + jnp.where(okb, shd, 0.0) * w_ref[kk:kk + 1, :]
            du_ref[r0:r0 + n, :] = acc
        for kk in range(kw):
            dw_ref[kk:kk + 1, :] = dws[kk]

    return pl.pallas_call(
        body, name=name, grid=(cw // LANES,),
        in_specs=[pl.BlockSpec((dm.T, LANES), lambda j: (0, j)), pl.BlockSpec((kw, LANES), lambda j: (0, j)),
                  pl.BlockSpec((dm.T, LANES), lambda j: (0, j))],
        out_specs=[pl.BlockSpec((dm.T, LANES), lambda j: (0, j)), pl.BlockSpec((kw, LANES), lambda j: (0, j))],
        out_shape=[jax.ShapeDtypeStruct((dm.T, cw), F32), jax.ShapeDtypeStruct((kw, cw), F32)],
        compiler_params=_cparams(("parallel",)),
    )(u, w, dy)


def pool_mix(u, transpose, dm, name, into=None):
    u, uw, ublk = _rowspec(u)
    gc = dm.GC
    ng = len(POOL_WINDOWS)
    rows = dm.SEQ // GRID_W
    segs = [(0, dm.CTX, 1, dm.CTX), (dm.CTX, dm.SEQ, GRID_W, rows)]

    def one_group(u_ref, o_ref, win):
        left = win // 2
        right = win - 1 - left
        for r0, n, stride, length in segs:
            useg = u_ref[r0:r0 + n, :].astype(F32)
            t = lax.broadcasted_iota(jnp.int32, (n, 1), 0)
            p = t if stride == 1 else jnp.right_shift(t, stride.bit_length() - 1)
            cnt = (jnp.minimum(p + right + 1, length) - jnp.maximum(p - left, 0)).astype(F32)
            src = useg / cnt if transpose else useg
            acc = jnp.zeros_like(useg)
            for d in range(-left, right + 1):
                dd = -d if transpose else d
                sh = src if d == 0 else pltpu.roll(src, (-dd * stride) % n, 0)
                ok = jnp.logical_and(p + dd >= 0, p + dd < length)
                acc = acc + jnp.where(ok, sh, 0.0)
            o_ref[r0:r0 + n, :] = ((acc - useg) if transpose else (acc / cnt - useg)).astype(o_ref.dtype)

    def body(u_ref, *rest):
        o_ref = rest[-1]
        g = pl.program_id(0)
        for gi, win in enumerate(POOL_WINDOWS):
            @pl.when(g == gi)
            def _(win=win):
                one_group(u_ref, o_ref, win)

    base = ublk * (uw // gc)
    if into is None:
        obase, out_shape, more, extra, aliases = 0, jax.ShapeDtypeStruct((dm.T, ng * gc), F32), [], [], {}
    else:
        buf, oblk = into
        obase, out_shape = oblk * ng, jax.ShapeDtypeStruct(buf.shape, buf.dtype)
        more, extra, aliases = [pl.BlockSpec(memory_space=pl.ANY)], [buf], {1: 0}
    return pl.pallas_call(
        body, name=name, grid=(ng,),
        in_specs=[pl.BlockSpec((dm.T, gc), lambda g: (0, base + g))] + more,
        out_specs=pl.BlockSpec((dm.T, gc), lambda g: (0, obase + g)),
        out_shape=out_shape, input_output_aliases=aliases,
        compiler_params=_cparams(("parallel",)),
    )(u, *extra)


def loss_head(x2, target, dm, name):
    tm, d = dm.tm, dm.D
    nctx = dm.CTX // tm

    def body(x_ref, t_ref, dx_ref, l_ref):
        i = pl.program_id(0)

        @pl.when(i == 0)
        def _():
            l_ref[...] = jnp.zeros_like(l_ref)

        @pl.when(i < nctx)
        def _():
            dx_ref[...] = jnp.zeros_like(dx_ref)

        @pl.when(i >= nctx)
        def _():
            e = x_ref[...] - t_ref[...]
            dx_ref[...] = e / d
            l_ref[...] += jnp.full(l_ref.shape, 0.5 * jnp.sum(jnp.mean(e * e, axis=-1)), F32)

    return pl.pallas_call(
        body, name=name, grid=(dm.T // tm,),
        in_specs=[pl.BlockSpec((tm, d), lambda i: (i, 0)),
                  pl.BlockSpec((tm, d), lambda i: (jnp.maximum(i - nctx, 0), 0))],
        out_specs=[pl.BlockSpec((tm, d), lambda i: (i, 0)), pl.BlockSpec((8, LANES), lambda i: (0, 0))],
        out_shape=[jax.ShapeDtypeStruct((dm.T, d), F32), jax.ShapeDtypeStruct((8, LANES), F32)],
        compiler_params=_cparams(("arbitrary",)),
    )(x2, target)


def adamw(w, g, m, v, name):
    r, c = w.shape
    tr = _tile(r, tuple(t for t in (512, 256, 128, 64, 32, 16, 8) if t * c * 4 <= (1 << 20)) or (8,))

    def body(w_ref, g_ref, m_ref, v_ref, d_ref, mo_ref, vo_ref):
        gg = g_ref[...]
        mm = ADAM_B1 * m_ref[...] + (1.0 - ADAM_B1) * gg
        vv = ADAM_B2 * v_ref[...] + (1.0 - ADAM_B2) * (gg * gg)
        m_hat = mm / (1.0 - ADAM_B1 ** ADAM_STEP)
        v_hat = vv / (1.0 - ADAM_B2 ** ADAM_STEP)
        d_ref[...] = -ADAM_LR * (m_hat / (jnp.sqrt(v_hat) + ADAM_EPS) + ADAM_WD * w_ref[...])
        mo_ref[...] = mm
        vo_ref[...] = vv

    spec = pl.BlockSpec((tr, c), lambda i: (i, 0))
    return pl.pallas_call(
        body, name=name, grid=(r // tr,), in_specs=[spec] * 4, out_specs=[spec] * 3,
        out_shape=[jax.ShapeDtypeStruct((r, c), F32)] * 3,
        compiler_params=_cparams(("parallel",)),
    )(w, g, m, v)


def pair_add(g, r1, cidx, name):
    ng, r_, n_ = r1.shape
    tr = _tile(r_, tuple(t for t in (1024, 512, 256, 128, 64, 32, 16) if t * n_ * 4 <= (2 << 20)))

    def body(s_ref, g_ref, r_ref, o_ref):
        o_ref[...] = (g_ref[...].astype(F32) + r_ref[...].astype(F32)).astype(o_ref.dtype)

    return pl.pallas_call(
        body, name=name,
        grid_spec=pltpu.PrefetchScalarGridSpec(
            num_scalar_prefetch=1, grid=(ng, r_ // tr),
            in_specs=[pl.BlockSpec((None, tr, n_), lambda k, i, s: (2 * k + s[0], i, 0)),
                      pl.BlockSpec((None, tr, n_), lambda k, i, s: (k, i, 0))],
            out_specs=pl.BlockSpec((None, tr, n_), lambda k, i, s: (k, i, 0))),
        out_shape=jax.ShapeDtypeStruct((ng, r_, n_), g.dtype),
        compiler_params=_cparams(("parallel", "parallel")),
    )(cidx, g, r1)


def chip_add(h, r2, axis, where, name, slab=True):
    _, kl, nl = r2.shape
    tr = _tile(kl, tuple(t for t in (1024, 512, 256, 128, 64, 32, 16) if t * nl * 4 <= (1 << 20)))
    nrb = kl // tr

    def body(s_ref, h_ref, r_ref, o_ref):
        acc = h_ref[...].astype(F32)
        for k in range(r2.shape[0]):
            acc = acc + r_ref[k].astype(F32)
        o_ref[...] = acc

    h_map = (lambda i, s: (s[0] * nrb + i, 0)) if axis == 0 else (lambda i, s: (i, s[0]))
    if slab:
        out_spec = pl.BlockSpec((None, tr, nl), lambda i, s: (s[1], i, 0))
        out_shape = jax.ShapeDtypeStruct((2, kl, nl), F32)
    else:
        out_spec = pl.BlockSpec((tr, nl), lambda i, s: (i, 0))
        out_shape = jax.ShapeDtypeStruct((kl, nl), F32)
    return pl.pallas_call(
        body, name=name,
        grid_spec=pltpu.PrefetchScalarGridSpec(
            num_scalar_prefetch=1, grid=(nrb,),
            in_specs=[pl.BlockSpec((tr, nl), h_map),
                      pl.BlockSpec((r2.shape[0], tr, nl), lambda i, s: (0, i, 0))],
            out_specs=out_spec),
        out_shape=out_shape,
        compiler_params=_cparams(("parallel",)),
    )(where, h, r2)


def adamw_layers(w, m, v, terms0, terms1, name):
    _, a_, b_ = w.shape
    tr = _tile(a_, tuple(t for t in (512, 256, 128, 64, 32) if t * b_ * 4 <= (1 << 20)))
    by_cols = tr == a_ and a_ * b_ * 4 > (1 << 20)
    blk = (a_, LANES) if by_cols else (tr, b_)
    steps = b_ // LANES if by_cols else a_ // tr
    at = (lambda i: (0, i)) if by_cols else (lambda i: (i, 0))
    n0 = len(terms0)

    def update(g, w_ref, m_ref, v_ref, g_ref, d_ref, mo_ref, vo_ref):
        mm = ADAM_B1 * m_ref[...] + (1.0 - ADAM_B1) * g
        vv = ADAM_B2 * v_ref[...] + (1.0 - ADAM_B2) * (g * g)
        m_hat = mm / (1.0 - ADAM_B1 ** ADAM_STEP)
        v_hat = vv / (1.0 - ADAM_B2 ** ADAM_STEP)
        g_ref[...] = g
        d_ref[...] = -ADAM_LR * (m_hat / (jnp.sqrt(v_hat) + ADAM_EPS) + ADAM_WD * w_ref[...])
        mo_ref[...] = mm
        vo_ref[...] = vv

    def total(refs):
        g = refs[0][...]
        for r in refs[1:]:
            g = g + r[...]
        return g

    def body(w_ref, m_ref, v_ref, *rest):
        t_refs, outs = rest[:-4], rest[-4:]
        layer = pl.program_id(0)

        @pl.when(layer == 0)
        def _():
            update(total(t_refs[:n0]), w_ref, m_ref, v_ref, *outs)

        @pl.when(layer == 1)
        def _():
            update(total(t_refs[n0:]), w_ref, m_ref, v_ref, *outs)

    stacked = pl.BlockSpec((None,) + blk, lambda l, i: (l,) + at(i))
    return pl.pallas_call(
        body, name=name, grid=(2, steps),
        in_specs=[stacked] * 3 + [pl.BlockSpec(blk, lambda l, i: at(i * (1 - l)))] * n0
        + [pl.BlockSpec(blk, lambda l, i: at(i * l))] * len(terms1),
        out_specs=[stacked] * 4, out_shape=[jax.ShapeDtypeStruct(w.shape, F32)] * 4,
        compiler_params=_cparams(("arbitrary", "arbitrary")),
    )(w, m, v, *terms0, *terms1)


MESH = pl.DeviceIdType.MESH
ANY = pl.BlockSpec(memory_space=pl.ANY)
HBM = pl.BlockSpec(memory_space=pltpu.HBM)
SEM = pl.BlockSpec(memory_space=pltpu.SEMAPHORE)
EFFECT = pltpu.SideEffectType.DATAFLOW_SIDE_EFFECTING


def _place():
    return lax.axis_index("x"), lax.axis_index("y"), lax.axis_index("c")


def _peers(x, y):
    return [(1 - x, y), (x, 1 - y), (1 - x, 1 - y)]


def _rcopy(src, dst, ssem, rsem, dev):
    return pltpu.make_async_remote_copy(src_ref=src, dst_ref=dst, send_sem=ssem, recv_sem=rsem,
                                        device_id=dev, device_id_type=MESH)


def _gathered_shape(src, kind):
    h, a_, b_ = src.shape
    return (h, a_, 4 * b_) if kind == 'col' else (4, h, a_, b_)


def _win(ref, kind, ch, width):
    return ref.at[:, :, pl.ds(ch * width, width)] if kind == 'col' else ref.at[ch]


def _rect(ref, kind, half, ch, width):
    return ref.at[half, :, pl.ds(ch * width, width)] if kind == 'col' else ref.at[ch, half]


def _gather_plan(kinds, widths):
    def plan(src, land, x, y, c):
        chip = 2 * x + y
        out = []
        for n in range(len(src)):
            for px, py in _peers(x, y):
                out.append((src[n].at[c], _rect(land[n], kinds[n], c, chip, widths[n]), (px, py, c),
                            _rect(land[n], kinds[n], c, 2 * px + py, widths[n])))
            mine = _win(land[n], kinds[n], chip, widths[n])
            out.append((src[n], mine, (x, y, 1 - c), mine))
        return out
    return plan


def forward_halves(lands, kinds, name):
    nw = len(lands)
    widths = [t.shape[-1] // 4 if k == 'col' else t.shape[-1] for t, k in zip(lands, kinds)]

    def body(*refs):
        o = refs[nw:2 * nw]
        ssem, rsem = refs[2 * nw:]
        x, y, c = _place()
        sib = (x, y, 1 - c)
        pidx = [2 * px + py for px, py in _peers(x, y)]
        cps = [_rcopy(_rect(o[n], kinds[n], c, pidx[j], widths[n]), _rect(o[n], kinds[n], c, pidx[j], widths[n]),
                      ssem.at[3 * n + j], rsem.at[3 * n + j], sib) for n in range(nw) for j in range(3)]
        for cp in cps:
            cp.start()
        for n in range(nw):
            for j in range(3):
                cps[3 * n + j].wait_send()
                _rcopy(_rect(o[n], kinds[n], 1 - c, pidx[j], widths[n]), _rect(o[n], kinds[n], 1 - c, pidx[j], widths[n]),
                       ssem.at[3 * n + j], rsem.at[3 * n + j], sib).wait_recv()

    return pl.pallas_call(
        body, name=name, in_specs=[ANY] * nw, out_specs=[ANY] * nw,
        out_shape=[jax.ShapeDtypeStruct(t.shape, t.dtype) for t in lands],
        input_output_aliases={n: n for n in range(nw)},
        scratch_shapes=[pltpu.SemaphoreType.DMA((3 * nw,)), pltpu.SemaphoreType.DMA((3 * nw,))],
    )(*lands)


def _scatter_plan(axes, widths):
    def plan(src, land, x, y, c):
        out = []
        for n in range(len(src)):
            for k, (px, py) in enumerate(_peers(x, y)):
                ch = 2 * px + py
                view = (src[n].at[:, pl.ds(ch * widths[n], widths[n])] if axes[n] == 1
                        else src[n].at[pl.ds(ch * widths[n], widths[n]), :])
                out.append((view, land[n].at[k], (px, py, c), land[n].at[k]))
        return out
    return plan


def _exchange_plan(kinds):
    def plan(src, land, x, y, c):
        out = []
        for n in range(len(src)):
            w = land[n].shape[2]
            for j, (px, py) in enumerate(_peers(x, y)):
                ch = 2 * px + py
                view = src[n].at[:, pl.ds(ch * w, w)] if kinds[n] == 'col' else src[n].at[ch]
                out.append((view, land[n].at[j], (px, py, c), land[n].at[j]))
        return out
    return plan


def start_copies(srcs, lands, plan, ncopies, after, name):
    ns, nl = len(srcs), len(lands)

    def body(*refs):
        src, land = refs[:ns], refs[ns:ns + nl]
        ssem, rsem = refs[ns + nl + 1], refs[ns + nl + 2]
        token = refs[-1]
        x, y, c = _place()
        for k, (sv, dv, dev, _) in enumerate(plan(src, land, x, y, c)):
            _rcopy(sv, dv, ssem.at[k], rsem.at[k], dev).start()
        token[...] = jnp.zeros_like(token)

    hbm = lambda t: pltpu.HBM(t.shape, t.dtype)
    res = pl.pallas_call(
        body, name=name,
        out_shape=(pltpu.SemaphoreType.DMA((ncopies,)), pltpu.SemaphoreType.DMA((ncopies,)),
                   *[hbm(t) for t in srcs], *[hbm(t) for t in lands], jax.ShapeDtypeStruct((8, LANES), F32)),
        in_specs=[HBM] * (ns + nl) + [ANY],
        out_specs=(SEM, SEM, *[HBM] * (ns + nl), pl.BlockSpec(memory_space=pltpu.VMEM)),
        input_output_aliases={k: 2 + k for k in range(ns + nl)},
        compiler_params=pltpu.CompilerParams(has_side_effects=EFFECT),
    )(*[pltpu.with_memory_space_constraint(t, pltpu.HBM) for t in list(srcs) + list(lands)], after)
    return res[0], res[1], list(res[2:2 + ns]), list(res[2 + ns:2 + ns + nl]), res[-1]


def wait_copies(ssem, rsem, srcs, lands, plan, after, name):
    ns, nl = len(srcs), len(lands)

    def body(*refs):
        src, land = refs[:ns], refs[ns:ns + nl]
        ss, rs = refs[ns + nl], refs[ns + nl + 1]
        x, y, c = _place()
        for k, (sv, dv, dev, mine) in enumerate(plan(src, land, x, y, c)):
            cp = _rcopy(sv, mine, ss.at[k], rs.at[k], dev)
            cp.wait_send()
            cp.wait_recv()

    hbm = lambda t: pltpu.HBM(t.shape, t.dtype)
    res = pl.pallas_call(
        body, name=name,
        out_shape=(*[hbm(t) for t in srcs], *[hbm(t) for t in lands]),
        in_specs=[HBM] * (ns + nl) + [SEM, SEM, ANY], out_specs=tuple([HBM] * (ns + nl)),
        input_output_aliases={k: k for k in range(ns + nl)},
        compiler_params=pltpu.CompilerParams(has_side_effects=EFFECT),
    )(*srcs, *lands, ssem, rsem, after)
    return list(res[ns:])


def pair_swap_halves(gs, kinds, name):
    nw = len(gs)

    def other(ref, kind, half):
        return ref.at[half] if kind == 'col' else ref.at[:, half]

    def body(*refs):
        g, o = refs[:nw], refs[nw:2 * nw]
        ssem, rsem = refs[2 * nw:]
        x, y, c = _place()
        cps = [_rcopy(other(g[n], kinds[n], 1 - c), o[n], ssem.at[n], rsem.at[n], (x, y, 1 - c)) for n in range(nw)]
        for cp in cps:
            cp.start()
        for cp in cps:
            cp.wait()

    return pl.pallas_call(
        body, name=name, in_specs=[ANY] * nw, out_specs=[ANY] * nw,
        out_shape=[jax.ShapeDtypeStruct(g.shape[1:] if k == 'col' else (g.shape[0],) + g.shape[2:], g.dtype)
                   for g, k in zip(gs, kinds)],
        scratch_shapes=[pltpu.SemaphoreType.DMA((nw,)), pltpu.SemaphoreType.DMA((nw,))],
    )(*gs)


def pair_swap(fs, name):
    nw = len(fs)

    def body(*refs):
        f, o = refs[:nw], refs[nw:2 * nw]
        ssem, rsem = refs[2 * nw:]
        x, y, c = _place()
        cps = [_rcopy(f[n], o[n], ssem.at[n], rsem.at[n], (x, y, 1 - c)) for n in range(nw)]
        for cp in cps:
            cp.start()
        for cp in cps:
            cp.wait()

    return pl.pallas_call(
        body, name=name, in_specs=[ANY] * nw, out_specs=[ANY] * nw,
        out_shape=[jax.ShapeDtypeStruct(f.shape, f.dtype) for f in fs],
        scratch_shapes=[pltpu.SemaphoreType.DMA((nw,)), pltpu.SemaphoreType.DMA((nw,))],
    )(*fs)


def chip_broadcast(h, name):
    def body(h_ref, o_ref, ssem, rsem):
        x, y, c = _place()
        cps = [_rcopy(h_ref, o_ref.at[j], ssem.at[j], rsem.at[j], (px, py, c)) for j, (px, py) in enumerate(_peers(x, y))]
        for cp in cps:
            cp.start()
        for cp in cps:
            cp.wait()

    return pl.pallas_call(
        body, name=name, in_specs=[ANY], out_specs=ANY,
        out_shape=jax.ShapeDtypeStruct((3,) + h.shape, h.dtype),
        scratch_shapes=[pltpu.SemaphoreType.DMA((3,)), pltpu.SemaphoreType.DMA((3,))],
    )(h)


def quad_sum(h, r, cidx, name):
    r_, c_ = h.shape
    tr = _tile(r_, (512, 256, 128, 64, 32, 16, 8))

    def body(s_ref, h_ref, r_ref, o_ref):
        o_ref[...] = (h_ref[...] + r_ref[2]) + (r_ref[0] + r_ref[1])

    return pl.pallas_call(
        body, name=name,
        grid_spec=pltpu.PrefetchScalarGridSpec(
            num_scalar_prefetch=1, grid=(r_ // tr,),
            in_specs=[pl.BlockSpec((tr, c_), lambda i, s: (i, 0)), pl.BlockSpec((3, tr, c_), lambda i, s: (0, i, 0))],
            out_specs=pl.BlockSpec((None, tr, c_), lambda i, s: (s[0], i, 0))),
        out_shape=jax.ShapeDtypeStruct((2, r_, c_), F32),
        compiler_params=_cparams(("parallel",)),
    )(cidx, h, r)


def pair_join_layers(fs, name):
    nw = len(fs)

    def body(*refs):
        o = refs[nw:2 * nw]
        ssem, rsem = refs[2 * nw:]
        x, y, c = _place()
        sib = (x, y, 1 - c)
        cps = [_rcopy(o[n].at[c], o[n].at[c], ssem.at[n], rsem.at[n], sib) for n in range(nw)]
        for cp in cps:
            cp.start()
        for n in range(nw):
            cps[n].wait_send()
            _rcopy(o[n].at[1 - c], o[n].at[1 - c], ssem.at[n], rsem.at[n], sib).wait_recv()

    return pl.pallas_call(
        body, name=name, in_specs=[ANY] * nw, out_specs=[ANY] * nw,
        out_shape=[jax.ShapeDtypeStruct(f.shape, f.dtype) for f in fs],
        input_output_aliases={n: n for n in range(nw)},
        scratch_shapes=[pltpu.SemaphoreType.DMA((nw,)), pltpu.SemaphoreType.DMA((nw,))],
    )(*fs)


def _flatten_pad(parts, dtype):
    flat = jnp.concatenate([p.reshape(-1).astype(dtype) for p in parts])
    q = 512 * LANES
    n = -(-flat.shape[0] // q) * q
    return jnp.pad(flat, (0, n - flat.shape[0])).reshape(n // LANES, LANES)


def _lane_pad(n):
    return -(-n // LANES) * LANES


def _in_proj_layout(d):
    gk, gv, cw, pw = d // 2, d, d // 2, d // 2
    own = [('q', gk), ('k', gk), ('v', gv), ('og', gv), ('lrf', GLA_LR), ('lrb', GLA_LR), ('ga', cw), ('gb', cw),
           ('pu', pw), ('mg', 3 * d)]
    padded = [('mg', 3 * d), ('og', gv), ('v', gv), ('q', gk), ('k', gk), ('ga', cw), ('gb', cw), ('pu', pw),
              ('lrf', GLA_LR), ('lrb', GLA_LR), ('pad', d // 2 - 2 * GLA_LR)]
    return own, padded


def _row_pieces(src, lo, hi, wl, wlp):
    out = []
    for k in range(4):
        s0, s1 = max(lo, k * wl), min(hi, (k + 1) * wl)
        if s0 < s1:
            out.append(src[k * wlp + s0 - k * wl:k * wlp + s1 - k * wl])
    return out


def _w_in_t_to_proj(g, d, wl, wlp):
    own, padded = _in_proj_layout(d)
    at, start = {}, 0
    for n, wd in own:
        at[n] = (start, start + wd)
        start += wd
    parts = []
    for n, wd in padded:
        parts += [jnp.zeros((wd, g.shape[1]), g.dtype)] if n == 'pad' else _row_pieces(g, *at[n], wl, wlp)
    return jnp.concatenate(parts, axis=0)


def _proj_to_w_in_t(gp, d, wl, wlp):
    own, padded = _in_proj_layout(d)
    pat, start = {}, 0
    for n, wd in padded:
        pat[n] = start
        start += wd
    parts = []
    for k in range(4):
        start = 0
        for n, wd in own:
            s0, s1 = max(start, k * wl), min(start + wd, (k + 1) * wl)
            if s0 < s1:
                parts.append(gp[pat[n] + s0 - start:pat[n] + s1 - start])
            start += wd
        parts.append(jnp.zeros((wlp - wl, gp.shape[1]), gp.dtype))
    return jnp.concatenate(parts, axis=0)


def _silu_grad(z):
    s = jax.nn.sigmoid(z)
    return s + z * s * (1.0 - s)


def kernel(x, c, ctx, c_ctx, w_ada, b_ada, g_pre_mix, g_post_mix, g_pre_mlp, g_post_mlp, w_in, w_decay, b_decay, g_gla, w_gla_o, w_dw, b_dw, g_conv_ln, b_conv_ln, w_conv_o, w_pool_g, s_pool, w_pool_o, b_gate, w_out, w_mlp1, w_mlp2, loss_target, m_c_ctx, m_w_ada, m_b_ada, m_g_pre_mix, m_g_post_mix, m_g_pre_mlp, m_g_post_mlp, m_w_in, m_w_decay, m_b_decay, m_g_gla, m_w_gla_o, m_w_dw, m_b_dw, m_g_conv_ln, m_b_conv_ln, m_w_conv_o, m_w_pool_g, m_s_pool, m_w_pool_o, m_b_gate, m_w_out, m_w_mlp1, m_w_mlp2, v_c_ctx, v_w_ada, v_b_ada, v_g_pre_mix, v_g_post_mix, v_g_pre_mlp, v_g_post_mlp, v_w_in, v_w_decay, v_b_decay, v_g_gla, v_w_gla_o, v_w_dw, v_b_dw, v_g_conv_ln, v_b_conv_ln, v_w_conv_o, v_w_pool_g, v_s_pool, v_w_pool_o, v_b_gate, v_w_out, v_w_mlp1, v_w_mlp2):
    a = dict(locals())
    for n in ('w_in', 'm_w_in', 'v_w_in'):
        a[n] = jnp.swapaxes(a[n], 1, 2)
    big_axis = dict(BIG, w_in=1)
    depth = w_in.shape[0]
    d = x.shape[-1]
    seq, nctx_rows = x.shape[1], ctx.shape[1]
    dm = types.SimpleNamespace(
        D=d, SEQ=seq, CTX=nctx_rows, T=seq + nctx_rows, DK=d // 8, DV=d // 4, GK=d // 2, GC=d // 8,
        tm=_tile(nctx_rows, (256, 128, 64)), TB=_tile(nctx_rows, (256, 128, 64)))
    assert dm.SEQ % dm.tm == 0 and dm.SEQ % GRID_W == 0 and dm.CTX % GLA_CHUNK == 0
    tmw = min(dm.tm, 128)
    chip = 2 * lax.axis_index("x") + lax.axis_index("y")
    core = lax.axis_index("c")
    chip1 = chip.astype(jnp.int32).reshape(1)
    core1 = core.astype(jnp.int32).reshape(1)

    big_names, small_names = list(BIG), list(SMALL_SHARDED)
    nbig = len(big_names)
    kinds = ['col' if big_axis[n] == 2 else 'row' for n in big_names]
    wl = w_in.shape[2]
    wlp = _lane_pad(wl)

    def rows8(t):
        t = t.reshape(t.shape[0], -1, t.shape[-1])
        return jnp.pad(t, ((0, 0), (0, -t.shape[1] % 8), (0, 0)))

    def halves(t):
        return t.reshape(2, t.shape[0] // 2, t.shape[1])

    def layer_src(l, tok=None):
        def one(n):
            t = a[n][l] if tok is None else a[n][l] + tok
            return halves((jnp.pad(t, ((0, wlp - wl), (0, 0))) if n == 'w_in' else t).astype(MM_DTYPE))
        return [one(n) for n in big_names]

    def whole(t):
        return t.reshape(-1, t.shape[-1])

    def start_gather(srcs, knds, after, name):
        plan = _gather_plan(knds, [t.shape[2] for t in srcs])
        lands = [lax.empty(_gathered_shape(t, k), t.dtype) for t, k in zip(srcs, knds)]
        return (plan,) + start_copies(srcs, lands, plan, 4 * len(srcs), after, name)

    late = [big_names.index(n) for n in ('w_gla_o', 'w_conv_o', 'w_pool_o', 'w_out', 'w_mlp1', 'w_mlp2')]
    early = [k for k in range(nbig) if k not in late]
    src0 = layer_src(0)
    kinds_e = [kinds[k] for k in early] + ['col'] * len(small_names)
    age = start_gather([src0[k] for k in early] + [rows8(a[n]) for n in small_names], kinds_e, src0[early[0]],
                       "gather_layer0_start")
    tok0 = age[-1][0, 0]
    src1 = layer_src(1, tok0)
    pk = lambda pre: _flatten_pad([a[pre + n] for n in SMALL], F32) + tok0
    small_w, small_m, small_v = pk(''), pk('m_'), pk('v_')
    X = jnp.concatenate([ctx[0], x[0]], axis=0) + tok0
    ready = (small_w[0, 0] + small_m[0, 0] + small_v[0, 0] + X[0, 0]
             + sum(t[0, 0, 0].astype(F32) for t in src1)).reshape(1, 1)
    g0 = wait_copies(age[1], age[2], age[3], age[4], age[0], ready, "gather_layer0_wait")
    g0 = forward_halves(g0, kinds_e, "gather_layer0_forward")
    ag0 = start_gather([src0[k] for k in late], [kinds[k] for k in late], g0[0], "gather_layer0_late_start")
    ag1 = start_gather(src1, kinds, ag0[-1], "gather_layer1_start")
    ag_token = ag1[-1]
    full = {n: [None, None] for n in big_names}
    for k, t in zip(early, g0):
        full[big_names[k]][0] = whole(t)
    for n, g in zip(small_names, g0[len(early):]):
        shp = a[n].shape
        full[n] = g[:, :math.prod(shp[1:-1])].reshape(shp[:-1] + (4 * shp[-1],))
    for n in SMALL:
        if n not in SMALL_SHARDED:
            full[n] = a[n]

    cvec = jnp.concatenate([c_ctx.reshape(1, d), c.reshape(1, d), jnp.zeros((6, d), F32)], axis=0)
    avec = (cvec * jax.nn.sigmoid(cvec) + ag_token[0, 0]).astype(MM_DTYPE)

    def row(v):
        return v.reshape(1, -1)

    saved = []
    gk, gv = dm.GK, d
    lrblk = (7 * d + d // 2) // LANES
    for l in range(depth):
        if l == 1:
            got = wait_copies(ag1[1], ag1[2], ag1[3], ag1[4], ag1[0], X, "gather_layer1_wait")
            got = forward_halves(got, kinds, "gather_layer1_forward")
            for n, t in zip(big_names, got):
                full[n][1] = whole(t)
        s = types.SimpleNamespace()
        s.w_in_p = _w_in_t_to_proj(full['w_in'][l], d, wl, wlp)
        wd = full['w_decay'][l]
        wdp = jnp.zeros((LANES, 2 * gk), F32)
        wdp = wdp.at[:GLA_LR, :gk].set(wd[0]).at[GLA_LR:2 * GLA_LR, gk:].set(wd[1])
        s.wdp = wdp.astype(MM_DTYPE)
        s.wdp_wide = jnp.pad(s.wdp, ((0, d // 2 - LANES), (0, 0)))
        s.bd = full['b_decay'][l].reshape(1, 2 * gk)
        modraw = matmul(avec, full['w_ada'][l], 'nn', F32, f"mod_{l}") + full['b_ada'][l][None, :]
        s.mod = [modraw[0:2, j * d:(j + 1) * d].reshape(2, 1, d) for j in range(6)]
        s.x = X
        (s.h,) = rowwise(pre_fn, [X], s.mod[0:2], [row(g_pre_mix[l])], [(d, MM_DTYPE)], dm, f"pre_{l}")
        s.P = matmul(s.h, s.w_in_p, 'nt', MM_DTYPE, f"in_proj_{l}")
        P = s.P
        s.z = matmul((P, LANES, lrblk), s.wdp, 'nn', F32, f"decay_proj_{l}", tk=LANES)
        la_f, la_b = rowwise(decay_fn, [s.z], [], [s.bd], [(gk, F32), (gk, F32)], dm, f"decay_{l}")
        s.la = jnp.concatenate([la_f, la_b], axis=1)
        s.o_f, s.st_f = gla_fwd(P, s.la, False, dm, f"gla_fwd_f_{l}")
        s.o_b, s.st_b = gla_fwd(P, s.la, True, dm, f"gla_fwd_b_{l}")
        (s.gin,) = rowwise(glaout_fn, [s.o_f, s.o_b, (P, d, 3)], [], [row(g_gla[l])], [(gv, MM_DTYPE)], dm,
                           f"gla_out_{l}")
        if l == 0:
            got = wait_copies(ag0[1], ag0[2], ag0[3], ag0[4], ag0[0], s.gin, "gather_layer0_late_wait")
            got = forward_halves(got, [kinds[k] for k in late], "gather_layer0_late_forward")
            for k, t in zip(late, got):
                full[big_names[k]][0] = whole(t)
        s.ya = matmul(s.gin, full['w_gla_o'][l], 'nn', MM_DTYPE, f"gla_o_{l}")
        (s.u,) = rowwise(glu_fn, [(P, d, 6)], [], [], [(d // 2, F32)], dm, f"glu_{l}")
        s.yconv = conv_fwd(s.u, full['w_dw'][l], dm, f"conv_{l}")
        (s.cin,) = rowwise(convpost_fn, [s.yconv], [], [row(b_dw[l]), row(g_conv_ln[l]), row(b_conv_ln[l])],
                           [(d // 2, MM_DTYPE)], dm, f"conv_post_{l}")
        s.yb = matmul(s.cin, full['w_conv_o'][l], 'nn', MM_DTYPE, f"conv_o_{l}")
        s.pm = pool_mix((P, d // 2, 14), False, dm, f"pool_mix_{l}")
        s.pc = group_mm(s.pm, w_pool_g[l], 'nn', F32, f"pool_g_{l}")
        (s.pin,) = rowwise(poolpost_fn, [s.pc], [], [row(s_pool[l])], [(d // 2, MM_DTYPE)], dm, f"pool_post_{l}")
        s.yc = matmul(s.pin, full['w_pool_o'][l], 'nn', MM_DTYPE, f"pool_o_{l}")
        s.bg = [row(full['b_gate'][l][j]) for j in range(3)]
        (s.mixed,) = rowwise(merge_fn, [s.ya, s.yb, s.yc, (P, 3 * d, 0)], [], s.bg, [(d, MM_DTYPE)], dm,
                             f"merge_{l}", tm=tmw)
        s.y = matmul(s.mixed, full['w_out'][l], 'nn', MM_DTYPE, f"out_proj_{l}")
        s.x1, s.h2 = rowwise(mid_fn, [X, s.y], s.mod[2:5], [row(g_post_mix[l]), row(g_pre_mlp[l])],
                             [(d, F32), (d, MM_DTYPE)], dm, f"mid_{l}")
        s.act = matmul(s.h2, full['w_mlp1'][l], 'nn', MM_DTYPE, f"mlp1_{l}", epi=relu2_epi)
        s.y2 = matmul(s.act, full['w_mlp2'][l], 'nn', MM_DTYPE, f"mlp2_{l}")
        (X,) = rowwise(post_fn, [s.x1, s.y2], s.mod[5:6], [row(g_post_mlp[l])], [(d, F32)], dm, f"post_{l}")
        saved.append(s)

    dX, lossv = loss_head(X, loss_target[0], dm, "loss_head")
    loss = lax.psum(lossv[0, 0], ("x", "y", "c"))

    grads = {n: [None] * depth for n in WEIGHTS if n != 'c_ctx' and n not in BIG}
    gbig = {n: [None] * depth for n in BIG}
    rs_token = None

    def start_scatter(idx, layer, after, name):
        gs = [gbig[big_names[k]][layer] for k in idx]
        wd = [t.shape[1] // 4 if kinds[k] == 'col' else t.shape[0] // 4 for t, k in zip(gs, idx)]
        plan = _scatter_plan([big_axis[big_names[k]] - 1 for k in idx], wd)
        lands = [lax.empty((3, t.shape[0], w) if kinds[k] == 'col' else (3, w, t.shape[1]), t.dtype)
                 for t, w, k in zip(gs, wd, idx)]
        return (plan,) + start_copies(gs, lands, plan, 3 * len(gs), after, name)

    g_cctx = jnp.zeros((d,), F32)
    for l in reversed(range(depth)):
        s = saved[l]
        P = s.P
        dmod = [None] * 6
        gpm = row(g_post_mlp[l]) if rs_token is None else row(g_post_mlp[l]) + rs_token[0, 0]
        (dx1, dy2), (dmod[5],), (dg,) = rowwise_vjp(post_fn, [s.x1, s.y2], s.mod[5:6], [gpm], [dX],
                                                     dm, f"post_bwd_{l}", narrow=(1,))
        grads['g_post_mlp'][l] = dg[0]
        du1 = matmul(dy2, full['w_mlp2'][l], 'nt', MM_DTYPE, f"mlp2_dx_{l}", epi=relu2_bwd_epi, extras=[s.act])
        gbig['w_mlp2'][l] = matmul(s.act, dy2, 'tn', MM_DTYPE, f"mlp2_dw_{l}")
        dh2 = matmul(du1, full['w_mlp1'][l], 'nt', MM_DTYPE, f"mlp1_dx_{l}")
        gbig['w_mlp1'][l] = matmul(s.h2, du1, 'tn', MM_DTYPE, f"mlp1_dw_{l}")
        gpx = row(g_post_mix[l])
        (dxa, dy), dmod[2:5], (dg1, dg2) = rowwise_vjp(
            mid_fn, [s.x, s.y], s.mod[2:5], [gpx, row(g_pre_mlp[l])], [dx1, dh2], dm, f"mid_bwd_{l}", narrow=(1,))
        grads['g_post_mix'][l], grads['g_pre_mlp'][l] = dg1[0], dg2[0]
        dmixed = matmul(dy, full['w_out'][l], 'nt', MM_DTYPE, f"out_proj_dx_{l}")
        gbig['w_out'][l] = matmul(s.mixed, dy, 'tn', MM_DTYPE, f"out_proj_dw_{l}")
        (dya, dyb, dyc, dP), _, dbg = rowwise_vjp(merge_fn, [s.ya, s.yb, s.yc, (P, 3 * d, 0)], [], s.bg, [dmixed],
                                                  dm, f"merge_bwd_{l}", tm=tmw, narrow=(0, 1, 2),
                                                  into=(3, None, P.shape))
        grads['b_gate'][l] = jnp.concatenate(dbg, axis=0)
        dgin = matmul(dya, full['w_gla_o'][l], 'nt', MM_DTYPE, f"gla_o_dx_{l}")
        gbig['w_gla_o'][l] = matmul(s.gin, dya, 'tn', MM_DTYPE, f"gla_o_dw_{l}")
        dcin = matmul(dyb, full['w_conv_o'][l], 'nt', MM_DTYPE, f"conv_o_dx_{l}")
        gbig['w_conv_o'][l] = matmul(s.cin, dyb, 'tn', MM_DTYPE, f"conv_o_dw_{l}")
        dpin = matmul(dyc, full['w_pool_o'][l], 'nt', MM_DTYPE, f"pool_o_dx_{l}")
        gbig['w_pool_o'][l] = matmul(s.pin, dyc, 'tn', MM_DTYPE, f"pool_o_dw_{l}")
        sp = row(s_pool[l])
        if l == 0:
            rs0 = start_scatter(late, 0, dpin, "grad_layer0_late_start")
            sp = sp + rs0[-1][0, 0]
        (dpc,), _, (dsp,) = rowwise_vjp(poolpost_fn, [s.pc], [], [sp], [dpin], dm, f"pool_post_bwd_{l}")
        grads['s_pool'][l] = dsp[0]
        grads['w_pool_g'][l] = group_mm(s.pm, w_pool_g[l], 'tn', F32, f"pool_g_dw_{l}", b=dpc)
        dpm = group_mm(dpc, w_pool_g[l], 'nt', F32, f"pool_g_dx_{l}")
        dP = pool_mix(dpm, True, dm, f"pool_mix_bwd_{l}", into=(dP, 14))
        (dyconv,), _, (dbdw, dgln, dbln) = rowwise_vjp(
            convpost_fn, [s.yconv], [], [row(b_dw[l]), row(g_conv_ln[l]), row(b_conv_ln[l])], [dcin], dm,
            f"conv_post_bwd_{l}")
        grads['b_dw'][l], grads['g_conv_ln'][l], grads['b_conv_ln'][l] = dbdw[0], dgln[0], dbln[0]
        du, grads['w_dw'][l] = conv_bwd(s.u, full['w_dw'][l], dyconv, dm, f"conv_bwd_{l}")
        (dP,), _, _ = rowwise_vjp(glu_fn, [(P, d, 6)], [], [], [du], dm, f"glu_bwd_{l}", into=(0, dP, P.shape))
        (do, _, dP), _, (dgg,) = rowwise_vjp(glaout_fn, [s.o_f, s.o_b, (P, d, 3)], [], [row(g_gla[l])], [dgin], dm,
                                             f"gla_out_bwd_{l}", want=[True, False, True], into=(2, dP, P.shape))
        grads['g_gla'][l] = dgg[0]
        dqf, dkf, dvf, dlaf = gla_bwd(P, s.la, do, s.st_f, False, dm, f"gla_bwd_f_{l}")
        dP, dlab = gla_bwd(P, s.la, do, s.st_b, True, dm, f"gla_bwd_b_{l}", prev=(dqf, dkf, dvf), into=dP)
        (dz,), _, (dbd,) = rowwise_vjp(decay_fn, [s.z], [], [s.bd], [dlaf, dlab], dm, f"decay_bwd_{l}", narrow=(0,))
        grads['b_decay'][l] = dbd.reshape(2, gk)
        dwdp = matmul((P, LANES, lrblk), dz, 'tn', F32, f"decay_proj_dw_{l}", tm=LANES)
        grads['w_decay'][l] = jnp.stack([dwdp[:GLA_LR, :gk], dwdp[GLA_LR:2 * GLA_LR, gk:]])
        dP = matmul(dz, s.wdp_wide, 'nt', MM_DTYPE, f"decay_proj_dx_{l}", into=(dP, 15))
        dh = matmul(dP, s.w_in_p, 'nn', MM_DTYPE, f"in_proj_dx_{l}")
        gbig['w_in'][l] = _proj_to_w_in_t(matmul(dP, s.h, 'tn', MM_DTYPE, f"in_proj_dw_{l}"), d, wl, wlp)
        (dX,), dmod[0:2], (dg,) = rowwise_vjp(pre_fn, [s.x], s.mod[0:2], [row(g_pre_mix[l])], [dh], dm,
                                               f"pre_bwd_{l}", adds={0: dxa})
        grads['g_pre_mix'][l] = dg[0]
        dmodflat = jnp.concatenate([jnp.concatenate([m_.reshape(2, d) for m_ in dmod], axis=1),
                                    jnp.zeros((6, 6 * d), F32)], axis=0)
        grads['b_ada'][l] = dmodflat[0] + dmodflat[1]
        gbig['w_ada'][l] = matmul(avec, dmodflat, 'tn', MM_DTYPE, f"ada_dw_{l}")
        dav = matmul(dmodflat, full['w_ada'][l], 'nt', F32, f"ada_dx_{l}")
        g_cctx = g_cctx + dav[0] * _silu_grad(c_ctx)
        if l == 1:
            rs1 = start_scatter(list(range(nbig)), 1, dav, "grad_layer1_start")
            rs_token = rs1[-1]

    grad_x = dX[dm.CTX:][None]
    gfull = {n: jnp.stack(v) for n, v in grads.items()}
    gfull['c_ctx'] = g_cctx
    where = jnp.concatenate([chip1, core1])

    def halves_view(t, k):
        return t.reshape(2, t.shape[0] // 2, t.shape[1]) if k == 'col' else t.reshape(4, 2, t.shape[0] // 8, t.shape[1])
    enames = [big_names[k] for k in early]
    ekinds = [kinds[k] for k in early]
    v0 = [halves_view(gbig[n][0], k) for n, k in zip(enames, ekinds)]
    r1 = pair_swap_halves(v0, ekinds, "grad_pair_swap")
    hs = [pair_add(v.reshape((-1,) + v.shape[-2:]), r.reshape((-1,) + r.shape[-2:]), core1, f"grad_pair_add_{n}")
          for n, v, r in zip(enames, v0, r1)]
    hx = [h.reshape(h.shape[1:]) if k == 'col' else h for h, k in zip(hs, ekinds)]
    ex_plan = _exchange_plan(ekinds)
    ex_lands = [lax.empty((3, h.shape[0], h.shape[1] // 4) if k == 'col' else (3,) + h.shape[1:], h.dtype)
                for h, k in zip(hx, ekinds)]
    ex = (ex_plan,) + start_copies(hx, ex_lands, ex_plan, 3 * len(hx), hx[0], "grad_chip_exchange_start")

    got0 = wait_copies(rs0[1], rs0[2], rs0[3], rs0[4], rs0[0], ex[-1], "grad_layer0_late_wait")
    got1 = wait_copies(rs1[1], rs1[2], rs1[3], rs1[4], rs1[0], ex[-1], "grad_layer1_wait")
    sa = [chip_add(g, r, big_axis[big_names[k]] - 1, where, f"grad_layer0_add_{big_names[k]}", slab=False)
          for k, g, r in zip(late, rs0[3], got0)]
    sa += [chip_add(g, r, big_axis[n] - 1, where, f"grad_layer1_add_{n}", slab=False)
           for n, g, r in zip(big_names, rs1[3], got1)]
    sb = pair_swap(sa, "grad_late_pair_swap")
    red0 = {big_names[k]: [sa[j], sb[j]] for j, k in enumerate(late)}
    red1 = {n: [sa[len(late) + k], sb[len(late) + k]] for k, n in enumerate(big_names)}

    sflat = _flatten_pad([gfull[n].astype(F32) for n in SMALL], F32)
    sv = sflat.reshape(2, sflat.shape[0] // 2, LANES)
    (sr,) = pair_swap_halves([sv], ['col'], "small_grad_pair_swap")
    sh = pair_add(sv, sr[None], core1, "small_grad_pair_add")[0]
    sq = quad_sum(sh, chip_broadcast(sh, "small_grad_chip_exchange"), core1, "small_grad_chip_sum")
    (ssum,) = pair_join_layers([sq], "small_grad_pair_join")
    ssum = ssum.reshape(-1)

    out_g, out_d, out_m, out_v = {}, {}, {}, {}

    def update_big(n, terms0):
        out_g[n], out_d[n], out_m[n], out_v[n] = adamw_layers(a[n], a['m_' + n], a['v_' + n], terms0, red1[n],
                                                              f"adamw_{n}")
    for k in late:
        update_big(big_names[k], red0[big_names[k]])
    start = 0
    sg = {}
    for n in SMALL:
        cnt = gfull[n].size
        g = ssum[start:start + cnt].reshape(gfull[n].shape)
        start += cnt
        if n in SMALL_SHARDED:
            ax = SMALL_SHARDED[n]
            wdt = a[n].shape[ax]
            g = lax.dynamic_slice_in_dim(g, chip * wdt, wdt, axis=ax)
        sg[n] = g
    gs = _flatten_pad([sg[n] for n in SMALL], F32)
    dl, mn, vn = adamw(small_w, gs, small_m, small_v, "adamw_small")
    done = (dl[0, 0] + sum(out_d[big_names[k]][0, 0, 0] for k in late)).reshape(1, 1)
    r2 = wait_copies(ex[1], ex[2], ex[3], ex[4], ex[0], done, "grad_chip_exchange_wait")
    dl, mn, vn = dl.reshape(-1), mn.reshape(-1), vn.reshape(-1)
    start = 0
    for n in SMALL:
        cnt, shp = a[n].size, a[n].shape
        out_g[n] = sg[n]
        out_d[n], out_m[n], out_v[n] = (t[start:start + cnt].reshape(shp) for t in (dl, mn, vn))
        start += cnt
    fs = [chip_add(h.reshape(-1, h.shape[-1]), r, big_axis[n] - 1, where, f"grad_chip_add_{n}")
          for n, h, r in zip(enames, ex[3], r2)]
    for n, t in zip(enames, pair_join_layers(fs, "grad_pair_join")):
        update_big(n, [t.reshape(-1, t.shape[-1])])
    for dct in (out_g, out_d, out_m, out_v):
        dct['w_in'] = jnp.swapaxes(dct['w_in'], 1, 2)
    return (loss, grad_x, *[out_g[n] for n in WEIGHTS], *[out_d[n] for n in WEIGHTS],
            *[out_m[n] for n in WEIGHTS], *[out_v[n] for n in WEIGHTS])
```

```python
import functools
import math
import types

import jax
import jax.numpy as jnp
from jax import lax
from jax.experimental import pallas as pl
from jax.experimental.pallas import tpu as pltpu

F32 = jnp.float32
MM_DTYPE = jnp.bfloat16
VMEM_LIMIT_V7X = 56 * 1024 * 1024
LANES = 128
EPS = 1e-6

N_HEADS = 4
GLA_CHUNK = 64
GLA_TAU = 16.0
GLA_LR = 16
GRID_W = 64
POOL_WINDOWS = (2, 4, 8, 16)

ADAM_LR = 0.001
ADAM_B1 = 0.9
ADAM_B2 = 0.999
ADAM_EPS = 1e-08
ADAM_WD = 0.01
ADAM_STEP = 10

NN = (((1,), (0,)), ((), ()))
NT = (((1,), (1,)), ((), ()))
TN = (((0,), (0,)), ((), ()))

WEIGHTS = ['c_ctx', 'w_ada', 'b_ada', 'g_pre_mix', 'g_post_mix', 'g_pre_mlp', 'g_post_mlp', 'w_in', 'w_decay',
           'b_decay', 'g_gla', 'w_gla_o', 'w_dw', 'b_dw', 'g_conv_ln', 'b_conv_ln', 'w_conv_o', 'w_pool_g',
           's_pool', 'w_pool_o', 'b_gate', 'w_out', 'w_mlp1', 'w_mlp2']
BIG = {'w_ada': 2, 'w_in': 2, 'w_gla_o': 1, 'w_conv_o': 2, 'w_pool_o': 2, 'w_out': 1, 'w_mlp1': 2, 'w_mlp2': 1}
SMALL_SHARDED = {'w_decay': 3, 'b_decay': 2, 'w_dw': 2, 'b_gate': 2}
SMALL = [n for n in WEIGHTS if n not in BIG]


def _tile(n, prefs):
    for t in prefs:
        if n % t == 0:
            return t
    return n


def _cparams(sem=None, **kw):
    return pltpu.CompilerParams(dimension_semantics=sem, vmem_limit_bytes=VMEM_LIMIT_V7X, **kw)


def _dot(a, b, dims=NN):
    return lax.dot_general(a.astype(MM_DTYPE), b.astype(MM_DTYPE), dims, preferred_element_type=F32)


def matmul(a, b, mode, out_dtype, name, tm=None, tn=None, tk=None, epi=None, extras=(), into=None):
    a, aw, ablk = a if isinstance(a, tuple) else (a, a.shape[1], 0)
    if mode == 'nn':
        M, K, N = a.shape[0], aw, b.shape[1]
    elif mode == 'nt':
        M, K, N = a.shape[0], aw, b.shape[0]
    else:
        K, M, N = a.shape[0], aw, b.shape[1]
    big = (1088, 1024, 640, 544, 512, 320, 256, 128, 64, 32, 16, 8)
    if mode == 'tn':
        tm = tm or _tile(M, (1024, 512, 256, 128))
        tn = tn or _tile(N, (1024, 512, 256, 128))
        tk = tk or _tile(K, big)
    else:
        tm = tm or _tile(M, big)
        tn = tn or _tile(N, (1024, 512, 256, 128))
        tk = tk or _tile(K, (1024, 512, 256, 128))
    if aw != a.shape[1]:
        assert (mode == 'tn' and tm == aw) or (mode != 'tn' and tk == aw)
    nk = K // tk
    ne = len(extras)
    dims = {'nn': NN, 'nt': NT, 'tn': TN}[mode]

    def body(a_ref, b_ref, *rest):
        e_refs, o_ref = rest[:ne], rest[ne + (into is not None)]

        def finish(acc):
            if epi is not None:
                acc = epi(acc, *[e[...] for e in e_refs])
            o_ref[...] = acc.astype(o_ref.dtype)

        p = _dot(a_ref[...], b_ref[...], dims)
        if nk == 1:
            finish(p)
            return
        acc = rest[-1]
        k = pl.program_id(2)

        @pl.when(k == 0)
        def _():
            acc[...] = p

        @pl.when(k > 0)
        def _():
            acc[...] += p

        @pl.when(k == nk - 1)
        def _():
            finish(acc[...])

    if mode == 'nn':
        a_spec = pl.BlockSpec((tm, tk), lambda i, j, k: (i, k + ablk))
        b_spec = pl.BlockSpec((tk, tn), lambda i, j, k: (k, j))
    elif mode == 'nt':
        a_spec = pl.BlockSpec((tm, tk), lambda i, j, k: (i, k + ablk))
        b_spec = pl.BlockSpec((tn, tk), lambda i, j, k: (j, k))
    else:
        a_spec = pl.BlockSpec((tk, tm), lambda i, j, k: (k, i + ablk))
        b_spec = pl.BlockSpec((tk, tn), lambda i, j, k: (k, j))
    tile = pl.BlockSpec((tm, tn), lambda i, j, k: (i, j))
    if into is None:
        out_spec, out_shape, more, extra, aliases = tile, jax.ShapeDtypeStruct((M, N), out_dtype), [], [], {}
    else:
        buf, oblk = into
        out_spec = pl.BlockSpec((tm, tn), lambda i, j, k: (i, oblk * (N // tn) + j))
        out_shape = jax.ShapeDtypeStruct(buf.shape, buf.dtype)
        more, extra, aliases = [pl.BlockSpec(memory_space=pl.ANY)], [buf], {2 + ne: 0}
    return pl.pallas_call(
        body, name=name, grid=(M // tm, N // tn, nk),
        in_specs=[a_spec, b_spec] + [tile] * ne + more, out_specs=out_spec,
        out_shape=out_shape, input_output_aliases=aliases,
        scratch_shapes=[] if nk == 1 else [pltpu.VMEM((tm, tn), F32)],
        compiler_params=_cparams(("parallel", "parallel", "arbitrary")),
    )(a, b, *extras, *extra)


def group_mm(a, w, mode, out_dtype, name, b=None):
    T = a.shape[0]
    G, gc, _ = w.shape
    col = pl.BlockSpec((T, gc), lambda g: (0, g))
    wsp = pl.BlockSpec((1, gc, gc), lambda g: (g, 0, 0))
    if mode == 'tn':
        def body(a_ref, b_ref, o_ref):
            o_ref[0] = _dot(a_ref[...], b_ref[...], TN).astype(o_ref.dtype)
        return pl.pallas_call(body, name=name, grid=(G,), in_specs=[col, col], out_specs=wsp,
                              out_shape=jax.ShapeDtypeStruct((G, gc, gc), out_dtype),
                              compiler_params=_cparams(("parallel",)))(a, b)
    dims = NN if mode == 'nn' else NT

    def body(a_ref, w_ref, o_ref):
        o_ref[...] = _dot(a_ref[...], w_ref[0], dims).astype(o_ref.dtype)
    return pl.pallas_call(body, name=name, grid=(G,), in_specs=[col, wsp], out_specs=col,
                          out_shape=jax.ShapeDtypeStruct((T, G * gc), out_dtype),
                          compiler_params=_cparams(("parallel",)))(a, w)


def _rowspec(r):
    return r if isinstance(r, tuple) else (r, r.shape[1], 0)


def _row_specs(rows, segs, consts, tm, nctx):
    specs = [pl.BlockSpec((tm, w), lambda i, b=b: (i, b)) for _, w, b in rows]
    specs += [pl.BlockSpec((1,) + s.shape[1:], lambda i, n=s.ndim: (jnp.where(i >= nctx, 1, 0),) + (0,) * (n - 1))
              for s in segs]
    specs += [pl.BlockSpec(c.shape, lambda i, n=c.ndim: (0,) * n) for c in consts]
    return specs


def rowwise(fn, rows, segs, consts, outs, dm, name, tm=None):
    tm = tm or dm.tm
    nctx = dm.CTX // tm
    rows = [_rowspec(r) for r in rows]
    nr, ns, nc = len(rows), len(segs), len(consts)

    def body(*refs):
        rin = [r[...] for r in refs[:nr]]
        sin = [s[0] for s in refs[nr:nr + ns]]
        cin = [c[...] for c in refs[nr + ns:nr + ns + nc]]
        res = fn(*rin, *sin, *cin)
        for o_ref, v in zip(refs[nr + ns + nc:], res):
            o_ref[...] = v.astype(o_ref.dtype)

    res = pl.pallas_call(
        body, name=name, grid=(dm.T // tm,),
        in_specs=_row_specs(rows, segs, consts, tm, nctx),
        out_specs=[pl.BlockSpec((tm, w), lambda i: (i, 0)) for w, _ in outs],
        out_shape=[jax.ShapeDtypeStruct((dm.T, w), dt) for w, dt in outs],
        compiler_params=_cparams(("parallel",)),
    )(*[r[0] for r in rows], *segs, *consts)
    return res


def rowwise_vjp(fn, rows, segs, consts, cots, dm, name, tm=None, want=None, adds=None, narrow=(), into=None):
    tm = tm or dm.tm
    nctx = dm.CTX // tm
    rows = [_rowspec(r) for r in rows]
    cots = [_rowspec(r) for r in cots]
    adds = adds or {}
    nr, ns, nc, nct = len(rows), len(segs), len(consts), len(cots)
    want = want or [True] * nr
    widx = [k for k in range(nr) if want[k]]
    akeys = sorted(adds)

    def body(*refs):
        i = pl.program_id(0)
        rin = [r[...] for r in refs[:nr]]
        sin = [s[0] for s in refs[nr:nr + ns]]
        cin = [c[...] for c in refs[nr + ns:nr + ns + nc]]
        p = nr + ns + nc
        cot_refs = refs[p:p + nct]
        add_refs = dict(zip(akeys, refs[p + nct:p + nct + len(akeys)]))
        p = p + nct + len(akeys) + (1 if (into is not None and into[1] is not None) else 0)
        rg_refs = refs[p:p + len(widx)]
        sg_refs = refs[p + len(widx):p + len(widx) + ns]
        cg_refs = refs[p + len(widx) + ns:]
        res, vjp = jax.vjp(fn, *rin, *sin, *cin)
        g = vjp(tuple(cr[...].astype(o.dtype) for cr, o in zip(cot_refs, res)))
        for o_ref, k in zip(rg_refs, widx):
            v = g[k].astype(F32)
            if k in add_refs:
                v = v + add_refs[k][...]
            o_ref[...] = v.astype(o_ref.dtype)
        first_seg = jnp.logical_or(i == 0, i == nctx)
        for o_ref, v in zip(sg_refs, g[nr:nr + ns]):
            @pl.when(first_seg)
            def _(o_ref=o_ref, v=v):
                o_ref[0] = v.astype(F32)

            @pl.when(jnp.logical_not(first_seg))
            def _(o_ref=o_ref, v=v):
                o_ref[0] += v.astype(F32)
        for o_ref, v in zip(cg_refs, g[nr + ns:]):
            @pl.when(i == 0)
            def _(o_ref=o_ref, v=v):
                o_ref[...] = v.astype(F32)

            @pl.when(i > 0)
            def _(o_ref=o_ref, v=v):
                o_ref[...] += v.astype(F32)

    in_specs = _row_specs(rows, segs, consts, tm, nctx)
    in_specs += [pl.BlockSpec((tm, w), lambda i, b=b: (i, b)) for _, w, b in cots]
    in_specs += [pl.BlockSpec((tm, adds[k].shape[1]), lambda i: (i, 0)) for k in akeys]
    out_specs = [pl.BlockSpec((tm, rows[k][1]), lambda i: (i, 0)) for k in widx]
    out_shape = [jax.ShapeDtypeStruct((dm.T, rows[k][1]), MM_DTYPE if k in narrow else rows[k][0].dtype)
                 for k in widx]
    extra, aliases = [], {}
    if into is not None:
        ik, ibuf, ishape = into
        out_specs[widx.index(ik)] = pl.BlockSpec((tm, rows[ik][1]), lambda i, b=rows[ik][2]: (i, b))
        out_shape[widx.index(ik)] = jax.ShapeDtypeStruct(ishape, MM_DTYPE)
        if ibuf is not None:
            aliases = {len(in_specs): widx.index(ik)}
            in_specs = in_specs + [pl.BlockSpec(memory_space=pl.ANY)]
            extra = [ibuf]
    out_specs += [pl.BlockSpec((1,) + s.shape[1:], lambda i, n=s.ndim: (jnp.where(i >= nctx, 1, 0),) + (0,) * (n - 1))
                  for s in segs]
    out_shape += [jax.ShapeDtypeStruct(s.shape, F32) for s in segs]
    out_specs += [pl.BlockSpec(c.shape, lambda i, n=c.ndim: (0,) * n) for c in consts]
    out_shape += [jax.ShapeDtypeStruct(c.shape, F32) for c in consts]
    res = pl.pallas_call(
        body, name=name, grid=(dm.T // tm,), in_specs=in_specs, out_specs=out_specs, out_shape=out_shape,
        input_output_aliases=aliases, compiler_params=_cparams(("arbitrary",)),
    )(*[r[0] for r in rows], *segs, *consts, *[r[0] for r in cots], *[adds[k] for k in akeys], *extra)
    rg = [None] * nr
    for k, v in zip(widx, res[:len(widx)]):
        rg[k] = v
    return rg, list(res[len(widx):len(widx) + ns]), list(res[len(widx) + ns:])


def _rms(x, g):
    return x * lax.rsqrt(jnp.mean(x * x, axis=-1, keepdims=True) + EPS) * g


def _sigmoid(x):
    return jax.nn.sigmoid(x)


def pre_fn(x, shift, scale, g):
    return ((_rms(x, g) * (1.0 + scale) + shift).astype(MM_DTYPE),)


def mid_fn(x, y, gate, shift, scale, g_post, g_pre):
    x1 = x + gate * _rms(y.astype(F32), g_post)
    return x1, (_rms(x1, g_pre) * (1.0 + scale) + shift).astype(MM_DTYPE)


def post_fn(x1, y2, gate, g):
    return (x1 + gate * _rms(y2.astype(F32), g),)


def relu2_epi(acc):
    r = jnp.maximum(acc, 0.0)
    return r * r


def relu2_bwd_epi(dact, act):
    return dact * (2.0 * jnp.sqrt(act.astype(F32)))


def decay_fn(z, bd):
    zz = z.astype(F32) + bd
    ls = jnp.minimum(zz, 0.0) - jnp.log(1.0 + jnp.exp(jnp.minimum(zz, -zz)))
    la = ls / GLA_TAU
    gk = la.shape[1] // 2
    return la[:, :gk], la[:, gk:]


def glu_fn(ab):
    h = ab.shape[1] // 2
    return (ab[:, :h].astype(F32) * _sigmoid(ab[:, h:].astype(F32)),)


def glaout_fn(o_f, o_b, og, g):
    o = o_f + o_b
    dv = o.shape[1] // N_HEADS
    hs = []
    for h in range(N_HEADS):
        oh = o[:, h * dv:(h + 1) * dv]
        hs.append(oh * lax.rsqrt(jnp.mean(oh * oh, axis=-1, keepdims=True) + EPS) * g[:, h * dv:(h + 1) * dv])
    og = og.astype(F32)
    return ((jnp.concatenate(hs, axis=1) * (og * _sigmoid(og))).astype(MM_DTYPE),)


def convpost_fn(y, b_dw, g, b):
    y = y + b_dw
    mu = jnp.mean(y, axis=-1, keepdims=True)
    xc = y - mu
    yn = xc * lax.rsqrt(jnp.mean(xc * xc, axis=-1, keepdims=True) + EPS) * g + b
    return ((yn * _sigmoid(yn)).astype(MM_DTYPE),)


def poolpost_fn(pc, s):
    return ((pc.astype(F32) * s).astype(MM_DTYPE),)


def merge_fn(ya, yb, yc, mg, bg0, bg1, bg2):
    d = ya.shape[1]
    mg = mg.astype(F32)
    mixed = (_sigmoid(mg[:, :d] + bg0) * ya.astype(F32) + _sigmoid(mg[:, d:2 * d] + bg1) * yb.astype(F32)
             + _sigmoid(mg[:, 2 * d:] + bg2) * yc.astype(F32))
    return (mixed.astype(MM_DTYPE),)


def _split_dot(lmat, x, dims):
    hi = x.astype(MM_DTYPE)
    lo = x - hi.astype(F32)
    return _dot(lmat, hi, dims) + _dot(lmat, lo, dims)


def _gla_block_order(dm, rev):
    nctx, nb = dm.CTX // dm.TB, dm.T // dm.TB

    def blk(i):
        if not rev:
            return i
        return jnp.where(i < nctx, nctx - 1 - i, nb - 1 - (i - nctx))
    return blk, nb


def _gla_tri(rev):
    c = GLA_CHUNK
    t = lax.broadcasted_iota(jnp.int32, (c, c), 0)
    s = lax.broadcasted_iota(jnp.int32, (c, c), 1)
    return (s >= t) if rev else (s <= t)


def _gla_cumsum(la, tri):
    lmat = tri.astype(MM_DTYPE)
    return lmat, _split_dot(lmat, la, NN), jnp.sum(la, axis=0, keepdims=True)


def _gla_chunk_terms(q, k, b, bend, tri, scale):
    eb = jnp.exp(b)
    enb = jnp.exp(-b)
    ee = jnp.exp(bend - b)
    qi = q * scale * eb
    ki = k * enb
    kend = k * ee
    att = jnp.where(tri, _dot(qi, ki, NT), 0.0)
    return eb, enb, ee, qi, ki, kend, att


def gla_fwd(P, la, rev, dm, name):
    c, tb, h_, dk, dv, d = GLA_CHUNK, dm.TB, N_HEADS, dm.DK, dm.DV, dm.D
    cpb = tb // c
    blk, nb = _gla_block_order(dm, rev)
    gk, gv = h_ * dk, h_ * dv
    qb, kb, vb, lb = (5 * d) // gk, (5 * d + d // 2) // gk, (4 * d) // gv, (1 if rev else 0)
    scale = dk ** -0.5
    order = list(range(cpb))[::-1] if rev else list(range(cpb))

    def body(q_ref, k_ref, v_ref, la_ref, o_ref, s_ref, st):
        @pl.when(pl.program_id(0) == 0)
        def _():
            st[...] = jnp.zeros_like(st)
        tri = _gla_tri(rev)
        terms = {}
        for n, ci in enumerate(order):
            r = pl.ds(ci * c, c)
            _, b_all, bend_all = _gla_cumsum(la_ref[r, :], tri)
            for hh in range(h_):
                ck, cv = pl.ds(hh * dk, dk), pl.ds(hh * dv, dv)
                hs = slice(hh * dk, (hh + 1) * dk)
                v = v_ref[r, cv]
                _, _, _, qi, _, kend, att = _gla_chunk_terms(
                    q_ref[r, ck].astype(F32), k_ref[r, ck].astype(F32), b_all[:, hs], bend_all[:, hs], tri, scale)
                terms[n, hh] = (_dot(att, v), qi.astype(MM_DTYPE), jnp.exp(bend_all[:, hs]), _dot(v, kend, TN))
        for n, ci in enumerate(order):
            r = pl.ds(ci * c, c)
            for hh in range(h_):
                intra, qi, gam, dstate = terms[n, hh]
                s_in = st[hh]
                o_ref[r, pl.ds(hh * dv, dv)] = intra + _dot(qi, s_in, NT)
                s_ref[n, hh] = s_in
                st[hh] = gam * s_in + dstate

    return pl.pallas_call(
        body, name=name, grid=(nb,),
        in_specs=[pl.BlockSpec((tb, gk), lambda i: (blk(i), qb)),
                  pl.BlockSpec((tb, gk), lambda i: (blk(i), kb)),
                  pl.BlockSpec((tb, gv), lambda i: (blk(i), vb)),
                  pl.BlockSpec((tb, gk), lambda i: (blk(i), lb))],
        out_specs=[pl.BlockSpec((tb, gv), lambda i: (blk(i), 0)),
                   pl.BlockSpec((cpb, h_, dv, dk), lambda i: (i, 0, 0, 0))],
        out_shape=[jax.ShapeDtypeStruct((dm.T, gv), F32),
                   jax.ShapeDtypeStruct((dm.T // c, h_, dv, dk), F32)],
        scratch_shapes=[pltpu.VMEM((h_, dv, dk), F32)],
        compiler_params=_cparams(("arbitrary",)),
    )(P, P, P, la)


def gla_bwd(P, la, do, states, rev, dm, name, prev=None, into=None):
    c, tb, h_, dk, dv, d = GLA_CHUNK, dm.TB, N_HEADS, dm.DK, dm.DV, dm.D
    cpb = tb // c
    blk, nb = _gla_block_order(dm, rev)
    gk, gv = h_ * dk, h_ * dv
    qb, kb, vb, lb = (5 * d) // gk, (5 * d + d // 2) // gk, (4 * d) // gv, (1 if rev else 0)
    scale = dk ** -0.5
    order = list(range(cpb))[::-1] if rev else list(range(cpb))

    fused = prev is not None

    def body(q_ref, k_ref, v_ref, la_ref, do_ref, s_ref, *rest):
        if fused:
            pq_ref, pk_ref, pv_ref, _, w_ref, dla_ref, dst = rest
        else:
            dq_ref, dk_ref, dv_ref, dla_ref, dst = rest

        def put(kind, r, cols, val):
            if not fused:
                {'q': dq_ref, 'k': dk_ref, 'v': dv_ref}[kind][r, cols] = val
                return
            p_ref, off = {'q': (pq_ref, gv), 'k': (pk_ref, gv + gk), 'v': (pv_ref, 0)}[kind]
            w_ref[r, pl.ds(off + cols.start, cols.size)] = (val + p_ref[r, cols]).astype(w_ref.dtype)

        @pl.when(pl.program_id(0) == 0)
        def _():
            dst[...] = jnp.zeros_like(dst)
        tri = _gla_tri(rev)
        for n in range(cpb - 1, -1, -1):
            r = pl.ds(order[n] * c, c)
            lmat, b_all, bend_all = _gla_cumsum(la_ref[r, :], tri)
            dbs = []
            for hh in range(h_):
                ck, cv = pl.ds(hh * dk, dk), pl.ds(hh * dv, dv)
                hs = slice(hh * dk, (hh + 1) * dk)
                q = q_ref[r, ck].astype(F32)
                k = k_ref[r, ck].astype(F32)
                v = v_ref[r, cv]
                bend = bend_all[:, hs]
                eb, enb, ee, qi, ki, kend, att = _gla_chunk_terms(q, k, b_all[:, hs], bend, tri, scale)
                s_in = s_ref[n, hh]
                ds_out = dst[hh]
                dob = do_ref[r, cv]
                datt = jnp.where(tri, _dot(dob, v, NT), 0.0)
                dqi = _dot(datt, ki) + _dot(dob, s_in)
                dki = _dot(datt, qi, TN)
                put('v', r, cv, _dot(att, dob, TN) + _dot(kend, ds_out, NT))
                dkend = _dot(v, ds_out)
                gam = jnp.exp(bend)
                dgam = jnp.sum(ds_out * s_in, axis=0, keepdims=True)
                dst[hh] = gam * ds_out + _dot(dob, qi, TN)
                put('q', r, ck, dqi * (scale * eb))
                put('k', r, ck, dki * enb + dkend * ee)
                dbend = jnp.sum(dkend * kend, axis=0, keepdims=True) + dgam * gam
                dbs.append((dqi * qi - dki * ki - dkend * kend, dbend))
            dla_ref[r, :] = (_split_dot(lmat, jnp.concatenate([t[0] for t in dbs], axis=1), TN)
                             + jnp.concatenate([t[1] for t in dbs], axis=1))

    def bi(j):
        return blk(nb - 1 - j)

    in_specs = [
        pl.BlockSpec((tb, gk), lambda j: (bi(j), qb)),
        pl.BlockSpec((tb, gk), lambda j: (bi(j), kb)),
        pl.BlockSpec((tb, gv), lambda j: (bi(j), vb)),
        pl.BlockSpec((tb, gk), lambda j: (bi(j), lb)),
        pl.BlockSpec((tb, gv), lambda j: (bi(j), 0)),
        pl.BlockSpec((cpb, h_, dv, dk), lambda j: (nb - 1 - j, 0, 0, 0)),
    ]
    small = pl.BlockSpec((tb, gk), lambda j: (bi(j), 0))
    wide = pl.BlockSpec((tb, gv), lambda j: (bi(j), 0))
    if not fused:
        return pl.pallas_call(
            body, name=name, grid=(nb,), in_specs=in_specs, out_specs=[small, small, wide, small],
            out_shape=[jax.ShapeDtypeStruct((dm.T, gk), F32), jax.ShapeDtypeStruct((dm.T, gk), F32),
                       jax.ShapeDtypeStruct((dm.T, gv), F32), jax.ShapeDtypeStruct((dm.T, gk), F32)],
            scratch_shapes=[pltpu.VMEM((h_, dv, dk), F32)],
            compiler_params=_cparams(("arbitrary",)),
        )(P, P, P, la, do, states)
    return pl.pallas_call(
        body, name=name, grid=(nb,),
        in_specs=in_specs + [small, small, wide, pl.BlockSpec(memory_space=pl.ANY)],
        out_specs=[pl.BlockSpec((tb, 2 * gv), lambda j: (bi(j), vb // 2)), small],
        out_shape=[jax.ShapeDtypeStruct(into.shape, into.dtype), jax.ShapeDtypeStruct((dm.T, gk), F32)],
        input_output_aliases={9: 0},
        scratch_shapes=[pltpu.VMEM((h_, dv, dk), F32)],
        compiler_params=_cparams(("arbitrary",)),
    )(P, P, P, la, do, states, *prev, into)


def _pos(n, period):
    t = lax.broadcasted_iota(jnp.int32, (n, 1), 0)
    if period & (period - 1) == 0:
        return jnp.bitwise_and(t, period - 1)
    return lax.rem(t, period)


def _conv_segments(dm):
    return [(0, dm.CTX, dm.CTX), (dm.CTX, dm.SEQ, GRID_W)]


def conv_fwd(u, w, dm, name):
    kw, cw = w.shape
    segs = _conv_segments(dm)

    def body(u_ref, w_ref, y_ref):
        for r0, n, per in segs:
            useg = u_ref[r0:r0 + n, :]
            p = _pos(n, per)
            acc = jnp.zeros_like(useg)
            for kk in range(kw):
                d = kk - kw // 2
                sh = useg if d == 0 else pltpu.roll(useg, (-d) % n, 0)
                ok = jnp.logical_and(p + d >= 0, p + d < per)
                acc = acc + jnp.where(ok, sh, 0.0) * w_ref[kk:kk + 1, :]
            y_ref[r0:r0 + n, :] = acc

    return pl.pallas_call(
        body, name=name, grid=(cw // LANES,),
        in_specs=[pl.BlockSpec((dm.T, LANES), lambda j: (0, j)), pl.BlockSpec((kw, LANES), lambda j: (0, j))],
        out_specs=pl.BlockSpec((dm.T, LANES), lambda j: (0, j)),
        out_shape=jax.ShapeDtypeStruct((dm.T, cw), F32),
        compiler_params=_cparams(("parallel",)),
    )(u, w)


def conv_bwd(u, w, dy, dm, name):
    kw, cw = w.shape
    segs = _conv_segments(dm)

    def body(u_ref, w_ref, dy_ref, du_ref, dw_ref):
        dws = [jnp.zeros((1, LANES), F32)] * kw
        for r0, n, per in segs:
            useg = u_ref[r0:r0 + n, :]
            dyseg = dy_ref[r0:r0 + n, :]
            p = _pos(n, per)
            acc = jnp.zeros_like(useg)
            for kk in range(kw):
                d = kk - kw // 2
                shu = useg if d == 0 else pltpu.roll(useg, (-d) % n, 0)
                okf = jnp.logical_and(p + d >= 0, p + d < per)
                dws[kk] = dws[kk] + jnp.sum(jnp.where(okf, shu, 0.0) * dyseg, axis=0, keepdims=True)
                shd = dyseg if d == 0 else pltpu.roll(dyseg, d % n, 0)
                okb = jnp.logical_and(p - d >= 0, p - d < per)
                acc = acc + jnp.where(okb, shd, 0.0) * w_ref[kk:kk + 1, :]
            du_ref[r0:r0 + n, :] = acc
        for kk in range(kw):
            dw_ref[kk:kk + 1, :] = dws[kk]

    return pl.pallas_call(
        body, name=name, grid=(cw // LANES,),
        in_specs=[pl.BlockSpec((dm.T, LANES), lambda j: (0, j)), pl.BlockSpec((kw, LANES), lambda j: (0, j)),
                  pl.BlockSpec((dm.T, LANES), lambda j: (0, j))],
        out_specs=[pl.BlockSpec((dm.T, LANES), lambda j: (0, j)), pl.BlockSpec((kw, LANES), lambda j: (0, j))],
        out_shape=[jax.ShapeDtypeStruct((dm.T, cw), F32), jax.ShapeDtypeStruct((kw, cw), F32)],
        compiler_params=_cparams(("parallel",)),
    )(u, w, dy)


def pool_mix(u, transpose, dm, name, into=None):
    u, uw, ublk = _rowspec(u)
    gc = dm.GC
    ng = len(POOL_WINDOWS)
    rows = dm.SEQ // GRID_W
    segs = [(0, dm.CTX, 1, dm.CTX), (dm.CTX, dm.SEQ, GRID_W, rows)]

    def one_group(u_ref, o_ref, win):
        left = win // 2
        right = win - 1 - left
        for r0, n, stride, length in segs:
            useg = u_ref[r0:r0 + n, :].astype(F32)
            t = lax.broadcasted_iota(jnp.int32, (n, 1), 0)
            p = t if stride == 1 else jnp.right_shift(t, stride.bit_length() - 1)
            cnt = (jnp.minimum(p + right + 1, length) - jnp.maximum(p - left, 0)).astype(F32)
            src = useg / cnt if transpose else useg
            acc = jnp.zeros_like(useg)
            for d in range(-left, right + 1):
                dd = -d if transpose else d
                sh = src if d == 0 else pltpu.roll(src, (-dd * stride) % n, 0)
                ok = jnp.logical_and(p + dd >= 0, p + dd < length)
                acc = acc + jnp.where(ok, sh, 0.0)
            o_ref[r0:r0 + n, :] = ((acc - useg) if transpose else (acc / cnt - useg)).astype(o_ref.dtype)

    def body(u_ref, *rest):
        o_ref = rest[-1]
        g = pl.program_id(0)
        for gi, win in enumerate(POOL_WINDOWS):
            @pl.when(g == gi)
            def _(win=win):
                one_group(u_ref, o_ref, win)

    base = ublk * (uw // gc)
    if into is None:
        obase, out_shape, more, extra, aliases = 0, jax.ShapeDtypeStruct((dm.T, ng * gc), F32), [], [], {}
    else:
        buf, oblk = into
        obase, out_shape = oblk * ng, jax.ShapeDtypeStruct(buf.shape, buf.dtype)
        more, extra, aliases = [pl.BlockSpec(memory_space=pl.ANY)], [buf], {1: 0}
    return pl.pallas_call(
        body, name=name, grid=(ng,),
        in_specs=[pl.BlockSpec((dm.T, gc), lambda g: (0, base + g))] + more,
        out_specs=pl.BlockSpec((dm.T, gc), lambda g: (0, obase + g)),
        out_shape=out_shape, input_output_aliases=aliases,
        compiler_params=_cparams(("parallel",)),
    )(u, *extra)


def loss_head(x2, target, dm, name):
    tm, d = dm.tm, dm.D
    nctx = dm.CTX // tm

    def body(x_ref, t_ref, dx_ref, l_ref):
        i = pl.program_id(0)

        @pl.when(i == 0)
        def _():
            l_ref[...] = jnp.zeros_like(l_ref)

        @pl.when(i < nctx)
        def _():
            dx_ref[...] = jnp.zeros_like(dx_ref)

        @pl.when(i >= nctx)
        def _():
            e = x_ref[...] - t_ref[...]
            dx_ref[...] = e / d
            l_ref[...] += jnp.full(l_ref.shape, 0.5 * jnp.sum(jnp.mean(e * e, axis=-1)), F32)

    return pl.pallas_call(
        body, name=name, grid=(dm.T // tm,),
        in_specs=[pl.BlockSpec((tm, d), lambda i: (i, 0)),
                  pl.BlockSpec((tm, d), lambda i: (jnp.maximum(i - nctx, 0), 0))],
        out_specs=[pl.BlockSpec((tm, d), lambda i: (i, 0)), pl.BlockSpec((8, LANES), lambda i: (0, 0))],
        out_shape=[jax.ShapeDtypeStruct((dm.T, d), F32), jax.ShapeDtypeStruct((8, LANES), F32)],
        compiler_params=_cparams(("arbitrary",)),
    )(x2, target)


def adamw(w, g, m, v, name):
    r, c = w.shape
    tr = _tile(r, tuple(t for t in (512, 256, 128, 64, 32, 16, 8) if t * c * 4 <= (1 << 20)) or (8,))

    def body(w_ref, g_ref, m_ref, v_ref, d_ref, mo_ref, vo_ref):
        gg = g_ref[...]
        mm = ADAM_B1 * m_ref[...] + (1.0 - ADAM_B1) * gg
        vv = ADAM_B2 * v_ref[...] + (1.0 - ADAM_B2) * (gg * gg)
        m_hat = mm / (1.0 - ADAM_B1 ** ADAM_STEP)
        v_hat = vv / (1.0 - ADAM_B2 ** ADAM_STEP)
        d_ref[...] = -ADAM_LR * (m_hat / (jnp.sqrt(v_hat) + ADAM_EPS) + ADAM_WD * w_ref[...])
        mo_ref[...] = mm
        vo_ref[...] = vv

    spec = pl.BlockSpec((tr, c), lambda i: (i, 0))
    return pl.pallas_call(
        body, name=name, grid=(r // tr,), in_specs=[spec] * 4, out_specs=[spec] * 3,
        out_shape=[jax.ShapeDtypeStruct((r, c), F32)] * 3,
        compiler_params=_cparams(("parallel",)),
    )(w, g, m, v)


def pair_add(g, r1, cidx, name):
    ng, r_, n_ = r1.shape
    tr = _tile(r_, tuple(t for t in (1024, 512, 256, 128, 64, 32, 16) if t * n_ * 4 <= (2 << 20)))

    def body(s_ref, g_ref, r_ref, o_ref):
        o_ref[...] = (g_ref[...].astype(F32) + r_ref[...].astype(F32)).astype(o_ref.dtype)

    return pl.pallas_call(
        body, name=name,
        grid_spec=pltpu.PrefetchScalarGridSpec(
            num_scalar_prefetch=1, grid=(ng, r_ // tr),
            in_specs=[pl.BlockSpec((None, tr, n_), lambda k, i, s: (2 * k + s[0], i, 0)),
                      pl.BlockSpec((None, tr, n_), lambda k, i, s: (k, i, 0))],
            out_specs=pl.BlockSpec((None, tr, n_), lambda k, i, s: (k, i, 0))),
        out_shape=jax.ShapeDtypeStruct((ng, r_, n_), g.dtype),
        compiler_params=_cparams(("parallel", "parallel")),
    )(cidx, g, r1)


def chip_add(h, r2, axis, where, name, slab=True):
    _, kl, nl = r2.shape
    tr = _tile(kl, tuple(t for t in (1024, 512, 256, 128, 64, 32, 16) if t * nl * 4 <= (1 << 20)))
    nrb = kl // tr

    def body(s_ref, h_ref, r_ref, o_ref):
        acc = h_ref[...].astype(F32)
        for k in range(r2.shape[0]):
            acc = acc + r_ref[k].astype(F32)
        o_ref[...] = acc

    h_map = (lambda i, s: (s[0] * nrb + i, 0)) if axis == 0 else (lambda i, s: (i, s[0]))
    if slab:
        out_spec = pl.BlockSpec((None, tr, nl), lambda i, s: (s[1], i, 0))
        out_shape = jax.ShapeDtypeStruct((2, kl, nl), F32)
    else:
        out_spec = pl.BlockSpec((tr, nl), lambda i, s: (i, 0))
        out_shape = jax.ShapeDtypeStruct((kl, nl), F32)
    return pl.pallas_call(
        body, name=name,
        grid_spec=pltpu.PrefetchScalarGridSpec(
            num_scalar_prefetch=1, grid=(nrb,),
            in_specs=[pl.BlockSpec((tr, nl), h_map),
                      pl.BlockSpec((r2.shape[0], tr, nl), lambda i, s: (0, i, 0))],
            out_specs=out_spec),
        out_shape=out_shape,
        compiler_params=_cparams(("parallel",)),
    )(where, h, r2)


def adamw_layers(w, m, v, terms, name, layer=None, prev=None):
    _, a_, b_ = w.shape
    tr = _tile(a_, tuple(t for t in (512, 256, 128, 64, 32) if t * b_ * 4 <= (1 << 20)))
    by_cols = tr == a_ and a_ * b_ * 4 > (1 << 20)
    blk = (a_, LANES) if by_cols else (tr, b_)
    steps = b_ // LANES if by_cols else a_ // tr
    at = (lambda i: (0, i)) if by_cols else (lambda i: (i, 0))
    layers = (0, 1) if layer is None else (layer,)
    counts = [len(terms[l]) for l in layers]
    nprev = 0 if prev is None else 4

    def update(g, w_ref, m_ref, v_ref, g_ref, d_ref, mo_ref, vo_ref):
        mm = ADAM_B1 * m_ref[...] + (1.0 - ADAM_B1) * g
        vv = ADAM_B2 * v_ref[...] + (1.0 - ADAM_B2) * (g * g)
        m_hat = mm / (1.0 - ADAM_B1 ** ADAM_STEP)
        v_hat = vv / (1.0 - ADAM_B2 ** ADAM_STEP)
        g_ref[...] = g
        d_ref[...] = -ADAM_LR * (m_hat / (jnp.sqrt(v_hat) + ADAM_EPS) + ADAM_WD * w_ref[...])
        mo_ref[...] = mm
        vo_ref[...] = vv

    def total(refs):
        g = refs[0][...]
        for r in refs[1:]:
            g = g + r[...]
        return g

    def body(w_ref, m_ref, v_ref, *rest):
        t_refs, outs = rest[:sum(counts)], rest[-4:]
        if len(layers) == 1:
            update(total(t_refs), w_ref, m_ref, v_ref, *outs)
            return
        which = pl.program_id(0)

        @pl.when(which == 0)
        def _():
            update(total(t_refs[:counts[0]]), w_ref, m_ref, v_ref, *outs)

        @pl.when(which == 1)
        def _():
            update(total(t_refs[counts[0]:]), w_ref, m_ref, v_ref, *outs)

    if len(layers) == 1:
        stacked = pl.BlockSpec((None,) + blk, lambda l, i: (layers[0],) + at(i))
        t_specs = [pl.BlockSpec(blk, lambda l, i: at(i))] * counts[0]
    else:
        stacked = pl.BlockSpec((None,) + blk, lambda l, i: (l,) + at(i))
        t_specs = ([pl.BlockSpec(blk, lambda l, i: at(i * (1 - l)))] * counts[0]
                   + [pl.BlockSpec(blk, lambda l, i: at(i * l))] * counts[1])
    nin = 3 + sum(counts)
    return pl.pallas_call(
        body, name=name, grid=(len(layers), steps),
        in_specs=[stacked] * 3 + t_specs + [pl.BlockSpec(memory_space=pl.ANY)] * nprev,
        out_specs=[stacked] * 4, out_shape=[jax.ShapeDtypeStruct(w.shape, F32)] * 4,
        input_output_aliases={nin + j: j for j in range(nprev)},
        compiler_params=_cparams(("arbitrary", "arbitrary")),
    )(w, m, v, *[t for l in layers for t in terms[l]], *(prev or ()))


MESH = pl.DeviceIdType.MESH
ANY = pl.BlockSpec(memory_space=pl.ANY)
HBM = pl.BlockSpec(memory_space=pltpu.HBM)
SEM = pl.BlockSpec(memory_space=pltpu.SEMAPHORE)
EFFECT = pltpu.SideEffectType.DATAFLOW_SIDE_EFFECTING


def _place():
    return lax.axis_index("x"), lax.axis_index("y"), lax.axis_index("c")


def _peers(x, y):
    return [(1 - x, y), (x, 1 - y), (1 - x, 1 - y)]


def _rcopy(src, dst, ssem, rsem, dev):
    return pltpu.make_async_remote_copy(src_ref=src, dst_ref=dst, send_sem=ssem, recv_sem=rsem,
                                        device_id=dev, device_id_type=MESH)


def _gathered_shape(src, kind):
    h, a_, b_ = src.shape
    return (h, a_, 4 * b_) if kind == 'col' else (4, h, a_, b_)


def _win(ref, kind, ch, width):
    return ref.at[:, :, pl.ds(ch * width, width)] if kind == 'col' else ref.at[ch]


def _rect(ref, kind, half, ch, width):
    return ref.at[half, :, pl.ds(ch * width, width)] if kind == 'col' else ref.at[ch, half]


def _gather_plan(kinds, widths):
    def plan(src, land, x, y, c):
        chip = 2 * x + y
        out = []
        for n in range(len(src)):
            for px, py in _peers(x, y):
                out.append((src[n].at[c], _rect(land[n], kinds[n], c, chip, widths[n]), (px, py, c),
                            _rect(land[n], kinds[n], c, 2 * px + py, widths[n])))
            mine = _win(land[n], kinds[n], chip, widths[n])
            out.append((src[n], mine, (x, y, 1 - c), mine))
        return out
    return plan


def forward_halves(lands, kinds, name):
    nw = len(lands)
    widths = [t.shape[-1] // 4 if k == 'col' else t.shape[-1] for t, k in zip(lands, kinds)]

    def body(*refs):
        o = refs[nw:2 * nw]
        ssem, rsem = refs[2 * nw:]
        x, y, c = _place()
        sib = (x, y, 1 - c)
        pidx = [2 * px + py for px, py in _peers(x, y)]
        cps = [_rcopy(_rect(o[n], kinds[n], c, pidx[j], widths[n]), _rect(o[n], kinds[n], c, pidx[j], widths[n]),
                      ssem.at[3 * n + j], rsem.at[3 * n + j], sib) for n in range(nw) for j in range(3)]
        for cp in cps:
            cp.start()
        for n in range(nw):
            for j in range(3):
                cps[3 * n + j].wait_send()
                _rcopy(_rect(o[n], kinds[n], 1 - c, pidx[j], widths[n]), _rect(o[n], kinds[n], 1 - c, pidx[j], widths[n]),
                       ssem.at[3 * n + j], rsem.at[3 * n + j], sib).wait_recv()

    return pl.pallas_call(
        body, name=name, in_specs=[ANY] * nw, out_specs=[ANY] * nw,
        out_shape=[jax.ShapeDtypeStruct(t.shape, t.dtype) for t in lands],
        input_output_aliases={n: n for n in range(nw)},
        scratch_shapes=[pltpu.SemaphoreType.DMA((3 * nw,)), pltpu.SemaphoreType.DMA((3 * nw,))],
    )(*lands)


def _scatter_plan(axes, widths):
    def plan(src, land, x, y, c):
        out = []
        for n in range(len(src)):
            for k, (px, py) in enumerate(_peers(x, y)):
                ch = 2 * px + py
                view = (src[n].at[:, pl.ds(ch * widths[n], widths[n])] if axes[n] == 1
                        else src[n].at[pl.ds(ch * widths[n], widths[n]), :])
                out.append((view, land[n].at[k], (px, py, c), land[n].at[k]))
        return out
    return plan


def _exchange_plan(kinds):
    def plan(src, land, x, y, c):
        out = []
        for n in range(len(src)):
            w = land[n].shape[2]
            for j, (px, py) in enumerate(_peers(x, y)):
                ch = 2 * px + py
                view = src[n].at[:, pl.ds(ch * w, w)] if kinds[n] == 'col' else src[n].at[ch]
                out.append((view, land[n].at[j], (px, py, c), land[n].at[j]))
        return out
    return plan


def start_copies(srcs, lands, plan, ncopies, after, name):
    ns, nl = len(srcs), len(lands)

    def body(*refs):
        src, land = refs[:ns], refs[ns:ns + nl]
        ssem, rsem = refs[ns + nl + 1], refs[ns + nl + 2]
        token = refs[-1]
        x, y, c = _place()
        for k, (sv, dv, dev, _) in enumerate(plan(src, land, x, y, c)):
            _rcopy(sv, dv, ssem.at[k], rsem.at[k], dev).start()
        token[...] = jnp.zeros_like(token)

    hbm = lambda t: pltpu.HBM(t.shape, t.dtype)
    res = pl.pallas_call(
        body, name=name,
        out_shape=(pltpu.SemaphoreType.DMA((ncopies,)), pltpu.SemaphoreType.DMA((ncopies,)),
                   *[hbm(t) for t in srcs], *[hbm(t) for t in lands], jax.ShapeDtypeStruct((8, LANES), F32)),
        in_specs=[HBM] * (ns + nl) + [ANY],
        out_specs=(SEM, SEM, *[HBM] * (ns + nl), pl.BlockSpec(memory_space=pltpu.VMEM)),
        input_output_aliases={k: 2 + k for k in range(ns + nl)},
        compiler_params=pltpu.CompilerParams(has_side_effects=EFFECT),
    )(*[pltpu.with_memory_space_constraint(t, pltpu.HBM) for t in list(srcs) + list(lands)], after)
    return res[0], res[1], list(res[2:2 + ns]), list(res[2 + ns:2 + ns + nl]), res[-1]


def wait_copies(ssem, rsem, srcs, lands, plan, after, name):
    ns, nl = len(srcs), len(lands)

    def body(*refs):
        src, land = refs[:ns], refs[ns:ns + nl]
        ss, rs = refs[ns + nl], refs[ns + nl + 1]
        x, y, c = _place()
        for k, (sv, dv, dev, mine) in enumerate(plan(src, land, x, y, c)):
            cp = _rcopy(sv, mine, ss.at[k], rs.at[k], dev)
            cp.wait_send()
            cp.wait_recv()

    hbm = lambda t: pltpu.HBM(t.shape, t.dtype)
    res = pl.pallas_call(
        body, name=name,
        out_shape=(*[hbm(t) for t in srcs], *[hbm(t) for t in lands]),
        in_specs=[HBM] * (ns + nl) + [SEM, SEM, ANY], out_specs=tuple([HBM] * (ns + nl)),
        input_output_aliases={k: k for k in range(ns + nl)},
        compiler_params=pltpu.CompilerParams(has_side_effects=EFFECT),
    )(*srcs, *lands, ssem, rsem, after)
    return list(res[ns:])


def pair_swap_halves(gs, kinds, name):
    nw = len(gs)

    def other(ref, kind, half):
        return ref.at[half] if kind == 'col' else ref.at[:, half]

    def body(*refs):
        g, o = refs[:nw], refs[nw:2 * nw]
        ssem, rsem = refs[2 * nw:]
        x, y, c = _place()
        cps = [_rcopy(other(g[n], kinds[n], 1 - c), o[n], ssem.at[n], rsem.at[n], (x, y, 1 - c)) for n in range(nw)]
        for cp in cps:
            cp.start()
        for cp in cps:
            cp.wait()

    return pl.pallas_call(
        body, name=name, in_specs=[ANY] * nw, out_specs=[ANY] * nw,
        out_shape=[jax.ShapeDtypeStruct(g.shape[1:] if k == 'col' else (g.shape[0],) + g.shape[2:], g.dtype)
                   for g, k in zip(gs, kinds)],
        scratch_shapes=[pltpu.SemaphoreType.DMA((nw,)), pltpu.SemaphoreType.DMA((nw,))],
    )(*gs)


def pair_swap(fs, name):
    nw = len(fs)

    def body(*refs):
        f, o = refs[:nw], refs[nw:2 * nw]
        ssem, rsem = refs[2 * nw:]
        x, y, c = _place()
        cps = [_rcopy(f[n], o[n], ssem.at[n], rsem.at[n], (x, y, 1 - c)) for n in range(nw)]
        for cp in cps:
            cp.start()
        for cp in cps:
            cp.wait()

    return pl.pallas_call(
        body, name=name, in_specs=[ANY] * nw, out_specs=[ANY] * nw,
        out_shape=[jax.ShapeDtypeStruct(f.shape, f.dtype) for f in fs],
        scratch_shapes=[pltpu.SemaphoreType.DMA((nw,)), pltpu.SemaphoreType.DMA((nw,))],
    )(*fs)


def chip_broadcast(h, name):
    def body(h_ref, o_ref, ssem, rsem):
        x, y, c = _place()
        cps = [_rcopy(h_ref, o_ref.at[j], ssem.at[j], rsem.at[j], (px, py, c)) for j, (px, py) in enumerate(_peers(x, y))]
        for cp in cps:
            cp.start()
        for cp in cps:
            cp.wait()

    return pl.pallas_call(
        body, name=name, in_specs=[ANY], out_specs=ANY,
        out_shape=jax.ShapeDtypeStruct((3,) + h.shape, h.dtype),
        scratch_shapes=[pltpu.SemaphoreType.DMA((3,)), pltpu.SemaphoreType.DMA((3,))],
    )(h)


def quad_sum(h, r, cidx, name):
    r_, c_ = h.shape
    tr = _tile(r_, (512, 256, 128, 64, 32, 16, 8))

    def body(s_ref, h_ref, r_ref, o_ref):
        o_ref[...] = (h_ref[...] + r_ref[2]) + (r_ref[0] + r_ref[1])

    return pl.pallas_call(
        body, name=name,
        grid_spec=pltpu.PrefetchScalarGridSpec(
            num_scalar_prefetch=1, grid=(r_ // tr,),
            in_specs=[pl.BlockSpec((tr, c_), lambda i, s: (i, 0)), pl.BlockSpec((3, tr, c_), lambda i, s: (0, i, 0))],
            out_specs=pl.BlockSpec((None, tr, c_), lambda i, s: (s[0], i, 0))),
        out_shape=jax.ShapeDtypeStruct((2, r_, c_), F32),
        compiler_params=_cparams(("parallel",)),
    )(cidx, h, r)


def pair_join_layers(fs, name):
    nw = len(fs)

    def body(*refs):
        o = refs[nw:2 * nw]
        ssem, rsem = refs[2 * nw:]
        x, y, c = _place()
        sib = (x, y, 1 - c)
        cps = [_rcopy(o[n].at[c], o[n].at[c], ssem.at[n], rsem.at[n], sib) for n in range(nw)]
        for cp in cps:
            cp.start()
        for n in range(nw):
            cps[n].wait_send()
            _rcopy(o[n].at[1 - c], o[n].at[1 - c], ssem.at[n], rsem.at[n], sib).wait_recv()

    return pl.pallas_call(
        body, name=name, in_specs=[ANY] * nw, out_specs=[ANY] * nw,
        out_shape=[jax.ShapeDtypeStruct(f.shape, f.dtype) for f in fs],
        input_output_aliases={n: n for n in range(nw)},
        scratch_shapes=[pltpu.SemaphoreType.DMA((nw,)), pltpu.SemaphoreType.DMA((nw,))],
    )(*fs)


def _flatten_pad(parts, dtype):
    flat = jnp.concatenate([p.reshape(-1).astype(dtype) for p in parts])
    q = 512 * LANES
    n = -(-flat.shape[0] // q) * q
    return jnp.pad(flat, (0, n - flat.shape[0])).reshape(n // LANES, LANES)


def _lane_pad(n):
    return -(-n // LANES) * LANES


def _in_proj_layout(d):
    gk, gv, cw, pw = d // 2, d, d // 2, d // 2
    own = [('q', gk), ('k', gk), ('v', gv), ('og', gv), ('lrf', GLA_LR), ('lrb', GLA_LR), ('ga', cw), ('gb', cw),
           ('pu', pw), ('mg', 3 * d)]
    padded = [('mg', 3 * d), ('og', gv), ('v', gv), ('q', gk), ('k', gk), ('ga', cw), ('gb', cw), ('pu', pw),
              ('lrf', GLA_LR), ('lrb', GLA_LR), ('pad', d // 2 - 2 * GLA_LR)]
    return own, padded


def _row_pieces(src, lo, hi, wl, wlp):
    out = []
    for k in range(4):
        s0, s1 = max(lo, k * wl), min(hi, (k + 1) * wl)
        if s0 < s1:
            out.append(src[k * wlp + s0 - k * wl:k * wlp + s1 - k * wl])
    return out


def _w_in_t_to_proj(g, d, wl, wlp):
    own, padded = _in_proj_layout(d)
    at, start = {}, 0
    for n, wd in own:
        at[n] = (start, start + wd)
        start += wd
    parts = []
    for n, wd in padded:
        parts += [jnp.zeros((wd, g.shape[1]), g.dtype)] if n == 'pad' else _row_pieces(g, *at[n], wl, wlp)
    return jnp.concatenate(parts, axis=0)


def _proj_to_w_in_t(gp, d, wl, wlp):
    own, padded = _in_proj_layout(d)
    pat, start = {}, 0
    for n, wd in padded:
        pat[n] = start
        start += wd
    parts = []
    for k in range(4):
        start = 0
        for n, wd in own:
            s0, s1 = max(start, k * wl), min(start + wd, (k + 1) * wl)
            if s0 < s1:
                parts.append(gp[pat[n] + s0 - start:pat[n] + s1 - start])
            start += wd
        parts.append(jnp.zeros((wlp - wl, gp.shape[1]), gp.dtype))
    return jnp.concatenate(parts, axis=0)


def _silu_grad(z):
    s = jax.nn.sigmoid(z)
    return s + z * s * (1.0 - s)


def kernel(x, c, ctx, c_ctx, w_ada, b_ada, g_pre_mix, g_post_mix, g_pre_mlp, g_post_mlp, w_in, w_decay, b_decay, g_gla, w_gla_o, w_dw, b_dw, g_conv_ln, b_conv_ln, w_conv_o, w_pool_g, s_pool, w_pool_o, b_gate, w_out, w_mlp1, w_mlp2, loss_target, m_c_ctx, m_w_ada, m_b_ada, m_g_pre_mix, m_g_post_mix, m_g_pre_mlp, m_g_post_mlp, m_w_in, m_w_decay, m_b_decay, m_g_gla, m_w_gla_o, m_w_dw, m_b_dw, m_g_conv_ln, m_b_conv_ln, m_w_conv_o, m_w_pool_g, m_s_pool, m_w_pool_o, m_b_gate, m_w_out, m_w_mlp1, m_w_mlp2, v_c_ctx, v_w_ada, v_b_ada, v_g_pre_mix, v_g_post_mix, v_g_pre_mlp, v_g_post_mlp, v_w_in, v_w_decay, v_b_decay, v_g_gla, v_w_gla_o, v_w_dw, v_b_dw, v_g_conv_ln, v_b_conv_ln, v_w_conv_o, v_w_pool_g, v_s_pool, v_w_pool_o, v_b_gate, v_w_out, v_w_mlp1, v_w_mlp2):
    a = dict(locals())
    for n in ('w_in', 'm_w_in', 'v_w_in'):
        a[n] = jnp.swapaxes(a[n], 1, 2)
    big_axis = dict(BIG, w_in=1)
    depth = w_in.shape[0]
    d = x.shape[-1]
    seq, nctx_rows = x.shape[1], ctx.shape[1]
    dm = types.SimpleNamespace(
        D=d, SEQ=seq, CTX=nctx_rows, T=seq + nctx_rows, DK=d // 8, DV=d // 4, GK=d // 2, GC=d // 8,
        tm=_tile(nctx_rows, (256, 128, 64)), TB=_tile(nctx_rows, (256, 128, 64)))
    assert dm.SEQ % dm.tm == 0 and dm.SEQ % GRID_W == 0 and dm.CTX % GLA_CHUNK == 0
    tmw = min(dm.tm, 128)
    chip = 2 * lax.axis_index("x") + lax.axis_index("y")
    core = lax.axis_index("c")
    chip1 = chip.astype(jnp.int32).reshape(1)
    core1 = core.astype(jnp.int32).reshape(1)

    big_names, small_names = list(BIG), list(SMALL_SHARDED)
    nbig = len(big_names)
    kinds = ['col' if big_axis[n] == 2 else 'row' for n in big_names]
    wl = w_in.shape[2]
    wlp = _lane_pad(wl)

    def rows8(t):
        t = t.reshape(t.shape[0], -1, t.shape[-1])
        return jnp.pad(t, ((0, 0), (0, -t.shape[1] % 8), (0, 0)))

    def halves(t):
        return t.reshape(2, t.shape[0] // 2, t.shape[1])

    def layer_src(l, tok=None):
        def one(n):
            t = a[n][l] if tok is None else a[n][l] + tok
            return halves((jnp.pad(t, ((0, wlp - wl), (0, 0))) if n == 'w_in' else t).astype(MM_DTYPE))
        return [one(n) for n in big_names]

    def whole(t):
        return t.reshape(-1, t.shape[-1])

    def start_gather(srcs, knds, after, name):
        plan = _gather_plan(knds, [t.shape[2] for t in srcs])
        lands = [lax.empty(_gathered_shape(t, k), t.dtype) for t, k in zip(srcs, knds)]
        return (plan,) + start_copies(srcs, lands, plan, 4 * len(srcs), after, name)

    late = [big_names.index(n) for n in ('w_gla_o', 'w_conv_o', 'w_pool_o', 'w_out', 'w_mlp1', 'w_mlp2')]
    early = [k for k in range(nbig) if k not in late]
    src0 = layer_src(0)
    kinds_e = [kinds[k] for k in early] + ['col'] * len(small_names)
    age = start_gather([src0[k] for k in early] + [rows8(a[n]) for n in small_names], kinds_e, src0[early[0]],
                       "gather_layer0_start")
    tok0 = age[-1][0, 0]
    src1 = layer_src(1, tok0)
    pk = lambda pre: _flatten_pad([a[pre + n] + tok0 for n in SMALL], F32)
    small_w, small_m, small_v = pk(''), pk('m_'), pk('v_')
    X = jnp.concatenate([ctx[0] + tok0, x[0] + tok0], axis=0)
    ready = (small_w[0, 0] + small_m[0, 0] + small_v[0, 0] + X[0, 0]
             + sum(t[0, 0, 0].astype(F32) for t in src1)).reshape(1, 1)
    g0 = wait_copies(age[1], age[2], age[3], age[4], age[0], ready, "gather_layer0_wait")
    g0 = forward_halves(g0, kinds_e, "gather_layer0_forward")
    ag0 = start_gather([src0[k] for k in late], [kinds[k] for k in late], g0[0], "gather_layer0_late_start")
    ag1 = start_gather(src1, kinds, ag0[-1], "gather_layer1_start")
    ag_token = ag1[-1]
    full = {n: [None, None] for n in big_names}
    for k, t in zip(early, g0):
        full[big_names[k]][0] = whole(t)
    for n, g in zip(small_names, g0[len(early):]):
        shp = a[n].shape
        full[n] = g[:, :math.prod(shp[1:-1])].reshape(shp[:-1] + (4 * shp[-1],))
    for n in SMALL:
        if n not in SMALL_SHARDED:
            full[n] = a[n]

    cvec = jnp.concatenate([c_ctx.reshape(1, d), c.reshape(1, d), jnp.zeros((6, d), F32)], axis=0)
    avec = (cvec * jax.nn.sigmoid(cvec) + ag_token[0, 0]).astype(MM_DTYPE)

    def row(v):
        return v.reshape(1, -1)

    saved = []
    gk, gv = dm.GK, d
    lrblk = (7 * d + d // 2) // LANES
    for l in range(depth):
        if l == 1:
            got = wait_copies(ag1[1], ag1[2], ag1[3], ag1[4], ag1[0], X, "gather_layer1_wait")
            got = forward_halves(got, kinds, "gather_layer1_forward")
            for n, t in zip(big_names, got):
                full[n][1] = whole(t)
        s = types.SimpleNamespace()
        s.w_in_p = _w_in_t_to_proj(full['w_in'][l], d, wl, wlp)
        wd = full['w_decay'][l]
        wdp = jnp.zeros((LANES, 2 * gk), F32)
        wdp = wdp.at[:GLA_LR, :gk].set(wd[0]).at[GLA_LR:2 * GLA_LR, gk:].set(wd[1])
        s.wdp = wdp.astype(MM_DTYPE)
        s.wdp_wide = jnp.pad(s.wdp, ((0, d // 2 - LANES), (0, 0)))
        s.bd = full['b_decay'][l].reshape(1, 2 * gk)
        modraw = matmul(avec, full['w_ada'][l], 'nn', F32, f"mod_{l}") + full['b_ada'][l][None, :]
        s.mod = [modraw[0:2, j * d:(j + 1) * d].reshape(2, 1, d) for j in range(6)]
        s.x = X
        (s.h,) = rowwise(pre_fn, [X], s.mod[0:2], [row(g_pre_mix[l])], [(d, MM_DTYPE)], dm, f"pre_{l}")
        s.P = matmul(s.h, s.w_in_p, 'nt', MM_DTYPE, f"in_proj_{l}")
        P = s.P
        s.z = matmul((P, LANES, lrblk), s.wdp, 'nn', F32, f"decay_proj_{l}", tk=LANES)
        la_f, la_b = rowwise(decay_fn, [s.z], [], [s.bd], [(gk, F32), (gk, F32)], dm, f"decay_{l}")
        s.la = jnp.concatenate([la_f, la_b], axis=1)
        s.o_f, s.st_f = gla_fwd(P, s.la, False, dm, f"gla_fwd_f_{l}")
        s.o_b, s.st_b = gla_fwd(P, s.la, True, dm, f"gla_fwd_b_{l}")
        (s.gin,) = rowwise(glaout_fn, [s.o_f, s.o_b, (P, d, 3)], [], [row(g_gla[l])], [(gv, MM_DTYPE)], dm,
                           f"gla_out_{l}")
        if l == 0:
            got = wait_copies(ag0[1], ag0[2], ag0[3], ag0[4], ag0[0], s.gin, "gather_layer0_late_wait")
            got = forward_halves(got, [kinds[k] for k in late], "gather_layer0_late_forward")
            for k, t in zip(late, got):
                full[big_names[k]][0] = whole(t)
        s.ya = matmul(s.gin, full['w_gla_o'][l], 'nn', MM_DTYPE, f"gla_o_{l}")
        (s.u,) = rowwise(glu_fn, [(P, d, 6)], [], [], [(d // 2, F32)], dm, f"glu_{l}")
        s.yconv = conv_fwd(s.u, full['w_dw'][l], dm, f"conv_{l}")
        (s.cin,) = rowwise(convpost_fn, [s.yconv], [], [row(b_dw[l]), row(g_conv_ln[l]), row(b_conv_ln[l])],
                           [(d // 2, MM_DTYPE)], dm, f"conv_post_{l}")
        s.yb = matmul(s.cin, full['w_conv_o'][l], 'nn', MM_DTYPE, f"conv_o_{l}")
        s.pm = pool_mix((P, d // 2, 14), False, dm, f"pool_mix_{l}")
        s.pc = group_mm(s.pm, w_pool_g[l], 'nn', F32, f"pool_g_{l}")
        (s.pin,) = rowwise(poolpost_fn, [s.pc], [], [row(s_pool[l])], [(d // 2, MM_DTYPE)], dm, f"pool_post_{l}")
        s.yc = matmul(s.pin, full['w_pool_o'][l], 'nn', MM_DTYPE, f"pool_o_{l}")
        s.bg = [row(full['b_gate'][l][j]) for j in range(3)]
        (s.mixed,) = rowwise(merge_fn, [s.ya, s.yb, s.yc, (P, 3 * d, 0)], [], s.bg, [(d, MM_DTYPE)], dm,
                             f"merge_{l}", tm=tmw)
        s.y = matmul(s.mixed, full['w_out'][l], 'nn', MM_DTYPE, f"out_proj_{l}")
        s.x1, s.h2 = rowwise(mid_fn, [X, s.y], s.mod[2:5], [row(g_post_mix[l]), row(g_pre_mlp[l])],
                             [(d, F32), (d, MM_DTYPE)], dm, f"mid_{l}")
        s.act = matmul(s.h2, full['w_mlp1'][l], 'nn', MM_DTYPE, f"mlp1_{l}", epi=relu2_epi)
        s.y2 = matmul(s.act, full['w_mlp2'][l], 'nn', MM_DTYPE, f"mlp2_{l}")
        (X,) = rowwise(post_fn, [s.x1, s.y2], s.mod[5:6], [row(g_post_mlp[l])], [(d, F32)], dm, f"post_{l}")
        saved.append(s)

    dX, lossv = loss_head(X, loss_target[0], dm, "loss_head")
    loss = lax.psum(lossv[0, 0], ("x", "y", "c"))

    grads = {n: [None] * depth for n in WEIGHTS if n != 'c_ctx' and n not in BIG}
    gbig = {n: [None] * depth for n in BIG}
    rs_token = None

    def start_scatter(idx, layer, after, name):
        gs = [gbig[big_names[k]][layer] for k in idx]
        wd = [t.shape[1] // 4 if kinds[k] == 'col' else t.shape[0] // 4 for t, k in zip(gs, idx)]
        plan = _scatter_plan([big_axis[big_names[k]] - 1 for k in idx], wd)
        lands = [lax.empty((3, t.shape[0], w) if kinds[k] == 'col' else (3, w, t.shape[1]), t.dtype)
                 for t, w, k in zip(gs, wd, idx)]
        return (plan,) + start_copies(gs, lands, plan, 3 * len(gs), after, name)

    g_cctx = jnp.zeros((d,), F32)
    for l in reversed(range(depth)):
        s = saved[l]
        P = s.P
        dmod = [None] * 6
        gpm = row(g_post_mlp[l]) if rs_token is None else row(g_post_mlp[l]) + rs_token[0, 0]
        (dx1, dy2), (dmod[5],), (dg,) = rowwise_vjp(post_fn, [s.x1, s.y2], s.mod[5:6], [gpm], [dX],
                                                     dm, f"post_bwd_{l}", narrow=(1,))
        grads['g_post_mlp'][l] = dg[0]
        du1 = matmul(dy2, full['w_mlp2'][l], 'nt', MM_DTYPE, f"mlp2_dx_{l}", epi=relu2_bwd_epi, extras=[s.act])
        gbig['w_mlp2'][l] = matmul(s.act, dy2, 'tn', MM_DTYPE, f"mlp2_dw_{l}")
        dh2 = matmul(du1, full['w_mlp1'][l], 'nt', MM_DTYPE, f"mlp1_dx_{l}")
        gbig['w_mlp1'][l] = matmul(s.h2, du1, 'tn', MM_DTYPE, f"mlp1_dw_{l}")
        gpx = row(g_post_mix[l])
        (dxa, dy), dmod[2:5], (dg1, dg2) = rowwise_vjp(
            mid_fn, [s.x, s.y], s.mod[2:5], [gpx, row(g_pre_mlp[l])], [dx1, dh2], dm, f"mid_bwd_{l}", narrow=(1,))
        grads['g_post_mix'][l], grads['g_pre_mlp'][l] = dg1[0], dg2[0]
        dmixed = matmul(dy, full['w_out'][l], 'nt', MM_DTYPE, f"out_proj_dx_{l}")
        gbig['w_out'][l] = matmul(s.mixed, dy, 'tn', MM_DTYPE, f"out_proj_dw_{l}")
        (dya, dyb, dyc, dP), _, dbg = rowwise_vjp(merge_fn, [s.ya, s.yb, s.yc, (P, 3 * d, 0)], [], s.bg, [dmixed],
                                                  dm, f"merge_bwd_{l}", tm=tmw, narrow=(0, 1, 2),
                                                  into=(3, None, P.shape))
        grads['b_gate'][l] = jnp.concatenate(dbg, axis=0)
        dgin = matmul(dya, full['w_gla_o'][l], 'nt', MM_DTYPE, f"gla_o_dx_{l}")
        gbig['w_gla_o'][l] = matmul(s.gin, dya, 'tn', MM_DTYPE, f"gla_o_dw_{l}")
        dcin = matmul(dyb, full['w_conv_o'][l], 'nt', MM_DTYPE, f"conv_o_dx_{l}")
        gbig['w_conv_o'][l] = matmul(s.cin, dyb, 'tn', MM_DTYPE, f"conv_o_dw_{l}")
        dpin = matmul(dyc, full['w_pool_o'][l], 'nt', MM_DTYPE, f"pool_o_dx_{l}")
        gbig['w_pool_o'][l] = matmul(s.pin, dyc, 'tn', MM_DTYPE, f"pool_o_dw_{l}")
        sp = row(s_pool[l])
        if l == 0:
            rs0 = start_scatter(late, 0, dpin, "grad_layer0_late_start")
            sp = sp + rs0[-1][0, 0]
        (dpc,), _, (dsp,) = rowwise_vjp(poolpost_fn, [s.pc], [], [sp], [dpin], dm, f"pool_post_bwd_{l}")
        grads['s_pool'][l] = dsp[0]
        grads['w_pool_g'][l] = group_mm(s.pm, w_pool_g[l], 'tn', F32, f"pool_g_dw_{l}", b=dpc)
        dpm = group_mm(dpc, w_pool_g[l], 'nt', F32, f"pool_g_dx_{l}")
        dP = pool_mix(dpm, True, dm, f"pool_mix_bwd_{l}", into=(dP, 14))
        (dyconv,), _, (dbdw, dgln, dbln) = rowwise_vjp(
            convpost_fn, [s.yconv], [], [row(b_dw[l]), row(g_conv_ln[l]), row(b_conv_ln[l])], [dcin], dm,
            f"conv_post_bwd_{l}")
        grads['b_dw'][l], grads['g_conv_ln'][l], grads['b_conv_ln'][l] = dbdw[0], dgln[0], dbln[0]
        du, grads['w_dw'][l] = conv_bwd(s.u, full['w_dw'][l], dyconv, dm, f"conv_bwd_{l}")
        (dP,), _, _ = rowwise_vjp(glu_fn, [(P, d, 6)], [], [], [du], dm, f"glu_bwd_{l}", into=(0, dP, P.shape))
        (do, _, dP), _, (dgg,) = rowwise_vjp(glaout_fn, [s.o_f, s.o_b, (P, d, 3)], [], [row(g_gla[l])], [dgin], dm,
                                             f"gla_out_bwd_{l}", want=[True, False, True], into=(2, dP, P.shape))
        grads['g_gla'][l] = dgg[0]
        dqf, dkf, dvf, dlaf = gla_bwd(P, s.la, do, s.st_f, False, dm, f"gla_bwd_f_{l}")
        dP, dlab = gla_bwd(P, s.la, do, s.st_b, True, dm, f"gla_bwd_b_{l}", prev=(dqf, dkf, dvf), into=dP)
        (dz,), _, (dbd,) = rowwise_vjp(decay_fn, [s.z], [], [s.bd], [dlaf, dlab], dm, f"decay_bwd_{l}", narrow=(0,))
        grads['b_decay'][l] = dbd.reshape(2, gk)
        dwdp = matmul((P, LANES, lrblk), dz, 'tn', F32, f"decay_proj_dw_{l}", tm=LANES)
        grads['w_decay'][l] = jnp.stack([dwdp[:GLA_LR, :gk], dwdp[GLA_LR:2 * GLA_LR, gk:]])
        dP = matmul(dz, s.wdp_wide, 'nt', MM_DTYPE, f"decay_proj_dx_{l}", into=(dP, 15))
        dh = matmul(dP, s.w_in_p, 'nn', MM_DTYPE, f"in_proj_dx_{l}")
        gbig['w_in'][l] = _proj_to_w_in_t(matmul(dP, s.h, 'tn', MM_DTYPE, f"in_proj_dw_{l}"), d, wl, wlp)
        (dX,), dmod[0:2], (dg,) = rowwise_vjp(pre_fn, [s.x], s.mod[0:2], [row(g_pre_mix[l])], [dh], dm,
                                               f"pre_bwd_{l}", adds={0: dxa})
        grads['g_pre_mix'][l] = dg[0]
        dmodflat = jnp.concatenate([jnp.concatenate([m_.reshape(2, d) for m_ in dmod], axis=1),
                                    jnp.zeros((6, 6 * d), F32)], axis=0)
        grads['b_ada'][l] = dmodflat[0] + dmodflat[1]
        gbig['w_ada'][l] = matmul(avec, dmodflat, 'tn', MM_DTYPE, f"ada_dw_{l}")
        dav = matmul(dmodflat, full['w_ada'][l], 'nt', F32, f"ada_dx_{l}")
        g_cctx = g_cctx + dav[0] * _silu_grad(c_ctx)
        if l == 1:
            rs1 = start_scatter(list(range(nbig)), 1, dav, "grad_layer1_start")
            rs_token = rs1[-1]

    grad_x = dX[dm.CTX:][None]
    gfull = {n: jnp.stack(v) for n, v in grads.items()}
    gfull['c_ctx'] = g_cctx
    where = jnp.concatenate([chip1, core1])

    def halves_view(t, k):
        return t.reshape(2, t.shape[0] // 2, t.shape[1]) if k == 'col' else t.reshape(4, 2, t.shape[0] // 8, t.shape[1])
    enames = [big_names[k] for k in early]
    ekinds = [kinds[k] for k in early]
    v0 = [halves_view(gbig[n][0], k) for n, k in zip(enames, ekinds)]
    r1 = pair_swap_halves(v0, ekinds, "grad_pair_swap")
    hs = [pair_add(v.reshape((-1,) + v.shape[-2:]), r.reshape((-1,) + r.shape[-2:]), core1, f"grad_pair_add_{n}")
          for n, v, r in zip(enames, v0, r1)]
    hx = [h.reshape(h.shape[1:]) if k == 'col' else h for h, k in zip(hs, ekinds)]
    ex_plan = _exchange_plan(ekinds)
    ex_lands = [lax.empty((3, h.shape[0], h.shape[1] // 4) if k == 'col' else (3,) + h.shape[1:], h.dtype)
                for h, k in zip(hx, ekinds)]
    ex = (ex_plan,) + start_copies(hx, ex_lands, ex_plan, 3 * len(hx), hx[0], "grad_chip_exchange_start")

    got0 = wait_copies(rs0[1], rs0[2], rs0[3], rs0[4], rs0[0], ex[-1], "grad_layer0_late_wait")
    got1 = wait_copies(rs1[1], rs1[2], rs1[3], rs1[4], rs1[0], ex[-1], "grad_layer1_wait")
    sa = [chip_add(g, r, big_axis[big_names[k]] - 1, where, f"grad_layer0_add_{big_names[k]}", slab=False)
          for k, g, r in zip(late, rs0[3], got0)]
    sa += [chip_add(g, r, big_axis[n] - 1, where, f"grad_layer1_add_{n}", slab=False)
           for n, g, r in zip(big_names, rs1[3], got1)]
    sb = pair_swap(sa, "grad_late_pair_swap")
    red0 = {big_names[k]: [sa[j], sb[j]] for j, k in enumerate(late)}
    red1 = {n: [sa[len(late) + k], sb[len(late) + k]] for k, n in enumerate(big_names)}

    sflat = _flatten_pad([gfull[n].astype(F32) for n in SMALL], F32)
    sv = sflat.reshape(2, sflat.shape[0] // 2, LANES)
    (sr,) = pair_swap_halves([sv], ['col'], "small_grad_pair_swap")
    sh = pair_add(sv, sr[None], core1, "small_grad_pair_add")[0]
    sq = quad_sum(sh, chip_broadcast(sh, "small_grad_chip_exchange"), core1, "small_grad_chip_sum")
    (ssum,) = pair_join_layers([sq], "small_grad_pair_join")
    ssum = ssum.reshape(-1)

    out_g, out_d, out_m, out_v = {}, {}, {}, {}

    def update_big(n, terms, **kw):
        res = adamw_layers(a[n], a['m_' + n], a['v_' + n], terms, f"adamw_{n}" + ("" if not kw else f"_{kw['layer']}"), **kw)
        out_g[n], out_d[n], out_m[n], out_v[n] = res
        return res
    for k in late:
        update_big(big_names[k], [red0[big_names[k]], red1[big_names[k]]])
    half_done = {n: update_big(n, {1: red1[n]}, layer=1) for n in enames}
    start = 0
    sg = {}
    for n in SMALL:
        cnt = gfull[n].size
        g = ssum[start:start + cnt].reshape(gfull[n].shape)
        start += cnt
        if n in SMALL_SHARDED:
            ax = SMALL_SHARDED[n]
            wdt = a[n].shape[ax]
            g = lax.dynamic_slice_in_dim(g, chip * wdt, wdt, axis=ax)
        sg[n] = g
    gs = _flatten_pad([sg[n] for n in SMALL], F32)
    dl, mn, vn = adamw(small_w, gs, small_m, small_v, "adamw_small")
    done = (dl[0, 0] + sum(out_d[n][1, 0, 0] for n in big_names)).reshape(1, 1)
    r2 = wait_copies(ex[1], ex[2], ex[3], ex[4], ex[0], done, "grad_chip_exchange_wait")
    dl, mn, vn = dl.reshape(-1), mn.reshape(-1), vn.reshape(-1)
    start = 0
    for n in SMALL:
        cnt, shp = a[n].size, a[n].shape
        out_g[n] = sg[n]
        out_d[n], out_m[n], out_v[n] = (t[start:start + cnt].reshape(shp) for t in (dl, mn, vn))
        start += cnt
    fs = [chip_add(h.reshape(-1, h.shape[-1]), r, big_axis[n] - 1, where, f"grad_chip_add_{n}")
          for n, h, r in zip(enames, ex[3], r2)]
    for n, t in zip(enames, pair_join_layers(fs, "grad_pair_join")):
        update_big(n, {0: [t.reshape(-1, t.shape[-1])]}, layer=0, prev=tuple(half_done[n]))
    for dct in (out_g, out_d, out_m, out_v):
        dct['w_in'] = jnp.swapaxes(dct['w_in'], 1, 2)
    return (loss, grad_x, *[out_g[n] for n in WEIGHTS], *[out_d[n] for n in WEIGHTS],
            *[out_m[n] for n in WEIGHTS], *[out_v[n] for n in WEIGHTS])
```

```python
import functools
import math
import types

import jax
import jax.numpy as jnp
from jax import lax
from jax.experimental import pallas as pl
from jax.experimental.pallas import tpu as pltpu

F32 = jnp.float32
MM_DTYPE = jnp.bfloat16
VMEM_LIMIT_V7X = 56 * 1024 * 1024
LANES = 128
EPS = 1e-6

N_HEADS = 4
GLA_CHUNK = 64
GLA_TAU = 16.0
GLA_LR = 16
GRID_W = 64
POOL_WINDOWS = (2, 4, 8, 16)

ADAM_LR = 0.001
ADAM_B1 = 0.9
ADAM_B2 = 0.999
ADAM_EPS = 1e-08
ADAM_WD = 0.01
ADAM_STEP = 10

NN = (((1,), (0,)), ((), ()))
NT = (((1,), (1,)), ((), ()))
TN = (((0,), (0,)), ((), ()))

WEIGHTS = ['c_ctx', 'w_ada', 'b_ada', 'g_pre_mix', 'g_post_mix', 'g_pre_mlp', 'g_post_mlp', 'w_in', 'w_decay',
           'b_decay', 'g_gla', 'w_gla_o', 'w_dw', 'b_dw', 'g_conv_ln', 'b_conv_ln', 'w_conv_o', 'w_pool_g',
           's_pool', 'w_pool_o', 'b_gate', 'w_out', 'w_mlp1', 'w_mlp2']
BIG = {'w_ada': 2, 'w_in': 2, 'w_gla_o': 1, 'w_conv_o': 2, 'w_pool_o': 2, 'w_out': 1, 'w_mlp1': 2, 'w_mlp2': 1}
SMALL_SHARDED = {'w_decay': 3, 'b_decay': 2, 'w_dw': 2, 'b_gate': 2}
SMALL = [n for n in WEIGHTS if n not in BIG]


def _tile(n, prefs):
    for t in prefs:
        if n % t == 0:
            return t
    return n


def _cparams(sem=None, **kw):
    return pltpu.CompilerParams(dimension_semantics=sem, vmem_limit_bytes=VMEM_LIMIT_V7X, **kw)


def _dot(a, b, dims=NN):
    return lax.dot_general(a.astype(MM_DTYPE), b.astype(MM_DTYPE), dims, preferred_element_type=F32)


def matmul(a, b, mode, out_dtype, name, tm=None, tn=None, tk=None, epi=None, extras=(), into=None):
    a, aw, ablk = a if isinstance(a, tuple) else (a, a.shape[1], 0)
    if mode == 'nn':
        M, K, N = a.shape[0], aw, b.shape[1]
    elif mode == 'nt':
        M, K, N = a.shape[0], aw, b.shape[0]
    else:
        K, M, N = a.shape[0], aw, b.shape[1]
    big = (1088, 1024, 640, 544, 512, 320, 256, 128, 64, 32, 16, 8)
    if mode == 'tn':
        tm = tm or _tile(M, (1024, 512, 256, 128))
        tn = tn or _tile(N, (1024, 512, 256, 128))
        tk = tk or _tile(K, big)
    else:
        tm = tm or _tile(M, big)
        tn = tn or _tile(N, (1024, 512, 256, 128))
        tk = tk or _tile(K, (1024, 512, 256, 128))
    if aw != a.shape[1]:
        assert (mode == 'tn' and tm == aw) or (mode != 'tn' and tk == aw)
    nk = K // tk
    ne = len(extras)
    dims = {'nn': NN, 'nt': NT, 'tn': TN}[mode]

    def body(a_ref, b_ref, *rest):
        e_refs, o_ref = rest[:ne], rest[ne + (into is not None)]

        def finish(acc):
            if epi is not None:
                acc = epi(acc, *[e[...] for e in e_refs])
            o_ref[...] = acc.astype(o_ref.dtype)

        p = _dot(a_ref[...], b_ref[...], dims)
        if nk == 1:
            finish(p)
            return
        acc = rest[-1]
        k = pl.program_id(2)

        @pl.when(k == 0)
        def _():
            acc[...] = p

        @pl.when(k > 0)
        def _():
            acc[...] += p

        @pl.when(k == nk - 1)
        def _():
            finish(acc[...])

    if mode == 'nn':
        a_spec = pl.BlockSpec((tm, tk), lambda i, j, k: (i, k + ablk))
        b_spec = pl.BlockSpec((tk, tn), lambda i, j, k: (k, j))
    elif mode == 'nt':
        a_spec = pl.BlockSpec((tm, tk), lambda i, j, k: (i, k + ablk))
        b_spec = pl.BlockSpec((tn, tk), lambda i, j, k: (j, k))
    else:
        a_spec = pl.BlockSpec((tk, tm), lambda i, j, k: (k, i + ablk))
        b_spec = pl.BlockSpec((tk, tn), lambda i, j, k: (k, j))
    tile = pl.BlockSpec((tm, tn), lambda i, j, k: (i, j))
    if into is None:
        out_spec, out_shape, more, extra, aliases = tile, jax.ShapeDtypeStruct((M, N), out_dtype), [], [], {}
    else:
        buf, oblk = into
        out_spec = pl.BlockSpec((tm, tn), lambda i, j, k: (i, oblk * (N // tn) + j))
        out_shape = jax.ShapeDtypeStruct(buf.shape, buf.dtype)
        more, extra, aliases = [pl.BlockSpec(memory_space=pl.ANY)], [buf], {2 + ne: 0}
    return pl.pallas_call(
        body, name=name, grid=(M // tm, N // tn, nk),
        in_specs=[a_spec, b_spec] + [tile] * ne + more, out_specs=out_spec,
        out_shape=out_shape, input_output_aliases=aliases,
        scratch_shapes=[] if nk == 1 else [pltpu.VMEM((tm, tn), F32)],
        compiler_params=_cparams(("parallel", "parallel", "arbitrary")),
    )(a, b, *extras, *extra)


def group_mm(a, w, mode, out_dtype, name, b=None):
    T = a.shape[0]
    G, gc, _ = w.shape
    col = pl.BlockSpec((T, gc), lambda g: (0, g))
    wsp = pl.BlockSpec((1, gc, gc), lambda g: (g, 0, 0))
    if mode == 'tn':
        def body(a_ref, b_ref, o_ref):
            o_ref[0] = _dot(a_ref[...], b_ref[...], TN).astype(o_ref.dtype)
        return pl.pallas_call(body, name=name, grid=(G,), in_specs=[col, col], out_specs=wsp,
                              out_shape=jax.ShapeDtypeStruct((G, gc, gc), out_dtype),
                              compiler_params=_cparams(("parallel",)))(a, b)
    dims = NN if mode == 'nn' else NT

    def body(a_ref, w_ref, o_ref):
        o_ref[...] = _dot(a_ref[...], w_ref[0], dims).astype(o_ref.dtype)
    return pl.pallas_call(body, name=name, grid=(G,), in_specs=[col, wsp], out_specs=col,
                          out_shape=jax.ShapeDtypeStruct((T, G * gc), out_dtype),
                          compiler_params=_cparams(("parallel",)))(a, w)


def _rowspec(r):
    return r if isinstance(r, tuple) else (r, r.shape[1], 0)


def _row_specs(rows, segs, consts, tm, nctx):
    specs = [pl.BlockSpec((tm, w), lambda i, b=b: (i, b)) for _, w, b in rows]
    specs += [pl.BlockSpec((1,) + s.shape[1:], lambda i, n=s.ndim: (jnp.where(i >= nctx, 1, 0),) + (0,) * (n - 1))
              for s in segs]
    specs += [pl.BlockSpec(c.shape, lambda i, n=c.ndim: (0,) * n) for c in consts]
    return specs


def rowwise(fn, rows, segs, consts, outs, dm, name, tm=None):
    tm = tm or dm.tm
    nctx = dm.CTX // tm
    rows = [_rowspec(r) for r in rows]
    nr, ns, nc = len(rows), len(segs), len(consts)

    def body(*refs):
        rin = [r[...] for r in refs[:nr]]
        sin = [s[0] for s in refs[nr:nr + ns]]
        cin = [c[...] for c in refs[nr + ns:nr + ns + nc]]
        res = fn(*rin, *sin, *cin)
        for o_ref, v in zip(refs[nr + ns + nc:], res):
            o_ref[...] = v.astype(o_ref.dtype)

    res = pl.pallas_call(
        body, name=name, grid=(dm.T // tm,),
        in_specs=_row_specs(rows, segs, consts, tm, nctx),
        out_specs=[pl.BlockSpec((tm, w), lambda i: (i, 0)) for w, _ in outs],
        out_shape=[jax.ShapeDtypeStruct((dm.T, w), dt) for w, dt in outs],
        compiler_params=_cparams(("parallel",)),
    )(*[r[0] for r in rows], *segs, *consts)
    return res


def rowwise_vjp(fn, rows, segs, consts, cots, dm, name, tm=None, want=None, adds=None, narrow=(), into=None):
    tm = tm or dm.tm
    nctx = dm.CTX // tm
    rows = [_rowspec(r) for r in rows]
    cots = [_rowspec(r) for r in cots]
    adds = adds or {}
    nr, ns, nc, nct = len(rows), len(segs), len(consts), len(cots)
    want = want or [True] * nr
    widx = [k for k in range(nr) if want[k]]
    akeys = sorted(adds)

    def body(*refs):
        i = pl.program_id(0)
        rin = [r[...] for r in refs[:nr]]
        sin = [s[0] for s in refs[nr:nr + ns]]
        cin = [c[...] for c in refs[nr + ns:nr + ns + nc]]
        p = nr + ns + nc
        cot_refs = refs[p:p + nct]
        add_refs = dict(zip(akeys, refs[p + nct:p + nct + len(akeys)]))
        p = p + nct + len(akeys) + (1 if (into is not None and into[1] is not None) else 0)
        rg_refs = refs[p:p + len(widx)]
        sg_refs = refs[p + len(widx):p + len(widx) + ns]
        cg_refs = refs[p + len(widx) + ns:]
        res, vjp = jax.vjp(fn, *rin, *sin, *cin)
        g = vjp(tuple(cr[...].astype(o.dtype) for cr, o in zip(cot_refs, res)))
        for o_ref, k in zip(rg_refs, widx):
            v = g[k].astype(F32)
            if k in add_refs:
                v = v + add_refs[k][...]
            o_ref[...] = v.astype(o_ref.dtype)
        first_seg = jnp.logical_or(i == 0, i == nctx)
        for o_ref, v in zip(sg_refs, g[nr:nr + ns]):
            @pl.when(first_seg)
            def _(o_ref=o_ref, v=v):
                o_ref[0] = v.astype(F32)

            @pl.when(jnp.logical_not(first_seg))
            def _(o_ref=o_ref, v=v):
                o_ref[0] += v.astype(F32)
        for o_ref, v in zip(cg_refs, g[nr + ns:]):
            @pl.when(i == 0)
            def _(o_ref=o_ref, v=v):
                o_ref[...] = v.astype(F32)

            @pl.when(i > 0)
            def _(o_ref=o_ref, v=v):
                o_ref[...] += v.astype(F32)

    in_specs = _row_specs(rows, segs, consts, tm, nctx)
    in_specs += [pl.BlockSpec((tm, w), lambda i, b=b: (i, b)) for _, w, b in cots]
    in_specs += [pl.BlockSpec((tm, adds[k].shape[1]), lambda i: (i, 0)) for k in akeys]
    out_specs = [pl.BlockSpec((tm, rows[k][1]), lambda i: (i, 0)) for k in widx]
    out_shape = [jax.ShapeDtypeStruct((dm.T, rows[k][1]), MM_DTYPE if k in narrow else rows[k][0].dtype)
                 for k in widx]
    extra, aliases = [], {}
    if into is not None:
        ik, ibuf, ishape = into
        out_specs[widx.index(ik)] = pl.BlockSpec((tm, rows[ik][1]), lambda i, b=rows[ik][2]: (i, b))
        out_shape[widx.index(ik)] = jax.ShapeDtypeStruct(ishape, MM_DTYPE)
        if ibuf is not None:
            aliases = {len(in_specs): widx.index(ik)}
            in_specs = in_specs + [pl.BlockSpec(memory_space=pl.ANY)]
            extra = [ibuf]
    out_specs += [pl.BlockSpec((1,) + s.shape[1:], lambda i, n=s.ndim: (jnp.where(i >= nctx, 1, 0),) + (0,) * (n - 1))
                  for s in segs]
    out_shape += [jax.ShapeDtypeStruct(s.shape, F32) for s in segs]
    out_specs += [pl.BlockSpec(c.shape, lambda i, n=c.ndim: (0,) * n) for c in consts]
    out_shape += [jax.ShapeDtypeStruct(c.shape, F32) for c in consts]
    res = pl.pallas_call(
        body, name=name, grid=(dm.T // tm,), in_specs=in_specs, out_specs=out_specs, out_shape=out_shape,
        input_output_aliases=aliases, compiler_params=_cparams(("arbitrary",)),
    )(*[r[0] for r in rows], *segs, *consts, *[r[0] for r in cots], *[adds[k] for k in akeys], *extra)
    rg = [None] * nr
    for k, v in zip(widx, res[:len(widx)]):
        rg[k] = v
    return rg, list(res[len(widx):len(widx) + ns]), list(res[len(widx) + ns:])


def _rms(x, g):
    return x * lax.rsqrt(jnp.mean(x * x, axis=-1, keepdims=True) + EPS) * g


def _sigmoid(x):
    return jax.nn.sigmoid(x)


def pre_fn(x, shift, scale, g):
    return ((_rms(x, g) * (1.0 + scale) + shift).astype(MM_DTYPE),)


def mid_fn(x, y, gate, shift, scale, g_post, g_pre):
    x1 = x + gate * _rms(y.astype(F32), g_post)
    return x1, (_rms(x1, g_pre) * (1.0 + scale) + shift).astype(MM_DTYPE)


def post_fn(x1, y2, gate, g):
    return (x1 + gate * _rms(y2.astype(F32), g),)


def relu2_epi(acc):
    r = jnp.maximum(acc, 0.0)
    return r * r


def relu2_bwd_epi(dact, act):
    return dact * (2.0 * jnp.sqrt(act.astype(F32)))


def decay_fn(z, bd):
    zz = z.astype(F32) + bd
    ls = jnp.minimum(zz, 0.0) - jnp.log(1.0 + jnp.exp(jnp.minimum(zz, -zz)))
    la = ls / GLA_TAU
    gk = la.shape[1] // 2
    return la[:, :gk], la[:, gk:]


def glu_fn(ab):
    h = ab.shape[1] // 2
    return (ab[:, :h].astype(F32) * _sigmoid(ab[:, h:].astype(F32)),)


def glaout_fn(o_f, o_b, og, g):
    o = o_f + o_b
    dv = o.shape[1] // N_HEADS
    hs = []
    for h in range(N_HEADS):
        oh = o[:, h * dv:(h + 1) * dv]
        hs.append(oh * lax.rsqrt(jnp.mean(oh * oh, axis=-1, keepdims=True) + EPS) * g[:, h * dv:(h + 1) * dv])
    og = og.astype(F32)
    return ((jnp.concatenate(hs, axis=1) * (og * _sigmoid(og))).astype(MM_DTYPE),)


def convpost_fn(y, b_dw, g, b):
    y = y + b_dw
    mu = jnp.mean(y, axis=-1, keepdims=True)
    xc = y - mu
    yn = xc * lax.rsqrt(jnp.mean(xc * xc, axis=-1, keepdims=True) + EPS) * g + b
    return ((yn * _sigmoid(yn)).astype(MM_DTYPE),)


def poolpost_fn(pc, s):
    return ((pc.astype(F32) * s).astype(MM_DTYPE),)


def merge_fn(ya, yb, yc, mg, bg0, bg1, bg2):
    d = ya.shape[1]
    mg = mg.astype(F32)
    mixed = (_sigmoid(mg[:, :d] + bg0) * ya.astype(F32) + _sigmoid(mg[:, d:2 * d] + bg1) * yb.astype(F32)
             + _sigmoid(mg[:, 2 * d:] + bg2) * yc.astype(F32))
    return (mixed.astype(MM_DTYPE),)


def _split_dot(lmat, x, dims):
    hi = x.astype(MM_DTYPE)
    lo = x - hi.astype(F32)
    return _dot(lmat, hi, dims) + _dot(lmat, lo, dims)


def _gla_block_order(dm, rev):
    nctx, nb = dm.CTX // dm.TB, dm.T // dm.TB

    def blk(i):
        if not rev:
            return i
        return jnp.where(i < nctx, nctx - 1 - i, nb - 1 - (i - nctx))
    return blk, nb


def _gla_tri(rev):
    c = GLA_CHUNK
    t = lax.broadcasted_iota(jnp.int32, (c, c), 0)
    s = lax.broadcasted_iota(jnp.int32, (c, c), 1)
    return (s >= t) if rev else (s <= t)


def _gla_cumsum(la, tri):
    lmat = tri.astype(MM_DTYPE)
    return lmat, _split_dot(lmat, la, NN), jnp.sum(la, axis=0, keepdims=True)


def _gla_chunk_terms(q, k, b, bend, tri, scale):
    eb = jnp.exp(b)
    enb = jnp.exp(-b)
    ee = jnp.exp(bend - b)
    qi = q * scale * eb
    ki = k * enb
    kend = k * ee
    att = jnp.where(tri, _dot(qi, ki, NT), 0.0)
    return eb, enb, ee, qi, ki, kend, att


def gla_fwd(P, la, rev, dm, name):
    c, tb, h_, dk, dv, d = GLA_CHUNK, dm.TB, N_HEADS, dm.DK, dm.DV, dm.D
    cpb = tb // c
    blk, nb = _gla_block_order(dm, rev)
    gk, gv = h_ * dk, h_ * dv
    qb, kb, vb, lb = (5 * d) // gk, (5 * d + d // 2) // gk, (4 * d) // gv, (1 if rev else 0)
    scale = dk ** -0.5
    order = list(range(cpb))[::-1] if rev else list(range(cpb))

    def body(q_ref, k_ref, v_ref, la_ref, o_ref, s_ref, st):
        @pl.when(pl.program_id(0) == 0)
        def _():
            st[...] = jnp.zeros_like(st)
        tri = _gla_tri(rev)
        terms = {}
        for n, ci in enumerate(order):
            r = pl.ds(ci * c, c)
            _, b_all, bend_all = _gla_cumsum(la_ref[r, :], tri)
            for hh in range(h_):
                ck, cv = pl.ds(hh * dk, dk), pl.ds(hh * dv, dv)
                hs = slice(hh * dk, (hh + 1) * dk)
                v = v_ref[r, cv]
                _, _, _, qi, _, kend, att = _gla_chunk_terms(
                    q_ref[r, ck].astype(F32), k_ref[r, ck].astype(F32), b_all[:, hs], bend_all[:, hs], tri, scale)
                terms[n, hh] = (_dot(att, v), qi.astype(MM_DTYPE), jnp.exp(bend_all[:, hs]), _dot(v, kend, TN))
        for n, ci in enumerate(order):
            r = pl.ds(ci * c, c)
            for hh in range(h_):
                intra, qi, gam, dstate = terms[n, hh]
                s_in = st[hh]
                o_ref[r, pl.ds(hh * dv, dv)] = intra + _dot(qi, s_in, NT)
                s_ref[n, hh] = s_in
                st[hh] = gam * s_in + dstate

    return pl.pallas_call(
        body, name=name, grid=(nb,),
        in_specs=[pl.BlockSpec((tb, gk), lambda i: (blk(i), qb)),
                  pl.BlockSpec((tb, gk), lambda i: (blk(i), kb)),
                  pl.BlockSpec((tb, gv), lambda i: (blk(i), vb)),
                  pl.BlockSpec((tb, gk), lambda i: (blk(i), lb))],
        out_specs=[pl.BlockSpec((tb, gv), lambda i: (blk(i), 0)),
                   pl.BlockSpec((cpb, h_, dv, dk), lambda i: (i, 0, 0, 0))],
        out_shape=[jax.ShapeDtypeStruct((dm.T, gv), F32),
                   jax.ShapeDtypeStruct((dm.T // c, h_, dv, dk), F32)],
        scratch_shapes=[pltpu.VMEM((h_, dv, dk), F32)],
        compiler_params=_cparams(("arbitrary",)),
    )(P, P, P, la)


def gla_bwd(P, la, do, states, rev, dm, name, prev=None, into=None):
    c, tb, h_, dk, dv, d = GLA_CHUNK, dm.TB, N_HEADS, dm.DK, dm.DV, dm.D
    cpb = tb // c
    blk, nb = _gla_block_order(dm, rev)
    gk, gv = h_ * dk, h_ * dv
    qb, kb, vb, lb = (5 * d) // gk, (5 * d + d // 2) // gk, (4 * d) // gv, (1 if rev else 0)
    scale = dk ** -0.5
    order = list(range(cpb))[::-1] if rev else list(range(cpb))

    fused = prev is not None

    def body(q_ref, k_ref, v_ref, la_ref, do_ref, s_ref, *rest):
        if fused:
            pq_ref, pk_ref, pv_ref, _, w_ref, dla_ref, dst = rest
        else:
            dq_ref, dk_ref, dv_ref, dla_ref, dst = rest

        def put(kind, r, cols, val):
            if not fused:
                {'q': dq_ref, 'k': dk_ref, 'v': dv_ref}[kind][r, cols] = val
                return
            p_ref, off = {'q': (pq_ref, gv), 'k': (pk_ref, gv + gk), 'v': (pv_ref, 0)}[kind]
            w_ref[r, pl.ds(off + cols.start, cols.size)] = (val + p_ref[r, cols]).astype(w_ref.dtype)

        @pl.when(pl.program_id(0) == 0)
        def _():
            dst[...] = jnp.zeros_like(dst)
        tri = _gla_tri(rev)
        for n in range(cpb - 1, -1, -1):
            r = pl.ds(order[n] * c, c)
            for hh in range(h_):
                ck, cv = pl.ds(hh * dk, dk), pl.ds(hh * dv, dv)
                q = q_ref[r, ck].astype(F32)
                k = k_ref[r, ck].astype(F32)
                v = v_ref[r, cv]
                lmat, b, bend = _gla_cumsum(la_ref[r, ck], tri)
                eb, enb, ee, qi, ki, kend, att = _gla_chunk_terms(q, k, b, bend, tri, scale)
                s_in = s_ref[n, hh]
                ds_out = dst[hh]
                dob = do_ref[r, cv]
                datt = jnp.where(tri, _dot(dob, v, NT), 0.0)
                dqi = _dot(datt, ki) + _dot(dob, s_in)
                dki = _dot(datt, qi, TN)
                put('v', r, cv, _dot(att, dob, TN) + _dot(kend, ds_out, NT))
                dkend = _dot(v, ds_out)
                gam = jnp.exp(bend)
                dgam = jnp.sum(ds_out * s_in, axis=0, keepdims=True)
                dst[hh] = gam * ds_out + _dot(dob, qi, TN)
                put('q', r, ck, dqi * (scale * eb))
                put('k', r, ck, dki * enb + dkend * ee)
                db = dqi * qi - dki * ki - dkend * kend
                dbend = jnp.sum(dkend * kend, axis=0, keepdims=True) + dgam * gam
                dla_ref[r, ck] = _split_dot(lmat, db, TN) + dbend

    def bi(j):
        return blk(nb - 1 - j)

    in_specs = [
        pl.BlockSpec((tb, gk), lambda j: (bi(j), qb)),
        pl.BlockSpec((tb, gk), lambda j: (bi(j), kb)),
        pl.BlockSpec((tb, gv), lambda j: (bi(j), vb)),
        pl.BlockSpec((tb, gk), lambda j: (bi(j), lb)),
        pl.BlockSpec((tb, gv), lambda j: (bi(j), 0)),
        pl.BlockSpec((cpb, h_, dv, dk), lambda j: (nb - 1 - j, 0, 0, 0)),
    ]
    small = pl.BlockSpec((tb, gk), lambda j: (bi(j), 0))
    wide = pl.BlockSpec((tb, gv), lambda j: (bi(j), 0))
    if not fused:
        return pl.pallas_call(
            body, name=name, grid=(nb,), in_specs=in_specs, out_specs=[small, small, wide, small],
            out_shape=[jax.ShapeDtypeStruct((dm.T, gk), F32), jax.ShapeDtypeStruct((dm.T, gk), F32),
                       jax.ShapeDtypeStruct((dm.T, gv), F32), jax.ShapeDtypeStruct((dm.T, gk), F32)],
            scratch_shapes=[pltpu.VMEM((h_, dv, dk), F32)],
            compiler_params=_cparams(("arbitrary",)),
        )(P, P, P, la, do, states)
    return pl.pallas_call(
        body, name=name, grid=(nb,),
        in_specs=in_specs + [small, small, wide, pl.BlockSpec(memory_space=pl.ANY)],
        out_specs=[pl.BlockSpec((tb, 2 * gv), lambda j: (bi(j), vb // 2)), small],
        out_shape=[jax.ShapeDtypeStruct(into.shape, into.dtype), jax.ShapeDtypeStruct((dm.T, gk), F32)],
        input_output_aliases={9: 0},
        scratch_shapes=[pltpu.VMEM((h_, dv, dk), F32)],
        compiler_params=_cparams(("arbitrary",)),
    )(P, P, P, la, do, states, *prev, into)


def _pos(n, period):
    t = lax.broadcasted_iota(jnp.int32, (n, 1), 0)
    if period & (period - 1) == 0:
        return jnp.bitwise_and(t, period - 1)
    return lax.rem(t, period)


def _conv_segments(dm):
    return [(0, dm.CTX, dm.CTX), (dm.CTX, dm.SEQ, GRID_W)]


def conv_fwd(u, w, dm, name):
    kw, cw = w.shape
    segs = _conv_segments(dm)

    def body(u_ref, w_ref, y_ref):
        for r0, n, per in segs:
            useg = u_ref[r0:r0 + n, :]
            p = _pos(n, per)
            acc = jnp.zeros_like(useg)
            for kk in range(kw):
                d = kk - kw // 2
                sh = useg if d == 0 else pltpu.roll(useg, (-d) % n, 0)
                ok = jnp.logical_and(p + d >= 0, p + d < per)
                acc = acc + jnp.where(ok, sh, 0.0) * w_ref[kk:kk + 1, :]
            y_ref[r0:r0 + n, :] = acc

    return pl.pallas_call(
        body, name=name, grid=(cw // LANES,),
        in_specs=[pl.BlockSpec((dm.T, LANES), lambda j: (0, j)), pl.BlockSpec((kw, LANES), lambda j: (0, j))],
        out_specs=pl.BlockSpec((dm.T, LANES), lambda j: (0, j)),
        out_shape=jax.ShapeDtypeStruct((dm.T, cw), F32),
        compiler_params=_cparams(("parallel",)),
    )(u, w)


def conv_bwd(u, w, dy, dm, name):
    kw, cw = w.shape
    segs = _conv_segments(dm)

    def body(u_ref, w_ref, dy_ref, du_ref, dw_ref):
        dws = [jnp.zeros((1, LANES), F32)] * kw
        for r0, n, per in segs:
            useg = u_ref[r0:r0 + n, :]
            dyseg = dy_ref[r0:r0 + n, :]
            p = _pos(n, per)
            acc = jnp.zeros_like(useg)
            for kk in range(kw):
                d = kk - kw // 2
                shu = useg if d == 0 else pltpu.roll(useg, (-d) % n, 0)
                okf = jnp.logical_and(p + d >= 0, p + d < per)
                dws[kk] = dws[kk] + jnp.sum(jnp.where(okf, shu, 0.0) * dyseg, axis=0, keepdims=True)
                shd = dyseg if d == 0 else pltpu.roll(dyseg, d % n, 0)
                okb = jnp.logical_and(p - d >= 0, p - d < per)
                acc = acc + jnp.where(okb, shd, 0.0) * w_ref[kk:kk + 1, :]
            du_ref[r0:r0 + n, :] = acc
        for kk in range(kw):
            dw_ref[kk:kk + 1, :] = dws[kk]

    return pl.pallas_call(
        body, name=name, grid=(cw // LANES,),
        in_specs=[pl.BlockSpec((dm.T, LANES), lambda j: (0, j)), pl.BlockSpec((kw, LANES), lambda j: (0, j)),
                  pl.BlockSpec((dm.T, LANES), lambda j: (0, j))],
        out_specs=[pl.BlockSpec((dm.T, LANES), lambda j: (0, j)), pl.BlockSpec((kw, LANES), lambda j: (0, j))],
        out_shape=[jax.ShapeDtypeStruct((dm.T, cw), F32), jax.ShapeDtypeStruct((kw, cw), F32)],
        compiler_params=_cparams(("parallel",)),
    )(u, w, dy)


def pool_mix(u, transpose, dm, name, into=None):
    u, uw, ublk = _rowspec(u)
    gc = dm.GC
    ng = len(POOL_WINDOWS)
    rows = dm.SEQ // GRID_W
    segs = [(0, dm.CTX, 1, dm.CTX), (dm.CTX, dm.SEQ, GRID_W, rows)]

    def one_group(u_ref, o_ref, win):
        left = win // 2
        right = win - 1 - left
        for r0, n, stride, length in segs:
            useg = u_ref[r0:r0 + n, :].astype(F32)
            t = lax.broadcasted_iota(jnp.int32, (n, 1), 0)
            p = t if stride == 1 else jnp.right_shift(t, stride.bit_length() - 1)
            cnt = (jnp.minimum(p + right + 1, length) - jnp.maximum(p - left, 0)).astype(F32)
            src = useg / cnt if transpose else useg
            acc = jnp.zeros_like(useg)
            for d in range(-left, right + 1):
                dd = -d if transpose else d
                sh = src if d == 0 else pltpu.roll(src, (-dd * stride) % n, 0)
                ok = jnp.logical_and(p + dd >= 0, p + dd < length)
                acc = acc + jnp.where(ok, sh, 0.0)
            o_ref[r0:r0 + n, :] = ((acc - useg) if transpose else (acc / cnt - useg)).astype(o_ref.dtype)

    def body(u_ref, *rest):
        o_ref = rest[-1]
        g = pl.program_id(0)
        for gi, win in enumerate(POOL_WINDOWS):
            @pl.when(g == gi)
            def _(win=win):
                one_group(u_ref, o_ref, win)

    base = ublk * (uw // gc)
    if into is None:
        obase, out_shape, more, extra, aliases = 0, jax.ShapeDtypeStruct((dm.T, ng * gc), F32), [], [], {}
    else:
        buf, oblk = into
        obase, out_shape = oblk * ng, jax.ShapeDtypeStruct(buf.shape, buf.dtype)
        more, extra, aliases = [pl.BlockSpec(memory_space=pl.ANY)], [buf], {1: 0}
    return pl.pallas_call(
        body, name=name, grid=(ng,),
        in_specs=[pl.BlockSpec((dm.T, gc), lambda g: (0, base + g))] + more,
        out_specs=pl.BlockSpec((dm.T, gc), lambda g: (0, obase + g)),
        out_shape=out_shape, input_output_aliases=aliases,
        compiler_params=_cparams(("parallel",)),
    )(u, *extra)


def loss_head(x2, target, dm, name):
    tm, d = dm.tm, dm.D
    nctx = dm.CTX // tm

    def body(x_ref, t_ref, dx_ref, l_ref):
        i = pl.program_id(0)

        @pl.when(i == 0)
        def _():
            l_ref[...] = jnp.zeros_like(l_ref)

        @pl.when(i < nctx)
        def _():
            dx_ref[...] = jnp.zeros_like(dx_ref)

        @pl.when(i >= nctx)
        def _():
            e = x_ref[...] - t_ref[...]
            dx_ref[...] = e / d
            l_ref[...] += jnp.full(l_ref.shape, 0.5 * jnp.sum(jnp.mean(e * e, axis=-1)), F32)

    return pl.pallas_call(
        body, name=name, grid=(dm.T // tm,),
        in_specs=[pl.BlockSpec((tm, d), lambda i: (i, 0)),
                  pl.BlockSpec((tm, d), lambda i: (jnp.maximum(i - nctx, 0), 0))],
        out_specs=[pl.BlockSpec((tm, d), lambda i: (i, 0)), pl.BlockSpec((8, LANES), lambda i: (0, 0))],
        out_shape=[jax.ShapeDtypeStruct((dm.T, d), F32), jax.ShapeDtypeStruct((8, LANES), F32)],
        compiler_params=_cparams(("arbitrary",)),
    )(x2, target)


def adamw(w, g, m, v, name):
    r, c = w.shape
    tr = _tile(r, tuple(t for t in (512, 256, 128, 64, 32, 16, 8) if t * c * 4 <= (1 << 20)) or (8,))

    def body(w_ref, g_ref, m_ref, v_ref, d_ref, mo_ref, vo_ref):
        gg = g_ref[...]
        mm = ADAM_B1 * m_ref[...] + (1.0 - ADAM_B1) * gg
        vv = ADAM_B2 * v_ref[...] + (1.0 - ADAM_B2) * (gg * gg)
        m_hat = mm / (1.0 - ADAM_B1 ** ADAM_STEP)
        v_hat = vv / (1.0 - ADAM_B2 ** ADAM_STEP)
        d_ref[...] = -ADAM_LR * (m_hat / (jnp.sqrt(v_hat) + ADAM_EPS) + ADAM_WD * w_ref[...])
        mo_ref[...] = mm
        vo_ref[...] = vv

    spec = pl.BlockSpec((tr, c), lambda i: (i, 0))
    return pl.pallas_call(
        body, name=name, grid=(r // tr,), in_specs=[spec] * 4, out_specs=[spec] * 3,
        out_shape=[jax.ShapeDtypeStruct((r, c), F32)] * 3,
        compiler_params=_cparams(("parallel",)),
    )(w, g, m, v)


def pair_add(g, r1, cidx, name):
    ng, r_, n_ = r1.shape
    tr = _tile(r_, tuple(t for t in (1024, 512, 256, 128, 64, 32, 16) if t * n_ * 4 <= (2 << 20)))

    def body(s_ref, g_ref, r_ref, o_ref):
        o_ref[...] = (g_ref[...].astype(F32) + r_ref[...].astype(F32)).astype(o_ref.dtype)

    return pl.pallas_call(
        body, name=name,
        grid_spec=pltpu.PrefetchScalarGridSpec(
            num_scalar_prefetch=1, grid=(ng, r_ // tr),
            in_specs=[pl.BlockSpec((None, tr, n_), lambda k, i, s: (2 * k + s[0], i, 0)),
                      pl.BlockSpec((None, tr, n_), lambda k, i, s: (k, i, 0))],
            out_specs=pl.BlockSpec((None, tr, n_), lambda k, i, s: (k, i, 0))),
        out_shape=jax.ShapeDtypeStruct((ng, r_, n_), g.dtype),
        compiler_params=_cparams(("parallel", "parallel")),
    )(cidx, g, r1)


def chip_add(h, r2, axis, where, name, slab=True):
    _, kl, nl = r2.shape
    tr = _tile(kl, tuple(t for t in (1024, 512, 256, 128, 64, 32, 16) if t * nl * 4 <= (1 << 20)))
    nrb = kl // tr

    def body(s_ref, h_ref, r_ref, o_ref):
        acc = h_ref[...].astype(F32)
        for k in range(r2.shape[0]):
            acc = acc + r_ref[k].astype(F32)
        o_ref[...] = acc

    h_map = (lambda i, s: (s[0] * nrb + i, 0)) if axis == 0 else (lambda i, s: (i, s[0]))
    if slab:
        out_spec = pl.BlockSpec((None, tr, nl), lambda i, s: (s[1], i, 0))
        out_shape = jax.ShapeDtypeStruct((2, kl, nl), F32)
    else:
        out_spec = pl.BlockSpec((tr, nl), lambda i, s: (i, 0))
        out_shape = jax.ShapeDtypeStruct((kl, nl), F32)
    return pl.pallas_call(
        body, name=name,
        grid_spec=pltpu.PrefetchScalarGridSpec(
            num_scalar_prefetch=1, grid=(nrb,),
            in_specs=[pl.BlockSpec((tr, nl), h_map),
                      pl.BlockSpec((r2.shape[0], tr, nl), lambda i, s: (0, i, 0))],
            out_specs=out_spec),
        out_shape=out_shape,
        compiler_params=_cparams(("parallel",)),
    )(where, h, r2)


def adamw_layers(w, m, v, terms, name, layer=None, prev=None):
    _, a_, b_ = w.shape
    tr = _tile(a_, tuple(t for t in (512, 256, 128, 64, 32) if t * b_ * 4 <= (1 << 20)))
    by_cols = tr == a_ and a_ * b_ * 4 > (1 << 20)
    blk = (a_, LANES) if by_cols else (tr, b_)
    steps = b_ // LANES if by_cols else a_ // tr
    at = (lambda i: (0, i)) if by_cols else (lambda i: (i, 0))
    layers = (0, 1) if layer is None else (layer,)
    counts = [len(terms[l]) for l in layers]
    nprev = 0 if prev is None else 4

    def update(g, w_ref, m_ref, v_ref, g_ref, d_ref, mo_ref, vo_ref):
        mm = ADAM_B1 * m_ref[...] + (1.0 - ADAM_B1) * g
        vv = ADAM_B2 * v_ref[...] + (1.0 - ADAM_B2) * (g * g)
        m_hat = mm / (1.0 - ADAM_B1 ** ADAM_STEP)
        v_hat = vv / (1.0 - ADAM_B2 ** ADAM_STEP)
        g_ref[...] = g
        d_ref[...] = -ADAM_LR * (m_hat / (jnp.sqrt(v_hat) + ADAM_EPS) + ADAM_WD * w_ref[...])
        mo_ref[...] = mm
        vo_ref[...] = vv

    def total(refs):
        g = refs[0][...]
        for r in refs[1:]:
            g = g + r[...]
        return g

    def body(w_ref, m_ref, v_ref, *rest):
        t_refs, outs = rest[:sum(counts)], rest[-4:]
        if len(layers) == 1:
            update(total(t_refs), w_ref, m_ref, v_ref, *outs)
            return
        which = pl.program_id(0)

        @pl.when(which == 0)
        def _():
            update(total(t_refs[:counts[0]]), w_ref, m_ref, v_ref, *outs)

        @pl.when(which == 1)
        def _():
            update(total(t_refs[counts[0]:]), w_ref, m_ref, v_ref, *outs)

    if len(layers) == 1:
        stacked = pl.BlockSpec((None,) + blk, lambda l, i: (layers[0],) + at(i))
        t_specs = [pl.BlockSpec(blk, lambda l, i: at(i))] * counts[0]
    else:
        stacked = pl.BlockSpec((None,) + blk, lambda l, i: (l,) + at(i))
        t_specs = ([pl.BlockSpec(blk, lambda l, i: at(i * (1 - l)))] * counts[0]
                   + [pl.BlockSpec(blk, lambda l, i: at(i * l))] * counts[1])
    nin = 3 + sum(counts)
    return pl.pallas_call(
        body, name=name, grid=(len(layers), steps),
        in_specs=[stacked] * 3 + t_specs + [pl.BlockSpec(memory_space=pl.ANY)] * nprev,
        out_specs=[stacked] * 4, out_shape=[jax.ShapeDtypeStruct(w.shape, F32)] * 4,
        input_output_aliases={nin + j: j for j in range(nprev)},
        compiler_params=_cparams(("arbitrary", "arbitrary")),
    )(w, m, v, *[t for l in layers for t in terms[l]], *(prev or ()))


MESH = pl.DeviceIdType.MESH
ANY = pl.BlockSpec(memory_space=pl.ANY)
HBM = pl.BlockSpec(memory_space=pltpu.HBM)
SEM = pl.BlockSpec(memory_space=pltpu.SEMAPHORE)
EFFECT = pltpu.SideEffectType.DATAFLOW_SIDE_EFFECTING


def _place():
    return lax.axis_index("x"), lax.axis_index("y"), lax.axis_index("c")


def _peers(x, y):
    return [(1 - x, y), (x, 1 - y), (1 - x, 1 - y)]


def _rcopy(src, dst, ssem, rsem, dev):
    return pltpu.make_async_remote_copy(src_ref=src, dst_ref=dst, send_sem=ssem, recv_sem=rsem,
                                        device_id=dev, device_id_type=MESH)


def _gathered_shape(src, kind):
    h, a_, b_ = src.shape
    return (h, a_, 4 * b_) if kind == 'col' else (4, h, a_, b_)


def _win(ref, kind, ch, width):
    return ref.at[:, :, pl.ds(ch * width, width)] if kind == 'col' else ref.at[ch]


def _rect(ref, kind, half, ch, width):
    return ref.at[half, :, pl.ds(ch * width, width)] if kind == 'col' else ref.at[ch, half]


def _gather_plan(kinds, widths):
    def plan(src, land, x, y, c):
        chip = 2 * x + y
        out = []
        for n in range(len(src)):
            for px, py in _peers(x, y):
                out.append((src[n].at[c], _rect(land[n], kinds[n], c, chip, widths[n]), (px, py, c),
                            _rect(land[n], kinds[n], c, 2 * px + py, widths[n])))
            mine = _win(land[n], kinds[n], chip, widths[n])
            out.append((src[n], mine, (x, y, 1 - c), mine))
        return out
    return plan


def forward_halves(lands, kinds, name):
    nw = len(lands)
    widths = [t.shape[-1] // 4 if k == 'col' else t.shape[-1] for t, k in zip(lands, kinds)]

    def body(*refs):
        o = refs[nw:2 * nw]
        ssem, rsem = refs[2 * nw:]
        x, y, c = _place()
        sib = (x, y, 1 - c)
        pidx = [2 * px + py for px, py in _peers(x, y)]
        cps = [_rcopy(_rect(o[n], kinds[n], c, pidx[j], widths[n]), _rect(o[n], kinds[n], c, pidx[j], widths[n]),
                      ssem.at[3 * n + j], rsem.at[3 * n + j], sib) for n in range(nw) for j in range(3)]
        for cp in cps:
            cp.start()
        for n in range(nw):
            for j in range(3):
                cps[3 * n + j].wait_send()
                _rcopy(_rect(o[n], kinds[n], 1 - c, pidx[j], widths[n]), _rect(o[n], kinds[n], 1 - c, pidx[j], widths[n]),
                       ssem.at[3 * n + j], rsem.at[3 * n + j], sib).wait_recv()

    return pl.pallas_call(
        body, name=name, in_specs=[ANY] * nw, out_specs=[ANY] * nw,
        out_shape=[jax.ShapeDtypeStruct(t.shape, t.dtype) for t in lands],
        input_output_aliases={n: n for n in range(nw)},
        scratch_shapes=[pltpu.SemaphoreType.DMA((3 * nw,)), pltpu.SemaphoreType.DMA((3 * nw,))],
    )(*lands)


def _scatter_plan(axes, widths):
    def plan(src, land, x, y, c):
        out = []
        for n in range(len(src)):
            for k, (px, py) in enumerate(_peers(x, y)):
                ch = 2 * px + py
                view = (src[n].at[:, pl.ds(ch * widths[n], widths[n])] if axes[n] == 1
                        else src[n].at[pl.ds(ch * widths[n], widths[n]), :])
                out.append((view, land[n].at[k], (px, py, c), land[n].at[k]))
        return out
    return plan


def _exchange_plan(kinds):
    def plan(src, land, x, y, c):
        out = []
        for n in range(len(src)):
            w = land[n].shape[2]
            for j, (px, py) in enumerate(_peers(x, y)):
                ch = 2 * px + py
                view = src[n].at[:, pl.ds(ch * w, w)] if kinds[n] == 'col' else src[n].at[ch]
                out.append((view, land[n].at[j], (px, py, c), land[n].at[j]))
        return out
    return plan


def start_copies(srcs, lands, plan, ncopies, after, name):
    ns, nl = len(srcs), len(lands)

    def body(*refs):
        src, land = refs[:ns], refs[ns:ns + nl]
        ssem, rsem = refs[ns + nl + 1], refs[ns + nl + 2]
        token = refs[-1]
        x, y, c = _place()
        for k, (sv, dv, dev, _) in enumerate(plan(src, land, x, y, c)):
            _rcopy(sv, dv, ssem.at[k], rsem.at[k], dev).start()
        token[...] = jnp.zeros_like(token)

    hbm = lambda t: pltpu.HBM(t.shape, t.dtype)
    res = pl.pallas_call(
        body, name=name,
        out_shape=(pltpu.SemaphoreType.DMA((ncopies,)), pltpu.SemaphoreType.DMA((ncopies,)),
                   *[hbm(t) for t in srcs], *[hbm(t) for t in lands], jax.ShapeDtypeStruct((8, LANES), F32)),
        in_specs=[HBM] * (ns + nl) + [ANY],
        out_specs=(SEM, SEM, *[HBM] * (ns + nl), pl.BlockSpec(memory_space=pltpu.VMEM)),
        input_output_aliases={k: 2 + k for k in range(ns + nl)},
        compiler_params=pltpu.CompilerParams(has_side_effects=EFFECT),
    )(*[pltpu.with_memory_space_constraint(t, pltpu.HBM) for t in list(srcs) + list(lands)], after)
    return res[0], res[1], list(res[2:2 + ns]), list(res[2 + ns:2 + ns + nl]), res[-1]


def wait_copies(ssem, rsem, srcs, lands, plan, after, name):
    ns, nl = len(srcs), len(lands)

    def body(*refs):
        src, land = refs[:ns], refs[ns:ns + nl]
        ss, rs = refs[ns + nl], refs[ns + nl + 1]
        x, y, c = _place()
        for k, (sv, dv, dev, mine) in enumerate(plan(src, land, x, y, c)):
            cp = _rcopy(sv, mine, ss.at[k], rs.at[k], dev)
            cp.wait_send()
            cp.wait_recv()

    hbm = lambda t: pltpu.HBM(t.shape, t.dtype)
    res = pl.pallas_call(
        body, name=name,
        out_shape=(*[hbm(t) for t in srcs], *[hbm(t) for t in lands]),
        in_specs=[HBM] * (ns + nl) + [SEM, SEM, ANY], out_specs=tuple([HBM] * (ns + nl)),
        input_output_aliases={k: k for k in range(ns + nl)},
        compiler_params=pltpu.CompilerParams(has_side_effects=EFFECT),
    )(*srcs, *lands, ssem, rsem, after)
    return list(res[ns:])


def pair_swap_halves(gs, kinds, name):
    nw = len(gs)

    def other(ref, kind, half):
        return ref.at[half] if kind == 'col' else ref.at[:, half]

    def body(*refs):
        g, o = refs[:nw], refs[nw:2 * nw]
        ssem, rsem = refs[2 * nw:]
        x, y, c = _place()
        cps = [_rcopy(other(g[n], kinds[n], 1 - c), o[n], ssem.at[n], rsem.at[n], (x, y, 1 - c)) for n in range(nw)]
        for cp in cps:
            cp.start()
        for cp in cps:
            cp.wait()

    return pl.pallas_call(
        body, name=name, in_specs=[ANY] * nw, out_specs=[ANY] * nw,
        out_shape=[jax.ShapeDtypeStruct(g.shape[1:] if k == 'col' else (g.shape[0],) + g.shape[2:], g.dtype)
                   for g, k in zip(gs, kinds)],
        scratch_shapes=[pltpu.SemaphoreType.DMA((nw,)), pltpu.SemaphoreType.DMA((nw,))],
    )(*gs)


def pair_swap(fs, name):
    nw = len(fs)

    def body(*refs):
        f, o = refs[:nw], refs[nw:2 * nw]
        ssem, rsem = refs[2 * nw:]
        x, y, c = _place()
        cps = [_rcopy(f[n], o[n], ssem.at[n], rsem.at[n], (x, y, 1 - c)) for n in range(nw)]
        for cp in cps:
            cp.start()
        for cp in cps:
            cp.wait()

    return pl.pallas_call(
        body, name=name, in_specs=[ANY] * nw, out_specs=[ANY] * nw,
        out_shape=[jax.ShapeDtypeStruct(f.shape, f.dtype) for f in fs],
        scratch_shapes=[pltpu.SemaphoreType.DMA((nw,)), pltpu.SemaphoreType.DMA((nw,))],
    )(*fs)


def chip_broadcast(h, name):
    def body(h_ref, o_ref, ssem, rsem):
        x, y, c = _place()
        cps = [_rcopy(h_ref, o_ref.at[j], ssem.at[j], rsem.at[j], (px, py, c)) for j, (px, py) in enumerate(_peers(x, y))]
        for cp in cps:
            cp.start()
        for cp in cps:
            cp.wait()

    return pl.pallas_call(
        body, name=name, in_specs=[ANY], out_specs=ANY,
        out_shape=jax.ShapeDtypeStruct((3,) + h.shape, h.dtype),
        scratch_shapes=[pltpu.SemaphoreType.DMA((3,)), pltpu.SemaphoreType.DMA((3,))],
    )(h)


def quad_sum(h, r, cidx, name):
    r_, c_ = h.shape
    tr = _tile(r_, (512, 256, 128, 64, 32, 16, 8))

    def body(s_ref, h_ref, r_ref, o_ref):
        o_ref[...] = (h_ref[...] + r_ref[2]) + (r_ref[0] + r_ref[1])

    return pl.pallas_call(
        body, name=name,
        grid_spec=pltpu.PrefetchScalarGridSpec(
            num_scalar_prefetch=1, grid=(r_ // tr,),
            in_specs=[pl.BlockSpec((tr, c_), lambda i, s: (i, 0)), pl.BlockSpec((3, tr, c_), lambda i, s: (0, i, 0))],
            out_specs=pl.BlockSpec((None, tr, c_), lambda i, s: (s[0], i, 0))),
        out_shape=jax.ShapeDtypeStruct((2, r_, c_), F32),
        compiler_params=_cparams(("parallel",)),
    )(cidx, h, r)


def pair_join_layers(fs, name):
    nw = len(fs)

    def body(*refs):
        o = refs[nw:2 * nw]
        ssem, rsem = refs[2 * nw:]
        x, y, c = _place()
        sib = (x, y, 1 - c)
        cps = [_rcopy(o[n].at[c], o[n].at[c], ssem.at[n], rsem.at[n], sib) for n in range(nw)]
        for cp in cps:
            cp.start()
        for n in range(nw):
            cps[n].wait_send()
            _rcopy(o[n].at[1 - c], o[n].at[1 - c], ssem.at[n], rsem.at[n], sib).wait_recv()

    return pl.pallas_call(
        body, name=name, in_specs=[ANY] * nw, out_specs=[ANY] * nw,
        out_shape=[jax.ShapeDtypeStruct(f.shape, f.dtype) for f in fs],
        input_output_aliases={n: n for n in range(nw)},
        scratch_shapes=[pltpu.SemaphoreType.DMA((nw,)), pltpu.SemaphoreType.DMA((nw,))],
    )(*fs)


def _flatten_pad(parts, dtype):
    flat = jnp.concatenate([p.reshape(-1).astype(dtype) for p in parts])
    q = 512 * LANES
    n = -(-flat.shape[0] // q) * q
    return jnp.pad(flat, (0, n - flat.shape[0])).reshape(n // LANES, LANES)


def _lane_pad(n):
    return -(-n // LANES) * LANES


def _in_proj_layout(d):
    gk, gv, cw, pw = d // 2, d, d // 2, d // 2
    own = [('q', gk), ('k', gk), ('v', gv), ('og', gv), ('lrf', GLA_LR), ('lrb', GLA_LR), ('ga', cw), ('gb', cw),
           ('pu', pw), ('mg', 3 * d)]
    padded = [('mg', 3 * d), ('og', gv), ('v', gv), ('q', gk), ('k', gk), ('ga', cw), ('gb', cw), ('pu', pw),
              ('lrf', GLA_LR), ('lrb', GLA_LR), ('pad', d // 2 - 2 * GLA_LR)]
    return own, padded


def _row_pieces(src, lo, hi, wl, wlp):
    out = []
    for k in range(4):
        s0, s1 = max(lo, k * wl), min(hi, (k + 1) * wl)
        if s0 < s1:
            out.append(src[k * wlp + s0 - k * wl:k * wlp + s1 - k * wl])
    return out


def _w_in_t_to_proj(g, d, wl, wlp):
    own, padded = _in_proj_layout(d)
    at, start = {}, 0
    for n, wd in own:
        at[n] = (start, start + wd)
        start += wd
    parts = []
    for n, wd in padded:
        parts += [jnp.zeros((wd, g.shape[1]), g.dtype)] if n == 'pad' else _row_pieces(g, *at[n], wl, wlp)
    return jnp.concatenate(parts, axis=0)


def _proj_to_w_in_t(gp, d, wl, wlp):
    own, padded = _in_proj_layout(d)
    pat, start = {}, 0
    for n, wd in padded:
        pat[n] = start
        start += wd
    parts = []
    for k in range(4):
        start = 0
        for n, wd in own:
            s0, s1 = max(start, k * wl), min(start + wd, (k + 1) * wl)
            if s0 < s1:
                parts.append(gp[pat[n] + s0 - start:pat[n] + s1 - start])
            start += wd
        parts.append(jnp.zeros((wlp - wl, gp.shape[1]), gp.dtype))
    return jnp.concatenate(parts, axis=0)


def _silu_grad(z):
    s = jax.nn.sigmoid(z)
    return s + z * s * (1.0 - s)


def kernel(x, c, ctx, c_ctx, w_ada, b_ada, g_pre_mix, g_post_mix, g_pre_mlp, g_post_mlp, w_in, w_decay, b_decay, g_gla, w_gla_o, w_dw, b_dw, g_conv_ln, b_conv_ln, w_conv_o, w_pool_g, s_pool, w_pool_o, b_gate, w_out, w_mlp1, w_mlp2, loss_target, m_c_ctx, m_w_ada, m_b_ada, m_g_pre_mix, m_g_post_mix, m_g_pre_mlp, m_g_post_mlp, m_w_in, m_w_decay, m_b_decay, m_g_gla, m_w_gla_o, m_w_dw, m_b_dw, m_g_conv_ln, m_b_conv_ln, m_w_conv_o, m_w_pool_g, m_s_pool, m_w_pool_o, m_b_gate, m_w_out, m_w_mlp1, m_w_mlp2, v_c_ctx, v_w_ada, v_b_ada, v_g_pre_mix, v_g_post_mix, v_g_pre_mlp, v_g_post_mlp, v_w_in, v_w_decay, v_b_decay, v_g_gla, v_w_gla_o, v_w_dw, v_b_dw, v_g_conv_ln, v_b_conv_ln, v_w_conv_o, v_w_pool_g, v_s_pool, v_w_pool_o, v_b_gate, v_w_out, v_w_mlp1, v_w_mlp2):
    a = dict(locals())
    for n in ('w_in', 'm_w_in', 'v_w_in'):
        a[n] = jnp.swapaxes(a[n], 1, 2)
    big_axis = dict(BIG, w_in=1)
    depth = w_in.shape[0]
    d = x.shape[-1]
    seq, nctx_rows = x.shape[1], ctx.shape[1]
    dm = types.SimpleNamespace(
        D=d, SEQ=seq, CTX=nctx_rows, T=seq + nctx_rows, DK=d // 8, DV=d // 4, GK=d // 2, GC=d // 8,
        tm=_tile(nctx_rows, (256, 128, 64)), TB=_tile(nctx_rows, (256, 128, 64)))
    assert dm.SEQ % dm.tm == 0 and dm.SEQ % GRID_W == 0 and dm.CTX % GLA_CHUNK == 0
    tmw = min(dm.tm, 128)
    chip = 2 * lax.axis_index("x") + lax.axis_index("y")
    core = lax.axis_index("c")
    chip1 = chip.astype(jnp.int32).reshape(1)
    core1 = core.astype(jnp.int32).reshape(1)

    big_names, small_names = list(BIG), list(SMALL_SHARDED)
    nbig = len(big_names)
    kinds = ['col' if big_axis[n] == 2 else 'row' for n in big_names]
    wl = w_in.shape[2]
    wlp = _lane_pad(wl)

    def rows8(t):
        t = t.reshape(t.shape[0], -1, t.shape[-1])
        return jnp.pad(t, ((0, 0), (0, -t.shape[1] % 8), (0, 0)))

    def halves(t):
        return t.reshape(2, t.shape[0] // 2, t.shape[1])

    def layer_src(l, tok=None):
        def one(n):
            t = a[n][l] if tok is None else a[n][l] + tok
            return halves((jnp.pad(t, ((0, wlp - wl), (0, 0))) if n == 'w_in' else t).astype(MM_DTYPE))
        return [one(n) for n in big_names]

    def whole(t):
        return t.reshape(-1, t.shape[-1])

    def start_gather(srcs, knds, after, name):
        plan = _gather_plan(knds, [t.shape[2] for t in srcs])
        lands = [lax.empty(_gathered_shape(t, k), t.dtype) for t, k in zip(srcs, knds)]
        return (plan,) + start_copies(srcs, lands, plan, 4 * len(srcs), after, name)

    late = [big_names.index(n) for n in ('w_gla_o', 'w_conv_o', 'w_pool_o', 'w_out', 'w_mlp1', 'w_mlp2')]
    early = [k for k in range(nbig) if k not in late]
    src0 = layer_src(0)
    kinds_e = [kinds[k] for k in early] + ['col'] * len(small_names)
    age = start_gather([src0[k] for k in early] + [rows8(a[n]) for n in small_names], kinds_e, src0[early[0]],
                       "gather_layer0_start")
    tok0 = age[-1][0, 0]
    src1 = layer_src(1, tok0)
    pk = lambda pre: _flatten_pad([a[pre + n] + tok0 for n in SMALL], F32)
    small_w, small_m, small_v = pk(''), pk('m_'), pk('v_')
    X = jnp.concatenate([ctx[0] + tok0, x[0] + tok0], axis=0)
    ready = (small_w[0, 0] + small_m[0, 0] + small_v[0, 0] + X[0, 0]
             + sum(t[0, 0, 0].astype(F32) for t in src1)).reshape(1, 1)
    g0 = wait_copies(age[1], age[2], age[3], age[4], age[0], ready, "gather_layer0_wait")
    g0 = forward_halves(g0, kinds_e, "gather_layer0_forward")
    ag0 = start_gather([src0[k] for k in late], [kinds[k] for k in late], g0[0], "gather_layer0_late_start")
    ag1 = start_gather(src1, kinds, ag0[-1], "gather_layer1_start")
    ag_token = ag1[-1]
    full = {n: [None, None] for n in big_names}
    for k, t in zip(early, g0):
        full[big_names[k]][0] = whole(t)
    for n, g in zip(small_names, g0[len(early):]):
        shp = a[n].shape
        full[n] = g[:, :math.prod(shp[1:-1])].reshape(shp[:-1] + (4 * shp[-1],))
    for n in SMALL:
        if n not in SMALL_SHARDED:
            full[n] = a[n]

    cvec = jnp.concatenate([c_ctx.reshape(1, d), c.reshape(1, d), jnp.zeros((6, d), F32)], axis=0)
    avec = (cvec * jax.nn.sigmoid(cvec) + ag_token[0, 0]).astype(MM_DTYPE)

    def row(v):
        return v.reshape(1, -1)

    saved = []
    gk, gv = dm.GK, d
    lrblk = (7 * d + d // 2) // LANES
    for l in range(depth):
        if l == 1:
            got = wait_copies(ag1[1], ag1[2], ag1[3], ag1[4], ag1[0], X, "gather_layer1_wait")
            got = forward_halves(got, kinds, "gather_layer1_forward")
            for n, t in zip(big_names, got):
                full[n][1] = whole(t)
        s = types.SimpleNamespace()
        s.w_in_p = _w_in_t_to_proj(full['w_in'][l], d, wl, wlp)
        wd = full['w_decay'][l]
        wdp = jnp.zeros((LANES, 2 * gk), F32)
        wdp = wdp.at[:GLA_LR, :gk].set(wd[0]).at[GLA_LR:2 * GLA_LR, gk:].set(wd[1])
        s.wdp = wdp.astype(MM_DTYPE)
        s.wdp_wide = jnp.pad(s.wdp, ((0, d // 2 - LANES), (0, 0)))
        s.bd = full['b_decay'][l].reshape(1, 2 * gk)
        modraw = matmul(avec, full['w_ada'][l], 'nn', F32, f"mod_{l}") + full['b_ada'][l][None, :]
        s.mod = [modraw[0:2, j * d:(j + 1) * d].reshape(2, 1, d) for j in range(6)]
        s.x = X
        (s.h,) = rowwise(pre_fn, [X], s.mod[0:2], [row(g_pre_mix[l])], [(d, MM_DTYPE)], dm, f"pre_{l}")
        s.P = matmul(s.h, s.w_in_p, 'nt', MM_DTYPE, f"in_proj_{l}")
        P = s.P
        s.z = matmul((P, LANES, lrblk), s.wdp, 'nn', F32, f"decay_proj_{l}", tk=LANES)
        la_f, la_b = rowwise(decay_fn, [s.z], [], [s.bd], [(gk, F32), (gk, F32)], dm, f"decay_{l}")
        s.la = jnp.concatenate([la_f, la_b], axis=1)
        s.o_f, s.st_f = gla_fwd(P, s.la, False, dm, f"gla_fwd_f_{l}")
        s.o_b, s.st_b = gla_fwd(P, s.la, True, dm, f"gla_fwd_b_{l}")
        (s.gin,) = rowwise(glaout_fn, [s.o_f, s.o_b, (P, d, 3)], [], [row(g_gla[l])], [(gv, MM_DTYPE)], dm,
                           f"gla_out_{l}")
        if l == 0:
            got = wait_copies(ag0[1], ag0[2], ag0[3], ag0[4], ag0[0], s.gin, "gather_layer0_late_wait")
            got = forward_halves(got, [kinds[k] for k in late], "gather_layer0_late_forward")
            for k, t in zip(late, got):
                full[big_names[k]][0] = whole(t)
        s.ya = matmul(s.gin, full['w_gla_o'][l], 'nn', MM_DTYPE, f"gla_o_{l}")
        (s.u,) = rowwise(glu_fn, [(P, d, 6)], [], [], [(d // 2, F32)], dm, f"glu_{l}")
        s.yconv = conv_fwd(s.u, full['w_dw'][l], dm, f"conv_{l}")
        (s.cin,) = rowwise(convpost_fn, [s.yconv], [], [row(b_dw[l]), row(g_conv_ln[l]), row(b_conv_ln[l])],
                           [(d // 2, MM_DTYPE)], dm, f"conv_post_{l}")
        s.yb = matmul(s.cin, full['w_conv_o'][l], 'nn', MM_DTYPE, f"conv_o_{l}")
        s.pm = pool_mix((P, d // 2, 14), False, dm, f"pool_mix_{l}")
        s.pc = group_mm(s.pm, w_pool_g[l], 'nn', F32, f"pool_g_{l}")
        (s.pin,) = rowwise(poolpost_fn, [s.pc], [], [row(s_pool[l])], [(d // 2, MM_DTYPE)], dm, f"pool_post_{l}")
        s.yc = matmul(s.pin, full['w_pool_o'][l], 'nn', MM_DTYPE, f"pool_o_{l}")
        s.bg = [row(full['b_gate'][l][j]) for j in range(3)]
        (s.mixed,) = rowwise(merge_fn, [s.ya, s.yb, s.yc, (P, 3 * d, 0)], [], s.bg, [(d, MM_DTYPE)], dm,
                             f"merge_{l}", tm=tmw)
        s.y = matmul(s.mixed, full['w_out'][l], 'nn', MM_DTYPE, f"out_proj_{l}")
        s.x1, s.h2 = rowwise(mid_fn, [X, s.y], s.mod[2:5], [row(g_post_mix[l]), row(g_pre_mlp[l])],
                             [(d, F32), (d, MM_DTYPE)], dm, f"mid_{l}")
        s.act = matmul(s.h2, full['w_mlp1'][l], 'nn', MM_DTYPE, f"mlp1_{l}", epi=relu2_epi)
        s.y2 = matmul(s.act, full['w_mlp2'][l], 'nn', MM_DTYPE, f"mlp2_{l}")
        (X,) = rowwise(post_fn, [s.x1, s.y2], s.mod[5:6], [row(g_post_mlp[l])], [(d, F32)], dm, f"post_{l}")
        saved.append(s)

    dX, lossv = loss_head(X, loss_target[0], dm, "loss_head")
    loss = lax.psum(lossv[0, 0], ("x", "y", "c"))

    grads = {n: [None] * depth for n in WEIGHTS if n != 'c_ctx' and n not in BIG}
    gbig = {n: [None] * depth for n in BIG}
    rs_token = None

    def start_scatter(idx, layer, after, name):
        gs = [gbig[big_names[k]][layer] for k in idx]
        wd = [t.shape[1] // 4 if kinds[k] == 'col' else t.shape[0] // 4 for t, k in zip(gs, idx)]
        plan = _scatter_plan([big_axis[big_names[k]] - 1 for k in idx], wd)
        lands = [lax.empty((3, t.shape[0], w) if kinds[k] == 'col' else (3, w, t.shape[1]), t.dtype)
                 for t, w, k in zip(gs, wd, idx)]
        return (plan,) + start_copies(gs, lands, plan, 3 * len(gs), after, name)

    g_cctx = jnp.zeros((d,), F32)
    for l in reversed(range(depth)):
        s = saved[l]
        P = s.P
        dmod = [None] * 6
        gpm = row(g_post_mlp[l]) if rs_token is None else row(g_post_mlp[l]) + rs_token[0, 0]
        (dx1, dy2), (dmod[5],), (dg,) = rowwise_vjp(post_fn, [s.x1, s.y2], s.mod[5:6], [gpm], [dX],
                                                     dm, f"post_bwd_{l}", narrow=(1,))
        grads['g_post_mlp'][l] = dg[0]
        du1 = matmul(dy2, full['w_mlp2'][l], 'nt', MM_DTYPE, f"mlp2_dx_{l}", epi=relu2_bwd_epi, extras=[s.act])
        gbig['w_mlp2'][l] = matmul(s.act, dy2, 'tn', MM_DTYPE, f"mlp2_dw_{l}")
        dh2 = matmul(du1, full['w_mlp1'][l], 'nt', MM_DTYPE, f"mlp1_dx_{l}")
        gbig['w_mlp1'][l] = matmul(s.h2, du1, 'tn', MM_DTYPE, f"mlp1_dw_{l}")
        gpx = row(g_post_mix[l])
        (dxa, dy), dmod[2:5], (dg1, dg2) = rowwise_vjp(
            mid_fn, [s.x, s.y], s.mod[2:5], [gpx, row(g_pre_mlp[l])], [dx1, dh2], dm, f"mid_bwd_{l}", narrow=(1,))
        grads['g_post_mix'][l], grads['g_pre_mlp'][l] = dg1[0], dg2[0]
        dmixed = matmul(dy, full['w_out'][l], 'nt', MM_DTYPE, f"out_proj_dx_{l}")
        gbig['w_out'][l] = matmul(s.mixed, dy, 'tn', MM_DTYPE, f"out_proj_dw_{l}")
        (dya, dyb, dyc, dP), _, dbg = rowwise_vjp(merge_fn, [s.ya, s.yb, s.yc, (P, 3 * d, 0)], [], s.bg, [dmixed],
                                                  dm, f"merge_bwd_{l}", tm=tmw, narrow=(0, 1, 2),
                                                  into=(3, None, P.shape))
        grads['b_gate'][l] = jnp.concatenate(dbg, axis=0)
        dgin = matmul(dya, full['w_gla_o'][l], 'nt', MM_DTYPE, f"gla_o_dx_{l}")
        gbig['w_gla_o'][l] = matmul(s.gin, dya, 'tn', MM_DTYPE, f"gla_o_dw_{l}")
        dcin = matmul(dyb, full['w_conv_o'][l], 'nt', MM_DTYPE, f"conv_o_dx_{l}")
        gbig['w_conv_o'][l] = matmul(s.cin, dyb, 'tn', MM_DTYPE, f"conv_o_dw_{l}")
        dpin = matmul(dyc, full['w_pool_o'][l], 'nt', MM_DTYPE, f"pool_o_dx_{l}")
        gbig['w_pool_o'][l] = matmul(s.pin, dyc, 'tn', MM_DTYPE, f"pool_o_dw_{l}")
        sp = row(s_pool[l])
        if l == 0:
            rs0 = start_scatter(late, 0, dpin, "grad_layer0_late_start")
            sp = sp + rs0[-1][0, 0]
        (dpc,), _, (dsp,) = rowwise_vjp(poolpost_fn, [s.pc], [], [sp], [dpin], dm, f"pool_post_bwd_{l}")
        grads['s_pool'][l] = dsp[0]
        grads['w_pool_g'][l] = group_mm(s.pm, w_pool_g[l], 'tn', F32, f"pool_g_dw_{l}", b=dpc)
        dpm = group_mm(dpc, w_pool_g[l], 'nt', F32, f"pool_g_dx_{l}")
        dP = pool_mix(dpm, True, dm, f"pool_mix_bwd_{l}", into=(dP, 14))
        (dyconv,), _, (dbdw, dgln, dbln) = rowwise_vjp(
            convpost_fn, [s.yconv], [], [row(b_dw[l]), row(g_conv_ln[l]), row(b_conv_ln[l])], [dcin], dm,
            f"conv_post_bwd_{l}")
        grads['b_dw'][l], grads['g_conv_ln'][l], grads['b_conv_ln'][l] = dbdw[0], dgln[0], dbln[0]
        du, grads['w_dw'][l] = conv_bwd(s.u, full['w_dw'][l], dyconv, dm, f"conv_bwd_{l}")
        (dP,), _, _ = rowwise_vjp(glu_fn, [(P, d, 6)], [], [], [du], dm, f"glu_bwd_{l}", into=(0, dP, P.shape))
        (do, _, dP), _, (dgg,) = rowwise_vjp(glaout_fn, [s.o_f, s.o_b, (P, d, 3)], [], [row(g_gla[l])], [dgin], dm,
                                             f"gla_out_bwd_{l}", want=[True, False, True], into=(2, dP, P.shape), narrow=(0,))
        grads['g_gla'][l] = dgg[0]
        dqf, dkf, dvf, dlaf = gla_bwd(P, s.la, do, s.st_f, False, dm, f"gla_bwd_f_{l}")
        dP, dlab = gla_bwd(P, s.la, do, s.st_b, True, dm, f"gla_bwd_b_{l}", prev=(dqf, dkf, dvf), into=dP)
        (dz,), _, (dbd,) = rowwise_vjp(decay_fn, [s.z], [], [s.bd], [dlaf, dlab], dm, f"decay_bwd_{l}", narrow=(0,))
        grads['b_decay'][l] = dbd.reshape(2, gk)
        dwdp = matmul((P, LANES, lrblk), dz, 'tn', F32, f"decay_proj_dw_{l}", tm=LANES)
        grads['w_decay'][l] = jnp.stack([dwdp[:GLA_LR, :gk], dwdp[GLA_LR:2 * GLA_LR, gk:]])
        dP = matmul(dz, s.wdp_wide, 'nt', MM_DTYPE, f"decay_proj_dx_{l}", into=(dP, 15))
        dh = matmul(dP, s.w_in_p, 'nn', MM_DTYPE, f"in_proj_dx_{l}")
        gbig['w_in'][l] = _proj_to_w_in_t(matmul(dP, s.h, 'tn', MM_DTYPE, f"in_proj_dw_{l}"), d, wl, wlp)
        (dX,), dmod[0:2], (dg,) = rowwise_vjp(pre_fn, [s.x], s.mod[0:2], [row(g_pre_mix[l])], [dh], dm,
                                               f"pre_bwd_{l}", adds={0: dxa})
        grads['g_pre_mix'][l] = dg[0]
        dmodflat = jnp.concatenate([jnp.concatenate([m_.reshape(2, d) for m_ in dmod], axis=1),
                                    jnp.zeros((6, 6 * d), F32)], axis=0)
        grads['b_ada'][l] = dmodflat[0] + dmodflat[1]
        gbig['w_ada'][l] = matmul(avec, dmodflat, 'tn', MM_DTYPE, f"ada_dw_{l}")
        dav = matmul(dmodflat, full['w_ada'][l], 'nt', F32, f"ada_dx_{l}")
        g_cctx = g_cctx + dav[0] * _silu_grad(c_ctx)
        if l == 1:
            rs1 = start_scatter(list(range(nbig)), 1, dav, "grad_layer1_start")
            rs_token = rs1[-1]

    grad_x = dX[dm.CTX:][None]
    gfull = {n: jnp.stack(v) for n, v in grads.items()}
    gfull['c_ctx'] = g_cctx
    where = jnp.concatenate([chip1, core1])

    def halves_view(t, k):
        return t.reshape(2, t.shape[0] // 2, t.shape[1]) if k == 'col' else t.reshape(4, 2, t.shape[0] // 8, t.shape[1])
    enames = [big_names[k] for k in early]
    ekinds = [kinds[k] for k in early]
    v0 = [halves_view(gbig[n][0], k) for n, k in zip(enames, ekinds)]
    r1 = pair_swap_halves(v0, ekinds, "grad_pair_swap")
    hs = [pair_add(v.reshape((-1,) + v.shape[-2:]), r.reshape((-1,) + r.shape[-2:]), core1, f"grad_pair_add_{n}")
          for n, v, r in zip(enames, v0, r1)]
    hx = [h.reshape(h.shape[1:]) if k == 'col' else h for h, k in zip(hs, ekinds)]
    ex_plan = _exchange_plan(ekinds)
    ex_lands = [lax.empty((3, h.shape[0], h.shape[1] // 4) if k == 'col' else (3,) + h.shape[1:], h.dtype)
                for h, k in zip(hx, ekinds)]
    ex = (ex_plan,) + start_copies(hx, ex_lands, ex_plan, 3 * len(hx), hx[0], "grad_chip_exchange_start")

    got0 = wait_copies(rs0[1], rs0[2], rs0[3], rs0[4], rs0[0], ex[-1], "grad_layer0_late_wait")
    got1 = wait_copies(rs1[1], rs1[2], rs1[3], rs1[4], rs1[0], ex[-1], "grad_layer1_wait")
    sa = [chip_add(g, r, big_axis[big_names[k]] - 1, where, f"grad_layer0_add_{big_names[k]}", slab=False)
          for k, g, r in zip(late, rs0[3], got0)]
    sa += [chip_add(g, r, big_axis[n] - 1, where, f"grad_layer1_add_{n}", slab=False)
           for n, g, r in zip(big_names, rs1[3], got1)]
    sb = pair_swap(sa, "grad_late_pair_swap")
    red0 = {big_names[k]: [sa[j], sb[j]] for j, k in enumerate(late)}
    red1 = {n: [sa[len(late) + k], sb[len(late) + k]] for k, n in enumerate(big_names)}

    sflat = _flatten_pad([gfull[n].astype(F32) for n in SMALL], F32)
    sv = sflat.reshape(2, sflat.shape[0] // 2, LANES)
    (sr,) = pair_swap_halves([sv], ['col'], "small_grad_pair_swap")
    sh = pair_add(sv, sr[None], core1, "small_grad_pair_add")[0]
    sq = quad_sum(sh, chip_broadcast(sh, "small_grad_chip_exchange"), core1, "small_grad_chip_sum")
    (ssum,) = pair_join_layers([sq], "small_grad_pair_join")
    ssum = ssum.reshape(-1)

    out_g, out_d, out_m, out_v = {}, {}, {}, {}

    def update_big(n, terms, **kw):
        res = adamw_layers(a[n], a['m_' + n], a['v_' + n], terms, f"adamw_{n}" + ("" if not kw else f"_{kw['layer']}"), **kw)
        out_g[n], out_d[n], out_m[n], out_v[n] = res
        return res
    for k in late:
        update_big(big_names[k], [red0[big_names[k]], red1[big_names[k]]])
    half_done = {n: update_big(n, {1: red1[n]}, layer=1) for n in enames}
    start = 0
    sg = {}
    for n in SMALL:
        cnt = gfull[n].size
        g = ssum[start:start + cnt].reshape(gfull[n].shape)
        start += cnt
        if n in SMALL_SHARDED:
            ax = SMALL_SHARDED[n]
            wdt = a[n].shape[ax]
            g = lax.dynamic_slice_in_dim(g, chip * wdt, wdt, axis=ax)
        sg[n] = g
    gs = _flatten_pad([sg[n] for n in SMALL], F32)
    dl, mn, vn = adamw(small_w, gs, small_m, small_v, "adamw_small")
    done = (dl[0, 0] + sum(out_d[n][1, 0, 0] for n in big_names)).reshape(1, 1)
    r2 = wait_copies(ex[1], ex[2], ex[3], ex[4], ex[0], done, "grad_chip_exchange_wait")
    dl, mn, vn = dl.reshape(-1), mn.reshape(-1), vn.reshape(-1)
    start = 0
    for n in SMALL:
        cnt, shp = a[n].size, a[n].shape
        out_g[n] = sg[n]
        out_d[n], out_m[n], out_v[n] = (t[start:start + cnt].reshape(shp) for t in (dl, mn, vn))
        start += cnt
    fs = [chip_add(h.reshape(-1, h.shape[-1]), r, big_axis[n] - 1, where, f"grad_chip_add_{n}")
          for n, h, r in zip(enames, ex[3], r2)]
    for n, t in zip(enames, pair_join_layers(fs, "grad_pair_join")):
        update_big(n, {0: [t.reshape(-1, t.shape[-1])]}, layer=0, prev=tuple(half_done[n]))
    for dct in (out_g, out_d, out_m, out_v):
        dct['w_in'] = jnp.swapaxes(dct['w_in'], 1, 2)
    return (loss, grad_x, *[out_g[n] for n in WEIGHTS], *[out_d[n] for n in WEIGHTS],
            *[out_m[n] for n in WEIGHTS], *[out_v[n] for n in WEIGHTS])
```

```python
import functools
import math
import types

import jax
import jax.numpy as jnp
from jax import lax
from jax.experimental import pallas as pl
from jax.experimental.pallas import tpu as pltpu

F32 = jnp.float32
MM_DTYPE = jnp.bfloat16
VMEM_LIMIT_V7X = 56 * 1024 * 1024
LANES = 128
EPS = 1e-6

N_HEADS = 4
GLA_CHUNK = 64
GLA_TAU = 16.0
GLA_LR = 16
GRID_W = 64
POOL_WINDOWS = (2, 4, 8, 16)

ADAM_LR = 0.001
ADAM_B1 = 0.9
ADAM_B2 = 0.999
ADAM_EPS = 1e-08
ADAM_WD = 0.01
ADAM_STEP = 10

NN = (((1,), (0,)), ((), ()))
NT = (((1,), (1,)), ((), ()))
TN = (((0,), (0,)), ((), ()))

WEIGHTS = ['c_ctx', 'w_ada', 'b_ada', 'g_pre_mix', 'g_post_mix', 'g_pre_mlp', 'g_post_mlp', 'w_in', 'w_decay',
           'b_decay', 'g_gla', 'w_gla_o', 'w_dw', 'b_dw', 'g_conv_ln', 'b_conv_ln', 'w_conv_o', 'w_pool_g',
           's_pool', 'w_pool_o', 'b_gate', 'w_out', 'w_mlp1', 'w_mlp2']
BIG = {'w_ada': 2, 'w_in': 2, 'w_gla_o': 1, 'w_conv_o': 2, 'w_pool_o': 2, 'w_out': 1, 'w_mlp1': 2, 'w_mlp2': 1}
SMALL_SHARDED = {'w_decay': 3, 'b_decay': 2, 'w_dw': 2, 'b_gate': 2}
SMALL = [n for n in WEIGHTS if n not in BIG]


def _tile(n, prefs):
    for t in prefs:
        if n % t == 0:
            return t
    return n


def _cparams(sem=None, **kw):
    return pltpu.CompilerParams(dimension_semantics=sem, vmem_limit_bytes=VMEM_LIMIT_V7X, **kw)


def _dot(a, b, dims=NN):
    return lax.dot_general(a.astype(MM_DTYPE), b.astype(MM_DTYPE), dims, preferred_element_type=F32)


def matmul(a, b, mode, out_dtype, name, tm=None, tn=None, tk=None, epi=None, extras=(), into=None):
    a, aw, ablk = a if isinstance(a, tuple) else (a, a.shape[1], 0)
    if mode == 'nn':
        M, K, N = a.shape[0], aw, b.shape[1]
    elif mode == 'nt':
        M, K, N = a.shape[0], aw, b.shape[0]
    else:
        K, M, N = a.shape[0], aw, b.shape[1]
    big = (1088, 1024, 640, 544, 512, 320, 256, 128, 64, 32, 16, 8)
    if mode == 'tn':
        tm = tm or _tile(M, (1024, 512, 256, 128))
        tn = tn or _tile(N, (1024, 512, 256, 128))
        tk = tk or _tile(K, big)
    else:
        tm = tm or _tile(M, big)
        tn = tn or _tile(N, (1024, 512, 256, 128))
        tk = tk or _tile(K, (1024, 512, 256, 128))
    if aw != a.shape[1]:
        assert (mode == 'tn' and tm == aw) or (mode != 'tn' and tk == aw)
    nk = K // tk
    ne = len(extras)
    dims = {'nn': NN, 'nt': NT, 'tn': TN}[mode]

    def body(a_ref, b_ref, *rest):
        e_refs, o_ref = rest[:ne], rest[ne + (into is not None)]

        def finish(acc):
            if epi is not None:
                acc = epi(acc, *[e[...] for e in e_refs])
            o_ref[...] = acc.astype(o_ref.dtype)

        p = _dot(a_ref[...], b_ref[...], dims)
        if nk == 1:
            finish(p)
            return
        acc = rest[-1]
        k = pl.program_id(2)

        @pl.when(k == 0)
        def _():
            acc[...] = p

        @pl.when(k > 0)
        def _():
            acc[...] += p

        @pl.when(k == nk - 1)
        def _():
            finish(acc[...])

    if mode == 'nn':
        a_spec = pl.BlockSpec((tm, tk), lambda i, j, k: (i, k + ablk))
        b_spec = pl.BlockSpec((tk, tn), lambda i, j, k: (k, j))
    elif mode == 'nt':
        a_spec = pl.BlockSpec((tm, tk), lambda i, j, k: (i, k + ablk))
        b_spec = pl.BlockSpec((tn, tk), lambda i, j, k: (j, k))
    else:
        a_spec = pl.BlockSpec((tk, tm), lambda i, j, k: (k, i + ablk))
        b_spec = pl.BlockSpec((tk, tn), lambda i, j, k: (k, j))
    tile = pl.BlockSpec((tm, tn), lambda i, j, k: (i, j))
    if into is None:
        out_spec, out_shape, more, extra, aliases = tile, jax.ShapeDtypeStruct((M, N), out_dtype), [], [], {}
    else:
        buf, oblk = into
        out_spec = pl.BlockSpec((tm, tn), lambda i, j, k: (i, oblk * (N // tn) + j))
        out_shape = jax.ShapeDtypeStruct(buf.shape, buf.dtype)
        more, extra, aliases = [pl.BlockSpec(memory_space=pl.ANY)], [buf], {2 + ne: 0}
    return pl.pallas_call(
        body, name=name, grid=(M // tm, N // tn, nk),
        in_specs=[a_spec, b_spec] + [tile] * ne + more, out_specs=out_spec,
        out_shape=out_shape, input_output_aliases=aliases,
        scratch_shapes=[] if nk == 1 else [pltpu.VMEM((tm, tn), F32)],
        compiler_params=_cparams(("parallel", "parallel", "arbitrary")),
    )(a, b, *extras, *extra)


def group_mm(a, w, mode, out_dtype, name, b=None):
    T = a.shape[0]
    G, gc, _ = w.shape
    col = pl.BlockSpec((T, gc), lambda g: (0, g))
    wsp = pl.BlockSpec((1, gc, gc), lambda g: (g, 0, 0))
    if mode == 'tn':
        def body(a_ref, b_ref, o_ref):
            o_ref[0] = _dot(a_ref[...], b_ref[...], TN).astype(o_ref.dtype)
        return pl.pallas_call(body, name=name, grid=(G,), in_specs=[col, col], out_specs=wsp,
                              out_shape=jax.ShapeDtypeStruct((G, gc, gc), out_dtype),
                              compiler_params=_cparams(("parallel",)))(a, b)
    dims = NN if mode == 'nn' else NT

    def body(a_ref, w_ref, o_ref):
        o_ref[...] = _dot(a_ref[...], w_ref[0], dims).astype(o_ref.dtype)
    return pl.pallas_call(body, name=name, grid=(G,), in_specs=[col, wsp], out_specs=col,
                          out_shape=jax.ShapeDtypeStruct((T, G * gc), out_dtype),
                          compiler_params=_cparams(("parallel",)))(a, w)


def _rowspec(r):
    return r if isinstance(r, tuple) else (r, r.shape[1], 0)


def _row_specs(rows, segs, consts, tm, nctx):
    specs = [pl.BlockSpec((tm, w), lambda i, b=b: (i, b)) for _, w, b in rows]
    specs += [pl.BlockSpec((1,) + s.shape[1:], lambda i, n=s.ndim: (jnp.where(i >= nctx, 1, 0),) + (0,) * (n - 1))
              for s in segs]
    specs += [pl.BlockSpec(c.shape, lambda i, n=c.ndim: (0,) * n) for c in consts]
    return specs


def rowwise(fn, rows, segs, consts, outs, dm, name, tm=None):
    tm = tm or dm.tm
    nctx = dm.CTX // tm
    rows = [_rowspec(r) for r in rows]
    nr, ns, nc = len(rows), len(segs), len(consts)

    def body(*refs):
        rin = [r[...] for r in refs[:nr]]
        sin = [s[0] for s in refs[nr:nr + ns]]
        cin = [c[...] for c in refs[nr + ns:nr + ns + nc]]
        res = fn(*rin, *sin, *cin)
        for o_ref, v in zip(refs[nr + ns + nc:], res):
            o_ref[...] = v.astype(o_ref.dtype)

    res = pl.pallas_call(
        body, name=name, grid=(dm.T // tm,),
        in_specs=_row_specs(rows, segs, consts, tm, nctx),
        out_specs=[pl.BlockSpec((tm, w), lambda i: (i, 0)) for w, _ in outs],
        out_shape=[jax.ShapeDtypeStruct((dm.T, w), dt) for w, dt in outs],
        compiler_params=_cparams(("parallel",)),
    )(*[r[0] for r in rows], *segs, *consts)
    return res


def rowwise_vjp(fn, rows, segs, consts, cots, dm, name, tm=None, want=None, adds=None, narrow=(), into=None):
    tm = tm or dm.tm
    nctx = dm.CTX // tm
    rows = [_rowspec(r) for r in rows]
    cots = [_rowspec(r) for r in cots]
    adds = adds or {}
    nr, ns, nc, nct = len(rows), len(segs), len(consts), len(cots)
    want = want or [True] * nr
    widx = [k for k in range(nr) if want[k]]
    akeys = sorted(adds)

    def body(*refs):
        i = pl.program_id(0)
        rin = [r[...] for r in refs[:nr]]
        sin = [s[0] for s in refs[nr:nr + ns]]
        cin = [c[...] for c in refs[nr + ns:nr + ns + nc]]
        p = nr + ns + nc
        cot_refs = refs[p:p + nct]
        add_refs = dict(zip(akeys, refs[p + nct:p + nct + len(akeys)]))
        p = p + nct + len(akeys) + (1 if (into is not None and into[1] is not None) else 0)
        rg_refs = refs[p:p + len(widx)]
        sg_refs = refs[p + len(widx):p + len(widx) + ns]
        cg_refs = refs[p + len(widx) + ns:]
        res, vjp = jax.vjp(fn, *rin, *sin, *cin)
        g = vjp(tuple(cr[...].astype(o.dtype) for cr, o in zip(cot_refs, res)))
        for o_ref, k in zip(rg_refs, widx):
            v = g[k].astype(F32)
            if k in add_refs:
                v = v + add_refs[k][...]
            o_ref[...] = v.astype(o_ref.dtype)
        first_seg = jnp.logical_or(i == 0, i == nctx)
        for o_ref, v in zip(sg_refs, g[nr:nr + ns]):
            @pl.when(first_seg)
            def _(o_ref=o_ref, v=v):
                o_ref[0] = v.astype(F32)

            @pl.when(jnp.logical_not(first_seg))
            def _(o_ref=o_ref, v=v):
                o_ref[0] += v.astype(F32)
        for o_ref, v in zip(cg_refs, g[nr + ns:]):
            @pl.when(i == 0)
            def _(o_ref=o_ref, v=v):
                o_ref[...] = v.astype(F32)

            @pl.when(i > 0)
            def _(o_ref=o_ref, v=v):
                o_ref[...] += v.astype(F32)

    in_specs = _row_specs(rows, segs, consts, tm, nctx)
    in_specs += [pl.BlockSpec((tm, w), lambda i, b=b: (i, b)) for _, w, b in cots]
    in_specs += [pl.BlockSpec((tm, adds[k].shape[1]), lambda i: (i, 0)) for k in akeys]
    out_specs = [pl.BlockSpec((tm, rows[k][1]), lambda i: (i, 0)) for k in widx]
    out_shape = [jax.ShapeDtypeStruct((dm.T, rows[k][1]), MM_DTYPE if k in narrow else rows[k][0].dtype)
                 for k in widx]
    extra, aliases = [], {}
    if into is not None:
        ik, ibuf, ishape = into
        out_specs[widx.index(ik)] = pl.BlockSpec((tm, rows[ik][1]), lambda i, b=rows[ik][2]: (i, b))
        out_shape[widx.index(ik)] = jax.ShapeDtypeStruct(ishape, MM_DTYPE)
        if ibuf is not None:
            aliases = {len(in_specs): widx.index(ik)}
            in_specs = in_specs + [pl.BlockSpec(memory_space=pl.ANY)]
            extra = [ibuf]
    out_specs += [pl.BlockSpec((1,) + s.shape[1:], lambda i, n=s.ndim: (jnp.where(i >= nctx, 1, 0),) + (0,) * (n - 1))
                  for s in segs]
    out_shape += [jax.ShapeDtypeStruct(s.shape, F32) for s in segs]
    out_specs += [pl.BlockSpec(c.shape, lambda i, n=c.ndim: (0,) * n) for c in consts]
    out_shape += [jax.ShapeDtypeStruct(c.shape, F32) for c in consts]
    res = pl.pallas_call(
        body, name=name, grid=(dm.T // tm,), in_specs=in_specs, out_specs=out_specs, out_shape=out_shape,
        input_output_aliases=aliases, compiler_params=_cparams(("arbitrary",)),
    )(*[r[0] for r in rows], *segs, *consts, *[r[0] for r in cots], *[adds[k] for k in akeys], *extra)
    rg = [None] * nr
    for k, v in zip(widx, res[:len(widx)]):
        rg[k] = v
    return rg, list(res[len(widx):len(widx) + ns]), list(res[len(widx) + ns:])


def _rms(x, g):
    return x * lax.rsqrt(jnp.mean(x * x, axis=-1, keepdims=True) + EPS) * g


def _sigmoid(x):
    return jax.nn.sigmoid(x)


def pre_fn(x, shift, scale, g):
    return ((_rms(x, g) * (1.0 + scale) + shift).astype(MM_DTYPE),)


def mid_fn(x, y, gate, shift, scale, g_post, g_pre):
    x1 = x + gate * _rms(y.astype(F32), g_post)
    return x1, (_rms(x1, g_pre) * (1.0 + scale) + shift).astype(MM_DTYPE)


def post_fn(x1, y2, gate, g):
    return (x1 + gate * _rms(y2.astype(F32), g),)


def relu2_epi(acc):
    r = jnp.maximum(acc, 0.0)
    return r * r


def relu2_bwd_epi(dact, act):
    return dact * (2.0 * jnp.sqrt(act.astype(F32)))


def decay_fn(z, bd):
    zz = z.astype(F32) + bd
    ls = jnp.minimum(zz, 0.0) - jnp.log(1.0 + jnp.exp(jnp.minimum(zz, -zz)))
    la = ls / GLA_TAU
    gk = la.shape[1] // 2
    return la[:, :gk], la[:, gk:]


def glu_fn(ab):
    h = ab.shape[1] // 2
    return (ab[:, :h].astype(F32) * _sigmoid(ab[:, h:].astype(F32)),)


def glaout_fn(o_f, o_b, og, g):
    o = o_f + o_b
    dv = o.shape[1] // N_HEADS
    hs = []
    for h in range(N_HEADS):
        oh = o[:, h * dv:(h + 1) * dv]
        hs.append(oh * lax.rsqrt(jnp.mean(oh * oh, axis=-1, keepdims=True) + EPS) * g[:, h * dv:(h + 1) * dv])
    og = og.astype(F32)
    return ((jnp.concatenate(hs, axis=1) * (og * _sigmoid(og))).astype(MM_DTYPE),)


def convpost_fn(y, b_dw, g, b):
    y = y + b_dw
    mu = jnp.mean(y, axis=-1, keepdims=True)
    xc = y - mu
    yn = xc * lax.rsqrt(jnp.mean(xc * xc, axis=-1, keepdims=True) + EPS) * g + b
    return ((yn * _sigmoid(yn)).astype(MM_DTYPE),)


def poolpost_fn(pc, s):
    return ((pc.astype(F32) * s).astype(MM_DTYPE),)


def merge_fn(ya, yb, yc, mg, bg0, bg1, bg2):
    d = ya.shape[1]
    mg = mg.astype(F32)
    mixed = (_sigmoid(mg[:, :d] + bg0) * ya.astype(F32) + _sigmoid(mg[:, d:2 * d] + bg1) * yb.astype(F32)
             + _sigmoid(mg[:, 2 * d:] + bg2) * yc.astype(F32))
    return (mixed.astype(MM_DTYPE),)


def _split_dot(lmat, x, dims):
    hi = x.astype(MM_DTYPE)
    lo = x - hi.astype(F32)
    return _dot(lmat, hi, dims) + _dot(lmat, lo, dims)


def _gla_block_order(dm, rev):
    nctx, nb = dm.CTX // dm.TB, dm.T // dm.TB

    def blk(i):
        if not rev:
            return i
        return jnp.where(i < nctx, nctx - 1 - i, nb - 1 - (i - nctx))
    return blk, nb


def _gla_tri(rev):
    c = GLA_CHUNK
    t = lax.broadcasted_iota(jnp.int32, (c, c), 0)
    s = lax.broadcasted_iota(jnp.int32, (c, c), 1)
    return (s >= t) if rev else (s <= t)


def _gla_cumsum(la, tri):
    lmat = tri.astype(MM_DTYPE)
    return lmat, _split_dot(lmat, la, NN), jnp.sum(la, axis=0, keepdims=True)


def _gla_chunk_terms(q, k, b, bend, tri, scale):
    eb = jnp.exp(b)
    enb = jnp.exp(-b)
    ee = jnp.exp(bend - b)
    qi = q * scale * eb
    ki = k * enb
    kend = k * ee
    att = jnp.where(tri, _dot(qi, ki, NT), 0.0)
    return eb, enb, ee, qi, ki, kend, att


def gla_fwd(P, la, rev, dm, name):
    c, tb, h_, dk, dv, d = GLA_CHUNK, dm.TB, N_HEADS, dm.DK, dm.DV, dm.D
    cpb = tb // c
    blk, nb = _gla_block_order(dm, rev)
    gk, gv = h_ * dk, h_ * dv
    qb, kb, vb, lb = (5 * d) // gk, (5 * d + d // 2) // gk, (4 * d) // gv, (1 if rev else 0)
    scale = dk ** -0.5
    order = list(range(cpb))[::-1] if rev else list(range(cpb))

    def body(q_ref, k_ref, v_ref, la_ref, o_ref, s_ref, st):
        @pl.when(pl.program_id(0) == 0)
        def _():
            st[...] = jnp.zeros_like(st)
        tri = _gla_tri(rev)
        terms = {}
        for n, ci in enumerate(order):
            r = pl.ds(ci * c, c)
            _, b_all, bend_all = _gla_cumsum(la_ref[r, :], tri)
            for hh in range(h_):
                ck, cv = pl.ds(hh * dk, dk), pl.ds(hh * dv, dv)
                hs = slice(hh * dk, (hh + 1) * dk)
                v = v_ref[r, cv]
                _, _, _, qi, _, kend, att = _gla_chunk_terms(
                    q_ref[r, ck].astype(F32), k_ref[r, ck].astype(F32), b_all[:, hs], bend_all[:, hs], tri, scale)
                terms[n, hh] = (_dot(att, v), qi.astype(MM_DTYPE), jnp.exp(bend_all[:, hs]), _dot(v, kend, TN))
        for n, ci in enumerate(order):
            r = pl.ds(ci * c, c)
            for hh in range(h_):
                intra, qi, gam, dstate = terms[n, hh]
                s_in = st[hh]
                o_ref[r, pl.ds(hh * dv, dv)] = intra + _dot(qi, s_in, NT)
                s_ref[n, hh] = s_in
                st[hh] = gam * s_in + dstate

    return pl.pallas_call(
        body, name=name, grid=(nb,),
        in_specs=[pl.BlockSpec((tb, gk), lambda i: (blk(i), qb)),
                  pl.BlockSpec((tb, gk), lambda i: (blk(i), kb)),
                  pl.BlockSpec((tb, gv), lambda i: (blk(i), vb)),
                  pl.BlockSpec((tb, gk), lambda i: (blk(i), lb))],
        out_specs=[pl.BlockSpec((tb, gv), lambda i: (blk(i), 0)),
                   pl.BlockSpec((cpb, h_, dv, dk), lambda i: (i, 0, 0, 0))],
        out_shape=[jax.ShapeDtypeStruct((dm.T, gv), F32),
                   jax.ShapeDtypeStruct((dm.T // c, h_, dv, dk), F32)],
        scratch_shapes=[pltpu.VMEM((h_, dv, dk), F32)],
        compiler_params=_cparams(("arbitrary",)),
    )(P, P, P, la)


def gla_bwd(P, la, do, states, rev, dm, name, prev=None, into=None):
    c, tb, h_, dk, dv, d = GLA_CHUNK, dm.TB, N_HEADS, dm.DK, dm.DV, dm.D
    cpb = tb // c
    blk, nb = _gla_block_order(dm, rev)
    gk, gv = h_ * dk, h_ * dv
    qb, kb, vb, lb = (5 * d) // gk, (5 * d + d // 2) // gk, (4 * d) // gv, (1 if rev else 0)
    scale = dk ** -0.5
    order = list(range(cpb))[::-1] if rev else list(range(cpb))

    fused = prev is not None

    def body(q_ref, k_ref, v_ref, la_ref, do_ref, s_ref, *rest):
        if fused:
            pq_ref, pk_ref, pv_ref, _, w_ref, dla_ref, dst = rest
        else:
            dq_ref, dk_ref, dv_ref, dla_ref, dst = rest

        def put(kind, r, cols, val):
            if not fused:
                {'q': dq_ref, 'k': dk_ref, 'v': dv_ref}[kind][r, cols] = val
                return
            p_ref, off = {'q': (pq_ref, gv), 'k': (pk_ref, gv + gk), 'v': (pv_ref, 0)}[kind]
            w_ref[r, pl.ds(off + cols.start, cols.size)] = (val + p_ref[r, cols]).astype(w_ref.dtype)

        @pl.when(pl.program_id(0) == 0)
        def _():
            dst[...] = jnp.zeros_like(dst)
        tri = _gla_tri(rev)
        for n in range(cpb - 1, -1, -1):
            r = pl.ds(order[n] * c, c)
            for hh in range(h_):
                ck, cv = pl.ds(hh * dk, dk), pl.ds(hh * dv, dv)
                q = q_ref[r, ck].astype(F32)
                k = k_ref[r, ck].astype(F32)
                v = v_ref[r, cv]
                lmat, b, bend = _gla_cumsum(la_ref[r, ck], tri)
                eb, enb, ee, qi, ki, kend, att = _gla_chunk_terms(q, k, b, bend, tri, scale)
                s_in = s_ref[n, hh]
                ds_out = dst[hh]
                dob = do_ref[r, cv]
                datt = jnp.where(tri, _dot(dob, v, NT), 0.0)
                dqi = _dot(datt, ki) + _dot(dob, s_in)
                dki = _dot(datt, qi, TN)
                put('v', r, cv, _dot(att, dob, TN) + _dot(kend, ds_out, NT))
                dkend = _dot(v, ds_out)
                gam = jnp.exp(bend)
                dgam = jnp.sum(ds_out * s_in, axis=0, keepdims=True)
                dst[hh] = gam * ds_out + _dot(dob, qi, TN)
                put('q', r, ck, dqi * (scale * eb))
                put('k', r, ck, dki * enb + dkend * ee)
                db = dqi * qi - dki * ki - dkend * kend
                dbend = jnp.sum(dkend * kend, axis=0, keepdims=True) + dgam * gam
                dla_ref[r, ck] = _split_dot(lmat, db, TN) + dbend

    def bi(j):
        return blk(nb - 1 - j)

    in_specs = [
        pl.BlockSpec((tb, gk), lambda j: (bi(j), qb)),
        pl.BlockSpec((tb, gk), lambda j: (bi(j), kb)),
        pl.BlockSpec((tb, gv), lambda j: (bi(j), vb)),
        pl.BlockSpec((tb, gk), lambda j: (bi(j), lb)),
        pl.BlockSpec((tb, gv), lambda j: (bi(j), 0)),
        pl.BlockSpec((cpb, h_, dv, dk), lambda j: (nb - 1 - j, 0, 0, 0)),
    ]
    small = pl.BlockSpec((tb, gk), lambda j: (bi(j), 0))
    wide = pl.BlockSpec((tb, gv), lambda j: (bi(j), 0))
    if not fused:
        return pl.pallas_call(
            body, name=name, grid=(nb,), in_specs=in_specs, out_specs=[small, small, wide, small],
            out_shape=[jax.ShapeDtypeStruct((dm.T, gk), F32), jax.ShapeDtypeStruct((dm.T, gk), F32),
                       jax.ShapeDtypeStruct((dm.T, gv), F32), jax.ShapeDtypeStruct((dm.T, gk), F32)],
            scratch_shapes=[pltpu.VMEM((h_, dv, dk), F32)],
            compiler_params=_cparams(("arbitrary",)),
        )(P, P, P, la, do, states)
    return pl.pallas_call(
        body, name=name, grid=(nb,),
        in_specs=in_specs + [small, small, wide, pl.BlockSpec(memory_space=pl.ANY)],
        out_specs=[pl.BlockSpec((tb, 2 * gv), lambda j: (bi(j), vb // 2)), small],
        out_shape=[jax.ShapeDtypeStruct(into.shape, into.dtype), jax.ShapeDtypeStruct((dm.T, gk), F32)],
        input_output_aliases={9: 0},
        scratch_shapes=[pltpu.VMEM((h_, dv, dk), F32)],
        compiler_params=_cparams(("arbitrary",)),
    )(P, P, P, la, do, states, *prev, into)


def _pos(n, period):
    t = lax.broadcasted_iota(jnp.int32, (n, 1), 0)
    if period & (period - 1) == 0:
        return jnp.bitwise_and(t, period - 1)
    return lax.rem(t, period)


def _conv_segments(dm):
    return [(0, dm.CTX, dm.CTX), (dm.CTX, dm.SEQ, GRID_W)]


def conv_fwd(u, w, dm, name):
    kw, cw = w.shape
    segs = _conv_segments(dm)

    def body(u_ref, w_ref, y_ref):
        for r0, n, per in segs:
            useg = u_ref[r0:r0 + n, :]
            p = _pos(n, per)
            acc = jnp.zeros_like(useg)
            for kk in range(kw):
                d = kk - kw // 2
                sh = useg if d == 0 else pltpu.roll(useg, (-d) % n, 0)
                ok = jnp.logical_and(p + d >= 0, p + d < per)
                acc = acc + jnp.where(ok, sh, 0.0) * w_ref[kk:kk + 1, :]
            y_ref[r0:r0 + n, :] = acc

    return pl.pallas_call(
        body, name=name, grid=(cw // LANES,),
        in_specs=[pl.BlockSpec((dm.T, LANES), lambda j: (0, j)), pl.BlockSpec((kw, LANES), lambda j: (0, j))],
        out_specs=pl.BlockSpec((dm.T, LANES), lambda j: (0, j)),
        out_shape=jax.ShapeDtypeStruct((dm.T, cw), F32),
        compiler_params=_cparams(("parallel",)),
    )(u, w)


def conv_bwd(u, w, dy, dm, name):
    kw, cw = w.shape
    segs = _conv_segments(dm)

    def body(u_ref, w_ref, dy_ref, du_ref, dw_ref):
        dws = [jnp.zeros((1, LANES), F32)] * kw
        for r0, n, per in segs:
            useg = u_ref[r0:r0 + n, :]
            dyseg = dy_ref[r0:r0 + n, :]
            p = _pos(n, per)
            acc = jnp.zeros_like(useg)
            for kk in range(kw):
                d = kk - kw // 2
                shu = useg if d == 0 else pltpu.roll(useg, (-d) % n, 0)
                okf = jnp.logical_and(p + d >= 0, p + d < per)
                dws[kk] = dws[kk] + jnp.sum(jnp.where(okf, shu, 0.0) * dyseg, axis=0, keepdims=True)
                shd = dyseg if d == 0 else pltpu.roll(dyseg, d % n, 0)
                okb = jnp.logical_and(p - d >= 0, p - d < per)
                acc = acc + jnp.where(okb, shd, 0.0) * w_ref[kk:kk + 1, :]
            du_ref[r0:r0 + n, :] = acc
        for kk in range(kw):
            dw_ref[kk:kk + 1, :] = dws[kk]

    return pl.pallas_call(
        body, name=name, grid=(cw // LANES,),
        in_specs=[pl.BlockSpec((dm.T, LANES), lambda j: (0, j)), pl.BlockSpec((kw, LANES), lambda j: (0, j)),
                  pl.BlockSpec((dm.T, LANES), lambda j: (0, j))],
        out_specs=[pl.BlockSpec((dm.T, LANES), lambda j: (0, j)), pl.BlockSpec((kw, LANES), lambda j: (0, j))],
        out_shape=[jax.ShapeDtypeStruct((dm.T, cw), F32), jax.ShapeDtypeStruct((kw, cw), F32)],
        compiler_params=_cparams(("parallel",)),
    )(u, w, dy)


def pool_mix(u, transpose, dm, name, into=None):
    u, uw, ublk = _rowspec(u)
    gc = dm.GC
    ng = len(POOL_WINDOWS)
    rows = dm.SEQ // GRID_W
    segs = [(0, dm.CTX, 1, dm.CTX), (dm.CTX, dm.SEQ, GRID_W, rows)]

    def one_group(u_ref, o_ref, win):
        left = win // 2
        right = win - 1 - left
        for r0, n, stride, length in segs:
            useg = u_ref[r0:r0 + n, :].astype(F32)
            t = lax.broadcasted_iota(jnp.int32, (n, 1), 0)
            p = t if stride == 1 else jnp.right_shift(t, stride.bit_length() - 1)
            cnt = (jnp.minimum(p + right + 1, length) - jnp.maximum(p - left, 0)).astype(F32)
            src = useg / cnt if transpose else useg
            acc = jnp.zeros_like(useg)
            for d in range(-left, right + 1):
                dd = -d if transpose else d
                sh = src if d == 0 else pltpu.roll(src, (-dd * stride) % n, 0)
                ok = jnp.logical_and(p + dd >= 0, p + dd < length)
                acc = acc + jnp.where(ok, sh, 0.0)
            o_ref[r0:r0 + n, :] = ((acc - useg) if transpose else (acc / cnt - useg)).astype(o_ref.dtype)

    def body(u_ref, *rest):
        o_ref = rest[-1]
        g = pl.program_id(0)
        for gi, win in enumerate(POOL_WINDOWS):
            @pl.when(g == gi)
            def _(win=win):
                one_group(u_ref, o_ref, win)

    base = ublk * (uw // gc)
    if into is None:
        obase, out_shape, more, extra, aliases = 0, jax.ShapeDtypeStruct((dm.T, ng * gc), F32), [], [], {}
    else:
        buf, oblk = into
        obase, out_shape = oblk * ng, jax.ShapeDtypeStruct(buf.shape, buf.dtype)
        more, extra, aliases = [pl.BlockSpec(memory_space=pl.ANY)], [buf], {1: 0}
    return pl.pallas_call(
        body, name=name, grid=(ng,),
        in_specs=[pl.BlockSpec((dm.T, gc), lambda g: (0, base + g))] + more,
        out_specs=pl.BlockSpec((dm.T, gc), lambda g: (0, obase + g)),
        out_shape=out_shape, input_output_aliases=aliases,
        compiler_params=_cparams(("parallel",)),
    )(u, *extra)


def loss_head(x2, target, dm, name):
    tm, d = dm.tm, dm.D
    nctx = dm.CTX // tm

    def body(x_ref, t_ref, dx_ref, l_ref):
        i = pl.program_id(0)

        @pl.when(i == 0)
        def _():
            l_ref[...] = jnp.zeros_like(l_ref)

        @pl.when(i < nctx)
        def _():
            dx_ref[...] = jnp.zeros_like(dx_ref)

        @pl.when(i >= nctx)
        def _():
            e = x_ref[...] - t_ref[...]
            dx_ref[...] = e / d
            l_ref[...] += jnp.full(l_ref.shape, 0.5 * jnp.sum(jnp.mean(e * e, axis=-1)), F32)

    return pl.pallas_call(
        body, name=name, grid=(dm.T // tm,),
        in_specs=[pl.BlockSpec((tm, d), lambda i: (i, 0)),
                  pl.BlockSpec((tm, d), lambda i: (jnp.maximum(i - nctx, 0), 0))],
        out_specs=[pl.BlockSpec((tm, d), lambda i: (i, 0)), pl.BlockSpec((8, LANES), lambda i: (0, 0))],
        out_shape=[jax.ShapeDtypeStruct((dm.T, d), F32), jax.ShapeDtypeStruct((8, LANES), F32)],
        compiler_params=_cparams(("arbitrary",)),
    )(x2, target)


def adamw(w, g, m, v, name):
    r, c = w.shape
    tr = _tile(r, tuple(t for t in (512, 256, 128, 64, 32, 16, 8) if t * c * 4 <= (1 << 20)) or (8,))

    def body(w_ref, g_ref, m_ref, v_ref, d_ref, mo_ref, vo_ref):
        gg = g_ref[...]
        mm = ADAM_B1 * m_ref[...] + (1.0 - ADAM_B1) * gg
        vv = ADAM_B2 * v_ref[...] + (1.0 - ADAM_B2) * (gg * gg)
        m_hat = mm / (1.0 - ADAM_B1 ** ADAM_STEP)
        v_hat = vv / (1.0 - ADAM_B2 ** ADAM_STEP)
        d_ref[...] = -ADAM_LR * (m_hat / (jnp.sqrt(v_hat) + ADAM_EPS) + ADAM_WD * w_ref[...])
        mo_ref[...] = mm
        vo_ref[...] = vv

    spec = pl.BlockSpec((tr, c), lambda i: (i, 0))
    return pl.pallas_call(
        body, name=name, grid=(r // tr,), in_specs=[spec] * 4, out_specs=[spec] * 3,
        out_shape=[jax.ShapeDtypeStruct((r, c), F32)] * 3,
        compiler_params=_cparams(("parallel",)),
    )(w, g, m, v)


def pair_add(g, r1, cidx, name):
    ng, r_, n_ = r1.shape
    tr = _tile(r_, tuple(t for t in (1024, 512, 256, 128, 64, 32, 16) if t * n_ * 4 <= (2 << 20)))

    def body(s_ref, g_ref, r_ref, o_ref):
        o_ref[...] = (g_ref[...].astype(F32) + r_ref[...].astype(F32)).astype(o_ref.dtype)

    return pl.pallas_call(
        body, name=name,
        grid_spec=pltpu.PrefetchScalarGridSpec(
            num_scalar_prefetch=1, grid=(ng, r_ // tr),
            in_specs=[pl.BlockSpec((None, tr, n_), lambda k, i, s: (2 * k + s[0], i, 0)),
                      pl.BlockSpec((None, tr, n_), lambda k, i, s: (k, i, 0))],
            out_specs=pl.BlockSpec((None, tr, n_), lambda k, i, s: (k, i, 0))),
        out_shape=jax.ShapeDtypeStruct((ng, r_, n_), g.dtype),
        compiler_params=_cparams(("parallel", "parallel")),
    )(cidx, g, r1)


def chip_add(h, r2, axis, where, name, slab=True):
    _, kl, nl = r2.shape
    tr = _tile(kl, tuple(t for t in (1024, 512, 256, 128, 64, 32, 16) if t * nl * 4 <= (1 << 20)))
    nrb = kl // tr

    def body(s_ref, h_ref, r_ref, o_ref):
        acc = h_ref[...].astype(F32)
        for k in range(r2.shape[0]):
            acc = acc + r_ref[k].astype(F32)
        o_ref[...] = acc

    h_map = (lambda i, s: (s[0] * nrb + i, 0)) if axis == 0 else (lambda i, s: (i, s[0]))
    if slab:
        out_spec = pl.BlockSpec((None, tr, nl), lambda i, s: (s[1], i, 0))
        out_shape = jax.ShapeDtypeStruct((2, kl, nl), F32)
    else:
        out_spec = pl.BlockSpec((tr, nl), lambda i, s: (i, 0))
        out_shape = jax.ShapeDtypeStruct((kl, nl), F32)
    return pl.pallas_call(
        body, name=name,
        grid_spec=pltpu.PrefetchScalarGridSpec(
            num_scalar_prefetch=1, grid=(nrb,),
            in_specs=[pl.BlockSpec((tr, nl), h_map),
                      pl.BlockSpec((r2.shape[0], tr, nl), lambda i, s: (0, i, 0))],
            out_specs=out_spec),
        out_shape=out_shape,
        compiler_params=_cparams(("parallel",)),
    )(where, h, r2)


def adamw_layers(w, m, v, terms, name, layer=None, prev=None):
    _, a_, b_ = w.shape
    tr = _tile(a_, tuple(t for t in (512, 256, 128, 64, 32) if t * b_ * 4 <= (1 << 20)))
    by_cols = tr == a_ and a_ * b_ * 4 > (1 << 20)
    blk = (a_, LANES) if by_cols else (tr, b_)
    steps = b_ // LANES if by_cols else a_ // tr
    at = (lambda i: (0, i)) if by_cols else (lambda i: (i, 0))
    layers = (0, 1) if layer is None else (layer,)
    counts = [len(terms[l]) for l in layers]
    nprev = 0 if prev is None else 4

    def update(g, w_ref, m_ref, v_ref, g_ref, d_ref, mo_ref, vo_ref):
        mm = ADAM_B1 * m_ref[...] + (1.0 - ADAM_B1) * g
        vv = ADAM_B2 * v_ref[...] + (1.0 - ADAM_B2) * (g * g)
        m_hat = mm / (1.0 - ADAM_B1 ** ADAM_STEP)
        v_hat = vv / (1.0 - ADAM_B2 ** ADAM_STEP)
        g_ref[...] = g
        d_ref[...] = -ADAM_LR * (m_hat / (jnp.sqrt(v_hat) + ADAM_EPS) + ADAM_WD * w_ref[...])
        mo_ref[...] = mm
        vo_ref[...] = vv

    def total(refs):
        g = refs[0][...]
        for r in refs[1:]:
            g = g + r[...]
        return g

    def body(w_ref, m_ref, v_ref, *rest):
        t_refs, outs = rest[:sum(counts)], rest[-4:]
        if len(layers) == 1:
            update(total(t_refs), w_ref, m_ref, v_ref, *outs)
            return
        which = pl.program_id(0)

        @pl.when(which == 0)
        def _():
            update(total(t_refs[:counts[0]]), w_ref, m_ref, v_ref, *outs)

        @pl.when(which == 1)
        def _():
            update(total(t_refs[counts[0]:]), w_ref, m_ref, v_ref, *outs)

    if len(layers) == 1:
        stacked = pl.BlockSpec((None,) + blk, lambda l, i: (layers[0],) + at(i))
        t_specs = [pl.BlockSpec(blk, lambda l, i: at(i))] * counts[0]
    else:
        stacked = pl.BlockSpec((None,) + blk, lambda l, i: (l,) + at(i))
        t_specs = ([pl.BlockSpec(blk, lambda l, i: at(i * (1 - l)))] * counts[0]
                   + [pl.BlockSpec(blk, lambda l, i: at(i * l))] * counts[1])
    nin = 3 + sum(counts)
    return pl.pallas_call(
        body, name=name, grid=(len(layers), steps),
        in_specs=[stacked] * 3 + t_specs + [pl.BlockSpec(memory_space=pl.ANY)] * nprev,
        out_specs=[stacked] * 4, out_shape=[jax.ShapeDtypeStruct(w.shape, F32)] * 4,
        input_output_aliases={nin + j: j for j in range(nprev)},
        compiler_params=_cparams(("arbitrary", "arbitrary")),
    )(w, m, v, *[t for l in layers for t in terms[l]], *(prev or ()))


MESH = pl.DeviceIdType.MESH
ANY = pl.BlockSpec(memory_space=pl.ANY)
HBM = pl.BlockSpec(memory_space=pltpu.HBM)
SEM = pl.BlockSpec(memory_space=pltpu.SEMAPHORE)
EFFECT = pltpu.SideEffectType.DATAFLOW_SIDE_EFFECTING


def _place():
    return lax.axis_index("x"), lax.axis_index("y"), lax.axis_index("c")


def _peers(x, y):
    return [(1 - x, y), (x, 1 - y), (1 - x, 1 - y)]


def _rcopy(src, dst, ssem, rsem, dev):
    return pltpu.make_async_remote_copy(src_ref=src, dst_ref=dst, send_sem=ssem, recv_sem=rsem,
                                        device_id=dev, device_id_type=MESH)


def _gathered_shape(src, kind):
    h, a_, b_ = src.shape
    return (h, a_, 4 * b_) if kind == 'col' else (4, h, a_, b_)


def _win(ref, kind, ch, width):
    return ref.at[:, :, pl.ds(ch * width, width)] if kind == 'col' else ref.at[ch]


def _rect(ref, kind, half, ch, width):
    return ref.at[half, :, pl.ds(ch * width, width)] if kind == 'col' else ref.at[ch, half]


def _gather_plan(kinds, widths):
    def plan(src, land, x, y, c):
        chip = 2 * x + y
        out = []
        for n in range(len(src)):
            for px, py in _peers(x, y):
                out.append((src[n].at[c], _rect(land[n], kinds[n], c, chip, widths[n]), (px, py, c),
                            _rect(land[n], kinds[n], c, 2 * px + py, widths[n])))
            mine = _win(land[n], kinds[n], chip, widths[n])
            out.append((src[n], mine, (x, y, 1 - c), mine))
        return out
    return plan


def forward_halves(lands, kinds, name):
    nw = len(lands)
    widths = [t.shape[-1] // 4 if k == 'col' else t.shape[-1] for t, k in zip(lands, kinds)]

    def body(*refs):
        o = refs[nw:2 * nw]
        ssem, rsem = refs[2 * nw:]
        x, y, c = _place()
        sib = (x, y, 1 - c)
        pidx = [2 * px + py for px, py in _peers(x, y)]
        cps = [_rcopy(_rect(o[n], kinds[n], c, pidx[j], widths[n]), _rect(o[n], kinds[n], c, pidx[j], widths[n]),
                      ssem.at[3 * n + j], rsem.at[3 * n + j], sib) for n in range(nw) for j in range(3)]
        for cp in cps:
            cp.start()
        for n in range(nw):
            for j in range(3):
                cps[3 * n + j].wait_send()
                _rcopy(_rect(o[n], kinds[n], 1 - c, pidx[j], widths[n]), _rect(o[n], kinds[n], 1 - c, pidx[j], widths[n]),
                       ssem.at[3 * n + j], rsem.at[3 * n + j], sib).wait_recv()

    return pl.pallas_call(
        body, name=name, in_specs=[ANY] * nw, out_specs=[ANY] * nw,
        out_shape=[jax.ShapeDtypeStruct(t.shape, t.dtype) for t in lands],
        input_output_aliases={n: n for n in range(nw)},
        scratch_shapes=[pltpu.SemaphoreType.DMA((3 * nw,)), pltpu.SemaphoreType.DMA((3 * nw,))],
    )(*lands)


def _scatter_plan(axes, widths):
    def plan(src, land, x, y, c):
        out = []
        for n in range(len(src)):
            for k, (px, py) in enumerate(_peers(x, y)):
                ch = 2 * px + py
                view = (src[n].at[:, pl.ds(ch * widths[n], widths[n])] if axes[n] == 1
                        else src[n].at[pl.ds(ch * widths[n], widths[n]), :])
                out.append((view, land[n].at[k], (px, py, c), land[n].at[k]))
        return out
    return plan


def _exchange_plan(kinds):
    def plan(src, land, x, y, c):
        out = []
        for n in range(len(src)):
            w = land[n].shape[2]
            for j, (px, py) in enumerate(_peers(x, y)):
                ch = 2 * px + py
                view = src[n].at[:, pl.ds(ch * w, w)] if kinds[n] == 'col' else src[n].at[ch]
                out.append((view, land[n].at[j], (px, py, c), land[n].at[j]))
        return out
    return plan


def start_copies(srcs, lands, plan, ncopies, after, name):
    ns, nl = len(srcs), len(lands)

    def body(*refs):
        src, land = refs[:ns], refs[ns:ns + nl]
        ssem, rsem = refs[ns + nl + 1], refs[ns + nl + 2]
        token = refs[-1]
        x, y, c = _place()
        for k, (sv, dv, dev, _) in enumerate(plan(src, land, x, y, c)):
            _rcopy(sv, dv, ssem.at[k], rsem.at[k], dev).start()
        token[...] = jnp.zeros_like(token)

    hbm = lambda t: pltpu.HBM(t.shape, t.dtype)
    res = pl.pallas_call(
        body, name=name,
        out_shape=(pltpu.SemaphoreType.DMA((ncopies,)), pltpu.SemaphoreType.DMA((ncopies,)),
                   *[hbm(t) for t in srcs], *[hbm(t) for t in lands], jax.ShapeDtypeStruct((8, LANES), F32)),
        in_specs=[HBM] * (ns + nl) + [ANY],
        out_specs=(SEM, SEM, *[HBM] * (ns + nl), pl.BlockSpec(memory_space=pltpu.VMEM)),
        input_output_aliases={k: 2 + k for k in range(ns + nl)},
        compiler_params=pltpu.CompilerParams(has_side_effects=EFFECT),
    )(*[pltpu.with_memory_space_constraint(t, pltpu.HBM) for t in list(srcs) + list(lands)], after)
    return res[0], res[1], list(res[2:2 + ns]), list(res[2 + ns:2 + ns + nl]), res[-1]


def wait_copies(ssem, rsem, srcs, lands, plan, after, name):
    ns, nl = len(srcs), len(lands)

    def body(*refs):
        src, land = refs[:ns], refs[ns:ns + nl]
        ss, rs = refs[ns + nl], refs[ns + nl + 1]
        x, y, c = _place()
        for k, (sv, dv, dev, mine) in enumerate(plan(src, land, x, y, c)):
            cp = _rcopy(sv, mine, ss.at[k], rs.at[k], dev)
            cp.wait_send()
            cp.wait_recv()

    hbm = lambda t: pltpu.HBM(t.shape, t.dtype)
    res = pl.pallas_call(
        body, name=name,
        out_shape=(*[hbm(t) for t in srcs], *[hbm(t) for t in lands]),
        in_specs=[HBM] * (ns + nl) + [SEM, SEM, ANY], out_specs=tuple([HBM] * (ns + nl)),
        input_output_aliases={k: k for k in range(ns + nl)},
        compiler_params=pltpu.CompilerParams(has_side_effects=EFFECT),
    )(*srcs, *lands, ssem, rsem, after)
    return list(res[:ns]), list(res[ns:])


def pair_swap_halves(gs, kinds, name):
    nw = len(gs)

    def other(ref, kind, half):
        return ref.at[half] if kind == 'col' else ref.at[:, half]

    def body(*refs):
        g, o = refs[:nw], refs[nw:2 * nw]
        ssem, rsem = refs[2 * nw:]
        x, y, c = _place()
        cps = [_rcopy(other(g[n], kinds[n], 1 - c), o[n], ssem.at[n], rsem.at[n], (x, y, 1 - c)) for n in range(nw)]
        for cp in cps:
            cp.start()
        for cp in cps:
            cp.wait()

    return pl.pallas_call(
        body, name=name, in_specs=[ANY] * nw, out_specs=[ANY] * nw,
        out_shape=[jax.ShapeDtypeStruct(g.shape[1:] if k == 'col' else (g.shape[0],) + g.shape[2:], g.dtype)
                   for g, k in zip(gs, kinds)],
        scratch_shapes=[pltpu.SemaphoreType.DMA((nw,)), pltpu.SemaphoreType.DMA((nw,))],
    )(*gs)


def pair_swap(fs, name):
    nw = len(fs)

    def body(*refs):
        f, o = refs[:nw], refs[nw:2 * nw]
        ssem, rsem = refs[2 * nw:]
        x, y, c = _place()
        cps = [_rcopy(f[n], o[n], ssem.at[n], rsem.at[n], (x, y, 1 - c)) for n in range(nw)]
        for cp in cps:
            cp.start()
        for cp in cps:
            cp.wait()

    return pl.pallas_call(
        body, name=name, in_specs=[ANY] * nw, out_specs=[ANY] * nw,
        out_shape=[jax.ShapeDtypeStruct(f.shape, f.dtype) for f in fs],
        scratch_shapes=[pltpu.SemaphoreType.DMA((nw,)), pltpu.SemaphoreType.DMA((nw,))],
    )(*fs)


def chip_broadcast(h, name):
    def body(h_ref, o_ref, ssem, rsem):
        x, y, c = _place()
        cps = [_rcopy(h_ref, o_ref.at[j], ssem.at[j], rsem.at[j], (px, py, c)) for j, (px, py) in enumerate(_peers(x, y))]
        for cp in cps:
            cp.start()
        for cp in cps:
            cp.wait()

    return pl.pallas_call(
        body, name=name, in_specs=[ANY], out_specs=ANY,
        out_shape=jax.ShapeDtypeStruct((3,) + h.shape, h.dtype),
        scratch_shapes=[pltpu.SemaphoreType.DMA((3,)), pltpu.SemaphoreType.DMA((3,))],
    )(h)


def quad_sum(h, r, cidx, name):
    r_, c_ = h.shape
    tr = _tile(r_, (512, 256, 128, 64, 32, 16, 8))

    def body(s_ref, h_ref, r_ref, o_ref):
        o_ref[...] = (h_ref[...] + r_ref[2]) + (r_ref[0] + r_ref[1])

    return pl.pallas_call(
        body, name=name,
        grid_spec=pltpu.PrefetchScalarGridSpec(
            num_scalar_prefetch=1, grid=(r_ // tr,),
            in_specs=[pl.BlockSpec((tr, c_), lambda i, s: (i, 0)), pl.BlockSpec((3, tr, c_), lambda i, s: (0, i, 0))],
            out_specs=pl.BlockSpec((None, tr, c_), lambda i, s: (s[0], i, 0))),
        out_shape=jax.ShapeDtypeStruct((2, r_, c_), F32),
        compiler_params=_cparams(("parallel",)),
    )(cidx, h, r)


def pair_join_layers(fs, name):
    nw = len(fs)

    def body(*refs):
        o = refs[nw:2 * nw]
        ssem, rsem = refs[2 * nw:]
        x, y, c = _place()
        sib = (x, y, 1 - c)
        cps = [_rcopy(o[n].at[c], o[n].at[c], ssem.at[n], rsem.at[n], sib) for n in range(nw)]
        for cp in cps:
            cp.start()
        for n in range(nw):
            cps[n].wait_send()
            _rcopy(o[n].at[1 - c], o[n].at[1 - c], ssem.at[n], rsem.at[n], sib).wait_recv()

    return pl.pallas_call(
        body, name=name, in_specs=[ANY] * nw, out_specs=[ANY] * nw,
        out_shape=[jax.ShapeDtypeStruct(f.shape, f.dtype) for f in fs],
        input_output_aliases={n: n for n in range(nw)},
        scratch_shapes=[pltpu.SemaphoreType.DMA((nw,)), pltpu.SemaphoreType.DMA((nw,))],
    )(*fs)


def _flatten_pad(parts, dtype):
    flat = jnp.concatenate([p.reshape(-1).astype(dtype) for p in parts])
    q = 512 * LANES
    n = -(-flat.shape[0] // q) * q
    return jnp.pad(flat, (0, n - flat.shape[0])).reshape(n // LANES, LANES)


def _lane_pad(n):
    return -(-n // LANES) * LANES


def _in_proj_layout(d):
    gk, gv, cw, pw = d // 2, d, d // 2, d // 2
    own = [('q', gk), ('k', gk), ('v', gv), ('og', gv), ('lrf', GLA_LR), ('lrb', GLA_LR), ('ga', cw), ('gb', cw),
           ('pu', pw), ('mg', 3 * d)]
    padded = [('mg', 3 * d), ('og', gv), ('v', gv), ('q', gk), ('k', gk), ('ga', cw), ('gb', cw), ('pu', pw),
              ('lrf', GLA_LR), ('lrb', GLA_LR), ('pad', d // 2 - 2 * GLA_LR)]
    return own, padded


def _row_pieces(src, lo, hi, wl, wlp):
    out = []
    for k in range(4):
        s0, s1 = max(lo, k * wl), min(hi, (k + 1) * wl)
        if s0 < s1:
            out.append(src[k * wlp + s0 - k * wl:k * wlp + s1 - k * wl])
    return out


def _w_in_t_to_proj(g, d, wl, wlp):
    own, padded = _in_proj_layout(d)
    at, start = {}, 0
    for n, wd in own:
        at[n] = (start, start + wd)
        start += wd
    parts = []
    for n, wd in padded:
        parts += [jnp.zeros((wd, g.shape[1]), g.dtype)] if n == 'pad' else _row_pieces(g, *at[n], wl, wlp)
    return jnp.concatenate(parts, axis=0)


def _proj_to_w_in_t(gp, d, wl, wlp):
    own, padded = _in_proj_layout(d)
    pat, start = {}, 0
    for n, wd in padded:
        pat[n] = start
        start += wd
    parts = []
    for k in range(4):
        start = 0
        for n, wd in own:
            s0, s1 = max(start, k * wl), min(start + wd, (k + 1) * wl)
            if s0 < s1:
                parts.append(gp[pat[n] + s0 - start:pat[n] + s1 - start])
            start += wd
        parts.append(jnp.zeros((wlp - wl, gp.shape[1]), gp.dtype))
    return jnp.concatenate(parts, axis=0)


def _silu_grad(z):
    s = jax.nn.sigmoid(z)
    return s + z * s * (1.0 - s)


def kernel(x, c, ctx, c_ctx, w_ada, b_ada, g_pre_mix, g_post_mix, g_pre_mlp, g_post_mlp, w_in, w_decay, b_decay, g_gla, w_gla_o, w_dw, b_dw, g_conv_ln, b_conv_ln, w_conv_o, w_pool_g, s_pool, w_pool_o, b_gate, w_out, w_mlp1, w_mlp2, loss_target, m_c_ctx, m_w_ada, m_b_ada, m_g_pre_mix, m_g_post_mix, m_g_pre_mlp, m_g_post_mlp, m_w_in, m_w_decay, m_b_decay, m_g_gla, m_w_gla_o, m_w_dw, m_b_dw, m_g_conv_ln, m_b_conv_ln, m_w_conv_o, m_w_pool_g, m_s_pool, m_w_pool_o, m_b_gate, m_w_out, m_w_mlp1, m_w_mlp2, v_c_ctx, v_w_ada, v_b_ada, v_g_pre_mix, v_g_post_mix, v_g_pre_mlp, v_g_post_mlp, v_w_in, v_w_decay, v_b_decay, v_g_gla, v_w_gla_o, v_w_dw, v_b_dw, v_g_conv_ln, v_b_conv_ln, v_w_conv_o, v_w_pool_g, v_s_pool, v_w_pool_o, v_b_gate, v_w_out, v_w_mlp1, v_w_mlp2):
    a = dict(locals())
    for n in ('w_in', 'm_w_in', 'v_w_in'):
        a[n] = jnp.swapaxes(a[n], 1, 2)
    big_axis = dict(BIG, w_in=1)
    depth = w_in.shape[0]
    d = x.shape[-1]
    seq, nctx_rows = x.shape[1], ctx.shape[1]
    dm = types.SimpleNamespace(
        D=d, SEQ=seq, CTX=nctx_rows, T=seq + nctx_rows, DK=d // 8, DV=d // 4, GK=d // 2, GC=d // 8,
        tm=_tile(nctx_rows, (256, 128, 64)), TB=_tile(nctx_rows, (256, 128, 64)))
    assert dm.SEQ % dm.tm == 0 and dm.SEQ % GRID_W == 0 and dm.CTX % GLA_CHUNK == 0
    tmw = min(dm.tm, 128)
    chip = 2 * lax.axis_index("x") + lax.axis_index("y")
    core = lax.axis_index("c")
    chip1 = chip.astype(jnp.int32).reshape(1)
    core1 = core.astype(jnp.int32).reshape(1)

    big_names, small_names = list(BIG), list(SMALL_SHARDED)
    nbig = len(big_names)
    kinds = ['col' if big_axis[n] == 2 else 'row' for n in big_names]
    wl = w_in.shape[2]
    wlp = _lane_pad(wl)

    def rows8(t):
        t = t.reshape(t.shape[0], -1, t.shape[-1])
        return jnp.pad(t, ((0, 0), (0, -t.shape[1] % 8), (0, 0)))

    def halves(t):
        return t.reshape(2, t.shape[0] // 2, t.shape[1])

    def layer_src(l, tok=None):
        def one(n):
            t = a[n][l] if tok is None else a[n][l] + tok
            return halves((jnp.pad(t, ((0, wlp - wl), (0, 0))) if n == 'w_in' else t).astype(MM_DTYPE))
        return [one(n) for n in big_names]

    def whole(t):
        return t.reshape(-1, t.shape[-1])

    def start_gather(srcs, knds, after, name):
        plan = _gather_plan(knds, [t.shape[2] for t in srcs])
        lands = [lax.empty(_gathered_shape(t, k), t.dtype) for t, k in zip(srcs, knds)]
        return (plan,) + start_copies(srcs, lands, plan, 4 * len(srcs), after, name)

    late = [big_names.index(n) for n in ('w_gla_o', 'w_conv_o', 'w_pool_o', 'w_out', 'w_mlp1', 'w_mlp2')]
    early = [k for k in range(nbig) if k not in late]
    src0 = layer_src(0)
    kinds_e = [kinds[k] for k in early] + ['col'] * len(small_names)
    age = start_gather([src0[k] for k in early] + [rows8(a[n]) for n in small_names], kinds_e, core1,
                       "gather_layer0_start")
    tok0 = age[-1][0, 0]
    src1 = layer_src(1, tok0)
    pk = lambda pre: _flatten_pad([a[pre + n] + tok0 for n in SMALL], F32)
    small_w, small_m, small_v = pk(''), pk('m_'), pk('v_')
    X = jnp.concatenate([ctx[0] + tok0, x[0] + tok0], axis=0)
    ready = (small_w[0, 0] + small_m[0, 0] + small_v[0, 0] + X[0, 0]
             + sum(t[0, 0, 0].astype(F32) for t in src1)).reshape(1, 1)
    _, g0 = wait_copies(age[1], age[2], age[3], age[4], age[0], ready, "gather_layer0_wait")
    g0 = forward_halves(g0, kinds_e, "gather_layer0_forward")
    ag0 = start_gather([src0[k] for k in late], [kinds[k] for k in late], g0[0], "gather_layer0_late_start")
    ag1 = start_gather(src1, kinds, ag0[-1], "gather_layer1_start")
    ag_token = ag1[-1]
    full = {n: [None, None] for n in big_names}
    for k, t in zip(early, g0):
        full[big_names[k]][0] = whole(t)
    for n, g in zip(small_names, g0[len(early):]):
        shp = a[n].shape
        full[n] = g[:, :math.prod(shp[1:-1])].reshape(shp[:-1] + (4 * shp[-1],))
    for n in SMALL:
        if n not in SMALL_SHARDED:
            full[n] = a[n]

    cvec = jnp.concatenate([c_ctx.reshape(1, d), c.reshape(1, d), jnp.zeros((6, d), F32)], axis=0)
    avec = (cvec * jax.nn.sigmoid(cvec) + ag_token[0, 0]).astype(MM_DTYPE)

    def row(v):
        return v.reshape(1, -1)

    saved = []
    gk, gv = dm.GK, d
    lrblk = (7 * d + d // 2) // LANES
    for l in range(depth):
        if l == 1:
            _, got = wait_copies(ag1[1], ag1[2], ag1[3], ag1[4], ag1[0], X, "gather_layer1_wait")
            got = forward_halves(got, kinds, "gather_layer1_forward")
            for n, t in zip(big_names, got):
                full[n][1] = whole(t)
        s = types.SimpleNamespace()
        s.w_in_p = _w_in_t_to_proj(full['w_in'][l], d, wl, wlp)
        wd = full['w_decay'][l]
        wdp = jnp.zeros((LANES, 2 * gk), F32)
        wdp = wdp.at[:GLA_LR, :gk].set(wd[0]).at[GLA_LR:2 * GLA_LR, gk:].set(wd[1])
        s.wdp = wdp.astype(MM_DTYPE)
        s.wdp_wide = jnp.pad(s.wdp, ((0, d // 2 - LANES), (0, 0)))
        s.bd = full['b_decay'][l].reshape(1, 2 * gk)
        modraw = matmul(avec, full['w_ada'][l], 'nn', F32, f"mod_{l}") + full['b_ada'][l][None, :]
        s.mod = [modraw[0:2, j * d:(j + 1) * d].reshape(2, 1, d) for j in range(6)]
        s.x = X
        (s.h,) = rowwise(pre_fn, [X], s.mod[0:2], [row(g_pre_mix[l])], [(d, MM_DTYPE)], dm, f"pre_{l}")
        s.P = matmul(s.h, s.w_in_p, 'nt', MM_DTYPE, f"in_proj_{l}")
        P = s.P
        s.z = matmul((P, LANES, lrblk), s.wdp, 'nn', F32, f"decay_proj_{l}", tk=LANES)
        la_f, la_b = rowwise(decay_fn, [s.z], [], [s.bd], [(gk, F32), (gk, F32)], dm, f"decay_{l}")
        s.la = jnp.concatenate([la_f, la_b], axis=1)
        s.o_f, s.st_f = gla_fwd(P, s.la, False, dm, f"gla_fwd_f_{l}")
        s.o_b, s.st_b = gla_fwd(P, s.la, True, dm, f"gla_fwd_b_{l}")
        (s.gin,) = rowwise(glaout_fn, [s.o_f, s.o_b, (P, d, 3)], [], [row(g_gla[l])], [(gv, MM_DTYPE)], dm,
                           f"gla_out_{l}")
        if l == 0:
            _, got = wait_copies(ag0[1], ag0[2], ag0[3], ag0[4], ag0[0], s.gin, "gather_layer0_late_wait")
            got = forward_halves(got, [kinds[k] for k in late], "gather_layer0_late_forward")
            for k, t in zip(late, got):
                full[big_names[k]][0] = whole(t)
        s.ya = matmul(s.gin, full['w_gla_o'][l], 'nn', MM_DTYPE, f"gla_o_{l}")
        (s.u,) = rowwise(glu_fn, [(P, d, 6)], [], [], [(d // 2, F32)], dm, f"glu_{l}")
        s.yconv = conv_fwd(s.u, full['w_dw'][l], dm, f"conv_{l}")
        (s.cin,) = rowwise(convpost_fn, [s.yconv], [], [row(b_dw[l]), row(g_conv_ln[l]), row(b_conv_ln[l])],
                           [(d // 2, MM_DTYPE)], dm, f"conv_post_{l}")
        s.yb = matmul(s.cin, full['w_conv_o'][l], 'nn', MM_DTYPE, f"conv_o_{l}")
        s.pm = pool_mix((P, d // 2, 14), False, dm, f"pool_mix_{l}")
        s.pc = group_mm(s.pm, w_pool_g[l], 'nn', F32, f"pool_g_{l}")
        (s.pin,) = rowwise(poolpost_fn, [s.pc], [], [row(s_pool[l])], [(d // 2, MM_DTYPE)], dm, f"pool_post_{l}")
        s.yc = matmul(s.pin, full['w_pool_o'][l], 'nn', MM_DTYPE, f"pool_o_{l}")
        s.bg = [row(full['b_gate'][l][j]) for j in range(3)]
        (s.mixed,) = rowwise(merge_fn, [s.ya, s.yb, s.yc, (P, 3 * d, 0)], [], s.bg, [(d, MM_DTYPE)], dm,
                             f"merge_{l}", tm=tmw)
        s.y = matmul(s.mixed, full['w_out'][l], 'nn', MM_DTYPE, f"out_proj_{l}")
        s.x1, s.h2 = rowwise(mid_fn, [X, s.y], s.mod[2:5], [row(g_post_mix[l]), row(g_pre_mlp[l])],
                             [(d, F32), (d, MM_DTYPE)], dm, f"mid_{l}")
        s.act = matmul(s.h2, full['w_mlp1'][l], 'nn', MM_DTYPE, f"mlp1_{l}", epi=relu2_epi)
        s.y2 = matmul(s.act, full['w_mlp2'][l], 'nn', MM_DTYPE, f"mlp2_{l}")
        (X,) = rowwise(post_fn, [s.x1, s.y2], s.mod[5:6], [row(g_post_mlp[l])], [(d, F32)], dm, f"post_{l}")
        saved.append(s)

    dX, lossv = loss_head(X, loss_target[0], dm, "loss_head")
    loss = lax.psum(lossv[0, 0], ("x", "y", "c"))

    grads = {n: [None] * depth for n in WEIGHTS if n != 'c_ctx' and n not in BIG}
    gbig = {n: [None] * depth for n in BIG}
    rs_token = None

    def start_scatter(idx, layer, after, name):
        gs = [gbig[big_names[k]][layer] for k in idx]
        wd = [t.shape[1] // 4 if kinds[k] == 'col' else t.shape[0] // 4 for t, k in zip(gs, idx)]
        plan = _scatter_plan([big_axis[big_names[k]] - 1 for k in idx], wd)
        lands = [lax.empty((3, t.shape[0], w) if kinds[k] == 'col' else (3, w, t.shape[1]), t.dtype)
                 for t, w, k in zip(gs, wd, idx)]
        return (plan,) + start_copies(gs, lands, plan, 3 * len(gs), after, name)

    g_cctx = jnp.zeros((d,), F32)
    for l in reversed(range(depth)):
        s = saved[l]
        P = s.P
        dmod = [None] * 6
        gpm = row(g_post_mlp[l]) if rs_token is None else row(g_post_mlp[l]) + rs_token[0, 0]
        (dx1, dy2), (dmod[5],), (dg,) = rowwise_vjp(post_fn, [s.x1, s.y2], s.mod[5:6], [gpm], [dX],
                                                     dm, f"post_bwd_{l}", narrow=(1,))
        grads['g_post_mlp'][l] = dg[0]
        du1 = matmul(dy2, full['w_mlp2'][l], 'nt', MM_DTYPE, f"mlp2_dx_{l}", epi=relu2_bwd_epi, extras=[s.act])
        gbig['w_mlp2'][l] = matmul(s.act, dy2, 'tn', MM_DTYPE, f"mlp2_dw_{l}")
        dh2 = matmul(du1, full['w_mlp1'][l], 'nt', MM_DTYPE, f"mlp1_dx_{l}")
        gbig['w_mlp1'][l] = matmul(s.h2, du1, 'tn', MM_DTYPE, f"mlp1_dw_{l}")
        gpx = row(g_post_mix[l])
        (dxa, dy), dmod[2:5], (dg1, dg2) = rowwise_vjp(
            mid_fn, [s.x, s.y], s.mod[2:5], [gpx, row(g_pre_mlp[l])], [dx1, dh2], dm, f"mid_bwd_{l}", narrow=(1,))
        grads['g_post_mix'][l], grads['g_pre_mlp'][l] = dg1[0], dg2[0]
        dmixed = matmul(dy, full['w_out'][l], 'nt', MM_DTYPE, f"out_proj_dx_{l}")
        gbig['w_out'][l] = matmul(s.mixed, dy, 'tn', MM_DTYPE, f"out_proj_dw_{l}")
        (dya, dyb, dyc, dP), _, dbg = rowwise_vjp(merge_fn, [s.ya, s.yb, s.yc, (P, 3 * d, 0)], [], s.bg, [dmixed],
                                                  dm, f"merge_bwd_{l}", tm=tmw, narrow=(0, 1, 2),
                                                  into=(3, None, P.shape))
        grads['b_gate'][l] = jnp.concatenate(dbg, axis=0)
        dgin = matmul(dya, full['w_gla_o'][l], 'nt', MM_DTYPE, f"gla_o_dx_{l}")
        gbig['w_gla_o'][l] = matmul(s.gin, dya, 'tn', MM_DTYPE, f"gla_o_dw_{l}")
        dcin = matmul(dyb, full['w_conv_o'][l], 'nt', MM_DTYPE, f"conv_o_dx_{l}")
        gbig['w_conv_o'][l] = matmul(s.cin, dyb, 'tn', MM_DTYPE, f"conv_o_dw_{l}")
        dpin = matmul(dyc, full['w_pool_o'][l], 'nt', MM_DTYPE, f"pool_o_dx_{l}")
        gbig['w_pool_o'][l] = matmul(s.pin, dyc, 'tn', MM_DTYPE, f"pool_o_dw_{l}")
        sp = row(s_pool[l])
        if l == 0:
            rs0 = start_scatter(late, 0, dpin, "grad_layer0_late_start")
            sp = sp + rs0[-1][0, 0]
        (dpc,), _, (dsp,) = rowwise_vjp(poolpost_fn, [s.pc], [], [sp], [dpin], dm, f"pool_post_bwd_{l}")
        grads['s_pool'][l] = dsp[0]
        grads['w_pool_g'][l] = group_mm(s.pm, w_pool_g[l], 'tn', F32, f"pool_g_dw_{l}", b=dpc)
        dpm = group_mm(dpc, w_pool_g[l], 'nt', F32, f"pool_g_dx_{l}")
        dP = pool_mix(dpm, True, dm, f"pool_mix_bwd_{l}", into=(dP, 14))
        (dyconv,), _, (dbdw, dgln, dbln) = rowwise_vjp(
            convpost_fn, [s.yconv], [], [row(b_dw[l]), row(g_conv_ln[l]), row(b_conv_ln[l])], [dcin], dm,
            f"conv_post_bwd_{l}")
        grads['b_dw'][l], grads['g_conv_ln'][l], grads['b_conv_ln'][l] = dbdw[0], dgln[0], dbln[0]
        du, grads['w_dw'][l] = conv_bwd(s.u, full['w_dw'][l], dyconv, dm, f"conv_bwd_{l}")
        (dP,), _, _ = rowwise_vjp(glu_fn, [(P, d, 6)], [], [], [du], dm, f"glu_bwd_{l}", into=(0, dP, P.shape))
        (do, _, dP), _, (dgg,) = rowwise_vjp(glaout_fn, [s.o_f, s.o_b, (P, d, 3)], [], [row(g_gla[l])], [dgin], dm,
                                             f"gla_out_bwd_{l}", want=[True, False, True], into=(2, dP, P.shape), narrow=(0,))
        grads['g_gla'][l] = dgg[0]
        dqf, dkf, dvf, dlaf = gla_bwd(P, s.la, do, s.st_f, False, dm, f"gla_bwd_f_{l}")
        dP, dlab = gla_bwd(P, s.la, do, s.st_b, True, dm, f"gla_bwd_b_{l}", prev=(dqf, dkf, dvf), into=dP)
        (dz,), _, (dbd,) = rowwise_vjp(decay_fn, [s.z], [], [s.bd], [dlaf, dlab], dm, f"decay_bwd_{l}", narrow=(0,))
        grads['b_decay'][l] = dbd.reshape(2, gk)
        dwdp = matmul((P, LANES, lrblk), dz, 'tn', F32, f"decay_proj_dw_{l}", tm=LANES)
        grads['w_decay'][l] = jnp.stack([dwdp[:GLA_LR, :gk], dwdp[GLA_LR:2 * GLA_LR, gk:]])
        dP = matmul(dz, s.wdp_wide, 'nt', MM_DTYPE, f"decay_proj_dx_{l}", into=(dP, 15))
        dh = matmul(dP, s.w_in_p, 'nn', MM_DTYPE, f"in_proj_dx_{l}")
        gbig['w_in'][l] = _proj_to_w_in_t(matmul(dP, s.h, 'tn', MM_DTYPE, f"in_proj_dw_{l}"), d, wl, wlp)
        (dX,), dmod[0:2], (dg,) = rowwise_vjp(pre_fn, [s.x], s.mod[0:2], [row(g_pre_mix[l])], [dh], dm,
                                               f"pre_bwd_{l}", adds={0: dxa})
        grads['g_pre_mix'][l] = dg[0]
        dmodflat = jnp.concatenate([jnp.concatenate([m_.reshape(2, d) for m_ in dmod], axis=1),
                                    jnp.zeros((6, 6 * d), F32)], axis=0)
        grads['b_ada'][l] = dmodflat[0] + dmodflat[1]
        gbig['w_ada'][l] = matmul(avec, dmodflat, 'tn', MM_DTYPE, f"ada_dw_{l}")
        dav = matmul(dmodflat, full['w_ada'][l], 'nt', F32, f"ada_dx_{l}")
        g_cctx = g_cctx + dav[0] * _silu_grad(c_ctx)
        if l == 1:
            rs1 = start_scatter(list(range(nbig)), 1, dav, "grad_layer1_start")
            rs_token = rs1[-1]

    grad_x = dX[dm.CTX:][None]
    gfull = {n: jnp.stack(v) for n, v in grads.items()}
    gfull['c_ctx'] = g_cctx
    where = jnp.concatenate([chip1, core1])

    def halves_view(t, k):
        return t.reshape(2, t.shape[0] // 2, t.shape[1]) if k == 'col' else t.reshape(4, 2, t.shape[0] // 8, t.shape[1])
    enames = [big_names[k] for k in early]
    ekinds = [kinds[k] for k in early]
    v0 = [halves_view(gbig[n][0], k) for n, k in zip(enames, ekinds)]
    r1 = pair_swap_halves(v0, ekinds, "grad_pair_swap")
    hs = [pair_add(v.reshape((-1,) + v.shape[-2:]), r.reshape((-1,) + r.shape[-2:]), core1, f"grad_pair_add_{n}")
          for n, v, r in zip(enames, v0, r1)]
    hx = [h.reshape(h.shape[1:]) if k == 'col' else h for h, k in zip(hs, ekinds)]
    ex_plan = _exchange_plan(ekinds)
    ex_lands = [lax.empty((3, h.shape[0], h.shape[1] // 4) if k == 'col' else (3,) + h.shape[1:], h.dtype)
                for h, k in zip(hx, ekinds)]
    ex = (ex_plan,) + start_copies(hx, ex_lands, ex_plan, 3 * len(hx), core1, "grad_chip_exchange_start")

    gs0, got0 = wait_copies(rs0[1], rs0[2], rs0[3], rs0[4], rs0[0], ex[-1], "grad_layer0_late_wait")
    gs1, got1 = wait_copies(rs1[1], rs1[2], rs1[3], rs1[4], rs1[0], ex[-1], "grad_layer1_wait")
    sa = [chip_add(g, r, big_axis[big_names[k]] - 1, where, f"grad_layer0_add_{big_names[k]}", slab=False)
          for k, g, r in zip(late, gs0, got0)]
    sa += [chip_add(g, r, big_axis[n] - 1, where, f"grad_layer1_add_{n}", slab=False)
           for n, g, r in zip(big_names, gs1, got1)]
    sb = pair_swap(sa, "grad_late_pair_swap")
    red0 = {big_names[k]: [sa[j], sb[j]] for j, k in enumerate(late)}
    red1 = {n: [sa[len(late) + k], sb[len(late) + k]] for k, n in enumerate(big_names)}

    sflat = _flatten_pad([gfull[n].astype(F32) for n in SMALL], F32)
    sv = sflat.reshape(2, sflat.shape[0] // 2, LANES)
    (sr,) = pair_swap_halves([sv], ['col'], "small_grad_pair_swap")
    sh = pair_add(sv, sr[None], core1, "small_grad_pair_add")[0]
    sq = quad_sum(sh, chip_broadcast(sh, "small_grad_chip_exchange"), core1, "small_grad_chip_sum")
    (ssum,) = pair_join_layers([sq], "small_grad_pair_join")
    ssum = ssum.reshape(-1)

    out_g, out_d, out_m, out_v = {}, {}, {}, {}

    def update_big(n, terms, **kw):
        res = adamw_layers(a[n], a['m_' + n], a['v_' + n], terms, f"adamw_{n}" + ("" if not kw else f"_{kw['layer']}"), **kw)
        out_g[n], out_d[n], out_m[n], out_v[n] = res
        return res
    for k in late:
        update_big(big_names[k], [red0[big_names[k]], red1[big_names[k]]])
    half_done = {n: update_big(n, {1: red1[n]}, layer=1) for n in enames}
    start = 0
    sg = {}
    for n in SMALL:
        cnt = gfull[n].size
        g = ssum[start:start + cnt].reshape(gfull[n].shape)
        start += cnt
        if n in SMALL_SHARDED:
            ax = SMALL_SHARDED[n]
            wdt = a[n].shape[ax]
            g = lax.dynamic_slice_in_dim(g, chip * wdt, wdt, axis=ax)
        sg[n] = g
    gs = _flatten_pad([sg[n] for n in SMALL], F32)
    dl, mn, vn = adamw(small_w, gs, small_m, small_v, "adamw_small")
    done = (dl[0, 0] + sum(out_d[n][1, 0, 0] for n in big_names)).reshape(1, 1)
    hx, r2 = wait_copies(ex[1], ex[2], ex[3], ex[4], ex[0], done, "grad_chip_exchange_wait")
    dl, mn, vn = dl.reshape(-1), mn.reshape(-1), vn.reshape(-1)
    start = 0
    for n in SMALL:
        cnt, shp = a[n].size, a[n].shape
        out_g[n] = sg[n]
        out_d[n], out_m[n], out_v[n] = (t[start:start + cnt].reshape(shp) for t in (dl, mn, vn))
        start += cnt
    fs = [chip_add(h.reshape(-1, h.shape[-1]), r, big_axis[n] - 1, where, f"grad_chip_add_{n}")
          for n, h, r in zip(enames, hx, r2)]
    for n, t in zip(enames, pair_join_layers(fs, "grad_pair_join")):
        update_big(n, {0: [t.reshape(-1, t.shape[-1])]}, layer=0, prev=tuple(half_done[n]))
    for dct in (out_g, out_d, out_m, out_v):
        dct['w_in'] = jnp.swapaxes(dct['w_in'], 1, 2)
    return (loss, grad_x, *[out_g[n] for n in WEIGHTS], *[out_d[n] for n in WEIGHTS],
            *[out_m[n] for n in WEIGHTS], *[out_v[n] for n in WEIGHTS])
```

```python
import functools
import math
import types

import jax
import jax.numpy as jnp
from jax import lax
from jax.experimental import pallas as pl
from jax.experimental.pallas import tpu as pltpu

F32 = jnp.float32
MM_DTYPE = jnp.bfloat16
VMEM_LIMIT_V7X = 56 * 1024 * 1024
LANES = 128
EPS = 1e-6

N_HEADS = 4
GLA_CHUNK = 64
GLA_TAU = 16.0
GLA_LR = 16
GRID_W = 64
POOL_WINDOWS = (2, 4, 8, 16)

ADAM_LR = 0.001
ADAM_B1 = 0.9
ADAM_B2 = 0.999
ADAM_EPS = 1e-08
ADAM_WD = 0.01
ADAM_STEP = 10

NN = (((1,), (0,)), ((), ()))
NT = (((1,), (1,)), ((), ()))
TN = (((0,), (0,)), ((), ()))

WEIGHTS = ['c_ctx', 'w_ada', 'b_ada', 'g_pre_mix', 'g_post_mix', 'g_pre_mlp', 'g_post_mlp', 'w_in', 'w_decay',
           'b_decay', 'g_gla', 'w_gla_o', 'w_dw', 'b_dw', 'g_conv_ln', 'b_conv_ln', 'w_conv_o', 'w_pool_g',
           's_pool', 'w_pool_o', 'b_gate', 'w_out', 'w_mlp1', 'w_mlp2']
BIG = {'w_ada': 2, 'w_in': 2, 'w_gla_o': 1, 'w_conv_o': 2, 'w_pool_o': 2, 'w_out': 1, 'w_mlp1': 2, 'w_mlp2': 1}
SMALL_SHARDED = {'w_decay': 3, 'b_decay': 2, 'w_dw': 2, 'b_gate': 2}
SMALL = [n for n in WEIGHTS if n not in BIG]


def _tile(n, prefs):
    for t in prefs:
        if n % t == 0:
            return t
    return n


def _cparams(sem=None, **kw):
    return pltpu.CompilerParams(dimension_semantics=sem, vmem_limit_bytes=VMEM_LIMIT_V7X, **kw)


def _dot(a, b, dims=NN):
    return lax.dot_general(a.astype(MM_DTYPE), b.astype(MM_DTYPE), dims, preferred_element_type=F32)


def matmul(a, b, mode, out_dtype, name, tm=None, tn=None, tk=None, epi=None, extras=(), into=None):
    a, aw, ablk = a if isinstance(a, tuple) else (a, a.shape[1], 0)
    if mode == 'nn':
        M, K, N = a.shape[0], aw, b.shape[1]
    elif mode == 'nt':
        M, K, N = a.shape[0], aw, b.shape[0]
    else:
        K, M, N = a.shape[0], aw, b.shape[1]
    big = (1088, 1024, 640, 544, 512, 320, 256, 128, 64, 32, 16, 8)
    if mode == 'tn':
        tm = tm or _tile(M, (1024, 512, 256, 128))
        tn = tn or _tile(N, (1024, 512, 256, 128))
        tk = tk or _tile(K, big)
    else:
        tm = tm or _tile(M, big)
        tn = tn or _tile(N, (1024, 512, 256, 128))
        tk = tk or _tile(K, (1024, 512, 256, 128))
    if aw != a.shape[1]:
        assert (mode == 'tn' and tm == aw) or (mode != 'tn' and tk == aw)
    nk = K // tk
    ne = len(extras)
    dims = {'nn': NN, 'nt': NT, 'tn': TN}[mode]

    def body(a_ref, b_ref, *rest):
        e_refs, o_ref = rest[:ne], rest[ne + (into is not None)]

        def finish(acc):
            if epi is not None:
                acc = epi(acc, *[e[...] for e in e_refs])
            o_ref[...] = acc.astype(o_ref.dtype)

        p = _dot(a_ref[...], b_ref[...], dims)
        if nk == 1:
            finish(p)
            return
        acc = rest[-1]
        k = pl.program_id(2)

        @pl.when(k == 0)
        def _():
            acc[...] = p

        @pl.when(k > 0)
        def _():
            acc[...] += p

        @pl.when(k == nk - 1)
        def _():
            finish(acc[...])

    if mode == 'nn':
        a_spec = pl.BlockSpec((tm, tk), lambda i, j, k: (i, k + ablk))
        b_spec = pl.BlockSpec((tk, tn), lambda i, j, k: (k, j))
    elif mode == 'nt':
        a_spec = pl.BlockSpec((tm, tk), lambda i, j, k: (i, k + ablk))
        b_spec = pl.BlockSpec((tn, tk), lambda i, j, k: (j, k))
    else:
        a_spec = pl.BlockSpec((tk, tm), lambda i, j, k: (k, i + ablk))
        b_spec = pl.BlockSpec((tk, tn), lambda i, j, k: (k, j))
    tile = pl.BlockSpec((tm, tn), lambda i, j, k: (i, j))
    if into is None:
        out_spec, out_shape, more, extra, aliases = tile, jax.ShapeDtypeStruct((M, N), out_dtype), [], [], {}
    else:
        buf, oblk = into
        out_spec = pl.BlockSpec((tm, tn), lambda i, j, k: (i, oblk * (N // tn) + j))
        out_shape = jax.ShapeDtypeStruct(buf.shape, buf.dtype)
        more, extra, aliases = [pl.BlockSpec(memory_space=pl.ANY)], [buf], {2 + ne: 0}
    return pl.pallas_call(
        body, name=name, grid=(M // tm, N // tn, nk),
        in_specs=[a_spec, b_spec] + [tile] * ne + more, out_specs=out_spec,
        out_shape=out_shape, input_output_aliases=aliases,
        scratch_shapes=[] if nk == 1 else [pltpu.VMEM((tm, tn), F32)],
        compiler_params=_cparams(("parallel", "parallel", "arbitrary")),
    )(a, b, *extras, *extra)


def group_mm(a, w, mode, out_dtype, name, b=None):
    T = a.shape[0]
    G, gc, _ = w.shape
    col = pl.BlockSpec((T, gc), lambda g: (0, g))
    wsp = pl.BlockSpec((1, gc, gc), lambda g: (g, 0, 0))
    if mode == 'tn':
        def body(a_ref, b_ref, o_ref):
            o_ref[0] = _dot(a_ref[...], b_ref[...], TN).astype(o_ref.dtype)
        return pl.pallas_call(body, name=name, grid=(G,), in_specs=[col, col], out_specs=wsp,
                              out_shape=jax.ShapeDtypeStruct((G, gc, gc), out_dtype),
                              compiler_params=_cparams(("parallel",)))(a, b)
    dims = NN if mode == 'nn' else NT

    def body(a_ref, w_ref, o_ref):
        o_ref[...] = _dot(a_ref[...], w_ref[0], dims).astype(o_ref.dtype)
    return pl.pallas_call(body, name=name, grid=(G,), in_specs=[col, wsp], out_specs=col,
                          out_shape=jax.ShapeDtypeStruct((T, G * gc), out_dtype),
                          compiler_params=_cparams(("parallel",)))(a, w)


def _rowspec(r):
    return r if isinstance(r, tuple) else (r, r.shape[1], 0)


def _row_specs(rows, segs, consts, tm, nctx):
    specs = [pl.BlockSpec((tm, w), lambda i, b=b: (i, b)) for _, w, b in rows]
    specs += [pl.BlockSpec((1,) + s.shape[1:], lambda i, n=s.ndim: (jnp.where(i >= nctx, 1, 0),) + (0,) * (n - 1))
              for s in segs]
    specs += [pl.BlockSpec(c.shape, lambda i, n=c.ndim: (0,) * n) for c in consts]
    return specs


def rowwise(fn, rows, segs, consts, outs, dm, name, tm=None):
    tm = tm or dm.tm
    nctx = dm.CTX // tm
    rows = [_rowspec(r) for r in rows]
    nr, ns, nc = len(rows), len(segs), len(consts)

    def body(*refs):
        rin = [r[...] for r in refs[:nr]]
        sin = [s[0] for s in refs[nr:nr + ns]]
        cin = [c[...] for c in refs[nr + ns:nr + ns + nc]]
        res = fn(*rin, *sin, *cin)
        for o_ref, v in zip(refs[nr + ns + nc:], res):
            o_ref[...] = v.astype(o_ref.dtype)

    res = pl.pallas_call(
        body, name=name, grid=(dm.T // tm,),
        in_specs=_row_specs(rows, segs, consts, tm, nctx),
        out_specs=[pl.BlockSpec((tm, w), lambda i: (i, 0)) for w, _ in outs],
        out_shape=[jax.ShapeDtypeStruct((dm.T, w), dt) for w, dt in outs],
        compiler_params=_cparams(("parallel",)),
    )(*[r[0] for r in rows], *segs, *consts)
    return res


def rowwise_vjp(fn, rows, segs, consts, cots, dm, name, tm=None, want=None, adds=None, narrow=(), into=None):
    tm = tm or dm.tm
    nctx = dm.CTX // tm
    rows = [_rowspec(r) for r in rows]
    cots = [_rowspec(r) for r in cots]
    adds = adds or {}
    nr, ns, nc, nct = len(rows), len(segs), len(consts), len(cots)
    want = want or [True] * nr
    widx = [k for k in range(nr) if want[k]]
    akeys = sorted(adds)

    def body(*refs):
        i = pl.program_id(0)
        rin = [r[...] for r in refs[:nr]]
        sin = [s[0] for s in refs[nr:nr + ns]]
        cin = [c[...] for c in refs[nr + ns:nr + ns + nc]]
        p = nr + ns + nc
        cot_refs = refs[p:p + nct]
        add_refs = dict(zip(akeys, refs[p + nct:p + nct + len(akeys)]))
        p = p + nct + len(akeys) + (1 if (into is not None and into[1] is not None) else 0)
        rg_refs = refs[p:p + len(widx)]
        sg_refs = refs[p + len(widx):p + len(widx) + ns]
        cg_refs = refs[p + len(widx) + ns:]
        res, vjp = jax.vjp(fn, *rin, *sin, *cin)
        g = vjp(tuple(cr[...].astype(o.dtype) for cr, o in zip(cot_refs, res)))
        for o_ref, k in zip(rg_refs, widx):
            v = g[k].astype(F32)
            if k in add_refs:
                v = v + add_refs[k][...]
            o_ref[...] = v.astype(o_ref.dtype)
        first_seg = jnp.logical_or(i == 0, i == nctx)
        for o_ref, v in zip(sg_refs, g[nr:nr + ns]):
            @pl.when(first_seg)
            def _(o_ref=o_ref, v=v):
                o_ref[0] = v.astype(F32)

            @pl.when(jnp.logical_not(first_seg))
            def _(o_ref=o_ref, v=v):
                o_ref[0] += v.astype(F32)
        for o_ref, v in zip(cg_refs, g[nr + ns:]):
            @pl.when(i == 0)
            def _(o_ref=o_ref, v=v):
                o_ref[...] = v.astype(F32)

            @pl.when(i > 0)
            def _(o_ref=o_ref, v=v):
                o_ref[...] += v.astype(F32)

    in_specs = _row_specs(rows, segs, consts, tm, nctx)
    in_specs += [pl.BlockSpec((tm, w), lambda i, b=b: (i, b)) for _, w, b in cots]
    in_specs += [pl.BlockSpec((tm, adds[k].shape[1]), lambda i: (i, 0)) for k in akeys]
    out_specs = [pl.BlockSpec((tm, rows[k][1]), lambda i: (i, 0)) for k in widx]
    out_shape = [jax.ShapeDtypeStruct((dm.T, rows[k][1]), MM_DTYPE if k in narrow else rows[k][0].dtype)
                 for k in widx]
    extra, aliases = [], {}
    if into is not None:
        ik, ibuf, ishape = into
        out_specs[widx.index(ik)] = pl.BlockSpec((tm, rows[ik][1]), lambda i, b=rows[ik][2]: (i, b))
        out_shape[widx.index(ik)] = jax.ShapeDtypeStruct(ishape, MM_DTYPE)
        if ibuf is not None:
            aliases = {len(in_specs): widx.index(ik)}
            in_specs = in_specs + [pl.BlockSpec(memory_space=pl.ANY)]
            extra = [ibuf]
    out_specs += [pl.BlockSpec((1,) + s.shape[1:], lambda i, n=s.ndim: (jnp.where(i >= nctx, 1, 0),) + (0,) * (n - 1))
                  for s in segs]
    out_shape += [jax.ShapeDtypeStruct(s.shape, F32) for s in segs]
    out_specs += [pl.BlockSpec(c.shape, lambda i, n=c.ndim: (0,) * n) for c in consts]
    out_shape += [jax.ShapeDtypeStruct(c.shape, F32) for c in consts]
    res = pl.pallas_call(
        body, name=name, grid=(dm.T // tm,), in_specs=in_specs, out_specs=out_specs, out_shape=out_shape,
        input_output_aliases=aliases, compiler_params=_cparams(("arbitrary",)),
    )(*[r[0] for r in rows], *segs, *consts, *[r[0] for r in cots], *[adds[k] for k in akeys], *extra)
    rg = [None] * nr
    for k, v in zip(widx, res[:len(widx)]):
        rg[k] = v
    return rg, list(res[len(widx):len(widx) + ns]), list(res[len(widx) + ns:])


def _rms(x, g):
    return x * lax.rsqrt(jnp.mean(x * x, axis=-1, keepdims=True) + EPS) * g


def _sigmoid(x):
    return jax.nn.sigmoid(x)


def pre_fn(x, shift, scale, g):
    return ((_rms(x, g) * (1.0 + scale) + shift).astype(MM_DTYPE),)


def mid_fn(x, y, gate, shift, scale, g_post, g_pre):
    x1 = x + gate * _rms(y.astype(F32), g_post)
    return x1, (_rms(x1, g_pre) * (1.0 + scale) + shift).astype(MM_DTYPE)


def post_fn(x1, y2, gate, g):
    return (x1 + gate * _rms(y2.astype(F32), g),)


def relu2_epi(acc):
    r = jnp.maximum(acc, 0.0)
    return r * r


def relu2_bwd_epi(dact, act):
    return dact * (2.0 * jnp.sqrt(act.astype(F32)))


def decay_fn(z, bd):
    zz = z.astype(F32) + bd
    ls = jnp.minimum(zz, 0.0) - jnp.log(1.0 + jnp.exp(jnp.minimum(zz, -zz)))
    la = ls / GLA_TAU
    gk = la.shape[1] // 2
    return la[:, :gk], la[:, gk:]


def glu_fn(ab):
    h = ab.shape[1] // 2
    return (ab[:, :h].astype(F32) * _sigmoid(ab[:, h:].astype(F32)),)


def glaout_fn(o_f, o_b, og, g):
    o = o_f + o_b
    dv = o.shape[1] // N_HEADS
    hs = []
    for h in range(N_HEADS):
        oh = o[:, h * dv:(h + 1) * dv]
        hs.append(oh * lax.rsqrt(jnp.mean(oh * oh, axis=-1, keepdims=True) + EPS) * g[:, h * dv:(h + 1) * dv])
    og = og.astype(F32)
    return ((jnp.concatenate(hs, axis=1) * (og * _sigmoid(og))).astype(MM_DTYPE),)


def convpost_fn(y, b_dw, g, b):
    y = y + b_dw
    mu = jnp.mean(y, axis=-1, keepdims=True)
    xc = y - mu
    yn = xc * lax.rsqrt(jnp.mean(xc * xc, axis=-1, keepdims=True) + EPS) * g + b
    return ((yn * _sigmoid(yn)).astype(MM_DTYPE),)


def poolpost_fn(pc, s):
    return ((pc.astype(F32) * s).astype(MM_DTYPE),)


def merge_fn(ya, yb, yc, mg, bg0, bg1, bg2):
    d = ya.shape[1]
    mg = mg.astype(F32)
    mixed = (_sigmoid(mg[:, :d] + bg0) * ya.astype(F32) + _sigmoid(mg[:, d:2 * d] + bg1) * yb.astype(F32)
             + _sigmoid(mg[:, 2 * d:] + bg2) * yc.astype(F32))
    return (mixed.astype(MM_DTYPE),)


def _split_dot(lmat, x, dims):
    hi = x.astype(MM_DTYPE)
    lo = x - hi.astype(F32)
    return _dot(lmat, hi, dims) + _dot(lmat, lo, dims)


def _gla_block_order(dm, rev):
    nctx, nb = dm.CTX // dm.TB, dm.T // dm.TB

    def blk(i):
        if not rev:
            return i
        return jnp.where(i < nctx, nctx - 1 - i, nb - 1 - (i - nctx))
    return blk, nb


def _gla_tri(rev):
    c = GLA_CHUNK
    t = lax.broadcasted_iota(jnp.int32, (c, c), 0)
    s = lax.broadcasted_iota(jnp.int32, (c, c), 1)
    return (s >= t) if rev else (s <= t)


def _gla_cumsum(la, tri):
    lmat = tri.astype(MM_DTYPE)
    return lmat, _split_dot(lmat, la, NN), jnp.sum(la, axis=0, keepdims=True)


def _gla_chunk_terms(q, k, b, bend, tri, scale):
    eb = jnp.exp(b)
    enb = jnp.exp(-b)
    ee = jnp.exp(bend - b)
    qi = q * scale * eb
    ki = k * enb
    kend = k * ee
    att = jnp.where(tri, _dot(qi, ki, NT), 0.0)
    return eb, enb, ee, qi, ki, kend, att


def gla_fwd(P, la, rev, dm, name):
    c, tb, h_, dk, dv, d = GLA_CHUNK, dm.TB, N_HEADS, dm.DK, dm.DV, dm.D
    cpb = tb // c
    blk, nb = _gla_block_order(dm, rev)
    gk, gv = h_ * dk, h_ * dv
    qb, kb, vb, lb = (5 * d) // gk, (5 * d + d // 2) // gk, (4 * d) // gv, (1 if rev else 0)
    scale = dk ** -0.5
    order = list(range(cpb))[::-1] if rev else list(range(cpb))

    def body(q_ref, k_ref, v_ref, la_ref, o_ref, s_ref, st):
        @pl.when(pl.program_id(0) == 0)
        def _():
            st[...] = jnp.zeros_like(st)
        tri = _gla_tri(rev)
        terms = {}
        for n, ci in enumerate(order):
            r = pl.ds(ci * c, c)
            _, b_all, bend_all = _gla_cumsum(la_ref[r, :], tri)
            for hh in range(h_):
                ck, cv = pl.ds(hh * dk, dk), pl.ds(hh * dv, dv)
                hs = slice(hh * dk, (hh + 1) * dk)
                v = v_ref[r, cv]
                _, _, _, qi, _, kend, att = _gla_chunk_terms(
                    q_ref[r, ck].astype(F32), k_ref[r, ck].astype(F32), b_all[:, hs], bend_all[:, hs], tri, scale)
                terms[n, hh] = (_dot(att, v), qi.astype(MM_DTYPE), jnp.exp(bend_all[:, hs]), _dot(v, kend, TN))
        for n, ci in enumerate(order):
            r = pl.ds(ci * c, c)
            for hh in range(h_):
                intra, qi, gam, dstate = terms[n, hh]
                s_in = st[hh]
                o_ref[r, pl.ds(hh * dv, dv)] = intra + _dot(qi, s_in, NT)
                s_ref[n, hh] = s_in
                st[hh] = gam * s_in + dstate

    return pl.pallas_call(
        body, name=name, grid=(nb,),
        in_specs=[pl.BlockSpec((tb, gk), lambda i: (blk(i), qb)),
                  pl.BlockSpec((tb, gk), lambda i: (blk(i), kb)),
                  pl.BlockSpec((tb, gv), lambda i: (blk(i), vb)),
                  pl.BlockSpec((tb, gk), lambda i: (blk(i), lb))],
        out_specs=[pl.BlockSpec((tb, gv), lambda i: (blk(i), 0)),
                   pl.BlockSpec((cpb, h_, dv, dk), lambda i: (i, 0, 0, 0))],
        out_shape=[jax.ShapeDtypeStruct((dm.T, gv), F32),
                   jax.ShapeDtypeStruct((dm.T // c, h_, dv, dk), F32)],
        scratch_shapes=[pltpu.VMEM((h_, dv, dk), F32)],
        compiler_params=_cparams(("arbitrary",)),
    )(P, P, P, la)


def gla_bwd(P, la, do, states, rev, dm, name, prev=None, into=None):
    c, tb, h_, dk, dv, d = GLA_CHUNK, dm.TB, N_HEADS, dm.DK, dm.DV, dm.D
    cpb = tb // c
    blk, nb = _gla_block_order(dm, rev)
    gk, gv = h_ * dk, h_ * dv
    qb, kb, vb, lb = (5 * d) // gk, (5 * d + d // 2) // gk, (4 * d) // gv, (1 if rev else 0)
    scale = dk ** -0.5
    order = list(range(cpb))[::-1] if rev else list(range(cpb))

    fused = prev is not None

    def body(q_ref, k_ref, v_ref, la_ref, do_ref, s_ref, *rest):
        if fused:
            pq_ref, pk_ref, pv_ref, _, w_ref, dla_ref, dst = rest
        else:
            dq_ref, dk_ref, dv_ref, dla_ref, dst = rest

        def put(kind, r, cols, val):
            if not fused:
                {'q': dq_ref, 'k': dk_ref, 'v': dv_ref}[kind][r, cols] = val
                return
            p_ref, off = {'q': (pq_ref, gv), 'k': (pk_ref, gv + gk), 'v': (pv_ref, 0)}[kind]
            w_ref[r, pl.ds(off + cols.start, cols.size)] = (val + p_ref[r, cols]).astype(w_ref.dtype)

        @pl.when(pl.program_id(0) == 0)
        def _():
            dst[...] = jnp.zeros_like(dst)
        tri = _gla_tri(rev)
        for n in range(cpb - 1, -1, -1):
            r = pl.ds(order[n] * c, c)
            for hh in range(h_):
                ck, cv = pl.ds(hh * dk, dk), pl.ds(hh * dv, dv)
                q = q_ref[r, ck].astype(F32)
                k = k_ref[r, ck].astype(F32)
                v = v_ref[r, cv]
                lmat, b, bend = _gla_cumsum(la_ref[r, ck], tri)
                eb, enb, ee, qi, ki, kend, att = _gla_chunk_terms(q, k, b, bend, tri, scale)
                s_in = s_ref[n, hh]
                ds_out = dst[hh]
                dob = do_ref[r, cv]
                datt = jnp.where(tri, _dot(dob, v, NT), 0.0)
                dqi = _dot(datt, ki) + _dot(dob, s_in)
                dki = _dot(datt, qi, TN)
                put('v', r, cv, _dot(att, dob, TN) + _dot(kend, ds_out, NT))
                dkend = _dot(v, ds_out)
                gam = jnp.exp(bend)
                dgam = jnp.sum(ds_out * s_in, axis=0, keepdims=True)
                dst[hh] = gam * ds_out + _dot(dob, qi, TN)
                put('q', r, ck, dqi * (scale * eb))
                put('k', r, ck, dki * enb + dkend * ee)
                db = dqi * qi - dki * ki - dkend * kend
                dbend = jnp.sum(dkend * kend, axis=0, keepdims=True) + dgam * gam
                dla_ref[r, ck] = _split_dot(lmat, db, TN) + dbend

    def bi(j):
        return blk(nb - 1 - j)

    in_specs = [
        pl.BlockSpec((tb, gk), lambda j: (bi(j), qb)),
        pl.BlockSpec((tb, gk), lambda j: (bi(j), kb)),
        pl.BlockSpec((tb, gv), lambda j: (bi(j), vb)),
        pl.BlockSpec((tb, gk), lambda j: (bi(j), lb)),
        pl.BlockSpec((tb, gv), lambda j: (bi(j), 0)),
        pl.BlockSpec((cpb, h_, dv, dk), lambda j: (nb - 1 - j, 0, 0, 0)),
    ]
    small = pl.BlockSpec((tb, gk), lambda j: (bi(j), 0))
    wide = pl.BlockSpec((tb, gv), lambda j: (bi(j), 0))
    if not fused:
        return pl.pallas_call(
            body, name=name, grid=(nb,), in_specs=in_specs, out_specs=[small, small, wide, small],
            out_shape=[jax.ShapeDtypeStruct((dm.T, gk), F32), jax.ShapeDtypeStruct((dm.T, gk), F32),
                       jax.ShapeDtypeStruct((dm.T, gv), F32), jax.ShapeDtypeStruct((dm.T, gk), F32)],
            scratch_shapes=[pltpu.VMEM((h_, dv, dk), F32)],
            compiler_params=_cparams(("arbitrary",)),
        )(P, P, P, la, do, states)
    return pl.pallas_call(
        body, name=name, grid=(nb,),
        in_specs=in_specs + [small, small, wide, pl.BlockSpec(memory_space=pl.ANY)],
        out_specs=[pl.BlockSpec((tb, 2 * gv), lambda j: (bi(j), vb // 2)), small],
        out_shape=[jax.ShapeDtypeStruct(into.shape, into.dtype), jax.ShapeDtypeStruct((dm.T, gk), F32)],
        input_output_aliases={9: 0},
        scratch_shapes=[pltpu.VMEM((h_, dv, dk), F32)],
        compiler_params=_cparams(("arbitrary",)),
    )(P, P, P, la, do, states, *prev, into)


def _pos(n, period):
    t = lax.broadcasted_iota(jnp.int32, (n, 1), 0)
    if period & (period - 1) == 0:
        return jnp.bitwise_and(t, period - 1)
    return lax.rem(t, period)


def _conv_segments(dm):
    return [(0, dm.CTX, dm.CTX), (dm.CTX, dm.SEQ, GRID_W)]


def conv_fwd(u, w, dm, name):
    kw, cw = w.shape
    segs = _conv_segments(dm)

    def body(u_ref, w_ref, y_ref):
        for r0, n, per in segs:
            useg = u_ref[r0:r0 + n, :]
            p = _pos(n, per)
            acc = jnp.zeros_like(useg)
            for kk in range(kw):
                d = kk - kw // 2
                sh = useg if d == 0 else pltpu.roll(useg, (-d) % n, 0)
                ok = jnp.logical_and(p + d >= 0, p + d < per)
                acc = acc + jnp.where(ok, sh, 0.0) * w_ref[kk:kk + 1, :]
            y_ref[r0:r0 + n, :] = acc

    return pl.pallas_call(
        body, name=name, grid=(cw // LANES,),
        in_specs=[pl.BlockSpec((dm.T, LANES), lambda j: (0, j)), pl.BlockSpec((kw, LANES), lambda j: (0, j))],
        out_specs=pl.BlockSpec((dm.T, LANES), lambda j: (0, j)),
        out_shape=jax.ShapeDtypeStruct((dm.T, cw), F32),
        compiler_params=_cparams(("parallel",)),
    )(u, w)


def conv_bwd(u, w, dy, dm, name):
    kw, cw = w.shape
    segs = _conv_segments(dm)

    def body(u_ref, w_ref, dy_ref, du_ref, dw_ref):
        dws = [jnp.zeros((1, LANES), F32)] * kw
        for r0, n, per in segs:
            useg = u_ref[r0:r0 + n, :]
            dyseg = dy_ref[r0:r0 + n, :]
            p = _pos(n, per)
            acc = jnp.zeros_like(useg)
            for kk in range(kw):
                d = kk - kw // 2
                shu = useg if d == 0 else pltpu.roll(useg, (-d) % n, 0)
                okf = jnp.logical_and(p + d >= 0, p + d < per)
                dws[kk] = dws[kk] + jnp.sum(jnp.where(okf, shu, 0.0) * dyseg, axis=0, keepdims=True)
                shd = dyseg if d == 0 else pltpu.roll(dyseg, d % n, 0)
                okb = jnp.logical_and(p - d >= 0, p - d < per)
                acc = acc + jnp.where(okb, shd, 0.0) * w_ref[kk:kk + 1, :]
            du_ref[r0:r0 + n, :] = acc
        for kk in range(kw):
            dw_ref[kk:kk + 1, :] = dws[kk]

    return pl.pallas_call(
        body, name=name, grid=(cw // LANES,),
        in_specs=[pl.BlockSpec((dm.T, LANES), lambda j: (0, j)), pl.BlockSpec((kw, LANES), lambda j: (0, j)),
                  pl.BlockSpec((dm.T, LANES), lambda j: (0, j))],
        out_specs=[pl.BlockSpec((dm.T, LANES), lambda j: (0, j)), pl.BlockSpec((kw, LANES), lambda j: (0, j))],
        out_shape=[jax.ShapeDtypeStruct((dm.T, cw), F32), jax.ShapeDtypeStruct((kw, cw), F32)],
        compiler_params=_cparams(("parallel",)),
    )(u, w, dy)


def pool_mix(u, transpose, dm, name, into=None):
    u, uw, ublk = _rowspec(u)
    gc = dm.GC
    ng = len(POOL_WINDOWS)
    rows = dm.SEQ // GRID_W
    segs = [(0, dm.CTX, 1, dm.CTX), (dm.CTX, dm.SEQ, GRID_W, rows)]

    def one_group(u_ref, o_ref, win):
        left = win // 2
        right = win - 1 - left
        for r0, n, stride, length in segs:
            useg = u_ref[r0:r0 + n, :].astype(F32)
            t = lax.broadcasted_iota(jnp.int32, (n, 1), 0)
            p = t if stride == 1 else jnp.right_shift(t, stride.bit_length() - 1)
            cnt = (jnp.minimum(p + right + 1, length) - jnp.maximum(p - left, 0)).astype(F32)
            src = useg / cnt if transpose else useg
            acc = jnp.zeros_like(useg)
            for d in range(-left, right + 1):
                dd = -d if transpose else d
                sh = src if d == 0 else pltpu.roll(src, (-dd * stride) % n, 0)
                ok = jnp.logical_and(p + dd >= 0, p + dd < length)
                acc = acc + jnp.where(ok, sh, 0.0)
            o_ref[r0:r0 + n, :] = ((acc - useg) if transpose else (acc / cnt - useg)).astype(o_ref.dtype)

    def body(u_ref, *rest):
        o_ref = rest[-1]
        g = pl.program_id(0)
        for gi, win in enumerate(POOL_WINDOWS):
            @pl.when(g == gi)
            def _(win=win):
                one_group(u_ref, o_ref, win)

    base = ublk * (uw // gc)
    if into is None:
        obase, out_shape, more, extra, aliases = 0, jax.ShapeDtypeStruct((dm.T, ng * gc), F32), [], [], {}
    else:
        buf, oblk = into
        obase, out_shape = oblk * ng, jax.ShapeDtypeStruct(buf.shape, buf.dtype)
        more, extra, aliases = [pl.BlockSpec(memory_space=pl.ANY)], [buf], {1: 0}
    return pl.pallas_call(
        body, name=name, grid=(ng,),
        in_specs=[pl.BlockSpec((dm.T, gc), lambda g: (0, base + g))] + more,
        out_specs=pl.BlockSpec((dm.T, gc), lambda g: (0, obase + g)),
        out_shape=out_shape, input_output_aliases=aliases,
        compiler_params=_cparams(("parallel",)),
    )(u, *extra)


def loss_head(x2, target, dm, name):
    tm, d = dm.tm, dm.D
    nctx = dm.CTX // tm

    def body(x_ref, t_ref, dx_ref, l_ref):
        i = pl.program_id(0)

        @pl.when(i == 0)
        def _():
            l_ref[...] = jnp.zeros_like(l_ref)

        @pl.when(i < nctx)
        def _():
            dx_ref[...] = jnp.zeros_like(dx_ref)

        @pl.when(i >= nctx)
        def _():
            e = x_ref[...] - t_ref[...]
            dx_ref[...] = e / d
            l_ref[...] += jnp.full(l_ref.shape, 0.5 * jnp.sum(jnp.mean(e * e, axis=-1)), F32)

    return pl.pallas_call(
        body, name=name, grid=(dm.T // tm,),
        in_specs=[pl.BlockSpec((tm, d), lambda i: (i, 0)),
                  pl.BlockSpec((tm, d), lambda i: (jnp.maximum(i - nctx, 0), 0))],
        out_specs=[pl.BlockSpec((tm, d), lambda i: (i, 0)), pl.BlockSpec((8, LANES), lambda i: (0, 0))],
        out_shape=[jax.ShapeDtypeStruct((dm.T, d), F32), jax.ShapeDtypeStruct((8, LANES), F32)],
        compiler_params=_cparams(("arbitrary",)),
    )(x2, target)


def adamw(w, g, m, v, name):
    r, c = w.shape
    tr = _tile(r, tuple(t for t in (512, 256, 128, 64, 32, 16, 8) if t * c * 4 <= (1 << 20)) or (8,))

    def body(w_ref, g_ref, m_ref, v_ref, d_ref, mo_ref, vo_ref):
        gg = g_ref[...]
        mm = ADAM_B1 * m_ref[...] + (1.0 - ADAM_B1) * gg
        vv = ADAM_B2 * v_ref[...] + (1.0 - ADAM_B2) * (gg * gg)
        m_hat = mm / (1.0 - ADAM_B1 ** ADAM_STEP)
        v_hat = vv / (1.0 - ADAM_B2 ** ADAM_STEP)
        d_ref[...] = -ADAM_LR * (m_hat / (jnp.sqrt(v_hat) + ADAM_EPS) + ADAM_WD * w_ref[...])
        mo_ref[...] = mm
        vo_ref[...] = vv

    spec = pl.BlockSpec((tr, c), lambda i: (i, 0))
    return pl.pallas_call(
        body, name=name, grid=(r // tr,), in_specs=[spec] * 4, out_specs=[spec] * 3,
        out_shape=[jax.ShapeDtypeStruct((r, c), F32)] * 3,
        compiler_params=_cparams(("parallel",)),
    )(w, g, m, v)


def pair_add(g, r1, cidx, name):
    ng, r_, n_ = r1.shape
    tr = _tile(r_, tuple(t for t in (1024, 512, 256, 128, 64, 32, 16) if t * n_ * 4 <= (2 << 20)))

    def body(s_ref, g_ref, r_ref, o_ref):
        o_ref[...] = (g_ref[...].astype(F32) + r_ref[...].astype(F32)).astype(o_ref.dtype)

    return pl.pallas_call(
        body, name=name,
        grid_spec=pltpu.PrefetchScalarGridSpec(
            num_scalar_prefetch=1, grid=(ng, r_ // tr),
            in_specs=[pl.BlockSpec((None, tr, n_), lambda k, i, s: (2 * k + s[0], i, 0)),
                      pl.BlockSpec((None, tr, n_), lambda k, i, s: (k, i, 0))],
            out_specs=pl.BlockSpec((None, tr, n_), lambda k, i, s: (k, i, 0))),
        out_shape=jax.ShapeDtypeStruct((ng, r_, n_), g.dtype),
        compiler_params=_cparams(("parallel", "parallel")),
    )(cidx, g, r1)


def chip_add(h, r2, axis, where, name, slab=True):
    _, kl, nl = r2.shape
    tr = _tile(kl, tuple(t for t in (1024, 512, 256, 128, 64, 32, 16) if t * nl * 4 <= (1 << 20)))
    nrb = kl // tr

    def body(s_ref, h_ref, r_ref, o_ref):
        acc = h_ref[...].astype(F32)
        for k in range(r2.shape[0]):
            acc = acc + r_ref[k].astype(F32)
        o_ref[...] = acc

    h_map = (lambda i, s: (s[0] * nrb + i, 0)) if axis == 0 else (lambda i, s: (i, s[0]))
    if slab:
        out_spec = pl.BlockSpec((None, tr, nl), lambda i, s: (s[1], i, 0))
        out_shape = jax.ShapeDtypeStruct((2, kl, nl), F32)
    else:
        out_spec = pl.BlockSpec((tr, nl), lambda i, s: (i, 0))
        out_shape = jax.ShapeDtypeStruct((kl, nl), F32)
    return pl.pallas_call(
        body, name=name,
        grid_spec=pltpu.PrefetchScalarGridSpec(
            num_scalar_prefetch=1, grid=(nrb,),
            in_specs=[pl.BlockSpec((tr, nl), h_map),
                      pl.BlockSpec((r2.shape[0], tr, nl), lambda i, s: (0, i, 0))],
            out_specs=out_spec),
        out_shape=out_shape,
        compiler_params=_cparams(("parallel",)),
    )(where, h, r2)


def adamw_layers(w, m, v, terms, name, layer=None, prev=None):
    _, a_, b_ = w.shape
    tr = _tile(a_, tuple(t for t in (512, 256, 128, 64, 32) if t * b_ * 4 <= (1 << 20)))
    by_cols = tr == a_ and a_ * b_ * 4 > (1 << 20)
    blk = (a_, LANES) if by_cols else (tr, b_)
    steps = b_ // LANES if by_cols else a_ // tr
    at = (lambda i: (0, i)) if by_cols else (lambda i: (i, 0))
    layers = (0, 1) if layer is None else (layer,)
    counts = [len(terms[l]) for l in layers]
    nprev = 0 if prev is None else 4

    def update(g, w_ref, m_ref, v_ref, g_ref, d_ref, mo_ref, vo_ref):
        mm = ADAM_B1 * m_ref[...] + (1.0 - ADAM_B1) * g
        vv = ADAM_B2 * v_ref[...] + (1.0 - ADAM_B2) * (g * g)
        m_hat = mm / (1.0 - ADAM_B1 ** ADAM_STEP)
        v_hat = vv / (1.0 - ADAM_B2 ** ADAM_STEP)
        g_ref[...] = g
        d_ref[...] = -ADAM_LR * (m_hat / (jnp.sqrt(v_hat) + ADAM_EPS) + ADAM_WD * w_ref[...])
        mo_ref[...] = mm
        vo_ref[...] = vv

    def total(refs):
        g = refs[0][...]
        for r in refs[1:]:
            g = g + r[...]
        return g

    def body(w_ref, m_ref, v_ref, *rest):
        t_refs, outs = rest[:sum(counts)], rest[-4:]
        if len(layers) == 1:
            update(total(t_refs), w_ref, m_ref, v_ref, *outs)
            return
        which = pl.program_id(0)

        @pl.when(which == 0)
        def _():
            update(total(t_refs[:counts[0]]), w_ref, m_ref, v_ref, *outs)

        @pl.when(which == 1)
        def _():
            update(total(t_refs[counts[0]:]), w_ref, m_ref, v_ref, *outs)

    if len(layers) == 1:
        stacked = pl.BlockSpec((None,) + blk, lambda l, i: (layers[0],) + at(i))
        t_specs = [pl.BlockSpec(blk, lambda l, i: at(i))] * counts[0]
    else:
        stacked = pl.BlockSpec((None,) + blk, lambda l, i: (l,) + at(i))
        t_specs = ([pl.BlockSpec(blk, lambda l, i: at(i * (1 - l)))] * counts[0]
                   + [pl.BlockSpec(blk, lambda l, i: at(i * l))] * counts[1])
    nin = 3 + sum(counts)
    return pl.pallas_call(
        body, name=name, grid=(len(layers), steps),
        in_specs=[stacked] * 3 + t_specs + [pl.BlockSpec(memory_space=pl.ANY)] * nprev,
        out_specs=[stacked] * 4, out_shape=[jax.ShapeDtypeStruct(w.shape, F32)] * 4,
        input_output_aliases={nin + j: j for j in range(nprev)},
        compiler_params=_cparams(("arbitrary", "arbitrary")),
    )(w, m, v, *[t for l in layers for t in terms[l]], *(prev or ()))


MESH = pl.DeviceIdType.MESH
ANY = pl.BlockSpec(memory_space=pl.ANY)
HBM = pl.BlockSpec(memory_space=pltpu.HBM)
SEM = pl.BlockSpec(memory_space=pltpu.SEMAPHORE)
EFFECT = pltpu.SideEffectType.DATAFLOW_SIDE_EFFECTING


def _place():
    return lax.axis_index("x"), lax.axis_index("y"), lax.axis_index("c")


def _peers(x, y):
    return [(1 - x, y), (x, 1 - y), (1 - x, 1 - y)]


def _rcopy(src, dst, ssem, rsem, dev):
    return pltpu.make_async_remote_copy(src_ref=src, dst_ref=dst, send_sem=ssem, recv_sem=rsem,
                                        device_id=dev, device_id_type=MESH)


def _gathered_shape(src, kind):
    h, a_, b_ = src.shape
    return (h, a_, 4 * b_) if kind == 'col' else (4, h, a_, b_)


def _win(ref, kind, ch, width):
    return ref.at[:, :, pl.ds(ch * width, width)] if kind == 'col' else ref.at[ch]


def _rect(ref, kind, half, ch, width):
    return ref.at[half, :, pl.ds(ch * width, width)] if kind == 'col' else ref.at[ch, half]


def _gather_plan(kinds, widths):
    def plan(src, land, x, y, c):
        chip = 2 * x + y
        out = []
        for n in range(len(src)):
            for px, py in _peers(x, y):
                out.append((src[n].at[c], _rect(land[n], kinds[n], c, chip, widths[n]), (px, py, c),
                            _rect(land[n], kinds[n], c, 2 * px + py, widths[n])))
            mine = _win(land[n], kinds[n], chip, widths[n])
            out.append((src[n], mine, (x, y, 1 - c), mine))
        return out
    return plan


def forward_halves(lands, kinds, name):
    nw = len(lands)
    widths = [t.shape[-1] // 4 if k == 'col' else t.shape[-1] for t, k in zip(lands, kinds)]

    def body(*refs):
        o = refs[nw:2 * nw]
        ssem, rsem = refs[2 * nw:]
        x, y, c = _place()
        sib = (x, y, 1 - c)
        pidx = [2 * px + py for px, py in _peers(x, y)]
        cps = [_rcopy(_rect(o[n], kinds[n], c, pidx[j], widths[n]), _rect(o[n], kinds[n], c, pidx[j], widths[n]),
                      ssem.at[3 * n + j], rsem.at[3 * n + j], sib) for n in range(nw) for j in range(3)]
        for cp in cps:
            cp.start()
        for n in range(nw):
            for j in range(3):
                cps[3 * n + j].wait_send()
                _rcopy(_rect(o[n], kinds[n], 1 - c, pidx[j], widths[n]), _rect(o[n], kinds[n], 1 - c, pidx[j], widths[n]),
                       ssem.at[3 * n + j], rsem.at[3 * n + j], sib).wait_recv()

    return pl.pallas_call(
        body, name=name, in_specs=[ANY] * nw, out_specs=[ANY] * nw,
        out_shape=[jax.ShapeDtypeStruct(t.shape, t.dtype) for t in lands],
        input_output_aliases={n: n for n in range(nw)},
        scratch_shapes=[pltpu.SemaphoreType.DMA((3 * nw,)), pltpu.SemaphoreType.DMA((3 * nw,))],
    )(*lands)


def _scatter_plan(axes, widths):
    def plan(src, land, x, y, c):
        out = []
        for n in range(len(src)):
            for k, (px, py) in enumerate(_peers(x, y)):
                ch = 2 * px + py
                view = (src[n].at[:, pl.ds(ch * widths[n], widths[n])] if axes[n] == 1
                        else src[n].at[pl.ds(ch * widths[n], widths[n]), :])
                out.append((view, land[n].at[k], (px, py, c), land[n].at[k]))
        return out
    return plan


def _exchange_plan(kinds):
    def plan(src, land, x, y, c):
        out = []
        for n in range(len(src)):
            w = land[n].shape[2]
            for j, (px, py) in enumerate(_peers(x, y)):
                ch = 2 * px + py
                view = src[n].at[:, pl.ds(ch * w, w)] if kinds[n] == 'col' else src[n].at[ch]
                out.append((view, land[n].at[j], (px, py, c), land[n].at[j]))
        return out
    return plan


def start_copies(srcs, lands, plan, ncopies, after, name):
    ns, nl = len(srcs), len(lands)

    def body(*refs):
        src, land = refs[:ns], refs[ns:ns + nl]
        ssem, rsem = refs[ns + nl + 1], refs[ns + nl + 2]
        token = refs[-1]
        x, y, c = _place()
        for k, (sv, dv, dev, _) in enumerate(plan(src, land, x, y, c)):
            _rcopy(sv, dv, ssem.at[k], rsem.at[k], dev).start()
        token[...] = jnp.zeros_like(token)

    hbm = lambda t: pltpu.HBM(t.shape, t.dtype)
    res = pl.pallas_call(
        body, name=name,
        out_shape=(pltpu.SemaphoreType.DMA((ncopies,)), pltpu.SemaphoreType.DMA((ncopies,)),
                   *[hbm(t) for t in srcs], *[hbm(t) for t in lands], jax.ShapeDtypeStruct((8, LANES), F32)),
        in_specs=[HBM] * (ns + nl) + [ANY],
        out_specs=(SEM, SEM, *[HBM] * (ns + nl), pl.BlockSpec(memory_space=pltpu.VMEM)),
        input_output_aliases={k: 2 + k for k in range(ns + nl)},
        compiler_params=pltpu.CompilerParams(has_side_effects=EFFECT),
    )(*[pltpu.with_memory_space_constraint(t, pltpu.HBM) for t in list(srcs) + list(lands)], after)
    return res[0], res[1], list(res[2:2 + ns]), list(res[2 + ns:2 + ns + nl]), res[-1]


def wait_copies(ssem, rsem, srcs, lands, plan, after, name):
    ns, nl = len(srcs), len(lands)

    def body(*refs):
        src, land = refs[:ns], refs[ns:ns + nl]
        ss, rs = refs[ns + nl], refs[ns + nl + 1]
        x, y, c = _place()
        for k, (sv, dv, dev, mine) in enumerate(plan(src, land, x, y, c)):
            cp = _rcopy(sv, mine, ss.at[k], rs.at[k], dev)
            cp.wait_send()
            cp.wait_recv()

    hbm = lambda t: pltpu.HBM(t.shape, t.dtype)
    res = pl.pallas_call(
        body, name=name,
        out_shape=(*[hbm(t) for t in srcs], *[hbm(t) for t in lands]),
        in_specs=[HBM] * (ns + nl) + [SEM, SEM, ANY], out_specs=tuple([HBM] * (ns + nl)),
        input_output_aliases={k: k for k in range(ns + nl)},
        compiler_params=pltpu.CompilerParams(has_side_effects=EFFECT),
    )(*srcs, *lands, ssem, rsem, after)
    return list(res[:ns]), list(res[ns:])


def pair_swap_halves(gs, kinds, name):
    nw = len(gs)

    def other(ref, kind, half):
        return ref.at[half] if kind == 'col' else ref.at[:, half]

    def body(*refs):
        g, o = refs[:nw], refs[nw:2 * nw]
        ssem, rsem = refs[2 * nw:]
        x, y, c = _place()
        cps = [_rcopy(other(g[n], kinds[n], 1 - c), o[n], ssem.at[n], rsem.at[n], (x, y, 1 - c)) for n in range(nw)]
        for cp in cps:
            cp.start()
        for cp in cps:
            cp.wait()

    return pl.pallas_call(
        body, name=name, in_specs=[ANY] * nw, out_specs=[ANY] * nw,
        out_shape=[jax.ShapeDtypeStruct(g.shape[1:] if k == 'col' else (g.shape[0],) + g.shape[2:], g.dtype)
                   for g, k in zip(gs, kinds)],
        scratch_shapes=[pltpu.SemaphoreType.DMA((nw,)), pltpu.SemaphoreType.DMA((nw,))],
    )(*gs)


def chip_broadcast(h, name):
    def body(h_ref, o_ref, ssem, rsem):
        x, y, c = _place()
        cps = [_rcopy(h_ref, o_ref.at[j], ssem.at[j], rsem.at[j], (px, py, c)) for j, (px, py) in enumerate(_peers(x, y))]
        for cp in cps:
            cp.start()
        for cp in cps:
            cp.wait()

    return pl.pallas_call(
        body, name=name, in_specs=[ANY], out_specs=ANY,
        out_shape=jax.ShapeDtypeStruct((3,) + h.shape, h.dtype),
        scratch_shapes=[pltpu.SemaphoreType.DMA((3,)), pltpu.SemaphoreType.DMA((3,))],
    )(h)


def quad_sum(h, r, cidx, name):
    r_, c_ = h.shape
    tr = _tile(r_, (512, 256, 128, 64, 32, 16, 8))

    def body(s_ref, h_ref, r_ref, o_ref):
        o_ref[...] = (h_ref[...] + r_ref[2]) + (r_ref[0] + r_ref[1])

    return pl.pallas_call(
        body, name=name,
        grid_spec=pltpu.PrefetchScalarGridSpec(
            num_scalar_prefetch=1, grid=(r_ // tr,),
            in_specs=[pl.BlockSpec((tr, c_), lambda i, s: (i, 0)), pl.BlockSpec((3, tr, c_), lambda i, s: (0, i, 0))],
            out_specs=pl.BlockSpec((None, tr, c_), lambda i, s: (s[0], i, 0))),
        out_shape=jax.ShapeDtypeStruct((2, r_, c_), F32),
        compiler_params=_cparams(("parallel",)),
    )(cidx, h, r)


def pair_join_layers(fs, name):
    nw = len(fs)

    def body(*refs):
        o = refs[nw:2 * nw]
        ssem, rsem = refs[2 * nw:]
        x, y, c = _place()
        sib = (x, y, 1 - c)
        cps = [_rcopy(o[n].at[c], o[n].at[c], ssem.at[n], rsem.at[n], sib) for n in range(nw)]
        for cp in cps:
            cp.start()
        for n in range(nw):
            cps[n].wait_send()
            _rcopy(o[n].at[1 - c], o[n].at[1 - c], ssem.at[n], rsem.at[n], sib).wait_recv()

    return pl.pallas_call(
        body, name=name, in_specs=[ANY] * nw, out_specs=[ANY] * nw,
        out_shape=[jax.ShapeDtypeStruct(f.shape, f.dtype) for f in fs],
        input_output_aliases={n: n for n in range(nw)},
        scratch_shapes=[pltpu.SemaphoreType.DMA((nw,)), pltpu.SemaphoreType.DMA((nw,))],
    )(*fs)


def _flatten_pad(parts, dtype):
    flat = jnp.concatenate([p.reshape(-1).astype(dtype) for p in parts])
    q = 512 * LANES
    n = -(-flat.shape[0] // q) * q
    return jnp.pad(flat, (0, n - flat.shape[0])).reshape(n // LANES, LANES)


def _lane_pad(n):
    return -(-n // LANES) * LANES


def _in_proj_layout(d):
    gk, gv, cw, pw = d // 2, d, d // 2, d // 2
    own = [('q', gk), ('k', gk), ('v', gv), ('og', gv), ('lrf', GLA_LR), ('lrb', GLA_LR), ('ga', cw), ('gb', cw),
           ('pu', pw), ('mg', 3 * d)]
    padded = [('mg', 3 * d), ('og', gv), ('v', gv), ('q', gk), ('k', gk), ('ga', cw), ('gb', cw), ('pu', pw),
              ('lrf', GLA_LR), ('lrb', GLA_LR), ('pad', d // 2 - 2 * GLA_LR)]
    return own, padded


def _row_pieces(src, lo, hi, wl, wlp):
    out = []
    for k in range(4):
        s0, s1 = max(lo, k * wl), min(hi, (k + 1) * wl)
        if s0 < s1:
            out.append(src[k * wlp + s0 - k * wl:k * wlp + s1 - k * wl])
    return out


def _proj_runs(d):
    own, padded = _in_proj_layout(d)
    oat, start = {}, 0
    for n, wd in own:
        oat[n] = start
        start += wd
    runs, start = [], 0
    for n, wd in padded:
        if n != 'pad':
            if runs and runs[-1][0] + runs[-1][2] == oat[n] and runs[-1][1] + runs[-1][2] == start:
                runs[-1] = (runs[-1][0], runs[-1][1], runs[-1][2] + wd)
            else:
                runs.append((oat[n], start, wd))
        start += wd
    return runs, start


def _w_in_t_to_proj(g, d, wl, wlp):
    runs, total = _proj_runs(d)
    parts, at = [], 0
    for o0, p0, wd in runs:
        if p0 > at:
            parts.append(jnp.zeros((p0 - at, g.shape[1]), g.dtype))
        parts += _row_pieces(g, o0, o0 + wd, wl, wlp)
        at = p0 + wd
    if total > at:
        parts.append(jnp.zeros((total - at, g.shape[1]), g.dtype))
    return jnp.concatenate(parts, axis=0)


def _proj_to_w_in_t(gp, d, wl, wlp):
    runs, _ = _proj_runs(d)
    runs = sorted(runs)
    parts = []
    for k in range(4):
        for o0, p0, wd in runs:
            s0, s1 = max(o0, k * wl), min(o0 + wd, (k + 1) * wl)
            if s0 < s1:
                parts.append(gp[p0 + s0 - o0:p0 + s1 - o0])
        parts.append(jnp.zeros((wlp - wl, gp.shape[1]), gp.dtype))
    return jnp.concatenate(parts, axis=0)


def _silu_grad(z):
    s = jax.nn.sigmoid(z)
    return s + z * s * (1.0 - s)


def kernel(x, c, ctx, c_ctx, w_ada, b_ada, g_pre_mix, g_post_mix, g_pre_mlp, g_post_mlp, w_in, w_decay, b_decay, g_gla, w_gla_o, w_dw, b_dw, g_conv_ln, b_conv_ln, w_conv_o, w_pool_g, s_pool, w_pool_o, b_gate, w_out, w_mlp1, w_mlp2, loss_target, m_c_ctx, m_w_ada, m_b_ada, m_g_pre_mix, m_g_post_mix, m_g_pre_mlp, m_g_post_mlp, m_w_in, m_w_decay, m_b_decay, m_g_gla, m_w_gla_o, m_w_dw, m_b_dw, m_g_conv_ln, m_b_conv_ln, m_w_conv_o, m_w_pool_g, m_s_pool, m_w_pool_o, m_b_gate, m_w_out, m_w_mlp1, m_w_mlp2, v_c_ctx, v_w_ada, v_b_ada, v_g_pre_mix, v_g_post_mix, v_g_pre_mlp, v_g_post_mlp, v_w_in, v_w_decay, v_b_decay, v_g_gla, v_w_gla_o, v_w_dw, v_b_dw, v_g_conv_ln, v_b_conv_ln, v_w_conv_o, v_w_pool_g, v_s_pool, v_w_pool_o, v_b_gate, v_w_out, v_w_mlp1, v_w_mlp2):
    a = dict(locals())
    for n in ('w_in', 'm_w_in', 'v_w_in'):
        a[n] = jnp.swapaxes(a[n], 1, 2)
    big_axis = dict(BIG, w_in=1)
    depth = w_in.shape[0]
    d = x.shape[-1]
    seq, nctx_rows = x.shape[1], ctx.shape[1]
    dm = types.SimpleNamespace(
        D=d, SEQ=seq, CTX=nctx_rows, T=seq + nctx_rows, DK=d // 8, DV=d // 4, GK=d // 2, GC=d // 8,
        tm=_tile(nctx_rows, (256, 128, 64)), TB=_tile(nctx_rows, (256, 128, 64)))
    assert dm.SEQ % dm.tm == 0 and dm.SEQ % GRID_W == 0 and dm.CTX % GLA_CHUNK == 0
    tmw = min(dm.tm, 128)
    chip = 2 * lax.axis_index("x") + lax.axis_index("y")
    core = lax.axis_index("c")
    chip1 = chip.astype(jnp.int32).reshape(1)
    core1 = core.astype(jnp.int32).reshape(1)

    big_names, small_names = list(BIG), list(SMALL_SHARDED)
    nbig = len(big_names)
    kinds = ['col' if big_axis[n] == 2 else 'row' for n in big_names]
    wl = w_in.shape[2]
    wlp = _lane_pad(wl)

    def rows8(t):
        t = t.reshape(t.shape[0], -1, t.shape[-1])
        return jnp.pad(t, ((0, 0), (0, -t.shape[1] % 8), (0, 0)))

    def halves(t):
        return t.reshape(2, t.shape[0] // 2, t.shape[1])

    def layer_src(l, tok=None):
        def one(n):
            t = a[n][l] if tok is None else a[n][l] + tok
            return halves((jnp.pad(t, ((0, wlp - wl), (0, 0))) if n == 'w_in' else t).astype(MM_DTYPE))
        return [one(n) for n in big_names]

    def whole(t):
        return t.reshape(-1, t.shape[-1])

    def start_gather(srcs, knds, after, name):
        plan = _gather_plan(knds, [t.shape[2] for t in srcs])
        lands = [lax.empty(_gathered_shape(t, k), t.dtype) for t, k in zip(srcs, knds)]
        return (plan,) + start_copies(srcs, lands, plan, 4 * len(srcs), after, name)

    late = [big_names.index(n) for n in ('w_gla_o', 'w_conv_o', 'w_pool_o', 'w_out', 'w_mlp1', 'w_mlp2')]
    early = [k for k in range(nbig) if k not in late]
    src0 = layer_src(0)
    kinds_e = [kinds[k] for k in early] + ['col'] * len(small_names)
    age = start_gather([src0[k] for k in early] + [rows8(a[n]) for n in small_names], kinds_e, core1,
                       "gather_layer0_start")
    tok0 = age[-1][0, 0]
    src1 = layer_src(1, tok0)
    pk = lambda pre: _flatten_pad([a[pre + n] + tok0 for n in SMALL], F32)
    small_w, small_m, small_v = pk(''), pk('m_'), pk('v_')
    X = jnp.concatenate([ctx[0] + tok0, x[0] + tok0], axis=0)
    ready = (small_w[0, 0] + small_m[0, 0] + small_v[0, 0] + X[0, 0]
             + sum(t[0, 0, 0].astype(F32) for t in src1)).reshape(1, 1)
    _, g0 = wait_copies(age[1], age[2], age[3], age[4], age[0], ready, "gather_layer0_wait")
    g0 = forward_halves(g0, kinds_e, "gather_layer0_forward")
    ag0 = start_gather([src0[k] for k in late], [kinds[k] for k in late], g0[0], "gather_layer0_late_start")
    ag1 = start_gather(src1, kinds, ag0[-1], "gather_layer1_start")
    ag_token = ag1[-1]
    full = {n: [None, None] for n in big_names}
    for k, t in zip(early, g0):
        full[big_names[k]][0] = whole(t)
    for n, g in zip(small_names, g0[len(early):]):
        shp = a[n].shape
        full[n] = g[:, :math.prod(shp[1:-1])].reshape(shp[:-1] + (4 * shp[-1],))
    for n in SMALL:
        if n not in SMALL_SHARDED:
            full[n] = a[n]

    cvec = jnp.concatenate([c_ctx.reshape(1, d), c.reshape(1, d), jnp.zeros((6, d), F32)], axis=0)
    avec = (cvec * jax.nn.sigmoid(cvec) + ag_token[0, 0]).astype(MM_DTYPE)

    def row(v):
        return v.reshape(1, -1)

    saved = []
    gk, gv = dm.GK, d
    lrblk = (7 * d + d // 2) // LANES
    for l in range(depth):
        if l == 1:
            _, got = wait_copies(ag1[1], ag1[2], ag1[3], ag1[4], ag1[0], X, "gather_layer1_wait")
            got = forward_halves(got, kinds, "gather_layer1_forward")
            for n, t in zip(big_names, got):
                full[n][1] = whole(t)
        s = types.SimpleNamespace()
        s.w_in_p = _w_in_t_to_proj(full['w_in'][l], d, wl, wlp)
        wd = full['w_decay'][l]
        wdp = jnp.zeros((LANES, 2 * gk), F32)
        wdp = wdp.at[:GLA_LR, :gk].set(wd[0]).at[GLA_LR:2 * GLA_LR, gk:].set(wd[1])
        s.wdp = wdp.astype(MM_DTYPE)
        s.wdp_wide = jnp.pad(s.wdp, ((0, d // 2 - LANES), (0, 0)))
        s.bd = full['b_decay'][l].reshape(1, 2 * gk)
        modraw = matmul(avec, full['w_ada'][l], 'nn', F32, f"mod_{l}") + full['b_ada'][l][None, :]
        s.mod = [modraw[0:2, j * d:(j + 1) * d].reshape(2, 1, d) for j in range(6)]
        s.x = X
        (s.h,) = rowwise(pre_fn, [X], s.mod[0:2], [row(g_pre_mix[l])], [(d, MM_DTYPE)], dm, f"pre_{l}")
        s.P = matmul(s.h, s.w_in_p, 'nt', MM_DTYPE, f"in_proj_{l}")
        P = s.P
        s.z = matmul((P, LANES, lrblk), s.wdp, 'nn', F32, f"decay_proj_{l}", tk=LANES)
        la_f, la_b = rowwise(decay_fn, [s.z], [], [s.bd], [(gk, F32), (gk, F32)], dm, f"decay_{l}")
        s.la = jnp.concatenate([la_f, la_b], axis=1)
        s.o_f, s.st_f = gla_fwd(P, s.la, False, dm, f"gla_fwd_f_{l}")
        s.o_b, s.st_b = gla_fwd(P, s.la, True, dm, f"gla_fwd_b_{l}")
        (s.gin,) = rowwise(glaout_fn, [s.o_f, s.o_b, (P, d, 3)], [], [row(g_gla[l])], [(gv, MM_DTYPE)], dm,
                           f"gla_out_{l}")
        if l == 0:
            _, got = wait_copies(ag0[1], ag0[2], ag0[3], ag0[4], ag0[0], s.gin, "gather_layer0_late_wait")
            got = forward_halves(got, [kinds[k] for k in late], "gather_layer0_late_forward")
            for k, t in zip(late, got):
                full[big_names[k]][0] = whole(t)
        s.ya = matmul(s.gin, full['w_gla_o'][l], 'nn', MM_DTYPE, f"gla_o_{l}")
        (s.u,) = rowwise(glu_fn, [(P, d, 6)], [], [], [(d // 2, F32)], dm, f"glu_{l}")
        s.yconv = conv_fwd(s.u, full['w_dw'][l], dm, f"conv_{l}")
        (s.cin,) = rowwise(convpost_fn, [s.yconv], [], [row(b_dw[l]), row(g_conv_ln[l]), row(b_conv_ln[l])],
                           [(d // 2, MM_DTYPE)], dm, f"conv_post_{l}")
        s.yb = matmul(s.cin, full['w_conv_o'][l], 'nn', MM_DTYPE, f"conv_o_{l}")
        s.pm = pool_mix((P, d // 2, 14), False, dm, f"pool_mix_{l}")
        s.pc = group_mm(s.pm, w_pool_g[l], 'nn', F32, f"pool_g_{l}")
        (s.pin,) = rowwise(poolpost_fn, [s.pc], [], [row(s_pool[l])], [(d // 2, MM_DTYPE)], dm, f"pool_post_{l}")
        s.yc = matmul(s.pin, full['w_pool_o'][l], 'nn', MM_DTYPE, f"pool_o_{l}")
        s.bg = [row(full['b_gate'][l][j]) for j in range(3)]
        (s.mixed,) = rowwise(merge_fn, [s.ya, s.yb, s.yc, (P, 3 * d, 0)], [], s.bg, [(d, MM_DTYPE)], dm,
                             f"merge_{l}", tm=tmw)
        s.y = matmul(s.mixed, full['w_out'][l], 'nn', MM_DTYPE, f"out_proj_{l}")
        s.x1, s.h2 = rowwise(mid_fn, [X, s.y], s.mod[2:5], [row(g_post_mix[l]), row(g_pre_mlp[l])],
                             [(d, F32), (d, MM_DTYPE)], dm, f"mid_{l}")
        s.act = matmul(s.h2, full['w_mlp1'][l], 'nn', MM_DTYPE, f"mlp1_{l}", epi=relu2_epi)
        s.y2 = matmul(s.act, full['w_mlp2'][l], 'nn', MM_DTYPE, f"mlp2_{l}")
        (X,) = rowwise(post_fn, [s.x1, s.y2], s.mod[5:6], [row(g_post_mlp[l])], [(d, F32)], dm, f"post_{l}")
        saved.append(s)

    dX, lossv = loss_head(X, loss_target[0], dm, "loss_head")
    loss = lax.psum(lossv[0, 0], ("x", "y", "c"))

    grads = {n: [None] * depth for n in WEIGHTS if n != 'c_ctx' and n not in BIG}
    gbig = {n: [None] * depth for n in BIG}
    rs_token = None

    def start_scatter(idx, layer, after, name):
        gs = [gbig[big_names[k]][layer] for k in idx]
        wd = [t.shape[1] // 4 if kinds[k] == 'col' else t.shape[0] // 4 for t, k in zip(gs, idx)]
        plan = _scatter_plan([big_axis[big_names[k]] - 1 for k in idx], wd)
        lands = [lax.empty((3, t.shape[0], w) if kinds[k] == 'col' else (3, w, t.shape[1]), t.dtype)
                 for t, w, k in zip(gs, wd, idx)]
        return (plan,) + start_copies(gs, lands, plan, 3 * len(gs), after, name)

    g_cctx = jnp.zeros((d,), F32)
    for l in reversed(range(depth)):
        s = saved[l]
        P = s.P
        dmod = [None] * 6
        gpm = row(g_post_mlp[l]) if rs_token is None else row(g_post_mlp[l]) + rs_token[0, 0]
        (dx1, dy2), (dmod[5],), (dg,) = rowwise_vjp(post_fn, [s.x1, s.y2], s.mod[5:6], [gpm], [dX],
                                                     dm, f"post_bwd_{l}", narrow=(1,))
        grads['g_post_mlp'][l] = dg[0]
        du1 = matmul(dy2, full['w_mlp2'][l], 'nt', MM_DTYPE, f"mlp2_dx_{l}", epi=relu2_bwd_epi, extras=[s.act])
        gbig['w_mlp2'][l] = matmul(s.act, dy2, 'tn', MM_DTYPE, f"mlp2_dw_{l}")
        dh2 = matmul(du1, full['w_mlp1'][l], 'nt', MM_DTYPE, f"mlp1_dx_{l}")
        gbig['w_mlp1'][l] = matmul(s.h2, du1, 'tn', MM_DTYPE, f"mlp1_dw_{l}")
        gpx = row(g_post_mix[l])
        (dxa, dy), dmod[2:5], (dg1, dg2) = rowwise_vjp(
            mid_fn, [s.x, s.y], s.mod[2:5], [gpx, row(g_pre_mlp[l])], [dx1, dh2], dm, f"mid_bwd_{l}", narrow=(1,))
        grads['g_post_mix'][l], grads['g_pre_mlp'][l] = dg1[0], dg2[0]
        dmixed = matmul(dy, full['w_out'][l], 'nt', MM_DTYPE, f"out_proj_dx_{l}")
        gbig['w_out'][l] = matmul(s.mixed, dy, 'tn', MM_DTYPE, f"out_proj_dw_{l}")
        (dya, dyb, dyc, dP), _, dbg = rowwise_vjp(merge_fn, [s.ya, s.yb, s.yc, (P, 3 * d, 0)], [], s.bg, [dmixed],
                                                  dm, f"merge_bwd_{l}", tm=tmw, narrow=(0, 1, 2),
                                                  into=(3, None, P.shape))
        grads['b_gate'][l] = jnp.concatenate(dbg, axis=0)
        dgin = matmul(dya, full['w_gla_o'][l], 'nt', MM_DTYPE, f"gla_o_dx_{l}")
        gbig['w_gla_o'][l] = matmul(s.gin, dya, 'tn', MM_DTYPE, f"gla_o_dw_{l}")
        dcin = matmul(dyb, full['w_conv_o'][l], 'nt', MM_DTYPE, f"conv_o_dx_{l}")
        gbig['w_conv_o'][l] = matmul(s.cin, dyb, 'tn', MM_DTYPE, f"conv_o_dw_{l}")
        dpin = matmul(dyc, full['w_pool_o'][l], 'nt', MM_DTYPE, f"pool_o_dx_{l}")
        gbig['w_pool_o'][l] = matmul(s.pin, dyc, 'tn', MM_DTYPE, f"pool_o_dw_{l}")
        sp = row(s_pool[l])
        if l == 0:
            rs0 = start_scatter(late, 0, dpin, "grad_layer0_late_start")
            sp = sp + rs0[-1][0, 0]
        (dpc,), _, (dsp,) = rowwise_vjp(poolpost_fn, [s.pc], [], [sp], [dpin], dm, f"pool_post_bwd_{l}")
        grads['s_pool'][l] = dsp[0]
        grads['w_pool_g'][l] = group_mm(s.pm, w_pool_g[l], 'tn', F32, f"pool_g_dw_{l}", b=dpc)
        dpm = group_mm(dpc, w_pool_g[l], 'nt', F32, f"pool_g_dx_{l}")
        dP = pool_mix(dpm, True, dm, f"pool_mix_bwd_{l}", into=(dP, 14))
        (dyconv,), _, (dbdw, dgln, dbln) = rowwise_vjp(
            convpost_fn, [s.yconv], [], [row(b_dw[l]), row(g_conv_ln[l]), row(b_conv_ln[l])], [dcin], dm,
            f"conv_post_bwd_{l}")
        grads['b_dw'][l], grads['g_conv_ln'][l], grads['b_conv_ln'][l] = dbdw[0], dgln[0], dbln[0]
        du, grads['w_dw'][l] = conv_bwd(s.u, full['w_dw'][l], dyconv, dm, f"conv_bwd_{l}")
        (dP,), _, _ = rowwise_vjp(glu_fn, [(P, d, 6)], [], [], [du], dm, f"glu_bwd_{l}", into=(0, dP, P.shape))
        (do, _, dP), _, (dgg,) = rowwise_vjp(glaout_fn, [s.o_f, s.o_b, (P, d, 3)], [], [row(g_gla[l])], [dgin], dm,
                                             f"gla_out_bwd_{l}", want=[True, False, True], into=(2, dP, P.shape), narrow=(0,))
        grads['g_gla'][l] = dgg[0]
        dqf, dkf, dvf, dlaf = gla_bwd(P, s.la, do, s.st_f, False, dm, f"gla_bwd_f_{l}")
        dP, dlab = gla_bwd(P, s.la, do, s.st_b, True, dm, f"gla_bwd_b_{l}", prev=(dqf, dkf, dvf), into=dP)
        (dz,), _, (dbd,) = rowwise_vjp(decay_fn, [s.z], [], [s.bd], [dlaf, dlab], dm, f"decay_bwd_{l}", narrow=(0,))
        grads['b_decay'][l] = dbd.reshape(2, gk)
        dwdp = matmul((P, LANES, lrblk), dz, 'tn', F32, f"decay_proj_dw_{l}", tm=LANES)
        grads['w_decay'][l] = jnp.stack([dwdp[:GLA_LR, :gk], dwdp[GLA_LR:2 * GLA_LR, gk:]])
        dP = matmul(dz, s.wdp_wide, 'nt', MM_DTYPE, f"decay_proj_dx_{l}", into=(dP, 15))
        dh = matmul(dP, s.w_in_p, 'nn', MM_DTYPE, f"in_proj_dx_{l}")
        gbig['w_in'][l] = _proj_to_w_in_t(matmul(dP, s.h, 'tn', MM_DTYPE, f"in_proj_dw_{l}"), d, wl, wlp)
        (dX,), dmod[0:2], (dg,) = rowwise_vjp(pre_fn, [s.x], s.mod[0:2], [row(g_pre_mix[l])], [dh], dm,
                                               f"pre_bwd_{l}", adds={0: dxa})
        grads['g_pre_mix'][l] = dg[0]
        dmodflat = jnp.concatenate([jnp.concatenate([m_.reshape(2, d) for m_ in dmod], axis=1),
                                    jnp.zeros((6, 6 * d), F32)], axis=0)
        grads['b_ada'][l] = dmodflat[0] + dmodflat[1]
        gbig['w_ada'][l] = matmul(avec, dmodflat, 'tn', MM_DTYPE, f"ada_dw_{l}")
        dav = matmul(dmodflat, full['w_ada'][l], 'nt', F32, f"ada_dx_{l}")
        g_cctx = g_cctx + dav[0] * _silu_grad(c_ctx)
        if l == 1:
            rs1 = start_scatter(list(range(nbig)), 1, dav, "grad_layer1_start")
            rs_token = rs1[-1]

    grad_x = dX[dm.CTX:][None]
    gfull = {n: jnp.stack(v) for n, v in grads.items()}
    gfull['c_ctx'] = g_cctx
    where = jnp.concatenate([chip1, core1])

    def halves_view(t, k):
        return t.reshape(2, t.shape[0] // 2, t.shape[1]) if k == 'col' else t.reshape(4, 2, t.shape[0] // 8, t.shape[1])
    enames = [big_names[k] for k in early]
    ekinds = [kinds[k] for k in early]
    v0 = [halves_view(gbig[n][0], k) for n, k in zip(enames, ekinds)]
    r1 = pair_swap_halves(v0, ekinds, "grad_pair_swap")
    hs = [pair_add(v.reshape((-1,) + v.shape[-2:]), r.reshape((-1,) + r.shape[-2:]), core1, f"grad_pair_add_{n}")
          for n, v, r in zip(enames, v0, r1)]
    hx = [h.reshape(h.shape[1:]) if k == 'col' else h for h, k in zip(hs, ekinds)]
    ex_plan = _exchange_plan(ekinds)
    ex_lands = [lax.empty((3, h.shape[0], h.shape[1] // 4) if k == 'col' else (3,) + h.shape[1:], h.dtype)
                for h, k in zip(hx, ekinds)]
    ex = (ex_plan,) + start_copies(hx, ex_lands, ex_plan, 3 * len(hx), core1, "grad_chip_exchange_start")

    gs0, got0 = wait_copies(rs0[1], rs0[2], rs0[3], rs0[4], rs0[0], ex[-1], "grad_layer0_late_wait")
    gs1, got1 = wait_copies(rs1[1], rs1[2], rs1[3], rs1[4], rs1[0], ex[-1], "grad_layer1_wait")
    sa = [chip_add(g, r, big_axis[big_names[k]] - 1, where, f"grad_layer0_add_{big_names[k]}", slab=False)
          for k, g, r in zip(late, gs0, got0)]
    sa += [chip_add(g, r, big_axis[n] - 1, where, f"grad_layer1_add_{n}", slab=False)
           for n, g, r in zip(big_names, gs1, got1)]
    def swap_plan(src, land, x, y, c):
        return [(src[n], land[n], (x, y, 1 - c), land[n]) for n in range(len(src))]
    swp = start_copies(sa, [lax.empty(t.shape, t.dtype) for t in sa], swap_plan, len(sa), core1,
                       "grad_late_pair_swap_start")

    sflat = _flatten_pad([gfull[n].astype(F32) for n in SMALL], F32) + swp[-1][0, 0]
    sv = sflat.reshape(2, sflat.shape[0] // 2, LANES)
    (sr,) = pair_swap_halves([sv], ['col'], "small_grad_pair_swap")
    sh = pair_add(sv, sr[None], core1, "small_grad_pair_add")[0]
    sq = quad_sum(sh, chip_broadcast(sh, "small_grad_chip_exchange"), core1, "small_grad_chip_sum")
    (ssum,) = pair_join_layers([sq], "small_grad_pair_join")
    ssum = ssum.reshape(-1)
    start = 0
    sg = {}
    for n in SMALL:
        cnt = gfull[n].size
        g = ssum[start:start + cnt].reshape(gfull[n].shape)
        start += cnt
        if n in SMALL_SHARDED:
            ax = SMALL_SHARDED[n]
            wdt = a[n].shape[ax]
            g = lax.dynamic_slice_in_dim(g, chip * wdt, wdt, axis=ax)
        sg[n] = g
    gs = _flatten_pad([sg[n] for n in SMALL], F32)
    dl, mn, vn = adamw(small_w, gs, small_m, small_v, "adamw_small")

    sa, sb = wait_copies(swp[0], swp[1], swp[2], swp[3], swap_plan, dl, "grad_late_pair_swap_wait")
    red0 = {big_names[k]: [sa[j], sb[j]] for j, k in enumerate(late)}
    red1 = {n: [sa[len(late) + k], sb[len(late) + k]] for k, n in enumerate(big_names)}

    out_g, out_d, out_m, out_v = {}, {}, {}, {}

    def update_big(n, terms, **kw):
        res = adamw_layers(a[n], a['m_' + n], a['v_' + n], terms, f"adamw_{n}" + ("" if not kw else f"_{kw['layer']}"), **kw)
        out_g[n], out_d[n], out_m[n], out_v[n] = res
        return res
    for k in late:
        update_big(big_names[k], [red0[big_names[k]], red1[big_names[k]]])
    half_done = {n: update_big(n, {1: red1[n]}, layer=1) for n in enames}
    done = (dl[0, 0] + sum(out_d[n][1, 0, 0] for n in big_names)).reshape(1, 1)
    hx, r2 = wait_copies(ex[1], ex[2], ex[3], ex[4], ex[0], done, "grad_chip_exchange_wait")
    dl, mn, vn = dl.reshape(-1), mn.reshape(-1), vn.reshape(-1)
    start = 0
    for n in SMALL:
        cnt, shp = a[n].size, a[n].shape
        out_g[n] = sg[n]
        out_d[n], out_m[n], out_v[n] = (t[start:start + cnt].reshape(shp) for t in (dl, mn, vn))
        start += cnt
    fs = [chip_add(h.reshape(-1, h.shape[-1]), r, big_axis[n] - 1, where, f"grad_chip_add_{n}")
          for n, h, r in zip(enames, hx, r2)]
    for n, t in zip(enames, pair_join_layers(fs, "grad_pair_join")):
        update_big(n, {0: [t.reshape(-1, t.shape[-1])]}, layer=0, prev=tuple(half_done[n]))
    for dct in (out_g, out_d, out_m, out_v):
        dct['w_in'] = jnp.swapaxes(dct['w_in'], 1, 2)
    return (loss, grad_x, *[out_g[n] for n in WEIGHTS], *[out_d[n] for n in WEIGHTS],
            *[out_m[n] for n in WEIGHTS], *[out_v[n] for n in WEIGHTS])
```

```python
import functools
import math
import types

import jax
import jax.numpy as jnp
from jax import lax
from jax.experimental import pallas as pl
from jax.experimental.pallas import tpu as pltpu

F32 = jnp.float32
MM_DTYPE = jnp.bfloat16
VMEM_LIMIT_V7X = 56 * 1024 * 1024
LANES = 128
EPS = 1e-6

N_HEADS = 4
GLA_CHUNK = 64
GLA_TAU = 16.0
GLA_LR = 16
GRID_W = 64
POOL_WINDOWS = (2, 4, 8, 16)

ADAM_LR = 0.001
ADAM_B1 = 0.9
ADAM_B2 = 0.999
ADAM_EPS = 1e-08
ADAM_WD = 0.01
ADAM_STEP = 10

NN = (((1,), (0,)), ((), ()))
NT = (((1,), (1,)), ((), ()))
TN = (((0,), (0,)), ((), ()))

WEIGHTS = ['c_ctx', 'w_ada', 'b_ada', 'g_pre_mix', 'g_post_mix', 'g_pre_mlp', 'g_post_mlp', 'w_in', 'w_decay',
           'b_decay', 'g_gla', 'w_gla_o', 'w_dw', 'b_dw', 'g_conv_ln', 'b_conv_ln', 'w_conv_o', 'w_pool_g',
           's_pool', 'w_pool_o', 'b_gate', 'w_out', 'w_mlp1', 'w_mlp2']
BIG = {'w_ada': 2, 'w_in': 2, 'w_gla_o': 1, 'w_conv_o': 2, 'w_pool_o': 2, 'w_out': 1, 'w_mlp1': 2, 'w_mlp2': 1}
SMALL_SHARDED = {'w_decay': 3, 'b_decay': 2, 'w_dw': 2, 'b_gate': 2}
SMALL = [n for n in WEIGHTS if n not in BIG]


def _tile(n, prefs):
    for t in prefs:
        if n % t == 0:
            return t
    return n


def _cparams(sem=None, **kw):
    return pltpu.CompilerParams(dimension_semantics=sem, vmem_limit_bytes=VMEM_LIMIT_V7X, **kw)


def _dot(a, b, dims=NN):
    return lax.dot_general(a.astype(MM_DTYPE), b.astype(MM_DTYPE), dims, preferred_element_type=F32)


def matmul(a, b, mode, out_dtype, name, tm=None, tn=None, tk=None, epi=None, extras=(), into=None):
    a, aw, ablk = a if isinstance(a, tuple) else (a, a.shape[1], 0)
    if mode == 'nn':
        M, K, N = a.shape[0], aw, b.shape[1]
    elif mode == 'nt':
        M, K, N = a.shape[0], aw, b.shape[0]
    else:
        K, M, N = a.shape[0], aw, b.shape[1]
    big = (1088, 1024, 640, 544, 512, 320, 256, 128, 64, 32, 16, 8)
    if mode == 'tn':
        tm = tm or _tile(M, (1024, 512, 256, 128))
        tn = tn or _tile(N, (1024, 512, 256, 128))
        tk = tk or _tile(K, big)
    else:
        tm = tm or _tile(M, big)
        tn = tn or _tile(N, (1024, 512, 256, 128))
        tk = tk or _tile(K, (1024, 512, 256, 128))
    if aw != a.shape[1]:
        assert (mode == 'tn' and tm == aw) or (mode != 'tn' and tk == aw)
    nk = K // tk
    ne = len(extras)
    dims = {'nn': NN, 'nt': NT, 'tn': TN}[mode]

    def body(a_ref, b_ref, *rest):
        e_refs, o_ref = rest[:ne], rest[ne + (into is not None)]

        def finish(acc):
            if epi is not None:
                acc = epi(acc, *[e[...] for e in e_refs])
            o_ref[...] = acc.astype(o_ref.dtype)

        p = _dot(a_ref[...], b_ref[...], dims)
        if nk == 1:
            finish(p)
            return
        acc = rest[-1]
        k = pl.program_id(2)

        @pl.when(k == 0)
        def _():
            acc[...] = p

        @pl.when(k > 0)
        def _():
            acc[...] += p

        @pl.when(k == nk - 1)
        def _():
            finish(acc[...])

    if mode == 'nn':
        a_spec = pl.BlockSpec((tm, tk), lambda i, j, k: (i, k + ablk))
        b_spec = pl.BlockSpec((tk, tn), lambda i, j, k: (k, j))
    elif mode == 'nt':
        a_spec = pl.BlockSpec((tm, tk), lambda i, j, k: (i, k + ablk))
        b_spec = pl.BlockSpec((tn, tk), lambda i, j, k: (j, k))
    else:
        a_spec = pl.BlockSpec((tk, tm), lambda i, j, k: (k, i + ablk))
        b_spec = pl.BlockSpec((tk, tn), lambda i, j, k: (k, j))
    tile = pl.BlockSpec((tm, tn), lambda i, j, k: (i, j))
    if into is None:
        out_spec, out_shape, more, extra, aliases = tile, jax.ShapeDtypeStruct((M, N), out_dtype), [], [], {}
    else:
        buf, oblk = into
        out_spec = pl.BlockSpec((tm, tn), lambda i, j, k: (i, oblk * (N // tn) + j))
        out_shape = jax.ShapeDtypeStruct(buf.shape, buf.dtype)
        more, extra, aliases = [pl.BlockSpec(memory_space=pl.ANY)], [buf], {2 + ne: 0}
    return pl.pallas_call(
        body, name=name, grid=(M // tm, N // tn, nk),
        in_specs=[a_spec, b_spec] + [tile] * ne + more, out_specs=out_spec,
        out_shape=out_shape, input_output_aliases=aliases,
        scratch_shapes=[] if nk == 1 else [pltpu.VMEM((tm, tn), F32)],
        compiler_params=_cparams(("parallel", "parallel", "arbitrary")),
    )(a, b, *extras, *extra)


def group_mm(a, w, mode, out_dtype, name, b=None):
    T = a.shape[0]
    G, gc, _ = w.shape
    col = pl.BlockSpec((T, gc), lambda g: (0, g))
    wsp = pl.BlockSpec((1, gc, gc), lambda g: (g, 0, 0))
    if mode == 'tn':
        def body(a_ref, b_ref, o_ref):
            o_ref[0] = _dot(a_ref[...], b_ref[...], TN).astype(o_ref.dtype)
        return pl.pallas_call(body, name=name, grid=(G,), in_specs=[col, col], out_specs=wsp,
                              out_shape=jax.ShapeDtypeStruct((G, gc, gc), out_dtype),
                              compiler_params=_cparams(("parallel",)))(a, b)
    dims = NN if mode == 'nn' else NT

    def body(a_ref, w_ref, o_ref):
        o_ref[...] = _dot(a_ref[...], w_ref[0], dims).astype(o_ref.dtype)
    return pl.pallas_call(body, name=name, grid=(G,), in_specs=[col, wsp], out_specs=col,
                          out_shape=jax.ShapeDtypeStruct((T, G * gc), out_dtype),
                          compiler_params=_cparams(("parallel",)))(a, w)


def _rowspec(r):
    return r if isinstance(r, tuple) else (r, r.shape[1], 0)


def _row_specs(rows, segs, consts, tm, nctx):
    specs = [pl.BlockSpec((tm, w), lambda i, b=b: (i, b)) for _, w, b in rows]
    specs += [pl.BlockSpec((1,) + s.shape[1:], lambda i, n=s.ndim: (jnp.where(i >= nctx, 1, 0),) + (0,) * (n - 1))
              for s in segs]
    specs += [pl.BlockSpec(c.shape, lambda i, n=c.ndim: (0,) * n) for c in consts]
    return specs


def rowwise(fn, rows, segs, consts, outs, dm, name, tm=None):
    tm = tm or dm.tm
    nctx = dm.CTX // tm
    rows = [_rowspec(r) for r in rows]
    nr, ns, nc = len(rows), len(segs), len(consts)

    def body(*refs):
        rin = [r[...] for r in refs[:nr]]
        sin = [s[0] for s in refs[nr:nr + ns]]
        cin = [c[...] for c in refs[nr + ns:nr + ns + nc]]
        res = fn(*rin, *sin, *cin)
        for o_ref, v in zip(refs[nr + ns + nc:], res):
            o_ref[...] = v.astype(o_ref.dtype)

    res = pl.pallas_call(
        body, name=name, grid=(dm.T // tm,),
        in_specs=_row_specs(rows, segs, consts, tm, nctx),
        out_specs=[pl.BlockSpec((tm, w), lambda i: (i, 0)) for w, _ in outs],
        out_shape=[jax.ShapeDtypeStruct((dm.T, w), dt) for w, dt in outs],
        compiler_params=_cparams(("parallel",)),
    )(*[r[0] for r in rows], *segs, *consts)
    return res


def rowwise_vjp(fn, rows, segs, consts, cots, dm, name, tm=None, want=None, adds=None, narrow=(), into=None):
    tm = tm or dm.tm
    nctx = dm.CTX // tm
    rows = [_rowspec(r) for r in rows]
    cots = [_rowspec(r) for r in cots]
    adds = adds or {}
    nr, ns, nc, nct = len(rows), len(segs), len(consts), len(cots)
    want = want or [True] * nr
    widx = [k for k in range(nr) if want[k]]
    akeys = sorted(adds)

    def body(*refs):
        i = pl.program_id(0)
        rin = [r[...] for r in refs[:nr]]
        sin = [s[0] for s in refs[nr:nr + ns]]
        cin = [c[...] for c in refs[nr + ns:nr + ns + nc]]
        p = nr + ns + nc
        cot_refs = refs[p:p + nct]
        add_refs = dict(zip(akeys, refs[p + nct:p + nct + len(akeys)]))
        p = p + nct + len(akeys) + (1 if (into is not None and into[1] is not None) else 0)
        rg_refs = refs[p:p + len(widx)]
        sg_refs = refs[p + len(widx):p + len(widx) + ns]
        cg_refs = refs[p + len(widx) + ns:]
        res, vjp = jax.vjp(fn, *rin, *sin, *cin)
        g = vjp(tuple(cr[...].astype(o.dtype) for cr, o in zip(cot_refs, res)))
        for o_ref, k in zip(rg_refs, widx):
            v = g[k].astype(F32)
            if k in add_refs:
                v = v + add_refs[k][...]
            o_ref[...] = v.astype(o_ref.dtype)
        first_seg = jnp.logical_or(i == 0, i == nctx)
        for o_ref, v in zip(sg_refs, g[nr:nr + ns]):
            @pl.when(first_seg)
            def _(o_ref=o_ref, v=v):
                o_ref[0] = v.astype(F32)

            @pl.when(jnp.logical_not(first_seg))
            def _(o_ref=o_ref, v=v):
                o_ref[0] += v.astype(F32)
        for o_ref, v in zip(cg_refs, g[nr + ns:]):
            @pl.when(i == 0)
            def _(o_ref=o_ref, v=v):
                o_ref[...] = v.astype(F32)

            @pl.when(i > 0)
            def _(o_ref=o_ref, v=v):
                o_ref[...] += v.astype(F32)

    in_specs = _row_specs(rows, segs, consts, tm, nctx)
    in_specs += [pl.BlockSpec((tm, w), lambda i, b=b: (i, b)) for _, w, b in cots]
    in_specs += [pl.BlockSpec((tm, adds[k].shape[1]), lambda i: (i, 0)) for k in akeys]
    out_specs = [pl.BlockSpec((tm, rows[k][1]), lambda i: (i, 0)) for k in widx]
    out_shape = [jax.ShapeDtypeStruct((dm.T, rows[k][1]), MM_DTYPE if k in narrow else rows[k][0].dtype)
                 for k in widx]
    extra, aliases = [], {}
    if into is not None:
        ik, ibuf, ishape = into
        out_specs[widx.index(ik)] = pl.BlockSpec((tm, rows[ik][1]), lambda i, b=rows[ik][2]: (i, b))
        out_shape[widx.index(ik)] = jax.ShapeDtypeStruct(ishape, MM_DTYPE)
        if ibuf is not None:
            aliases = {len(in_specs): widx.index(ik)}
            in_specs = in_specs + [pl.BlockSpec(memory_space=pl.ANY)]
            extra = [ibuf]
    out_specs += [pl.BlockSpec((1,) + s.shape[1:], lambda i, n=s.ndim: (jnp.where(i >= nctx, 1, 0),) + (0,) * (n - 1))
                  for s in segs]
    out_shape += [jax.ShapeDtypeStruct(s.shape, F32) for s in segs]
    out_specs += [pl.BlockSpec(c.shape, lambda i, n=c.ndim: (0,) * n) for c in consts]
    out_shape += [jax.ShapeDtypeStruct(c.shape, F32) for c in consts]
    res = pl.pallas_call(
        body, name=name, grid=(dm.T // tm,), in_specs=in_specs, out_specs=out_specs, out_shape=out_shape,
        input_output_aliases=aliases, compiler_params=_cparams(("arbitrary",)),
    )(*[r[0] for r in rows], *segs, *consts, *[r[0] for r in cots], *[adds[k] for k in akeys], *extra)
    rg = [None] * nr
    for k, v in zip(widx, res[:len(widx)]):
        rg[k] = v
    return rg, list(res[len(widx):len(widx) + ns]), list(res[len(widx) + ns:])


def _rms(x, g):
    return x * lax.rsqrt(jnp.mean(x * x, axis=-1, keepdims=True) + EPS) * g


def _sigmoid(x):
    return jax.nn.sigmoid(x)


def pre_fn(x, shift, scale, g):
    return ((_rms(x, g) * (1.0 + scale) + shift).astype(MM_DTYPE),)


def mid_fn(x, y, gate, shift, scale, g_post, g_pre):
    x1 = x + gate * _rms(y.astype(F32), g_post)
    return x1, (_rms(x1, g_pre) * (1.0 + scale) + shift).astype(MM_DTYPE)


def post_fn(x1, y2, gate, g):
    return (x1 + gate * _rms(y2.astype(F32), g),)


def relu2_epi(acc):
    r = jnp.maximum(acc, 0.0)
    return r * r


def relu2_bwd_epi(dact, act):
    return dact * (2.0 * jnp.sqrt(act.astype(F32)))


def decay_fn(z, bd):
    zz = z.astype(F32) + bd
    ls = jnp.minimum(zz, 0.0) - jnp.log(1.0 + jnp.exp(jnp.minimum(zz, -zz)))
    la = ls / GLA_TAU
    gk = la.shape[1] // 2
    return la[:, :gk], la[:, gk:]


def glu_fn(ab):
    h = ab.shape[1] // 2
    return (ab[:, :h].astype(F32) * _sigmoid(ab[:, h:].astype(F32)),)


def glaout_fn(o_f, o_b, og, g):
    o = o_f + o_b
    dv = o.shape[1] // N_HEADS
    hs = []
    for h in range(N_HEADS):
        oh = o[:, h * dv:(h + 1) * dv]
        hs.append(oh * lax.rsqrt(jnp.mean(oh * oh, axis=-1, keepdims=True) + EPS) * g[:, h * dv:(h + 1) * dv])
    og = og.astype(F32)
    return ((jnp.concatenate(hs, axis=1) * (og * _sigmoid(og))).astype(MM_DTYPE),)


def convpost_fn(y, b_dw, g, b):
    y = y + b_dw
    mu = jnp.mean(y, axis=-1, keepdims=True)
    xc = y - mu
    yn = xc * lax.rsqrt(jnp.mean(xc * xc, axis=-1, keepdims=True) + EPS) * g + b
    return ((yn * _sigmoid(yn)).astype(MM_DTYPE),)


def poolpost_fn(pc, s):
    return ((pc.astype(F32) * s).astype(MM_DTYPE),)


def merge_fn(ya, yb, yc, mg, bg0, bg1, bg2):
    d = ya.shape[1]
    mg = mg.astype(F32)
    mixed = (_sigmoid(mg[:, :d] + bg0) * ya.astype(F32) + _sigmoid(mg[:, d:2 * d] + bg1) * yb.astype(F32)
             + _sigmoid(mg[:, 2 * d:] + bg2) * yc.astype(F32))
    return (mixed.astype(MM_DTYPE),)


def _split_dot(lmat, x, dims):
    hi = x.astype(MM_DTYPE)
    lo = x - hi.astype(F32)
    return _dot(lmat, hi, dims) + _dot(lmat, lo, dims)


def _gla_block_order(dm, rev):
    nctx, nb = dm.CTX // dm.TB, dm.T // dm.TB

    def blk(i):
        if not rev:
            return i
        return jnp.where(i < nctx, nctx - 1 - i, nb - 1 - (i - nctx))
    return blk, nb


def _gla_tri(rev):
    c = GLA_CHUNK
    t = lax.broadcasted_iota(jnp.int32, (c, c), 0)
    s = lax.broadcasted_iota(jnp.int32, (c, c), 1)
    return (s >= t) if rev else (s <= t)


def _gla_cumsum(la, tri):
    lmat = tri.astype(MM_DTYPE)
    return lmat, _split_dot(lmat, la, NN), jnp.sum(la, axis=0, keepdims=True)


def _gla_chunk_terms(q, k, b, bend, tri, scale):
    eb = jnp.exp(b)
    enb = jnp.exp(-b)
    ee = jnp.exp(bend - b)
    qi = q * scale * eb
    ki = k * enb
    kend = k * ee
    att = jnp.where(tri, _dot(qi, ki, NT), 0.0)
    return eb, enb, ee, qi, ki, kend, att


def gla_fwd(P, la, rev, dm, name):
    c, tb, h_, dk, dv, d = GLA_CHUNK, dm.TB, N_HEADS, dm.DK, dm.DV, dm.D
    cpb = tb // c
    blk, nb = _gla_block_order(dm, rev)
    gk, gv = h_ * dk, h_ * dv
    qb, kb, vb, lb = (5 * d) // gk, (5 * d + d // 2) // gk, (4 * d) // gv, (1 if rev else 0)
    scale = dk ** -0.5
    order = list(range(cpb))[::-1] if rev else list(range(cpb))

    def body(q_ref, k_ref, v_ref, la_ref, o_ref, s_ref, st):
        @pl.when(pl.program_id(0) == 0)
        def _():
            st[...] = jnp.zeros_like(st)
        tri = _gla_tri(rev)
        terms = {}
        for n, ci in enumerate(order):
            r = pl.ds(ci * c, c)
            _, b_all, bend_all = _gla_cumsum(la_ref[r, :], tri)
            for hh in range(h_):
                ck, cv = pl.ds(hh * dk, dk), pl.ds(hh * dv, dv)
                hs = slice(hh * dk, (hh + 1) * dk)
                v = v_ref[r, cv]
                _, _, _, qi, _, kend, att = _gla_chunk_terms(
                    q_ref[r, ck].astype(F32), k_ref[r, ck].astype(F32), b_all[:, hs], bend_all[:, hs], tri, scale)
                terms[n, hh] = (_dot(att, v), qi.astype(MM_DTYPE), jnp.exp(bend_all[:, hs]), _dot(v, kend, TN))
        for n, ci in enumerate(order):
            r = pl.ds(ci * c, c)
            for hh in range(h_):
                intra, qi, gam, dstate = terms[n, hh]
                s_in = st[hh]
                o_ref[r, pl.ds(hh * dv, dv)] = intra + _dot(qi, s_in, NT)
                s_ref[n, hh] = s_in
                st[hh] = gam * s_in + dstate

    return pl.pallas_call(
        body, name=name, grid=(nb,),
        in_specs=[pl.BlockSpec((tb, gk), lambda i: (blk(i), qb)),
                  pl.BlockSpec((tb, gk), lambda i: (blk(i), kb)),
                  pl.BlockSpec((tb, gv), lambda i: (blk(i), vb)),
                  pl.BlockSpec((tb, gk), lambda i: (blk(i), lb))],
        out_specs=[pl.BlockSpec((tb, gv), lambda i: (blk(i), 0)),
                   pl.BlockSpec((cpb, h_, dv, dk), lambda i: (i, 0, 0, 0))],
        out_shape=[jax.ShapeDtypeStruct((dm.T, gv), F32),
                   jax.ShapeDtypeStruct((dm.T // c, h_, dv, dk), F32)],
        scratch_shapes=[pltpu.VMEM((h_, dv, dk), F32)],
        compiler_params=_cparams(("arbitrary",)),
    )(P, P, P, la)


def gla_bwd(P, la, do, states, rev, dm, name, prev=None, into=None):
    c, tb, h_, dk, dv, d = GLA_CHUNK, dm.TB, N_HEADS, dm.DK, dm.DV, dm.D
    cpb = tb // c
    blk, nb = _gla_block_order(dm, rev)
    gk, gv = h_ * dk, h_ * dv
    qb, kb, vb, lb = (5 * d) // gk, (5 * d + d // 2) // gk, (4 * d) // gv, (1 if rev else 0)
    scale = dk ** -0.5
    order = list(range(cpb))[::-1] if rev else list(range(cpb))

    fused = prev is not None

    def body(q_ref, k_ref, v_ref, la_ref, do_ref, s_ref, *rest):
        if fused:
            pq_ref, pk_ref, pv_ref, _, w_ref, dla_ref, dst = rest
        else:
            dq_ref, dk_ref, dv_ref, dla_ref, dst = rest

        def put(kind, r, cols, val):
            if not fused:
                {'q': dq_ref, 'k': dk_ref, 'v': dv_ref}[kind][r, cols] = val
                return
            p_ref, off = {'q': (pq_ref, gv), 'k': (pk_ref, gv + gk), 'v': (pv_ref, 0)}[kind]
            w_ref[r, pl.ds(off + cols.start, cols.size)] = (val + p_ref[r, cols]).astype(w_ref.dtype)

        @pl.when(pl.program_id(0) == 0)
        def _():
            dst[...] = jnp.zeros_like(dst)
        tri = _gla_tri(rev)
        for n in range(cpb - 1, -1, -1):
            r = pl.ds(order[n] * c, c)
            for hh in range(h_):
                ck, cv = pl.ds(hh * dk, dk), pl.ds(hh * dv, dv)
                q = q_ref[r, ck].astype(F32)
                k = k_ref[r, ck].astype(F32)
                v = v_ref[r, cv]
                lmat, b, bend = _gla_cumsum(la_ref[r, ck], tri)
                eb, enb, ee, qi, ki, kend, att = _gla_chunk_terms(q, k, b, bend, tri, scale)
                s_in = s_ref[n, hh]
                ds_out = dst[hh]
                dob = do_ref[r, cv]
                datt = jnp.where(tri, _dot(dob, v, NT), 0.0)
                dqi = _dot(datt, ki) + _dot(dob, s_in)
                dki = _dot(datt, qi, TN)
                put('v', r, cv, _dot(att, dob, TN) + _dot(kend, ds_out, NT))
                dkend = _dot(v, ds_out)
                gam = jnp.exp(bend)
                dgam = jnp.sum(ds_out * s_in, axis=0, keepdims=True)
                dst[hh] = gam * ds_out + _dot(dob, qi, TN)
                put('q', r, ck, dqi * (scale * eb))
                put('k', r, ck, dki * enb + dkend * ee)
                db = dqi * qi - dki * ki - dkend * kend
                dbend = jnp.sum(dkend * kend, axis=0, keepdims=True) + dgam * gam
                dla_ref[r, ck] = _split_dot(lmat, db, TN) + dbend

    def bi(j):
        return blk(nb - 1 - j)

    in_specs = [
        pl.BlockSpec((tb, gk), lambda j: (bi(j), qb)),
        pl.BlockSpec((tb, gk), lambda j: (bi(j), kb)),
        pl.BlockSpec((tb, gv), lambda j: (bi(j), vb)),
        pl.BlockSpec((tb, gk), lambda j: (bi(j), lb)),
        pl.BlockSpec((tb, gv), lambda j: (bi(j), 0)),
        pl.BlockSpec((cpb, h_, dv, dk), lambda j: (nb - 1 - j, 0, 0, 0)),
    ]
    small = pl.BlockSpec((tb, gk), lambda j: (bi(j), 0))
    wide = pl.BlockSpec((tb, gv), lambda j: (bi(j), 0))
    if not fused:
        return pl.pallas_call(
            body, name=name, grid=(nb,), in_specs=in_specs, out_specs=[small, small, wide, small],
            out_shape=[jax.ShapeDtypeStruct((dm.T, gk), F32), jax.ShapeDtypeStruct((dm.T, gk), F32),
                       jax.ShapeDtypeStruct((dm.T, gv), F32), jax.ShapeDtypeStruct((dm.T, gk), F32)],
            scratch_shapes=[pltpu.VMEM((h_, dv, dk), F32)],
            compiler_params=_cparams(("arbitrary",)),
        )(P, P, P, la, do, states)
    return pl.pallas_call(
        body, name=name, grid=(nb,),
        in_specs=in_specs + [small, small, wide, pl.BlockSpec(memory_space=pl.ANY)],
        out_specs=[pl.BlockSpec((tb, 2 * gv), lambda j: (bi(j), vb // 2)), small],
        out_shape=[jax.ShapeDtypeStruct(into.shape, into.dtype), jax.ShapeDtypeStruct((dm.T, gk), F32)],
        input_output_aliases={9: 0},
        scratch_shapes=[pltpu.VMEM((h_, dv, dk), F32)],
        compiler_params=_cparams(("arbitrary",)),
    )(P, P, P, la, do, states, *prev, into)


def _pos(n, period):
    t = lax.broadcasted_iota(jnp.int32, (n, 1), 0)
    if period & (period - 1) == 0:
        return jnp.bitwise_and(t, period - 1)
    return lax.rem(t, period)


def _conv_segments(dm):
    return [(0, dm.CTX, dm.CTX), (dm.CTX, dm.SEQ, GRID_W)]


def conv_fwd(u, w, dm, name):
    kw, cw = w.shape
    segs = _conv_segments(dm)

    def body(u_ref, w_ref, y_ref):
        for r0, n, per in segs:
            useg = u_ref[r0:r0 + n, :]
            p = _pos(n, per)
            acc = jnp.zeros_like(useg)
            for kk in range(kw):
                d = kk - kw // 2
                sh = useg if d == 0 else pltpu.roll(useg, (-d) % n, 0)
                ok = jnp.logical_and(p + d >= 0, p + d < per)
                acc = acc + jnp.where(ok, sh, 0.0) * w_ref[kk:kk + 1, :]
            y_ref[r0:r0 + n, :] = acc

    return pl.pallas_call(
        body, name=name, grid=(cw // LANES,),
        in_specs=[pl.BlockSpec((dm.T, LANES), lambda j: (0, j)), pl.BlockSpec((kw, LANES), lambda j: (0, j))],
        out_specs=pl.BlockSpec((dm.T, LANES), lambda j: (0, j)),
        out_shape=jax.ShapeDtypeStruct((dm.T, cw), F32),
        compiler_params=_cparams(("parallel",)),
    )(u, w)


def conv_bwd(u, w, dy, dm, name):
    kw, cw = w.shape
    segs = _conv_segments(dm)

    def body(u_ref, w_ref, dy_ref, du_ref, dw_ref):
        dws = [jnp.zeros((1, LANES), F32)] * kw
        for r0, n, per in segs:
            useg = u_ref[r0:r0 + n, :]
            dyseg = dy_ref[r0:r0 + n, :]
            p = _pos(n, per)
            acc = jnp.zeros_like(useg)
            for kk in range(kw):
                d = kk - kw // 2
                shu = useg if d == 0 else pltpu.roll(useg, (-d) % n, 0)
                okf = jnp.logical_and(p + d >= 0, p + d < per)
                dws[kk] = dws[kk] + jnp.sum(jnp.where(okf, shu, 0.0) * dyseg, axis=0, keepdims=True)
                shd = dyseg if d == 0 else pltpu.roll(dyseg, d % n, 0)
                okb = jnp.logical_and(p - d >= 0, p - d < per)
                acc = acc + jnp.where(okb, shd, 0.0) * w_ref[kk:kk + 1, :]
            du_ref[r0:r0 + n, :] = acc
        for kk in range(kw):
            dw_ref[kk:kk + 1, :] = dws[kk]

    return pl.pallas_call(
        body, name=name, grid=(cw // LANES,),
        in_specs=[pl.BlockSpec((dm.T, LANES), lambda j: (0, j)), pl.BlockSpec((kw, LANES), lambda j: (0, j)),
                  pl.BlockSpec((dm.T, LANES), lambda j: (0, j))],
        out_specs=[pl.BlockSpec((dm.T, LANES), lambda j: (0, j)), pl.BlockSpec((kw, LANES), lambda j: (0, j))],
        out_shape=[jax.ShapeDtypeStruct((dm.T, cw), F32), jax.ShapeDtypeStruct((kw, cw), F32)],
        compiler_params=_cparams(("parallel",)),
    )(u, w, dy)


def pool_mix(u, transpose, dm, name, into=None):
    u, uw, ublk = _rowspec(u)
    gc = dm.GC
    ng = len(POOL_WINDOWS)
    rows = dm.SEQ // GRID_W
    segs = [(0, dm.CTX, 1, dm.CTX), (dm.CTX, dm.SEQ, GRID_W, rows)]

    def one_group(u_ref, o_ref, win):
        left = win // 2
        right = win - 1 - left
        for r0, n, stride, length in segs:
            useg = u_ref[r0:r0 + n, :].astype(F32)
            t = lax.broadcasted_iota(jnp.int32, (n, 1), 0)
            p = t if stride == 1 else jnp.right_shift(t, stride.bit_length() - 1)
            cnt = (jnp.minimum(p + right + 1, length) - jnp.maximum(p - left, 0)).astype(F32)
            src = useg / cnt if transpose else useg
            acc = jnp.zeros_like(useg)
            for d in range(-left, right + 1):
                dd = -d if transpose else d
                sh = src if d == 0 else pltpu.roll(src, (-dd * stride) % n, 0)
                ok = jnp.logical_and(p + dd >= 0, p + dd < length)
                acc = acc + jnp.where(ok, sh, 0.0)
            o_ref[r0:r0 + n, :] = ((acc - useg) if transpose else (acc / cnt - useg)).astype(o_ref.dtype)

    def body(u_ref, *rest):
        o_ref = rest[-1]
        g = pl.program_id(0)
        for gi, win in enumerate(POOL_WINDOWS):
            @pl.when(g == gi)
            def _(win=win):
                one_group(u_ref, o_ref, win)

    base = ublk * (uw // gc)
    if into is None:
        obase, out_shape, more, extra, aliases = 0, jax.ShapeDtypeStruct((dm.T, ng * gc), F32), [], [], {}
    else:
        buf, oblk = into
        obase, out_shape = oblk * ng, jax.ShapeDtypeStruct(buf.shape, buf.dtype)
        more, extra, aliases = [pl.BlockSpec(memory_space=pl.ANY)], [buf], {1: 0}
    return pl.pallas_call(
        body, name=name, grid=(ng,),
        in_specs=[pl.BlockSpec((dm.T, gc), lambda g: (0, base + g))] + more,
        out_specs=pl.BlockSpec((dm.T, gc), lambda g: (0, obase + g)),
        out_shape=out_shape, input_output_aliases=aliases,
        compiler_params=_cparams(("parallel",)),
    )(u, *extra)


def loss_head(x2, target, dm, name):
    tm, d = dm.tm, dm.D
    nctx = dm.CTX // tm

    def body(x_ref, t_ref, dx_ref, l_ref):
        i = pl.program_id(0)

        @pl.when(i == 0)
        def _():
            l_ref[...] = jnp.zeros_like(l_ref)

        @pl.when(i < nctx)
        def _():
            dx_ref[...] = jnp.zeros_like(dx_ref)

        @pl.when(i >= nctx)
        def _():
            e = x_ref[...] - t_ref[...]
            dx_ref[...] = e / d
            l_ref[...] += jnp.full(l_ref.shape, 0.5 * jnp.sum(jnp.mean(e * e, axis=-1)), F32)

    return pl.pallas_call(
        body, name=name, grid=(dm.T // tm,),
        in_specs=[pl.BlockSpec((tm, d), lambda i: (i, 0)),
                  pl.BlockSpec((tm, d), lambda i: (jnp.maximum(i - nctx, 0), 0))],
        out_specs=[pl.BlockSpec((tm, d), lambda i: (i, 0)), pl.BlockSpec((8, LANES), lambda i: (0, 0))],
        out_shape=[jax.ShapeDtypeStruct((dm.T, d), F32), jax.ShapeDtypeStruct((8, LANES), F32)],
        compiler_params=_cparams(("arbitrary",)),
    )(x2, target)


def adamw(w, g, m, v, name):
    r, c = w.shape
    tr = _tile(r, tuple(t for t in (512, 256, 128, 64, 32, 16, 8) if t * c * 4 <= (1 << 20)) or (8,))

    def body(w_ref, g_ref, m_ref, v_ref, d_ref, mo_ref, vo_ref):
        gg = g_ref[...]
        mm = ADAM_B1 * m_ref[...] + (1.0 - ADAM_B1) * gg
        vv = ADAM_B2 * v_ref[...] + (1.0 - ADAM_B2) * (gg * gg)
        m_hat = mm / (1.0 - ADAM_B1 ** ADAM_STEP)
        v_hat = vv / (1.0 - ADAM_B2 ** ADAM_STEP)
        d_ref[...] = -ADAM_LR * (m_hat / (jnp.sqrt(v_hat) + ADAM_EPS) + ADAM_WD * w_ref[...])
        mo_ref[...] = mm
        vo_ref[...] = vv

    spec = pl.BlockSpec((tr, c), lambda i: (i, 0))
    return pl.pallas_call(
        body, name=name, grid=(r // tr,), in_specs=[spec] * 4, out_specs=[spec] * 3,
        out_shape=[jax.ShapeDtypeStruct((r, c), F32)] * 3,
        compiler_params=_cparams(("parallel",)),
    )(w, g, m, v)


def pair_add(g, r1, cidx, name):
    ng, r_, n_ = r1.shape
    tr = _tile(r_, tuple(t for t in (1024, 512, 256, 128, 64, 32, 16) if t * n_ * 4 <= (2 << 20)))

    def body(s_ref, g_ref, r_ref, o_ref):
        o_ref[...] = (g_ref[...].astype(F32) + r_ref[...].astype(F32)).astype(o_ref.dtype)

    return pl.pallas_call(
        body, name=name,
        grid_spec=pltpu.PrefetchScalarGridSpec(
            num_scalar_prefetch=1, grid=(ng, r_ // tr),
            in_specs=[pl.BlockSpec((None, tr, n_), lambda k, i, s: (2 * k + s[0], i, 0)),
                      pl.BlockSpec((None, tr, n_), lambda k, i, s: (k, i, 0))],
            out_specs=pl.BlockSpec((None, tr, n_), lambda k, i, s: (k, i, 0))),
        out_shape=jax.ShapeDtypeStruct((ng, r_, n_), g.dtype),
        compiler_params=_cparams(("parallel", "parallel")),
    )(cidx, g, r1)


def chip_add(h, r2, axis, where, name, slab=True):
    _, kl, nl = r2.shape
    tr = _tile(kl, tuple(t for t in (1024, 512, 256, 128, 64, 32, 16) if t * nl * 4 <= (1 << 20)))
    nrb = kl // tr

    def body(s_ref, h_ref, r_ref, o_ref):
        acc = h_ref[...].astype(F32)
        for k in range(r2.shape[0]):
            acc = acc + r_ref[k].astype(F32)
        o_ref[...] = acc

    h_map = (lambda i, s: (s[0] * nrb + i, 0)) if axis == 0 else (lambda i, s: (i, s[0]))
    if slab:
        out_spec = pl.BlockSpec((None, tr, nl), lambda i, s: (s[1], i, 0))
        out_shape = jax.ShapeDtypeStruct((2, kl, nl), F32)
    else:
        out_spec = pl.BlockSpec((tr, nl), lambda i, s: (i, 0))
        out_shape = jax.ShapeDtypeStruct((kl, nl), F32)
    return pl.pallas_call(
        body, name=name,
        grid_spec=pltpu.PrefetchScalarGridSpec(
            num_scalar_prefetch=1, grid=(nrb,),
            in_specs=[pl.BlockSpec((tr, nl), h_map),
                      pl.BlockSpec((r2.shape[0], tr, nl), lambda i, s: (0, i, 0))],
            out_specs=out_spec),
        out_shape=out_shape,
        compiler_params=_cparams(("parallel",)),
    )(where, h, r2)


def adamw_layers(w, m, v, terms, name, layer=None, prev=None):
    _, a_, b_ = w.shape
    tr = _tile(a_, tuple(t for t in (512, 256, 128, 64, 32) if t * b_ * 4 <= (1 << 20)))
    by_cols = tr == a_ and a_ * b_ * 4 > (1 << 20)
    blk = (a_, LANES) if by_cols else (tr, b_)
    steps = b_ // LANES if by_cols else a_ // tr
    at = (lambda i: (0, i)) if by_cols else (lambda i: (i, 0))
    layers = (0, 1) if layer is None else (layer,)
    counts = [len(terms[l]) for l in layers]
    nprev = 0 if prev is None else 4

    def update(g, w_ref, m_ref, v_ref, g_ref, d_ref, mo_ref, vo_ref):
        mm = ADAM_B1 * m_ref[...] + (1.0 - ADAM_B1) * g
        vv = ADAM_B2 * v_ref[...] + (1.0 - ADAM_B2) * (g * g)
        m_hat = mm / (1.0 - ADAM_B1 ** ADAM_STEP)
        v_hat = vv / (1.0 - ADAM_B2 ** ADAM_STEP)
        g_ref[...] = g
        d_ref[...] = -ADAM_LR * (m_hat / (jnp.sqrt(v_hat) + ADAM_EPS) + ADAM_WD * w_ref[...])
        mo_ref[...] = mm
        vo_ref[...] = vv

    def total(refs):
        g = refs[0][...]
        for r in refs[1:]:
            g = g + r[...]
        return g

    def body(w_ref, m_ref, v_ref, *rest):
        t_refs, outs = rest[:sum(counts)], rest[-4:]
        if len(layers) == 1:
            update(total(t_refs), w_ref, m_ref, v_ref, *outs)
            return
        which = pl.program_id(0)

        @pl.when(which == 0)
        def _():
            update(total(t_refs[:counts[0]]), w_ref, m_ref, v_ref, *outs)

        @pl.when(which == 1)
        def _():
            update(total(t_refs[counts[0]:]), w_ref, m_ref, v_ref, *outs)

    if len(layers) == 1:
        stacked = pl.BlockSpec((None,) + blk, lambda l, i: (layers[0],) + at(i))
        t_specs = [pl.BlockSpec(blk, lambda l, i: at(i))] * counts[0]
    else:
        stacked = pl.BlockSpec((None,) + blk, lambda l, i: (l,) + at(i))
        t_specs = ([pl.BlockSpec(blk, lambda l, i: at(i * (1 - l)))] * counts[0]
                   + [pl.BlockSpec(blk, lambda l, i: at(i * l))] * counts[1])
    nin = 3 + sum(counts)
    return pl.pallas_call(
        body, name=name, grid=(len(layers), steps),
        in_specs=[stacked] * 3 + t_specs + [pl.BlockSpec(memory_space=pl.ANY)] * nprev,
        out_specs=[stacked] * 4, out_shape=[jax.ShapeDtypeStruct(w.shape, F32)] * 4,
        input_output_aliases={nin + j: j for j in range(nprev)},
        compiler_params=_cparams(("arbitrary", "arbitrary")),
    )(w, m, v, *[t for l in layers for t in terms[l]], *(prev or ()))


MESH = pl.DeviceIdType.MESH
ANY = pl.BlockSpec(memory_space=pl.ANY)
HBM = pl.BlockSpec(memory_space=pltpu.HBM)
SEM = pl.BlockSpec(memory_space=pltpu.SEMAPHORE)
EFFECT = pltpu.SideEffectType.DATAFLOW_SIDE_EFFECTING


def _place():
    return lax.axis_index("x"), lax.axis_index("y"), lax.axis_index("c")


def _peers(x, y):
    return [(1 - x, y), (x, 1 - y), (1 - x, 1 - y)]


def _rcopy(src, dst, ssem, rsem, dev):
    return pltpu.make_async_remote_copy(src_ref=src, dst_ref=dst, send_sem=ssem, recv_sem=rsem,
                                        device_id=dev, device_id_type=MESH)


def _gathered_shape(src, kind):
    h, a_, b_ = src.shape
    return (h, a_, 4 * b_) if kind == 'col' else (4, h, a_, b_)


def _win(ref, kind, ch, width):
    return ref.at[:, :, pl.ds(ch * width, width)] if kind == 'col' else ref.at[ch]


def _rect(ref, kind, half, ch, width):
    return ref.at[half, :, pl.ds(ch * width, width)] if kind == 'col' else ref.at[ch, half]


def _gather_plan(kinds, widths):
    def plan(src, land, x, y, c):
        chip = 2 * x + y
        out = []
        for n in range(len(src)):
            for px, py in _peers(x, y):
                out.append((src[n].at[c], _rect(land[n], kinds[n], c, chip, widths[n]), (px, py, c),
                            _rect(land[n], kinds[n], c, 2 * px + py, widths[n])))
            mine = _win(land[n], kinds[n], chip, widths[n])
            out.append((src[n], mine, (x, y, 1 - c), mine))
        return out
    return plan


def forward_halves(lands, kinds, name):
    nw = len(lands)
    widths = [t.shape[-1] // 4 if k == 'col' else t.shape[-1] for t, k in zip(lands, kinds)]

    def body(*refs):
        o = refs[nw:2 * nw]
        ssem, rsem = refs[2 * nw:]
        x, y, c = _place()
        sib = (x, y, 1 - c)
        pidx = [2 * px + py for px, py in _peers(x, y)]
        cps = [_rcopy(_rect(o[n], kinds[n], c, pidx[j], widths[n]), _rect(o[n], kinds[n], c, pidx[j], widths[n]),
                      ssem.at[3 * n + j], rsem.at[3 * n + j], sib) for n in range(nw) for j in range(3)]
        for cp in cps:
            cp.start()
        for n in range(nw):
            for j in range(3):
                cps[3 * n + j].wait_send()
                _rcopy(_rect(o[n], kinds[n], 1 - c, pidx[j], widths[n]), _rect(o[n], kinds[n], 1 - c, pidx[j], widths[n]),
                       ssem.at[3 * n + j], rsem.at[3 * n + j], sib).wait_recv()

    return pl.pallas_call(
        body, name=name, in_specs=[ANY] * nw, out_specs=[ANY] * nw,
        out_shape=[jax.ShapeDtypeStruct(t.shape, t.dtype) for t in lands],
        input_output_aliases={n: n for n in range(nw)},
        scratch_shapes=[pltpu.SemaphoreType.DMA((3 * nw,)), pltpu.SemaphoreType.DMA((3 * nw,))],
    )(*lands)


def _scatter_plan(axes, widths):
    def plan(src, land, x, y, c):
        out = []
        for n in range(len(src)):
            for k, (px, py) in enumerate(_peers(x, y)):
                ch = 2 * px + py
                view = (src[n].at[:, pl.ds(ch * widths[n], widths[n])] if axes[n] == 1
                        else src[n].at[pl.ds(ch * widths[n], widths[n]), :])
                out.append((view, land[n].at[k], (px, py, c), land[n].at[k]))
        return out
    return plan


def _exchange_plan(kinds):
    def plan(src, land, x, y, c):
        out = []
        for n in range(len(src)):
            w = land[n].shape[2]
            for j, (px, py) in enumerate(_peers(x, y)):
                ch = 2 * px + py
                view = src[n].at[:, pl.ds(ch * w, w)] if kinds[n] == 'col' else src[n].at[ch]
                out.append((view, land[n].at[j], (px, py, c), land[n].at[j]))
        return out
    return plan


def start_copies(srcs, lands, plan, ncopies, after, name):
    ns, nl = len(srcs), len(lands)

    def body(*refs):
        src, land = refs[:ns], refs[ns:ns + nl]
        ssem, rsem = refs[ns + nl + 1], refs[ns + nl + 2]
        token = refs[-1]
        x, y, c = _place()
        for k, (sv, dv, dev, _) in enumerate(plan(src, land, x, y, c)):
            _rcopy(sv, dv, ssem.at[k], rsem.at[k], dev).start()
        token[...] = jnp.zeros_like(token)

    hbm = lambda t: pltpu.HBM(t.shape, t.dtype)
    res = pl.pallas_call(
        body, name=name,
        out_shape=(pltpu.SemaphoreType.DMA((ncopies,)), pltpu.SemaphoreType.DMA((ncopies,)),
                   *[hbm(t) for t in srcs], *[hbm(t) for t in lands], jax.ShapeDtypeStruct((8, LANES), F32)),
        in_specs=[HBM] * (ns + nl) + [ANY],
        out_specs=(SEM, SEM, *[HBM] * (ns + nl), pl.BlockSpec(memory_space=pltpu.VMEM)),
        input_output_aliases={k: 2 + k for k in range(ns + nl)},
        compiler_params=pltpu.CompilerParams(has_side_effects=EFFECT),
    )(*[pltpu.with_memory_space_constraint(t, pltpu.HBM) for t in list(srcs) + list(lands)], after)
    return res[0], res[1], list(res[2:2 + ns]), list(res[2 + ns:2 + ns + nl]), res[-1]


def wait_copies(ssem, rsem, srcs, lands, plan, after, name):
    ns, nl = len(srcs), len(lands)

    def body(*refs):
        src, land = refs[:ns], refs[ns:ns + nl]
        ss, rs = refs[ns + nl], refs[ns + nl + 1]
        x, y, c = _place()
        for k, (sv, dv, dev, mine) in enumerate(plan(src, land, x, y, c)):
            cp = _rcopy(sv, mine, ss.at[k], rs.at[k], dev)
            cp.wait_send()
            cp.wait_recv()

    hbm = lambda t: pltpu.HBM(t.shape, t.dtype)
    res = pl.pallas_call(
        body, name=name,
        out_shape=(*[hbm(t) for t in srcs], *[hbm(t) for t in lands]),
        in_specs=[HBM] * (ns + nl) + [SEM, SEM, ANY], out_specs=tuple([HBM] * (ns + nl)),
        input_output_aliases={k: k for k in range(ns + nl)},
        compiler_params=pltpu.CompilerParams(has_side_effects=EFFECT),
    )(*srcs, *lands, ssem, rsem, after)
    return list(res[:ns]), list(res[ns:])


def pair_swap_halves(gs, kinds, name):
    nw = len(gs)

    def other(ref, kind, half):
        return ref.at[half] if kind == 'col' else ref.at[:, half]

    def body(*refs):
        g, o = refs[:nw], refs[nw:2 * nw]
        ssem, rsem = refs[2 * nw:]
        x, y, c = _place()
        cps = [_rcopy(other(g[n], kinds[n], 1 - c), o[n], ssem.at[n], rsem.at[n], (x, y, 1 - c)) for n in range(nw)]
        for cp in cps:
            cp.start()
        for cp in cps:
            cp.wait()

    return pl.pallas_call(
        body, name=name, in_specs=[ANY] * nw, out_specs=[ANY] * nw,
        out_shape=[jax.ShapeDtypeStruct(g.shape[1:] if k == 'col' else (g.shape[0],) + g.shape[2:], g.dtype)
                   for g, k in zip(gs, kinds)],
        scratch_shapes=[pltpu.SemaphoreType.DMA((nw,)), pltpu.SemaphoreType.DMA((nw,))],
    )(*gs)


def chip_broadcast(h, name):
    def body(h_ref, o_ref, ssem, rsem):
        x, y, c = _place()
        cps = [_rcopy(h_ref, o_ref.at[j], ssem.at[j], rsem.at[j], (px, py, c)) for j, (px, py) in enumerate(_peers(x, y))]
        for cp in cps:
            cp.start()
        for cp in cps:
            cp.wait()

    return pl.pallas_call(
        body, name=name, in_specs=[ANY], out_specs=ANY,
        out_shape=jax.ShapeDtypeStruct((3,) + h.shape, h.dtype),
        scratch_shapes=[pltpu.SemaphoreType.DMA((3,)), pltpu.SemaphoreType.DMA((3,))],
    )(h)


def quad_sum(h, r, cidx, name):
    r_, c_ = h.shape
    tr = _tile(r_, (512, 256, 128, 64, 32, 16, 8))

    def body(s_ref, h_ref, r_ref, o_ref):
        o_ref[...] = (h_ref[...] + r_ref[2]) + (r_ref[0] + r_ref[1])

    return pl.pallas_call(
        body, name=name,
        grid_spec=pltpu.PrefetchScalarGridSpec(
            num_scalar_prefetch=1, grid=(r_ // tr,),
            in_specs=[pl.BlockSpec((tr, c_), lambda i, s: (i, 0)), pl.BlockSpec((3, tr, c_), lambda i, s: (0, i, 0))],
            out_specs=pl.BlockSpec((None, tr, c_), lambda i, s: (s[0], i, 0))),
        out_shape=jax.ShapeDtypeStruct((2, r_, c_), F32),
        compiler_params=_cparams(("parallel",)),
    )(cidx, h, r)


def pair_join_layers(fs, name):
    nw = len(fs)

    def body(*refs):
        o = refs[nw:2 * nw]
        ssem, rsem = refs[2 * nw:]
        x, y, c = _place()
        sib = (x, y, 1 - c)
        cps = [_rcopy(o[n].at[c], o[n].at[c], ssem.at[n], rsem.at[n], sib) for n in range(nw)]
        for cp in cps:
            cp.start()
        for n in range(nw):
            cps[n].wait_send()
            _rcopy(o[n].at[1 - c], o[n].at[1 - c], ssem.at[n], rsem.at[n], sib).wait_recv()

    return pl.pallas_call(
        body, name=name, in_specs=[ANY] * nw, out_specs=[ANY] * nw,
        out_shape=[jax.ShapeDtypeStruct(f.shape, f.dtype) for f in fs],
        input_output_aliases={n: n for n in range(nw)},
        scratch_shapes=[pltpu.SemaphoreType.DMA((nw,)), pltpu.SemaphoreType.DMA((nw,))],
    )(*fs)


def _flatten_pad(parts, dtype):
    flat = jnp.concatenate([p.reshape(-1).astype(dtype) for p in parts])
    q = 512 * LANES
    n = -(-flat.shape[0] // q) * q
    return jnp.pad(flat, (0, n - flat.shape[0])).reshape(n // LANES, LANES)


def _lane_pad(n):
    return -(-n // LANES) * LANES


def _in_proj_layout(d):
    gk, gv, cw, pw = d // 2, d, d // 2, d // 2
    own = [('q', gk), ('k', gk), ('v', gv), ('og', gv), ('lrf', GLA_LR), ('lrb', GLA_LR), ('ga', cw), ('gb', cw),
           ('pu', pw), ('mg', 3 * d)]
    padded = [('mg', 3 * d), ('og', gv), ('v', gv), ('q', gk), ('k', gk), ('ga', cw), ('gb', cw), ('pu', pw),
              ('lrf', GLA_LR), ('lrb', GLA_LR), ('pad', d // 2 - 2 * GLA_LR)]
    return own, padded


def _row_pieces(src, lo, hi, wl, wlp):
    out = []
    for k in range(4):
        s0, s1 = max(lo, k * wl), min(hi, (k + 1) * wl)
        if s0 < s1:
            out.append(src[k * wlp + s0 - k * wl:k * wlp + s1 - k * wl])
    return out


def _proj_runs(d):
    own, padded = _in_proj_layout(d)
    oat, start = {}, 0
    for n, wd in own:
        oat[n] = start
        start += wd
    runs, start = [], 0
    for n, wd in padded:
        if n != 'pad':
            if runs and runs[-1][0] + runs[-1][2] == oat[n] and runs[-1][1] + runs[-1][2] == start:
                runs[-1] = (runs[-1][0], runs[-1][1], runs[-1][2] + wd)
            else:
                runs.append((oat[n], start, wd))
        start += wd
    return runs, start


def _w_in_t_to_proj(g, d, wl, wlp):
    runs, total = _proj_runs(d)
    parts, at = [], 0
    for o0, p0, wd in runs:
        if p0 > at:
            parts.append(jnp.zeros((p0 - at, g.shape[1]), g.dtype))
        parts += _row_pieces(g, o0, o0 + wd, wl, wlp)
        at = p0 + wd
    if total > at:
        parts.append(jnp.zeros((total - at, g.shape[1]), g.dtype))
    return jnp.concatenate(parts, axis=0)


def _proj_to_w_in_t(gp, d, wl, wlp):
    runs, _ = _proj_runs(d)
    runs = sorted(runs)
    parts = []
    for k in range(4):
        for o0, p0, wd in runs:
            s0, s1 = max(o0, k * wl), min(o0 + wd, (k + 1) * wl)
            if s0 < s1:
                parts.append(gp[p0 + s0 - o0:p0 + s1 - o0])
        parts.append(jnp.zeros((wlp - wl, gp.shape[1]), gp.dtype))
    return jnp.concatenate(parts, axis=0)


def _silu_grad(z):
    s = jax.nn.sigmoid(z)
    return s + z * s * (1.0 - s)


def kernel(x, c, ctx, c_ctx, w_ada, b_ada, g_pre_mix, g_post_mix, g_pre_mlp, g_post_mlp, w_in, w_decay, b_decay, g_gla, w_gla_o, w_dw, b_dw, g_conv_ln, b_conv_ln, w_conv_o, w_pool_g, s_pool, w_pool_o, b_gate, w_out, w_mlp1, w_mlp2, loss_target, m_c_ctx, m_w_ada, m_b_ada, m_g_pre_mix, m_g_post_mix, m_g_pre_mlp, m_g_post_mlp, m_w_in, m_w_decay, m_b_decay, m_g_gla, m_w_gla_o, m_w_dw, m_b_dw, m_g_conv_ln, m_b_conv_ln, m_w_conv_o, m_w_pool_g, m_s_pool, m_w_pool_o, m_b_gate, m_w_out, m_w_mlp1, m_w_mlp2, v_c_ctx, v_w_ada, v_b_ada, v_g_pre_mix, v_g_post_mix, v_g_pre_mlp, v_g_post_mlp, v_w_in, v_w_decay, v_b_decay, v_g_gla, v_w_gla_o, v_w_dw, v_b_dw, v_g_conv_ln, v_b_conv_ln, v_w_conv_o, v_w_pool_g, v_s_pool, v_w_pool_o, v_b_gate, v_w_out, v_w_mlp1, v_w_mlp2):
    a = dict(locals())
    for n in ('w_in', 'm_w_in', 'v_w_in'):
        a[n] = jnp.swapaxes(a[n], 1, 2)
    big_axis = dict(BIG, w_in=1)
    depth = w_in.shape[0]
    d = x.shape[-1]
    seq, nctx_rows = x.shape[1], ctx.shape[1]
    dm = types.SimpleNamespace(
        D=d, SEQ=seq, CTX=nctx_rows, T=seq + nctx_rows, DK=d // 8, DV=d // 4, GK=d // 2, GC=d // 8,
        tm=_tile(nctx_rows, (256, 128, 64)), TB=_tile(nctx_rows, (256, 128, 64)))
    assert dm.SEQ % dm.tm == 0 and dm.SEQ % GRID_W == 0 and dm.CTX % GLA_CHUNK == 0
    tmw = min(dm.tm, 128)
    chip = 2 * lax.axis_index("x") + lax.axis_index("y")
    core = lax.axis_index("c")
    chip1 = chip.astype(jnp.int32).reshape(1)
    core1 = core.astype(jnp.int32).reshape(1)

    big_names, small_names = list(BIG), list(SMALL_SHARDED)
    nbig = len(big_names)
    kinds = ['col' if big_axis[n] == 2 else 'row' for n in big_names]
    wl = w_in.shape[2]
    wlp = _lane_pad(wl)

    def rows8(t):
        t = t.reshape(t.shape[0], -1, t.shape[-1])
        return jnp.pad(t, ((0, 0), (0, -t.shape[1] % 8), (0, 0)))

    def halves(t):
        return t.reshape(2, t.shape[0] // 2, t.shape[1])

    def layer_src(l, tok=None):
        def one(n):
            t = a[n][l] if tok is None else a[n][l] + tok
            return halves((jnp.pad(t, ((0, wlp - wl), (0, 0))) if n == 'w_in' else t).astype(MM_DTYPE))
        return [one(n) for n in big_names]

    def whole(t):
        return t.reshape(-1, t.shape[-1])

    def start_gather(srcs, knds, after, name):
        plan = _gather_plan(knds, [t.shape[2] for t in srcs])
        lands = [lax.empty(_gathered_shape(t, k), t.dtype) for t, k in zip(srcs, knds)]
        return (plan,) + start_copies(srcs, lands, plan, 4 * len(srcs), after, name)

    late = [big_names.index(n) for n in ('w_gla_o', 'w_conv_o', 'w_pool_o', 'w_out', 'w_mlp1', 'w_mlp2')]
    early = [k for k in range(nbig) if k not in late]
    src0 = layer_src(0)
    kinds_e = [kinds[k] for k in early] + ['col'] * len(small_names)
    age = start_gather([src0[k] for k in early] + [rows8(a[n]) for n in small_names], kinds_e, core1,
                       "gather_layer0_start")
    tok0 = age[-1][0, 0]
    src1 = layer_src(1, tok0)
    pk = lambda pre: _flatten_pad([a[pre + n] + tok0 for n in SMALL], F32)
    small_w, small_m, small_v = pk(''), pk('m_'), pk('v_')
    X = jnp.concatenate([ctx[0] + tok0, x[0] + tok0], axis=0)
    ready = (small_w[0, 0] + small_m[0, 0] + small_v[0, 0] + X[0, 0]
             + sum(t[0, 0, 0].astype(F32) for t in src1)).reshape(1, 1)
    _, g0 = wait_copies(age[1], age[2], age[3], age[4], age[0], ready, "gather_layer0_wait")
    g0 = forward_halves(g0, kinds_e, "gather_layer0_forward")
    ag0 = start_gather([src0[k] for k in late], [kinds[k] for k in late], g0[0], "gather_layer0_late_start")
    ag1 = start_gather(src1, kinds, ag0[-1], "gather_layer1_start")
    ag_token = ag1[-1]
    full = {n: [None, None] for n in big_names}
    for k, t in zip(early, g0):
        full[big_names[k]][0] = whole(t)
    for n, g in zip(small_names, g0[len(early):]):
        shp = a[n].shape
        full[n] = g[:, :math.prod(shp[1:-1])].reshape(shp[:-1] + (4 * shp[-1],))
    for n in SMALL:
        if n not in SMALL_SHARDED:
            full[n] = a[n]

    cvec = jnp.concatenate([c_ctx.reshape(1, d), c.reshape(1, d), jnp.zeros((6, d), F32)], axis=0)
    avec = (cvec * jax.nn.sigmoid(cvec) + ag_token[0, 0]).astype(MM_DTYPE)

    def row(v):
        return v.reshape(1, -1)

    saved = []
    gk, gv = dm.GK, d
    lrblk = (7 * d + d // 2) // LANES
    for l in range(depth):
        if l == 1:
            _, got = wait_copies(ag1[1], ag1[2], ag1[3], ag1[4], ag1[0], X, "gather_layer1_wait")
            got = forward_halves(got, kinds, "gather_layer1_forward")
            for n, t in zip(big_names, got):
                full[n][1] = whole(t)
        s = types.SimpleNamespace()
        s.w_in_p = _w_in_t_to_proj(full['w_in'][l], d, wl, wlp)
        wd = full['w_decay'][l]
        wdp = jnp.zeros((LANES, 2 * gk), F32)
        wdp = wdp.at[:GLA_LR, :gk].set(wd[0]).at[GLA_LR:2 * GLA_LR, gk:].set(wd[1])
        s.wdp = wdp.astype(MM_DTYPE)
        s.wdp_wide = jnp.pad(s.wdp, ((0, d // 2 - LANES), (0, 0)))
        s.bd = full['b_decay'][l].reshape(1, 2 * gk)
        modraw = matmul(avec, full['w_ada'][l], 'nn', F32, f"mod_{l}") + full['b_ada'][l][None, :]
        s.mod = [modraw[0:2, j * d:(j + 1) * d].reshape(2, 1, d) for j in range(6)]
        s.x = X
        (s.h,) = rowwise(pre_fn, [X], s.mod[0:2], [row(g_pre_mix[l])], [(d, MM_DTYPE)], dm, f"pre_{l}")
        s.P = matmul(s.h, s.w_in_p, 'nt', MM_DTYPE, f"in_proj_{l}")
        P = s.P
        s.z = matmul((P, LANES, lrblk), s.wdp, 'nn', F32, f"decay_proj_{l}", tk=LANES)
        la_f, la_b = rowwise(decay_fn, [s.z], [], [s.bd], [(gk, F32), (gk, F32)], dm, f"decay_{l}")
        s.la = jnp.concatenate([la_f, la_b], axis=1)
        s.o_f, s.st_f = gla_fwd(P, s.la, False, dm, f"gla_fwd_f_{l}")
        s.o_b, s.st_b = gla_fwd(P, s.la, True, dm, f"gla_fwd_b_{l}")
        (s.gin,) = rowwise(glaout_fn, [s.o_f, s.o_b, (P, d, 3)], [], [row(g_gla[l])], [(gv, MM_DTYPE)], dm,
                           f"gla_out_{l}")
        if l == 0:
            _, got = wait_copies(ag0[1], ag0[2], ag0[3], ag0[4], ag0[0], s.gin, "gather_layer0_late_wait")
            got = forward_halves(got, [kinds[k] for k in late], "gather_layer0_late_forward")
            for k, t in zip(late, got):
                full[big_names[k]][0] = whole(t)
        s.ya = matmul(s.gin, full['w_gla_o'][l], 'nn', MM_DTYPE, f"gla_o_{l}")
        (s.u,) = rowwise(glu_fn, [(P, d, 6)], [], [], [(d // 2, F32)], dm, f"glu_{l}")
        s.yconv = conv_fwd(s.u, full['w_dw'][l], dm, f"conv_{l}")
        (s.cin,) = rowwise(convpost_fn, [s.yconv], [], [row(b_dw[l]), row(g_conv_ln[l]), row(b_conv_ln[l])],
                           [(d // 2, MM_DTYPE)], dm, f"conv_post_{l}")
        s.yb = matmul(s.cin, full['w_conv_o'][l], 'nn', MM_DTYPE, f"conv_o_{l}")
        s.pm = pool_mix((P, d // 2, 14), False, dm, f"pool_mix_{l}")
        s.pc = group_mm(s.pm, w_pool_g[l], 'nn', F32, f"pool_g_{l}")
        (s.pin,) = rowwise(poolpost_fn, [s.pc], [], [row(s_pool[l])], [(d // 2, MM_DTYPE)], dm, f"pool_post_{l}")
        s.yc = matmul(s.pin, full['w_pool_o'][l], 'nn', MM_DTYPE, f"pool_o_{l}")
        s.bg = [row(full['b_gate'][l][j]) for j in range(3)]
        (s.mixed,) = rowwise(merge_fn, [s.ya, s.yb, s.yc, (P, 3 * d, 0)], [], s.bg, [(d, MM_DTYPE)], dm,
                             f"merge_{l}", tm=tmw)
        s.y = matmul(s.mixed, full['w_out'][l], 'nn', MM_DTYPE, f"out_proj_{l}")
        s.x1, s.h2 = rowwise(mid_fn, [X, s.y], s.mod[2:5], [row(g_post_mix[l]), row(g_pre_mlp[l])],
                             [(d, F32), (d, MM_DTYPE)], dm, f"mid_{l}")
        s.act = matmul(s.h2, full['w_mlp1'][l], 'nn', MM_DTYPE, f"mlp1_{l}", epi=relu2_epi)
        s.y2 = matmul(s.act, full['w_mlp2'][l], 'nn', MM_DTYPE, f"mlp2_{l}")
        (X,) = rowwise(post_fn, [s.x1, s.y2], s.mod[5:6], [row(g_post_mlp[l])], [(d, F32)], dm, f"post_{l}")
        saved.append(s)

    dX, lossv = loss_head(X, loss_target[0], dm, "loss_head")
    loss = lax.psum(lossv[0, 0], ("x", "y", "c"))

    grads = {n: [None] * depth for n in WEIGHTS if n != 'c_ctx' and n not in BIG}
    gbig = {n: [None] * depth for n in BIG}
    rs_token = None

    def start_scatter(idx, layer, after, name):
        gs = [gbig[big_names[k]][layer] for k in idx]
        wd = [t.shape[1] // 4 if kinds[k] == 'col' else t.shape[0] // 4 for t, k in zip(gs, idx)]
        plan = _scatter_plan([big_axis[big_names[k]] - 1 for k in idx], wd)
        lands = [lax.empty((3, t.shape[0], w) if kinds[k] == 'col' else (3, w, t.shape[1]), t.dtype)
                 for t, w, k in zip(gs, wd, idx)]
        return (plan,) + start_copies(gs, lands, plan, 3 * len(gs), after, name)

    g_cctx = jnp.zeros((d,), F32)
    for l in reversed(range(depth)):
        s = saved[l]
        P = s.P
        dmod = [None] * 6
        gpm = row(g_post_mlp[l]) if rs_token is None else row(g_post_mlp[l]) + rs_token[0, 0]
        (dx1, dy2), (dmod[5],), (dg,) = rowwise_vjp(post_fn, [s.x1, s.y2], s.mod[5:6], [gpm], [dX],
                                                     dm, f"post_bwd_{l}", narrow=(1,))
        grads['g_post_mlp'][l] = dg[0]
        du1 = matmul(dy2, full['w_mlp2'][l], 'nt', MM_DTYPE, f"mlp2_dx_{l}", epi=relu2_bwd_epi, extras=[s.act])
        gbig['w_mlp2'][l] = matmul(s.act, dy2, 'tn', MM_DTYPE, f"mlp2_dw_{l}")
        dh2 = matmul(du1, full['w_mlp1'][l], 'nt', MM_DTYPE, f"mlp1_dx_{l}")
        gbig['w_mlp1'][l] = matmul(s.h2, du1, 'tn', MM_DTYPE, f"mlp1_dw_{l}")
        gpx = row(g_post_mix[l])
        (dxa, dy), dmod[2:5], (dg1, dg2) = rowwise_vjp(
            mid_fn, [s.x, s.y], s.mod[2:5], [gpx, row(g_pre_mlp[l])], [dx1, dh2], dm, f"mid_bwd_{l}", narrow=(1,))
        grads['g_post_mix'][l], grads['g_pre_mlp'][l] = dg1[0], dg2[0]
        dmixed = matmul(dy, full['w_out'][l], 'nt', MM_DTYPE, f"out_proj_dx_{l}")
        gbig['w_out'][l] = matmul(s.mixed, dy, 'tn', MM_DTYPE, f"out_proj_dw_{l}")
        (dya, dyb, dyc, dP), _, dbg = rowwise_vjp(merge_fn, [s.ya, s.yb, s.yc, (P, 3 * d, 0)], [], s.bg, [dmixed],
                                                  dm, f"merge_bwd_{l}", tm=tmw, narrow=(0, 1, 2),
                                                  into=(3, None, P.shape))
        grads['b_gate'][l] = jnp.concatenate(dbg, axis=0)
        dgin = matmul(dya, full['w_gla_o'][l], 'nt', MM_DTYPE, f"gla_o_dx_{l}")
        gbig['w_gla_o'][l] = matmul(s.gin, dya, 'tn', MM_DTYPE, f"gla_o_dw_{l}")
        dcin = matmul(dyb, full['w_conv_o'][l], 'nt', MM_DTYPE, f"conv_o_dx_{l}")
        gbig['w_conv_o'][l] = matmul(s.cin, dyb, 'tn', MM_DTYPE, f"conv_o_dw_{l}")
        dpin = matmul(dyc, full['w_pool_o'][l], 'nt', MM_DTYPE, f"pool_o_dx_{l}")
        gbig['w_pool_o'][l] = matmul(s.pin, dyc, 'tn', MM_DTYPE, f"pool_o_dw_{l}")
        sp = row(s_pool[l])
        if l == 0:
            rs0 = start_scatter(late, 0, dpin, "grad_layer0_late_start")
            sp = sp + rs0[-1][0, 0]
        (dpc,), _, (dsp,) = rowwise_vjp(poolpost_fn, [s.pc], [], [sp], [dpin], dm, f"pool_post_bwd_{l}")
        grads['s_pool'][l] = dsp[0]
        grads['w_pool_g'][l] = group_mm(s.pm, w_pool_g[l], 'tn', F32, f"pool_g_dw_{l}", b=dpc)
        dpm = group_mm(dpc, w_pool_g[l], 'nt', F32, f"pool_g_dx_{l}")
        dP = pool_mix(dpm, True, dm, f"pool_mix_bwd_{l}", into=(dP, 14))
        (dyconv,), _, (dbdw, dgln, dbln) = rowwise_vjp(
            convpost_fn, [s.yconv], [], [row(b_dw[l]), row(g_conv_ln[l]), row(b_conv_ln[l])], [dcin], dm,
            f"conv_post_bwd_{l}")
        grads['b_dw'][l], grads['g_conv_ln'][l], grads['b_conv_ln'][l] = dbdw[0], dgln[0], dbln[0]
        du, grads['w_dw'][l] = conv_bwd(s.u, full['w_dw'][l], dyconv, dm, f"conv_bwd_{l}")
        (dP,), _, _ = rowwise_vjp(glu_fn, [(P, d, 6)], [], [], [du], dm, f"glu_bwd_{l}", into=(0, dP, P.shape))
        (do, _, dP), _, (dgg,) = rowwise_vjp(glaout_fn, [s.o_f, s.o_b, (P, d, 3)], [], [row(g_gla[l])], [dgin], dm,
                                             f"gla_out_bwd_{l}", want=[True, False, True], into=(2, dP, P.shape), narrow=(0,))
        grads['g_gla'][l] = dgg[0]
        dqf, dkf, dvf, dlaf = gla_bwd(P, s.la, do, s.st_f, False, dm, f"gla_bwd_f_{l}")
        dP, dlab = gla_bwd(P, s.la, do, s.st_b, True, dm, f"gla_bwd_b_{l}", prev=(dqf, dkf, dvf), into=dP)
        (dz,), _, (dbd,) = rowwise_vjp(decay_fn, [s.z], [], [s.bd], [dlaf, dlab], dm, f"decay_bwd_{l}", narrow=(0,))
        grads['b_decay'][l] = dbd.reshape(2, gk)
        dwdp = matmul((P, LANES, lrblk), dz, 'tn', F32, f"decay_proj_dw_{l}", tm=LANES)
        grads['w_decay'][l] = jnp.stack([dwdp[:GLA_LR, :gk], dwdp[GLA_LR:2 * GLA_LR, gk:]])
        dP = matmul(dz, s.wdp_wide, 'nt', MM_DTYPE, f"decay_proj_dx_{l}", into=(dP, 15))
        dh = matmul(dP, s.w_in_p, 'nn', MM_DTYPE, f"in_proj_dx_{l}")
        gbig['w_in'][l] = _proj_to_w_in_t(matmul(dP, s.h, 'tn', MM_DTYPE, f"in_proj_dw_{l}"), d, wl, wlp)
        (dX,), dmod[0:2], (dg,) = rowwise_vjp(pre_fn, [s.x], s.mod[0:2], [row(g_pre_mix[l])], [dh], dm,
                                               f"pre_bwd_{l}", adds={0: dxa})
        grads['g_pre_mix'][l] = dg[0]
        dmodflat = jnp.concatenate([jnp.concatenate([m_.reshape(2, d) for m_ in dmod], axis=1),
                                    jnp.zeros((6, 6 * d), F32)], axis=0)
        grads['b_ada'][l] = dmodflat[0] + dmodflat[1]
        gbig['w_ada'][l] = matmul(avec, dmodflat, 'tn', MM_DTYPE, f"ada_dw_{l}")
        dav = matmul(dmodflat, full['w_ada'][l], 'nt', F32, f"ada_dx_{l}")
        g_cctx = g_cctx + dav[0] * _silu_grad(c_ctx)
        if l == 1:
            rs1 = start_scatter(list(range(nbig)), 1, dav, "grad_layer1_start")
            rs_token = rs1[-1]

    grad_x = dX[dm.CTX:][None]
    gfull = {n: jnp.stack(v) for n, v in grads.items()}
    gfull['c_ctx'] = g_cctx
    where = jnp.concatenate([chip1, core1])

    def halves_view(t, k):
        return t.reshape(2, t.shape[0] // 2, t.shape[1]) if k == 'col' else t.reshape(4, 2, t.shape[0] // 8, t.shape[1])
    enames = [big_names[k] for k in early]
    ekinds = [kinds[k] for k in early]
    v0 = [halves_view(gbig[n][0], k) for n, k in zip(enames, ekinds)]
    r1 = pair_swap_halves(v0, ekinds, "grad_pair_swap")
    hs = [pair_add(v.reshape((-1,) + v.shape[-2:]), r.reshape((-1,) + r.shape[-2:]), core1, f"grad_pair_add_{n}")
          for n, v, r in zip(enames, v0, r1)]
    hx = [h.reshape(h.shape[1:]) if k == 'col' else h for h, k in zip(hs, ekinds)]
    ex_plan = _exchange_plan(ekinds)
    ex_lands = [lax.empty((3, h.shape[0], h.shape[1] // 4) if k == 'col' else (3,) + h.shape[1:], h.dtype)
                for h, k in zip(hx, ekinds)]
    ex = (ex_plan,) + start_copies(hx, ex_lands, ex_plan, 3 * len(hx), core1, "grad_chip_exchange_start")

    gs0, got0 = wait_copies(rs0[1], rs0[2], rs0[3], rs0[4], rs0[0], ex[-1], "grad_layer0_late_wait")
    gs1, got1 = wait_copies(rs1[1], rs1[2], rs1[3], rs1[4], rs1[0], ex[-1], "grad_layer1_wait")
    sa = [chip_add(g, r, big_axis[big_names[k]] - 1, where, f"grad_layer0_add_{big_names[k]}", slab=False)
          for k, g, r in zip(late, gs0, got0)]
    sa += [chip_add(g, r, big_axis[n] - 1, where, f"grad_layer1_add_{n}", slab=False)
           for n, g, r in zip(big_names, gs1, got1)]
    sflat = _flatten_pad([gfull[n].astype(F32) for n in SMALL], F32)
    sv = sflat.reshape(2, sflat.shape[0] // 2, LANES)
    (sr,) = pair_swap_halves([sv], ['col'], "small_grad_pair_swap")
    sh = pair_add(sv, sr[None], core1, "small_grad_pair_add")[0]
    sq = quad_sum(sh, chip_broadcast(sh, "small_grad_chip_exchange"), core1, "small_grad_chip_sum")
    (ssum,) = pair_join_layers([sq], "small_grad_pair_join")

    def swap_plan(src, land, x, y, c):
        return [(src[n], land[n], (x, y, 1 - c), land[n]) for n in range(len(src))]
    swp = start_copies(sa, [lax.empty(t.shape, t.dtype) for t in sa], swap_plan, len(sa), ssum,
                       "grad_late_pair_swap_start")
    ssum = ssum.reshape(-1) + swp[-1][0, 0]
    start = 0
    sg = {}
    for n in SMALL:
        cnt = gfull[n].size
        g = ssum[start:start + cnt].reshape(gfull[n].shape)
        start += cnt
        if n in SMALL_SHARDED:
            ax = SMALL_SHARDED[n]
            wdt = a[n].shape[ax]
            g = lax.dynamic_slice_in_dim(g, chip * wdt, wdt, axis=ax)
        sg[n] = g
    gs = _flatten_pad([sg[n] for n in SMALL], F32)
    dl, mn, vn = adamw(small_w, gs, small_m, small_v, "adamw_small")

    sa, sb = wait_copies(swp[0], swp[1], swp[2], swp[3], swap_plan, dl, "grad_late_pair_swap_wait")
    red0 = {big_names[k]: [sa[j], sb[j]] for j, k in enumerate(late)}
    red1 = {n: [sa[len(late) + k], sb[len(late) + k]] for k, n in enumerate(big_names)}

    out_g, out_d, out_m, out_v = {}, {}, {}, {}

    def update_big(n, terms, **kw):
        res = adamw_layers(a[n], a['m_' + n], a['v_' + n], terms, f"adamw_{n}" + ("" if not kw else f"_{kw['layer']}"), **kw)
        out_g[n], out_d[n], out_m[n], out_v[n] = res
        return res
    for k in late:
        update_big(big_names[k], [red0[big_names[k]], red1[big_names[k]]])
    half_done = {n: update_big(n, {1: red1[n]}, layer=1) for n in enames}
    done = (dl[0, 0] + sum(out_d[n][1, 0, 0] for n in big_names)).reshape(1, 1)
    hx, r2 = wait_copies(ex[1], ex[2], ex[3], ex[4], ex[0], done, "grad_chip_exchange_wait")
    dl, mn, vn = dl.reshape(-1), mn.reshape(-1), vn.reshape(-1)
    start = 0
    for n in SMALL:
        cnt, shp = a[n].size, a[n].shape
        out_g[n] = sg[n]
        out_d[n], out_m[n], out_v[n] = (t[start:start + cnt].reshape(shp) for t in (dl, mn, vn))
        start += cnt
    fs = [chip_add(h.reshape(-1, h.shape[-1]), r, big_axis[n] - 1, where, f"grad_chip_add_{n}")
          for n, h, r in zip(enames, hx, r2)]
    for n, t in zip(enames, pair_join_layers(fs, "grad_pair_join")):
        update_big(n, {0: [t.reshape(-1, t.shape[-1])]}, layer=0, prev=tuple(half_done[n]))
    for dct in (out_g, out_d, out_m, out_v):
        dct['w_in'] = jnp.swapaxes(dct['w_in'], 1, 2)
    return (loss, grad_x, *[out_g[n] for n in WEIGHTS], *[out_d[n] for n in WEIGHTS],
            *[out_m[n] for n in WEIGHTS], *[out_v[n] for n in WEIGHTS])
```

```python
import math
import types

import jax
import jax.numpy as jnp
from jax import lax
from jax.experimental import pallas as pl
from jax.experimental.pallas import tpu as pltpu

F32 = jnp.float32
MM_DTYPE = jnp.bfloat16
VMEM_LIMIT_V7X = 56 * 1024 * 1024
LANES = 128
EPS = 1e-6

N_HEADS = 4
GLA_CHUNK = 64
GLA_TAU = 16.0
GLA_LR = 16
GRID_W = 64
POOL_WINDOWS = (2, 4, 8, 16)

ADAM_LR = 0.001
ADAM_B1 = 0.9
ADAM_B2 = 0.999
ADAM_EPS = 1e-08
ADAM_WD = 0.01
ADAM_STEP = 10

NN = (((1,), (0,)), ((), ()))
NT = (((1,), (1,)), ((), ()))
TN = (((0,), (0,)), ((), ()))

WEIGHTS = ['c_ctx', 'w_ada', 'b_ada', 'g_pre_mix', 'g_post_mix', 'g_pre_mlp', 'g_post_mlp', 'w_in', 'w_decay',
           'b_decay', 'g_gla', 'w_gla_o', 'w_dw', 'b_dw', 'g_conv_ln', 'b_conv_ln', 'w_conv_o', 'w_pool_g',
           's_pool', 'w_pool_o', 'b_gate', 'w_out', 'w_mlp1', 'w_mlp2']
BIG = {'w_ada': 2, 'w_in': 2, 'w_gla_o': 1, 'w_conv_o': 2, 'w_pool_o': 2, 'w_out': 1, 'w_mlp1': 2, 'w_mlp2': 1}
SMALL_SHARDED = {'w_decay': 3, 'b_decay': 2, 'w_dw': 2, 'b_gate': 2}
SMALL = [n for n in WEIGHTS if n not in BIG]


def _tile(n, prefs):
    for t in prefs:
        if n % t == 0:
            return t
    return n


def _cparams(sem=None, **kw):
    return pltpu.CompilerParams(dimension_semantics=sem, vmem_limit_bytes=VMEM_LIMIT_V7X, **kw)


def _dot(a, b, dims=NN):
    return lax.dot_general(a.astype(MM_DTYPE), b.astype(MM_DTYPE), dims, preferred_element_type=F32)


def matmul(a, b, mode, out_dtype, name, tm=None, tn=None, tk=None, epi=None, extras=(), into=None):
    a, aw, ablk = a if isinstance(a, tuple) else (a, a.shape[1], 0)
    if mode == 'nn':
        M, K, N = a.shape[0], aw, b.shape[1]
    elif mode == 'nt':
        M, K, N = a.shape[0], aw, b.shape[0]
    else:
        K, M, N = a.shape[0], aw, b.shape[1]
    big = (1088, 1024, 640, 544, 512, 320, 256, 128, 64, 32, 16, 8)
    if mode == 'tn':
        tm = tm or _tile(M, (1024, 512, 256, 128))
        tn = tn or _tile(N, (1024, 512, 256, 128))
        tk = tk or _tile(K, big)
    else:
        tm = tm or _tile(M, big)
        tn = tn or _tile(N, (1024, 512, 256, 128))
        tk = tk or _tile(K, (1024, 512, 256, 128))
    if aw != a.shape[1]:
        assert (mode == 'tn' and tm == aw) or (mode != 'tn' and tk == aw)
    nk = K // tk
    ne = len(extras)
    dims = {'nn': NN, 'nt': NT, 'tn': TN}[mode]

    def body(a_ref, b_ref, *rest):
        e_refs, o_ref = rest[:ne], rest[ne + (into is not None)]

        def finish(acc):
            if epi is not None:
                acc = epi(acc, *[e[...] for e in e_refs])
            o_ref[...] = acc.astype(o_ref.dtype)

        p = _dot(a_ref[...], b_ref[...], dims)
        if nk == 1:
            finish(p)
            return
        acc = rest[-1]
        k = pl.program_id(2)

        @pl.when(k == 0)
        def _():
            acc[...] = p

        @pl.when(k > 0)
        def _():
            acc[...] += p

        @pl.when(k == nk - 1)
        def _():
            finish(acc[...])

    if mode == 'nn':
        a_spec = pl.BlockSpec((tm, tk), lambda i, j, k: (i, k + ablk))
        b_spec = pl.BlockSpec((tk, tn), lambda i, j, k: (k, j))
    elif mode == 'nt':
        a_spec = pl.BlockSpec((tm, tk), lambda i, j, k: (i, k + ablk))
        b_spec = pl.BlockSpec((tn, tk), lambda i, j, k: (j, k))
    else:
        a_spec = pl.BlockSpec((tk, tm), lambda i, j, k: (k, i + ablk))
        b_spec = pl.BlockSpec((tk, tn), lambda i, j, k: (k, j))
    tile = pl.BlockSpec((tm, tn), lambda i, j, k: (i, j))
    if into is None:
        out_spec, out_shape, more, extra, aliases = tile, jax.ShapeDtypeStruct((M, N), out_dtype), [], [], {}
    else:
        buf, oblk = into
        out_spec = pl.BlockSpec((tm, tn), lambda i, j, k: (i, oblk * (N // tn) + j))
        out_shape = jax.ShapeDtypeStruct(buf.shape, buf.dtype)
        more, extra, aliases = [pl.BlockSpec(memory_space=pl.ANY)], [buf], {2 + ne: 0}
    return pl.pallas_call(
        body, name=name, grid=(M // tm, N // tn, nk),
        in_specs=[a_spec, b_spec] + [tile] * ne + more, out_specs=out_spec,
        out_shape=out_shape, input_output_aliases=aliases,
        scratch_shapes=[] if nk == 1 else [pltpu.VMEM((tm, tn), F32)],
        compiler_params=_cparams(("parallel", "parallel", "arbitrary")),
    )(a, b, *extras, *extra)


def group_mm(a, w, mode, out_dtype, name, b=None):
    T = a.shape[0]
    G, gc, _ = w.shape
    col = pl.BlockSpec((T, gc), lambda g: (0, g))
    wsp = pl.BlockSpec((1, gc, gc), lambda g: (g, 0, 0))
    if mode == 'tn':
        def body(a_ref, b_ref, o_ref):
            o_ref[0] = _dot(a_ref[...], b_ref[...], TN).astype(o_ref.dtype)
        return pl.pallas_call(body, name=name, grid=(G,), in_specs=[col, col], out_specs=wsp,
                              out_shape=jax.ShapeDtypeStruct((G, gc, gc), out_dtype),
                              compiler_params=_cparams(("parallel",)))(a, b)
    dims = NN if mode == 'nn' else NT

    def body(a_ref, w_ref, o_ref):
        o_ref[...] = _dot(a_ref[...], w_ref[0], dims).astype(o_ref.dtype)
    return pl.pallas_call(body, name=name, grid=(G,), in_specs=[col, wsp], out_specs=col,
                          out_shape=jax.ShapeDtypeStruct((T, G * gc), out_dtype),
                          compiler_params=_cparams(("parallel",)))(a, w)


def _rowspec(r):
    return r if isinstance(r, tuple) else (r, r.shape[1], 0)


def _row_specs(rows, segs, consts, tm, nctx):
    specs = [pl.BlockSpec((tm, w), lambda i, b=b: (i, b)) for _, w, b in rows]
    specs += [pl.BlockSpec((1,) + s.shape[1:], lambda i, n=s.ndim: (jnp.where(i >= nctx, 1, 0),) + (0,) * (n - 1))
              for s in segs]
    specs += [pl.BlockSpec(c.shape, lambda i, n=c.ndim: (0,) * n) for c in consts]
    return specs


def rowwise(fn, rows, segs, consts, outs, dm, name, tm=None):
    tm = tm or dm.tm
    nctx = dm.CTX // tm
    rows = [_rowspec(r) for r in rows]
    nr, ns, nc = len(rows), len(segs), len(consts)

    def body(*refs):
        rin = [r[...] for r in refs[:nr]]
        sin = [s[0] for s in refs[nr:nr + ns]]
        cin = [c[...] for c in refs[nr + ns:nr + ns + nc]]
        res = fn(*rin, *sin, *cin)
        for o_ref, v in zip(refs[nr + ns + nc:], res):
            o_ref[...] = v.astype(o_ref.dtype)

    res = pl.pallas_call(
        body, name=name, grid=(dm.T // tm,),
        in_specs=_row_specs(rows, segs, consts, tm, nctx),
        out_specs=[pl.BlockSpec((tm, w), lambda i: (i, 0)) for w, _ in outs],
        out_shape=[jax.ShapeDtypeStruct((dm.T, w), dt) for w, dt in outs],
        compiler_params=_cparams(("parallel",)),
    )(*[r[0] for r in rows], *segs, *consts)
    return res


def rowwise_vjp(fn, rows, segs, consts, cots, dm, name, tm=None, want=None, adds=None, narrow=(), into=None):
    tm = tm or dm.tm
    nctx = dm.CTX // tm
    rows = [_rowspec(r) for r in rows]
    cots = [_rowspec(r) for r in cots]
    adds = adds or {}
    nr, ns, nc, nct = len(rows), len(segs), len(consts), len(cots)
    want = want or [True] * nr
    widx = [k for k in range(nr) if want[k]]
    akeys = sorted(adds)

    def body(*refs):
        i = pl.program_id(0)
        rin = [r[...] for r in refs[:nr]]
        sin = [s[0] for s in refs[nr:nr + ns]]
        cin = [c[...] for c in refs[nr + ns:nr + ns + nc]]
        p = nr + ns + nc
        cot_refs = refs[p:p + nct]
        add_refs = dict(zip(akeys, refs[p + nct:p + nct + len(akeys)]))
        p = p + nct + len(akeys) + (1 if (into is not None and into[1] is not None) else 0)
        rg_refs = refs[p:p + len(widx)]
        sg_refs = refs[p + len(widx):p + len(widx) + ns]
        cg_refs = refs[p + len(widx) + ns:]
        res, vjp = jax.vjp(fn, *rin, *sin, *cin)
        g = vjp(tuple(cr[...].astype(o.dtype) for cr, o in zip(cot_refs, res)))
        for o_ref, k in zip(rg_refs, widx):
            v = g[k].astype(F32)
            if k in add_refs:
                v = v + add_refs[k][...]
            o_ref[...] = v.astype(o_ref.dtype)
        first_seg = jnp.logical_or(i == 0, i == nctx)
        for o_ref, v in zip(sg_refs, g[nr:nr + ns]):
            @pl.when(first_seg)
            def _(o_ref=o_ref, v=v):
                o_ref[0] = v.astype(F32)

            @pl.when(jnp.logical_not(first_seg))
            def _(o_ref=o_ref, v=v):
                o_ref[0] += v.astype(F32)
        for o_ref, v in zip(cg_refs, g[nr + ns:]):
            @pl.when(i == 0)
            def _(o_ref=o_ref, v=v):
                o_ref[...] = v.astype(F32)

            @pl.when(i > 0)
            def _(o_ref=o_ref, v=v):
                o_ref[...] += v.astype(F32)

    in_specs = _row_specs(rows, segs, consts, tm, nctx)
    in_specs += [pl.BlockSpec((tm, w), lambda i, b=b: (i, b)) for _, w, b in cots]
    in_specs += [pl.BlockSpec((tm, adds[k].shape[1]), lambda i: (i, 0)) for k in akeys]
    out_specs = [pl.BlockSpec((tm, rows[k][1]), lambda i: (i, 0)) for k in widx]
    out_shape = [jax.ShapeDtypeStruct((dm.T, rows[k][1]), MM_DTYPE if k in narrow else rows[k][0].dtype)
                 for k in widx]
    extra, aliases = [], {}
    if into is not None:
        ik, ibuf, ishape = into
        out_specs[widx.index(ik)] = pl.BlockSpec((tm, rows[ik][1]), lambda i, b=rows[ik][2]: (i, b))
        out_shape[widx.index(ik)] = jax.ShapeDtypeStruct(ishape, MM_DTYPE)
        if ibuf is not None:
            aliases = {len(in_specs): widx.index(ik)}
            in_specs = in_specs + [pl.BlockSpec(memory_space=pl.ANY)]
            extra = [ibuf]
    out_specs += [pl.BlockSpec((1,) + s.shape[1:], lambda i, n=s.ndim: (jnp.where(i >= nctx, 1, 0),) + (0,) * (n - 1))
                  for s in segs]
    out_shape += [jax.ShapeDtypeStruct(s.shape, F32) for s in segs]
    out_specs += [pl.BlockSpec(c.shape, lambda i, n=c.ndim: (0,) * n) for c in consts]
    out_shape += [jax.ShapeDtypeStruct(c.shape, F32) for c in consts]
    res = pl.pallas_call(
        body, name=name, grid=(dm.T // tm,), in_specs=in_specs, out_specs=out_specs, out_shape=out_shape,
        input_output_aliases=aliases, compiler_params=_cparams(("arbitrary",)),
    )(*[r[0] for r in rows], *segs, *consts, *[r[0] for r in cots], *[adds[k] for k in akeys], *extra)
    rg = [None] * nr
    for k, v in zip(widx, res[:len(widx)]):
        rg[k] = v
    return rg, list(res[len(widx):len(widx) + ns]), list(res[len(widx) + ns:])


def _rms(x, g):
    return x * lax.rsqrt(jnp.mean(x * x, axis=-1, keepdims=True) + EPS) * g


def _sigmoid(x):
    return jax.nn.sigmoid(x)


def pre_fn(x, shift, scale, g):
    return ((_rms(x, g) * (1.0 + scale) + shift).astype(MM_DTYPE),)


def mid_fn(x, y, gate, shift, scale, g_post, g_pre):
    x1 = x + gate * _rms(y.astype(F32), g_post)
    return x1, (_rms(x1, g_pre) * (1.0 + scale) + shift).astype(MM_DTYPE)


def post_fn(x1, y2, gate, g):
    return (x1 + gate * _rms(y2.astype(F32), g),)


def relu2_epi(acc):
    r = jnp.maximum(acc, 0.0)
    return r * r


def relu2_bwd_epi(dact, act):
    return dact * (2.0 * jnp.sqrt(act.astype(F32)))


def decay_fn(z, bd):
    zz = z.astype(F32) + bd
    ls = jnp.minimum(zz, 0.0) - jnp.log(1.0 + jnp.exp(jnp.minimum(zz, -zz)))
    la = ls / GLA_TAU
    gk = la.shape[1] // 2
    return la[:, :gk], la[:, gk:]


def glu_fn(ab):
    h = ab.shape[1] // 2
    return (ab[:, :h].astype(F32) * _sigmoid(ab[:, h:].astype(F32)),)


def glaout_fn(o_f, o_b, og, g):
    o = o_f + o_b
    dv = o.shape[1] // N_HEADS
    hs = []
    for h in range(N_HEADS):
        oh = o[:, h * dv:(h + 1) * dv]
        hs.append(oh * lax.rsqrt(jnp.mean(oh * oh, axis=-1, keepdims=True) + EPS) * g[:, h * dv:(h + 1) * dv])
    og = og.astype(F32)
    return ((jnp.concatenate(hs, axis=1) * (og * _sigmoid(og))).astype(MM_DTYPE),)


def convpost_fn(y, b_dw, g, b):
    y = y + b_dw
    mu = jnp.mean(y, axis=-1, keepdims=True)
    xc = y - mu
    yn = xc * lax.rsqrt(jnp.mean(xc * xc, axis=-1, keepdims=True) + EPS) * g + b
    return ((yn * _sigmoid(yn)).astype(MM_DTYPE),)


def poolpost_fn(pc, s):
    return ((pc.astype(F32) * s).astype(MM_DTYPE),)


def merge_fn(ya, yb, yc, mg, bg0, bg1, bg2):
    d = ya.shape[1]
    mg = mg.astype(F32)
    mixed = (_sigmoid(mg[:, :d] + bg0) * ya.astype(F32) + _sigmoid(mg[:, d:2 * d] + bg1) * yb.astype(F32)
             + _sigmoid(mg[:, 2 * d:] + bg2) * yc.astype(F32))
    return (mixed.astype(MM_DTYPE),)


def _split_dot(lmat, x, dims):
    hi = x.astype(MM_DTYPE)
    lo = x - hi.astype(F32)
    return _dot(lmat, hi, dims) + _dot(lmat, lo, dims)


def _gla_block_order(dm, rev):
    nctx, nb = dm.CTX // dm.TB, dm.T // dm.TB

    def blk(i):
        if not rev:
            return i
        return jnp.where(i < nctx, nctx - 1 - i, nb - 1 - (i - nctx))
    return blk, nb


def _gla_tri(rev):
    c = GLA_CHUNK
    t = lax.broadcasted_iota(jnp.int32, (c, c), 0)
    s = lax.broadcasted_iota(jnp.int32, (c, c), 1)
    return (s >= t) if rev else (s <= t)


def _gla_cumsum(la, tri):
    lmat = tri.astype(MM_DTYPE)
    return lmat, _split_dot(lmat, la, NN), jnp.sum(la, axis=0, keepdims=True)


def _gla_chunk_terms(q, k, b, bend, tri, scale):
    eb = jnp.exp(b)
    enb = jnp.exp(-b)
    ee = jnp.exp(bend - b)
    qi = q * scale * eb
    ki = k * enb
    kend = k * ee
    att = jnp.where(tri, _dot(qi, ki, NT), 0.0)
    return eb, enb, ee, qi, ki, kend, att


def gla_fwd(P, la, rev, dm, name):
    c, tb, h_, dk, dv, d = GLA_CHUNK, dm.TB, N_HEADS, dm.DK, dm.DV, dm.D
    cpb = tb // c
    blk, nb = _gla_block_order(dm, rev)
    gk, gv = h_ * dk, h_ * dv
    qb, kb, vb, lb = (5 * d) // gk, (5 * d + d // 2) // gk, (4 * d) // gv, (1 if rev else 0)
    scale = dk ** -0.5
    order = list(range(cpb))[::-1] if rev else list(range(cpb))

    def body(q_ref, k_ref, v_ref, la_ref, o_ref, s_ref, st):
        @pl.when(pl.program_id(0) == 0)
        def _():
            st[...] = jnp.zeros_like(st)
        tri = _gla_tri(rev)
        terms = {}
        for n, ci in enumerate(order):
            r = pl.ds(ci * c, c)
            _, b_all, bend_all = _gla_cumsum(la_ref[r, :], tri)
            for hh in range(h_):
                ck, cv = pl.ds(hh * dk, dk), pl.ds(hh * dv, dv)
                hs = slice(hh * dk, (hh + 1) * dk)
                v = v_ref[r, cv]
                _, _, _, qi, _, kend, att = _gla_chunk_terms(
                    q_ref[r, ck].astype(F32), k_ref[r, ck].astype(F32), b_all[:, hs], bend_all[:, hs], tri, scale)
                terms[n, hh] = (_dot(att, v), qi.astype(MM_DTYPE), jnp.exp(bend_all[:, hs]), _dot(v, kend, TN))
        for n, ci in enumerate(order):
            r = pl.ds(ci * c, c)
            for hh in range(h_):
                intra, qi, gam, dstate = terms[n, hh]
                s_in = st[hh]
                o_ref[r, pl.ds(hh * dv, dv)] = intra + _dot(qi, s_in, NT)
                s_ref[n, hh] = s_in
                st[hh] = gam * s_in + dstate

    return pl.pallas_call(
        body, name=name, grid=(nb,),
        in_specs=[pl.BlockSpec((tb, gk), lambda i: (blk(i), qb)),
                  pl.BlockSpec((tb, gk), lambda i: (blk(i), kb)),
                  pl.BlockSpec((tb, gv), lambda i: (blk(i), vb)),
                  pl.BlockSpec((tb, gk), lambda i: (blk(i), lb))],
        out_specs=[pl.BlockSpec((tb, gv), lambda i: (blk(i), 0)),
                   pl.BlockSpec((cpb, h_, dv, dk), lambda i: (i, 0, 0, 0))],
        out_shape=[jax.ShapeDtypeStruct((dm.T, gv), F32),
                   jax.ShapeDtypeStruct((dm.T // c, h_, dv, dk), F32)],
        scratch_shapes=[pltpu.VMEM((h_, dv, dk), F32)],
        compiler_params=_cparams(("arbitrary",)),
    )(P, P, P, la)


def gla_bwd(P, la, do, states, rev, dm, name, prev=None, into=None):
    c, tb, h_, dk, dv, d = GLA_CHUNK, dm.TB, N_HEADS, dm.DK, dm.DV, dm.D
    cpb = tb // c
    blk, nb = _gla_block_order(dm, rev)
    gk, gv = h_ * dk, h_ * dv
    qb, kb, vb, lb = (5 * d) // gk, (5 * d + d // 2) // gk, (4 * d) // gv, (1 if rev else 0)
    scale = dk ** -0.5
    order = list(range(cpb))[::-1] if rev else list(range(cpb))

    fused = prev is not None

    def body(q_ref, k_ref, v_ref, la_ref, do_ref, s_ref, *rest):
        if fused:
            pq_ref, pk_ref, pv_ref, _, w_ref, dla_ref, dst = rest
        else:
            dq_ref, dk_ref, dv_ref, dla_ref, dst = rest

        def put(kind, r, cols, val):
            if not fused:
                {'q': dq_ref, 'k': dk_ref, 'v': dv_ref}[kind][r, cols] = val
                return
            p_ref, off = {'q': (pq_ref, gv), 'k': (pk_ref, gv + gk), 'v': (pv_ref, 0)}[kind]
            w_ref[r, pl.ds(off + cols.start, cols.size)] = (val + p_ref[r, cols]).astype(w_ref.dtype)

        @pl.when(pl.program_id(0) == 0)
        def _():
            dst[...] = jnp.zeros_like(dst)
        tri = _gla_tri(rev)
        for n in range(cpb - 1, -1, -1):
            r = pl.ds(order[n] * c, c)
            for hh in range(h_):
                ck, cv = pl.ds(hh * dk, dk), pl.ds(hh * dv, dv)
                q = q_ref[r, ck].astype(F32)
                k = k_ref[r, ck].astype(F32)
                v = v_ref[r, cv]
                lmat, b, bend = _gla_cumsum(la_ref[r, ck], tri)
                eb, enb, ee, qi, ki, kend, att = _gla_chunk_terms(q, k, b, bend, tri, scale)
                s_in = s_ref[n, hh]
                ds_out = dst[hh]
                dob = do_ref[r, cv]
                datt = jnp.where(tri, _dot(dob, v, NT), 0.0)
                dqi = _dot(datt, ki) + _dot(dob, s_in)
                dki = _dot(datt, qi, TN)
                put('v', r, cv, _dot(att, dob, TN) + _dot(kend, ds_out, NT))
                dkend = _dot(v, ds_out)
                gam = jnp.exp(bend)
                dgam = jnp.sum(ds_out * s_in, axis=0, keepdims=True)
                dst[hh] = gam * ds_out + _dot(dob, qi, TN)
                put('q', r, ck, dqi * (scale * eb))
                put('k', r, ck, dki * enb + dkend * ee)
                db = dqi * qi - dki * ki - dkend * kend
                dbend = jnp.sum(dkend * kend, axis=0, keepdims=True) + dgam * gam
                dla_ref[r, ck] = _split_dot(lmat, db, TN) + dbend

    def bi(j):
        return blk(nb - 1 - j)

    in_specs = [
        pl.BlockSpec((tb, gk), lambda j: (bi(j), qb)),
        pl.BlockSpec((tb, gk), lambda j: (bi(j), kb)),
        pl.BlockSpec((tb, gv), lambda j: (bi(j), vb)),
        pl.BlockSpec((tb, gk), lambda j: (bi(j), lb)),
        pl.BlockSpec((tb, gv), lambda j: (bi(j), 0)),
        pl.BlockSpec((cpb, h_, dv, dk), lambda j: (nb - 1 - j, 0, 0, 0)),
    ]
    small = pl.BlockSpec((tb, gk), lambda j: (bi(j), 0))
    wide = pl.BlockSpec((tb, gv), lambda j: (bi(j), 0))
    if not fused:
        return pl.pallas_call(
            body, name=name, grid=(nb,), in_specs=in_specs, out_specs=[small, small, wide, small],
            out_shape=[jax.ShapeDtypeStruct((dm.T, gk), F32), jax.ShapeDtypeStruct((dm.T, gk), F32),
                       jax.ShapeDtypeStruct((dm.T, gv), F32), jax.ShapeDtypeStruct((dm.T, gk), F32)],
            scratch_shapes=[pltpu.VMEM((h_, dv, dk), F32)],
            compiler_params=_cparams(("arbitrary",)),
        )(P, P, P, la, do, states)
    return pl.pallas_call(
        body, name=name, grid=(nb,),
        in_specs=in_specs + [small, small, wide, pl.BlockSpec(memory_space=pl.ANY)],
        out_specs=[pl.BlockSpec((tb, 2 * gv), lambda j: (bi(j), vb // 2)), small],
        out_shape=[jax.ShapeDtypeStruct(into.shape, into.dtype), jax.ShapeDtypeStruct((dm.T, gk), F32)],
        input_output_aliases={9: 0},
        scratch_shapes=[pltpu.VMEM((h_, dv, dk), F32)],
        compiler_params=_cparams(("arbitrary",)),
    )(P, P, P, la, do, states, *prev, into)


def _pos(n, period):
    t = lax.broadcasted_iota(jnp.int32, (n, 1), 0)
    if period & (period - 1) == 0:
        return jnp.bitwise_and(t, period - 1)
    return lax.rem(t, period)


def _conv_segments(dm):
    return [(0, dm.CTX, dm.CTX), (dm.CTX, dm.SEQ, GRID_W)]


def conv_fwd(u, w, dm, name):
    kw, cw = w.shape
    segs = _conv_segments(dm)

    def body(u_ref, w_ref, y_ref):
        for r0, n, per in segs:
            useg = u_ref[r0:r0 + n, :]
            p = _pos(n, per)
            acc = jnp.zeros_like(useg)
            for kk in range(kw):
                d = kk - kw // 2
                sh = useg if d == 0 else pltpu.roll(useg, (-d) % n, 0)
                ok = jnp.logical_and(p + d >= 0, p + d < per)
                acc = acc + jnp.where(ok, sh, 0.0) * w_ref[kk:kk + 1, :]
            y_ref[r0:r0 + n, :] = acc

    return pl.pallas_call(
        body, name=name, grid=(cw // LANES,),
        in_specs=[pl.BlockSpec((dm.T, LANES), lambda j: (0, j)), pl.BlockSpec((kw, LANES), lambda j: (0, j))],
        out_specs=pl.BlockSpec((dm.T, LANES), lambda j: (0, j)),
        out_shape=jax.ShapeDtypeStruct((dm.T, cw), F32),
        compiler_params=_cparams(("parallel",)),
    )(u, w)


def conv_bwd(u, w, dy, dm, name):
    kw, cw = w.shape
    segs = _conv_segments(dm)

    def body(u_ref, w_ref, dy_ref, du_ref, dw_ref):
        dws = [jnp.zeros((1, LANES), F32)] * kw
        for r0, n, per in segs:
            useg = u_ref[r0:r0 + n, :]
            dyseg = dy_ref[r0:r0 + n, :]
            p = _pos(n, per)
            acc = jnp.zeros_like(useg)
            for kk in range(kw):
                d = kk - kw // 2
                shu = useg if d == 0 else pltpu.roll(useg, (-d) % n, 0)
                okf = jnp.logical_and(p + d >= 0, p + d < per)
                dws[kk] = dws[kk] + jnp.sum(jnp.where(okf, shu, 0.0) * dyseg, axis=0, keepdims=True)
                shd = dyseg if d == 0 else pltpu.roll(dyseg, d % n, 0)
                okb = jnp.logical_and(p - d >= 0, p - d < per)
                acc = acc + jnp.where(okb, shd, 0.0) * w_ref[kk:kk + 1, :]
            du_ref[r0:r0 + n, :] = acc
        for kk in range(kw):
            dw_ref[kk:kk + 1, :] = dws[kk]

    return pl.pallas_call(
        body, name=name, grid=(cw // LANES,),
        in_specs=[pl.BlockSpec((dm.T, LANES), lambda j: (0, j)), pl.BlockSpec((kw, LANES), lambda j: (0, j)),
                  pl.BlockSpec((dm.T, LANES), lambda j: (0, j))],
        out_specs=[pl.BlockSpec((dm.T, LANES), lambda j: (0, j)), pl.BlockSpec((kw, LANES), lambda j: (0, j))],
        out_shape=[jax.ShapeDtypeStruct((dm.T, cw), F32), jax.ShapeDtypeStruct((kw, cw), F32)],
        compiler_params=_cparams(("parallel",)),
    )(u, w, dy)


def pool_mix(u, transpose, dm, name, into=None):
    u, uw, ublk = _rowspec(u)
    gc = dm.GC
    ng = len(POOL_WINDOWS)
    rows = dm.SEQ // GRID_W
    segs = [(0, dm.CTX, 1, dm.CTX), (dm.CTX, dm.SEQ, GRID_W, rows)]

    def one_group(u_ref, o_ref, win):
        left = win // 2
        right = win - 1 - left
        for r0, n, stride, length in segs:
            useg = u_ref[r0:r0 + n, :].astype(F32)
            t = lax.broadcasted_iota(jnp.int32, (n, 1), 0)
            p = t if stride == 1 else jnp.right_shift(t, stride.bit_length() - 1)
            cnt = (jnp.minimum(p + right + 1, length) - jnp.maximum(p - left, 0)).astype(F32)
            src = useg / cnt if transpose else useg
            acc = jnp.zeros_like(useg)
            for d in range(-left, right + 1):
                dd = -d if transpose else d
                sh = src if d == 0 else pltpu.roll(src, (-dd * stride) % n, 0)
                ok = jnp.logical_and(p + dd >= 0, p + dd < length)
                acc = acc + jnp.where(ok, sh, 0.0)
            o_ref[r0:r0 + n, :] = ((acc - useg) if transpose else (acc / cnt - useg)).astype(o_ref.dtype)

    def body(u_ref, *rest):
        o_ref = rest[-1]
        g = pl.program_id(0)
        for gi, win in enumerate(POOL_WINDOWS):
            @pl.when(g == gi)
            def _(win=win):
                one_group(u_ref, o_ref, win)

    base = ublk * (uw // gc)
    if into is None:
        obase, out_shape, more, extra, aliases = 0, jax.ShapeDtypeStruct((dm.T, ng * gc), F32), [], [], {}
    else:
        buf, oblk = into
        obase, out_shape = oblk * ng, jax.ShapeDtypeStruct(buf.shape, buf.dtype)
        more, extra, aliases = [pl.BlockSpec(memory_space=pl.ANY)], [buf], {1: 0}
    return pl.pallas_call(
        body, name=name, grid=(ng,),
        in_specs=[pl.BlockSpec((dm.T, gc), lambda g: (0, base + g))] + more,
        out_specs=pl.BlockSpec((dm.T, gc), lambda g: (0, obase + g)),
        out_shape=out_shape, input_output_aliases=aliases,
        compiler_params=_cparams(("parallel",)),
    )(u, *extra)


def loss_head(x2, target, dm, name):
    tm, d = dm.tm, dm.D
    nctx = dm.CTX // tm

    def body(x_ref, t_ref, dx_ref, l_ref):
        i = pl.program_id(0)

        @pl.when(i == 0)
        def _():
            l_ref[...] = jnp.zeros_like(l_ref)

        @pl.when(i < nctx)
        def _():
            dx_ref[...] = jnp.zeros_like(dx_ref)

        @pl.when(i >= nctx)
        def _():
            e = x_ref[...] - t_ref[...]
            dx_ref[...] = e / d
            l_ref[...] += jnp.full(l_ref.shape, 0.5 * jnp.sum(jnp.mean(e * e, axis=-1)), F32)

    return pl.pallas_call(
        body, name=name, grid=(dm.T // tm,),
        in_specs=[pl.BlockSpec((tm, d), lambda i: (i, 0)),
                  pl.BlockSpec((tm, d), lambda i: (jnp.maximum(i - nctx, 0), 0))],
        out_specs=[pl.BlockSpec((tm, d), lambda i: (i, 0)), pl.BlockSpec((8, LANES), lambda i: (0, 0))],
        out_shape=[jax.ShapeDtypeStruct((dm.T, d), F32), jax.ShapeDtypeStruct((8, LANES), F32)],
        compiler_params=_cparams(("arbitrary",)),
    )(x2, target)


def adamw(w, g, m, v, name):
    r, c = w.shape
    tr = _tile(r, tuple(t for t in (512, 256, 128, 64, 32, 16, 8) if t * c * 4 <= (1 << 20)) or (8,))

    def body(w_ref, g_ref, m_ref, v_ref, d_ref, mo_ref, vo_ref):
        gg = g_ref[...]
        mm = ADAM_B1 * m_ref[...] + (1.0 - ADAM_B1) * gg
        vv = ADAM_B2 * v_ref[...] + (1.0 - ADAM_B2) * (gg * gg)
        m_hat = mm / (1.0 - ADAM_B1 ** ADAM_STEP)
        v_hat = vv / (1.0 - ADAM_B2 ** ADAM_STEP)
        d_ref[...] = -ADAM_LR * (m_hat / (jnp.sqrt(v_hat) + ADAM_EPS) + ADAM_WD * w_ref[...])
        mo_ref[...] = mm
        vo_ref[...] = vv

    spec = pl.BlockSpec((tr, c), lambda i: (i, 0))
    return pl.pallas_call(
        body, name=name, grid=(r // tr,), in_specs=[spec] * 4, out_specs=[spec] * 3,
        out_shape=[jax.ShapeDtypeStruct((r, c), F32)] * 3,
        compiler_params=_cparams(("parallel",)),
    )(w, g, m, v)


def pair_add(g, r1, cidx, name):
    ng, r_, n_ = r1.shape
    tr = _tile(r_, tuple(t for t in (1024, 512, 256, 128, 64, 32, 16) if t * n_ * 4 <= (2 << 20)))

    def body(s_ref, g_ref, r_ref, o_ref):
        o_ref[...] = (g_ref[...].astype(F32) + r_ref[...].astype(F32)).astype(o_ref.dtype)

    return pl.pallas_call(
        body, name=name,
        grid_spec=pltpu.PrefetchScalarGridSpec(
            num_scalar_prefetch=1, grid=(ng, r_ // tr),
            in_specs=[pl.BlockSpec((None, tr, n_), lambda k, i, s: (2 * k + s[0], i, 0)),
                      pl.BlockSpec((None, tr, n_), lambda k, i, s: (k, i, 0))],
            out_specs=pl.BlockSpec((None, tr, n_), lambda k, i, s: (k, i, 0))),
        out_shape=jax.ShapeDtypeStruct((ng, r_, n_), g.dtype),
        compiler_params=_cparams(("parallel", "parallel")),
    )(cidx, g, r1)


def chip_add(h, r2, axis, where, name, slab=True):
    _, kl, nl = r2.shape
    tr = _tile(kl, tuple(t for t in (1024, 512, 256, 128, 64, 32, 16) if t * nl * 4 <= (1 << 20)))
    nrb = kl // tr

    def body(s_ref, h_ref, r_ref, o_ref):
        acc = h_ref[...].astype(F32)
        for k in range(r2.shape[0]):
            acc = acc + r_ref[k].astype(F32)
        o_ref[...] = acc

    h_map = (lambda i, s: (s[0] * nrb + i, 0)) if axis == 0 else (lambda i, s: (i, s[0]))
    if slab:
        out_spec = pl.BlockSpec((None, tr, nl), lambda i, s: (s[1], i, 0))
        out_shape = jax.ShapeDtypeStruct((2, kl, nl), F32)
    else:
        out_spec = pl.BlockSpec((tr, nl), lambda i, s: (i, 0))
        out_shape = jax.ShapeDtypeStruct((kl, nl), F32)
    return pl.pallas_call(
        body, name=name,
        grid_spec=pltpu.PrefetchScalarGridSpec(
            num_scalar_prefetch=1, grid=(nrb,),
            in_specs=[pl.BlockSpec((tr, nl), h_map),
                      pl.BlockSpec((r2.shape[0], tr, nl), lambda i, s: (0, i, 0))],
            out_specs=out_spec),
        out_shape=out_shape,
        compiler_params=_cparams(("parallel",)),
    )(where, h, r2)


def adamw_layers(w, m, v, terms, name, layer=None, prev=None):
    _, a_, b_ = w.shape
    tr = _tile(a_, tuple(t for t in (512, 256, 128, 64, 32) if t * b_ * 4 <= (1 << 20)))
    by_cols = tr == a_ and a_ * b_ * 4 > (1 << 20)
    blk = (a_, LANES) if by_cols else (tr, b_)
    steps = b_ // LANES if by_cols else a_ // tr
    at = (lambda i: (0, i)) if by_cols else (lambda i: (i, 0))
    layers = (0, 1) if layer is None else (layer,)
    counts = [len(terms[l]) for l in layers]
    nprev = 0 if prev is None else 4

    def update(g, w_ref, m_ref, v_ref, g_ref, d_ref, mo_ref, vo_ref):
        mm = ADAM_B1 * m_ref[...] + (1.0 - ADAM_B1) * g
        vv = ADAM_B2 * v_ref[...] + (1.0 - ADAM_B2) * (g * g)
        m_hat = mm / (1.0 - ADAM_B1 ** ADAM_STEP)
        v_hat = vv / (1.0 - ADAM_B2 ** ADAM_STEP)
        g_ref[...] = g
        d_ref[...] = -ADAM_LR * (m_hat / (jnp.sqrt(v_hat) + ADAM_EPS) + ADAM_WD * w_ref[...])
        mo_ref[...] = mm
        vo_ref[...] = vv

    def total(refs):
        g = refs[0][...]
        for r in refs[1:]:
            g = g + r[...]
        return g

    def body(w_ref, m_ref, v_ref, *rest):
        t_refs, outs = rest[:sum(counts)], rest[-4:]
        if len(layers) == 1:
            update(total(t_refs), w_ref, m_ref, v_ref, *outs)
            return
        which = pl.program_id(0)

        @pl.when(which == 0)
        def _():
            update(total(t_refs[:counts[0]]), w_ref, m_ref, v_ref, *outs)

        @pl.when(which == 1)
        def _():
            update(total(t_refs[counts[0]:]), w_ref, m_ref, v_ref, *outs)

    if len(layers) == 1:
        stacked = pl.BlockSpec((None,) + blk, lambda l, i: (layers[0],) + at(i))
        t_specs = [pl.BlockSpec(blk, lambda l, i: at(i))] * counts[0]
    else:
        stacked = pl.BlockSpec((None,) + blk, lambda l, i: (l,) + at(i))
        t_specs = ([pl.BlockSpec(blk, lambda l, i: at(i * (1 - l)))] * counts[0]
                   + [pl.BlockSpec(blk, lambda l, i: at(i * l))] * counts[1])
    nin = 3 + sum(counts)
    return pl.pallas_call(
        body, name=name, grid=(len(layers), steps),
        in_specs=[stacked] * 3 + t_specs + [pl.BlockSpec(memory_space=pl.ANY)] * nprev,
        out_specs=[stacked] * 4, out_shape=[jax.ShapeDtypeStruct(w.shape, F32)] * 4,
        input_output_aliases={nin + j: j for j in range(nprev)},
        compiler_params=_cparams(("arbitrary", "arbitrary")),
    )(w, m, v, *[t for l in layers for t in terms[l]], *(prev or ()))


MESH = pl.DeviceIdType.MESH
ANY = pl.BlockSpec(memory_space=pl.ANY)
HBM = pl.BlockSpec(memory_space=pltpu.HBM)
SEM = pl.BlockSpec(memory_space=pltpu.SEMAPHORE)
EFFECT = pltpu.SideEffectType.DATAFLOW_SIDE_EFFECTING


def _place():
    return lax.axis_index("x"), lax.axis_index("y"), lax.axis_index("c")


def _peers(x, y):
    return [(1 - x, y), (x, 1 - y), (1 - x, 1 - y)]


def _rcopy(src, dst, ssem, rsem, dev):
    return pltpu.make_async_remote_copy(src_ref=src, dst_ref=dst, send_sem=ssem, recv_sem=rsem,
                                        device_id=dev, device_id_type=MESH)


def _gathered_shape(src, kind):
    h, a_, b_ = src.shape
    return (h, a_, 4 * b_) if kind == 'col' else (4, h, a_, b_)


def _win(ref, kind, ch, width):
    return ref.at[:, :, pl.ds(ch * width, width)] if kind == 'col' else ref.at[ch]


def _rect(ref, kind, half, ch, width):
    return ref.at[half, :, pl.ds(ch * width, width)] if kind == 'col' else ref.at[ch, half]


def _gather_plan(kinds, widths):
    def plan(src, land, x, y, c):
        chip = 2 * x + y
        out = []
        for n in range(len(src)):
            for px, py in _peers(x, y):
                out.append((src[n].at[c], _rect(land[n], kinds[n], c, chip, widths[n]), (px, py, c),
                            _rect(land[n], kinds[n], c, 2 * px + py, widths[n])))
            mine = _win(land[n], kinds[n], chip, widths[n])
            out.append((src[n], mine, (x, y, 1 - c), mine))
        return out
    return plan


def forward_halves(lands, kinds, name):
    nw = len(lands)
    widths = [t.shape[-1] // 4 if k == 'col' else t.shape[-1] for t, k in zip(lands, kinds)]

    def body(*refs):
        o = refs[nw:2 * nw]
        ssem, rsem = refs[2 * nw:]
        x, y, c = _place()
        sib = (x, y, 1 - c)
        pidx = [2 * px + py for px, py in _peers(x, y)]
        cps = [_rcopy(_rect(o[n], kinds[n], c, pidx[j], widths[n]), _rect(o[n], kinds[n], c, pidx[j], widths[n]),
                      ssem.at[3 * n + j], rsem.at[3 * n + j], sib) for n in range(nw) for j in range(3)]
        for cp in cps:
            cp.start()
        for n in range(nw):
            for j in range(3):
                cps[3 * n + j].wait_send()
                _rcopy(_rect(o[n], kinds[n], 1 - c, pidx[j], widths[n]), _rect(o[n], kinds[n], 1 - c, pidx[j], widths[n]),
                       ssem.at[3 * n + j], rsem.at[3 * n + j], sib).wait_recv()

    return pl.pallas_call(
        body, name=name, in_specs=[ANY] * nw, out_specs=[ANY] * nw,
        out_shape=[jax.ShapeDtypeStruct(t.shape, t.dtype) for t in lands],
        input_output_aliases={n: n for n in range(nw)},
        scratch_shapes=[pltpu.SemaphoreType.DMA((3 * nw,)), pltpu.SemaphoreType.DMA((3 * nw,))],
    )(*lands)


def _scatter_plan(axes, widths):
    def plan(src, land, x, y, c):
        out = []
        for n in range(len(src)):
            for k, (px, py) in enumerate(_peers(x, y)):
                ch = 2 * px + py
                view = (src[n].at[:, pl.ds(ch * widths[n], widths[n])] if axes[n] == 1
                        else src[n].at[pl.ds(ch * widths[n], widths[n]), :])
                out.append((view, land[n].at[k], (px, py, c), land[n].at[k]))
        return out
    return plan


def _exchange_plan(kinds):
    def plan(src, land, x, y, c):
        out = []
        for n in range(len(src)):
            w = land[n].shape[2]
            for j, (px, py) in enumerate(_peers(x, y)):
                ch = 2 * px + py
                view = src[n].at[:, pl.ds(ch * w, w)] if kinds[n] == 'col' else src[n].at[ch]
                out.append((view, land[n].at[j], (px, py, c), land[n].at[j]))
        return out
    return plan


def start_copies(srcs, lands, plan, ncopies, after, name):
    ns, nl = len(srcs), len(lands)

    def body(*refs):
        src, land = refs[:ns], refs[ns:ns + nl]
        ssem, rsem = refs[ns + nl + 1], refs[ns + nl + 2]
        token = refs[-1]
        x, y, c = _place()
        for k, (sv, dv, dev, _) in enumerate(plan(src, land, x, y, c)):
            _rcopy(sv, dv, ssem.at[k], rsem.at[k], dev).start()
        token[...] = jnp.zeros_like(token)

    hbm = lambda t: pltpu.HBM(t.shape, t.dtype)
    res = pl.pallas_call(
        body, name=name,
        out_shape=(pltpu.SemaphoreType.DMA((ncopies,)), pltpu.SemaphoreType.DMA((ncopies,)),
                   *[hbm(t) for t in srcs], *[hbm(t) for t in lands], jax.ShapeDtypeStruct((8, LANES), F32)),
        in_specs=[HBM] * (ns + nl) + [ANY],
        out_specs=(SEM, SEM, *[HBM] * (ns + nl), pl.BlockSpec(memory_space=pltpu.VMEM)),
        input_output_aliases={k: 2 + k for k in range(ns + nl)},
        compiler_params=pltpu.CompilerParams(has_side_effects=EFFECT),
    )(*[pltpu.with_memory_space_constraint(t, pltpu.HBM) for t in list(srcs) + list(lands)], after)
    return res[0], res[1], list(res[2:2 + ns]), list(res[2 + ns:2 + ns + nl]), res[-1]


def wait_copies(ssem, rsem, srcs, lands, plan, after, name):
    ns, nl = len(srcs), len(lands)

    def body(*refs):
        src, land = refs[:ns], refs[ns:ns + nl]
        ss, rs = refs[ns + nl], refs[ns + nl + 1]
        x, y, c = _place()
        for k, (sv, dv, dev, mine) in enumerate(plan(src, land, x, y, c)):
            cp = _rcopy(sv, mine, ss.at[k], rs.at[k], dev)
            cp.wait_send()
            cp.wait_recv()

    hbm = lambda t: pltpu.HBM(t.shape, t.dtype)
    res = pl.pallas_call(
        body, name=name,
        out_shape=(*[hbm(t) for t in srcs], *[hbm(t) for t in lands]),
        in_specs=[HBM] * (ns + nl) + [SEM, SEM, ANY], out_specs=tuple([HBM] * (ns + nl)),
        input_output_aliases={k: k for k in range(ns + nl)},
        compiler_params=pltpu.CompilerParams(has_side_effects=EFFECT),
    )(*srcs, *lands, ssem, rsem, after)
    return list(res[:ns]), list(res[ns:])


def pair_swap_halves(gs, kinds, name):
    nw = len(gs)

    def other(ref, kind, half):
        return ref.at[half] if kind == 'col' else ref.at[:, half]

    def body(*refs):
        g, o = refs[:nw], refs[nw:2 * nw]
        ssem, rsem = refs[2 * nw:]
        x, y, c = _place()
        cps = [_rcopy(other(g[n], kinds[n], 1 - c), o[n], ssem.at[n], rsem.at[n], (x, y, 1 - c)) for n in range(nw)]
        for cp in cps:
            cp.start()
        for cp in cps:
            cp.wait()

    return pl.pallas_call(
        body, name=name, in_specs=[ANY] * nw, out_specs=[ANY] * nw,
        out_shape=[jax.ShapeDtypeStruct(g.shape[1:] if k == 'col' else (g.shape[0],) + g.shape[2:], g.dtype)
                   for g, k in zip(gs, kinds)],
        scratch_shapes=[pltpu.SemaphoreType.DMA((nw,)), pltpu.SemaphoreType.DMA((nw,))],
    )(*gs)


def chip_broadcast(h, name):
    def body(h_ref, o_ref, ssem, rsem):
        x, y, c = _place()
        cps = [_rcopy(h_ref, o_ref.at[j], ssem.at[j], rsem.at[j], (px, py, c)) for j, (px, py) in enumerate(_peers(x, y))]
        for cp in cps:
            cp.start()
        for cp in cps:
            cp.wait()

    return pl.pallas_call(
        body, name=name, in_specs=[ANY], out_specs=ANY,
        out_shape=jax.ShapeDtypeStruct((3,) + h.shape, h.dtype),
        scratch_shapes=[pltpu.SemaphoreType.DMA((3,)), pltpu.SemaphoreType.DMA((3,))],
    )(h)


def quad_sum(h, r, cidx, name):
    r_, c_ = h.shape
    tr = _tile(r_, (512, 256, 128, 64, 32, 16, 8))

    def body(s_ref, h_ref, r_ref, o_ref):
        o_ref[...] = (h_ref[...] + r_ref[2]) + (r_ref[0] + r_ref[1])

    return pl.pallas_call(
        body, name=name,
        grid_spec=pltpu.PrefetchScalarGridSpec(
            num_scalar_prefetch=1, grid=(r_ // tr,),
            in_specs=[pl.BlockSpec((tr, c_), lambda i, s: (i, 0)), pl.BlockSpec((3, tr, c_), lambda i, s: (0, i, 0))],
            out_specs=pl.BlockSpec((None, tr, c_), lambda i, s: (s[0], i, 0))),
        out_shape=jax.ShapeDtypeStruct((2, r_, c_), F32),
        compiler_params=_cparams(("parallel",)),
    )(cidx, h, r)


def pair_join_layers(fs, name):
    nw = len(fs)

    def body(*refs):
        o = refs[nw:2 * nw]
        ssem, rsem = refs[2 * nw:]
        x, y, c = _place()
        sib = (x, y, 1 - c)
        cps = [_rcopy(o[n].at[c], o[n].at[c], ssem.at[n], rsem.at[n], sib) for n in range(nw)]
        for cp in cps:
            cp.start()
        for n in range(nw):
            cps[n].wait_send()
            _rcopy(o[n].at[1 - c], o[n].at[1 - c], ssem.at[n], rsem.at[n], sib).wait_recv()

    return pl.pallas_call(
        body, name=name, in_specs=[ANY] * nw, out_specs=[ANY] * nw,
        out_shape=[jax.ShapeDtypeStruct(f.shape, f.dtype) for f in fs],
        input_output_aliases={n: n for n in range(nw)},
        scratch_shapes=[pltpu.SemaphoreType.DMA((nw,)), pltpu.SemaphoreType.DMA((nw,))],
    )(*fs)


def _flatten_pad(parts, dtype):
    flat = jnp.concatenate([p.reshape(-1).astype(dtype) for p in parts])
    q = 512 * LANES
    n = -(-flat.shape[0] // q) * q
    return jnp.pad(flat, (0, n - flat.shape[0])).reshape(n // LANES, LANES)


def _lane_pad(n):
    return -(-n // LANES) * LANES


def _in_proj_layout(d):
    gk, gv, cw, pw = d // 2, d, d // 2, d // 2
    own = [('q', gk), ('k', gk), ('v', gv), ('og', gv), ('lrf', GLA_LR), ('lrb', GLA_LR), ('ga', cw), ('gb', cw),
           ('pu', pw), ('mg', 3 * d)]
    padded = [('mg', 3 * d), ('og', gv), ('v', gv), ('q', gk), ('k', gk), ('ga', cw), ('gb', cw), ('pu', pw),
              ('lrf', GLA_LR), ('lrb', GLA_LR), ('pad', d // 2 - 2 * GLA_LR)]
    return own, padded


def _row_pieces(src, lo, hi, wl, wlp):
    out = []
    for k in range(4):
        s0, s1 = max(lo, k * wl), min(hi, (k + 1) * wl)
        if s0 < s1:
            out.append(src[k * wlp + s0 - k * wl:k * wlp + s1 - k * wl])
    return out


def _proj_runs(d):
    own, padded = _in_proj_layout(d)
    oat, start = {}, 0
    for n, wd in own:
        oat[n] = start
        start += wd
    runs, start = [], 0
    for n, wd in padded:
        if n != 'pad':
            if runs and runs[-1][0] + runs[-1][2] == oat[n] and runs[-1][1] + runs[-1][2] == start:
                runs[-1] = (runs[-1][0], runs[-1][1], runs[-1][2] + wd)
            else:
                runs.append((oat[n], start, wd))
        start += wd
    return runs, start


def _w_in_t_to_proj(g, d, wl, wlp):
    runs, total = _proj_runs(d)
    parts, at = [], 0
    for o0, p0, wd in runs:
        if p0 > at:
            parts.append(jnp.zeros((p0 - at, g.shape[1]), g.dtype))
        parts += _row_pieces(g, o0, o0 + wd, wl, wlp)
        at = p0 + wd
    if total > at:
        parts.append(jnp.zeros((total - at, g.shape[1]), g.dtype))
    return jnp.concatenate(parts, axis=0)


def _proj_to_w_in_t(gp, d, wl, wlp):
    runs, _ = _proj_runs(d)
    runs = sorted(runs)
    parts = []
    for k in range(4):
        for o0, p0, wd in runs:
            s0, s1 = max(o0, k * wl), min(o0 + wd, (k + 1) * wl)
            if s0 < s1:
                parts.append(gp[p0 + s0 - o0:p0 + s1 - o0])
        parts.append(jnp.zeros((wlp - wl, gp.shape[1]), gp.dtype))
    return jnp.concatenate(parts, axis=0)


def _silu_grad(z):
    s = jax.nn.sigmoid(z)
    return s + z * s * (1.0 - s)


def kernel(x, c, ctx, c_ctx, w_ada, b_ada, g_pre_mix, g_post_mix, g_pre_mlp, g_post_mlp, w_in, w_decay, b_decay, g_gla, w_gla_o, w_dw, b_dw, g_conv_ln, b_conv_ln, w_conv_o, w_pool_g, s_pool, w_pool_o, b_gate, w_out, w_mlp1, w_mlp2, loss_target, m_c_ctx, m_w_ada, m_b_ada, m_g_pre_mix, m_g_post_mix, m_g_pre_mlp, m_g_post_mlp, m_w_in, m_w_decay, m_b_decay, m_g_gla, m_w_gla_o, m_w_dw, m_b_dw, m_g_conv_ln, m_b_conv_ln, m_w_conv_o, m_w_pool_g, m_s_pool, m_w_pool_o, m_b_gate, m_w_out, m_w_mlp1, m_w_mlp2, v_c_ctx, v_w_ada, v_b_ada, v_g_pre_mix, v_g_post_mix, v_g_pre_mlp, v_g_post_mlp, v_w_in, v_w_decay, v_b_decay, v_g_gla, v_w_gla_o, v_w_dw, v_b_dw, v_g_conv_ln, v_b_conv_ln, v_w_conv_o, v_w_pool_g, v_s_pool, v_w_pool_o, v_b_gate, v_w_out, v_w_mlp1, v_w_mlp2):
    a = dict(locals())
    for n in ('w_in', 'm_w_in', 'v_w_in'):
        a[n] = jnp.swapaxes(a[n], 1, 2)
    big_axis = dict(BIG, w_in=1)
    depth = w_in.shape[0]
    d = x.shape[-1]
    seq, nctx_rows = x.shape[1], ctx.shape[1]
    dm = types.SimpleNamespace(
        D=d, SEQ=seq, CTX=nctx_rows, T=seq + nctx_rows, DK=d // 8, DV=d // 4, GK=d // 2, GC=d // 8,
        tm=_tile(nctx_rows, (256, 128, 64)), TB=_tile(nctx_rows, (256, 128, 64)))
    assert dm.SEQ % dm.tm == 0 and dm.SEQ % GRID_W == 0 and dm.CTX % GLA_CHUNK == 0
    tmw = min(dm.tm, 128)
    chip = 2 * lax.axis_index("x") + lax.axis_index("y")
    core = lax.axis_index("c")
    chip1 = chip.astype(jnp.int32).reshape(1)
    core1 = core.astype(jnp.int32).reshape(1)

    big_names, small_names = list(BIG), list(SMALL_SHARDED)
    nbig = len(big_names)
    kinds = ['col' if big_axis[n] == 2 else 'row' for n in big_names]
    wl = w_in.shape[2]
    wlp = _lane_pad(wl)

    def rows8(t):
        t = t.reshape(t.shape[0], -1, t.shape[-1])
        return jnp.pad(t, ((0, 0), (0, -t.shape[1] % 8), (0, 0)))

    def halves(t):
        return t.reshape(2, t.shape[0] // 2, t.shape[1])

    def layer_src(l, tok=None):
        def one(n):
            t = a[n][l] if tok is None else a[n][l] + tok
            return halves((jnp.pad(t, ((0, wlp - wl), (0, 0))) if n == 'w_in' else t).astype(MM_DTYPE))
        return [one(n) for n in big_names]

    def whole(t):
        return t.reshape(-1, t.shape[-1])

    def start_gather(srcs, knds, after, name):
        plan = _gather_plan(knds, [t.shape[2] for t in srcs])
        lands = [lax.empty(_gathered_shape(t, k), t.dtype) for t, k in zip(srcs, knds)]
        return (plan,) + start_copies(srcs, lands, plan, 4 * len(srcs), after, name)

    late = [big_names.index(n) for n in ('w_gla_o', 'w_conv_o', 'w_pool_o', 'w_out', 'w_mlp1', 'w_mlp2')]
    early = [k for k in range(nbig) if k not in late]
    src0 = layer_src(0)
    kinds_e = [kinds[k] for k in early] + ['col'] * len(small_names)
    age = start_gather([src0[k] for k in early] + [rows8(a[n]) for n in small_names], kinds_e, core1,
                       "gather_layer0_start")
    tok0 = age[-1][0, 0]
    src1 = layer_src(1, tok0)
    pk = lambda pre: _flatten_pad([a[pre + n] + tok0 for n in SMALL], F32)
    small_w, small_m, small_v = pk(''), pk('m_'), pk('v_')
    X = jnp.concatenate([ctx[0] + tok0, x[0] + tok0], axis=0)
    ready = (small_w[0, 0] + small_m[0, 0] + small_v[0, 0] + X[0, 0]
             + sum(t[0, 0, 0].astype(F32) for t in src1)).reshape(1, 1)
    _, g0 = wait_copies(age[1], age[2], age[3], age[4], age[0], ready, "gather_layer0_wait")
    g0 = forward_halves(g0, kinds_e, "gather_layer0_forward")
    ag0 = start_gather([src0[k] for k in late], [kinds[k] for k in late], g0[0], "gather_layer0_late_start")
    ag1 = start_gather(src1, kinds, ag0[-1], "gather_layer1_start")
    ag_token = ag1[-1]
    full = {n: [None, None] for n in big_names}
    for k, t in zip(early, g0):
        full[big_names[k]][0] = whole(t)
    for n, g in zip(small_names, g0[len(early):]):
        shp = a[n].shape
        full[n] = g[:, :math.prod(shp[1:-1])].reshape(shp[:-1] + (4 * shp[-1],))
    for n in SMALL:
        if n not in SMALL_SHARDED:
            full[n] = a[n]

    cvec = jnp.concatenate([c_ctx.reshape(1, d), c.reshape(1, d), jnp.zeros((6, d), F32)], axis=0)
    avec = (cvec * jax.nn.sigmoid(cvec) + ag_token[0, 0]).astype(MM_DTYPE)

    def row(v):
        return v.reshape(1, -1)

    saved = []
    gk, gv = dm.GK, d
    lrblk = (7 * d + d // 2) // LANES
    for l in range(depth):
        if l == 1:
            _, got = wait_copies(ag1[1], ag1[2], ag1[3], ag1[4], ag1[0], X, "gather_layer1_wait")
            got = forward_halves(got, kinds, "gather_layer1_forward")
            for n, t in zip(big_names, got):
                full[n][1] = whole(t)
        s = types.SimpleNamespace()
        s.w_in_p = _w_in_t_to_proj(full['w_in'][l], d, wl, wlp)
        wd = full['w_decay'][l]
        wdp = jnp.zeros((LANES, 2 * gk), F32)
        wdp = wdp.at[:GLA_LR, :gk].set(wd[0]).at[GLA_LR:2 * GLA_LR, gk:].set(wd[1])
        s.wdp = wdp.astype(MM_DTYPE)
        s.wdp_wide = jnp.pad(s.wdp, ((0, d // 2 - LANES), (0, 0)))
        s.bd = full['b_decay'][l].reshape(1, 2 * gk)
        modraw = matmul(avec, full['w_ada'][l], 'nn', F32, f"mod_{l}") + full['b_ada'][l][None, :]
        s.mod = [modraw[0:2, j * d:(j + 1) * d].reshape(2, 1, d) for j in range(6)]
        s.x = X
        (s.h,) = rowwise(pre_fn, [X], s.mod[0:2], [row(g_pre_mix[l])], [(d, MM_DTYPE)], dm, f"pre_{l}")
        s.P = matmul(s.h, s.w_in_p, 'nt', MM_DTYPE, f"in_proj_{l}")
        P = s.P
        s.z = matmul((P, LANES, lrblk), s.wdp, 'nn', F32, f"decay_proj_{l}", tk=LANES)
        la_f, la_b = rowwise(decay_fn, [s.z], [], [s.bd], [(gk, F32), (gk, F32)], dm, f"decay_{l}")
        s.la = jnp.concatenate([la_f, la_b], axis=1)
        s.o_f, s.st_f = gla_fwd(P, s.la, False, dm, f"gla_fwd_f_{l}")
        s.o_b, s.st_b = gla_fwd(P, s.la, True, dm, f"gla_fwd_b_{l}")
        (s.gin,) = rowwise(glaout_fn, [s.o_f, s.o_b, (P, d, 3)], [], [row(g_gla[l])], [(gv, MM_DTYPE)], dm,
                           f"gla_out_{l}")
        if l == 0:
            _, got = wait_copies(ag0[1], ag0[2], ag0[3], ag0[4], ag0[0], s.gin, "gather_layer0_late_wait")
            got = forward_halves(got, [kinds[k] for k in late], "gather_layer0_late_forward")
            for k, t in zip(late, got):
                full[big_names[k]][0] = whole(t)
        s.ya = matmul(s.gin, full['w_gla_o'][l], 'nn', MM_DTYPE, f"gla_o_{l}")
        (s.u,) = rowwise(glu_fn, [(P, d, 6)], [], [], [(d // 2, F32)], dm, f"glu_{l}")
        s.yconv = conv_fwd(s.u, full['w_dw'][l], dm, f"conv_{l}")
        (s.cin,) = rowwise(convpost_fn, [s.yconv], [], [row(b_dw[l]), row(g_conv_ln[l]), row(b_conv_ln[l])],
                           [(d // 2, MM_DTYPE)], dm, f"conv_post_{l}")
        s.yb = matmul(s.cin, full['w_conv_o'][l], 'nn', MM_DTYPE, f"conv_o_{l}")
        s.pm = pool_mix((P, d // 2, 14), False, dm, f"pool_mix_{l}")
        s.pc = group_mm(s.pm, w_pool_g[l], 'nn', F32, f"pool_g_{l}")
        (s.pin,) = rowwise(poolpost_fn, [s.pc], [], [row(s_pool[l])], [(d // 2, MM_DTYPE)], dm, f"pool_post_{l}")
        s.yc = matmul(s.pin, full['w_pool_o'][l], 'nn', MM_DTYPE, f"pool_o_{l}")
        s.bg = [row(full['b_gate'][l][j]) for j in range(3)]
        (s.mixed,) = rowwise(merge_fn, [s.ya, s.yb, s.yc, (P, 3 * d, 0)], [], s.bg, [(d, MM_DTYPE)], dm,
                             f"merge_{l}", tm=tmw)
        s.y = matmul(s.mixed, full['w_out'][l], 'nn', MM_DTYPE, f"out_proj_{l}")
        s.x1, s.h2 = rowwise(mid_fn, [X, s.y], s.mod[2:5], [row(g_post_mix[l]), row(g_pre_mlp[l])],
                             [(d, F32), (d, MM_DTYPE)], dm, f"mid_{l}")
        s.act = matmul(s.h2, full['w_mlp1'][l], 'nn', MM_DTYPE, f"mlp1_{l}", epi=relu2_epi)
        s.y2 = matmul(s.act, full['w_mlp2'][l], 'nn', MM_DTYPE, f"mlp2_{l}")
        (X,) = rowwise(post_fn, [s.x1, s.y2], s.mod[5:6], [row(g_post_mlp[l])], [(d, F32)], dm, f"post_{l}")
        saved.append(s)

    dX, lossv = loss_head(X, loss_target[0], dm, "loss_head")
    loss = lax.psum(lossv[0, 0], ("x", "y", "c"))

    grads = {n: [None] * depth for n in WEIGHTS if n != 'c_ctx' and n not in BIG}
    gbig = {n: [None] * depth for n in BIG}
    rs_token = None
    where = jnp.concatenate([chip1, core1])

    def swap_plan(src, land, x, y, c):
        return [(src[n], land[n], (x, y, 1 - c), land[n]) for n in range(len(src))]


    def start_scatter(idx, layer, after, name):
        gs = [gbig[big_names[k]][layer] for k in idx]
        wd = [t.shape[1] // 4 if kinds[k] == 'col' else t.shape[0] // 4 for t, k in zip(gs, idx)]
        plan = _scatter_plan([big_axis[big_names[k]] - 1 for k in idx], wd)
        lands = [lax.empty((3, t.shape[0], w) if kinds[k] == 'col' else (3, w, t.shape[1]), t.dtype)
                 for t, w, k in zip(gs, wd, idx)]
        return (plan,) + start_copies(gs, lands, plan, 3 * len(gs), after, name)

    g_cctx = jnp.zeros((d,), F32)
    for l in reversed(range(depth)):
        s = saved[l]
        P = s.P
        dmod = [None] * 6
        gpm = row(g_post_mlp[l]) if rs_token is None else row(g_post_mlp[l]) + rs_token[0, 0]
        (dx1, dy2), (dmod[5],), (dg,) = rowwise_vjp(post_fn, [s.x1, s.y2], s.mod[5:6], [gpm], [dX],
                                                     dm, f"post_bwd_{l}", narrow=(1,))
        grads['g_post_mlp'][l] = dg[0]
        du1 = matmul(dy2, full['w_mlp2'][l], 'nt', MM_DTYPE, f"mlp2_dx_{l}", epi=relu2_bwd_epi, extras=[s.act])
        gbig['w_mlp2'][l] = matmul(s.act, dy2, 'tn', MM_DTYPE, f"mlp2_dw_{l}")
        dh2 = matmul(du1, full['w_mlp1'][l], 'nt', MM_DTYPE, f"mlp1_dx_{l}")
        gbig['w_mlp1'][l] = matmul(s.h2, du1, 'tn', MM_DTYPE, f"mlp1_dw_{l}")
        gpx = row(g_post_mix[l])
        (dxa, dy), dmod[2:5], (dg1, dg2) = rowwise_vjp(
            mid_fn, [s.x, s.y], s.mod[2:5], [gpx, row(g_pre_mlp[l])], [dx1, dh2], dm, f"mid_bwd_{l}", narrow=(1,))
        grads['g_post_mix'][l], grads['g_pre_mlp'][l] = dg1[0], dg2[0]
        dmixed = matmul(dy, full['w_out'][l], 'nt', MM_DTYPE, f"out_proj_dx_{l}")
        gbig['w_out'][l] = matmul(s.mixed, dy, 'tn', MM_DTYPE, f"out_proj_dw_{l}")
        (dya, dyb, dyc, dP), _, dbg = rowwise_vjp(merge_fn, [s.ya, s.yb, s.yc, (P, 3 * d, 0)], [], s.bg, [dmixed],
                                                  dm, f"merge_bwd_{l}", tm=tmw, narrow=(0, 1, 2),
                                                  into=(3, None, P.shape))
        grads['b_gate'][l] = jnp.concatenate(dbg, axis=0)
        dgin = matmul(dya, full['w_gla_o'][l], 'nt', MM_DTYPE, f"gla_o_dx_{l}")
        gbig['w_gla_o'][l] = matmul(s.gin, dya, 'tn', MM_DTYPE, f"gla_o_dw_{l}")
        dcin = matmul(dyb, full['w_conv_o'][l], 'nt', MM_DTYPE, f"conv_o_dx_{l}")
        gbig['w_conv_o'][l] = matmul(s.cin, dyb, 'tn', MM_DTYPE, f"conv_o_dw_{l}")
        dpin = matmul(dyc, full['w_pool_o'][l], 'nt', MM_DTYPE, f"pool_o_dx_{l}")
        gbig['w_pool_o'][l] = matmul(s.pin, dyc, 'tn', MM_DTYPE, f"pool_o_dw_{l}")
        sp = row(s_pool[l])
        if l == 0:
            rs0 = start_scatter(late, 0, dpin, "grad_layer0_late_start")
            sp = sp + rs0[-1][0, 0]
        (dpc,), _, (dsp,) = rowwise_vjp(poolpost_fn, [s.pc], [], [sp], [dpin], dm, f"pool_post_bwd_{l}")
        grads['s_pool'][l] = dsp[0]
        grads['w_pool_g'][l] = group_mm(s.pm, w_pool_g[l], 'tn', F32, f"pool_g_dw_{l}", b=dpc)
        dpm = group_mm(dpc, w_pool_g[l], 'nt', F32, f"pool_g_dx_{l}")
        dP = pool_mix(dpm, True, dm, f"pool_mix_bwd_{l}", into=(dP, 14))
        (dyconv,), _, (dbdw, dgln, dbln) = rowwise_vjp(
            convpost_fn, [s.yconv], [], [row(b_dw[l]), row(g_conv_ln[l]), row(b_conv_ln[l])], [dcin], dm,
            f"conv_post_bwd_{l}")
        grads['b_dw'][l], grads['g_conv_ln'][l], grads['b_conv_ln'][l] = dbdw[0], dgln[0], dbln[0]
        du, grads['w_dw'][l] = conv_bwd(s.u, full['w_dw'][l], dyconv, dm, f"conv_bwd_{l}")
        (dP,), _, _ = rowwise_vjp(glu_fn, [(P, d, 6)], [], [], [du], dm, f"glu_bwd_{l}", into=(0, dP, P.shape))
        (do, _, dP), _, (dgg,) = rowwise_vjp(glaout_fn, [s.o_f, s.o_b, (P, d, 3)], [], [row(g_gla[l])], [dgin], dm,
                                             f"gla_out_bwd_{l}", want=[True, False, True], into=(2, dP, P.shape), narrow=(0,))
        grads['g_gla'][l] = dgg[0]
        dqf, dkf, dvf, dlaf = gla_bwd(P, s.la, do, s.st_f, False, dm, f"gla_bwd_f_{l}")
        dP, dlab = gla_bwd(P, s.la, do, s.st_b, True, dm, f"gla_bwd_b_{l}", prev=(dqf, dkf, dvf), into=dP)
        (dz,), _, (dbd,) = rowwise_vjp(decay_fn, [s.z], [], [s.bd], [dlaf, dlab], dm, f"decay_bwd_{l}", narrow=(0,))
        grads['b_decay'][l] = dbd.reshape(2, gk)
        dwdp = matmul((P, LANES, lrblk), dz, 'tn', F32, f"decay_proj_dw_{l}", tm=LANES)
        grads['w_decay'][l] = jnp.stack([dwdp[:GLA_LR, :gk], dwdp[GLA_LR:2 * GLA_LR, gk:]])
        dP = matmul(dz, s.wdp_wide, 'nt', MM_DTYPE, f"decay_proj_dx_{l}", into=(dP, 15))
        gpre = row(g_pre_mix[l])
        if l == 0:
            gs1, got1 = wait_copies(rs1[1], rs1[2], rs1[3], rs1[4], rs1[0], dP, "grad_layer1_wait")
            sa1 = [chip_add(g, r, big_axis[n] - 1, where, f"grad_layer1_add_{n}", slab=False)
                   for n, g, r in zip(big_names, gs1, got1)]
            swp1 = start_copies(sa1, [lax.empty(t.shape, t.dtype) for t in sa1], swap_plan, len(sa1), core1,
                                "grad_layer1_pair_swap_start")
            gpre = gpre + swp1[-1][0, 0]
        dh = matmul(dP, s.w_in_p, 'nn', MM_DTYPE, f"in_proj_dx_{l}")
        gbig['w_in'][l] = _proj_to_w_in_t(matmul(dP, s.h, 'tn', MM_DTYPE, f"in_proj_dw_{l}"), d, wl, wlp)
        (dX,), dmod[0:2], (dg,) = rowwise_vjp(pre_fn, [s.x], s.mod[0:2], [gpre], [dh], dm,
                                               f"pre_bwd_{l}", adds={0: dxa})
        grads['g_pre_mix'][l] = dg[0]
        dmodflat = jnp.concatenate([jnp.concatenate([m_.reshape(2, d) for m_ in dmod], axis=1),
                                    jnp.zeros((6, 6 * d), F32)], axis=0)
        grads['b_ada'][l] = dmodflat[0] + dmodflat[1]
        gbig['w_ada'][l] = matmul(avec, dmodflat, 'tn', MM_DTYPE, f"ada_dw_{l}")
        dav = matmul(dmodflat, full['w_ada'][l], 'nt', F32, f"ada_dx_{l}")
        g_cctx = g_cctx + dav[0] * _silu_grad(c_ctx)
        if l == 1:
            rs1 = start_scatter(list(range(nbig)), 1, dav, "grad_layer1_start")
            rs_token = rs1[-1]

    grad_x = dX[dm.CTX:][None]
    gfull = {n: jnp.stack(v) for n, v in grads.items()}
    gfull['c_ctx'] = g_cctx

    def halves_view(t, k):
        return t.reshape(2, t.shape[0] // 2, t.shape[1]) if k == 'col' else t.reshape(4, 2, t.shape[0] // 8, t.shape[1])
    enames = [big_names[k] for k in early]
    ekinds = [kinds[k] for k in early]
    v0 = [halves_view(gbig[n][0], k) for n, k in zip(enames, ekinds)]
    r1 = pair_swap_halves(v0, ekinds, "grad_pair_swap")
    hs = [pair_add(v.reshape((-1,) + v.shape[-2:]), r.reshape((-1,) + r.shape[-2:]), core1, f"grad_pair_add_{n}")
          for n, v, r in zip(enames, v0, r1)]
    hx = [h.reshape(h.shape[1:]) if k == 'col' else h for h, k in zip(hs, ekinds)]
    ex_plan = _exchange_plan(ekinds)
    ex_lands = [lax.empty((3, h.shape[0], h.shape[1] // 4) if k == 'col' else (3,) + h.shape[1:], h.dtype)
                for h, k in zip(hx, ekinds)]
    ex = (ex_plan,) + start_copies(hx, ex_lands, ex_plan, 3 * len(hx), core1, "grad_chip_exchange_start")

    gs0, got0 = wait_copies(rs0[1], rs0[2], rs0[3], rs0[4], rs0[0], ex[-1], "grad_layer0_late_wait")
    sa = [chip_add(g, r, big_axis[big_names[k]] - 1, where, f"grad_layer0_add_{big_names[k]}", slab=False)
          for k, g, r in zip(late, gs0, got0)]
    sflat = _flatten_pad([gfull[n].astype(F32) for n in SMALL], F32)
    sv = sflat.reshape(2, sflat.shape[0] // 2, LANES)
    (sr,) = pair_swap_halves([sv], ['col'], "small_grad_pair_swap")
    sh = pair_add(sv, sr[None], core1, "small_grad_pair_add")[0]
    sq = quad_sum(sh, chip_broadcast(sh, "small_grad_chip_exchange"), core1, "small_grad_chip_sum")
    (ssum,) = pair_join_layers([sq], "small_grad_pair_join")

    swp = start_copies(sa, [lax.empty(t.shape, t.dtype) for t in sa], swap_plan, len(sa), ssum,
                       "grad_late_pair_swap_start")
    ssum = ssum.reshape(-1) + swp[-1][0, 0]
    start = 0
    sg = {}
    for n in SMALL:
        cnt = gfull[n].size
        g = ssum[start:start + cnt].reshape(gfull[n].shape)
        start += cnt
        if n in SMALL_SHARDED:
            ax = SMALL_SHARDED[n]
            wdt = a[n].shape[ax]
            g = lax.dynamic_slice_in_dim(g, chip * wdt, wdt, axis=ax)
        sg[n] = g
    gs = _flatten_pad([sg[n] for n in SMALL], F32)
    dl, mn, vn = adamw(small_w, gs, small_m, small_v, "adamw_small")

    sa, sb = wait_copies(swp[0], swp[1], swp[2], swp[3], swap_plan, dl, "grad_late_pair_swap_wait")
    sa1, sb1 = wait_copies(swp1[0], swp1[1], swp1[2], swp1[3], swap_plan, dl, "grad_layer1_pair_swap_wait")
    red0 = {big_names[k]: [sa[j], sb[j]] for j, k in enumerate(late)}
    red1 = {n: [sa1[k], sb1[k]] for k, n in enumerate(big_names)}

    out_g, out_d, out_m, out_v = {}, {}, {}, {}

    def update_big(n, terms, **kw):
        res = adamw_layers(a[n], a['m_' + n], a['v_' + n], terms, f"adamw_{n}" + ("" if not kw else f"_{kw['layer']}"), **kw)
        out_g[n], out_d[n], out_m[n], out_v[n] = res
        return res
    for k in late:
        update_big(big_names[k], [red0[big_names[k]], red1[big_names[k]]])
    half_done = {n: update_big(n, {1: red1[n]}, layer=1) for n in enames}
    done = (dl[0, 0] + sum(out_d[n][1, 0, 0] for n in big_names)).reshape(1, 1)
    hx, r2 = wait_copies(ex[1], ex[2], ex[3], ex[4], ex[0], done, "grad_chip_exchange_wait")
    dl, mn, vn = dl.reshape(-1), mn.reshape(-1), vn.reshape(-1)
    start = 0
    for n in SMALL:
        cnt, shp = a[n].size, a[n].shape
        out_g[n] = sg[n]
        out_d[n], out_m[n], out_v[n] = (t[start:start + cnt].reshape(shp) for t in (dl, mn, vn))
        start += cnt
    fs = [chip_add(h.reshape(-1, h.shape[-1]), r, big_axis[n] - 1, where, f"grad_chip_add_{n}")
          for n, h, r in zip(enames, hx, r2)]
    for n, t in zip(enames, pair_join_layers(fs, "grad_pair_join")):
        update_big(n, {0: [t.reshape(-1, t.shape[-1])]}, layer=0, prev=tuple(half_done[n]))
    for dct in (out_g, out_d, out_m, out_v):
        dct['w_in'] = jnp.swapaxes(dct['w_in'], 1, 2)
    return (loss, grad_x, *[out_g[n] for n in WEIGHTS], *[out_d[n] for n in WEIGHTS],
            *[out_m[n] for n in WEIGHTS], *[out_v[n] for n in WEIGHTS])
```

```python
import math
import types

import jax
import jax.numpy as jnp
from jax import lax
from jax.experimental import pallas as pl
from jax.experimental.pallas import tpu as pltpu

F32 = jnp.float32
MM_DTYPE = jnp.bfloat16
VMEM_LIMIT_V7X = 56 * 1024 * 1024
LANES = 128
EPS = 1e-6

N_HEADS = 4
GLA_CHUNK = 64
GLA_TAU = 16.0
GLA_LR = 16
GRID_W = 64
POOL_WINDOWS = (2, 4, 8, 16)

ADAM_LR = 0.001
ADAM_B1 = 0.9
ADAM_B2 = 0.999
ADAM_EPS = 1e-08
ADAM_WD = 0.01
ADAM_STEP = 10

NN = (((1,), (0,)), ((), ()))
NT = (((1,), (1,)), ((), ()))
TN = (((0,), (0,)), ((), ()))

WEIGHTS = ['c_ctx', 'w_ada', 'b_ada', 'g_pre_mix', 'g_post_mix', 'g_pre_mlp', 'g_post_mlp', 'w_in', 'w_decay',
           'b_decay', 'g_gla', 'w_gla_o', 'w_dw', 'b_dw', 'g_conv_ln', 'b_conv_ln', 'w_conv_o', 'w_pool_g',
           's_pool', 'w_pool_o', 'b_gate', 'w_out', 'w_mlp1', 'w_mlp2']
BIG = {'w_ada': 2, 'w_in': 2, 'w_gla_o': 1, 'w_conv_o': 2, 'w_pool_o': 2, 'w_out': 1, 'w_mlp1': 2, 'w_mlp2': 1}
SMALL_SHARDED = {'w_decay': 3, 'b_decay': 2, 'w_dw': 2, 'b_gate': 2}
SMALL = [n for n in WEIGHTS if n not in BIG]


def _tile(n, prefs):
    for t in prefs:
        if n % t == 0:
            return t
    return n


def _cparams(sem=None, **kw):
    return pltpu.CompilerParams(dimension_semantics=sem, vmem_limit_bytes=VMEM_LIMIT_V7X, **kw)


def _dot(a, b, dims=NN):
    return lax.dot_general(a.astype(MM_DTYPE), b.astype(MM_DTYPE), dims, preferred_element_type=F32)


def matmul(a, b, mode, out_dtype, name, tm=None, tn=None, tk=None, epi=None, extras=(), into=None):
    a, aw, ablk = a if isinstance(a, tuple) else (a, a.shape[1], 0)
    if mode == 'nn':
        M, K, N = a.shape[0], aw, b.shape[1]
    elif mode == 'nt':
        M, K, N = a.shape[0], aw, b.shape[0]
    else:
        K, M, N = a.shape[0], aw, b.shape[1]
    big = (1088, 1024, 640, 544, 512, 320, 256, 128, 64, 32, 16, 8)
    if mode == 'tn':
        tm = tm or _tile(M, (1024, 512, 256, 128))
        tn = tn or _tile(N, (1024, 512, 256, 128))
        tk = tk or _tile(K, big)
    else:
        tm = tm or _tile(M, big)
        tn = tn or _tile(N, (1024, 512, 256, 128))
        tk = tk or _tile(K, (1024, 512, 256, 128))
    if aw != a.shape[1]:
        assert (mode == 'tn' and tm == aw) or (mode != 'tn' and tk == aw)
    nk = K // tk
    ne = len(extras)
    dims = {'nn': NN, 'nt': NT, 'tn': TN}[mode]

    def body(a_ref, b_ref, *rest):
        e_refs, o_ref = rest[:ne], rest[ne + (into is not None)]

        def finish(acc):
            if epi is not None:
                acc = epi(acc, *[e[...] for e in e_refs])
            o_ref[...] = acc.astype(o_ref.dtype)

        p = _dot(a_ref[...], b_ref[...], dims)
        if nk == 1:
            finish(p)
            return
        acc = rest[-1]
        k = pl.program_id(2)

        @pl.when(k == 0)
        def _():
            acc[...] = p

        @pl.when(k > 0)
        def _():
            acc[...] += p

        @pl.when(k == nk - 1)
        def _():
            finish(acc[...])

    if mode == 'nn':
        a_spec = pl.BlockSpec((tm, tk), lambda i, j, k: (i, k + ablk))
        b_spec = pl.BlockSpec((tk, tn), lambda i, j, k: (k, j))
    elif mode == 'nt':
        a_spec = pl.BlockSpec((tm, tk), lambda i, j, k: (i, k + ablk))
        b_spec = pl.BlockSpec((tn, tk), lambda i, j, k: (j, k))
    else:
        a_spec = pl.BlockSpec((tk, tm), lambda i, j, k: (k, i + ablk))
        b_spec = pl.BlockSpec((tk, tn), lambda i, j, k: (k, j))
    tile = pl.BlockSpec((tm, tn), lambda i, j, k: (i, j))
    if into is None:
        out_spec, out_shape, more, extra, aliases = tile, jax.ShapeDtypeStruct((M, N), out_dtype), [], [], {}
    else:
        buf, oblk = into
        out_spec = pl.BlockSpec((tm, tn), lambda i, j, k: (i, oblk * (N // tn) + j))
        out_shape = jax.ShapeDtypeStruct(buf.shape, buf.dtype)
        more, extra, aliases = [pl.BlockSpec(memory_space=pl.ANY)], [buf], {2 + ne: 0}
    return pl.pallas_call(
        body, name=name, grid=(M // tm, N // tn, nk),
        in_specs=[a_spec, b_spec] + [tile] * ne + more, out_specs=out_spec,
        out_shape=out_shape, input_output_aliases=aliases,
        scratch_shapes=[] if nk == 1 else [pltpu.VMEM((tm, tn), F32)],
        compiler_params=_cparams(("parallel", "parallel", "arbitrary")),
    )(a, b, *extras, *extra)


def group_mm(a, w, mode, out_dtype, name, b=None):
    T = a.shape[0]
    G, gc, _ = w.shape
    col = pl.BlockSpec((T, gc), lambda g: (0, g))
    wsp = pl.BlockSpec((1, gc, gc), lambda g: (g, 0, 0))
    if mode == 'tn':
        def body(a_ref, b_ref, o_ref):
            o_ref[0] = _dot(a_ref[...], b_ref[...], TN).astype(o_ref.dtype)
        return pl.pallas_call(body, name=name, grid=(G,), in_specs=[col, col], out_specs=wsp,
                              out_shape=jax.ShapeDtypeStruct((G, gc, gc), out_dtype),
                              compiler_params=_cparams(("parallel",)))(a, b)
    dims = NN if mode == 'nn' else NT

    def body(a_ref, w_ref, o_ref):
        o_ref[...] = _dot(a_ref[...], w_ref[0], dims).astype(o_ref.dtype)
    return pl.pallas_call(body, name=name, grid=(G,), in_specs=[col, wsp], out_specs=col,
                          out_shape=jax.ShapeDtypeStruct((T, G * gc), out_dtype),
                          compiler_params=_cparams(("parallel",)))(a, w)


def _rowspec(r):
    return r if isinstance(r, tuple) else (r, r.shape[1], 0)


def _row_specs(rows, segs, consts, tm, nctx):
    specs = [pl.BlockSpec((tm, w), lambda i, b=b: (i, b)) for _, w, b in rows]
    specs += [pl.BlockSpec((1,) + s.shape[1:], lambda i, n=s.ndim: (jnp.where(i >= nctx, 1, 0),) + (0,) * (n - 1))
              for s in segs]
    specs += [pl.BlockSpec(c.shape, lambda i, n=c.ndim: (0,) * n) for c in consts]
    return specs


def rowwise(fn, rows, segs, consts, outs, dm, name, tm=None):
    tm = tm or dm.tm
    nctx = dm.CTX // tm
    rows = [_rowspec(r) for r in rows]
    nr, ns, nc = len(rows), len(segs), len(consts)

    def body(*refs):
        rin = [r[...] for r in refs[:nr]]
        sin = [s[0] for s in refs[nr:nr + ns]]
        cin = [c[...] for c in refs[nr + ns:nr + ns + nc]]
        res = fn(*rin, *sin, *cin)
        for o_ref, v in zip(refs[nr + ns + nc:], res):
            o_ref[...] = v.astype(o_ref.dtype)

    res = pl.pallas_call(
        body, name=name, grid=(dm.T // tm,),
        in_specs=_row_specs(rows, segs, consts, tm, nctx),
        out_specs=[pl.BlockSpec((tm, w), lambda i: (i, 0)) for w, _ in outs],
        out_shape=[jax.ShapeDtypeStruct((dm.T, w), dt) for w, dt in outs],
        compiler_params=_cparams(("parallel",)),
    )(*[r[0] for r in rows], *segs, *consts)
    return res


def rowwise_vjp(fn, rows, segs, consts, cots, dm, name, tm=None, want=None, adds=None, narrow=(), into=None):
    tm = tm or dm.tm
    nctx = dm.CTX // tm
    rows = [_rowspec(r) for r in rows]
    cots = [_rowspec(r) for r in cots]
    adds = adds or {}
    nr, ns, nc, nct = len(rows), len(segs), len(consts), len(cots)
    want = want or [True] * nr
    widx = [k for k in range(nr) if want[k]]
    akeys = sorted(adds)

    def body(*refs):
        i = pl.program_id(0)
        rin = [r[...] for r in refs[:nr]]
        sin = [s[0] for s in refs[nr:nr + ns]]
        cin = [c[...] for c in refs[nr + ns:nr + ns + nc]]
        p = nr + ns + nc
        cot_refs = refs[p:p + nct]
        add_refs = dict(zip(akeys, refs[p + nct:p + nct + len(akeys)]))
        p = p + nct + len(akeys) + (1 if (into is not None and into[1] is not None) else 0)
        rg_refs = refs[p:p + len(widx)]
        sg_refs = refs[p + len(widx):p + len(widx) + ns]
        cg_refs = refs[p + len(widx) + ns:]
        res, vjp = jax.vjp(fn, *rin, *sin, *cin)
        g = vjp(tuple(cr[...].astype(o.dtype) for cr, o in zip(cot_refs, res)))
        for o_ref, k in zip(rg_refs, widx):
            v = g[k].astype(F32)
            if k in add_refs:
                v = v + add_refs[k][...]
            o_ref[...] = v.astype(o_ref.dtype)
        first_seg = jnp.logical_or(i == 0, i == nctx)
        for o_ref, v in zip(sg_refs, g[nr:nr + ns]):
            @pl.when(first_seg)
            def _(o_ref=o_ref, v=v):
                o_ref[0] = v.astype(F32)

            @pl.when(jnp.logical_not(first_seg))
            def _(o_ref=o_ref, v=v):
                o_ref[0] += v.astype(F32)
        for o_ref, v in zip(cg_refs, g[nr + ns:]):
            @pl.when(i == 0)
            def _(o_ref=o_ref, v=v):
                o_ref[...] = v.astype(F32)

            @pl.when(i > 0)
            def _(o_ref=o_ref, v=v):
                o_ref[...] += v.astype(F32)

    in_specs = _row_specs(rows, segs, consts, tm, nctx)
    in_specs += [pl.BlockSpec((tm, w), lambda i, b=b: (i, b)) for _, w, b in cots]
    in_specs += [pl.BlockSpec((tm, adds[k].shape[1]), lambda i: (i, 0)) for k in akeys]
    out_specs = [pl.BlockSpec((tm, rows[k][1]), lambda i: (i, 0)) for k in widx]
    out_shape = [jax.ShapeDtypeStruct((dm.T, rows[k][1]), MM_DTYPE if k in narrow else rows[k][0].dtype)
                 for k in widx]
    extra, aliases = [], {}
    if into is not None:
        ik, ibuf, ishape = into
        out_specs[widx.index(ik)] = pl.BlockSpec((tm, rows[ik][1]), lambda i, b=rows[ik][2]: (i, b))
        out_shape[widx.index(ik)] = jax.ShapeDtypeStruct(ishape, MM_DTYPE)
        if ibuf is not None:
            aliases = {len(in_specs): widx.index(ik)}
            in_specs = in_specs + [pl.BlockSpec(memory_space=pl.ANY)]
            extra = [ibuf]
    out_specs += [pl.BlockSpec((1,) + s.shape[1:], lambda i, n=s.ndim: (jnp.where(i >= nctx, 1, 0),) + (0,) * (n - 1))
                  for s in segs]
    out_shape += [jax.ShapeDtypeStruct(s.shape, F32) for s in segs]
    out_specs += [pl.BlockSpec(c.shape, lambda i, n=c.ndim: (0,) * n) for c in consts]
    out_shape += [jax.ShapeDtypeStruct(c.shape, F32) for c in consts]
    res = pl.pallas_call(
        body, name=name, grid=(dm.T // tm,), in_specs=in_specs, out_specs=out_specs, out_shape=out_shape,
        input_output_aliases=aliases, compiler_params=_cparams(("arbitrary",)),
    )(*[r[0] for r in rows], *segs, *consts, *[r[0] for r in cots], *[adds[k] for k in akeys], *extra)
    rg = [None] * nr
    for k, v in zip(widx, res[:len(widx)]):
        rg[k] = v
    return rg, list(res[len(widx):len(widx) + ns]), list(res[len(widx) + ns:])


def _rms(x, g):
    return x * lax.rsqrt(jnp.mean(x * x, axis=-1, keepdims=True) + EPS) * g


def _sigmoid(x):
    return jax.nn.sigmoid(x)


def pre_fn(x, shift, scale, g):
    return ((_rms(x, g) * (1.0 + scale) + shift).astype(MM_DTYPE),)


def mid_fn(x, y, gate, shift, scale, g_post, g_pre):
    x1 = x + gate * _rms(y.astype(F32), g_post)
    return x1, (_rms(x1, g_pre) * (1.0 + scale) + shift).astype(MM_DTYPE)


def post_fn(x1, y2, gate, g):
    return (x1 + gate * _rms(y2.astype(F32), g),)


def relu2_epi(acc):
    r = jnp.maximum(acc, 0.0)
    return r * r


def relu2_bwd_epi(dact, act):
    return dact * (2.0 * jnp.sqrt(act.astype(F32)))


def decay_fn(z, bd):
    zz = z.astype(F32) + bd
    ls = jnp.minimum(zz, 0.0) - jnp.log(1.0 + jnp.exp(jnp.minimum(zz, -zz)))
    la = ls / GLA_TAU
    gk = la.shape[1] // 2
    return la[:, :gk], la[:, gk:]


def glu_fn(ab):
    h = ab.shape[1] // 2
    return (ab[:, :h].astype(F32) * _sigmoid(ab[:, h:].astype(F32)),)


def glaout_fn(o_f, o_b, og, g):
    o = o_f + o_b
    dv = o.shape[1] // N_HEADS
    hs = []
    for h in range(N_HEADS):
        oh = o[:, h * dv:(h + 1) * dv]
        hs.append(oh * lax.rsqrt(jnp.mean(oh * oh, axis=-1, keepdims=True) + EPS) * g[:, h * dv:(h + 1) * dv])
    og = og.astype(F32)
    return ((jnp.concatenate(hs, axis=1) * (og * _sigmoid(og))).astype(MM_DTYPE),)


def convpost_fn(y, b_dw, g, b):
    y = y + b_dw
    mu = jnp.mean(y, axis=-1, keepdims=True)
    xc = y - mu
    yn = xc * lax.rsqrt(jnp.mean(xc * xc, axis=-1, keepdims=True) + EPS) * g + b
    return ((yn * _sigmoid(yn)).astype(MM_DTYPE),)


def poolpost_fn(pc, s):
    return ((pc.astype(F32) * s).astype(MM_DTYPE),)


def merge_fn(ya, yb, yc, mg, bg0, bg1, bg2):
    d = ya.shape[1]
    mg = mg.astype(F32)
    mixed = (_sigmoid(mg[:, :d] + bg0) * ya.astype(F32) + _sigmoid(mg[:, d:2 * d] + bg1) * yb.astype(F32)
             + _sigmoid(mg[:, 2 * d:] + bg2) * yc.astype(F32))
    return (mixed.astype(MM_DTYPE),)


def _split_dot(lmat, x, dims):
    hi = x.astype(MM_DTYPE)
    lo = x - hi.astype(F32)
    return _dot(lmat, hi, dims) + _dot(lmat, lo, dims)


def _gla_block_order(dm, rev):
    nctx, nb = dm.CTX // dm.TB, dm.T // dm.TB

    def blk(i):
        if not rev:
            return i
        return jnp.where(i < nctx, nctx - 1 - i, nb - 1 - (i - nctx))
    return blk, nb


def _gla_tri(rev):
    c = GLA_CHUNK
    t = lax.broadcasted_iota(jnp.int32, (c, c), 0)
    s = lax.broadcasted_iota(jnp.int32, (c, c), 1)
    return (s >= t) if rev else (s <= t)


def _gla_cumsum(la, tri):
    lmat = tri.astype(MM_DTYPE)
    return lmat, _split_dot(lmat, la, NN), jnp.sum(la, axis=0, keepdims=True)


def _gla_chunk_terms(q, k, b, bend, tri, scale):
    eb = jnp.exp(b)
    enb = jnp.exp(-b)
    ee = jnp.exp(bend - b)
    qi = q * scale * eb
    ki = k * enb
    kend = k * ee
    att = jnp.where(tri, _dot(qi, ki, NT), 0.0)
    return eb, enb, ee, qi, ki, kend, att


def gla_fwd(P, la, rev, dm, name):
    c, tb, h_, dk, dv, d = GLA_CHUNK, dm.TB, N_HEADS, dm.DK, dm.DV, dm.D
    cpb = tb // c
    blk, nb = _gla_block_order(dm, rev)
    gk, gv = h_ * dk, h_ * dv
    qb, kb, vb, lb = (5 * d) // gk, (5 * d + d // 2) // gk, (4 * d) // gv, (1 if rev else 0)
    scale = dk ** -0.5
    order = list(range(cpb))[::-1] if rev else list(range(cpb))

    def body(q_ref, k_ref, v_ref, la_ref, o_ref, s_ref, st):
        @pl.when(pl.program_id(0) == 0)
        def _():
            st[...] = jnp.zeros_like(st)
        tri = _gla_tri(rev)
        terms = {}
        for n, ci in enumerate(order):
            r = pl.ds(ci * c, c)
            _, b_all, bend_all = _gla_cumsum(la_ref[r, :], tri)
            for hh in range(h_):
                ck, cv = pl.ds(hh * dk, dk), pl.ds(hh * dv, dv)
                hs = slice(hh * dk, (hh + 1) * dk)
                v = v_ref[r, cv]
                _, _, _, qi, _, kend, att = _gla_chunk_terms(
                    q_ref[r, ck].astype(F32), k_ref[r, ck].astype(F32), b_all[:, hs], bend_all[:, hs], tri, scale)
                terms[n, hh] = (_dot(att, v), qi.astype(MM_DTYPE), jnp.exp(bend_all[:, hs]), _dot(v, kend, TN))
        for n, ci in enumerate(order):
            r = pl.ds(ci * c, c)
            for hh in range(h_):
                intra, qi, gam, dstate = terms[n, hh]
                s_in = st[hh]
                o_ref[r, pl.ds(hh * dv, dv)] = intra + _dot(qi, s_in, NT)
                s_ref[n, hh] = s_in
                st[hh] = gam * s_in + dstate

    return pl.pallas_call(
        body, name=name, grid=(nb,),
        in_specs=[pl.BlockSpec((tb, gk), lambda i: (blk(i), qb)),
                  pl.BlockSpec((tb, gk), lambda i: (blk(i), kb)),
                  pl.BlockSpec((tb, gv), lambda i: (blk(i), vb)),
                  pl.BlockSpec((tb, gk), lambda i: (blk(i), lb))],
        out_specs=[pl.BlockSpec((tb, gv), lambda i: (blk(i), 0)),
                   pl.BlockSpec((cpb, h_, dv, dk), lambda i: (i, 0, 0, 0))],
        out_shape=[jax.ShapeDtypeStruct((dm.T, gv), F32),
                   jax.ShapeDtypeStruct((dm.T // c, h_, dv, dk), F32)],
        scratch_shapes=[pltpu.VMEM((h_, dv, dk), F32)],
        compiler_params=_cparams(("arbitrary",)),
    )(P, P, P, la)


def gla_bwd(P, la, do, states, rev, dm, name, prev=None, into=None):
    c, tb, h_, dk, dv, d = GLA_CHUNK, dm.TB, N_HEADS, dm.DK, dm.DV, dm.D
    cpb = tb // c
    blk, nb = _gla_block_order(dm, rev)
    gk, gv = h_ * dk, h_ * dv
    qb, kb, vb, lb = (5 * d) // gk, (5 * d + d // 2) // gk, (4 * d) // gv, (1 if rev else 0)
    scale = dk ** -0.5
    order = list(range(cpb))[::-1] if rev else list(range(cpb))

    fused = prev is not None

    def body(q_ref, k_ref, v_ref, la_ref, do_ref, s_ref, *rest):
        if fused:
            pq_ref, pk_ref, pv_ref, _, w_ref, dla_ref, dst = rest
        else:
            dq_ref, dk_ref, dv_ref, dla_ref, dst = rest

        def put(kind, r, cols, val):
            if not fused:
                {'q': dq_ref, 'k': dk_ref, 'v': dv_ref}[kind][r, cols] = val
                return
            p_ref, off = {'q': (pq_ref, gv), 'k': (pk_ref, gv + gk), 'v': (pv_ref, 0)}[kind]
            w_ref[r, pl.ds(off + cols.start, cols.size)] = (val + p_ref[r, cols]).astype(w_ref.dtype)

        @pl.when(pl.program_id(0) == 0)
        def _():
            dst[...] = jnp.zeros_like(dst)
        tri = _gla_tri(rev)
        for n in range(cpb - 1, -1, -1):
            r = pl.ds(order[n] * c, c)
            for hh in range(h_):
                ck, cv = pl.ds(hh * dk, dk), pl.ds(hh * dv, dv)
                q = q_ref[r, ck].astype(F32)
                k = k_ref[r, ck].astype(F32)
                v = v_ref[r, cv]
                lmat, b, bend = _gla_cumsum(la_ref[r, ck], tri)
                eb, enb, ee, qi, ki, kend, att = _gla_chunk_terms(q, k, b, bend, tri, scale)
                s_in = s_ref[n, hh]
                ds_out = dst[hh]
                dob = do_ref[r, cv]
                datt = jnp.where(tri, _dot(dob, v, NT), 0.0)
                dqi = _dot(datt, ki) + _dot(dob, s_in)
                dki = _dot(datt, qi, TN)
                put('v', r, cv, _dot(att, dob, TN) + _dot(kend, ds_out, NT))
                dkend = _dot(v, ds_out)
                gam = jnp.exp(bend)
                dgam = jnp.sum(ds_out * s_in, axis=0, keepdims=True)
                dst[hh] = gam * ds_out + _dot(dob, qi, TN)
                put('q', r, ck, dqi * (scale * eb))
                put('k', r, ck, dki * enb + dkend * ee)
                db = dqi * qi - dki * ki - dkend * kend
                dbend = jnp.sum(dkend * kend, axis=0, keepdims=True) + dgam * gam
                dla_ref[r, ck] = _split_dot(lmat, db, TN) + dbend

    def bi(j):
        return blk(nb - 1 - j)

    in_specs = [
        pl.BlockSpec((tb, gk), lambda j: (bi(j), qb)),
        pl.BlockSpec((tb, gk), lambda j: (bi(j), kb)),
        pl.BlockSpec((tb, gv), lambda j: (bi(j), vb)),
        pl.BlockSpec((tb, gk), lambda j: (bi(j), lb)),
        pl.BlockSpec((tb, gv), lambda j: (bi(j), 0)),
        pl.BlockSpec((cpb, h_, dv, dk), lambda j: (nb - 1 - j, 0, 0, 0)),
    ]
    small = pl.BlockSpec((tb, gk), lambda j: (bi(j), 0))
    wide = pl.BlockSpec((tb, gv), lambda j: (bi(j), 0))
    if not fused:
        return pl.pallas_call(
            body, name=name, grid=(nb,), in_specs=in_specs, out_specs=[small, small, wide, small],
            out_shape=[jax.ShapeDtypeStruct((dm.T, gk), F32), jax.ShapeDtypeStruct((dm.T, gk), F32),
                       jax.ShapeDtypeStruct((dm.T, gv), F32), jax.ShapeDtypeStruct((dm.T, gk), F32)],
            scratch_shapes=[pltpu.VMEM((h_, dv, dk), F32)],
            compiler_params=_cparams(("arbitrary",)),
        )(P, P, P, la, do, states)
    return pl.pallas_call(
        body, name=name, grid=(nb,),
        in_specs=in_specs + [small, small, wide, pl.BlockSpec(memory_space=pl.ANY)],
        out_specs=[pl.BlockSpec((tb, 2 * gv), lambda j: (bi(j), vb // 2)), small],
        out_shape=[jax.ShapeDtypeStruct(into.shape, into.dtype), jax.ShapeDtypeStruct((dm.T, gk), F32)],
        input_output_aliases={9: 0},
        scratch_shapes=[pltpu.VMEM((h_, dv, dk), F32)],
        compiler_params=_cparams(("arbitrary",)),
    )(P, P, P, la, do, states, *prev, into)


def _pos(n, period):
    t = lax.broadcasted_iota(jnp.int32, (n, 1), 0)
    if period & (period - 1) == 0:
        return jnp.bitwise_and(t, period - 1)
    return lax.rem(t, period)


def _conv_segments(dm):
    return [(0, dm.CTX, dm.CTX), (dm.CTX, dm.SEQ, GRID_W)]


def conv_fwd(u, w, dm, name):
    kw, cw = w.shape
    segs = _conv_segments(dm)

    def body(u_ref, w_ref, y_ref):
        for r0, n, per in segs:
            useg = u_ref[r0:r0 + n, :]
            p = _pos(n, per)
            acc = jnp.zeros_like(useg)
            for kk in range(kw):
                d = kk - kw // 2
                sh = useg if d == 0 else pltpu.roll(useg, (-d) % n, 0)
                ok = jnp.logical_and(p + d >= 0, p + d < per)
                acc = acc + jnp.where(ok, sh, 0.0) * w_ref[kk:kk + 1, :]
            y_ref[r0:r0 + n, :] = acc

    return pl.pallas_call(
        body, name=name, grid=(cw // LANES,),
        in_specs=[pl.BlockSpec((dm.T, LANES), lambda j: (0, j)), pl.BlockSpec((kw, LANES), lambda j: (0, j))],
        out_specs=pl.BlockSpec((dm.T, LANES), lambda j: (0, j)),
        out_shape=jax.ShapeDtypeStruct((dm.T, cw), F32),
        compiler_params=_cparams(("parallel",)),
    )(u, w)


def conv_bwd(u, w, dy, dm, name):
    kw, cw = w.shape
    segs = _conv_segments(dm)

    def body(u_ref, w_ref, dy_ref, du_ref, dw_ref):
        dws = [jnp.zeros((1, LANES), F32)] * kw
        for r0, n, per in segs:
            useg = u_ref[r0:r0 + n, :]
            dyseg = dy_ref[r0:r0 + n, :]
            p = _pos(n, per)
            acc = jnp.zeros_like(useg)
            for kk in range(kw):
                d = kk - kw // 2
                shu = useg if d == 0 else pltpu.roll(useg, (-d) % n, 0)
                okf = jnp.logical_and(p + d >= 0, p + d < per)
                dws[kk] = dws[kk] + jnp.sum(jnp.where(okf, shu, 0.0) * dyseg, axis=0, keepdims=True)
                shd = dyseg if d == 0 else pltpu.roll(dyseg, d % n, 0)
                okb = jnp.logical_and(p - d >= 0, p - d < per)
                acc = acc + jnp.where(okb, shd, 0.0) * w_ref[kk:kk + 1, :]
            du_ref[r0:r0 + n, :] = acc
        for kk in range(kw):
            dw_ref[kk:kk + 1, :] = dws[kk]

    return pl.pallas_call(
        body, name=name, grid=(cw // LANES,),
        in_specs=[pl.BlockSpec((dm.T, LANES), lambda j: (0, j)), pl.BlockSpec((kw, LANES), lambda j: (0, j)),
                  pl.BlockSpec((dm.T, LANES), lambda j: (0, j))],
        out_specs=[pl.BlockSpec((dm.T, LANES), lambda j: (0, j)), pl.BlockSpec((kw, LANES), lambda j: (0, j))],
        out_shape=[jax.ShapeDtypeStruct((dm.T, cw), F32), jax.ShapeDtypeStruct((kw, cw), F32)],
        compiler_params=_cparams(("parallel",)),
    )(u, w, dy)


def pool_mix(u, transpose, dm, name, into=None):
    u, uw, ublk = _rowspec(u)
    gc = dm.GC
    ng = len(POOL_WINDOWS)
    rows = dm.SEQ // GRID_W
    segs = [(0, dm.CTX, 1, dm.CTX), (dm.CTX, dm.SEQ, GRID_W, rows)]

    def one_group(u_ref, o_ref, win):
        left = win // 2
        right = win - 1 - left
        for r0, n, stride, length in segs:
            useg = u_ref[r0:r0 + n, :].astype(F32)
            t = lax.broadcasted_iota(jnp.int32, (n, 1), 0)
            p = t if stride == 1 else jnp.right_shift(t, stride.bit_length() - 1)
            cnt = (jnp.minimum(p + right + 1, length) - jnp.maximum(p - left, 0)).astype(F32)
            src = useg / cnt if transpose else useg
            acc = jnp.zeros_like(useg)
            for d in range(-left, right + 1):
                dd = -d if transpose else d
                sh = src if d == 0 else pltpu.roll(src, (-dd * stride) % n, 0)
                ok = jnp.logical_and(p + dd >= 0, p + dd < length)
                acc = acc + jnp.where(ok, sh, 0.0)
            o_ref[r0:r0 + n, :] = ((acc - useg) if transpose else (acc / cnt - useg)).astype(o_ref.dtype)

    def body(u_ref, *rest):
        o_ref = rest[-1]
        g = pl.program_id(0)
        for gi, win in enumerate(POOL_WINDOWS):
            @pl.when(g == gi)
            def _(win=win):
                one_group(u_ref, o_ref, win)

    base = ublk * (uw // gc)
    if into is None:
        obase, out_shape, more, extra, aliases = 0, jax.ShapeDtypeStruct((dm.T, ng * gc), F32), [], [], {}
    else:
        buf, oblk = into
        obase, out_shape = oblk * ng, jax.ShapeDtypeStruct(buf.shape, buf.dtype)
        more, extra, aliases = [pl.BlockSpec(memory_space=pl.ANY)], [buf], {1: 0}
    return pl.pallas_call(
        body, name=name, grid=(ng,),
        in_specs=[pl.BlockSpec((dm.T, gc), lambda g: (0, base + g))] + more,
        out_specs=pl.BlockSpec((dm.T, gc), lambda g: (0, obase + g)),
        out_shape=out_shape, input_output_aliases=aliases,
        compiler_params=_cparams(("parallel",)),
    )(u, *extra)


def loss_head(x2, target, dm, name):
    tm, d = dm.tm, dm.D
    nctx = dm.CTX // tm

    def body(x_ref, t_ref, dx_ref, l_ref):
        i = pl.program_id(0)

        @pl.when(i == 0)
        def _():
            l_ref[...] = jnp.zeros_like(l_ref)

        @pl.when(i < nctx)
        def _():
            dx_ref[...] = jnp.zeros_like(dx_ref)

        @pl.when(i >= nctx)
        def _():
            e = x_ref[...] - t_ref[...]
            dx_ref[...] = e / d
            l_ref[...] += jnp.full(l_ref.shape, 0.5 * jnp.sum(jnp.mean(e * e, axis=-1)), F32)

    return pl.pallas_call(
        body, name=name, grid=(dm.T // tm,),
        in_specs=[pl.BlockSpec((tm, d), lambda i: (i, 0)),
                  pl.BlockSpec((tm, d), lambda i: (jnp.maximum(i - nctx, 0), 0))],
        out_specs=[pl.BlockSpec((tm, d), lambda i: (i, 0)), pl.BlockSpec((8, LANES), lambda i: (0, 0))],
        out_shape=[jax.ShapeDtypeStruct((dm.T, d), F32), jax.ShapeDtypeStruct((8, LANES), F32)],
        compiler_params=_cparams(("arbitrary",)),
    )(x2, target)


def adamw(w, g, m, v, name):
    r, c = w.shape
    tr = _tile(r, tuple(t for t in (512, 256, 128, 64, 32, 16, 8) if t * c * 4 <= (1 << 20)) or (8,))

    def body(w_ref, g_ref, m_ref, v_ref, d_ref, mo_ref, vo_ref):
        gg = g_ref[...]
        mm = ADAM_B1 * m_ref[...] + (1.0 - ADAM_B1) * gg
        vv = ADAM_B2 * v_ref[...] + (1.0 - ADAM_B2) * (gg * gg)
        m_hat = mm / (1.0 - ADAM_B1 ** ADAM_STEP)
        v_hat = vv / (1.0 - ADAM_B2 ** ADAM_STEP)
        d_ref[...] = -ADAM_LR * (m_hat / (jnp.sqrt(v_hat) + ADAM_EPS) + ADAM_WD * w_ref[...])
        mo_ref[...] = mm
        vo_ref[...] = vv

    spec = pl.BlockSpec((tr, c), lambda i: (i, 0))
    return pl.pallas_call(
        body, name=name, grid=(r // tr,), in_specs=[spec] * 4, out_specs=[spec] * 3,
        out_shape=[jax.ShapeDtypeStruct((r, c), F32)] * 3,
        compiler_params=_cparams(("parallel",)),
    )(w, g, m, v)


def pair_add(g, r1, cidx, name):
    ng, r_, n_ = r1.shape
    tr = _tile(r_, tuple(t for t in (1024, 512, 256, 128, 64, 32, 16) if t * n_ * 4 <= (2 << 20)))

    def body(s_ref, g_ref, r_ref, o_ref):
        o_ref[...] = (g_ref[...].astype(F32) + r_ref[...].astype(F32)).astype(o_ref.dtype)

    return pl.pallas_call(
        body, name=name,
        grid_spec=pltpu.PrefetchScalarGridSpec(
            num_scalar_prefetch=1, grid=(ng, r_ // tr),
            in_specs=[pl.BlockSpec((None, tr, n_), lambda k, i, s: (2 * k + s[0], i, 0)),
                      pl.BlockSpec((None, tr, n_), lambda k, i, s: (k, i, 0))],
            out_specs=pl.BlockSpec((None, tr, n_), lambda k, i, s: (k, i, 0))),
        out_shape=jax.ShapeDtypeStruct((ng, r_, n_), g.dtype),
        compiler_params=_cparams(("parallel", "parallel")),
    )(cidx, g, r1)


def chip_add(h, r2, axis, where, name, slab=True):
    _, kl, nl = r2.shape
    tr = _tile(kl, tuple(t for t in (1024, 512, 256, 128, 64, 32, 16) if t * nl * 4 <= (1 << 20)))
    nrb = kl // tr

    def body(s_ref, h_ref, r_ref, o_ref):
        acc = h_ref[...].astype(F32)
        for k in range(r2.shape[0]):
            acc = acc + r_ref[k].astype(F32)
        o_ref[...] = acc

    h_map = (lambda i, s: (s[0] * nrb + i, 0)) if axis == 0 else (lambda i, s: (i, s[0]))
    if slab:
        out_spec = pl.BlockSpec((None, tr, nl), lambda i, s: (s[1], i, 0))
        out_shape = jax.ShapeDtypeStruct((2, kl, nl), F32)
    else:
        out_spec = pl.BlockSpec((tr, nl), lambda i, s: (i, 0))
        out_shape = jax.ShapeDtypeStruct((kl, nl), F32)
    return pl.pallas_call(
        body, name=name,
        grid_spec=pltpu.PrefetchScalarGridSpec(
            num_scalar_prefetch=1, grid=(nrb,),
            in_specs=[pl.BlockSpec((tr, nl), h_map),
                      pl.BlockSpec((r2.shape[0], tr, nl), lambda i, s: (0, i, 0))],
            out_specs=out_spec),
        out_shape=out_shape,
        compiler_params=_cparams(("parallel",)),
    )(where, h, r2)


def adamw_layers(w, m, v, terms, name, layer=None, prev=None):
    _, a_, b_ = w.shape
    tr = _tile(a_, tuple(t for t in (512, 256, 128, 64, 32) if t * b_ * 4 <= (1 << 20)))
    by_cols = tr == a_ and a_ * b_ * 4 > (1 << 20)
    blk = (a_, LANES) if by_cols else (tr, b_)
    steps = b_ // LANES if by_cols else a_ // tr
    at = (lambda i: (0, i)) if by_cols else (lambda i: (i, 0))
    layers = (0, 1) if layer is None else (layer,)
    counts = [len(terms[l]) for l in layers]
    nprev = 0 if prev is None else 4

    def update(g, w_ref, m_ref, v_ref, g_ref, d_ref, mo_ref, vo_ref):
        mm = ADAM_B1 * m_ref[...] + (1.0 - ADAM_B1) * g
        vv = ADAM_B2 * v_ref[...] + (1.0 - ADAM_B2) * (g * g)
        m_hat = mm / (1.0 - ADAM_B1 ** ADAM_STEP)
        v_hat = vv / (1.0 - ADAM_B2 ** ADAM_STEP)
        g_ref[...] = g
        d_ref[...] = -ADAM_LR * (m_hat / (jnp.sqrt(v_hat) + ADAM_EPS) + ADAM_WD * w_ref[...])
        mo_ref[...] = mm
        vo_ref[...] = vv

    def total(refs):
        g = refs[0][...]
        for r in refs[1:]:
            g = g + r[...]
        return g

    def body(w_ref, m_ref, v_ref, *rest):
        t_refs, outs = rest[:sum(counts)], rest[-4:]
        if len(layers) == 1:
            update(total(t_refs), w_ref, m_ref, v_ref, *outs)
            return
        which = pl.program_id(0)

        @pl.when(which == 0)
        def _():
            update(total(t_refs[:counts[0]]), w_ref, m_ref, v_ref, *outs)

        @pl.when(which == 1)
        def _():
            update(total(t_refs[counts[0]:]), w_ref, m_ref, v_ref, *outs)

    if len(layers) == 1:
        stacked = pl.BlockSpec((None,) + blk, lambda l, i: (layers[0],) + at(i))
        t_specs = [pl.BlockSpec(blk, lambda l, i: at(i))] * counts[0]
    else:
        stacked = pl.BlockSpec((None,) + blk, lambda l, i: (l,) + at(i))
        t_specs = ([pl.BlockSpec(blk, lambda l, i: at(i * (1 - l)))] * counts[0]
                   + [pl.BlockSpec(blk, lambda l, i: at(i * l))] * counts[1])
    nin = 3 + sum(counts)
    return pl.pallas_call(
        body, name=name, grid=(len(layers), steps),
        in_specs=[stacked] * 3 + t_specs + [pl.BlockSpec(memory_space=pl.ANY)] * nprev,
        out_specs=[stacked] * 4, out_shape=[jax.ShapeDtypeStruct(w.shape, F32)] * 4,
        input_output_aliases={nin + j: j for j in range(nprev)},
        compiler_params=_cparams(("arbitrary", "arbitrary")),
    )(w, m, v, *[t for l in layers for t in terms[l]], *(prev or ()))


MESH = pl.DeviceIdType.MESH
ANY = pl.BlockSpec(memory_space=pl.ANY)
HBM = pl.BlockSpec(memory_space=pltpu.HBM)
SEM = pl.BlockSpec(memory_space=pltpu.SEMAPHORE)
EFFECT = pltpu.SideEffectType.DATAFLOW_SIDE_EFFECTING


def _place():
    return lax.axis_index("x"), lax.axis_index("y"), lax.axis_index("c")


def _peers(x, y):
    return [(1 - x, y), (x, 1 - y), (1 - x, 1 - y)]


def _rcopy(src, dst, ssem, rsem, dev):
    return pltpu.make_async_remote_copy(src_ref=src, dst_ref=dst, send_sem=ssem, recv_sem=rsem,
                                        device_id=dev, device_id_type=MESH)


def _gathered_shape(src, kind):
    h, a_, b_ = src.shape
    return (h, a_, 4 * b_) if kind == 'col' else (4, h, a_, b_)


def _win(ref, kind, ch, width):
    return ref.at[:, :, pl.ds(ch * width, width)] if kind == 'col' else ref.at[ch]


def _rect(ref, kind, half, ch, width):
    return ref.at[half, :, pl.ds(ch * width, width)] if kind == 'col' else ref.at[ch, half]


def _gather_plan(kinds, widths):
    def plan(src, land, x, y, c):
        chip = 2 * x + y
        out = []
        for n in range(len(src)):
            for px, py in _peers(x, y):
                out.append((src[n].at[c], _rect(land[n], kinds[n], c, chip, widths[n]), (px, py, c),
                            _rect(land[n], kinds[n], c, 2 * px + py, widths[n])))
            mine = _win(land[n], kinds[n], chip, widths[n])
            out.append((src[n], mine, (x, y, 1 - c), mine))
        return out
    return plan


def forward_halves(lands, kinds, name):
    nw = len(lands)
    widths = [t.shape[-1] // 4 if k == 'col' else t.shape[-1] for t, k in zip(lands, kinds)]

    def body(*refs):
        o = refs[nw:2 * nw]
        ssem, rsem = refs[2 * nw:]
        x, y, c = _place()
        sib = (x, y, 1 - c)
        pidx = [2 * px + py for px, py in _peers(x, y)]
        cps = [_rcopy(_rect(o[n], kinds[n], c, pidx[j], widths[n]), _rect(o[n], kinds[n], c, pidx[j], widths[n]),
                      ssem.at[3 * n + j], rsem.at[3 * n + j], sib) for n in range(nw) for j in range(3)]
        for cp in cps:
            cp.start()
        for n in range(nw):
            for j in range(3):
                cps[3 * n + j].wait_send()
                _rcopy(_rect(o[n], kinds[n], 1 - c, pidx[j], widths[n]), _rect(o[n], kinds[n], 1 - c, pidx[j], widths[n]),
                       ssem.at[3 * n + j], rsem.at[3 * n + j], sib).wait_recv()

    return pl.pallas_call(
        body, name=name, in_specs=[ANY] * nw, out_specs=[ANY] * nw,
        out_shape=[jax.ShapeDtypeStruct(t.shape, t.dtype) for t in lands],
        input_output_aliases={n: n for n in range(nw)},
        scratch_shapes=[pltpu.SemaphoreType.DMA((3 * nw,)), pltpu.SemaphoreType.DMA((3 * nw,))],
    )(*lands)


def _forward_plan(kinds, widths):
    def plan(src, land, x, y, c):
        out = []
        for n in range(len(land)):
            for px, py in _peers(x, y):
                mine = _rect(land[n], kinds[n], c, 2 * px + py, widths[n])
                out.append((mine, mine, (x, y, 1 - c), _rect(land[n], kinds[n], 1 - c, 2 * px + py, widths[n])))
        return out
    return plan


def _scatter_plan(axes, widths):
    def plan(src, land, x, y, c):
        out = []
        for n in range(len(src)):
            for k, (px, py) in enumerate(_peers(x, y)):
                ch = 2 * px + py
                view = (src[n].at[:, pl.ds(ch * widths[n], widths[n])] if axes[n] == 1
                        else src[n].at[pl.ds(ch * widths[n], widths[n]), :])
                out.append((view, land[n].at[k], (px, py, c), land[n].at[k]))
        return out
    return plan


def _exchange_plan(kinds):
    def plan(src, land, x, y, c):
        out = []
        for n in range(len(src)):
            w = land[n].shape[2]
            for j, (px, py) in enumerate(_peers(x, y)):
                ch = 2 * px + py
                view = src[n].at[:, pl.ds(ch * w, w)] if kinds[n] == 'col' else src[n].at[ch]
                out.append((view, land[n].at[j], (px, py, c), land[n].at[j]))
        return out
    return plan


def start_copies(srcs, lands, plan, ncopies, after, name):
    ns, nl = len(srcs), len(lands)

    def body(*refs):
        src, land = refs[:ns], refs[ns:ns + nl]
        ssem, rsem = refs[ns + nl + 1], refs[ns + nl + 2]
        token = refs[-1]
        x, y, c = _place()
        for k, (sv, dv, dev, _) in enumerate(plan(src, land, x, y, c)):
            _rcopy(sv, dv, ssem.at[k], rsem.at[k], dev).start()
        token[...] = jnp.zeros_like(token)

    hbm = lambda t: pltpu.HBM(t.shape, t.dtype)
    res = pl.pallas_call(
        body, name=name,
        out_shape=(pltpu.SemaphoreType.DMA((ncopies,)), pltpu.SemaphoreType.DMA((ncopies,)),
                   *[hbm(t) for t in srcs], *[hbm(t) for t in lands], jax.ShapeDtypeStruct((8, LANES), F32)),
        in_specs=[HBM] * (ns + nl) + [ANY],
        out_specs=(SEM, SEM, *[HBM] * (ns + nl), pl.BlockSpec(memory_space=pltpu.VMEM)),
        input_output_aliases={k: 2 + k for k in range(ns + nl)},
        compiler_params=pltpu.CompilerParams(has_side_effects=EFFECT),
    )(*[pltpu.with_memory_space_constraint(t, pltpu.HBM) for t in list(srcs) + list(lands)], after)
    return res[0], res[1], list(res[2:2 + ns]), list(res[2 + ns:2 + ns + nl]), res[-1]


def wait_copies(ssem, rsem, srcs, lands, plan, after, name):
    ns, nl = len(srcs), len(lands)

    def body(*refs):
        src, land = refs[:ns], refs[ns:ns + nl]
        ss, rs = refs[ns + nl], refs[ns + nl + 1]
        x, y, c = _place()
        for k, (sv, dv, dev, mine) in enumerate(plan(src, land, x, y, c)):
            cp = _rcopy(sv, mine, ss.at[k], rs.at[k], dev)
            cp.wait_send()
            cp.wait_recv()

    hbm = lambda t: pltpu.HBM(t.shape, t.dtype)
    res = pl.pallas_call(
        body, name=name,
        out_shape=(*[hbm(t) for t in srcs], *[hbm(t) for t in lands]),
        in_specs=[HBM] * (ns + nl) + [SEM, SEM, ANY], out_specs=tuple([HBM] * (ns + nl)),
        input_output_aliases={k: k for k in range(ns + nl)},
        compiler_params=pltpu.CompilerParams(has_side_effects=EFFECT),
    )(*srcs, *lands, ssem, rsem, after)
    return list(res[:ns]), list(res[ns:])


def pair_swap_halves(gs, kinds, name):
    nw = len(gs)

    def other(ref, kind, half):
        return ref.at[half] if kind == 'col' else ref.at[:, half]

    def body(*refs):
        g, o = refs[:nw], refs[nw:2 * nw]
        ssem, rsem = refs[2 * nw:]
        x, y, c = _place()
        cps = [_rcopy(other(g[n], kinds[n], 1 - c), o[n], ssem.at[n], rsem.at[n], (x, y, 1 - c)) for n in range(nw)]
        for cp in cps:
            cp.start()
        for cp in cps:
            cp.wait()

    return pl.pallas_call(
        body, name=name, in_specs=[ANY] * nw, out_specs=[ANY] * nw,
        out_shape=[jax.ShapeDtypeStruct(g.shape[1:] if k == 'col' else (g.shape[0],) + g.shape[2:], g.dtype)
                   for g, k in zip(gs, kinds)],
        scratch_shapes=[pltpu.SemaphoreType.DMA((nw,)), pltpu.SemaphoreType.DMA((nw,))],
    )(*gs)


def chip_broadcast(h, name):
    def body(h_ref, o_ref, ssem, rsem):
        x, y, c = _place()
        cps = [_rcopy(h_ref, o_ref.at[j], ssem.at[j], rsem.at[j], (px, py, c)) for j, (px, py) in enumerate(_peers(x, y))]
        for cp in cps:
            cp.start()
        for cp in cps:
            cp.wait()

    return pl.pallas_call(
        body, name=name, in_specs=[ANY], out_specs=ANY,
        out_shape=jax.ShapeDtypeStruct((3,) + h.shape, h.dtype),
        scratch_shapes=[pltpu.SemaphoreType.DMA((3,)), pltpu.SemaphoreType.DMA((3,))],
    )(h)


def quad_sum(h, r, cidx, name):
    r_, c_ = h.shape
    tr = _tile(r_, (512, 256, 128, 64, 32, 16, 8))

    def body(s_ref, h_ref, r_ref, o_ref):
        o_ref[...] = (h_ref[...] + r_ref[2]) + (r_ref[0] + r_ref[1])

    return pl.pallas_call(
        body, name=name,
        grid_spec=pltpu.PrefetchScalarGridSpec(
            num_scalar_prefetch=1, grid=(r_ // tr,),
            in_specs=[pl.BlockSpec((tr, c_), lambda i, s: (i, 0)), pl.BlockSpec((3, tr, c_), lambda i, s: (0, i, 0))],
            out_specs=pl.BlockSpec((None, tr, c_), lambda i, s: (s[0], i, 0))),
        out_shape=jax.ShapeDtypeStruct((2, r_, c_), F32),
        compiler_params=_cparams(("parallel",)),
    )(cidx, h, r)


def pair_join_layers(fs, name):
    nw = len(fs)

    def body(*refs):
        o = refs[nw:2 * nw]
        ssem, rsem = refs[2 * nw:]
        x, y, c = _place()
        sib = (x, y, 1 - c)
        cps = [_rcopy(o[n].at[c], o[n].at[c], ssem.at[n], rsem.at[n], sib) for n in range(nw)]
        for cp in cps:
            cp.start()
        for n in range(nw):
            cps[n].wait_send()
            _rcopy(o[n].at[1 - c], o[n].at[1 - c], ssem.at[n], rsem.at[n], sib).wait_recv()

    return pl.pallas_call(
        body, name=name, in_specs=[ANY] * nw, out_specs=[ANY] * nw,
        out_shape=[jax.ShapeDtypeStruct(f.shape, f.dtype) for f in fs],
        input_output_aliases={n: n for n in range(nw)},
        scratch_shapes=[pltpu.SemaphoreType.DMA((nw,)), pltpu.SemaphoreType.DMA((nw,))],
    )(*fs)


def _flatten_pad(parts, dtype):
    flat = jnp.concatenate([p.reshape(-1).astype(dtype) for p in parts])
    q = 512 * LANES
    n = -(-flat.shape[0] // q) * q
    return jnp.pad(flat, (0, n - flat.shape[0])).reshape(n // LANES, LANES)


def _lane_pad(n):
    return -(-n // LANES) * LANES


def _in_proj_layout(d):
    gk, gv, cw, pw = d // 2, d, d // 2, d // 2
    own = [('q', gk), ('k', gk), ('v', gv), ('og', gv), ('lrf', GLA_LR), ('lrb', GLA_LR), ('ga', cw), ('gb', cw),
           ('pu', pw), ('mg', 3 * d)]
    padded = [('mg', 3 * d), ('og', gv), ('v', gv), ('q', gk), ('k', gk), ('ga', cw), ('gb', cw), ('pu', pw),
              ('lrf', GLA_LR), ('lrb', GLA_LR), ('pad', d // 2 - 2 * GLA_LR)]
    return own, padded


def _row_pieces(src, lo, hi, wl, wlp):
    out = []
    for k in range(4):
        s0, s1 = max(lo, k * wl), min(hi, (k + 1) * wl)
        if s0 < s1:
            out.append(src[k * wlp + s0 - k * wl:k * wlp + s1 - k * wl])
    return out


def _proj_runs(d):
    own, padded = _in_proj_layout(d)
    oat, start = {}, 0
    for n, wd in own:
        oat[n] = start
        start += wd
    runs, start = [], 0
    for n, wd in padded:
        if n != 'pad':
            if runs and runs[-1][0] + runs[-1][2] == oat[n] and runs[-1][1] + runs[-1][2] == start:
                runs[-1] = (runs[-1][0], runs[-1][1], runs[-1][2] + wd)
            else:
                runs.append((oat[n], start, wd))
        start += wd
    return runs, start


def _w_in_t_to_proj(g, d, wl, wlp):
    runs, total = _proj_runs(d)
    parts, at = [], 0
    for o0, p0, wd in runs:
        if p0 > at:
            parts.append(jnp.zeros((p0 - at, g.shape[1]), g.dtype))
        parts += _row_pieces(g, o0, o0 + wd, wl, wlp)
        at = p0 + wd
    if total > at:
        parts.append(jnp.zeros((total - at, g.shape[1]), g.dtype))
    return jnp.concatenate(parts, axis=0)


def _proj_to_w_in_t(gp, d, wl, wlp):
    runs, _ = _proj_runs(d)
    runs = sorted(runs)
    parts = []
    for k in range(4):
        for o0, p0, wd in runs:
            s0, s1 = max(o0, k * wl), min(o0 + wd, (k + 1) * wl)
            if s0 < s1:
                parts.append(gp[p0 + s0 - o0:p0 + s1 - o0])
        parts.append(jnp.zeros((wlp - wl, gp.shape[1]), gp.dtype))
    return jnp.concatenate(parts, axis=0)


def _silu_grad(z):
    s = jax.nn.sigmoid(z)
    return s + z * s * (1.0 - s)


def kernel(x, c, ctx, c_ctx, w_ada, b_ada, g_pre_mix, g_post_mix, g_pre_mlp, g_post_mlp, w_in, w_decay, b_decay, g_gla, w_gla_o, w_dw, b_dw, g_conv_ln, b_conv_ln, w_conv_o, w_pool_g, s_pool, w_pool_o, b_gate, w_out, w_mlp1, w_mlp2, loss_target, m_c_ctx, m_w_ada, m_b_ada, m_g_pre_mix, m_g_post_mix, m_g_pre_mlp, m_g_post_mlp, m_w_in, m_w_decay, m_b_decay, m_g_gla, m_w_gla_o, m_w_dw, m_b_dw, m_g_conv_ln, m_b_conv_ln, m_w_conv_o, m_w_pool_g, m_s_pool, m_w_pool_o, m_b_gate, m_w_out, m_w_mlp1, m_w_mlp2, v_c_ctx, v_w_ada, v_b_ada, v_g_pre_mix, v_g_post_mix, v_g_pre_mlp, v_g_post_mlp, v_w_in, v_w_decay, v_b_decay, v_g_gla, v_w_gla_o, v_w_dw, v_b_dw, v_g_conv_ln, v_b_conv_ln, v_w_conv_o, v_w_pool_g, v_s_pool, v_w_pool_o, v_b_gate, v_w_out, v_w_mlp1, v_w_mlp2):
    a = dict(locals())
    for n in ('w_in', 'm_w_in', 'v_w_in'):
        a[n] = jnp.swapaxes(a[n], 1, 2)
    big_axis = dict(BIG, w_in=1)
    depth = w_in.shape[0]
    d = x.shape[-1]
    seq, nctx_rows = x.shape[1], ctx.shape[1]
    dm = types.SimpleNamespace(
        D=d, SEQ=seq, CTX=nctx_rows, T=seq + nctx_rows, DK=d // 8, DV=d // 4, GK=d // 2, GC=d // 8,
        tm=_tile(nctx_rows, (256, 128, 64)), TB=_tile(nctx_rows, (256, 128, 64)))
    assert dm.SEQ % dm.tm == 0 and dm.SEQ % GRID_W == 0 and dm.CTX % GLA_CHUNK == 0
    tmw = min(dm.tm, 128)
    chip = 2 * lax.axis_index("x") + lax.axis_index("y")
    core = lax.axis_index("c")
    chip1 = chip.astype(jnp.int32).reshape(1)
    core1 = core.astype(jnp.int32).reshape(1)

    big_names, small_names = list(BIG), list(SMALL_SHARDED)
    nbig = len(big_names)
    kinds = ['col' if big_axis[n] == 2 else 'row' for n in big_names]
    wl = w_in.shape[2]
    wlp = _lane_pad(wl)

    def rows8(t):
        t = t.reshape(t.shape[0], -1, t.shape[-1])
        return jnp.pad(t, ((0, 0), (0, -t.shape[1] % 8), (0, 0)))

    def halves(t):
        return t.reshape(2, t.shape[0] // 2, t.shape[1])

    def layer_src(l, tok=None):
        def one(n):
            t = a[n][l] if tok is None else a[n][l] + tok
            return halves((jnp.pad(t, ((0, wlp - wl), (0, 0))) if n == 'w_in' else t).astype(MM_DTYPE))
        return [one(n) for n in big_names]

    def whole(t):
        return t.reshape(-1, t.shape[-1])

    def start_gather(srcs, knds, after, name):
        plan = _gather_plan(knds, [t.shape[2] for t in srcs])
        lands = [lax.empty(_gathered_shape(t, k), t.dtype) for t, k in zip(srcs, knds)]
        return (plan,) + start_copies(srcs, lands, plan, 4 * len(srcs), after, name)

    late = [big_names.index(n) for n in ('w_gla_o', 'w_conv_o', 'w_pool_o', 'w_out', 'w_mlp1', 'w_mlp2')]
    early = [k for k in range(nbig) if k not in late]
    src0 = layer_src(0)
    kinds_e = [kinds[k] for k in early] + ['col'] * len(small_names)
    age = start_gather([src0[k] for k in early] + [rows8(a[n]) for n in small_names], kinds_e, core1,
                       "gather_layer0_start")
    tok0 = age[-1][0, 0]
    src1 = layer_src(1, tok0)
    pk = lambda pre: _flatten_pad([a[pre + n] + tok0 for n in SMALL], F32)
    small_w, small_m, small_v = pk(''), pk('m_'), pk('v_')
    X = jnp.concatenate([ctx[0] + tok0, x[0] + tok0], axis=0)
    ready = (small_w[0, 0] + small_m[0, 0] + small_v[0, 0] + X[0, 0]
             + sum(t[0, 0, 0].astype(F32) for t in src1)).reshape(1, 1)
    _, g0 = wait_copies(age[1], age[2], age[3], age[4], age[0], ready, "gather_layer0_wait")
    g0 = forward_halves(g0, kinds_e, "gather_layer0_forward")
    ag0 = start_gather([src0[k] for k in late], [kinds[k] for k in late], g0[0], "gather_layer0_late_start")
    ag1 = start_gather(src1, kinds, ag0[-1], "gather_layer1_start")
    ag_token = ag1[-1]
    full = {n: [None, None] for n in big_names}
    for k, t in zip(early, g0):
        full[big_names[k]][0] = whole(t)
    for n, g in zip(small_names, g0[len(early):]):
        shp = a[n].shape
        full[n] = g[:, :math.prod(shp[1:-1])].reshape(shp[:-1] + (4 * shp[-1],))
    for n in SMALL:
        if n not in SMALL_SHARDED:
            full[n] = a[n]

    cvec = jnp.concatenate([c_ctx.reshape(1, d), c.reshape(1, d), jnp.zeros((6, d), F32)], axis=0)
    avec = (cvec * jax.nn.sigmoid(cvec) + ag_token[0, 0]).astype(MM_DTYPE)

    def row(v):
        return v.reshape(1, -1)

    saved = []
    gk, gv = dm.GK, d
    lrblk = (7 * d + d // 2) // LANES
    for l in range(depth):
        if l == 1:
            _, got = wait_copies(fw1[0], fw1[1], fw1[2], fw1[3], fw1_plan, X, "gather_layer1_forward_wait")
            for n, t in zip(big_names, got):
                full[n][1] = whole(t)
        s = types.SimpleNamespace()
        s.w_in_p = _w_in_t_to_proj(full['w_in'][l], d, wl, wlp)
        wd = full['w_decay'][l]
        wdp = jnp.zeros((LANES, 2 * gk), F32)
        wdp = wdp.at[:GLA_LR, :gk].set(wd[0]).at[GLA_LR:2 * GLA_LR, gk:].set(wd[1])
        s.wdp = wdp.astype(MM_DTYPE)
        s.wdp_wide = jnp.pad(s.wdp, ((0, d // 2 - LANES), (0, 0)))
        s.bd = full['b_decay'][l].reshape(1, 2 * gk)
        modraw = matmul(avec, full['w_ada'][l], 'nn', F32, f"mod_{l}") + full['b_ada'][l][None, :]
        s.mod = [modraw[0:2, j * d:(j + 1) * d].reshape(2, 1, d) for j in range(6)]
        s.x = X
        (s.h,) = rowwise(pre_fn, [X], s.mod[0:2], [row(g_pre_mix[l])], [(d, MM_DTYPE)], dm, f"pre_{l}")
        s.P = matmul(s.h, s.w_in_p, 'nt', MM_DTYPE, f"in_proj_{l}")
        P = s.P
        s.z = matmul((P, LANES, lrblk), s.wdp, 'nn', F32, f"decay_proj_{l}", tk=LANES)
        la_f, la_b = rowwise(decay_fn, [s.z], [], [s.bd], [(gk, F32), (gk, F32)], dm, f"decay_{l}")
        s.la = jnp.concatenate([la_f, la_b], axis=1)
        s.o_f, s.st_f = gla_fwd(P, s.la, False, dm, f"gla_fwd_f_{l}")
        s.o_b, s.st_b = gla_fwd(P, s.la, True, dm, f"gla_fwd_b_{l}")
        (s.gin,) = rowwise(glaout_fn, [s.o_f, s.o_b, (P, d, 3)], [], [row(g_gla[l])], [(gv, MM_DTYPE)], dm,
                           f"gla_out_{l}")
        if l == 0:
            _, got = wait_copies(ag0[1], ag0[2], ag0[3], ag0[4], ag0[0], s.gin, "gather_layer0_late_wait")
            got = forward_halves(got, [kinds[k] for k in late], "gather_layer0_late_forward")
            for k, t in zip(late, got):
                full[big_names[k]][0] = whole(t)
        s.ya = matmul(s.gin, full['w_gla_o'][l], 'nn', MM_DTYPE, f"gla_o_{l}")
        (s.u,) = rowwise(glu_fn, [(P, d, 6)], [], [], [(d // 2, F32)], dm, f"glu_{l}")
        s.yconv = conv_fwd(s.u, full['w_dw'][l], dm, f"conv_{l}")
        (s.cin,) = rowwise(convpost_fn, [s.yconv], [], [row(b_dw[l]), row(g_conv_ln[l]), row(b_conv_ln[l])],
                           [(d // 2, MM_DTYPE)], dm, f"conv_post_{l}")
        s.yb = matmul(s.cin, full['w_conv_o'][l], 'nn', MM_DTYPE, f"conv_o_{l}")
        s.pm = pool_mix((P, d // 2, 14), False, dm, f"pool_mix_{l}")
        s.pc = group_mm(s.pm, w_pool_g[l], 'nn', F32, f"pool_g_{l}")
        (s.pin,) = rowwise(poolpost_fn, [s.pc], [], [row(s_pool[l])], [(d // 2, MM_DTYPE)], dm, f"pool_post_{l}")
        s.yc = matmul(s.pin, full['w_pool_o'][l], 'nn', MM_DTYPE, f"pool_o_{l}")
        s.bg = [row(full['b_gate'][l][j]) for j in range(3)]
        (s.mixed,) = rowwise(merge_fn, [s.ya, s.yb, s.yc, (P, 3 * d, 0)], [], s.bg, [(d, MM_DTYPE)], dm,
                             f"merge_{l}", tm=tmw)
        s.y = matmul(s.mixed, full['w_out'][l], 'nn', MM_DTYPE, f"out_proj_{l}")
        s.x1, s.h2 = rowwise(mid_fn, [X, s.y], s.mod[2:5], [row(g_post_mix[l]), row(g_pre_mlp[l])],
                             [(d, F32), (d, MM_DTYPE)], dm, f"mid_{l}")
        if l == 0:
            _, got1 = wait_copies(ag1[1], ag1[2], ag1[3], ag1[4], ag1[0], s.h2, "gather_layer1_wait")
            fw1_plan = _forward_plan(kinds, [t.shape[-1] // 4 if k == 'col' else t.shape[-1]
                                             for t, k in zip(got1, kinds)])
            fw1 = start_copies([], got1, fw1_plan, 3 * nbig, core1, "gather_layer1_forward_start")
        s.act = matmul(s.h2, full['w_mlp1'][l], 'nn', MM_DTYPE, f"mlp1_{l}", epi=relu2_epi)
        s.y2 = matmul(s.act, full['w_mlp2'][l], 'nn', MM_DTYPE, f"mlp2_{l}")
        (X,) = rowwise(post_fn, [s.x1, s.y2], s.mod[5:6], [row(g_post_mlp[l])], [(d, F32)], dm, f"post_{l}")
        saved.append(s)

    dX, lossv = loss_head(X, loss_target[0], dm, "loss_head")
    loss = lax.psum(lossv[0, 0], ("x", "y", "c"))

    grads = {n: [None] * depth for n in WEIGHTS if n != 'c_ctx' and n not in BIG}
    gbig = {n: [None] * depth for n in BIG}
    rs_token = None
    where = jnp.concatenate([chip1, core1])

    def swap_plan(src, land, x, y, c):
        return [(src[n], land[n], (x, y, 1 - c), land[n]) for n in range(len(src))]


    def start_scatter(idx, layer, after, name):
        gs = [gbig[big_names[k]][layer] for k in idx]
        wd = [t.shape[1] // 4 if kinds[k] == 'col' else t.shape[0] // 4 for t, k in zip(gs, idx)]
        plan = _scatter_plan([big_axis[big_names[k]] - 1 for k in idx], wd)
        lands = [lax.empty((3, t.shape[0], w) if kinds[k] == 'col' else (3, w, t.shape[1]), t.dtype)
                 for t, w, k in zip(gs, wd, idx)]
        return (plan,) + start_copies(gs, lands, plan, 3 * len(gs), after, name)

    g_cctx = jnp.zeros((d,), F32)
    for l in reversed(range(depth)):
        s = saved[l]
        P = s.P
        dmod = [None] * 6
        gpm = row(g_post_mlp[l]) if rs_token is None else row(g_post_mlp[l]) + rs_token[0, 0]
        (dx1, dy2), (dmod[5],), (dg,) = rowwise_vjp(post_fn, [s.x1, s.y2], s.mod[5:6], [gpm], [dX],
                                                     dm, f"post_bwd_{l}", narrow=(1,))
        grads['g_post_mlp'][l] = dg[0]
        du1 = matmul(dy2, full['w_mlp2'][l], 'nt', MM_DTYPE, f"mlp2_dx_{l}", epi=relu2_bwd_epi, extras=[s.act])
        gbig['w_mlp2'][l] = matmul(s.act, dy2, 'tn', MM_DTYPE, f"mlp2_dw_{l}")
        dh2 = matmul(du1, full['w_mlp1'][l], 'nt', MM_DTYPE, f"mlp1_dx_{l}")
        gbig['w_mlp1'][l] = matmul(s.h2, du1, 'tn', MM_DTYPE, f"mlp1_dw_{l}")
        gpx = row(g_post_mix[l])
        (dxa, dy), dmod[2:5], (dg1, dg2) = rowwise_vjp(
            mid_fn, [s.x, s.y], s.mod[2:5], [gpx, row(g_pre_mlp[l])], [dx1, dh2], dm, f"mid_bwd_{l}", narrow=(1,))
        grads['g_post_mix'][l], grads['g_pre_mlp'][l] = dg1[0], dg2[0]
        dmixed = matmul(dy, full['w_out'][l], 'nt', MM_DTYPE, f"out_proj_dx_{l}")
        gbig['w_out'][l] = matmul(s.mixed, dy, 'tn', MM_DTYPE, f"out_proj_dw_{l}")
        (dya, dyb, dyc, dP), _, dbg = rowwise_vjp(merge_fn, [s.ya, s.yb, s.yc, (P, 3 * d, 0)], [], s.bg, [dmixed],
                                                  dm, f"merge_bwd_{l}", tm=tmw, narrow=(0, 1, 2),
                                                  into=(3, None, P.shape))
        grads['b_gate'][l] = jnp.concatenate(dbg, axis=0)
        dgin = matmul(dya, full['w_gla_o'][l], 'nt', MM_DTYPE, f"gla_o_dx_{l}")
        gbig['w_gla_o'][l] = matmul(s.gin, dya, 'tn', MM_DTYPE, f"gla_o_dw_{l}")
        dcin = matmul(dyb, full['w_conv_o'][l], 'nt', MM_DTYPE, f"conv_o_dx_{l}")
        gbig['w_conv_o'][l] = matmul(s.cin, dyb, 'tn', MM_DTYPE, f"conv_o_dw_{l}")
        dpin = matmul(dyc, full['w_pool_o'][l], 'nt', MM_DTYPE, f"pool_o_dx_{l}")
        gbig['w_pool_o'][l] = matmul(s.pin, dyc, 'tn', MM_DTYPE, f"pool_o_dw_{l}")
        sp = row(s_pool[l])
        if l == 0:
            rs0 = start_scatter(late, 0, dpin, "grad_layer0_late_start")
            sp = sp + rs0[-1][0, 0]
        (dpc,), _, (dsp,) = rowwise_vjp(poolpost_fn, [s.pc], [], [sp], [dpin], dm, f"pool_post_bwd_{l}")
        grads['s_pool'][l] = dsp[0]
        grads['w_pool_g'][l] = group_mm(s.pm, w_pool_g[l], 'tn', F32, f"pool_g_dw_{l}", b=dpc)
        dpm = group_mm(dpc, w_pool_g[l], 'nt', F32, f"pool_g_dx_{l}")
        dP = pool_mix(dpm, True, dm, f"pool_mix_bwd_{l}", into=(dP, 14))
        (dyconv,), _, (dbdw, dgln, dbln) = rowwise_vjp(
            convpost_fn, [s.yconv], [], [row(b_dw[l]), row(g_conv_ln[l]), row(b_conv_ln[l])], [dcin], dm,
            f"conv_post_bwd_{l}")
        grads['b_dw'][l], grads['g_conv_ln'][l], grads['b_conv_ln'][l] = dbdw[0], dgln[0], dbln[0]
        du, grads['w_dw'][l] = conv_bwd(s.u, full['w_dw'][l], dyconv, dm, f"conv_bwd_{l}")
        (dP,), _, _ = rowwise_vjp(glu_fn, [(P, d, 6)], [], [], [du], dm, f"glu_bwd_{l}", into=(0, dP, P.shape))
        (do, _, dP), _, (dgg,) = rowwise_vjp(glaout_fn, [s.o_f, s.o_b, (P, d, 3)], [], [row(g_gla[l])], [dgin], dm,
                                             f"gla_out_bwd_{l}", want=[True, False, True], into=(2, dP, P.shape), narrow=(0,))
        grads['g_gla'][l] = dgg[0]
        dqf, dkf, dvf, dlaf = gla_bwd(P, s.la, do, s.st_f, False, dm, f"gla_bwd_f_{l}")
        dP, dlab = gla_bwd(P, s.la, do, s.st_b, True, dm, f"gla_bwd_b_{l}", prev=(dqf, dkf, dvf), into=dP)
        (dz,), _, (dbd,) = rowwise_vjp(decay_fn, [s.z], [], [s.bd], [dlaf, dlab], dm, f"decay_bwd_{l}", narrow=(0,))
        grads['b_decay'][l] = dbd.reshape(2, gk)
        dwdp = matmul((P, LANES, lrblk), dz, 'tn', F32, f"decay_proj_dw_{l}", tm=LANES)
        grads['w_decay'][l] = jnp.stack([dwdp[:GLA_LR, :gk], dwdp[GLA_LR:2 * GLA_LR, gk:]])
        dP = matmul(dz, s.wdp_wide, 'nt', MM_DTYPE, f"decay_proj_dx_{l}", into=(dP, 15))
        gpre = row(g_pre_mix[l])
        if l == 0:
            gs1, got1 = wait_copies(rs1[1], rs1[2], rs1[3], rs1[4], rs1[0], dP, "grad_layer1_wait")
            sa1 = [chip_add(g, r, big_axis[n] - 1, where, f"grad_layer1_add_{n}", slab=False)
                   for n, g, r in zip(big_names, gs1, got1)]
            swp1 = start_copies(sa1, [lax.empty(t.shape, t.dtype) for t in sa1], swap_plan, len(sa1), core1,
                                "grad_layer1_pair_swap_start")
            gpre = gpre + swp1[-1][0, 0]
        dh = matmul(dP, s.w_in_p, 'nn', MM_DTYPE, f"in_proj_dx_{l}")
        gbig['w_in'][l] = _proj_to_w_in_t(matmul(dP, s.h, 'tn', MM_DTYPE, f"in_proj_dw_{l}"), d, wl, wlp)
        (dX,), dmod[0:2], (dg,) = rowwise_vjp(pre_fn, [s.x], s.mod[0:2], [gpre], [dh], dm,
                                               f"pre_bwd_{l}", adds={0: dxa})
        grads['g_pre_mix'][l] = dg[0]
        dmodflat = jnp.concatenate([jnp.concatenate([m_.reshape(2, d) for m_ in dmod], axis=1),
                                    jnp.zeros((6, 6 * d), F32)], axis=0)
        grads['b_ada'][l] = dmodflat[0] + dmodflat[1]
        gbig['w_ada'][l] = matmul(avec, dmodflat, 'tn', MM_DTYPE, f"ada_dw_{l}")
        dav = matmul(dmodflat, full['w_ada'][l], 'nt', F32, f"ada_dx_{l}")
        g_cctx = g_cctx + dav[0] * _silu_grad(c_ctx)
        if l == 1:
            rs1 = start_scatter(list(range(nbig)), 1, dav, "grad_layer1_start")
            rs_token = rs1[-1]

    grad_x = dX[dm.CTX:][None]
    gfull = {n: jnp.stack(v) for n, v in grads.items()}
    gfull['c_ctx'] = g_cctx

    def halves_view(t, k):
        return t.reshape(2, t.shape[0] // 2, t.shape[1]) if k == 'col' else t.reshape(4, 2, t.shape[0] // 8, t.shape[1])
    enames = [big_names[k] for k in early]
    ekinds = [kinds[k] for k in early]
    v0 = [halves_view(gbig[n][0], k) for n, k in zip(enames, ekinds)]
    r1 = pair_swap_halves(v0, ekinds, "grad_pair_swap")
    hs = [pair_add(v.reshape((-1,) + v.shape[-2:]), r.reshape((-1,) + r.shape[-2:]), core1, f"grad_pair_add_{n}")
          for n, v, r in zip(enames, v0, r1)]
    hx = [h.reshape(h.shape[1:]) if k == 'col' else h for h, k in zip(hs, ekinds)]
    ex_plan = _exchange_plan(ekinds)
    ex_lands = [lax.empty((3, h.shape[0], h.shape[1] // 4) if k == 'col' else (3,) + h.shape[1:], h.dtype)
                for h, k in zip(hx, ekinds)]
    ex = (ex_plan,) + start_copies(hx, ex_lands, ex_plan, 3 * len(hx), core1, "grad_chip_exchange_start")

    gs0, got0 = wait_copies(rs0[1], rs0[2], rs0[3], rs0[4], rs0[0], ex[-1], "grad_layer0_late_wait")
    sa = [chip_add(g, r, big_axis[big_names[k]] - 1, where, f"grad_layer0_add_{big_names[k]}", slab=False)
          for k, g, r in zip(late, gs0, got0)]
    sflat = _flatten_pad([gfull[n].astype(F32) for n in SMALL], F32)
    sv = sflat.reshape(2, sflat.shape[0] // 2, LANES)
    (sr,) = pair_swap_halves([sv], ['col'], "small_grad_pair_swap")
    sh = pair_add(sv, sr[None], core1, "small_grad_pair_add")[0]
    sq = quad_sum(sh, chip_broadcast(sh, "small_grad_chip_exchange"), core1, "small_grad_chip_sum")
    (ssum,) = pair_join_layers([sq], "small_grad_pair_join")

    swp = start_copies(sa, [lax.empty(t.shape, t.dtype) for t in sa], swap_plan, len(sa), ssum,
                       "grad_late_pair_swap_start")
    ssum = ssum.reshape(-1) + swp[-1][0, 0]
    start = 0
    sg = {}
    for n in SMALL:
        cnt = gfull[n].size
        g = ssum[start:start + cnt].reshape(gfull[n].shape)
        start += cnt
        if n in SMALL_SHARDED:
            ax = SMALL_SHARDED[n]
            wdt = a[n].shape[ax]
            g = lax.dynamic_slice_in_dim(g, chip * wdt, wdt, axis=ax)
        sg[n] = g
    gs = _flatten_pad([sg[n] for n in SMALL], F32)
    dl, mn, vn = adamw(small_w, gs, small_m, small_v, "adamw_small")

    sa, sb = wait_copies(swp[0], swp[1], swp[2], swp[3], swap_plan, dl, "grad_late_pair_swap_wait")
    sa1, sb1 = wait_copies(swp1[0], swp1[1], swp1[2], swp1[3], swap_plan, dl, "grad_layer1_pair_swap_wait")
    red0 = {big_names[k]: [sa[j], sb[j]] for j, k in enumerate(late)}
    red1 = {n: [sa1[k], sb1[k]] for k, n in enumerate(big_names)}

    out_g, out_d, out_m, out_v = {}, {}, {}, {}

    def update_big(n, terms, **kw):
        res = adamw_layers(a[n], a['m_' + n], a['v_' + n], terms, f"adamw_{n}" + ("" if not kw else f"_{kw['layer']}"), **kw)
        out_g[n], out_d[n], out_m[n], out_v[n] = res
        return res
    for k in late:
        update_big(big_names[k], [red0[big_names[k]], red1[big_names[k]]])
    half_done = {n: update_big(n, {1: red1[n]}, layer=1) for n in enames}
    done = (dl[0, 0] + sum(out_d[n][1, 0, 0] for n in big_names)).reshape(1, 1)
    hx, r2 = wait_copies(ex[1], ex[2], ex[3], ex[4], ex[0], done, "grad_chip_exchange_wait")
    dl, mn, vn = dl.reshape(-1), mn.reshape(-1), vn.reshape(-1)
    start = 0
    for n in SMALL:
        cnt, shp = a[n].size, a[n].shape
        out_g[n] = sg[n]
        out_d[n], out_m[n], out_v[n] = (t[start:start + cnt].reshape(shp) for t in (dl, mn, vn))
        start += cnt
    fs = [chip_add(h.reshape(-1, h.shape[-1]), r, big_axis[n] - 1, where, f"grad_chip_add_{n}")
          for n, h, r in zip(enames, hx, r2)]
    for n, t in zip(enames, pair_join_layers(fs, "grad_pair_join")):
        update_big(n, {0: [t.reshape(-1, t.shape[-1])]}, layer=0, prev=tuple(half_done[n]))
    for dct in (out_g, out_d, out_m, out_v):
        dct['w_in'] = jnp.swapaxes(dct['w_in'], 1, 2)
    return (loss, grad_x, *[out_g[n] for n in WEIGHTS], *[out_d[n] for n in WEIGHTS],
            *[out_m[n] for n in WEIGHTS], *[out_v[n] for n in WEIGHTS])
```

```python
import math
import types

import jax
import jax.numpy as jnp
from jax import lax
from jax.experimental import pallas as pl
from jax.experimental.pallas import tpu as pltpu

F32 = jnp.float32
MM_DTYPE = jnp.bfloat16
VMEM_LIMIT_V7X = 56 * 1024 * 1024
LANES = 128
EPS = 1e-6

N_HEADS = 4
GLA_CHUNK = 64
GLA_TAU = 16.0
GLA_LR = 16
GRID_W = 64
POOL_WINDOWS = (2, 4, 8, 16)

ADAM_LR = 0.001
ADAM_B1 = 0.9
ADAM_B2 = 0.999
ADAM_EPS = 1e-08
ADAM_WD = 0.01
ADAM_STEP = 10

NN = (((1,), (0,)), ((), ()))
NT = (((1,), (1,)), ((), ()))
TN = (((0,), (0,)), ((), ()))

WEIGHTS = ['c_ctx', 'w_ada', 'b_ada', 'g_pre_mix', 'g_post_mix', 'g_pre_mlp', 'g_post_mlp', 'w_in', 'w_decay',
           'b_decay', 'g_gla', 'w_gla_o', 'w_dw', 'b_dw', 'g_conv_ln', 'b_conv_ln', 'w_conv_o', 'w_pool_g',
           's_pool', 'w_pool_o', 'b_gate', 'w_out', 'w_mlp1', 'w_mlp2']
BIG = {'w_ada': 2, 'w_in': 2, 'w_gla_o': 1, 'w_conv_o': 2, 'w_pool_o': 2, 'w_out': 1, 'w_mlp1': 2, 'w_mlp2': 1}
SMALL_SHARDED = {'w_decay': 3, 'b_decay': 2, 'w_dw': 2, 'b_gate': 2}
SMALL = [n for n in WEIGHTS if n not in BIG]


def _tile(n, prefs):
    for t in prefs:
        if n % t == 0:
            return t
    return n


def _cparams(sem=None, **kw):
    return pltpu.CompilerParams(dimension_semantics=sem, vmem_limit_bytes=VMEM_LIMIT_V7X, **kw)


def _dot(a, b, dims=NN):
    return lax.dot_general(a.astype(MM_DTYPE), b.astype(MM_DTYPE), dims, preferred_element_type=F32)


def matmul(a, b, mode, out_dtype, name, tm=None, tn=None, tk=None, epi=None, extras=(), into=None):
    a, aw, ablk = a if isinstance(a, tuple) else (a, a.shape[1], 0)
    if mode == 'nn':
        M, K, N = a.shape[0], aw, b.shape[1]
    elif mode == 'nt':
        M, K, N = a.shape[0], aw, b.shape[0]
    else:
        K, M, N = a.shape[0], aw, b.shape[1]
    big = (1088, 1024, 640, 544, 512, 320, 256, 128, 64, 32, 16, 8)
    if mode == 'tn':
        tm = tm or _tile(M, (1024, 512, 256, 128))
        tn = tn or _tile(N, (1024, 512, 256, 128))
        tk = tk or _tile(K, big)
    else:
        tm = tm or _tile(M, big)
        tn = tn or _tile(N, (1024, 512, 256, 128))
        tk = tk or _tile(K, (1024, 512, 256, 128))
    if aw != a.shape[1]:
        assert (mode == 'tn' and tm == aw) or (mode != 'tn' and tk == aw)
    nk = K // tk
    ne = len(extras)
    dims = {'nn': NN, 'nt': NT, 'tn': TN}[mode]

    def body(a_ref, b_ref, *rest):
        e_refs, o_ref = rest[:ne], rest[ne + (into is not None)]

        def finish(acc):
            if epi is not None:
                acc = epi(acc, *[e[...] for e in e_refs])
            o_ref[...] = acc.astype(o_ref.dtype)

        p = _dot(a_ref[...], b_ref[...], dims)
        if nk == 1:
            finish(p)
            return
        acc = rest[-1]
        k = pl.program_id(2)

        @pl.when(k == 0)
        def _():
            acc[...] = p

        @pl.when(k > 0)
        def _():
            acc[...] += p

        @pl.when(k == nk - 1)
        def _():
            finish(acc[...])

    if mode == 'nn':
        a_spec = pl.BlockSpec((tm, tk), lambda i, j, k: (i, k + ablk))
        b_spec = pl.BlockSpec((tk, tn), lambda i, j, k: (k, j))
    elif mode == 'nt':
        a_spec = pl.BlockSpec((tm, tk), lambda i, j, k: (i, k + ablk))
        b_spec = pl.BlockSpec((tn, tk), lambda i, j, k: (j, k))
    else:
        a_spec = pl.BlockSpec((tk, tm), lambda i, j, k: (k, i + ablk))
        b_spec = pl.BlockSpec((tk, tn), lambda i, j, k: (k, j))
    tile = pl.BlockSpec((tm, tn), lambda i, j, k: (i, j))
    if into is None:
        out_spec, out_shape, more, extra, aliases = tile, jax.ShapeDtypeStruct((M, N), out_dtype), [], [], {}
    else:
        buf, oblk = into
        out_spec = pl.BlockSpec((tm, tn), lambda i, j, k: (i, oblk * (N // tn) + j))
        out_shape = jax.ShapeDtypeStruct(buf.shape, buf.dtype)
        more, extra, aliases = [pl.BlockSpec(memory_space=pl.ANY)], [buf], {2 + ne: 0}
    return pl.pallas_call(
        body, name=name, grid=(M // tm, N // tn, nk),
        in_specs=[a_spec, b_spec] + [tile] * ne + more, out_specs=out_spec,
        out_shape=out_shape, input_output_aliases=aliases,
        scratch_shapes=[] if nk == 1 else [pltpu.VMEM((tm, tn), F32)],
        compiler_params=_cparams(("parallel", "parallel", "arbitrary")),
    )(a, b, *extras, *extra)


def group_mm(a, w, mode, out_dtype, name, b=None):
    T = a.shape[0]
    G, gc, _ = w.shape
    col = pl.BlockSpec((T, gc), lambda g: (0, g))
    wsp = pl.BlockSpec((1, gc, gc), lambda g: (g, 0, 0))
    if mode == 'tn':
        def body(a_ref, b_ref, o_ref):
            o_ref[0] = _dot(a_ref[...], b_ref[...], TN).astype(o_ref.dtype)
        return pl.pallas_call(body, name=name, grid=(G,), in_specs=[col, col], out_specs=wsp,
                              out_shape=jax.ShapeDtypeStruct((G, gc, gc), out_dtype),
                              compiler_params=_cparams(("parallel",)))(a, b)
    dims = NN if mode == 'nn' else NT

    def body(a_ref, w_ref, o_ref):
        o_ref[...] = _dot(a_ref[...], w_ref[0], dims).astype(o_ref.dtype)
    return pl.pallas_call(body, name=name, grid=(G,), in_specs=[col, wsp], out_specs=col,
                          out_shape=jax.ShapeDtypeStruct((T, G * gc), out_dtype),
                          compiler_params=_cparams(("parallel",)))(a, w)


def _rowspec(r):
    return r if isinstance(r, tuple) else (r, r.shape[1], 0)


def _row_specs(rows, segs, consts, tm, nctx):
    specs = [pl.BlockSpec((tm, w), lambda i, b=b: (i, b)) for _, w, b in rows]
    specs += [pl.BlockSpec((1,) + s.shape[1:], lambda i, n=s.ndim: (jnp.where(i >= nctx, 1, 0),) + (0,) * (n - 1))
              for s in segs]
    specs += [pl.BlockSpec(c.shape, lambda i, n=c.ndim: (0,) * n) for c in consts]
    return specs


def rowwise(fn, rows, segs, consts, outs, dm, name, tm=None):
    tm = tm or dm.tm
    nctx = dm.CTX // tm
    rows = [_rowspec(r) for r in rows]
    nr, ns, nc = len(rows), len(segs), len(consts)

    def body(*refs):
        rin = [r[...] for r in refs[:nr]]
        sin = [s[0] for s in refs[nr:nr + ns]]
        cin = [c[...] for c in refs[nr + ns:nr + ns + nc]]
        res = fn(*rin, *sin, *cin)
        for o_ref, v in zip(refs[nr + ns + nc:], res):
            o_ref[...] = v.astype(o_ref.dtype)

    res = pl.pallas_call(
        body, name=name, grid=(dm.T // tm,),
        in_specs=_row_specs(rows, segs, consts, tm, nctx),
        out_specs=[pl.BlockSpec((tm, w), lambda i: (i, 0)) for w, _ in outs],
        out_shape=[jax.ShapeDtypeStruct((dm.T, w), dt) for w, dt in outs],
        compiler_params=_cparams(("parallel",)),
    )(*[r[0] for r in rows], *segs, *consts)
    return res


def rowwise_vjp(fn, rows, segs, consts, cots, dm, name, tm=None, want=None, adds=None, narrow=(), into=None):
    tm = tm or dm.tm
    nctx = dm.CTX // tm
    rows = [_rowspec(r) for r in rows]
    cots = [_rowspec(r) for r in cots]
    adds = adds or {}
    nr, ns, nc, nct = len(rows), len(segs), len(consts), len(cots)
    want = want or [True] * nr
    widx = [k for k in range(nr) if want[k]]
    akeys = sorted(adds)

    def body(*refs):
        i = pl.program_id(0)
        rin = [r[...] for r in refs[:nr]]
        sin = [s[0] for s in refs[nr:nr + ns]]
        cin = [c[...] for c in refs[nr + ns:nr + ns + nc]]
        p = nr + ns + nc
        cot_refs = refs[p:p + nct]
        add_refs = dict(zip(akeys, refs[p + nct:p + nct + len(akeys)]))
        p = p + nct + len(akeys) + (1 if (into is not None and into[1] is not None) else 0)
        rg_refs = refs[p:p + len(widx)]
        sg_refs = refs[p + len(widx):p + len(widx) + ns]
        cg_refs = refs[p + len(widx) + ns:]
        res, vjp = jax.vjp(fn, *rin, *sin, *cin)
        g = vjp(tuple(cr[...].astype(o.dtype) for cr, o in zip(cot_refs, res)))
        for o_ref, k in zip(rg_refs, widx):
            v = g[k].astype(F32)
            if k in add_refs:
                v = v + add_refs[k][...]
            o_ref[...] = v.astype(o_ref.dtype)
        first_seg = jnp.logical_or(i == 0, i == nctx)
        for o_ref, v in zip(sg_refs, g[nr:nr + ns]):
            @pl.when(first_seg)
            def _(o_ref=o_ref, v=v):
                o_ref[0] = v.astype(F32)

            @pl.when(jnp.logical_not(first_seg))
            def _(o_ref=o_ref, v=v):
                o_ref[0] += v.astype(F32)
        for o_ref, v in zip(cg_refs, g[nr + ns:]):
            @pl.when(i == 0)
            def _(o_ref=o_ref, v=v):
                o_ref[...] = v.astype(F32)

            @pl.when(i > 0)
            def _(o_ref=o_ref, v=v):
                o_ref[...] += v.astype(F32)

    in_specs = _row_specs(rows, segs, consts, tm, nctx)
    in_specs += [pl.BlockSpec((tm, w), lambda i, b=b: (i, b)) for _, w, b in cots]
    in_specs += [pl.BlockSpec((tm, adds[k].shape[1]), lambda i: (i, 0)) for k in akeys]
    out_specs = [pl.BlockSpec((tm, rows[k][1]), lambda i: (i, 0)) for k in widx]
    out_shape = [jax.ShapeDtypeStruct((dm.T, rows[k][1]), MM_DTYPE if k in narrow else rows[k][0].dtype)
                 for k in widx]
    extra, aliases = [], {}
    if into is not None:
        ik, ibuf, ishape = into
        out_specs[widx.index(ik)] = pl.BlockSpec((tm, rows[ik][1]), lambda i, b=rows[ik][2]: (i, b))
        out_shape[widx.index(ik)] = jax.ShapeDtypeStruct(ishape, MM_DTYPE)
        if ibuf is not None:
            aliases = {len(in_specs): widx.index(ik)}
            in_specs = in_specs + [pl.BlockSpec(memory_space=pl.ANY)]
            extra = [ibuf]
    out_specs += [pl.BlockSpec((1,) + s.shape[1:], lambda i, n=s.ndim: (jnp.where(i >= nctx, 1, 0),) + (0,) * (n - 1))
                  for s in segs]
    out_shape += [jax.ShapeDtypeStruct(s.shape, F32) for s in segs]
    out_specs += [pl.BlockSpec(c.shape, lambda i, n=c.ndim: (0,) * n) for c in consts]
    out_shape += [jax.ShapeDtypeStruct(c.shape, F32) for c in consts]
    res = pl.pallas_call(
        body, name=name, grid=(dm.T // tm,), in_specs=in_specs, out_specs=out_specs, out_shape=out_shape,
        input_output_aliases=aliases, compiler_params=_cparams(("arbitrary",)),
    )(*[r[0] for r in rows], *segs, *consts, *[r[0] for r in cots], *[adds[k] for k in akeys], *extra)
    rg = [None] * nr
    for k, v in zip(widx, res[:len(widx)]):
        rg[k] = v
    return rg, list(res[len(widx):len(widx) + ns]), list(res[len(widx) + ns:])


def _rms(x, g):
    return x * lax.rsqrt(jnp.mean(x * x, axis=-1, keepdims=True) + EPS) * g


def _sigmoid(x):
    return jax.nn.sigmoid(x)


def pre_fn(x, shift, scale, g):
    return ((_rms(x, g) * (1.0 + scale) + shift).astype(MM_DTYPE),)


def mid_fn(x, y, gate, shift, scale, g_post, g_pre):
    x1 = x + gate * _rms(y.astype(F32), g_post)
    return x1, (_rms(x1, g_pre) * (1.0 + scale) + shift).astype(MM_DTYPE)


def post_fn(x1, y2, gate, g):
    return (x1 + gate * _rms(y2.astype(F32), g),)


def relu2_epi(acc):
    r = jnp.maximum(acc, 0.0)
    return r * r


def relu2_bwd_epi(dact, act):
    return dact * (2.0 * jnp.sqrt(act.astype(F32)))


def decay_fn(z, bd):
    zz = z.astype(F32) + bd
    ls = jnp.minimum(zz, 0.0) - jnp.log(1.0 + jnp.exp(jnp.minimum(zz, -zz)))
    la = ls / GLA_TAU
    gk = la.shape[1] // 2
    return la[:, :gk], la[:, gk:]


def glu_fn(ab):
    h = ab.shape[1] // 2
    return (ab[:, :h].astype(F32) * _sigmoid(ab[:, h:].astype(F32)),)


def glaout_fn(o_f, o_b, og, g):
    o = o_f + o_b
    dv = o.shape[1] // N_HEADS
    hs = []
    for h in range(N_HEADS):
        oh = o[:, h * dv:(h + 1) * dv]
        hs.append(oh * lax.rsqrt(jnp.mean(oh * oh, axis=-1, keepdims=True) + EPS) * g[:, h * dv:(h + 1) * dv])
    og = og.astype(F32)
    return ((jnp.concatenate(hs, axis=1) * (og * _sigmoid(og))).astype(MM_DTYPE),)


def convpost_fn(y, b_dw, g, b):
    y = y + b_dw
    mu = jnp.mean(y, axis=-1, keepdims=True)
    xc = y - mu
    yn = xc * lax.rsqrt(jnp.mean(xc * xc, axis=-1, keepdims=True) + EPS) * g + b
    return ((yn * _sigmoid(yn)).astype(MM_DTYPE),)


def poolpost_fn(pc, s):
    return ((pc.astype(F32) * s).astype(MM_DTYPE),)


def merge_fn(ya, yb, yc, mg, bg0, bg1, bg2):
    d = ya.shape[1]
    mg = mg.astype(F32)
    mixed = (_sigmoid(mg[:, :d] + bg0) * ya.astype(F32) + _sigmoid(mg[:, d:2 * d] + bg1) * yb.astype(F32)
             + _sigmoid(mg[:, 2 * d:] + bg2) * yc.astype(F32))
    return (mixed.astype(MM_DTYPE),)


def _split_dot(lmat, x, dims):
    hi = x.astype(MM_DTYPE)
    lo = x - hi.astype(F32)
    return _dot(lmat, hi, dims) + _dot(lmat, lo, dims)


def _gla_block_order(dm, rev):
    nctx, nb = dm.CTX // dm.TB, dm.T // dm.TB

    def blk(i):
        if not rev:
            return i
        return jnp.where(i < nctx, nctx - 1 - i, nb - 1 - (i - nctx))
    return blk, nb


def _gla_tri(rev):
    c = GLA_CHUNK
    t = lax.broadcasted_iota(jnp.int32, (c, c), 0)
    s = lax.broadcasted_iota(jnp.int32, (c, c), 1)
    return (s >= t) if rev else (s <= t)


def _gla_cumsum(la, tri):
    lmat = tri.astype(MM_DTYPE)
    return lmat, _split_dot(lmat, la, NN), jnp.sum(la, axis=0, keepdims=True)


def _gla_chunk_terms(q, k, b, bend, tri, scale):
    eb = jnp.exp(b)
    enb = jnp.exp(-b)
    ee = jnp.exp(bend - b)
    qi = q * scale * eb
    ki = k * enb
    kend = k * ee
    att = jnp.where(tri, _dot(qi, ki, NT), 0.0)
    return eb, enb, ee, qi, ki, kend, att


def gla_fwd(P, la, rev, dm, name):
    c, tb, h_, dk, dv, d = GLA_CHUNK, dm.TB, N_HEADS, dm.DK, dm.DV, dm.D
    cpb = tb // c
    blk, nb = _gla_block_order(dm, rev)
    gk, gv = h_ * dk, h_ * dv
    qb, kb, vb, lb = (5 * d) // gk, (5 * d + d // 2) // gk, (4 * d) // gv, (1 if rev else 0)
    scale = dk ** -0.5
    order = list(range(cpb))[::-1] if rev else list(range(cpb))

    def body(q_ref, k_ref, v_ref, la_ref, o_ref, s_ref, st):
        @pl.when(pl.program_id(0) == 0)
        def _():
            st[...] = jnp.zeros_like(st)
        tri = _gla_tri(rev)
        terms = {}
        for n, ci in enumerate(order):
            r = pl.ds(ci * c, c)
            _, b_all, bend_all = _gla_cumsum(la_ref[r, :], tri)
            for hh in range(h_):
                ck, cv = pl.ds(hh * dk, dk), pl.ds(hh * dv, dv)
                hs = slice(hh * dk, (hh + 1) * dk)
                v = v_ref[r, cv]
                _, _, _, qi, _, kend, att = _gla_chunk_terms(
                    q_ref[r, ck].astype(F32), k_ref[r, ck].astype(F32), b_all[:, hs], bend_all[:, hs], tri, scale)
                terms[n, hh] = (_dot(att, v), qi.astype(MM_DTYPE), jnp.exp(bend_all[:, hs]), _dot(v, kend, TN))
        for n, ci in enumerate(order):
            r = pl.ds(ci * c, c)
            for hh in range(h_):
                intra, qi, gam, dstate = terms[n, hh]
                s_in = st[hh]
                o_ref[r, pl.ds(hh * dv, dv)] = intra + _dot(qi, s_in, NT)
                s_ref[n, hh] = s_in
                st[hh] = gam * s_in + dstate

    return pl.pallas_call(
        body, name=name, grid=(nb,),
        in_specs=[pl.BlockSpec((tb, gk), lambda i: (blk(i), qb)),
                  pl.BlockSpec((tb, gk), lambda i: (blk(i), kb)),
                  pl.BlockSpec((tb, gv), lambda i: (blk(i), vb)),
                  pl.BlockSpec((tb, gk), lambda i: (blk(i), lb))],
        out_specs=[pl.BlockSpec((tb, gv), lambda i: (blk(i), 0)),
                   pl.BlockSpec((cpb, h_, dv, dk), lambda i: (i, 0, 0, 0))],
        out_shape=[jax.ShapeDtypeStruct((dm.T, gv), F32),
                   jax.ShapeDtypeStruct((dm.T // c, h_, dv, dk), F32)],
        scratch_shapes=[pltpu.VMEM((h_, dv, dk), F32)],
        compiler_params=_cparams(("arbitrary",)),
    )(P, P, P, la)


def gla_bwd(P, la, do, states, rev, dm, name, prev=None, into=None):
    c, tb, h_, dk, dv, d = GLA_CHUNK, dm.TB, N_HEADS, dm.DK, dm.DV, dm.D
    cpb = tb // c
    blk, nb = _gla_block_order(dm, rev)
    gk, gv = h_ * dk, h_ * dv
    qb, kb, vb, lb = (5 * d) // gk, (5 * d + d // 2) // gk, (4 * d) // gv, (1 if rev else 0)
    scale = dk ** -0.5
    order = list(range(cpb))[::-1] if rev else list(range(cpb))

    fused = prev is not None

    def body(q_ref, k_ref, v_ref, la_ref, do_ref, s_ref, *rest):
        if fused:
            pq_ref, pk_ref, pv_ref, _, w_ref, dla_ref, dst = rest
        else:
            dq_ref, dk_ref, dv_ref, dla_ref, dst = rest

        def put(kind, r, cols, val):
            if not fused:
                {'q': dq_ref, 'k': dk_ref, 'v': dv_ref}[kind][r, cols] = val
                return
            p_ref, off = {'q': (pq_ref, gv), 'k': (pk_ref, gv + gk), 'v': (pv_ref, 0)}[kind]
            w_ref[r, pl.ds(off + cols.start, cols.size)] = (val + p_ref[r, cols]).astype(w_ref.dtype)

        @pl.when(pl.program_id(0) == 0)
        def _():
            dst[...] = jnp.zeros_like(dst)
        tri = _gla_tri(rev)
        for n in range(cpb - 1, -1, -1):
            r = pl.ds(order[n] * c, c)
            for hh in range(h_):
                ck, cv = pl.ds(hh * dk, dk), pl.ds(hh * dv, dv)
                q = q_ref[r, ck].astype(F32)
                k = k_ref[r, ck].astype(F32)
                v = v_ref[r, cv]
                lmat, b, bend = _gla_cumsum(la_ref[r, ck], tri)
                eb, enb, ee, qi, ki, kend, att = _gla_chunk_terms(q, k, b, bend, tri, scale)
                s_in = s_ref[n, hh]
                ds_out = dst[hh]
                dob = do_ref[r, cv]
                datt = jnp.where(tri, _dot(dob, v, NT), 0.0)
                dqi = _dot(datt, ki) + _dot(dob, s_in)
                dki = _dot(datt, qi, TN)
                put('v', r, cv, _dot(att, dob, TN) + _dot(kend, ds_out, NT))
                dkend = _dot(v, ds_out)
                gam = jnp.exp(bend)
                dgam = jnp.sum(ds_out * s_in, axis=0, keepdims=True)
                dst[hh] = gam * ds_out + _dot(dob, qi, TN)
                put('q', r, ck, dqi * (scale * eb))
                put('k', r, ck, dki * enb + dkend * ee)
                db = dqi * qi - dki * ki - dkend * kend
                dbend = jnp.sum(dkend * kend, axis=0, keepdims=True) + dgam * gam
                dla_ref[r, ck] = _split_dot(lmat, db, TN) + dbend

    def bi(j):
        return blk(nb - 1 - j)

    in_specs = [
        pl.BlockSpec((tb, gk), lambda j: (bi(j), qb)),
        pl.BlockSpec((tb, gk), lambda j: (bi(j), kb)),
        pl.BlockSpec((tb, gv), lambda j: (bi(j), vb)),
        pl.BlockSpec((tb, gk), lambda j: (bi(j), lb)),
        pl.BlockSpec((tb, gv), lambda j: (bi(j), 0)),
        pl.BlockSpec((cpb, h_, dv, dk), lambda j: (nb - 1 - j, 0, 0, 0)),
    ]
    small = pl.BlockSpec((tb, gk), lambda j: (bi(j), 0))
    wide = pl.BlockSpec((tb, gv), lambda j: (bi(j), 0))
    if not fused:
        return pl.pallas_call(
            body, name=name, grid=(nb,), in_specs=in_specs, out_specs=[small, small, wide, small],
            out_shape=[jax.ShapeDtypeStruct((dm.T, gk), F32), jax.ShapeDtypeStruct((dm.T, gk), F32),
                       jax.ShapeDtypeStruct((dm.T, gv), F32), jax.ShapeDtypeStruct((dm.T, gk), F32)],
            scratch_shapes=[pltpu.VMEM((h_, dv, dk), F32)],
            compiler_params=_cparams(("arbitrary",)),
        )(P, P, P, la, do, states)
    return pl.pallas_call(
        body, name=name, grid=(nb,),
        in_specs=in_specs + [small, small, wide, pl.BlockSpec(memory_space=pl.ANY)],
        out_specs=[pl.BlockSpec((tb, 2 * gv), lambda j: (bi(j), vb // 2)), small],
        out_shape=[jax.ShapeDtypeStruct(into.shape, into.dtype), jax.ShapeDtypeStruct((dm.T, gk), F32)],
        input_output_aliases={9: 0},
        scratch_shapes=[pltpu.VMEM((h_, dv, dk), F32)],
        compiler_params=_cparams(("arbitrary",)),
    )(P, P, P, la, do, states, *prev, into)


def _pos(n, period):
    t = lax.broadcasted_iota(jnp.int32, (n, 1), 0)
    if period & (period - 1) == 0:
        return jnp.bitwise_and(t, period - 1)
    return lax.rem(t, period)


def _conv_segments(dm):
    return [(0, dm.CTX, dm.CTX), (dm.CTX, dm.SEQ, GRID_W)]


def conv_fwd(u, w, dm, name):
    kw, cw = w.shape
    segs = _conv_segments(dm)

    def body(u_ref, w_ref, y_ref):
        for r0, n, per in segs:
            useg = u_ref[r0:r0 + n, :]
            p = _pos(n, per)
            acc = jnp.zeros_like(useg)
            for kk in range(kw):
                d = kk - kw // 2
                sh = useg if d == 0 else pltpu.roll(useg, (-d) % n, 0)
                ok = jnp.logical_and(p + d >= 0, p + d < per)
                acc = acc + jnp.where(ok, sh, 0.0) * w_ref[kk:kk + 1, :]
            y_ref[r0:r0 + n, :] = acc

    return pl.pallas_call(
        body, name=name, grid=(cw // LANES,),
        in_specs=[pl.BlockSpec((dm.T, LANES), lambda j: (0, j)), pl.BlockSpec((kw, LANES), lambda j: (0, j))],
        out_specs=pl.BlockSpec((dm.T, LANES), lambda j: (0, j)),
        out_shape=jax.ShapeDtypeStruct((dm.T, cw), F32),
        compiler_params=_cparams(("parallel",)),
    )(u, w)


def conv_bwd(u, w, dy, dm, name):
    kw, cw = w.shape
    segs = _conv_segments(dm)

    def body(u_ref, w_ref, dy_ref, du_ref, dw_ref):
        dws = [jnp.zeros((1, LANES), F32)] * kw
        for r0, n, per in segs:
            useg = u_ref[r0:r0 + n, :]
            dyseg = dy_ref[r0:r0 + n, :]
            p = _pos(n, per)
            acc = jnp.zeros_like(useg)
            for kk in range(kw):
                d = kk - kw // 2
                shu = useg if d == 0 else pltpu.roll(useg, (-d) % n, 0)
                okf = jnp.logical_and(p + d >= 0, p + d < per)
                dws[kk] = dws[kk] + jnp.sum(jnp.where(okf, shu, 0.0) * dyseg, axis=0, keepdims=True)
                shd = dyseg if d == 0 else pltpu.roll(dyseg, d % n, 0)
                okb = jnp.logical_and(p - d >= 0, p - d < per)
                acc = acc + jnp.where(okb, shd, 0.0) * w_ref[kk:kk + 1, :]
            du_ref[r0:r0 + n, :] = acc
        for kk in range(kw):
            dw_ref[kk:kk + 1, :] = dws[kk]

    return pl.pallas_call(
        body, name=name, grid=(cw // LANES,),
        in_specs=[pl.BlockSpec((dm.T, LANES), lambda j: (0, j)), pl.BlockSpec((kw, LANES), lambda j: (0, j)),
                  pl.BlockSpec((dm.T, LANES), lambda j: (0, j))],
        out_specs=[pl.BlockSpec((dm.T, LANES), lambda j: (0, j)), pl.BlockSpec((kw, LANES), lambda j: (0, j))],
        out_shape=[jax.ShapeDtypeStruct((dm.T, cw), F32), jax.ShapeDtypeStruct((kw, cw), F32)],
        compiler_params=_cparams(("parallel",)),
    )(u, w, dy)


def pool_mix(u, transpose, dm, name, into=None):
    u, uw, ublk = _rowspec(u)
    gc = dm.GC
    ng = len(POOL_WINDOWS)
    rows = dm.SEQ // GRID_W
    segs = [(0, dm.CTX, 1, dm.CTX), (dm.CTX, dm.SEQ, GRID_W, rows)]

    def one_group(u_ref, o_ref, win):
        left = win // 2
        right = win - 1 - left
        for r0, n, stride, length in segs:
            useg = u_ref[r0:r0 + n, :].astype(F32)
            t = lax.broadcasted_iota(jnp.int32, (n, 1), 0)
            p = t if stride == 1 else jnp.right_shift(t, stride.bit_length() - 1)
            cnt = (jnp.minimum(p + right + 1, length) - jnp.maximum(p - left, 0)).astype(F32)
            src = useg / cnt if transpose else useg
            acc = jnp.zeros_like(useg)
            for d in range(-left, right + 1):
                dd = -d if transpose else d
                sh = src if d == 0 else pltpu.roll(src, (-dd * stride) % n, 0)
                ok = jnp.logical_and(p + dd >= 0, p + dd < length)
                acc = acc + jnp.where(ok, sh, 0.0)
            o_ref[r0:r0 + n, :] = ((acc - useg) if transpose else (acc / cnt - useg)).astype(o_ref.dtype)

    def body(u_ref, *rest):
        o_ref = rest[-1]
        g = pl.program_id(0)
        for gi, win in enumerate(POOL_WINDOWS):
            @pl.when(g == gi)
            def _(win=win):
                one_group(u_ref, o_ref, win)

    base = ublk * (uw // gc)
    if into is None:
        obase, out_shape, more, extra, aliases = 0, jax.ShapeDtypeStruct((dm.T, ng * gc), F32), [], [], {}
    else:
        buf, oblk = into
        obase, out_shape = oblk * ng, jax.ShapeDtypeStruct(buf.shape, buf.dtype)
        more, extra, aliases = [pl.BlockSpec(memory_space=pl.ANY)], [buf], {1: 0}
    return pl.pallas_call(
        body, name=name, grid=(ng,),
        in_specs=[pl.BlockSpec((dm.T, gc), lambda g: (0, base + g))] + more,
        out_specs=pl.BlockSpec((dm.T, gc), lambda g: (0, obase + g)),
        out_shape=out_shape, input_output_aliases=aliases,
        compiler_params=_cparams(("parallel",)),
    )(u, *extra)


def loss_head(x2, target, dm, name):
    tm, d = dm.tm, dm.D
    nctx = dm.CTX // tm

    def body(x_ref, t_ref, dx_ref, l_ref):
        i = pl.program_id(0)

        @pl.when(i == 0)
        def _():
            l_ref[...] = jnp.zeros_like(l_ref)

        @pl.when(i < nctx)
        def _():
            dx_ref[...] = jnp.zeros_like(dx_ref)

        @pl.when(i >= nctx)
        def _():
            e = x_ref[...] - t_ref[...]
            dx_ref[...] = e / d
            l_ref[...] += jnp.full(l_ref.shape, 0.5 * jnp.sum(jnp.mean(e * e, axis=-1)), F32)

    return pl.pallas_call(
        body, name=name, grid=(dm.T // tm,),
        in_specs=[pl.BlockSpec((tm, d), lambda i: (i, 0)),
                  pl.BlockSpec((tm, d), lambda i: (jnp.maximum(i - nctx, 0), 0))],
        out_specs=[pl.BlockSpec((tm, d), lambda i: (i, 0)), pl.BlockSpec((8, LANES), lambda i: (0, 0))],
        out_shape=[jax.ShapeDtypeStruct((dm.T, d), F32), jax.ShapeDtypeStruct((8, LANES), F32)],
        compiler_params=_cparams(("arbitrary",)),
    )(x2, target)


def adamw(w, g, m, v, name):
    r, c = w.shape
    tr = _tile(r, tuple(t for t in (512, 256, 128, 64, 32, 16, 8) if t * c * 4 <= (1 << 20)) or (8,))

    def body(w_ref, g_ref, m_ref, v_ref, d_ref, mo_ref, vo_ref):
        gg = g_ref[...]
        mm = ADAM_B1 * m_ref[...] + (1.0 - ADAM_B1) * gg
        vv = ADAM_B2 * v_ref[...] + (1.0 - ADAM_B2) * (gg * gg)
        m_hat = mm / (1.0 - ADAM_B1 ** ADAM_STEP)
        v_hat = vv / (1.0 - ADAM_B2 ** ADAM_STEP)
        d_ref[...] = -ADAM_LR * (m_hat / (jnp.sqrt(v_hat) + ADAM_EPS) + ADAM_WD * w_ref[...])
        mo_ref[...] = mm
        vo_ref[...] = vv

    spec = pl.BlockSpec((tr, c), lambda i: (i, 0))
    return pl.pallas_call(
        body, name=name, grid=(r // tr,), in_specs=[spec] * 4, out_specs=[spec] * 3,
        out_shape=[jax.ShapeDtypeStruct((r, c), F32)] * 3,
        compiler_params=_cparams(("parallel",)),
    )(w, g, m, v)


def pair_add(g, r1, cidx, name):
    ng, r_, n_ = r1.shape
    tr = _tile(r_, tuple(t for t in (1024, 512, 256, 128, 64, 32, 16) if t * n_ * 4 <= (2 << 20)))

    def body(s_ref, g_ref, r_ref, o_ref):
        o_ref[...] = (g_ref[...].astype(F32) + r_ref[...].astype(F32)).astype(o_ref.dtype)

    return pl.pallas_call(
        body, name=name,
        grid_spec=pltpu.PrefetchScalarGridSpec(
            num_scalar_prefetch=1, grid=(ng, r_ // tr),
            in_specs=[pl.BlockSpec((None, tr, n_), lambda k, i, s: (2 * k + s[0], i, 0)),
                      pl.BlockSpec((None, tr, n_), lambda k, i, s: (k, i, 0))],
            out_specs=pl.BlockSpec((None, tr, n_), lambda k, i, s: (k, i, 0))),
        out_shape=jax.ShapeDtypeStruct((ng, r_, n_), g.dtype),
        compiler_params=_cparams(("parallel", "parallel")),
    )(cidx, g, r1)


def chip_add(h, r2, axis, where, name, slab=True):
    _, kl, nl = r2.shape
    tr = _tile(kl, tuple(t for t in (1024, 512, 256, 128, 64, 32, 16) if t * nl * 4 <= (1 << 20)))
    nrb = kl // tr

    def body(s_ref, h_ref, r_ref, o_ref):
        acc = h_ref[...].astype(F32)
        for k in range(r2.shape[0]):
            acc = acc + r_ref[k].astype(F32)
        o_ref[...] = acc

    h_map = (lambda i, s: (s[0] * nrb + i, 0)) if axis == 0 else (lambda i, s: (i, s[0]))
    if slab:
        out_spec = pl.BlockSpec((None, tr, nl), lambda i, s: (s[1], i, 0))
        out_shape = jax.ShapeDtypeStruct((2, kl, nl), F32)
    else:
        out_spec = pl.BlockSpec((tr, nl), lambda i, s: (i, 0))
        out_shape = jax.ShapeDtypeStruct((kl, nl), F32)
    return pl.pallas_call(
        body, name=name,
        grid_spec=pltpu.PrefetchScalarGridSpec(
            num_scalar_prefetch=1, grid=(nrb,),
            in_specs=[pl.BlockSpec((tr, nl), h_map),
                      pl.BlockSpec((r2.shape[0], tr, nl), lambda i, s: (0, i, 0))],
            out_specs=out_spec),
        out_shape=out_shape,
        compiler_params=_cparams(("parallel",)),
    )(where, h, r2)


def adamw_layers(w, m, v, terms, name, layer=None, prev=None):
    _, a_, b_ = w.shape
    tr = _tile(a_, tuple(t for t in (512, 256, 128, 64, 32) if t * b_ * 4 <= (1 << 20)))
    by_cols = tr == a_ and a_ * b_ * 4 > (1 << 20)
    blk = (a_, LANES) if by_cols else (tr, b_)
    steps = b_ // LANES if by_cols else a_ // tr
    at = (lambda i: (0, i)) if by_cols else (lambda i: (i, 0))
    layers = (0, 1) if layer is None else (layer,)
    counts = [len(terms[l]) for l in layers]
    nprev = 0 if prev is None else 4

    def update(g, w_ref, m_ref, v_ref, g_ref, d_ref, mo_ref, vo_ref):
        mm = ADAM_B1 * m_ref[...] + (1.0 - ADAM_B1) * g
        vv = ADAM_B2 * v_ref[...] + (1.0 - ADAM_B2) * (g * g)
        m_hat = mm / (1.0 - ADAM_B1 ** ADAM_STEP)
        v_hat = vv / (1.0 - ADAM_B2 ** ADAM_STEP)
        g_ref[...] = g
        d_ref[...] = -ADAM_LR * (m_hat / (jnp.sqrt(v_hat) + ADAM_EPS) + ADAM_WD * w_ref[...])
        mo_ref[...] = mm
        vo_ref[...] = vv

    def total(refs):
        g = refs[0][...]
        for r in refs[1:]:
            g = g + r[...]
        return g

    def body(w_ref, m_ref, v_ref, *rest):
        t_refs, outs = rest[:sum(counts)], rest[-4:]
        if len(layers) == 1:
            update(total(t_refs), w_ref, m_ref, v_ref, *outs)
            return
        which = pl.program_id(0)

        @pl.when(which == 0)
        def _():
            update(total(t_refs[:counts[0]]), w_ref, m_ref, v_ref, *outs)

        @pl.when(which == 1)
        def _():
            update(total(t_refs[counts[0]:]), w_ref, m_ref, v_ref, *outs)

    if len(layers) == 1:
        stacked = pl.BlockSpec((None,) + blk, lambda l, i: (layers[0],) + at(i))
        t_specs = [pl.BlockSpec(blk, lambda l, i: at(i))] * counts[0]
    else:
        stacked = pl.BlockSpec((None,) + blk, lambda l, i: (l,) + at(i))
        t_specs = ([pl.BlockSpec(blk, lambda l, i: at(i * (1 - l)))] * counts[0]
                   + [pl.BlockSpec(blk, lambda l, i: at(i * l))] * counts[1])
    nin = 3 + sum(counts)
    return pl.pallas_call(
        body, name=name, grid=(len(layers), steps),
        in_specs=[stacked] * 3 + t_specs + [pl.BlockSpec(memory_space=pl.ANY)] * nprev,
        out_specs=[stacked] * 4, out_shape=[jax.ShapeDtypeStruct(w.shape, F32)] * 4,
        input_output_aliases={nin + j: j for j in range(nprev)},
        compiler_params=_cparams(("arbitrary", "arbitrary")),
    )(w, m, v, *[t for l in layers for t in terms[l]], *(prev or ()))


MESH = pl.DeviceIdType.MESH
ANY = pl.BlockSpec(memory_space=pl.ANY)
HBM = pl.BlockSpec(memory_space=pltpu.HBM)
SEM = pl.BlockSpec(memory_space=pltpu.SEMAPHORE)
EFFECT = pltpu.SideEffectType.DATAFLOW_SIDE_EFFECTING


def _place():
    return lax.axis_index("x"), lax.axis_index("y"), lax.axis_index("c")


def _peers(x, y):
    return [(1 - x, y), (x, 1 - y), (1 - x, 1 - y)]


def _rcopy(src, dst, ssem, rsem, dev):
    return pltpu.make_async_remote_copy(src_ref=src, dst_ref=dst, send_sem=ssem, recv_sem=rsem,
                                        device_id=dev, device_id_type=MESH)


def _gathered_shape(src, kind):
    h, a_, b_ = src.shape
    return (h, a_, 4 * b_) if kind == 'col' else (4, h, a_, b_)


def _win(ref, kind, ch, width):
    return ref.at[:, :, pl.ds(ch * width, width)] if kind == 'col' else ref.at[ch]


def _rect(ref, kind, half, ch, width):
    return ref.at[half, :, pl.ds(ch * width, width)] if kind == 'col' else ref.at[ch, half]


def _gather_plan(kinds, widths):
    def plan(src, land, x, y, c):
        chip = 2 * x + y
        out = []
        for n in range(len(src)):
            for px, py in _peers(x, y):
                out.append((src[n].at[c], _rect(land[n], kinds[n], c, chip, widths[n]), (px, py, c),
                            _rect(land[n], kinds[n], c, 2 * px + py, widths[n])))
            mine = _win(land[n], kinds[n], chip, widths[n])
            out.append((src[n], mine, (x, y, 1 - c), mine))
        return out
    return plan


def forward_halves(lands, kinds, name):
    nw = len(lands)
    widths = [t.shape[-1] // 4 if k == 'col' else t.shape[-1] for t, k in zip(lands, kinds)]

    def body(*refs):
        o = refs[nw:2 * nw]
        ssem, rsem = refs[2 * nw:]
        x, y, c = _place()
        sib = (x, y, 1 - c)
        pidx = [2 * px + py for px, py in _peers(x, y)]
        cps = [_rcopy(_rect(o[n], kinds[n], c, pidx[j], widths[n]), _rect(o[n], kinds[n], c, pidx[j], widths[n]),
                      ssem.at[3 * n + j], rsem.at[3 * n + j], sib) for n in range(nw) for j in range(3)]
        for cp in cps:
            cp.start()
        for n in range(nw):
            for j in range(3):
                cps[3 * n + j].wait_send()
                _rcopy(_rect(o[n], kinds[n], 1 - c, pidx[j], widths[n]), _rect(o[n], kinds[n], 1 - c, pidx[j], widths[n]),
                       ssem.at[3 * n + j], rsem.at[3 * n + j], sib).wait_recv()

    return pl.pallas_call(
        body, name=name, in_specs=[ANY] * nw, out_specs=[ANY] * nw,
        out_shape=[jax.ShapeDtypeStruct(t.shape, t.dtype) for t in lands],
        input_output_aliases={n: n for n in range(nw)},
        scratch_shapes=[pltpu.SemaphoreType.DMA((3 * nw,)), pltpu.SemaphoreType.DMA((3 * nw,))],
    )(*lands)


def _forward_plan(kinds, widths):
    def plan(src, land, x, y, c):
        out = []
        for n in range(len(land)):
            for px, py in _peers(x, y):
                mine = _rect(land[n], kinds[n], c, 2 * px + py, widths[n])
                out.append((mine, mine, (x, y, 1 - c), _rect(land[n], kinds[n], 1 - c, 2 * px + py, widths[n])))
        return out
    return plan


def _scatter_plan(axes, widths):
    def plan(src, land, x, y, c):
        out = []
        for n in range(len(src)):
            for k, (px, py) in enumerate(_peers(x, y)):
                ch = 2 * px + py
                view = (src[n].at[:, pl.ds(ch * widths[n], widths[n])] if axes[n] == 1
                        else src[n].at[pl.ds(ch * widths[n], widths[n]), :])
                out.append((view, land[n].at[k], (px, py, c), land[n].at[k]))
        return out
    return plan


def _exchange_plan(kinds):
    def plan(src, land, x, y, c):
        out = []
        for n in range(len(src)):
            w = land[n].shape[2]
            for j, (px, py) in enumerate(_peers(x, y)):
                ch = 2 * px + py
                view = src[n].at[:, pl.ds(ch * w, w)] if kinds[n] == 'col' else src[n].at[ch]
                out.append((view, land[n].at[j], (px, py, c), land[n].at[j]))
        return out
    return plan


def start_copies(srcs, lands, plan, ncopies, after, name):
    ns, nl = len(srcs), len(lands)

    def body(*refs):
        src, land = refs[:ns], refs[ns:ns + nl]
        ssem, rsem = refs[ns + nl + 1], refs[ns + nl + 2]
        token = refs[-1]
        x, y, c = _place()
        for k, (sv, dv, dev, _) in enumerate(plan(src, land, x, y, c)):
            _rcopy(sv, dv, ssem.at[k], rsem.at[k], dev).start()
        token[...] = jnp.zeros_like(token)

    hbm = lambda t: pltpu.HBM(t.shape, t.dtype)
    res = pl.pallas_call(
        body, name=name,
        out_shape=(pltpu.SemaphoreType.DMA((ncopies,)), pltpu.SemaphoreType.DMA((ncopies,)),
                   *[hbm(t) for t in srcs], *[hbm(t) for t in lands], jax.ShapeDtypeStruct((8, LANES), F32)),
        in_specs=[HBM] * (ns + nl) + [ANY],
        out_specs=(SEM, SEM, *[HBM] * (ns + nl), pl.BlockSpec(memory_space=pltpu.VMEM)),
        input_output_aliases={k: 2 + k for k in range(ns + nl)},
        compiler_params=pltpu.CompilerParams(has_side_effects=EFFECT),
    )(*[pltpu.with_memory_space_constraint(t, pltpu.HBM) for t in list(srcs) + list(lands)], after)
    return res[0], res[1], list(res[2:2 + ns]), list(res[2 + ns:2 + ns + nl]), res[-1]


def wait_copies(ssem, rsem, srcs, lands, plan, after, name):
    ns, nl = len(srcs), len(lands)

    def body(*refs):
        src, land = refs[:ns], refs[ns:ns + nl]
        ss, rs = refs[ns + nl], refs[ns + nl + 1]
        x, y, c = _place()
        for k, (sv, dv, dev, mine) in enumerate(plan(src, land, x, y, c)):
            cp = _rcopy(sv, mine, ss.at[k], rs.at[k], dev)
            cp.wait_send()
            cp.wait_recv()

    hbm = lambda t: pltpu.HBM(t.shape, t.dtype)
    res = pl.pallas_call(
        body, name=name,
        out_shape=(*[hbm(t) for t in srcs], *[hbm(t) for t in lands]),
        in_specs=[HBM] * (ns + nl) + [SEM, SEM, ANY], out_specs=tuple([HBM] * (ns + nl)),
        input_output_aliases={k: k for k in range(ns + nl)},
        compiler_params=pltpu.CompilerParams(has_side_effects=EFFECT),
    )(*srcs, *lands, ssem, rsem, after)
    return list(res[:ns]), list(res[ns:])


def pair_swap_halves(gs, kinds, name):
    nw = len(gs)

    def other(ref, kind, half):
        return ref.at[half] if kind == 'col' else ref.at[:, half]

    def body(*refs):
        g, o = refs[:nw], refs[nw:2 * nw]
        ssem, rsem = refs[2 * nw:]
        x, y, c = _place()
        cps = [_rcopy(other(g[n], kinds[n], 1 - c), o[n], ssem.at[n], rsem.at[n], (x, y, 1 - c)) for n in range(nw)]
        for cp in cps:
            cp.start()
        for cp in cps:
            cp.wait()

    return pl.pallas_call(
        body, name=name, in_specs=[ANY] * nw, out_specs=[ANY] * nw,
        out_shape=[jax.ShapeDtypeStruct(g.shape[1:] if k == 'col' else (g.shape[0],) + g.shape[2:], g.dtype)
                   for g, k in zip(gs, kinds)],
        scratch_shapes=[pltpu.SemaphoreType.DMA((nw,)), pltpu.SemaphoreType.DMA((nw,))],
    )(*gs)


def chip_broadcast(h, name):
    def body(h_ref, o_ref, ssem, rsem):
        x, y, c = _place()
        cps = [_rcopy(h_ref, o_ref.at[j], ssem.at[j], rsem.at[j], (px, py, c)) for j, (px, py) in enumerate(_peers(x, y))]
        for cp in cps:
            cp.start()
        for cp in cps:
            cp.wait()

    return pl.pallas_call(
        body, name=name, in_specs=[ANY], out_specs=ANY,
        out_shape=jax.ShapeDtypeStruct((3,) + h.shape, h.dtype),
        scratch_shapes=[pltpu.SemaphoreType.DMA((3,)), pltpu.SemaphoreType.DMA((3,))],
    )(h)


def quad_sum(h, r, cidx, name):
    r_, c_ = h.shape
    tr = _tile(r_, (512, 256, 128, 64, 32, 16, 8))

    def body(s_ref, h_ref, r_ref, o_ref):
        o_ref[...] = (h_ref[...] + r_ref[2]) + (r_ref[0] + r_ref[1])

    return pl.pallas_call(
        body, name=name,
        grid_spec=pltpu.PrefetchScalarGridSpec(
            num_scalar_prefetch=1, grid=(r_ // tr,),
            in_specs=[pl.BlockSpec((tr, c_), lambda i, s: (i, 0)), pl.BlockSpec((3, tr, c_), lambda i, s: (0, i, 0))],
            out_specs=pl.BlockSpec((None, tr, c_), lambda i, s: (s[0], i, 0))),
        out_shape=jax.ShapeDtypeStruct((2, r_, c_), F32),
        compiler_params=_cparams(("parallel",)),
    )(cidx, h, r)


def pair_join_layers(fs, name):
    nw = len(fs)

    def body(*refs):
        o = refs[nw:2 * nw]
        ssem, rsem = refs[2 * nw:]
        x, y, c = _place()
        sib = (x, y, 1 - c)
        cps = [_rcopy(o[n].at[c], o[n].at[c], ssem.at[n], rsem.at[n], sib) for n in range(nw)]
        for cp in cps:
            cp.start()
        for n in range(nw):
            cps[n].wait_send()
            _rcopy(o[n].at[1 - c], o[n].at[1 - c], ssem.at[n], rsem.at[n], sib).wait_recv()

    return pl.pallas_call(
        body, name=name, in_specs=[ANY] * nw, out_specs=[ANY] * nw,
        out_shape=[jax.ShapeDtypeStruct(f.shape, f.dtype) for f in fs],
        input_output_aliases={n: n for n in range(nw)},
        scratch_shapes=[pltpu.SemaphoreType.DMA((nw,)), pltpu.SemaphoreType.DMA((nw,))],
    )(*fs)


def _flatten_pad(parts, dtype):
    flat = jnp.concatenate([p.reshape(-1).astype(dtype) for p in parts])
    q = 512 * LANES
    n = -(-flat.shape[0] // q) * q
    return jnp.pad(flat, (0, n - flat.shape[0])).reshape(n // LANES, LANES)


def _lane_pad(n):
    return -(-n // LANES) * LANES


def _in_proj_layout(d):
    gk, gv, cw, pw = d // 2, d, d // 2, d // 2
    own = [('q', gk), ('k', gk), ('v', gv), ('og', gv), ('lrf', GLA_LR), ('lrb', GLA_LR), ('ga', cw), ('gb', cw),
           ('pu', pw), ('mg', 3 * d)]
    padded = [('mg', 3 * d), ('og', gv), ('v', gv), ('q', gk), ('k', gk), ('ga', cw), ('gb', cw), ('pu', pw),
              ('lrf', GLA_LR), ('lrb', GLA_LR), ('pad', d // 2 - 2 * GLA_LR)]
    return own, padded


def _row_pieces(src, lo, hi, wl, wlp):
    out = []
    for k in range(4):
        s0, s1 = max(lo, k * wl), min(hi, (k + 1) * wl)
        if s0 < s1:
            out.append(src[k * wlp + s0 - k * wl:k * wlp + s1 - k * wl])
    return out


def _proj_runs(d):
    own, padded = _in_proj_layout(d)
    oat, start = {}, 0
    for n, wd in own:
        oat[n] = start
        start += wd
    runs, start = [], 0
    for n, wd in padded:
        if n != 'pad':
            if runs and runs[-1][0] + runs[-1][2] == oat[n] and runs[-1][1] + runs[-1][2] == start:
                runs[-1] = (runs[-1][0], runs[-1][1], runs[-1][2] + wd)
            else:
                runs.append((oat[n], start, wd))
        start += wd
    return runs, start


def _w_in_t_to_proj(g, d, wl, wlp):
    runs, total = _proj_runs(d)
    parts, at = [], 0
    for o0, p0, wd in runs:
        if p0 > at:
            parts.append(jnp.zeros((p0 - at, g.shape[1]), g.dtype))
        parts += _row_pieces(g, o0, o0 + wd, wl, wlp)
        at = p0 + wd
    if total > at:
        parts.append(jnp.zeros((total - at, g.shape[1]), g.dtype))
    return jnp.concatenate(parts, axis=0)


def _proj_to_w_in_t(gp, d, wl, wlp):
    runs, _ = _proj_runs(d)
    runs = sorted(runs)
    parts = []
    for k in range(4):
        for o0, p0, wd in runs:
            s0, s1 = max(o0, k * wl), min(o0 + wd, (k + 1) * wl)
            if s0 < s1:
                parts.append(gp[p0 + s0 - o0:p0 + s1 - o0])
        parts.append(jnp.zeros((wlp - wl, gp.shape[1]), gp.dtype))
    return jnp.concatenate(parts, axis=0)


def _silu_grad(z):
    s = jax.nn.sigmoid(z)
    return s + z * s * (1.0 - s)


def kernel(x, c, ctx, c_ctx, w_ada, b_ada, g_pre_mix, g_post_mix, g_pre_mlp, g_post_mlp, w_in, w_decay, b_decay, g_gla, w_gla_o, w_dw, b_dw, g_conv_ln, b_conv_ln, w_conv_o, w_pool_g, s_pool, w_pool_o, b_gate, w_out, w_mlp1, w_mlp2, loss_target, m_c_ctx, m_w_ada, m_b_ada, m_g_pre_mix, m_g_post_mix, m_g_pre_mlp, m_g_post_mlp, m_w_in, m_w_decay, m_b_decay, m_g_gla, m_w_gla_o, m_w_dw, m_b_dw, m_g_conv_ln, m_b_conv_ln, m_w_conv_o, m_w_pool_g, m_s_pool, m_w_pool_o, m_b_gate, m_w_out, m_w_mlp1, m_w_mlp2, v_c_ctx, v_w_ada, v_b_ada, v_g_pre_mix, v_g_post_mix, v_g_pre_mlp, v_g_post_mlp, v_w_in, v_w_decay, v_b_decay, v_g_gla, v_w_gla_o, v_w_dw, v_b_dw, v_g_conv_ln, v_b_conv_ln, v_w_conv_o, v_w_pool_g, v_s_pool, v_w_pool_o, v_b_gate, v_w_out, v_w_mlp1, v_w_mlp2):
    a = dict(locals())
    for n in ('w_in', 'm_w_in', 'v_w_in'):
        a[n] = jnp.swapaxes(a[n], 1, 2)
    big_axis = dict(BIG, w_in=1)
    depth = w_in.shape[0]
    d = x.shape[-1]
    seq, nctx_rows = x.shape[1], ctx.shape[1]
    dm = types.SimpleNamespace(
        D=d, SEQ=seq, CTX=nctx_rows, T=seq + nctx_rows, DK=d // 8, DV=d // 4, GK=d // 2, GC=d // 8,
        tm=_tile(nctx_rows, (256, 128, 64)), TB=_tile(nctx_rows, (256, 128, 64)))
    assert dm.SEQ % dm.tm == 0 and dm.SEQ % GRID_W == 0 and dm.CTX % GLA_CHUNK == 0
    tmw = min(dm.tm, 128)
    chip = 2 * lax.axis_index("x") + lax.axis_index("y")
    core = lax.axis_index("c")
    chip1 = chip.astype(jnp.int32).reshape(1)
    core1 = core.astype(jnp.int32).reshape(1)

    big_names, small_names = list(BIG), list(SMALL_SHARDED)
    nbig = len(big_names)
    kinds = ['col' if big_axis[n] == 2 else 'row' for n in big_names]
    wl = w_in.shape[2]
    wlp = _lane_pad(wl)

    def rows8(t):
        t = t.reshape(t.shape[0], -1, t.shape[-1])
        return jnp.pad(t, ((0, 0), (0, -t.shape[1] % 8), (0, 0)))

    def halves(t):
        return t.reshape(2, t.shape[0] // 2, t.shape[1])

    def layer_src(l, tok=None):
        def one(n):
            t = a[n][l] if tok is None else a[n][l] + tok
            return halves((jnp.pad(t, ((0, wlp - wl), (0, 0))) if n == 'w_in' else t).astype(MM_DTYPE))
        return [one(n) for n in big_names]

    def whole(t):
        return t.reshape(-1, t.shape[-1])

    def start_gather(srcs, knds, after, name):
        plan = _gather_plan(knds, [t.shape[2] for t in srcs])
        lands = [lax.empty(_gathered_shape(t, k), t.dtype) for t, k in zip(srcs, knds)]
        return (plan,) + start_copies(srcs, lands, plan, 4 * len(srcs), after, name)

    late = [big_names.index(n) for n in ('w_gla_o', 'w_conv_o', 'w_pool_o', 'w_out', 'w_mlp1', 'w_mlp2')]
    early = [k for k in range(nbig) if k not in late]
    src0 = layer_src(0)
    kinds_e = [kinds[k] for k in early] + ['col'] * len(small_names)
    age = start_gather([src0[k] for k in early] + [rows8(a[n]) for n in small_names], kinds_e, core1,
                       "gather_layer0_start")
    tok0 = age[-1][0, 0]
    src1 = layer_src(1, tok0)
    pk = lambda pre: _flatten_pad([a[pre + n] + tok0 for n in SMALL], F32)
    small_w, small_m, small_v = pk(''), pk('m_'), pk('v_')
    X = jnp.concatenate([ctx[0] + tok0, x[0] + tok0], axis=0)
    ready = (small_w[0, 0] + small_m[0, 0] + small_v[0, 0] + X[0, 0]
             + sum(t[0, 0, 0].astype(F32) for t in src1)).reshape(1, 1)
    _, g0 = wait_copies(age[1], age[2], age[3], age[4], age[0], ready, "gather_layer0_wait")
    g0 = forward_halves(g0, kinds_e, "gather_layer0_forward")
    ag0 = start_gather([src0[k] for k in late], [kinds[k] for k in late], g0[0], "gather_layer0_late_start")
    ag1 = start_gather(src1, kinds, ag0[-1], "gather_layer1_start")
    ag_token = ag1[-1]
    full = {n: [None, None] for n in big_names}
    for k, t in zip(early, g0):
        full[big_names[k]][0] = whole(t)
    for n, g in zip(small_names, g0[len(early):]):
        shp = a[n].shape
        full[n] = g[:, :math.prod(shp[1:-1])].reshape(shp[:-1] + (4 * shp[-1],))
    for n in SMALL:
        if n not in SMALL_SHARDED:
            full[n] = a[n]

    cvec = jnp.concatenate([c_ctx.reshape(1, d), c.reshape(1, d), jnp.zeros((6, d), F32)], axis=0)
    avec = (cvec * jax.nn.sigmoid(cvec) + ag_token[0, 0]).astype(MM_DTYPE)

    def row(v):
        return v.reshape(1, -1)

    saved = []
    gk, gv = dm.GK, d
    lrblk = (7 * d + d // 2) // LANES
    for l in range(depth):
        if l == 1:
            _, got = wait_copies(fw1[0], fw1[1], fw1[2], fw1[3], fw1_plan, X, "gather_layer1_forward_wait")
            for n, t in zip(big_names, got):
                full[n][1] = whole(t)
        s = types.SimpleNamespace()
        s.w_in_p = _w_in_t_to_proj(full['w_in'][l], d, wl, wlp)
        wd = full['w_decay'][l]
        wdp = jnp.zeros((LANES, 2 * gk), F32)
        wdp = wdp.at[:GLA_LR, :gk].set(wd[0]).at[GLA_LR:2 * GLA_LR, gk:].set(wd[1])
        s.wdp = wdp.astype(MM_DTYPE)
        s.wdp_wide = jnp.pad(s.wdp, ((0, d // 2 - LANES), (0, 0)))
        s.bd = full['b_decay'][l].reshape(1, 2 * gk)
        modraw = matmul(avec, full['w_ada'][l], 'nn', F32, f"mod_{l}") + full['b_ada'][l][None, :]
        s.mod = [modraw[0:2, j * d:(j + 1) * d].reshape(2, 1, d) for j in range(6)]
        s.x = X
        (s.h,) = rowwise(pre_fn, [X], s.mod[0:2], [row(g_pre_mix[l])], [(d, MM_DTYPE)], dm, f"pre_{l}")
        s.P = matmul(s.h, s.w_in_p, 'nt', MM_DTYPE, f"in_proj_{l}")
        P = s.P
        s.z = matmul((P, LANES, lrblk), s.wdp, 'nn', F32, f"decay_proj_{l}", tk=LANES)
        la_f, la_b = rowwise(decay_fn, [s.z], [], [s.bd], [(gk, F32), (gk, F32)], dm, f"decay_{l}")
        s.la = jnp.concatenate([la_f, la_b], axis=1)
        if l == 0:
            kinds_l = [kinds[k] for k in late]
            _, got0 = wait_copies(ag0[1], ag0[2], ag0[3], ag0[4], ag0[0], s.la, "gather_layer0_late_wait")
            fw0_plan = _forward_plan(kinds_l, [t.shape[-1] // 4 if k == 'col' else t.shape[-1]
                                               for t, k in zip(got0, kinds_l)])
            fw0 = start_copies([], got0, fw0_plan, 3 * len(late), core1, "gather_layer0_late_forward_start")
        s.o_f, s.st_f = gla_fwd(P, s.la, False, dm, f"gla_fwd_f_{l}")
        s.o_b, s.st_b = gla_fwd(P, s.la, True, dm, f"gla_fwd_b_{l}")
        (s.gin,) = rowwise(glaout_fn, [s.o_f, s.o_b, (P, d, 3)], [], [row(g_gla[l])], [(gv, MM_DTYPE)], dm,
                           f"gla_out_{l}")
        if l == 0:
            _, got = wait_copies(fw0[0], fw0[1], fw0[2], fw0[3], fw0_plan, s.gin, "gather_layer0_late_forward_wait")
            for k, t in zip(late, got):
                full[big_names[k]][0] = whole(t)
        s.ya = matmul(s.gin, full['w_gla_o'][l], 'nn', MM_DTYPE, f"gla_o_{l}")
        (s.u,) = rowwise(glu_fn, [(P, d, 6)], [], [], [(d // 2, F32)], dm, f"glu_{l}")
        s.yconv = conv_fwd(s.u, full['w_dw'][l], dm, f"conv_{l}")
        (s.cin,) = rowwise(convpost_fn, [s.yconv], [], [row(b_dw[l]), row(g_conv_ln[l]), row(b_conv_ln[l])],
                           [(d // 2, MM_DTYPE)], dm, f"conv_post_{l}")
        s.yb = matmul(s.cin, full['w_conv_o'][l], 'nn', MM_DTYPE, f"conv_o_{l}")
        s.pm = pool_mix((P, d // 2, 14), False, dm, f"pool_mix_{l}")
        s.pc = group_mm(s.pm, w_pool_g[l], 'nn', F32, f"pool_g_{l}")
        (s.pin,) = rowwise(poolpost_fn, [s.pc], [], [row(s_pool[l])], [(d // 2, MM_DTYPE)], dm, f"pool_post_{l}")
        s.yc = matmul(s.pin, full['w_pool_o'][l], 'nn', MM_DTYPE, f"pool_o_{l}")
        s.bg = [row(full['b_gate'][l][j]) for j in range(3)]
        (s.mixed,) = rowwise(merge_fn, [s.ya, s.yb, s.yc, (P, 3 * d, 0)], [], s.bg, [(d, MM_DTYPE)], dm,
                             f"merge_{l}", tm=tmw)
        s.y = matmul(s.mixed, full['w_out'][l], 'nn', MM_DTYPE, f"out_proj_{l}")
        s.x1, s.h2 = rowwise(mid_fn, [X, s.y], s.mod[2:5], [row(g_post_mix[l]), row(g_pre_mlp[l])],
                             [(d, F32), (d, MM_DTYPE)], dm, f"mid_{l}")
        if l == 0:
            _, got1 = wait_copies(ag1[1], ag1[2], ag1[3], ag1[4], ag1[0], s.h2, "gather_layer1_wait")
            fw1_plan = _forward_plan(kinds, [t.shape[-1] // 4 if k == 'col' else t.shape[-1]
                                             for t, k in zip(got1, kinds)])
            fw1 = start_copies([], got1, fw1_plan, 3 * nbig, core1, "gather_layer1_forward_start")
        s.act = matmul(s.h2, full['w_mlp1'][l], 'nn', MM_DTYPE, f"mlp1_{l}", epi=relu2_epi)
        s.y2 = matmul(s.act, full['w_mlp2'][l], 'nn', MM_DTYPE, f"mlp2_{l}")
        (X,) = rowwise(post_fn, [s.x1, s.y2], s.mod[5:6], [row(g_post_mlp[l])], [(d, F32)], dm, f"post_{l}")
        saved.append(s)

    dX, lossv = loss_head(X, loss_target[0], dm, "loss_head")
    loss = lax.psum(lossv[0, 0], ("x", "y", "c"))

    grads = {n: [None] * depth for n in WEIGHTS if n != 'c_ctx' and n not in BIG}
    gbig = {n: [None] * depth for n in BIG}
    rs_token = None
    where = jnp.concatenate([chip1, core1])

    def swap_plan(src, land, x, y, c):
        return [(src[n], land[n], (x, y, 1 - c), land[n]) for n in range(len(src))]


    def start_scatter(idx, layer, after, name):
        gs = [gbig[big_names[k]][layer] for k in idx]
        wd = [t.shape[1] // 4 if kinds[k] == 'col' else t.shape[0] // 4 for t, k in zip(gs, idx)]
        plan = _scatter_plan([big_axis[big_names[k]] - 1 for k in idx], wd)
        lands = [lax.empty((3, t.shape[0], w) if kinds[k] == 'col' else (3, w, t.shape[1]), t.dtype)
                 for t, w, k in zip(gs, wd, idx)]
        return (plan,) + start_copies(gs, lands, plan, 3 * len(gs), after, name)

    g_cctx = jnp.zeros((d,), F32)
    for l in reversed(range(depth)):
        s = saved[l]
        P = s.P
        dmod = [None] * 6
        gpm = row(g_post_mlp[l]) if rs_token is None else row(g_post_mlp[l]) + rs_token[0, 0]
        (dx1, dy2), (dmod[5],), (dg,) = rowwise_vjp(post_fn, [s.x1, s.y2], s.mod[5:6], [gpm], [dX],
                                                     dm, f"post_bwd_{l}", narrow=(1,))
        grads['g_post_mlp'][l] = dg[0]
        du1 = matmul(dy2, full['w_mlp2'][l], 'nt', MM_DTYPE, f"mlp2_dx_{l}", epi=relu2_bwd_epi, extras=[s.act])
        gbig['w_mlp2'][l] = matmul(s.act, dy2, 'tn', MM_DTYPE, f"mlp2_dw_{l}")
        dh2 = matmul(du1, full['w_mlp1'][l], 'nt', MM_DTYPE, f"mlp1_dx_{l}")
        gbig['w_mlp1'][l] = matmul(s.h2, du1, 'tn', MM_DTYPE, f"mlp1_dw_{l}")
        gpx = row(g_post_mix[l])
        (dxa, dy), dmod[2:5], (dg1, dg2) = rowwise_vjp(
            mid_fn, [s.x, s.y], s.mod[2:5], [gpx, row(g_pre_mlp[l])], [dx1, dh2], dm, f"mid_bwd_{l}", narrow=(1,))
        grads['g_post_mix'][l], grads['g_pre_mlp'][l] = dg1[0], dg2[0]
        dmixed = matmul(dy, full['w_out'][l], 'nt', MM_DTYPE, f"out_proj_dx_{l}")
        gbig['w_out'][l] = matmul(s.mixed, dy, 'tn', MM_DTYPE, f"out_proj_dw_{l}")
        (dya, dyb, dyc, dP), _, dbg = rowwise_vjp(merge_fn, [s.ya, s.yb, s.yc, (P, 3 * d, 0)], [], s.bg, [dmixed],
                                                  dm, f"merge_bwd_{l}", tm=tmw, narrow=(0, 1, 2),
                                                  into=(3, None, P.shape))
        grads['b_gate'][l] = jnp.concatenate(dbg, axis=0)
        dgin = matmul(dya, full['w_gla_o'][l], 'nt', MM_DTYPE, f"gla_o_dx_{l}")
        gbig['w_gla_o'][l] = matmul(s.gin, dya, 'tn', MM_DTYPE, f"gla_o_dw_{l}")
        dcin = matmul(dyb, full['w_conv_o'][l], 'nt', MM_DTYPE, f"conv_o_dx_{l}")
        gbig['w_conv_o'][l] = matmul(s.cin, dyb, 'tn', MM_DTYPE, f"conv_o_dw_{l}")
        dpin = matmul(dyc, full['w_pool_o'][l], 'nt', MM_DTYPE, f"pool_o_dx_{l}")
        gbig['w_pool_o'][l] = matmul(s.pin, dyc, 'tn', MM_DTYPE, f"pool_o_dw_{l}")
        sp = row(s_pool[l])
        if l == 0:
            rs0 = start_scatter(late, 0, dpin, "grad_layer0_late_start")
            sp = sp + rs0[-1][0, 0]
        (dpc,), _, (dsp,) = rowwise_vjp(poolpost_fn, [s.pc], [], [sp], [dpin], dm, f"pool_post_bwd_{l}")
        grads['s_pool'][l] = dsp[0]
        grads['w_pool_g'][l] = group_mm(s.pm, w_pool_g[l], 'tn', F32, f"pool_g_dw_{l}", b=dpc)
        dpm = group_mm(dpc, w_pool_g[l], 'nt', F32, f"pool_g_dx_{l}")
        dP = pool_mix(dpm, True, dm, f"pool_mix_bwd_{l}", into=(dP, 14))
        (dyconv,), _, (dbdw, dgln, dbln) = rowwise_vjp(
            convpost_fn, [s.yconv], [], [row(b_dw[l]), row(g_conv_ln[l]), row(b_conv_ln[l])], [dcin], dm,
            f"conv_post_bwd_{l}")
        grads['b_dw'][l], grads['g_conv_ln'][l], grads['b_conv_ln'][l] = dbdw[0], dgln[0], dbln[0]
        du, grads['w_dw'][l] = conv_bwd(s.u, full['w_dw'][l], dyconv, dm, f"conv_bwd_{l}")
        (dP,), _, _ = rowwise_vjp(glu_fn, [(P, d, 6)], [], [], [du], dm, f"glu_bwd_{l}", into=(0, dP, P.shape))
        (do, _, dP), _, (dgg,) = rowwise_vjp(glaout_fn, [s.o_f, s.o_b, (P, d, 3)], [], [row(g_gla[l])], [dgin], dm,
                                             f"gla_out_bwd_{l}", want=[True, False, True], into=(2, dP, P.shape), narrow=(0,))
        grads['g_gla'][l] = dgg[0]
        dqf, dkf, dvf, dlaf = gla_bwd(P, s.la, do, s.st_f, False, dm, f"gla_bwd_f_{l}")
        dP, dlab = gla_bwd(P, s.la, do, s.st_b, True, dm, f"gla_bwd_b_{l}", prev=(dqf, dkf, dvf), into=dP)
        (dz,), _, (dbd,) = rowwise_vjp(decay_fn, [s.z], [], [s.bd], [dlaf, dlab], dm, f"decay_bwd_{l}", narrow=(0,))
        grads['b_decay'][l] = dbd.reshape(2, gk)
        dwdp = matmul((P, LANES, lrblk), dz, 'tn', F32, f"decay_proj_dw_{l}", tm=LANES)
        grads['w_decay'][l] = jnp.stack([dwdp[:GLA_LR, :gk], dwdp[GLA_LR:2 * GLA_LR, gk:]])
        dP = matmul(dz, s.wdp_wide, 'nt', MM_DTYPE, f"decay_proj_dx_{l}", into=(dP, 15))
        gpre = row(g_pre_mix[l])
        if l == 0:
            gs1, got1 = wait_copies(rs1[1], rs1[2], rs1[3], rs1[4], rs1[0], dP, "grad_layer1_wait")
            sa1 = [chip_add(g, r, big_axis[n] - 1, where, f"grad_layer1_add_{n}", slab=False)
                   for n, g, r in zip(big_names, gs1, got1)]
            swp1 = start_copies(sa1, [lax.empty(t.shape, t.dtype) for t in sa1], swap_plan, len(sa1), core1,
                                "grad_layer1_pair_swap_start")
            gpre = gpre + swp1[-1][0, 0]
        dh = matmul(dP, s.w_in_p, 'nn', MM_DTYPE, f"in_proj_dx_{l}")
        gbig['w_in'][l] = _proj_to_w_in_t(matmul(dP, s.h, 'tn', MM_DTYPE, f"in_proj_dw_{l}"), d, wl, wlp)
        (dX,), dmod[0:2], (dg,) = rowwise_vjp(pre_fn, [s.x], s.mod[0:2], [gpre], [dh], dm,
                                               f"pre_bwd_{l}", adds={0: dxa})
        grads['g_pre_mix'][l] = dg[0]
        dmodflat = jnp.concatenate([jnp.concatenate([m_.reshape(2, d) for m_ in dmod], axis=1),
                                    jnp.zeros((6, 6 * d), F32)], axis=0)
        grads['b_ada'][l] = dmodflat[0] + dmodflat[1]
        gbig['w_ada'][l] = matmul(avec, dmodflat, 'tn', MM_DTYPE, f"ada_dw_{l}")
        dav = matmul(dmodflat, full['w_ada'][l], 'nt', F32, f"ada_dx_{l}")
        g_cctx = g_cctx + dav[0] * _silu_grad(c_ctx)
        if l == 1:
            rs1 = start_scatter(list(range(nbig)), 1, dav, "grad_layer1_start")
            rs_token = rs1[-1]

    grad_x = dX[dm.CTX:][None]
    gfull = {n: jnp.stack(v) for n, v in grads.items()}
    gfull['c_ctx'] = g_cctx

    def halves_view(t, k):
        return t.reshape(2, t.shape[0] // 2, t.shape[1]) if k == 'col' else t.reshape(4, 2, t.shape[0] // 8, t.shape[1])
    enames = [big_names[k] for k in early]
    ekinds = [kinds[k] for k in early]
    v0 = [halves_view(gbig[n][0], k) for n, k in zip(enames, ekinds)]
    r1 = pair_swap_halves(v0, ekinds, "grad_pair_swap")
    hs = [pair_add(v.reshape((-1,) + v.shape[-2:]), r.reshape((-1,) + r.shape[-2:]), core1, f"grad_pair_add_{n}")
          for n, v, r in zip(enames, v0, r1)]
    hx = [h.reshape(h.shape[1:]) if k == 'col' else h for h, k in zip(hs, ekinds)]
    ex_plan = _exchange_plan(ekinds)
    ex_lands = [lax.empty((3, h.shape[0], h.shape[1] // 4) if k == 'col' else (3,) + h.shape[1:], h.dtype)
                for h, k in zip(hx, ekinds)]
    ex = (ex_plan,) + start_copies(hx, ex_lands, ex_plan, 3 * len(hx), core1, "grad_chip_exchange_start")

    gs0, got0 = wait_copies(rs0[1], rs0[2], rs0[3], rs0[4], rs0[0], ex[-1], "grad_layer0_late_wait")
    sa = [chip_add(g, r, big_axis[big_names[k]] - 1, where, f"grad_layer0_add_{big_names[k]}", slab=False)
          for k, g, r in zip(late, gs0, got0)]
    sflat = _flatten_pad([gfull[n].astype(F32) for n in SMALL], F32)
    sv = sflat.reshape(2, sflat.shape[0] // 2, LANES)
    (sr,) = pair_swap_halves([sv], ['col'], "small_grad_pair_swap")
    sh = pair_add(sv, sr[None], core1, "small_grad_pair_add")[0]
    sq = quad_sum(sh, chip_broadcast(sh, "small_grad_chip_exchange"), core1, "small_grad_chip_sum")
    (ssum,) = pair_join_layers([sq], "small_grad_pair_join")

    swp = start_copies(sa, [lax.empty(t.shape, t.dtype) for t in sa], swap_plan, len(sa), ssum,
                       "grad_late_pair_swap_start")
    ssum = ssum.reshape(-1) + swp[-1][0, 0]
    start = 0
    sg = {}
    for n in SMALL:
        cnt = gfull[n].size
        g = ssum[start:start + cnt].reshape(gfull[n].shape)
        start += cnt
        if n in SMALL_SHARDED:
            ax = SMALL_SHARDED[n]
            wdt = a[n].shape[ax]
            g = lax.dynamic_slice_in_dim(g, chip * wdt, wdt, axis=ax)
        sg[n] = g
    gs = _flatten_pad([sg[n] for n in SMALL], F32)
    dl, mn, vn = adamw(small_w, gs, small_m, small_v, "adamw_small")

    sa, sb = wait_copies(swp[0], swp[1], swp[2], swp[3], swap_plan, dl, "grad_late_pair_swap_wait")
    sa1, sb1 = wait_copies(swp1[0], swp1[1], swp1[2], swp1[3], swap_plan, dl, "grad_layer1_pair_swap_wait")
    red0 = {big_names[k]: [sa[j], sb[j]] for j, k in enumerate(late)}
    red1 = {n: [sa1[k], sb1[k]] for k, n in enumerate(big_names)}

    out_g, out_d, out_m, out_v = {}, {}, {}, {}

    def update_big(n, terms, **kw):
        res = adamw_layers(a[n], a['m_' + n], a['v_' + n], terms, f"adamw_{n}" + ("" if not kw else f"_{kw['layer']}"), **kw)
        out_g[n], out_d[n], out_m[n], out_v[n] = res
        return res
    for k in late:
        update_big(big_names[k], [red0[big_names[k]], red1[big_names[k]]])
    half_done = {n: update_big(n, {1: red1[n]}, layer=1) for n in enames}
    done = (dl[0, 0] + sum(out_d[n][1, 0, 0] for n in big_names)).reshape(1, 1)
    hx, r2 = wait_copies(ex[1], ex[2], ex[3], ex[4], ex[0], done, "grad_chip_exchange_wait")
    dl, mn, vn = dl.reshape(-1), mn.reshape(-1), vn.reshape(-1)
    start = 0
    for n in SMALL:
        cnt, shp = a[n].size, a[n].shape
        out_g[n] = sg[n]
        out_d[n], out_m[n], out_v[n] = (t[start:start + cnt].reshape(shp) for t in (dl, mn, vn))
        start += cnt
    fs = [chip_add(h.reshape(-1, h.shape[-1]), r, big_axis[n] - 1, where, f"grad_chip_add_{n}")
          for n, h, r in zip(enames, hx, r2)]
    for n, t in zip(enames, pair_join_layers(fs, "grad_pair_join")):
        update_big(n, {0: [t.reshape(-1, t.shape[-1])]}, layer=0, prev=tuple(half_done[n]))
    for dct in (out_g, out_d, out_m, out_v):
        dct['w_in'] = jnp.swapaxes(dct['w_in'], 1, 2)
    return (loss, grad_x, *[out_g[n] for n in WEIGHTS], *[out_d[n] for n in WEIGHTS],
            *[out_m[n] for n in WEIGHTS], *[out_v[n] for n in WEIGHTS])
```
